```python
import math
import jax
import jax.numpy as jnp
from jax import lax
import numpy as np

D_MODEL = 1024
BATCH = 8
SEQ = 2048
DEPTH = 1
DEC_BATCH = 128
DEC_SEQ = 4
PAST_LEN = 8192
PAGE_SIZE = 128

EPS = 1e-6
N_MEM = 256
SSD_HEADS = 8
SSD_HEAD_DIM = 64
SSD_WIDTH = SSD_HEADS * SSD_HEAD_DIM
SSD_GROUPS = 2
SSD_STATE = 128
SSD_CONV = 4
SSD_CHUNK = 128
SSD_CONV_DIM = SSD_WIDTH + 2 * SSD_GROUPS * SSD_STATE
ATT_HEADS = 8
ATT_KV_HEADS = 2
ATT_GROUP = ATT_HEADS // ATT_KV_HEADS
ATT_HEAD_DIM = 64
ATT_WIDTH = ATT_HEADS * ATT_HEAD_DIM
ATT_KV_WIDTH = ATT_KV_HEADS * ATT_HEAD_DIM
WINDOW = 128
ATT_SCALE = ATT_HEAD_DIM ** -0.5
MIX_WIDTH = SSD_WIDTH + ATT_WIDTH
X_HEADS = 4
X_HEAD_DIM = D_MODEL // X_HEADS
X_SCALE = X_HEAD_DIM ** -0.5
D_FF = -(-8 * D_MODEL // (3 * 256)) * 256
IN_SPLITS = (SSD_WIDTH, SSD_CONV_DIM, SSD_HEADS, ATT_WIDTH, ATT_KV_WIDTH, ATT_KV_WIDTH)
IN_DIM = sum(IN_SPLITS)
SPLIT_AT = tuple(int(s) for s in np.cumsum(IN_SPLITS)[:-1])

kernel_name = 'hymba_ssd_swa_sink_memxattn_step'


def rmsnorm(x, g):
    xf = x.astype(jnp.float32)
    y = xf * lax.rsqrt(jnp.mean(xf * xf, axis=-1, keepdims=True) + EPS)
    return (y * g.astype(jnp.float32)).astype(x.dtype)


def gated_group_rmsnorm(y, z, g):
    b, l, _ = y.shape
    u = (y * jax.nn.silu(z.astype(jnp.float32))).reshape(b, l, SSD_GROUPS, SSD_WIDTH // SSD_GROUPS)
    u = u * lax.rsqrt(jnp.mean(u * u, axis=-1, keepdims=True) + EPS)
    return u.reshape(b, l, SSD_WIDTH) * g.astype(jnp.float32)


def causal_conv(u, prev, w, bias):
    l = u.shape[1]
    up = jnp.concatenate([prev.astype(u.dtype), u], axis=1)
    out = bias
    for j in range(SSD_CONV):
        out = out + up[:, j:j + l] * w[j]
    return jax.nn.silu(out), up[:, l:]


def segsum(a):
    t = a.shape[-1]
    cs = jnp.cumsum(a, axis=-1)
    diff = cs[..., :, None] - cs[..., None, :]
    lower = jnp.tril(jnp.ones((t, t), dtype=bool))
    return jnp.where(lower, diff, -jnp.inf)


def ssd_scan(x, dt, a, bm, cm, h0):
    bsz, l = x.shape[0], x.shape[1]
    t = SSD_CHUNK if l % SSD_CHUNK == 0 else l
    nc = l // t
    r = SSD_HEADS // SSD_GROUPS
    xdt = (x * dt[..., None]).reshape(bsz, nc, t, SSD_GROUPS, r, SSD_HEAD_DIM)
    adt = (dt * a).reshape(bsz, nc, t, SSD_GROUPS, r).transpose(0, 3, 4, 1, 2)
    bc = bm.reshape(bsz, nc, t, SSD_GROUPS, SSD_STATE)
    cc = cm.reshape(bsz, nc, t, SSD_GROUPS, SSD_STATE)
    a_cum = jnp.cumsum(adt, axis=-1)
    decay_in = jnp.exp(segsum(adt))
    cb = jnp.einsum('bctgn,bcsgn->bgcts', cc, bc)
    y_diag = jnp.einsum('bgcts,bgrcts,bcsgrp->bctgrp', cb, decay_in, xdt)
    decay_to_end = jnp.exp(a_cum[..., -1:] - a_cum)
    chunk_states = jnp.einsum('bctgn,bgrct,bctgrp->bcgrpn', bc, decay_to_end, xdt)
    h0g = h0.reshape(bsz, 1, SSD_GROUPS, r, SSD_HEAD_DIM, SSD_STATE)
    states = jnp.concatenate([h0g, chunk_states], axis=1)
    chunk_tot = jnp.pad(a_cum[..., -1], ((0, 0), (0, 0), (0, 0), (1, 0)))
    decay_chunk = jnp.exp(segsum(chunk_tot))
    states = jnp.einsum('bgrzc,bcgrpn->bzgrpn', decay_chunk, states)
    y_off = jnp.einsum('bctgn,bcgrpn,bgrct->bctgrp', cc, states[:, :-1], jnp.exp(a_cum))
    y = (y_diag + y_off).reshape(bsz, l, SSD_HEADS, SSD_HEAD_DIM)
    return y, states[:, -1].reshape(bsz, SSD_HEADS, SSD_HEAD_DIM, SSD_STATE)


def ssd_mixer(z, xbc_raw, dt_raw, conv_prev, h0, conv_w, conv_b, dt_bias, a_log, d_skip, g_norm):
    dtype = z.dtype
    xbc, conv_new = causal_conv(xbc_raw, conv_prev, conv_w, conv_b)
    b, l, _ = xbc.shape
    xbc = xbc.astype(jnp.float32)
    gn = SSD_GROUPS * SSD_STATE
    xs = xbc[..., :SSD_WIDTH].reshape(b, l, SSD_HEADS, SSD_HEAD_DIM)
    bm = xbc[..., SSD_WIDTH:SSD_WIDTH + gn].reshape(b, l, SSD_GROUPS, SSD_STATE)
    cm = xbc[..., SSD_WIDTH + gn:].reshape(b, l, SSD_GROUPS, SSD_STATE)
    dt = jax.nn.softplus(dt_raw.astype(jnp.float32) + dt_bias.astype(jnp.float32))
    a = -jnp.exp(a_log.astype(jnp.float32))
    y, h_new = ssd_scan(xs, dt, a, bm, cm, h0.astype(jnp.float32))
    y = y + xs * d_skip.astype(jnp.float32)[:, None]
    y = gated_group_rmsnorm(y.reshape(b, l, SSD_WIDTH), z, g_norm)
    return y.astype(dtype), conv_new, h_new.astype(h0.dtype)


def sink_softmax(s, mask, sinks):
    sink = sinks.astype(jnp.float32).reshape(ATT_KV_HEADS, ATT_GROUP, 1, 1)
    s = jnp.where(mask, s, -jnp.inf)
    m = jnp.maximum(jnp.max(s, axis=-1, keepdims=True), sink)
    e = jnp.exp(s - m)
    return e / (jnp.sum(e, axis=-1, keepdims=True) + jnp.exp(sink - m))


def swa_prompt(q, k, v, sinks):
    bsz, l = q.shape[0], q.shape[1]
    nb = l // WINDOW
    qb = q.reshape(bsz, nb, WINDOW, ATT_KV_HEADS, ATT_GROUP, ATT_HEAD_DIM)
    kb = k.reshape(bsz, nb, WINDOW, ATT_KV_HEADS, ATT_HEAD_DIM)
    vb = v.reshape(bsz, nb, WINDOW, ATT_KV_HEADS, ATT_HEAD_DIM)
    pad = ((0, 0), (1, 0), (0, 0), (0, 0), (0, 0))
    kk = jnp.concatenate([jnp.pad(kb, pad)[:, :-1], kb], axis=2)
    vv = jnp.concatenate([jnp.pad(vb, pad)[:, :-1], vb], axis=2)
    s = jnp.einsum('bnqhrd,bnkhd->bnhrqk', qb, kk, preferred_element_type=jnp.float32) * ATT_SCALE
    qi = jnp.arange(WINDOW)[:, None] + WINDOW
    kj = jnp.arange(2 * WINDOW)[None, :]
    band = (kj <= qi) & (qi - kj < WINDOW)
    has_prev = (jnp.arange(nb) > 0)[:, None, None]
    mask = band[None] & (has_prev | (kj >= WINDOW)[None])
    pr = sink_softmax(s, mask[:, None, None], sinks)
    o = jnp.einsum('bnhrqk,bnkhd->bnqhrd', pr.astype(vv.dtype), vv)
    return o.reshape(bsz, l, ATT_WIDTH)


def swa_sample(q, k, v, buf_k, buf_v, sinks):
    bsz, l = q.shape[0], q.shape[1]
    wb = buf_k.shape[1]
    kk = jnp.concatenate([buf_k.astype(k.dtype), k], axis=1)
    vv = jnp.concatenate([buf_v.astype(v.dtype), v], axis=1)
    s = jnp.einsum('bqhrd,bkhd->bhrqk', q, kk, preferred_element_type=jnp.float32) * ATT_SCALE
    qpos = PAST_LEN + jnp.arange(l)[:, None]
    kpos = PAST_LEN - wb + jnp.arange(wb + l)[None, :]
    mask = (kpos <= qpos) & (qpos - kpos < WINDOW)
    pr = sink_softmax(s, mask, sinks)
    o = jnp.einsum('bhrqk,bkhd->bqhrd', pr.astype(vv.dtype), vv)
    return o.reshape(bsz, l, ATT_WIDTH), kk[:, -wb:], vv[:, -wb:]


def memory_kv(mem, g, w_k, w_v):
    b, m, _ = mem.shape
    mn = rmsnorm(mem, g)
    return ((mn @ w_k).reshape(b, m, X_HEADS, X_HEAD_DIM),
            (mn @ w_v).reshape(b, m, X_HEADS, X_HEAD_DIM))


def cross_attend(h, mem_k, mem_v, w_q, w_o):
    bsz, l, _ = h.shape
    q = (h @ w_q).reshape(bsz, l, X_HEADS, X_HEAD_DIM)
    s = jnp.einsum('blhd,bmhd->bhlm', q, mem_k.astype(q.dtype), preferred_element_type=jnp.float32) * X_SCALE
    pr = jax.nn.softmax(s, axis=-1)
    o = jnp.einsum('bhlm,bmhd->blhd', pr.astype(q.dtype), mem_v.astype(q.dtype))
    return o.reshape(bsz, l, D_MODEL) @ w_o


def decoder_layer(x, conv_prev, h0, win_k, win_v, mem_k, mem_v, p):
    bsz, l, _ = x.shape
    h = rmsnorm(x, p['g_mix_pre'])
    proj = h @ p['w_in']
    z, xbc_raw, dt_raw, q, k, v = jnp.split(proj, SPLIT_AT, axis=-1)
    y_ssd, conv_new, h_new = ssd_mixer(z, xbc_raw, dt_raw, conv_prev, h0, p['conv_w'], p['conv_b'],
                                       p['dt_bias'], p['a_log'], p['d_skip'], p['g_ssd_norm'])
    q = q.reshape(bsz, l, ATT_KV_HEADS, ATT_GROUP, ATT_HEAD_DIM)
    k = k.reshape(bsz, l, ATT_KV_HEADS, ATT_HEAD_DIM)
    v = v.reshape(bsz, l, ATT_KV_HEADS, ATT_HEAD_DIM)
    if win_k is None:
        y_att = swa_prompt(q, k, v, p['sinks'])
        new_k, new_v = k[:, -WINDOW:], v[:, -WINDOW:]
    else:
        y_att, new_k, new_v = swa_sample(q, k, v, win_k, win_v, p['sinks'])
    mix = jnp.concatenate([y_ssd, y_att.astype(x.dtype)], axis=-1) @ p['w_out']
    x = x + rmsnorm(mix, p['g_mix_post'])
    c = cross_attend(rmsnorm(x, p['g_x_pre']), mem_k, mem_v, p['w_xq'], p['w_xo'])
    x = x + rmsnorm(c, p['g_x_post'])
    hf = rmsnorm(x, p['g_ffn_pre'])
    f = (jax.nn.silu(hf @ p['w_gate']) * (hf @ p['w_up'])) @ p['w_down']
    x = x + rmsnorm(f, p['g_ffn_post'])
    return x, conv_new, h_new, new_k, new_v


def setup_inputs(seed: int = 0) -> dict:
    key = jax.random.key(seed)
    ks = iter(jax.random.split(key, 40))
    f32 = jnp.float32

    def nrm(shape, scale):
        return jax.random.normal(next(ks), shape, f32) * scale

    def gain(width):
        return 1.0 + nrm((DEPTH, width), 0.02)

    win_buf = min(WINDOW, PAST_LEN)
    x_prompt = nrm((BATCH, SEQ, D_MODEL), 1.0)
    x_sample = nrm((DEC_BATCH, DEC_SEQ, D_MODEL), 1.0)
    state_ssm = nrm((DEPTH, DEC_BATCH, SSD_HEADS, SSD_HEAD_DIM, SSD_STATE), 0.5)
    state_conv = nrm((DEPTH, DEC_BATCH, SSD_CONV - 1, SSD_CONV_DIM), 1.0)
    cache_win_k = nrm((DEPTH, DEC_BATCH, win_buf, ATT_KV_HEADS, ATT_HEAD_DIM), 1.0)
    cache_win_v = nrm((DEPTH, DEC_BATCH, win_buf, ATT_KV_HEADS, ATT_HEAD_DIM), 1.0)
    cache_mem_k = nrm((DEPTH, DEC_BATCH, N_MEM, X_HEADS, X_HEAD_DIM), 1.0)
    cache_mem_v = nrm((DEPTH, DEC_BATCH, N_MEM, X_HEADS, X_HEAD_DIM), 1.0)
    mem_prompt = nrm((BATCH, N_MEM, D_MODEL), 1.0)
    u = jax.random.uniform(next(ks), (DEPTH, SSD_HEADS), f32)
    dt0 = jnp.exp(u * (math.log(0.1) - math.log(0.001)) + math.log(0.001))
    dt_bias = dt0 + jnp.log(-jnp.expm1(-dt0))
    a_log = jnp.log(jax.random.uniform(next(ks), (DEPTH, SSD_HEADS), f32, 1.0, 16.0))
    return {
        'x_prompt': x_prompt, 'x_sample': x_sample,
        'state_ssm': state_ssm, 'state_conv': state_conv,
        'cache_win_k': cache_win_k, 'cache_win_v': cache_win_v,
        'cache_mem_k': cache_mem_k, 'cache_mem_v': cache_mem_v,
        'mem_prompt': mem_prompt,
        'g_mix_pre': gain(D_MODEL),
        'w_in': nrm((DEPTH, D_MODEL, IN_DIM), D_MODEL ** -0.5),
        'conv_w': nrm((DEPTH, SSD_CONV, SSD_CONV_DIM), SSD_CONV ** -0.5),
        'conv_b': nrm((DEPTH, SSD_CONV_DIM), 0.02),
        'dt_bias': dt_bias,
        'a_log': a_log,
        'd_skip': 1.0 + nrm((DEPTH, SSD_HEADS), 0.1),
        'g_ssd_norm': gain(SSD_WIDTH),
        'sinks': nrm((DEPTH, ATT_HEADS), 0.5),
        'w_out': nrm((DEPTH, MIX_WIDTH, D_MODEL), MIX_WIDTH ** -0.5),
        'g_mix_post': gain(D_MODEL),
        'g_x_pre': gain(D_MODEL),
        'w_xq': nrm((DEPTH, D_MODEL, D_MODEL), D_MODEL ** -0.5),
        'g_mem': gain(D_MODEL),
        'w_xk': nrm((DEPTH, D_MODEL, D_MODEL), D_MODEL ** -0.5),
        'w_xv': nrm((DEPTH, D_MODEL, D_MODEL), D_MODEL ** -0.5),
        'w_xo': nrm((DEPTH, D_MODEL, D_MODEL), D_MODEL ** -0.5),
        'g_x_post': gain(D_MODEL),
        'g_ffn_pre': gain(D_MODEL),
        'w_gate': nrm((DEPTH, D_MODEL, D_FF), D_MODEL ** -0.5),
        'w_up': nrm((DEPTH, D_MODEL, D_FF), D_MODEL ** -0.5),
        'w_down': nrm((DEPTH, D_FF, D_MODEL), D_FF ** -0.5),
        'g_ffn_post': gain(D_MODEL),
    }


def reference(x_prompt, x_sample, state_ssm, state_conv, cache_win_k, cache_win_v, cache_mem_k, cache_mem_v,
              mem_prompt, g_mix_pre, w_in, conv_w, conv_b, dt_bias, a_log, d_skip, g_ssd_norm, sinks, w_out,
              g_mix_post, g_x_pre, w_xq, g_mem, w_xk, w_xv, w_xo, g_x_post, g_ffn_pre, w_gate, w_up, w_down,
              g_ffn_post):
    yp, ys = x_prompt, x_sample
    bp = x_prompt.shape[0]
    p_ssm, p_conv, p_wk, p_wv, p_mk, p_mv = [], [], [], [], [], []
    s_ssm, s_conv, s_wk, s_wv = [], [], [], []
    for i in range(DEPTH):
        p = {'g_mix_pre': g_mix_pre[i], 'w_in': w_in[i], 'conv_w': conv_w[i], 'conv_b': conv_b[i],
             'dt_bias': dt_bias[i], 'a_log': a_log[i], 'd_skip': d_skip[i], 'g_ssd_norm': g_ssd_norm[i],
             'sinks': sinks[i], 'w_out': w_out[i], 'g_mix_post': g_mix_post[i], 'g_x_pre': g_x_pre[i],
             'w_xq': w_xq[i], 'w_xo': w_xo[i], 'g_x_post': g_x_post[i], 'g_ffn_pre': g_ffn_pre[i],
             'w_gate': w_gate[i], 'w_up': w_up[i], 'w_down': w_down[i], 'g_ffn_post': g_ffn_post[i]}
        mk, mv = memory_kv(mem_prompt, g_mem[i], w_xk[i], w_xv[i])
        conv0 = jnp.zeros((bp, SSD_CONV - 1, SSD_CONV_DIM), yp.dtype)
        h00 = jnp.zeros((bp, SSD_HEADS, SSD_HEAD_DIM, SSD_STATE), yp.dtype)
        yp, pc, ph, pk, pv = decoder_layer(yp, conv0, h00, None, None, mk, mv, p)
        p_ssm.append(ph); p_conv.append(pc); p_wk.append(pk); p_wv.append(pv); p_mk.append(mk); p_mv.append(mv)
        ys, sc, sh, sk, sv = decoder_layer(ys, state_conv[i], state_ssm[i], cache_win_k[i], cache_win_v[i],
                                           cache_mem_k[i], cache_mem_v[i], p)
        s_ssm.append(sh); s_conv.append(sc); s_wk.append(sk); s_wv.append(sv)
    return (yp, ys,
            jnp.stack(p_ssm), jnp.stack(p_conv), jnp.stack(p_wk), jnp.stack(p_wv), jnp.stack(p_mk), jnp.stack(p_mv),
            jnp.stack(s_ssm), jnp.stack(s_conv), jnp.stack(s_wk), jnp.stack(s_wv))
```

```python
import functools

import jax
import jax.numpy as jnp
from jax import lax
from jax.experimental import pallas as pl
from jax.experimental.pallas import tpu as pltpu

f32 = jnp.float32
bf16 = jnp.bfloat16

D_MODEL = 1024
EPS = 1e-6
N_MEM = 256
SSD_HEADS = 8
SSD_HEAD_DIM = 64
SSD_WIDTH = 512
SSD_GROUPS = 2
SSD_STATE = 128
SSD_CONV = 4
SSD_CHUNK = 128
SSD_CONV_DIM = 1024
ATT_HEADS = 8
ATT_KV_HEADS = 2
ATT_HEAD_DIM = 64
ATT_WIDTH = 512
ATT_KV_WIDTH = 128
WINDOW = 128
ATT_SCALE = ATT_HEAD_DIM ** -0.5
X_HEADS = 4
X_HEAD_DIM = 256
X_SCALE = X_HEAD_DIM ** -0.5
D_FF = 2816
LANES = 128
SUBLANES = 8
VMEM_LIMIT = 56 * 1024 * 1024

P_Z, P_XBC, P_Q, P_K, P_V, P_DT, P_END = 0, 512, 1536, 2048, 2176, 2304, 2432
S_Z, S_XBC, S_Q, S_K, S_V, S_DT, S_END = 0, 512, 1536, 2560, 2688, 2816, 2944


def _dot(a, b):
    return jnp.dot(a.astype(bf16), b.astype(bf16), preferred_element_type=f32)


def _dot_nt(a, b):
    return lax.dot_general(a.astype(bf16), b.astype(bf16), (((1,), (1,)), ((), ())),
                           preferred_element_type=f32)


def _split2(x):
    hi = x.astype(bf16)
    lo = (x - hi.astype(f32)).astype(bf16)
    return hi, lo


def _dot_x2(x, m):
    hi, lo = _split2(x)
    return (jnp.dot(hi, m, preferred_element_type=f32)
            + jnp.dot(lo, m, preferred_element_type=f32))


def _dot_x3_left(m, x):
    hi = x.astype(bf16)
    r1 = x - hi.astype(f32)
    mid = r1.astype(bf16)
    lo = (r1 - mid.astype(f32)).astype(bf16)
    return (jnp.dot(m, hi, preferred_element_type=f32)
            + jnp.dot(m, mid, preferred_element_type=f32)
            + jnp.dot(m, lo, preferred_element_type=f32))


def _rmsnorm(x, g):
    ms = jnp.mean(x * x, axis=-1, keepdims=True)
    return x * lax.rsqrt(ms + EPS) * g


def _silu(x):
    return x * jax.nn.sigmoid(x)


def _softplus(x):
    return jnp.maximum(x, 0.0) + jnp.log1p(jnp.exp(-jnp.abs(x)))


def _head_expand_matrix():
    r = lax.broadcasted_iota(jnp.int32, (LANES, SSD_WIDTH), 0)
    c = lax.broadcasted_iota(jnp.int32, (LANES, SSD_WIDTH), 1)
    return (r == (c >> 6)).astype(bf16)


def _gated_group_norm(y, z, g):
    u = y * _silu(z)
    half = SSD_WIDTH // SSD_GROUPS
    parts = []
    for gi in range(SSD_GROUPS):
        ug = u[:, gi * half:(gi + 1) * half]
        parts.append(ug * lax.rsqrt(jnp.mean(ug * ug, axis=-1, keepdims=True) + EPS))
    return jnp.concatenate(parts, axis=-1) * g


def _sink_softmax(s, mask, sink):
    s = jnp.where(mask, s, -jnp.inf)
    m = jnp.maximum(jnp.max(s, axis=-1, keepdims=True), sink)
    e = jnp.exp(s - m)
    return e / (jnp.sum(e, axis=-1, keepdims=True) + jnp.exp(sink - m))


def _prompt_mixer_kernel(x_ref, gpre_ref, win_ref, convw_ref, convb_ref, dtb_ref, alog_ref,
                         dskip_ref, gssd_ref, sinks_ref, wout_ref, gpost_ref,
                         y_ref, ssm_ref, conv_ref, wk_ref, wv_ref,
                         statet_sc, xbc_ext_sc, xbc_sc, z_sc, q_sc, k_sc, v_sc, dt_sc,
                         kprev_sc, vprev_sc, mix_sc, *, tile):
    i = pl.program_id(1)
    nchunk = tile // SSD_CHUNK
    C = SSD_CHUNK

    @pl.when(i == 0)
    def _init():
        statet_sc[...] = jnp.zeros_like(statet_sc)
        xbc_ext_sc[0:SUBLANES, :] = jnp.zeros((SUBLANES, SSD_CONV_DIM), f32)
        kprev_sc[...] = jnp.zeros_like(kprev_sc)
        vprev_sc[...] = jnp.zeros_like(vprev_sc)

    x = x_ref[0]
    h = _rmsnorm(x, gpre_ref[...]).astype(bf16)
    z_sc[...] = jnp.dot(h, win_ref[:, P_Z:P_XBC], preferred_element_type=f32)
    xbc_ext_sc[SUBLANES:SUBLANES + tile, :] = jnp.dot(h, win_ref[:, P_XBC:P_Q],
                                                      preferred_element_type=f32)
    q_sc[...] = jnp.dot(h, win_ref[:, P_Q:P_K], preferred_element_type=f32)
    k_sc[...] = jnp.dot(h, win_ref[:, P_K:P_V], preferred_element_type=f32)
    v_sc[...] = jnp.dot(h, win_ref[:, P_V:P_DT], preferred_element_type=f32)
    dt_raw = jnp.dot(h, win_ref[:, P_DT:P_END], preferred_element_type=f32)
    lane = lax.broadcasted_iota(jnp.int32, (1, LANES), 1)
    dt_sc[...] = jnp.where(lane < SSD_HEADS, _softplus(dt_raw + dtb_ref[...]), 0.0)

    acc = convb_ref[...]
    for j in range(SSD_CONV):
        off = SUBLANES - (SSD_CONV - 1) + j
        acc = acc + xbc_ext_sc[off:off + tile, :] * convw_ref[j:j + 1, :]
    xbc_sc[...] = _silu(acc)
    tail = xbc_ext_sc[tile:tile + SUBLANES, :]
    conv_ref[0] = tail
    xbc_ext_sc[0:SUBLANES, :] = tail

    a_row = -jnp.exp(alog_ref[...])
    expand = _head_expand_matrix()
    row_i = lax.broadcasted_iota(jnp.int32, (C, C), 0)
    col_i = lax.broadcasted_iota(jnp.int32, (C, C), 1)
    lower = col_i <= row_i
    tri = lower.astype(bf16)
    lo_half = lane < ATT_HEAD_DIM
    qrow = lax.broadcasted_iota(jnp.int32, (C, 2 * C), 0)
    kcol = lax.broadcasted_iota(jnp.int32, (C, 2 * C), 1)
    band = (kcol > qrow) & (kcol <= qrow + WINDOW)
    cur_blk = kcol >= WINDOW
    half = SSD_WIDTH // SSD_GROUPS

    def chunk(c, carry):
        r0 = pl.multiple_of(c * C, C)
        rows = pl.ds(r0, C)
        xs = xbc_sc[rows, 0:SSD_WIDTH]
        bm = xbc_sc[rows, SSD_WIDTH:SSD_WIDTH + half]
        cm = xbc_sc[rows, SSD_WIDTH + half:SSD_CONV_DIM]
        dtc = dt_sc[rows, :]
        adt = dtc * a_row
        cs = _dot_x3_left(tri, adt)
        cs_t = cs.T
        tot = cs[C - 1:C, :]
        xdt = xs * _dot_x2(dtc, expand)
        ecs_e = _dot_x2(jnp.exp(cs), expand)
        w_end = xdt * _dot_x2(jnp.exp(tot - cs), expand)
        y_blocks = []
        for g in range(SSD_GROUPS):
            bg = bm[:, g * SSD_STATE:(g + 1) * SSD_STATE]
            cg = cm[:, g * SSD_STATE:(g + 1) * SSD_STATE]
            cb = _dot_nt(cg, bg)
            st = statet_sc[g]
            gl = slice(g * half, (g + 1) * half)
            y_off = _dot(cg, st) * ecs_e[:, gl]
            for jb in range(2):
                blk = g * 2 + jb
                xblk = xdt[:, blk * LANES:(blk + 1) * LANES]
                y_d = None
                for sub in range(2):
                    hh = blk * 2 + sub
                    diff = cs[:, hh:hh + 1] - cs_t[hh:hh + 1, :]
                    m_h = cb * jnp.exp(jnp.where(lower, diff, -jnp.inf))
                    x_h = jnp.where(lo_half, xblk, 0.0) if sub == 0 else jnp.where(lo_half, 0.0, xblk)
                    t_h = _dot(m_h, x_h)
                    y_d = t_h if y_d is None else y_d + t_h
                y_blocks.append(y_d + y_off[:, jb * LANES:(jb + 1) * LANES])
            statet_sc[g] = st * ecs_e[C - 1:C, gl] + _dot(bg.T, w_end[:, gl])
        y = jnp.concatenate(y_blocks, axis=-1) + xs * dskip_ref[...]
        y_ssd = _gated_group_norm(y, z_sc[rows, :], gssd_ref[...])
        mix_sc[rows, 0:SSD_WIDTH] = y_ssd.astype(bf16)

        q = q_sc[rows, :]
        k = k_sc[rows, :]
        v = v_sc[rows, :]
        kk = jnp.concatenate([kprev_sc[...], k], axis=0)
        vv = jnp.concatenate([vprev_sc[...], v], axis=0)
        kk_r = pltpu.roll(kk, ATT_HEAD_DIM, 1)
        vv_r = pltpu.roll(vv, ATT_HEAD_DIM, 1)
        k_lo = [jnp.where(lo_half, kk, 0.0).astype(bf16), jnp.where(lo_half, kk_r, 0.0).astype(bf16)]
        k_hi = [jnp.where(lo_half, 0.0, kk_r).astype(bf16), jnp.where(lo_half, 0.0, kk).astype(bf16)]
        v_lo = [jnp.where(lo_half, vv, 0.0).astype(bf16), jnp.where(lo_half, vv_r, 0.0).astype(bf16)]
        v_hi = [jnp.where(lo_half, 0.0, vv_r).astype(bf16), jnp.where(lo_half, 0.0, vv).astype(bf16)]
        has_prev = (i * nchunk + c) > 0
        mask = band & (cur_blk | has_prev)
        for blk in range(ATT_HEADS // 2):
            g = blk // 2
            qb = q[:, blk * LANES:(blk + 1) * LANES].astype(bf16)
            p_a = _sink_softmax(_dot_nt(qb, k_lo[g]) * ATT_SCALE, mask, sinks_ref[2 * blk])
            p_b = _sink_softmax(_dot_nt(qb, k_hi[g]) * ATT_SCALE, mask, sinks_ref[2 * blk + 1])
            o = _dot(p_a, v_lo[g]) + _dot(p_b, v_hi[g])
            mix_sc[rows, SSD_WIDTH + blk * LANES:SSD_WIDTH + (blk + 1) * LANES] = o.astype(bf16)
        kprev_sc[...] = k
        vprev_sc[...] = v
        return carry

    lax.fori_loop(0, nchunk, chunk, 0)

    wk_ref[0] = k_sc[tile - WINDOW:tile, :]
    wv_ref[0] = v_sc[tile - WINDOW:tile, :]

    @pl.when(i == pl.num_programs(1) - 1)
    def _final_state():
        for g in range(SSD_GROUPS):
            ssm_ref[0, g * half:(g + 1) * half, :] = statet_sc[g].T

    mix = jnp.dot(mix_sc[...], wout_ref[...], preferred_element_type=f32)
    y_ref[0] = x + _rmsnorm(mix, gpost_ref[...])


def _const_spec(shape):
    nd = len(shape)
    return pl.BlockSpec(shape, lambda *_: (0,) * nd)


def _prompt_mixer(x, gpre, win, convw, convb, dtb, alog, dskip_e, gssd, sinks, wout, gpost, tile):
    B, L, D = x.shape
    grid = (B, L // tile)
    kern = functools.partial(_prompt_mixer_kernel, tile=tile)
    out_shape = (
        jax.ShapeDtypeStruct((B, L, D), f32),
        jax.ShapeDtypeStruct((B, SSD_WIDTH, SSD_STATE), f32),
        jax.ShapeDtypeStruct((B, SUBLANES, SSD_CONV_DIM), f32),
        jax.ShapeDtypeStruct((B, WINDOW, ATT_KV_WIDTH), f32),
        jax.ShapeDtypeStruct((B, WINDOW, ATT_KV_WIDTH), f32),
    )
    in_specs = [
        pl.BlockSpec((1, tile, D), lambda b, i: (b, i, 0)),
        _const_spec(gpre.shape), _const_spec(win.shape), _const_spec(convw.shape),
        _const_spec(convb.shape), _const_spec(dtb.shape), _const_spec(alog.shape),
        _const_spec(dskip_e.shape), _const_spec(gssd.shape),
        pl.BlockSpec(memory_space=pltpu.SMEM),
        _const_spec(wout.shape), _const_spec(gpost.shape),
    ]
    out_specs = (
        pl.BlockSpec((1, tile, D), lambda b, i: (b, i, 0)),
        pl.BlockSpec((1, SSD_WIDTH, SSD_STATE), lambda b, i: (b, 0, 0)),
        pl.BlockSpec((1, SUBLANES, SSD_CONV_DIM), lambda b, i: (b, 0, 0)),
        pl.BlockSpec((1, WINDOW, ATT_KV_WIDTH), lambda b, i: (b, 0, 0)),
        pl.BlockSpec((1, WINDOW, ATT_KV_WIDTH), lambda b, i: (b, 0, 0)),
    )
    scratch = [
        pltpu.VMEM((SSD_GROUPS, SSD_STATE, SSD_WIDTH // SSD_GROUPS), f32),
        pltpu.VMEM((tile + 2 * SUBLANES, SSD_CONV_DIM), f32),
        pltpu.VMEM((tile, SSD_CONV_DIM), f32),
        pltpu.VMEM((tile, SSD_WIDTH), f32),
        pltpu.VMEM((tile, ATT_WIDTH), f32),
        pltpu.VMEM((tile, ATT_KV_WIDTH), f32),
        pltpu.VMEM((tile, ATT_KV_WIDTH), f32),
        pltpu.VMEM((tile, LANES), f32),
        pltpu.VMEM((WINDOW, ATT_KV_WIDTH), f32),
        pltpu.VMEM((WINDOW, ATT_KV_WIDTH), f32),
        pltpu.VMEM((tile, 2 * SSD_WIDTH), bf16),
    ]
    return pl.pallas_call(
        kern, grid=grid, in_specs=in_specs, out_specs=out_specs, out_shape=out_shape,
        scratch_shapes=scratch, name="prompt_mixer",
        compiler_params=pltpu.CompilerParams(
            dimension_semantics=("arbitrary", "arbitrary"), vmem_limit_bytes=VMEM_LIMIT),
    )(x, gpre, win, convw, convb, dtb, alog, dskip_e, gssd, sinks, wout, gpost)


def _memkv_kernel(m_ref, g_ref, wk_ref, wv_ref, k_ref, v_ref):
    mn = _rmsnorm(m_ref[...], g_ref[...]).astype(bf16)
    k_ref[...] = jnp.dot(mn, wk_ref[...], preferred_element_type=f32)
    v_ref[...] = jnp.dot(mn, wv_ref[...], preferred_element_type=f32)


def _memkv(mem2d, g, wk, wv, tile):
    n, d = mem2d.shape
    row = pl.BlockSpec((tile, d), lambda i: (i, 0))
    return pl.pallas_call(
        _memkv_kernel, grid=(n // tile,),
        in_specs=[row, _const_spec(g.shape), _const_spec(wk.shape), _const_spec(wv.shape)],
        out_specs=(row, row),
        out_shape=(jax.ShapeDtypeStruct((n, d), f32), jax.ShapeDtypeStruct((n, d), f32)),
        name="memory_kv",
        compiler_params=pltpu.CompilerParams(
            dimension_semantics=("arbitrary",), vmem_limit_bytes=VMEM_LIMIT),
    )(mem2d, g, wk, wv)


def _prompt_xattn_kernel(x_ref, gpre_ref, wq_ref, mk_ref, mv_ref, wo_ref, gpost_ref, y_ref):
    x = x_ref[0]
    hn = _rmsnorm(x, gpre_ref[...]).astype(bf16)
    q = jnp.dot(hn, wq_ref[...], preferred_element_type=f32)
    outs = []
    for hd in range(X_HEADS):
        sl = slice(hd * X_HEAD_DIM, (hd + 1) * X_HEAD_DIM)
        s = _dot_nt(q[:, sl], mk_ref[0, :, sl]) * X_SCALE
        m = jnp.max(s, axis=-1, keepdims=True)
        e = jnp.exp(s - m)
        p = e / jnp.sum(e, axis=-1, keepdims=True)
        outs.append(_dot(p, mv_ref[0, :, sl]))
    o = jnp.concatenate(outs, axis=-1)
    c = _dot(o, wo_ref[...])
    y_ref[0] = x + _rmsnorm(c, gpost_ref[...])


def _prompt_xattn(x, gpre, wq, mk, mv, wo, gpost, tile):
    B, L, D = x.shape
    xs = pl.BlockSpec((1, tile, D), lambda b, i: (b, i, 0))
    ms = pl.BlockSpec((1, N_MEM, D), lambda b, i: (b, 0, 0))
    return pl.pallas_call(
        _prompt_xattn_kernel, grid=(B, L // tile),
        in_specs=[xs, _const_spec(gpre.shape), _const_spec(wq.shape), ms, ms,
                  _const_spec(wo.shape), _const_spec(gpost.shape)],
        out_specs=xs, out_shape=jax.ShapeDtypeStruct((B, L, D), f32),
        name="prompt_xattn",
        compiler_params=pltpu.CompilerParams(
            dimension_semantics=("arbitrary", "arbitrary"), vmem_limit_bytes=VMEM_LIMIT),
    )(x, gpre, wq, mk, mv, wo, gpost)


def _ffn_kernel(x_ref, gpre_ref, wg_ref, wu_ref, wd_ref, gpost_ref, y_ref):
    x = x_ref[...]
    hf = _rmsnorm(x, gpre_ref[...]).astype(bf16)
    gate = jnp.dot(hf, wg_ref[...], preferred_element_type=f32)
    up = jnp.dot(hf, wu_ref[...], preferred_element_type=f32)
    act = (_silu(gate) * up).astype(bf16)
    f = jnp.dot(act, wd_ref[...], preferred_element_type=f32)
    y_ref[...] = x + _rmsnorm(f, gpost_ref[...])


def _ffn(x2d, gpre, wg, wu, wd, gpost, tile):
    n, d = x2d.shape
    row = pl.BlockSpec((tile, d), lambda i: (i, 0))
    return pl.pallas_call(
        _ffn_kernel, grid=(n // tile,),
        in_specs=[row, _const_spec(gpre.shape), _const_spec(wg.shape), _const_spec(wu.shape),
                  _const_spec(wd.shape), _const_spec(gpost.shape)],
        out_specs=row, out_shape=jax.ShapeDtypeStruct((n, d), f32),
        name="ffn",
        compiler_params=pltpu.CompilerParams(
            dimension_semantics=("arbitrary",), vmem_limit_bytes=VMEM_LIMIT),
    )(x2d, gpre, wg, wu, wd, gpost)


def _pad_rows(a, rows):
    if a.shape[0] == rows:
        return a
    return jnp.concatenate([a, jnp.zeros((rows - a.shape[0], a.shape[1]), a.dtype)], axis=0)


def _sample_mixer_kernel(x_ref, cprev_ref, st_ref, ck_ref, cv_ref,
                         gpre_ref, win_ref, convw_ref, convb_ref, dtb_ref, alog_ref,
                         dskip_ref, gssd_ref, sinkcol_ref, wout_ref, gpost_ref,
                         y_ref, ssm_ref, cnew_ref, knew_ref, vnew_ref, *, bt, steps):
    R = steps * bt
    half = SSD_WIDTH // SSD_GROUPS
    x = x_ref[...].reshape(R, D_MODEL)
    h = _rmsnorm(x, gpre_ref[...]).astype(bf16)
    z = jnp.dot(h, win_ref[:, S_Z:S_XBC], preferred_element_type=f32)
    u = jnp.dot(h, win_ref[:, S_XBC:S_Q], preferred_element_type=f32)
    qpad = jnp.dot(h, win_ref[:, S_Q:S_K], preferred_element_type=f32)
    k_new = jnp.dot(h, win_ref[:, S_K:S_V], preferred_element_type=f32)
    v_new = jnp.dot(h, win_ref[:, S_V:S_DT], preferred_element_type=f32)
    dt_raw = jnp.dot(h, win_ref[:, S_DT:S_END], preferred_element_type=f32)
    lane = lax.broadcasted_iota(jnp.int32, (1, LANES), 1)
    dt = jnp.where(lane < SSD_HEADS, _softplus(dt_raw + dtb_ref[...]), 0.0)

    def slab(a, t):
        return a[t * bt:(t + 1) * bt]

    hist = [cprev_ref[j] for j in range(SSD_CONV - 1)] + [slab(u, t) for t in range(steps)]
    xbc_t = []
    for t in range(steps):
        acc = convb_ref[...]
        for j in range(SSD_CONV):
            acc = acc + hist[t + j] * convw_ref[j:j + 1, :]
        xbc_t.append(_silu(acc))
    for j in range(SSD_CONV - 1):
        cnew_ref[j] = hist[steps + j]
    xbc = jnp.concatenate(xbc_t, axis=0)
    xs = xbc[:, 0:SSD_WIDTH]
    bm = xbc[:, SSD_WIDTH:SSD_WIDTH + half]
    cm = xbc[:, SSD_WIDTH + half:SSD_CONV_DIM]

    a_row = -jnp.exp(alog_ref[...])
    adt = dt * a_row
    cs_t = [slab(adt, 0)]
    for t in range(1, steps):
        cs_t.append(cs_t[-1] + slab(adt, t))
    cs = jnp.concatenate(cs_t, axis=0)
    tot = cs_t[-1]
    tot_rows = jnp.concatenate([tot] * steps, axis=0)
    expand = _head_expand_matrix()
    xdt = xs * _dot_x2(dt, expand)
    ecs_e = _dot_x2(jnp.exp(cs), expand)
    w_end = xdt * _dot_x2(jnp.exp(tot_rows - cs), expand)
    dec_e = _dot_x2(jnp.exp(tot), expand)

    gr = lax.broadcasted_iota(jnp.int32, (half, SSD_WIDTH), 0)
    gc = lax.broadcasted_iota(jnp.int32, (half, SSD_WIDTH), 1)
    gsum = ((gr >> 7) == (gc >> 8)).astype(bf16)

    y_t = []
    for t in range(steps):
        acc = None
        for s in range(t + 1):
            coef = _dot_x2(slab(cm, t) * slab(bm, s), gsum)
            if s < t:
                coef = coef * _dot_x2(jnp.exp(cs_t[t] - cs_t[s]), expand)
            term = coef * slab(xdt, s)
            acc = term if acc is None else acc + term
        y_t.append(acc)
    y_intra = jnp.concatenate(y_t, axis=0)

    b_idx = lax.broadcasted_iota(jnp.int32, (bt, 1, LANES), 0)
    l_idx = lax.broadcasted_iota(jnp.int32, (bt, 1, LANES), 2)
    pair = ((l_idx & (bt - 1)) == b_idx) & (l_idx < R)
    own = (l_idx == b_idx)
    y_off_parts = []
    for g in range(SSD_GROUPS):
        gl = slice(g * half, (g + 1) * half)
        h0 = st_ref[:, gl, :]
        cg = _pad_rows(cm[:, g * SSD_STATE:(g + 1) * SSD_STATE], LANES)
        zz = _dot_nt(h0.reshape(bt * half, SSD_STATE), cg).reshape(bt, half, LANES)
        yt = jnp.sum(jnp.where(pair, zz, 0.0), axis=0)
        y_off_parts.append(yt.T[0:R, :])
        wt = _pad_rows(w_end[:, gl], LANES).T
        lhs = jnp.where(pair, wt[None], 0.0).reshape(bt * half, LANES)
        bg = _pad_rows(bm[:, g * SSD_STATE:(g + 1) * SSD_STATE], LANES)
        contrib = _dot(lhs, bg).reshape(bt, half, SSD_STATE)
        dec_t = _pad_rows(dec_e[:, gl], LANES).T
        dec = jnp.sum(jnp.where(own, dec_t[None], 0.0), axis=-1, keepdims=True)
        ssm_ref[:, gl, :] = h0 * dec + contrib
    y_off = jnp.concatenate(y_off_parts, axis=-1) * ecs_e
    y = y_intra + y_off + xs * dskip_ref[...]
    y_ssd = _gated_group_norm(y, z, gssd_ref[...])

    qr = jnp.concatenate([qpad[:, hd * LANES:(hd + 1) * LANES] for hd in range(ATT_HEADS)], axis=0)
    HR = ATT_HEADS * R
    qr_b = qr.astype(bf16)
    ridx = lax.broadcasted_iota(jnp.int32, (HR, 1), 0)
    rb = ridx & (bt - 1)
    rt = (ridx // bt) & (steps - 1)
    s_c = jnp.zeros((HR, WINDOW), f32)
    for b in range(bt):
        s_c = jnp.where(rb == b, _dot_nt(qr_b, ck_ref[b]), s_c)
    s_c = s_c * ATT_SCALE
    reps = HR // bt
    s_n = []
    for t2 in range(steps):
        kt = jnp.concatenate([slab(k_new, t2)] * reps, axis=0)
        s_n.append(jnp.sum(qr * kt, axis=-1, keepdims=True) * ATT_SCALE)
    jcol = lax.broadcasted_iota(jnp.int32, (HR, WINDOW), 1)
    mask_c = jcol > rt
    sink = sinkcol_ref[...]
    s_c = jnp.where(mask_c, s_c, -jnp.inf)
    m = jnp.maximum(jnp.max(s_c, axis=-1, keepdims=True), sink)
    for t2 in range(steps):
        s_n[t2] = jnp.where(rt >= t2, s_n[t2], -jnp.inf)
        m = jnp.maximum(m, s_n[t2])
    e_c = jnp.exp(s_c - m)
    e_n = [jnp.exp(s - m) for s in s_n]
    den = jnp.sum(e_c, axis=-1, keepdims=True) + jnp.exp(sink - m)
    for e in e_n:
        den = den + e
    p_c = (e_c / den).astype(bf16)
    o = jnp.zeros((HR, LANES), f32)
    for b in range(bt):
        o = jnp.where(rb == b, _dot(p_c, cv_ref[b]), o)
    for t2 in range(steps):
        vt = jnp.concatenate([slab(v_new, t2)] * reps, axis=0)
        o = o + (e_n[t2] / den) * vt
    y_att = jnp.concatenate([o[hd * R:(hd + 1) * R] for hd in range(ATT_HEADS)], axis=-1)

    mix = (jnp.dot(y_ssd.astype(bf16), wout_ref[0:SSD_WIDTH, :], preferred_element_type=f32)
           + jnp.dot(y_att.astype(bf16), wout_ref[SSD_WIDTH:, :], preferred_element_type=f32))
    y_ref[...] = (x + _rmsnorm(mix, gpost_ref[...])).reshape(steps, bt, D_MODEL)
    knew_ref[...] = k_new.reshape(steps, bt, ATT_KV_WIDTH)
    vnew_ref[...] = v_new.reshape(steps, bt, ATT_KV_WIDTH)


def _sample_mixer(x_tm, cprev_tm, st, ck, cv, gpre, win, convw, convb, dtb, alog, dskip_e, gssd,
                  sinkcol, wout, gpost, bt):
    steps, nb, D = x_tm.shape
    kern = functools.partial(_sample_mixer_kernel, bt=bt, steps=steps)
    tm = lambda w: pl.BlockSpec((steps, bt, w), lambda i: (0, i, 0))
    in_specs = [
        tm(D),
        pl.BlockSpec((SSD_CONV - 1, bt, SSD_CONV_DIM), lambda i: (0, i, 0)),
        pl.BlockSpec((bt, SSD_WIDTH, SSD_STATE), lambda i: (i, 0, 0)),
        pl.BlockSpec((bt, WINDOW, ATT_KV_WIDTH), lambda i: (i, 0, 0)),
        pl.BlockSpec((bt, WINDOW, ATT_KV_WIDTH), lambda i: (i, 0, 0)),
    ] + [_const_spec(a.shape) for a in (gpre, win, convw, convb, dtb, alog, dskip_e, gssd,
                                        sinkcol, wout, gpost)]
    out_specs = (
        tm(D),
        pl.BlockSpec((bt, SSD_WIDTH, SSD_STATE), lambda i: (i, 0, 0)),
        pl.BlockSpec((SSD_CONV - 1, bt, SSD_CONV_DIM), lambda i: (0, i, 0)),
        tm(ATT_KV_WIDTH), tm(ATT_KV_WIDTH),
    )
    out_shape = (
        jax.ShapeDtypeStruct((steps, nb, D), f32),
        jax.ShapeDtypeStruct((nb, SSD_WIDTH, SSD_STATE), f32),
        jax.ShapeDtypeStruct((SSD_CONV - 1, nb, SSD_CONV_DIM), f32),
        jax.ShapeDtypeStruct((steps, nb, ATT_KV_WIDTH), f32),
        jax.ShapeDtypeStruct((steps, nb, ATT_KV_WIDTH), f32),
    )
    return pl.pallas_call(
        kern, grid=(nb // bt,), in_specs=in_specs, out_specs=out_specs, out_shape=out_shape,
        name="sample_mixer",
        compiler_params=pltpu.CompilerParams(
            dimension_semantics=("arbitrary",), vmem_limit_bytes=VMEM_LIMIT),
    )(x_tm, cprev_tm, st, ck, cv, gpre, win, convw, convb, dtb, alog, dskip_e, gssd, sinkcol,
      wout, gpost)


def _sample_xattn_kernel(x_ref, mk_ref, mv_ref, gpre_ref, wq_ref, wo_ref, gpost_ref, y_ref,
                         *, bt, steps):
    R = steps * bt
    x = x_ref[...].reshape(R, D_MODEL)
    hn = _rmsnorm(x, gpre_ref[...]).astype(bf16)
    q = _pad_rows(jnp.dot(hn, wq_ref[...], preferred_element_type=f32), LANES).astype(bf16)
    b_idx = lax.broadcasted_iota(jnp.int32, (bt, 1, LANES), 0)
    l_idx = lax.broadcasted_iota(jnp.int32, (bt, 1, LANES), 2)
    pair = ((l_idx & (bt - 1)) == b_idx) & (l_idx < R)
    ridx = lax.broadcasted_iota(jnp.int32, (LANES, 1), 0)
    rb = ridx & (bt - 1)
    outs = []
    for hd in range(X_HEADS):
        sl = slice(hd * X_HEAD_DIM, (hd + 1) * X_HEAD_DIM)
        kh = mk_ref[:, :, sl].reshape(bt * N_MEM, X_HEAD_DIM)
        zz = _dot_nt(kh, q[:, sl]).reshape(bt, N_MEM, LANES)
        st = jnp.sum(jnp.where(pair, zz, 0.0), axis=0) * X_SCALE
        m = jnp.max(st, axis=0, keepdims=True)
        e = jnp.exp(st - m)
        p = (e / jnp.sum(e, axis=0, keepdims=True)).T.astype(bf16)
        o = jnp.zeros((LANES, X_HEAD_DIM), f32)
        for b in range(bt):
            o = jnp.where(rb == b, _dot(p, mv_ref[b, :, sl]), o)
        outs.append(o[0:R])
    o = jnp.concatenate(outs, axis=-1)
    c = _dot(o, wo_ref[...])
    y_ref[...] = (x + _rmsnorm(c, gpost_ref[...])).reshape(steps, bt, D_MODEL)


def _sample_xattn(x_tm, mk, mv, gpre, wq, wo, gpost, bt):
    steps, nb, D = x_tm.shape
    kern = functools.partial(_sample_xattn_kernel, bt=bt, steps=steps)
    xs = pl.BlockSpec((steps, bt, D), lambda i: (0, i, 0))
    ms = pl.BlockSpec((bt, N_MEM, D), lambda i: (i, 0, 0))
    return pl.pallas_call(
        kern, grid=(nb // bt,),
        in_specs=[xs, ms, ms, _const_spec(gpre.shape), _const_spec(wq.shape),
                  _const_spec(wo.shape), _const_spec(gpost.shape)],
        out_specs=xs, out_shape=jax.ShapeDtypeStruct((steps, nb, D), f32),
        name="sample_xattn",
        compiler_params=pltpu.CompilerParams(
            dimension_semantics=("arbitrary",), vmem_limit_bytes=VMEM_LIMIT),
    )(x_tm, mk, mv, gpre, wq, wo, gpost)


def _row(v, width=None):
    v = v.reshape(1, -1).astype(f32)
    if width is not None and v.shape[1] < width:
        v = jnp.pad(v, ((0, 0), (0, width - v.shape[1])))
    return v


def kernel(x_prompt, x_sample, state_ssm, state_conv, cache_win_k, cache_win_v, cache_mem_k, cache_mem_v, mem_prompt, g_mix_pre, w_in, conv_w, conv_b, dt_bias, a_log, d_skip, g_ssd_norm, sinks, w_out, g_mix_post, g_x_pre, w_xq, g_mem, w_xk, w_xv, w_xo, g_x_post, g_ffn_pre, w_gate, w_up, w_down, g_ffn_post):
    depth = w_in.shape[0]
    assert depth == 1
    B, L, D = x_prompt.shape
    NB, steps, _ = x_sample.shape
    li = 0

    w = w_in[li]
    wz, wxbc, wdt = w[:, 0:512], w[:, 512:1536], w[:, 1536:1544]
    wq, wk, wv = w[:, 1544:2056], w[:, 2056:2184], w[:, 2184:2312]
    wdt_p = jnp.pad(wdt, ((0, 0), (0, LANES - SSD_HEADS)))
    win_p = jnp.concatenate([wz, wxbc, wq, wk, wv, wdt_p], axis=1).astype(bf16)
    zero_half = jnp.zeros((D, ATT_HEAD_DIM), f32)
    q_blocks = []
    for hd in range(ATT_HEADS):
        wq_h = wq[:, hd * ATT_HEAD_DIM:(hd + 1) * ATT_HEAD_DIM]
        q_blocks += [wq_h, zero_half] if hd < ATT_HEADS // ATT_KV_HEADS else [zero_half, wq_h]
    win_s = jnp.concatenate([wz, wxbc] + q_blocks + [wk, wv, wdt_p], axis=1).astype(bf16)
    wo = w_out[li]
    wo_b = wo.astype(bf16)
    zero_rows = jnp.zeros((ATT_HEAD_DIM, D), f32)
    o_blocks = [wo[0:SSD_WIDTH]]
    for hd in range(ATT_HEADS):
        wo_h = wo[SSD_WIDTH + hd * ATT_HEAD_DIM:SSD_WIDTH + (hd + 1) * ATT_HEAD_DIM]
        o_blocks += [wo_h, zero_rows] if hd < ATT_HEADS // ATT_KV_HEADS else [zero_rows, wo_h]
    wo_s = jnp.concatenate(o_blocks, axis=0).astype(bf16)

    gpre, gpost = _row(g_mix_pre[li]), _row(g_mix_post[li])
    convw, convb = conv_w[li].astype(f32), _row(conv_b[li])
    dtb, alog = _row(dt_bias[li], LANES), _row(a_log[li], LANES)
    dskip_e = _row(jnp.repeat(d_skip[li], SSD_HEAD_DIM))
    gssd = _row(g_ssd_norm[li])
    sk = sinks[li].astype(f32)

    mk2d, mv2d = _memkv(mem_prompt.reshape(B * N_MEM, D), _row(g_mem[li]),
                        w_xk[li].astype(bf16), w_xv[li].astype(bf16), tile=512)
    mk3, mv3 = mk2d.reshape(B, N_MEM, D), mv2d.reshape(B, N_MEM, D)
    x1, p_ssm, p_conv8, p_wk, p_wv = _prompt_mixer(
        x_prompt, gpre, win_p, convw, convb, dtb, alog, dskip_e, gssd, sk, wo_b, gpost, tile=512)
    wxq_b, wxo_b = w_xq[li].astype(bf16), w_xo[li].astype(bf16)
    gxpre, gxpost = _row(g_x_pre[li]), _row(g_x_post[li])
    x2 = _prompt_xattn(x1, gxpre, wxq_b, mk3, mv3, wxo_b, gxpost, tile=512)
    wg_b, wu_b, wd_b = w_gate[li].astype(bf16), w_up[li].astype(bf16), w_down[li].astype(bf16)
    gfpre, gfpost = _row(g_ffn_pre[li]), _row(g_ffn_post[li])
    yp = _ffn(x2.reshape(B * L, D), gfpre, wg_b, wu_b, wd_b, gfpost, tile=512).reshape(B, L, D)

    bt = 8
    x_tm = jnp.transpose(x_sample, (1, 0, 2))
    cprev_tm = jnp.transpose(state_conv[li], (1, 0, 2))
    st = state_ssm[li].reshape(NB, SSD_WIDTH, SSD_STATE)
    ck = cache_win_k[li].reshape(NB, WINDOW, ATT_KV_WIDTH)
    cv = cache_win_v[li].reshape(NB, WINDOW, ATT_KV_WIDTH)
    sinkcol = jnp.repeat(sk, steps * bt).reshape(ATT_HEADS * steps * bt, 1)
    x1s, s_ssm, cnew_tm, knew_tm, vnew_tm = _sample_mixer(
        x_tm, cprev_tm, st, ck, cv, gpre, win_s, convw, convb, dtb, alog, dskip_e, gssd,
        sinkcol, wo_s, gpost, bt=bt)
    cmk = cache_mem_k[li].reshape(NB, N_MEM, D)
    cmv = cache_mem_v[li].reshape(NB, N_MEM, D)
    x2s = _sample_xattn(x1s, cmk, cmv, gxpre, wxq_b, wxo_b, gxpost, bt=bt)
    ys_tm = _ffn(x2s.reshape(steps * NB, D), gfpre, wg_b, wu_b, wd_b, gfpost, tile=steps * NB)
    ys = jnp.transpose(ys_tm.reshape(steps, NB, D), (1, 0, 2))

    s_conv = jnp.transpose(cnew_tm, (1, 0, 2))
    knew = jnp.transpose(knew_tm, (1, 0, 2))
    vnew = jnp.transpose(vnew_tm, (1, 0, 2))
    s_wk = jnp.concatenate([ck[:, steps:], knew], axis=1)
    s_wv = jnp.concatenate([cv[:, steps:], vnew], axis=1)
    kv_shape = (ATT_KV_HEADS, ATT_HEAD_DIM)
    return (
        yp, ys,
        p_ssm.reshape(1, B, SSD_HEADS, SSD_HEAD_DIM, SSD_STATE),
        p_conv8[:, SUBLANES - (SSD_CONV - 1):, :][None],
        p_wk.reshape(1, B, WINDOW, *kv_shape), p_wv.reshape(1, B, WINDOW, *kv_shape),
        mk3.reshape(1, B, N_MEM, X_HEADS, X_HEAD_DIM), mv3.reshape(1, B, N_MEM, X_HEADS, X_HEAD_DIM),
        s_ssm.reshape(1, NB, SSD_HEADS, SSD_HEAD_DIM, SSD_STATE),
        s_conv[None],
        s_wk.reshape(1, NB, WINDOW, *kv_shape), s_wv.reshape(1, NB, WINDOW, *kv_shape),
    )
```

```python
import functools

import jax
import jax.numpy as jnp
from jax import lax
from jax.experimental import pallas as pl
from jax.experimental.pallas import tpu as pltpu

f32 = jnp.float32
bf16 = jnp.bfloat16

D_MODEL = 1024
EPS = 1e-6
N_MEM = 256
SSD_HEADS = 8
SSD_HEAD_DIM = 64
SSD_WIDTH = 512
SSD_GROUPS = 2
SSD_STATE = 128
SSD_CONV = 4
SSD_CHUNK = 128
SSD_CONV_DIM = 1024
ATT_HEADS = 8
ATT_KV_HEADS = 2
ATT_HEAD_DIM = 64
ATT_WIDTH = 512
ATT_KV_WIDTH = 128
WINDOW = 128
ATT_SCALE = ATT_HEAD_DIM ** -0.5
X_HEADS = 4
X_HEAD_DIM = 256
X_SCALE = X_HEAD_DIM ** -0.5
D_FF = 2816
LANES = 128
SUBLANES = 8
VMEM_LIMIT = 56 * 1024 * 1024

P_Z, P_XBC, P_Q, P_K, P_V, P_DT, P_END = 0, 512, 1536, 2048, 2176, 2304, 2432
S_Z, S_XBC, S_Q, S_K, S_V, S_DT, S_END = 0, 512, 1536, 2560, 2688, 2816, 2944


def _dot(a, b):
    return jnp.dot(a.astype(bf16), b.astype(bf16), preferred_element_type=f32)


def _dot_nt(a, b):
    return lax.dot_general(a.astype(bf16), b.astype(bf16), (((1,), (1,)), ((), ())),
                           preferred_element_type=f32)


def _split2(x):
    hi = x.astype(bf16)
    lo = (x - hi.astype(f32)).astype(bf16)
    return hi, lo


def _dot_x2(x, m):
    hi, lo = _split2(x)
    return (jnp.dot(hi, m, preferred_element_type=f32)
            + jnp.dot(lo, m, preferred_element_type=f32))


def _dot_x3_left(m, x):
    hi = x.astype(bf16)
    r1 = x - hi.astype(f32)
    mid = r1.astype(bf16)
    lo = (r1 - mid.astype(f32)).astype(bf16)
    return (jnp.dot(m, hi, preferred_element_type=f32)
            + jnp.dot(m, mid, preferred_element_type=f32)
            + jnp.dot(m, lo, preferred_element_type=f32))


def _rmsnorm(x, g):
    ms = jnp.mean(x * x, axis=-1, keepdims=True)
    return x * lax.rsqrt(ms + EPS) * g


def _silu(x):
    return x * jax.nn.sigmoid(x)


def _softplus(x):
    return jnp.maximum(x, 0.0) + jnp.log1p(jnp.exp(-jnp.abs(x)))


def _head_expand_matrix():
    r = lax.broadcasted_iota(jnp.int32, (LANES, SSD_WIDTH), 0)
    c = lax.broadcasted_iota(jnp.int32, (LANES, SSD_WIDTH), 1)
    return (r == (c >> 6)).astype(bf16)


def _gated_group_norm(y, z, g):
    u = y * _silu(z)
    half = SSD_WIDTH // SSD_GROUPS
    parts = []
    for gi in range(SSD_GROUPS):
        ug = u[:, gi * half:(gi + 1) * half]
        parts.append(ug * lax.rsqrt(jnp.mean(ug * ug, axis=-1, keepdims=True) + EPS))
    return jnp.concatenate(parts, axis=-1) * g


def _sink_softmax(s, mask, sink):
    s = jnp.where(mask, s, -jnp.inf)
    m = jnp.maximum(jnp.max(s, axis=-1, keepdims=True), sink)
    e = jnp.exp(s - m)
    return e / (jnp.sum(e, axis=-1, keepdims=True) + jnp.exp(sink - m))


def _prompt_mixer_kernel(x_ref, gpre_ref, win_ref, convw_ref, convb_ref, dtb_ref, alog_ref,
                         dskip_ref, gssd_ref, sinks_ref, wout_ref, gpost_ref,
                         y_ref, ssm_ref, conv_ref, wk_ref, wv_ref,
                         statet_sc, xbc_ext_sc, xbc_sc, z_sc, q_sc, k_sc, v_sc, dt_sc,
                         kprev_sc, vprev_sc, mix_sc, *, tile):
    i = pl.program_id(1)
    nchunk = tile // SSD_CHUNK
    C = SSD_CHUNK

    @pl.when(i == 0)
    def _init():
        statet_sc[...] = jnp.zeros_like(statet_sc)
        xbc_ext_sc[0:SUBLANES, :] = jnp.zeros((SUBLANES, SSD_CONV_DIM), f32)
        kprev_sc[...] = jnp.zeros_like(kprev_sc)
        vprev_sc[...] = jnp.zeros_like(vprev_sc)

    x = x_ref[0]
    h = _rmsnorm(x, gpre_ref[...]).astype(bf16)
    z_sc[...] = jnp.dot(h, win_ref[:, P_Z:P_XBC], preferred_element_type=f32)
    xbc_ext_sc[SUBLANES:SUBLANES + tile, :] = jnp.dot(h, win_ref[:, P_XBC:P_Q],
                                                      preferred_element_type=f32)
    q_sc[...] = jnp.dot(h, win_ref[:, P_Q:P_K], preferred_element_type=f32)
    k_sc[...] = jnp.dot(h, win_ref[:, P_K:P_V], preferred_element_type=f32)
    v_sc[...] = jnp.dot(h, win_ref[:, P_V:P_DT], preferred_element_type=f32)
    dt_raw = jnp.dot(h, win_ref[:, P_DT:P_END], preferred_element_type=f32)
    lane = lax.broadcasted_iota(jnp.int32, (1, LANES), 1)
    dt_sc[...] = jnp.where(lane < SSD_HEADS, _softplus(dt_raw + dtb_ref[...]), 0.0)

    acc = convb_ref[...]
    for j in range(SSD_CONV):
        off = SUBLANES - (SSD_CONV - 1) + j
        acc = acc + xbc_ext_sc[off:off + tile, :] * convw_ref[j:j + 1, :]
    xbc_sc[...] = _silu(acc)
    tail = xbc_ext_sc[tile:tile + SUBLANES, :]
    conv_ref[0] = tail
    xbc_ext_sc[0:SUBLANES, :] = tail

    a_row = -jnp.exp(alog_ref[...])
    expand = _head_expand_matrix()
    row_i = lax.broadcasted_iota(jnp.int32, (C, C), 0)
    col_i = lax.broadcasted_iota(jnp.int32, (C, C), 1)
    lower = col_i <= row_i
    tri = lower.astype(bf16)
    lo_half = lane < ATT_HEAD_DIM
    qrow = lax.broadcasted_iota(jnp.int32, (C, 2 * C), 0)
    kcol = lax.broadcasted_iota(jnp.int32, (C, 2 * C), 1)
    band = (kcol > qrow) & (kcol <= qrow + WINDOW)
    cur_blk = kcol >= WINDOW
    half = SSD_WIDTH // SSD_GROUPS

    def chunk(c, carry):
        r0 = pl.multiple_of(c * C, C)
        rows = pl.ds(r0, C)
        xs = xbc_sc[rows, 0:SSD_WIDTH]
        bm = xbc_sc[rows, SSD_WIDTH:SSD_WIDTH + half]
        cm = xbc_sc[rows, SSD_WIDTH + half:SSD_CONV_DIM]
        dtc = dt_sc[rows, :]
        adt = dtc * a_row
        cs = _dot_x3_left(tri, adt)
        cs_t = cs.T
        tot = cs[C - 1:C, :]
        xdt = xs * _dot_x2(dtc, expand)
        ecs_e = _dot_x2(jnp.exp(cs), expand)
        w_end = xdt * _dot_x2(jnp.exp(tot - cs), expand)
        y_blocks = []
        for g in range(SSD_GROUPS):
            bg = bm[:, g * SSD_STATE:(g + 1) * SSD_STATE]
            cg = cm[:, g * SSD_STATE:(g + 1) * SSD_STATE]
            cb = _dot_nt(cg, bg)
            st = statet_sc[g]
            gl = slice(g * half, (g + 1) * half)
            y_off = _dot(cg, st) * ecs_e[:, gl]
            for jb in range(2):
                blk = g * 2 + jb
                xblk = xdt[:, blk * LANES:(blk + 1) * LANES]
                y_d = None
                for sub in range(2):
                    hh = blk * 2 + sub
                    diff = cs[:, hh:hh + 1] - cs_t[hh:hh + 1, :]
                    m_h = cb * jnp.exp(jnp.where(lower, diff, -jnp.inf))
                    x_h = jnp.where(lo_half, xblk, 0.0) if sub == 0 else jnp.where(lo_half, 0.0, xblk)
                    t_h = _dot(m_h, x_h)
                    y_d = t_h if y_d is None else y_d + t_h
                y_blocks.append(y_d + y_off[:, jb * LANES:(jb + 1) * LANES])
            statet_sc[g] = st * ecs_e[C - 1:C, gl] + _dot(bg.T, w_end[:, gl])
        y = jnp.concatenate(y_blocks, axis=-1) + xs * dskip_ref[...]
        y_ssd = _gated_group_norm(y, z_sc[rows, :], gssd_ref[...])
        mix_sc[rows, 0:SSD_WIDTH] = y_ssd.astype(bf16)

        q = q_sc[rows, :]
        k = k_sc[rows, :]
        v = v_sc[rows, :]
        kk = jnp.concatenate([kprev_sc[...], k], axis=0)
        vv = jnp.concatenate([vprev_sc[...], v], axis=0)
        kk_r = pltpu.roll(kk, ATT_HEAD_DIM, 1)
        vv_r = pltpu.roll(vv, ATT_HEAD_DIM, 1)
        k_lo = [jnp.where(lo_half, kk, 0.0).astype(bf16), jnp.where(lo_half, kk_r, 0.0).astype(bf16)]
        k_hi = [jnp.where(lo_half, 0.0, kk_r).astype(bf16), jnp.where(lo_half, 0.0, kk).astype(bf16)]
        v_lo = [jnp.where(lo_half, vv, 0.0).astype(bf16), jnp.where(lo_half, vv_r, 0.0).astype(bf16)]
        v_hi = [jnp.where(lo_half, 0.0, vv_r).astype(bf16), jnp.where(lo_half, 0.0, vv).astype(bf16)]
        has_prev = (i * nchunk + c) > 0
        mask = band & (cur_blk | has_prev)
        for blk in range(ATT_HEADS // 2):
            g = blk // 2
            qb = q[:, blk * LANES:(blk + 1) * LANES].astype(bf16)
            p_a = _sink_softmax(_dot_nt(qb, k_lo[g]) * ATT_SCALE, mask, sinks_ref[2 * blk])
            p_b = _sink_softmax(_dot_nt(qb, k_hi[g]) * ATT_SCALE, mask, sinks_ref[2 * blk + 1])
            o = _dot(p_a, v_lo[g]) + _dot(p_b, v_hi[g])
            mix_sc[rows, SSD_WIDTH + blk * LANES:SSD_WIDTH + (blk + 1) * LANES] = o.astype(bf16)
        kprev_sc[...] = k
        vprev_sc[...] = v
        return carry

    lax.fori_loop(0, nchunk, chunk, 0)

    wk_ref[0] = k_sc[tile - WINDOW:tile, :]
    wv_ref[0] = v_sc[tile - WINDOW:tile, :]

    @pl.when(i == pl.num_programs(1) - 1)
    def _final_state():
        for g in range(SSD_GROUPS):
            ssm_ref[0, g * half:(g + 1) * half, :] = statet_sc[g].T

    mix = jnp.dot(mix_sc[...], wout_ref[...], preferred_element_type=f32)
    y_ref[0] = x + _rmsnorm(mix, gpost_ref[...])


def _const_spec(shape):
    nd = len(shape)
    return pl.BlockSpec(shape, lambda *_: (0,) * nd)


def _prompt_mixer(x, gpre, win, convw, convb, dtb, alog, dskip_e, gssd, sinks, wout, gpost, tile):
    B, L, D = x.shape
    grid = (B, L // tile)
    kern = functools.partial(_prompt_mixer_kernel, tile=tile)
    out_shape = (
        jax.ShapeDtypeStruct((B, L, D), f32),
        jax.ShapeDtypeStruct((B, SSD_WIDTH, SSD_STATE), f32),
        jax.ShapeDtypeStruct((B, SUBLANES, SSD_CONV_DIM), f32),
        jax.ShapeDtypeStruct((B, WINDOW, ATT_KV_WIDTH), f32),
        jax.ShapeDtypeStruct((B, WINDOW, ATT_KV_WIDTH), f32),
    )
    in_specs = [
        pl.BlockSpec((1, tile, D), lambda b, i: (b, i, 0)),
        _const_spec(gpre.shape), _const_spec(win.shape), _const_spec(convw.shape),
        _const_spec(convb.shape), _const_spec(dtb.shape), _const_spec(alog.shape),
        _const_spec(dskip_e.shape), _const_spec(gssd.shape),
        pl.BlockSpec(memory_space=pltpu.SMEM),
        _const_spec(wout.shape), _const_spec(gpost.shape),
    ]
    out_specs = (
        pl.BlockSpec((1, tile, D), lambda b, i: (b, i, 0)),
        pl.BlockSpec((1, SSD_WIDTH, SSD_STATE), lambda b, i: (b, 0, 0)),
        pl.BlockSpec((1, SUBLANES, SSD_CONV_DIM), lambda b, i: (b, 0, 0)),
        pl.BlockSpec((1, WINDOW, ATT_KV_WIDTH), lambda b, i: (b, 0, 0)),
        pl.BlockSpec((1, WINDOW, ATT_KV_WIDTH), lambda b, i: (b, 0, 0)),
    )
    scratch = [
        pltpu.VMEM((SSD_GROUPS, SSD_STATE, SSD_WIDTH // SSD_GROUPS), f32),
        pltpu.VMEM((tile + 2 * SUBLANES, SSD_CONV_DIM), f32),
        pltpu.VMEM((tile, SSD_CONV_DIM), f32),
        pltpu.VMEM((tile, SSD_WIDTH), f32),
        pltpu.VMEM((tile, ATT_WIDTH), f32),
        pltpu.VMEM((tile, ATT_KV_WIDTH), f32),
        pltpu.VMEM((tile, ATT_KV_WIDTH), f32),
        pltpu.VMEM((tile, LANES), f32),
        pltpu.VMEM((WINDOW, ATT_KV_WIDTH), f32),
        pltpu.VMEM((WINDOW, ATT_KV_WIDTH), f32),
        pltpu.VMEM((tile, 2 * SSD_WIDTH), bf16),
    ]
    return pl.pallas_call(
        kern, grid=grid, in_specs=in_specs, out_specs=out_specs, out_shape=out_shape,
        scratch_shapes=scratch, name="prompt_mixer",
        compiler_params=pltpu.CompilerParams(
            dimension_semantics=("arbitrary", "arbitrary"), vmem_limit_bytes=VMEM_LIMIT),
    )(x, gpre, win, convw, convb, dtb, alog, dskip_e, gssd, sinks, wout, gpost)


def _memkv_kernel(m_ref, g_ref, wk_ref, wv_ref, k_ref, v_ref):
    mn = _rmsnorm(m_ref[...], g_ref[...]).astype(bf16)
    k_ref[...] = jnp.dot(mn, wk_ref[...], preferred_element_type=f32)
    v_ref[...] = jnp.dot(mn, wv_ref[...], preferred_element_type=f32)


def _memkv(mem2d, g, wk, wv, tile):
    n, d = mem2d.shape
    row = pl.BlockSpec((tile, d), lambda i: (i, 0))
    return pl.pallas_call(
        _memkv_kernel, grid=(n // tile,),
        in_specs=[row, _const_spec(g.shape), _const_spec(wk.shape), _const_spec(wv.shape)],
        out_specs=(row, row),
        out_shape=(jax.ShapeDtypeStruct((n, d), f32), jax.ShapeDtypeStruct((n, d), f32)),
        name="memory_kv",
        compiler_params=pltpu.CompilerParams(
            dimension_semantics=("arbitrary",), vmem_limit_bytes=VMEM_LIMIT),
    )(mem2d, g, wk, wv)


def _prompt_xattn_kernel(x_ref, gpre_ref, wq_ref, mk_ref, mv_ref, wo_ref, gpost_ref, y_ref):
    x = x_ref[0]
    hn = _rmsnorm(x, gpre_ref[...]).astype(bf16)
    q = jnp.dot(hn, wq_ref[...], preferred_element_type=f32)
    outs = []
    for hd in range(X_HEADS):
        sl = slice(hd * X_HEAD_DIM, (hd + 1) * X_HEAD_DIM)
        s = _dot_nt(q[:, sl], mk_ref[0, :, sl]) * X_SCALE
        m = jnp.max(s, axis=-1, keepdims=True)
        e = jnp.exp(s - m)
        p = e / jnp.sum(e, axis=-1, keepdims=True)
        outs.append(_dot(p, mv_ref[0, :, sl]))
    o = jnp.concatenate(outs, axis=-1)
    c = _dot(o, wo_ref[...])
    y_ref[0] = x + _rmsnorm(c, gpost_ref[...])


def _prompt_xattn(x, gpre, wq, mk, mv, wo, gpost, tile):
    B, L, D = x.shape
    xs = pl.BlockSpec((1, tile, D), lambda b, i: (b, i, 0))
    ms = pl.BlockSpec((1, N_MEM, D), lambda b, i: (b, 0, 0))
    return pl.pallas_call(
        _prompt_xattn_kernel, grid=(B, L // tile),
        in_specs=[xs, _const_spec(gpre.shape), _const_spec(wq.shape), ms, ms,
                  _const_spec(wo.shape), _const_spec(gpost.shape)],
        out_specs=xs, out_shape=jax.ShapeDtypeStruct((B, L, D), f32),
        name="prompt_xattn",
        compiler_params=pltpu.CompilerParams(
            dimension_semantics=("arbitrary", "arbitrary"), vmem_limit_bytes=VMEM_LIMIT),
    )(x, gpre, wq, mk, mv, wo, gpost)


def _ffn_kernel(x_ref, gpre_ref, wg_ref, wu_ref, wd_ref, gpost_ref, y_ref):
    x = x_ref[...]
    hf = _rmsnorm(x, gpre_ref[...]).astype(bf16)
    gate = jnp.dot(hf, wg_ref[...], preferred_element_type=f32)
    up = jnp.dot(hf, wu_ref[...], preferred_element_type=f32)
    act = (_silu(gate) * up).astype(bf16)
    f = jnp.dot(act, wd_ref[...], preferred_element_type=f32)
    y_ref[...] = x + _rmsnorm(f, gpost_ref[...])


def _ffn(x2d, gpre, wg, wu, wd, gpost, tile):
    n, d = x2d.shape
    row = pl.BlockSpec((tile, d), lambda i: (i, 0))
    return pl.pallas_call(
        _ffn_kernel, grid=(n // tile,),
        in_specs=[row, _const_spec(gpre.shape), _const_spec(wg.shape), _const_spec(wu.shape),
                  _const_spec(wd.shape), _const_spec(gpost.shape)],
        out_specs=row, out_shape=jax.ShapeDtypeStruct((n, d), f32),
        name="ffn",
        compiler_params=pltpu.CompilerParams(
            dimension_semantics=("arbitrary",), vmem_limit_bytes=VMEM_LIMIT),
    )(x2d, gpre, wg, wu, wd, gpost)


def _pad_rows(a, rows):
    if a.shape[0] == rows:
        return a
    return jnp.concatenate([a, jnp.zeros((rows - a.shape[0], a.shape[1]), a.dtype)], axis=0)


def _sample_mixer_kernel(x_ref, cprev_ref, st_ref, ck_ref, cv_ref,
                         gpre_ref, win_ref, convw_ref, convb_ref, dtb_ref, alog_ref,
                         dskip_ref, gssd_ref, sinkcol_ref, wout_ref, gpost_ref,
                         y_ref, ssm_ref, cnew_ref, knew_ref, vnew_ref, *, bt, steps):
    R = steps * bt
    half = SSD_WIDTH // SSD_GROUPS
    x = x_ref[...].reshape(R, D_MODEL)
    h = _rmsnorm(x, gpre_ref[...]).astype(bf16)
    z = jnp.dot(h, win_ref[:, S_Z:S_XBC], preferred_element_type=f32)
    u = jnp.dot(h, win_ref[:, S_XBC:S_Q], preferred_element_type=f32)
    qpad = jnp.dot(h, win_ref[:, S_Q:S_K], preferred_element_type=f32)
    k_new = jnp.dot(h, win_ref[:, S_K:S_V], preferred_element_type=f32)
    v_new = jnp.dot(h, win_ref[:, S_V:S_DT], preferred_element_type=f32)
    dt_raw = jnp.dot(h, win_ref[:, S_DT:S_END], preferred_element_type=f32)
    lane = lax.broadcasted_iota(jnp.int32, (1, LANES), 1)
    dt = jnp.where(lane < SSD_HEADS, _softplus(dt_raw + dtb_ref[...]), 0.0)

    def slab(a, t):
        return a[t * bt:(t + 1) * bt]

    hist = [cprev_ref[j] for j in range(SSD_CONV - 1)] + [slab(u, t) for t in range(steps)]
    xbc_t = []
    for t in range(steps):
        acc = convb_ref[...]
        for j in range(SSD_CONV):
            acc = acc + hist[t + j] * convw_ref[j:j + 1, :]
        xbc_t.append(_silu(acc))
    for j in range(SSD_CONV - 1):
        cnew_ref[j] = hist[steps + j]
    xbc = jnp.concatenate(xbc_t, axis=0)
    xs = xbc[:, 0:SSD_WIDTH]
    bm = xbc[:, SSD_WIDTH:SSD_WIDTH + half]
    cm = xbc[:, SSD_WIDTH + half:SSD_CONV_DIM]

    a_row = -jnp.exp(alog_ref[...])
    adt = dt * a_row
    cs_t = [slab(adt, 0)]
    for t in range(1, steps):
        cs_t.append(cs_t[-1] + slab(adt, t))
    cs = jnp.concatenate(cs_t, axis=0)
    tot = cs_t[-1]
    tot_rows = jnp.concatenate([tot] * steps, axis=0)
    expand = _head_expand_matrix()
    xdt = xs * _dot_x2(dt, expand)
    ecs_e = _dot_x2(jnp.exp(cs), expand)
    w_end = xdt * _dot_x2(jnp.exp(tot_rows - cs), expand)
    dec_e = _dot_x2(jnp.exp(tot), expand)

    gr = lax.broadcasted_iota(jnp.int32, (half, SSD_WIDTH), 0)
    gc = lax.broadcasted_iota(jnp.int32, (half, SSD_WIDTH), 1)
    gsum = ((gr >> 7) == (gc >> 8)).astype(bf16)

    y_t = []
    for t in range(steps):
        acc = None
        for s in range(t + 1):
            coef = _dot_x2(slab(cm, t) * slab(bm, s), gsum)
            if s < t:
                coef = coef * _dot_x2(jnp.exp(cs_t[t] - cs_t[s]), expand)
            term = coef * slab(xdt, s)
            acc = term if acc is None else acc + term
        y_t.append(acc)
    y_intra = jnp.concatenate(y_t, axis=0)

    b_idx = lax.broadcasted_iota(jnp.int32, (bt, 1, LANES), 0)
    l_idx = lax.broadcasted_iota(jnp.int32, (bt, 1, LANES), 2)
    pair = ((l_idx & (bt - 1)) == b_idx) & (l_idx < R)
    own = (l_idx == b_idx)
    y_off_parts = []
    for g in range(SSD_GROUPS):
        gl = slice(g * half, (g + 1) * half)
        h0 = st_ref[:, gl, :]
        cg = _pad_rows(cm[:, g * SSD_STATE:(g + 1) * SSD_STATE], LANES)
        zz = _dot_nt(h0.reshape(bt * half, SSD_STATE), cg).reshape(bt, half, LANES)
        yt = jnp.sum(jnp.where(pair, zz, 0.0), axis=0)
        y_off_parts.append(yt.T[0:R, :])
        wt = _pad_rows(w_end[:, gl], LANES).T
        lhs = jnp.where(pair, wt[None], 0.0).reshape(bt * half, LANES)
        bg = _pad_rows(bm[:, g * SSD_STATE:(g + 1) * SSD_STATE], LANES)
        contrib = _dot(lhs, bg).reshape(bt, half, SSD_STATE)
        dec_t = _pad_rows(dec_e[:, gl], LANES).T
        dec = jnp.sum(jnp.where(own, dec_t[None], 0.0), axis=-1, keepdims=True)
        ssm_ref[:, gl, :] = h0 * dec + contrib
    y_off = jnp.concatenate(y_off_parts, axis=-1) * ecs_e
    y = y_intra + y_off + xs * dskip_ref[...]
    y_ssd = _gated_group_norm(y, z, gssd_ref[...])

    qr = jnp.concatenate([qpad[:, hd * LANES:(hd + 1) * LANES] for hd in range(ATT_HEADS)], axis=0)
    HR = ATT_HEADS * R
    qr_b = qr.astype(bf16)
    ridx = lax.broadcasted_iota(jnp.int32, (HR, 1), 0)
    rb = ridx & (bt - 1)
    rt = (ridx // bt) & (steps - 1)
    s_c = jnp.zeros((HR, WINDOW), f32)
    for b in range(bt):
        s_c = jnp.where(rb == b, _dot_nt(qr_b, ck_ref[b]), s_c)
    s_c = s_c * ATT_SCALE
    reps = HR // bt
    s_n = []
    for t2 in range(steps):
        kt = jnp.concatenate([slab(k_new, t2)] * reps, axis=0)
        s_n.append(jnp.sum(qr * kt, axis=-1, keepdims=True) * ATT_SCALE)
    jcol = lax.broadcasted_iota(jnp.int32, (HR, WINDOW), 1)
    mask_c = jcol > rt
    sink = sinkcol_ref[...]
    s_c = jnp.where(mask_c, s_c, -jnp.inf)
    m = jnp.maximum(jnp.max(s_c, axis=-1, keepdims=True), sink)
    for t2 in range(steps):
        s_n[t2] = jnp.where(rt >= t2, s_n[t2], -jnp.inf)
        m = jnp.maximum(m, s_n[t2])
    e_c = jnp.exp(s_c - m)
    e_n = [jnp.exp(s - m) for s in s_n]
    den = jnp.sum(e_c, axis=-1, keepdims=True) + jnp.exp(sink - m)
    for e in e_n:
        den = den + e
    p_c = (e_c / den).astype(bf16)
    o = jnp.zeros((HR, LANES), f32)
    for b in range(bt):
        o = jnp.where(rb == b, _dot(p_c, cv_ref[b]), o)
    for t2 in range(steps):
        vt = jnp.concatenate([slab(v_new, t2)] * reps, axis=0)
        o = o + (e_n[t2] / den) * vt
    y_att = jnp.concatenate([o[hd * R:(hd + 1) * R] for hd in range(ATT_HEADS)], axis=-1)

    mix = (jnp.dot(y_ssd.astype(bf16), wout_ref[0:SSD_WIDTH, :], preferred_element_type=f32)
           + jnp.dot(y_att.astype(bf16), wout_ref[SSD_WIDTH:, :], preferred_element_type=f32))
    y_ref[...] = (x + _rmsnorm(mix, gpost_ref[...])).reshape(steps, bt, D_MODEL)
    knew_ref[...] = k_new.reshape(steps, bt, ATT_KV_WIDTH)
    vnew_ref[...] = v_new.reshape(steps, bt, ATT_KV_WIDTH)


def _sample_mixer(x_tm, cprev_tm, st, ck, cv, gpre, win, convw, convb, dtb, alog, dskip_e, gssd,
                  sinkcol, wout, gpost, bt):
    steps, nb, D = x_tm.shape
    kern = functools.partial(_sample_mixer_kernel, bt=bt, steps=steps)
    tm = lambda w: pl.BlockSpec((steps, bt, w), lambda i: (0, i, 0))
    in_specs = [
        tm(D),
        pl.BlockSpec((SSD_CONV - 1, bt, SSD_CONV_DIM), lambda i: (0, i, 0)),
        pl.BlockSpec((bt, SSD_WIDTH, SSD_STATE), lambda i: (i, 0, 0)),
        pl.BlockSpec((bt, WINDOW, ATT_KV_WIDTH), lambda i: (i, 0, 0)),
        pl.BlockSpec((bt, WINDOW, ATT_KV_WIDTH), lambda i: (i, 0, 0)),
    ] + [_const_spec(a.shape) for a in (gpre, win, convw, convb, dtb, alog, dskip_e, gssd,
                                        sinkcol, wout, gpost)]
    out_specs = (
        tm(D),
        pl.BlockSpec((bt, SSD_WIDTH, SSD_STATE), lambda i: (i, 0, 0)),
        pl.BlockSpec((SSD_CONV - 1, bt, SSD_CONV_DIM), lambda i: (0, i, 0)),
        tm(ATT_KV_WIDTH), tm(ATT_KV_WIDTH),
    )
    out_shape = (
        jax.ShapeDtypeStruct((steps, nb, D), f32),
        jax.ShapeDtypeStruct((nb, SSD_WIDTH, SSD_STATE), f32),
        jax.ShapeDtypeStruct((SSD_CONV - 1, nb, SSD_CONV_DIM), f32),
        jax.ShapeDtypeStruct((steps, nb, ATT_KV_WIDTH), f32),
        jax.ShapeDtypeStruct((steps, nb, ATT_KV_WIDTH), f32),
    )
    return pl.pallas_call(
        kern, grid=(nb // bt,), in_specs=in_specs, out_specs=out_specs, out_shape=out_shape,
        name="sample_mixer",
        compiler_params=pltpu.CompilerParams(
            dimension_semantics=("arbitrary",), vmem_limit_bytes=VMEM_LIMIT),
    )(x_tm, cprev_tm, st, ck, cv, gpre, win, convw, convb, dtb, alog, dskip_e, gssd, sinkcol,
      wout, gpost)


def _sample_xattn_kernel(x_ref, mk_ref, mv_ref, gpre_ref, wq_ref, wo_ref, gpost_ref, y_ref,
                         *, bt, steps):
    R = steps * bt
    nrow = bt * N_MEM * X_HEADS
    x = x_ref[...].reshape(R, D_MODEL)
    hn = _rmsnorm(x, gpre_ref[...]).astype(bf16)
    q = jnp.dot(hn, wq_ref[...], preferred_element_type=f32)
    qs = jnp.concatenate([q[:, hd * X_HEAD_DIM:(hd + 1) * X_HEAD_DIM] for hd in range(X_HEADS)],
                         axis=0)
    kall = mk_ref[...].reshape(nrow, X_HEAD_DIM)
    vall = mv_ref[...].reshape(nrow, X_HEAD_DIM)
    z = _dot_nt(kall, qs) * X_SCALE
    ncol = X_HEADS * R
    z = z.reshape(bt, N_MEM * X_HEADS // SUBLANES, SUBLANES, ncol)
    mshape = (bt, 1, SUBLANES, ncol)
    b_i = lax.broadcasted_iota(jnp.int32, mshape, 0)
    r_i = lax.broadcasted_iota(jnp.int32, mshape, 2)
    c_i = lax.broadcasted_iota(jnp.int32, mshape, 3)
    keep = (b_i == (c_i % bt)) & ((r_i % X_HEADS) == (c_i // R))
    z = jnp.where(keep, z, -jnp.inf).reshape(nrow, ncol)
    m = jnp.max(z, axis=0, keepdims=True)
    e = jnp.exp(z - m)
    p = e * (1.0 / jnp.sum(e, axis=0, keepdims=True))
    o = lax.dot_general(p.astype(bf16), vall.astype(bf16), (((0,), (0,)), ((), ())),
                        preferred_element_type=f32)
    o = jnp.concatenate([o[hd * R:(hd + 1) * R] for hd in range(X_HEADS)], axis=-1)
    cc = _dot(o, wo_ref[...])
    y_ref[...] = (x + _rmsnorm(cc, gpost_ref[...])).reshape(steps, bt, D_MODEL)


def _sample_xattn(x_tm, mk, mv, gpre, wq, wo, gpost, bt):
    steps, nb, D = x_tm.shape
    kern = functools.partial(_sample_xattn_kernel, bt=bt, steps=steps)
    xs = pl.BlockSpec((steps, bt, D), lambda i: (0, i, 0))
    ms = pl.BlockSpec((bt, N_MEM, X_HEADS, X_HEAD_DIM), lambda i: (i, 0, 0, 0))
    return pl.pallas_call(
        kern, grid=(nb // bt,),
        in_specs=[xs, ms, ms, _const_spec(gpre.shape), _const_spec(wq.shape),
                  _const_spec(wo.shape), _const_spec(gpost.shape)],
        out_specs=xs, out_shape=jax.ShapeDtypeStruct((steps, nb, D), f32),
        name="sample_xattn",
        compiler_params=pltpu.CompilerParams(
            dimension_semantics=("arbitrary",), vmem_limit_bytes=VMEM_LIMIT),
    )(x_tm, mk, mv, gpre, wq, wo, gpost)


def _row(v, width=None):
    v = v.reshape(1, -1).astype(f32)
    if width is not None and v.shape[1] < width:
        v = jnp.pad(v, ((0, 0), (0, width - v.shape[1])))
    return v


def kernel(x_prompt, x_sample, state_ssm, state_conv, cache_win_k, cache_win_v, cache_mem_k, cache_mem_v, mem_prompt, g_mix_pre, w_in, conv_w, conv_b, dt_bias, a_log, d_skip, g_ssd_norm, sinks, w_out, g_mix_post, g_x_pre, w_xq, g_mem, w_xk, w_xv, w_xo, g_x_post, g_ffn_pre, w_gate, w_up, w_down, g_ffn_post):
    depth = w_in.shape[0]
    assert depth == 1
    B, L, D = x_prompt.shape
    NB, steps, _ = x_sample.shape
    li = 0

    w = w_in[li]
    wz, wxbc, wdt = w[:, 0:512], w[:, 512:1536], w[:, 1536:1544]
    wq, wk, wv = w[:, 1544:2056], w[:, 2056:2184], w[:, 2184:2312]
    wdt_p = jnp.pad(wdt, ((0, 0), (0, LANES - SSD_HEADS)))
    win_p = jnp.concatenate([wz, wxbc, wq, wk, wv, wdt_p], axis=1).astype(bf16)
    zero_half = jnp.zeros((D, ATT_HEAD_DIM), f32)
    q_blocks = []
    for hd in range(ATT_HEADS):
        wq_h = wq[:, hd * ATT_HEAD_DIM:(hd + 1) * ATT_HEAD_DIM]
        q_blocks += [wq_h, zero_half] if hd < ATT_HEADS // ATT_KV_HEADS else [zero_half, wq_h]
    win_s = jnp.concatenate([wz, wxbc] + q_blocks + [wk, wv, wdt_p], axis=1).astype(bf16)
    wo = w_out[li]
    wo_b = wo.astype(bf16)
    zero_rows = jnp.zeros((ATT_HEAD_DIM, D), f32)
    o_blocks = [wo[0:SSD_WIDTH]]
    for hd in range(ATT_HEADS):
        wo_h = wo[SSD_WIDTH + hd * ATT_HEAD_DIM:SSD_WIDTH + (hd + 1) * ATT_HEAD_DIM]
        o_blocks += [wo_h, zero_rows] if hd < ATT_HEADS // ATT_KV_HEADS else [zero_rows, wo_h]
    wo_s = jnp.concatenate(o_blocks, axis=0).astype(bf16)

    gpre, gpost = _row(g_mix_pre[li]), _row(g_mix_post[li])
    convw, convb = conv_w[li].astype(f32), _row(conv_b[li])
    dtb, alog = _row(dt_bias[li], LANES), _row(a_log[li], LANES)
    dskip_e = _row(jnp.repeat(d_skip[li], SSD_HEAD_DIM))
    gssd = _row(g_ssd_norm[li])
    sk = sinks[li].astype(f32)

    mk2d, mv2d = _memkv(mem_prompt.reshape(B * N_MEM, D), _row(g_mem[li]),
                        w_xk[li].astype(bf16), w_xv[li].astype(bf16), tile=512)
    mk3, mv3 = mk2d.reshape(B, N_MEM, D), mv2d.reshape(B, N_MEM, D)
    x1, p_ssm, p_conv8, p_wk, p_wv = _prompt_mixer(
        x_prompt, gpre, win_p, convw, convb, dtb, alog, dskip_e, gssd, sk, wo_b, gpost, tile=512)
    wxq_b, wxo_b = w_xq[li].astype(bf16), w_xo[li].astype(bf16)
    gxpre, gxpost = _row(g_x_pre[li]), _row(g_x_post[li])
    x2 = _prompt_xattn(x1, gxpre, wxq_b, mk3, mv3, wxo_b, gxpost, tile=512)
    wg_b, wu_b, wd_b = w_gate[li].astype(bf16), w_up[li].astype(bf16), w_down[li].astype(bf16)
    gfpre, gfpost = _row(g_ffn_pre[li]), _row(g_ffn_post[li])
    yp = _ffn(x2.reshape(B * L, D), gfpre, wg_b, wu_b, wd_b, gfpost, tile=512).reshape(B, L, D)

    bt = 8
    x_tm = jnp.transpose(x_sample, (1, 0, 2))
    cprev_tm = jnp.transpose(state_conv[li], (1, 0, 2))
    st = state_ssm[li].reshape(NB, SSD_WIDTH, SSD_STATE)
    ck = cache_win_k[li].reshape(NB, WINDOW, ATT_KV_WIDTH)
    cv = cache_win_v[li].reshape(NB, WINDOW, ATT_KV_WIDTH)
    sinkcol = jnp.repeat(sk, steps * bt).reshape(ATT_HEADS * steps * bt, 1)
    x1s, s_ssm, cnew_tm, knew_tm, vnew_tm = _sample_mixer(
        x_tm, cprev_tm, st, ck, cv, gpre, win_s, convw, convb, dtb, alog, dskip_e, gssd,
        sinkcol, wo_s, gpost, bt=bt)
    cmk = cache_mem_k.reshape(NB, N_MEM, X_HEADS, X_HEAD_DIM)
    cmv = cache_mem_v.reshape(NB, N_MEM, X_HEADS, X_HEAD_DIM)
    x2s = _sample_xattn(x1s, cmk, cmv, gxpre, wxq_b, wxo_b, gxpost, bt=bt)
    ys_tm = _ffn(x2s.reshape(steps * NB, D), gfpre, wg_b, wu_b, wd_b, gfpost, tile=steps * NB)
    ys = jnp.transpose(ys_tm.reshape(steps, NB, D), (1, 0, 2))

    s_conv = jnp.transpose(cnew_tm, (1, 0, 2))
    knew = jnp.transpose(knew_tm, (1, 0, 2))
    vnew = jnp.transpose(vnew_tm, (1, 0, 2))
    s_wk = jnp.concatenate([ck[:, steps:], knew], axis=1)
    s_wv = jnp.concatenate([cv[:, steps:], vnew], axis=1)
    kv_shape = (ATT_KV_HEADS, ATT_HEAD_DIM)
    return (
        yp, ys,
        p_ssm.reshape(1, B, SSD_HEADS, SSD_HEAD_DIM, SSD_STATE),
        p_conv8[:, SUBLANES - (SSD_CONV - 1):, :][None],
        p_wk.reshape(1, B, WINDOW, *kv_shape), p_wv.reshape(1, B, WINDOW, *kv_shape),
        mk3.reshape(1, B, N_MEM, X_HEADS, X_HEAD_DIM), mv3.reshape(1, B, N_MEM, X_HEADS, X_HEAD_DIM),
        s_ssm.reshape(1, NB, SSD_HEADS, SSD_HEAD_DIM, SSD_STATE),
        s_conv[None],
        s_wk.reshape(1, NB, WINDOW, *kv_shape), s_wv.reshape(1, NB, WINDOW, *kv_shape),
    )
```

```python
import functools

import jax
import jax.numpy as jnp
from jax import lax
from jax.experimental import pallas as pl
from jax.experimental.pallas import tpu as pltpu

f32 = jnp.float32
bf16 = jnp.bfloat16

D_MODEL = 1024
EPS = 1e-6
N_MEM = 256
SSD_HEADS = 8
SSD_HEAD_DIM = 64
SSD_WIDTH = 512
SSD_GROUPS = 2
SSD_STATE = 128
SSD_CONV = 4
SSD_CHUNK = 128
SSD_CONV_DIM = 1024
ATT_HEADS = 8
ATT_KV_HEADS = 2
ATT_HEAD_DIM = 64
ATT_WIDTH = 512
ATT_KV_WIDTH = 128
WINDOW = 128
ATT_SCALE = ATT_HEAD_DIM ** -0.5
X_HEADS = 4
X_HEAD_DIM = 256
X_SCALE = X_HEAD_DIM ** -0.5
D_FF = 2816
LANES = 128
SUBLANES = 8
VMEM_LIMIT = 56 * 1024 * 1024

P_Z, P_XBC, P_Q, P_K, P_V, P_DT, P_END = 0, 512, 1536, 2048, 2176, 2304, 2432
S_Z, S_XBC, S_Q, S_K, S_V, S_DT, S_END = 0, 512, 1536, 2560, 2688, 2816, 2944


def _dot(a, b):
    return jnp.dot(a.astype(bf16), b.astype(bf16), preferred_element_type=f32)


def _dot_nt(a, b):
    return lax.dot_general(a.astype(bf16), b.astype(bf16), (((1,), (1,)), ((), ())),
                           preferred_element_type=f32)


def _split2(x):
    hi = x.astype(bf16)
    lo = (x - hi.astype(f32)).astype(bf16)
    return hi, lo


def _dot_x2(x, m):
    hi, lo = _split2(x)
    return (jnp.dot(hi, m, preferred_element_type=f32)
            + jnp.dot(lo, m, preferred_element_type=f32))


def _dot_x3_left(m, x):
    hi = x.astype(bf16)
    r1 = x - hi.astype(f32)
    mid = r1.astype(bf16)
    lo = (r1 - mid.astype(f32)).astype(bf16)
    return (jnp.dot(m, hi, preferred_element_type=f32)
            + jnp.dot(m, mid, preferred_element_type=f32)
            + jnp.dot(m, lo, preferred_element_type=f32))


def _rmsnorm(x, g):
    ms = jnp.mean(x * x, axis=-1, keepdims=True)
    return x * lax.rsqrt(ms + EPS) * g


def _silu(x):
    return x * jax.nn.sigmoid(x)


def _softplus(x):
    return jnp.maximum(x, 0.0) + jnp.log1p(jnp.exp(-jnp.abs(x)))


def _head_expand_matrix():
    r = lax.broadcasted_iota(jnp.int32, (LANES, SSD_WIDTH), 0)
    c = lax.broadcasted_iota(jnp.int32, (LANES, SSD_WIDTH), 1)
    return (r == (c >> 6)).astype(bf16)


def _gated_group_norm(y, z, g):
    u = y * _silu(z)
    half = SSD_WIDTH // SSD_GROUPS
    parts = []
    for gi in range(SSD_GROUPS):
        ug = u[:, gi * half:(gi + 1) * half]
        parts.append(ug * lax.rsqrt(jnp.mean(ug * ug, axis=-1, keepdims=True) + EPS))
    return jnp.concatenate(parts, axis=-1) * g


def _sink_softmax(s, mask, sink):
    s = jnp.where(mask, s, -jnp.inf)
    m = jnp.maximum(jnp.max(s, axis=-1, keepdims=True), sink)
    e = jnp.exp(s - m)
    return e / (jnp.sum(e, axis=-1, keepdims=True) + jnp.exp(sink - m))


def _prompt_mixer_kernel(x_ref, gpre_ref, win_ref, convw_ref, convb_ref, dtb_ref, alog_ref,
                         dskip_ref, gssd_ref, sinks_ref, wout_ref, gpost_ref,
                         y_ref, ssm_ref, conv_ref, wk_ref, wv_ref,
                         statet_sc, xbc_ext_sc, xbc_sc, z_sc, q_sc, k_sc, v_sc, dt_sc,
                         kprev_sc, vprev_sc, mix_sc, *, tile):
    i = pl.program_id(1)
    nchunk = tile // SSD_CHUNK
    C = SSD_CHUNK

    @pl.when(i == 0)
    def _init():
        statet_sc[...] = jnp.zeros_like(statet_sc)
        xbc_ext_sc[0:SUBLANES, :] = jnp.zeros((SUBLANES, SSD_CONV_DIM), f32)
        kprev_sc[...] = jnp.zeros_like(kprev_sc)
        vprev_sc[...] = jnp.zeros_like(vprev_sc)

    x = x_ref[0]
    h = _rmsnorm(x, gpre_ref[...]).astype(bf16)
    z_sc[...] = jnp.dot(h, win_ref[:, P_Z:P_XBC], preferred_element_type=f32)
    xbc_ext_sc[SUBLANES:SUBLANES + tile, :] = jnp.dot(h, win_ref[:, P_XBC:P_Q],
                                                      preferred_element_type=f32)
    q_sc[...] = jnp.dot(h, win_ref[:, P_Q:P_K], preferred_element_type=f32)
    k_sc[...] = jnp.dot(h, win_ref[:, P_K:P_V], preferred_element_type=f32)
    v_sc[...] = jnp.dot(h, win_ref[:, P_V:P_DT], preferred_element_type=f32)
    dt_raw = jnp.dot(h, win_ref[:, P_DT:P_END], preferred_element_type=f32)
    lane = lax.broadcasted_iota(jnp.int32, (1, LANES), 1)
    dt_sc[...] = jnp.where(lane < SSD_HEADS, _softplus(dt_raw + dtb_ref[...]), 0.0)

    acc = convb_ref[...]
    for j in range(SSD_CONV):
        off = SUBLANES - (SSD_CONV - 1) + j
        acc = acc + xbc_ext_sc[off:off + tile, :] * convw_ref[j:j + 1, :]
    xbc_sc[...] = _silu(acc)
    tail = xbc_ext_sc[tile:tile + SUBLANES, :]
    conv_ref[0] = tail
    xbc_ext_sc[0:SUBLANES, :] = tail

    a_row = -jnp.exp(alog_ref[...])
    expand = _head_expand_matrix()
    expand2 = jnp.concatenate([expand, expand], axis=0)
    row_i = lax.broadcasted_iota(jnp.int32, (C, C), 0)
    col_i = lax.broadcasted_iota(jnp.int32, (C, C), 1)
    lower = col_i <= row_i
    tri = lower.astype(bf16)
    tri3 = jnp.concatenate([tri, tri, tri], axis=1)
    lo_half = lane < ATT_HEAD_DIM
    half = SSD_WIDTH // SSD_GROUPS
    head_of_lane = lax.broadcasted_iota(jnp.int32, (1, half), 1) >> 6

    def chunk(c, carry):
        r0 = pl.multiple_of(c * C, C)
        rows = pl.ds(r0, C)
        xs = xbc_sc[rows, 0:SSD_WIDTH]
        bm = xbc_sc[rows, SSD_WIDTH:SSD_WIDTH + half]
        cm = xbc_sc[rows, SSD_WIDTH + half:SSD_CONV_DIM]
        dtc = dt_sc[rows, :]
        adt = dtc * a_row
        a_hi = adt.astype(bf16)
        a_r1 = adt - a_hi.astype(f32)
        a_mid = a_r1.astype(bf16)
        a_lo = (a_r1 - a_mid.astype(f32)).astype(bf16)
        cs = jnp.dot(tri3, jnp.concatenate([a_hi, a_mid, a_lo], axis=0), preferred_element_type=f32)
        cs_t = cs.T
        tot = cs[C - 1:C, :]
        fac = jnp.concatenate([dtc, jnp.exp(cs), jnp.exp(tot - cs)], axis=0)
        f_hi = fac.astype(bf16)
        f_lo = (fac - f_hi.astype(f32)).astype(bf16)
        fac_e = jnp.dot(jnp.concatenate([f_hi, f_lo], axis=1), expand2,
                        preferred_element_type=f32)
        xdt = xs * fac_e[0:C]
        ecs_e = fac_e[C:2 * C]
        w_end = xdt * fac_e[2 * C:3 * C]
        cb_all = _dot_nt(jnp.concatenate([cm[:, 0:SSD_STATE], cm[:, SSD_STATE:half]], axis=0),
                         jnp.concatenate([bm[:, 0:SSD_STATE], bm[:, SSD_STATE:half]], axis=0))
        bm_t = bm.T
        xdt_b = xdt.astype(bf16)
        y_parts = []
        for g in range(SSD_GROUPS):
            gl = slice(g * half, (g + 1) * half)
            cg = cm[:, g * SSD_STATE:(g + 1) * SSD_STATE]
            cb = cb_all[g * C:(g + 1) * C, g * C:(g + 1) * C]
            st = statet_sc[g]
            y_off = _dot(cg, st) * ecs_e[:, gl]
            m_parts, x_parts = [], []
            for r in range(SSD_HEADS // SSD_GROUPS):
                hh = g * (SSD_HEADS // SSD_GROUPS) + r
                diff = cs[:, hh:hh + 1] - cs_t[hh:hh + 1, :]
                m_parts.append((cb * jnp.exp(jnp.where(lower, diff, -jnp.inf))).astype(bf16))
                x_parts.append(jnp.where(head_of_lane == r, xdt_b[:, gl], jnp.zeros((), bf16)))
            y_d = jnp.dot(jnp.concatenate(m_parts, axis=1), jnp.concatenate(x_parts, axis=0),
                          preferred_element_type=f32)
            y_parts.append(y_d + y_off)
            statet_sc[g] = (st * ecs_e[C - 1:C, gl]
                            + _dot(bm_t[g * SSD_STATE:(g + 1) * SSD_STATE, :], w_end[:, gl]))
        y = jnp.concatenate(y_parts, axis=-1) + xs * dskip_ref[...]
        y_ssd = _gated_group_norm(y, z_sc[rows, :], gssd_ref[...])
        mix_sc[rows, 0:SSD_WIDTH] = y_ssd.astype(bf16)

        q = q_sc[rows, :].astype(bf16)
        k = k_sc[rows, :]
        v = v_sc[rows, :]
        kk = jnp.concatenate([kprev_sc[...], k], axis=0)
        vv = jnp.concatenate([vprev_sc[...], v], axis=0)
        kk_r = pltpu.roll(kk, ATT_HEAD_DIM, 1)
        vv_r = pltpu.roll(vv, ATT_HEAD_DIM, 1)
        k_lo = [jnp.where(lo_half, kk, 0.0).astype(bf16), jnp.where(lo_half, kk_r, 0.0).astype(bf16)]
        k_hi = [jnp.where(lo_half, 0.0, kk_r).astype(bf16), jnp.where(lo_half, 0.0, kk).astype(bf16)]
        v_lo = [jnp.where(lo_half, vv, 0.0).astype(bf16), jnp.where(lo_half, vv_r, 0.0).astype(bf16)]
        v_hi = [jnp.where(lo_half, 0.0, vv_r).astype(bf16), jnp.where(lo_half, 0.0, vv).astype(bf16)]
        prev_bias = jnp.where((i * nchunk + c) > 0, 0.0, -jnp.inf)
        for g in range(ATT_KV_HEADS):
            qg = jnp.concatenate([q[:, (2 * g) * LANES:(2 * g + 1) * LANES],
                                  q[:, (2 * g + 1) * LANES:(2 * g + 2) * LANES]], axis=0)
            s = _dot_nt(qg, jnp.concatenate([k_lo[g], k_hi[g]], axis=0))
            p_rows = []
            for jb in range(2):
                p_cols = []
                for sub in range(2):
                    sink = sinks_ref[(2 * g + jb) * 2 + sub]
                    s_prev = s[jb * C:(jb + 1) * C, (2 * sub) * C:(2 * sub + 1) * C]
                    s_cur = s[jb * C:(jb + 1) * C, (2 * sub + 1) * C:(2 * sub + 2) * C]
                    sc = jnp.where(lower, s_cur, s_prev + prev_bias)
                    m = jnp.maximum(jnp.max(sc, axis=-1, keepdims=True), sink)
                    e = jnp.exp(sc - m)
                    den = jnp.sum(e, axis=-1, keepdims=True) + jnp.exp(sink - m)
                    p = (e * (1.0 / den)).astype(bf16)
                    zero = jnp.zeros((), bf16)
                    p_cols += [jnp.where(lower, zero, p), jnp.where(lower, p, zero)]
                p_rows.append(jnp.concatenate(p_cols, axis=1))
            o = jnp.dot(jnp.concatenate(p_rows, axis=0),
                        jnp.concatenate([v_lo[g], v_hi[g]], axis=0),
                        preferred_element_type=f32)
            for jb in range(2):
                lo_l = SSD_WIDTH + (2 * g + jb) * LANES
                mix_sc[rows, lo_l:lo_l + LANES] = o[jb * C:(jb + 1) * C].astype(bf16)
        kprev_sc[...] = k
        vprev_sc[...] = v
        return carry

    lax.fori_loop(0, nchunk, chunk, 0)

    wk_ref[0] = k_sc[tile - WINDOW:tile, :]
    wv_ref[0] = v_sc[tile - WINDOW:tile, :]

    @pl.when(i == pl.num_programs(1) - 1)
    def _final_state():
        for g in range(SSD_GROUPS):
            ssm_ref[0, g * half:(g + 1) * half, :] = statet_sc[g].T

    mix = jnp.dot(mix_sc[...], wout_ref[...], preferred_element_type=f32)
    y_ref[0] = x + _rmsnorm(mix, gpost_ref[...])


def _const_spec(shape):
    nd = len(shape)
    return pl.BlockSpec(shape, lambda *_: (0,) * nd)


def _prompt_mixer(x, gpre, win, convw, convb, dtb, alog, dskip_e, gssd, sinks, wout, gpost, tile):
    B, L, D = x.shape
    grid = (B, L // tile)
    kern = functools.partial(_prompt_mixer_kernel, tile=tile)
    out_shape = (
        jax.ShapeDtypeStruct((B, L, D), f32),
        jax.ShapeDtypeStruct((B, SSD_WIDTH, SSD_STATE), f32),
        jax.ShapeDtypeStruct((B, SUBLANES, SSD_CONV_DIM), f32),
        jax.ShapeDtypeStruct((B, WINDOW, ATT_KV_WIDTH), f32),
        jax.ShapeDtypeStruct((B, WINDOW, ATT_KV_WIDTH), f32),
    )
    in_specs = [
        pl.BlockSpec((1, tile, D), lambda b, i: (b, i, 0)),
        _const_spec(gpre.shape), _const_spec(win.shape), _const_spec(convw.shape),
        _const_spec(convb.shape), _const_spec(dtb.shape), _const_spec(alog.shape),
        _const_spec(dskip_e.shape), _const_spec(gssd.shape),
        pl.BlockSpec(memory_space=pltpu.SMEM),
        _const_spec(wout.shape), _const_spec(gpost.shape),
    ]
    out_specs = (
        pl.BlockSpec((1, tile, D), lambda b, i: (b, i, 0)),
        pl.BlockSpec((1, SSD_WIDTH, SSD_STATE), lambda b, i: (b, 0, 0)),
        pl.BlockSpec((1, SUBLANES, SSD_CONV_DIM), lambda b, i: (b, 0, 0)),
        pl.BlockSpec((1, WINDOW, ATT_KV_WIDTH), lambda b, i: (b, 0, 0)),
        pl.BlockSpec((1, WINDOW, ATT_KV_WIDTH), lambda b, i: (b, 0, 0)),
    )
    scratch = [
        pltpu.VMEM((SSD_GROUPS, SSD_STATE, SSD_WIDTH // SSD_GROUPS), f32),
        pltpu.VMEM((tile + 2 * SUBLANES, SSD_CONV_DIM), f32),
        pltpu.VMEM((tile, SSD_CONV_DIM), f32),
        pltpu.VMEM((tile, SSD_WIDTH), f32),
        pltpu.VMEM((tile, ATT_WIDTH), f32),
        pltpu.VMEM((tile, ATT_KV_WIDTH), f32),
        pltpu.VMEM((tile, ATT_KV_WIDTH), f32),
        pltpu.VMEM((tile, LANES), f32),
        pltpu.VMEM((WINDOW, ATT_KV_WIDTH), f32),
        pltpu.VMEM((WINDOW, ATT_KV_WIDTH), f32),
        pltpu.VMEM((tile, 2 * SSD_WIDTH), bf16),
    ]
    return pl.pallas_call(
        kern, grid=grid, in_specs=in_specs, out_specs=out_specs, out_shape=out_shape,
        scratch_shapes=scratch, name="prompt_mixer",
        compiler_params=pltpu.CompilerParams(
            dimension_semantics=("arbitrary", "arbitrary"), vmem_limit_bytes=VMEM_LIMIT),
    )(x, gpre, win, convw, convb, dtb, alog, dskip_e, gssd, sinks, wout, gpost)


def _memkv_kernel(m_ref, g_ref, wk_ref, wv_ref, k_ref, v_ref):
    mn = _rmsnorm(m_ref[...], g_ref[...]).astype(bf16)
    k_ref[...] = jnp.dot(mn, wk_ref[...], preferred_element_type=f32)
    v_ref[...] = jnp.dot(mn, wv_ref[...], preferred_element_type=f32)


def _memkv(mem2d, g, wk, wv, tile):
    n, d = mem2d.shape
    row = pl.BlockSpec((tile, d), lambda i: (i, 0))
    return pl.pallas_call(
        _memkv_kernel, grid=(n // tile,),
        in_specs=[row, _const_spec(g.shape), _const_spec(wk.shape), _const_spec(wv.shape)],
        out_specs=(row, row),
        out_shape=(jax.ShapeDtypeStruct((n, d), f32), jax.ShapeDtypeStruct((n, d), f32)),
        name="memory_kv",
        compiler_params=pltpu.CompilerParams(
            dimension_semantics=("arbitrary",), vmem_limit_bytes=VMEM_LIMIT),
    )(mem2d, g, wk, wv)


def _prompt_xattn_kernel(x_ref, gpre_ref, wq_ref, mk_ref, mv_ref, wo_ref, gpost_ref, y_ref):
    x = x_ref[0]
    hn = _rmsnorm(x, gpre_ref[...]).astype(bf16)
    q = jnp.dot(hn, wq_ref[...], preferred_element_type=f32)
    outs = []
    for hd in range(X_HEADS):
        sl = slice(hd * X_HEAD_DIM, (hd + 1) * X_HEAD_DIM)
        s = _dot_nt(q[:, sl], mk_ref[0, :, sl]) * X_SCALE
        m = jnp.max(s, axis=-1, keepdims=True)
        e = jnp.exp(s - m)
        p = e * (1.0 / jnp.sum(e, axis=-1, keepdims=True))
        outs.append(_dot(p, mv_ref[0, :, sl]))
    o = jnp.concatenate(outs, axis=-1)
    c = _dot(o, wo_ref[...])
    y_ref[0] = x + _rmsnorm(c, gpost_ref[...])


def _prompt_xattn(x, gpre, wq, mk, mv, wo, gpost, tile):
    B, L, D = x.shape
    xs = pl.BlockSpec((1, tile, D), lambda b, i: (b, i, 0))
    ms = pl.BlockSpec((1, N_MEM, D), lambda b, i: (b, 0, 0))
    return pl.pallas_call(
        _prompt_xattn_kernel, grid=(B, L // tile),
        in_specs=[xs, _const_spec(gpre.shape), _const_spec(wq.shape), ms, ms,
                  _const_spec(wo.shape), _const_spec(gpost.shape)],
        out_specs=xs, out_shape=jax.ShapeDtypeStruct((B, L, D), f32),
        name="prompt_xattn",
        compiler_params=pltpu.CompilerParams(
            dimension_semantics=("arbitrary", "arbitrary"), vmem_limit_bytes=VMEM_LIMIT),
    )(x, gpre, wq, mk, mv, wo, gpost)


def _ffn_kernel(x_ref, gpre_ref, wg_ref, wu_ref, wd_ref, gpost_ref, y_ref):
    x = x_ref[...]
    hf = _rmsnorm(x, gpre_ref[...]).astype(bf16)
    gate = jnp.dot(hf, wg_ref[...], preferred_element_type=f32)
    up = jnp.dot(hf, wu_ref[...], preferred_element_type=f32)
    act = (_silu(gate) * up).astype(bf16)
    f = jnp.dot(act, wd_ref[...], preferred_element_type=f32)
    y_ref[...] = x + _rmsnorm(f, gpost_ref[...])


def _ffn(x2d, gpre, wg, wu, wd, gpost, tile):
    n, d = x2d.shape
    row = pl.BlockSpec((tile, d), lambda i: (i, 0))
    return pl.pallas_call(
        _ffn_kernel, grid=(n // tile,),
        in_specs=[row, _const_spec(gpre.shape), _const_spec(wg.shape), _const_spec(wu.shape),
                  _const_spec(wd.shape), _const_spec(gpost.shape)],
        out_specs=row, out_shape=jax.ShapeDtypeStruct((n, d), f32),
        name="ffn",
        compiler_params=pltpu.CompilerParams(
            dimension_semantics=("arbitrary",), vmem_limit_bytes=VMEM_LIMIT),
    )(x2d, gpre, wg, wu, wd, gpost)


def _pad_rows(a, rows):
    if a.shape[0] == rows:
        return a
    return jnp.concatenate([a, jnp.zeros((rows - a.shape[0], a.shape[1]), a.dtype)], axis=0)


def _sample_mixer_kernel(x_ref, cprev_ref, st_ref, ck_ref, cv_ref,
                         gpre_ref, win_ref, convw_ref, convb_ref, dtb_ref, alog_ref,
                         dskip_ref, gssd_ref, sinkcol_ref, wout_ref, gpost_ref,
                         y_ref, ssm_ref, cnew_ref, knew_ref, vnew_ref, *, bt, steps):
    R = steps * bt
    half = SSD_WIDTH // SSD_GROUPS
    x = x_ref[...].reshape(R, D_MODEL)
    h = _rmsnorm(x, gpre_ref[...]).astype(bf16)
    z = jnp.dot(h, win_ref[:, S_Z:S_XBC], preferred_element_type=f32)
    u = jnp.dot(h, win_ref[:, S_XBC:S_Q], preferred_element_type=f32)
    qpad = jnp.dot(h, win_ref[:, S_Q:S_K], preferred_element_type=f32)
    k_new = jnp.dot(h, win_ref[:, S_K:S_V], preferred_element_type=f32)
    v_new = jnp.dot(h, win_ref[:, S_V:S_DT], preferred_element_type=f32)
    dt_raw = jnp.dot(h, win_ref[:, S_DT:S_END], preferred_element_type=f32)
    lane = lax.broadcasted_iota(jnp.int32, (1, LANES), 1)
    dt = jnp.where(lane < SSD_HEADS, _softplus(dt_raw + dtb_ref[...]), 0.0)

    def slab(a, t):
        return a[t * bt:(t + 1) * bt]

    hist = [cprev_ref[j] for j in range(SSD_CONV - 1)] + [slab(u, t) for t in range(steps)]
    xbc_t = []
    for t in range(steps):
        acc = convb_ref[...]
        for j in range(SSD_CONV):
            acc = acc + hist[t + j] * convw_ref[j:j + 1, :]
        xbc_t.append(_silu(acc))
    for j in range(SSD_CONV - 1):
        cnew_ref[j] = hist[steps + j]
    xbc = jnp.concatenate(xbc_t, axis=0)
    xs = xbc[:, 0:SSD_WIDTH]
    bm = xbc[:, SSD_WIDTH:SSD_WIDTH + half]
    cm = xbc[:, SSD_WIDTH + half:SSD_CONV_DIM]

    a_row = -jnp.exp(alog_ref[...])
    adt = dt * a_row
    cs_t = [slab(adt, 0)]
    for t in range(1, steps):
        cs_t.append(cs_t[-1] + slab(adt, t))
    cs = jnp.concatenate(cs_t, axis=0)
    tot = cs_t[-1]
    tot_rows = jnp.concatenate([tot] * steps, axis=0)
    expand = _head_expand_matrix()
    xdt = xs * _dot_x2(dt, expand)
    ecs_e = _dot_x2(jnp.exp(cs), expand)
    w_end = xdt * _dot_x2(jnp.exp(tot_rows - cs), expand)
    dec_e = _dot_x2(jnp.exp(tot), expand)

    gr = lax.broadcasted_iota(jnp.int32, (half, SSD_WIDTH), 0)
    gc = lax.broadcasted_iota(jnp.int32, (half, SSD_WIDTH), 1)
    gsum = ((gr >> 7) == (gc >> 8)).astype(bf16)

    y_t = []
    for t in range(steps):
        acc = None
        for s in range(t + 1):
            coef = _dot_x2(slab(cm, t) * slab(bm, s), gsum)
            if s < t:
                coef = coef * _dot_x2(jnp.exp(cs_t[t] - cs_t[s]), expand)
            term = coef * slab(xdt, s)
            acc = term if acc is None else acc + term
        y_t.append(acc)
    y_intra = jnp.concatenate(y_t, axis=0)

    b_idx = lax.broadcasted_iota(jnp.int32, (bt, 1, LANES), 0)
    l_idx = lax.broadcasted_iota(jnp.int32, (bt, 1, LANES), 2)
    pair = ((l_idx & (bt - 1)) == b_idx) & (l_idx < R)
    own = (l_idx == b_idx)
    y_off_parts = []
    for g in range(SSD_GROUPS):
        gl = slice(g * half, (g + 1) * half)
        h0 = st_ref[:, gl, :]
        cg = _pad_rows(cm[:, g * SSD_STATE:(g + 1) * SSD_STATE], LANES)
        zz = _dot_nt(h0.reshape(bt * half, SSD_STATE), cg).reshape(bt, half, LANES)
        yt = jnp.sum(jnp.where(pair, zz, 0.0), axis=0)
        y_off_parts.append(yt.T[0:R, :])
        wt = _pad_rows(w_end[:, gl], LANES).T
        lhs = jnp.where(pair, wt[None], 0.0).reshape(bt * half, LANES)
        bg = _pad_rows(bm[:, g * SSD_STATE:(g + 1) * SSD_STATE], LANES)
        contrib = _dot(lhs, bg).reshape(bt, half, SSD_STATE)
        dec_t = _pad_rows(dec_e[:, gl], LANES).T
        dec = jnp.sum(jnp.where(own, dec_t[None], 0.0), axis=-1, keepdims=True)
        ssm_ref[:, gl, :] = h0 * dec + contrib
    y_off = jnp.concatenate(y_off_parts, axis=-1) * ecs_e
    y = y_intra + y_off + xs * dskip_ref[...]
    y_ssd = _gated_group_norm(y, z, gssd_ref[...])

    qr = jnp.concatenate([qpad[:, hd * LANES:(hd + 1) * LANES] for hd in range(ATT_HEADS)], axis=0)
    HR = ATT_HEADS * R
    qr_b = qr.astype(bf16)
    ridx = lax.broadcasted_iota(jnp.int32, (HR, 1), 0)
    rb = ridx & (bt - 1)
    rt = (ridx // bt) & (steps - 1)
    s_c = jnp.zeros((HR, WINDOW), f32)
    for b in range(bt):
        s_c = jnp.where(rb == b, _dot_nt(qr_b, ck_ref[b]), s_c)
    s_c = s_c * ATT_SCALE
    reps = HR // bt
    s_n = []
    for t2 in range(steps):
        kt = jnp.concatenate([slab(k_new, t2)] * reps, axis=0)
        s_n.append(jnp.sum(qr * kt, axis=-1, keepdims=True) * ATT_SCALE)
    jcol = lax.broadcasted_iota(jnp.int32, (HR, WINDOW), 1)
    mask_c = jcol > rt
    sink = sinkcol_ref[...]
    s_c = jnp.where(mask_c, s_c, -jnp.inf)
    m = jnp.maximum(jnp.max(s_c, axis=-1, keepdims=True), sink)
    for t2 in range(steps):
        s_n[t2] = jnp.where(rt >= t2, s_n[t2], -jnp.inf)
        m = jnp.maximum(m, s_n[t2])
    e_c = jnp.exp(s_c - m)
    e_n = [jnp.exp(s - m) for s in s_n]
    den = jnp.sum(e_c, axis=-1, keepdims=True) + jnp.exp(sink - m)
    for e in e_n:
        den = den + e
    p_c = (e_c / den).astype(bf16)
    o = jnp.zeros((HR, LANES), f32)
    for b in range(bt):
        o = jnp.where(rb == b, _dot(p_c, cv_ref[b]), o)
    for t2 in range(steps):
        vt = jnp.concatenate([slab(v_new, t2)] * reps, axis=0)
        o = o + (e_n[t2] / den) * vt
    y_att = jnp.concatenate([o[hd * R:(hd + 1) * R] for hd in range(ATT_HEADS)], axis=-1)

    mix = (jnp.dot(y_ssd.astype(bf16), wout_ref[0:SSD_WIDTH, :], preferred_element_type=f32)
           + jnp.dot(y_att.astype(bf16), wout_ref[SSD_WIDTH:, :], preferred_element_type=f32))
    y_ref[...] = (x + _rmsnorm(mix, gpost_ref[...])).reshape(steps, bt, D_MODEL)
    knew_ref[...] = k_new.reshape(steps, bt, ATT_KV_WIDTH)
    vnew_ref[...] = v_new.reshape(steps, bt, ATT_KV_WIDTH)


def _sample_mixer(x_tm, cprev_tm, st, ck, cv, gpre, win, convw, convb, dtb, alog, dskip_e, gssd,
                  sinkcol, wout, gpost, bt):
    steps, nb, D = x_tm.shape
    kern = functools.partial(_sample_mixer_kernel, bt=bt, steps=steps)
    tm = lambda w: pl.BlockSpec((steps, bt, w), lambda i: (0, i, 0))
    in_specs = [
        tm(D),
        pl.BlockSpec((SSD_CONV - 1, bt, SSD_CONV_DIM), lambda i: (0, i, 0)),
        pl.BlockSpec((bt, SSD_WIDTH, SSD_STATE), lambda i: (i, 0, 0)),
        pl.BlockSpec((bt, WINDOW, ATT_KV_WIDTH), lambda i: (i, 0, 0)),
        pl.BlockSpec((bt, WINDOW, ATT_KV_WIDTH), lambda i: (i, 0, 0)),
    ] + [_const_spec(a.shape) for a in (gpre, win, convw, convb, dtb, alog, dskip_e, gssd,
                                        sinkcol, wout, gpost)]
    out_specs = (
        tm(D),
        pl.BlockSpec((bt, SSD_WIDTH, SSD_STATE), lambda i: (i, 0, 0)),
        pl.BlockSpec((SSD_CONV - 1, bt, SSD_CONV_DIM), lambda i: (0, i, 0)),
        tm(ATT_KV_WIDTH), tm(ATT_KV_WIDTH),
    )
    out_shape = (
        jax.ShapeDtypeStruct((steps, nb, D), f32),
        jax.ShapeDtypeStruct((nb, SSD_WIDTH, SSD_STATE), f32),
        jax.ShapeDtypeStruct((SSD_CONV - 1, nb, SSD_CONV_DIM), f32),
        jax.ShapeDtypeStruct((steps, nb, ATT_KV_WIDTH), f32),
        jax.ShapeDtypeStruct((steps, nb, ATT_KV_WIDTH), f32),
    )
    return pl.pallas_call(
        kern, grid=(nb // bt,), in_specs=in_specs, out_specs=out_specs, out_shape=out_shape,
        name="sample_mixer",
        compiler_params=pltpu.CompilerParams(
            dimension_semantics=("arbitrary",), vmem_limit_bytes=VMEM_LIMIT),
    )(x_tm, cprev_tm, st, ck, cv, gpre, win, convw, convb, dtb, alog, dskip_e, gssd, sinkcol,
      wout, gpost)


def _sample_xattn_kernel(x_ref, mk_ref, mv_ref, gpre_ref, wq_ref, wo_ref, gpost_ref, y_ref,
                         *, bt, steps):
    R = steps * bt
    nrow = bt * N_MEM * X_HEADS
    x = x_ref[...].reshape(R, D_MODEL)
    hn = _rmsnorm(x, gpre_ref[...]).astype(bf16)
    q = jnp.dot(hn, wq_ref[...], preferred_element_type=f32)
    qs = jnp.concatenate([q[:, hd * X_HEAD_DIM:(hd + 1) * X_HEAD_DIM] for hd in range(X_HEADS)],
                         axis=0)
    kall = mk_ref[...].reshape(nrow, X_HEAD_DIM)
    vall = mv_ref[...].reshape(nrow, X_HEAD_DIM)
    z = _dot_nt(kall, qs) * X_SCALE
    ncol = X_HEADS * R
    z = z.reshape(bt, N_MEM * X_HEADS // SUBLANES, SUBLANES, ncol)
    mshape = (bt, 1, SUBLANES, ncol)
    b_i = lax.broadcasted_iota(jnp.int32, mshape, 0)
    r_i = lax.broadcasted_iota(jnp.int32, mshape, 2)
    c_i = lax.broadcasted_iota(jnp.int32, mshape, 3)
    keep = (b_i == (c_i % bt)) & ((r_i % X_HEADS) == (c_i // R))
    z = jnp.where(keep, z, -jnp.inf).reshape(nrow, ncol)
    m = jnp.max(z, axis=0, keepdims=True)
    e = jnp.exp(z - m)
    p = e * (1.0 / jnp.sum(e, axis=0, keepdims=True))
    o = lax.dot_general(p.astype(bf16), vall.astype(bf16), (((0,), (0,)), ((), ())),
                        preferred_element_type=f32)
    o = jnp.concatenate([o[hd * R:(hd + 1) * R] for hd in range(X_HEADS)], axis=-1)
    cc = _dot(o, wo_ref[...])
    y_ref[...] = (x + _rmsnorm(cc, gpost_ref[...])).reshape(steps, bt, D_MODEL)


def _sample_xattn(x_tm, mk, mv, gpre, wq, wo, gpost, bt):
    steps, nb, D = x_tm.shape
    kern = functools.partial(_sample_xattn_kernel, bt=bt, steps=steps)
    xs = pl.BlockSpec((steps, bt, D), lambda i: (0, i, 0))
    ms = pl.BlockSpec((bt, N_MEM, X_HEADS, X_HEAD_DIM), lambda i: (i, 0, 0, 0))
    return pl.pallas_call(
        kern, grid=(nb // bt,),
        in_specs=[xs, ms, ms, _const_spec(gpre.shape), _const_spec(wq.shape),
                  _const_spec(wo.shape), _const_spec(gpost.shape)],
        out_specs=xs, out_shape=jax.ShapeDtypeStruct((steps, nb, D), f32),
        name="sample_xattn",
        compiler_params=pltpu.CompilerParams(
            dimension_semantics=("arbitrary",), vmem_limit_bytes=VMEM_LIMIT),
    )(x_tm, mk, mv, gpre, wq, wo, gpost)


def _row(v, width=None):
    v = v.reshape(1, -1).astype(f32)
    if width is not None and v.shape[1] < width:
        v = jnp.pad(v, ((0, 0), (0, width - v.shape[1])))
    return v


def kernel(x_prompt, x_sample, state_ssm, state_conv, cache_win_k, cache_win_v, cache_mem_k, cache_mem_v, mem_prompt, g_mix_pre, w_in, conv_w, conv_b, dt_bias, a_log, d_skip, g_ssd_norm, sinks, w_out, g_mix_post, g_x_pre, w_xq, g_mem, w_xk, w_xv, w_xo, g_x_post, g_ffn_pre, w_gate, w_up, w_down, g_ffn_post):
    depth = w_in.shape[0]
    assert depth == 1
    B, L, D = x_prompt.shape
    NB, steps, _ = x_sample.shape
    li = 0

    w = w_in[li]
    wz, wxbc, wdt = w[:, 0:512], w[:, 512:1536], w[:, 1536:1544]
    wq, wk, wv = w[:, 1544:2056], w[:, 2056:2184], w[:, 2184:2312]
    wdt_p = jnp.pad(wdt, ((0, 0), (0, LANES - SSD_HEADS)))
    win_p = jnp.concatenate([wz, wxbc, wq * ATT_SCALE, wk, wv, wdt_p], axis=1).astype(bf16)
    zero_half = jnp.zeros((D, ATT_HEAD_DIM), f32)
    q_blocks = []
    for hd in range(ATT_HEADS):
        wq_h = wq[:, hd * ATT_HEAD_DIM:(hd + 1) * ATT_HEAD_DIM]
        q_blocks += [wq_h, zero_half] if hd < ATT_HEADS // ATT_KV_HEADS else [zero_half, wq_h]
    win_s = jnp.concatenate([wz, wxbc] + q_blocks + [wk, wv, wdt_p], axis=1).astype(bf16)
    wo = w_out[li]
    wo_b = wo.astype(bf16)
    zero_rows = jnp.zeros((ATT_HEAD_DIM, D), f32)
    o_blocks = [wo[0:SSD_WIDTH]]
    for hd in range(ATT_HEADS):
        wo_h = wo[SSD_WIDTH + hd * ATT_HEAD_DIM:SSD_WIDTH + (hd + 1) * ATT_HEAD_DIM]
        o_blocks += [wo_h, zero_rows] if hd < ATT_HEADS // ATT_KV_HEADS else [zero_rows, wo_h]
    wo_s = jnp.concatenate(o_blocks, axis=0).astype(bf16)

    gpre, gpost = _row(g_mix_pre[li]), _row(g_mix_post[li])
    convw, convb = conv_w[li].astype(f32), _row(conv_b[li])
    dtb, alog = _row(dt_bias[li], LANES), _row(a_log[li], LANES)
    dskip_e = _row(jnp.repeat(d_skip[li], SSD_HEAD_DIM))
    gssd = _row(g_ssd_norm[li])
    sk = sinks[li].astype(f32)

    mk2d, mv2d = _memkv(mem_prompt.reshape(B * N_MEM, D), _row(g_mem[li]),
                        w_xk[li].astype(bf16), w_xv[li].astype(bf16), tile=512)
    mk3, mv3 = mk2d.reshape(B, N_MEM, D), mv2d.reshape(B, N_MEM, D)
    x1, p_ssm, p_conv8, p_wk, p_wv = _prompt_mixer(
        x_prompt, gpre, win_p, convw, convb, dtb, alog, dskip_e, gssd, sk, wo_b, gpost, tile=512)
    wxq_b, wxo_b = w_xq[li].astype(bf16), w_xo[li].astype(bf16)
    gxpre, gxpost = _row(g_x_pre[li]), _row(g_x_post[li])
    x2 = _prompt_xattn(x1, gxpre, wxq_b, mk3, mv3, wxo_b, gxpost, tile=512)
    wg_b, wu_b, wd_b = w_gate[li].astype(bf16), w_up[li].astype(bf16), w_down[li].astype(bf16)
    gfpre, gfpost = _row(g_ffn_pre[li]), _row(g_ffn_post[li])
    yp = _ffn(x2.reshape(B * L, D), gfpre, wg_b, wu_b, wd_b, gfpost, tile=512).reshape(B, L, D)

    bt = 8
    x_tm = jnp.transpose(x_sample, (1, 0, 2))
    cprev_tm = jnp.transpose(state_conv[li], (1, 0, 2))
    st = state_ssm[li].reshape(NB, SSD_WIDTH, SSD_STATE)
    ck = cache_win_k[li].reshape(NB, WINDOW, ATT_KV_WIDTH)
    cv = cache_win_v[li].reshape(NB, WINDOW, ATT_KV_WIDTH)
    sinkcol = jnp.repeat(sk, steps * bt).reshape(ATT_HEADS * steps * bt, 1)
    x1s, s_ssm, cnew_tm, knew_tm, vnew_tm = _sample_mixer(
        x_tm, cprev_tm, st, ck, cv, gpre, win_s, convw, convb, dtb, alog, dskip_e, gssd,
        sinkcol, wo_s, gpost, bt=bt)
    cmk = cache_mem_k.reshape(NB, N_MEM, X_HEADS, X_HEAD_DIM)
    cmv = cache_mem_v.reshape(NB, N_MEM, X_HEADS, X_HEAD_DIM)
    x2s = _sample_xattn(x1s, cmk, cmv, gxpre, wxq_b, wxo_b, gxpost, bt=bt)
    ys_tm = _ffn(x2s.reshape(steps * NB, D), gfpre, wg_b, wu_b, wd_b, gfpost, tile=steps * NB)
    ys = jnp.transpose(ys_tm.reshape(steps, NB, D), (1, 0, 2))

    s_conv = jnp.transpose(cnew_tm, (1, 0, 2))
    knew = jnp.transpose(knew_tm, (1, 0, 2))
    vnew = jnp.transpose(vnew_tm, (1, 0, 2))
    s_wk = jnp.concatenate([ck[:, steps:], knew], axis=1)
    s_wv = jnp.concatenate([cv[:, steps:], vnew], axis=1)
    kv_shape = (ATT_KV_HEADS, ATT_HEAD_DIM)
    return (
        yp, ys,
        p_ssm.reshape(1, B, SSD_HEADS, SSD_HEAD_DIM, SSD_STATE),
        p_conv8[:, SUBLANES - (SSD_CONV - 1):, :][None],
        p_wk.reshape(1, B, WINDOW, *kv_shape), p_wv.reshape(1, B, WINDOW, *kv_shape),
        mk3.reshape(1, B, N_MEM, X_HEADS, X_HEAD_DIM), mv3.reshape(1, B, N_MEM, X_HEADS, X_HEAD_DIM),
        s_ssm.reshape(1, NB, SSD_HEADS, SSD_HEAD_DIM, SSD_STATE),
        s_conv[None],
        s_wk.reshape(1, NB, WINDOW, *kv_shape), s_wv.reshape(1, NB, WINDOW, *kv_shape),
    )
```

```python
import functools

import jax
import jax.numpy as jnp
from jax import lax
from jax.experimental import pallas as pl
from jax.experimental.pallas import tpu as pltpu

f32 = jnp.float32
bf16 = jnp.bfloat16

D_MODEL = 1024
EPS = 1e-6
N_MEM = 256
SSD_HEADS = 8
SSD_HEAD_DIM = 64
SSD_WIDTH = 512
SSD_GROUPS = 2
SSD_STATE = 128
SSD_CONV = 4
SSD_CHUNK = 128
SSD_CONV_DIM = 1024
ATT_HEADS = 8
ATT_KV_HEADS = 2
ATT_HEAD_DIM = 64
ATT_WIDTH = 512
ATT_KV_WIDTH = 128
WINDOW = 128
ATT_SCALE = ATT_HEAD_DIM ** -0.5
X_HEADS = 4
X_HEAD_DIM = 256
X_SCALE = X_HEAD_DIM ** -0.5
D_FF = 2816
LANES = 128
SUBLANES = 8
VMEM_LIMIT = 56 * 1024 * 1024

P_Z, P_XBC, P_Q, P_K, P_V, P_DT, P_END = 0, 512, 1536, 2048, 2176, 2304, 2432


def _dot(a, b):
    return jnp.dot(a.astype(bf16), b.astype(bf16), preferred_element_type=f32)


def _dot_nt(a, b):
    return lax.dot_general(a.astype(bf16), b.astype(bf16), (((1,), (1,)), ((), ())),
                           preferred_element_type=f32)


def _split2(x):
    hi = x.astype(bf16)
    lo = (x - hi.astype(f32)).astype(bf16)
    return hi, lo


def _dot_x2(x, m):
    hi, lo = _split2(x)
    return (jnp.dot(hi, m, preferred_element_type=f32)
            + jnp.dot(lo, m, preferred_element_type=f32))


def _rmsnorm(x, g):
    ms = jnp.mean(x * x, axis=-1, keepdims=True)
    return x * lax.rsqrt(ms + EPS) * g


def _silu(x):
    return x * jax.nn.sigmoid(x)


def _softplus(x):
    return jnp.maximum(x, 0.0) + jnp.log1p(jnp.exp(-jnp.abs(x)))


def _head_expand_matrix():
    r = lax.broadcasted_iota(jnp.int32, (LANES, SSD_WIDTH), 0)
    c = lax.broadcasted_iota(jnp.int32, (LANES, SSD_WIDTH), 1)
    return (r == (c >> 6)).astype(bf16)


def _gated_group_norm(y, z, g):
    u = y * _silu(z)
    half = SSD_WIDTH // SSD_GROUPS
    parts = []
    for gi in range(SSD_GROUPS):
        ug = u[:, gi * half:(gi + 1) * half]
        parts.append(ug * lax.rsqrt(jnp.mean(ug * ug, axis=-1, keepdims=True) + EPS))
    return jnp.concatenate(parts, axis=-1) * g


def _prompt_mixer_kernel(x_ref, gpre_ref, win_ref, convw_ref, convb_ref, dtb_ref, alog_ref,
                         dskip_ref, gssd_ref, sinks_ref, wout_ref, gpost_ref,
                         y_ref, ssm_ref, conv_ref, wk_ref, wv_ref,
                         statet_sc, xbc_ext_sc, xbc_sc, z_sc, q_sc, k_sc, v_sc, dt_sc,
                         kprev_sc, vprev_sc, mix_sc, *, tile):
    i = pl.program_id(1)
    nchunk = tile // SSD_CHUNK
    C = SSD_CHUNK

    @pl.when(i == 0)
    def _init():
        statet_sc[...] = jnp.zeros_like(statet_sc)
        xbc_ext_sc[0:SUBLANES, :] = jnp.zeros((SUBLANES, SSD_CONV_DIM), f32)
        kprev_sc[...] = jnp.zeros_like(kprev_sc)
        vprev_sc[...] = jnp.zeros_like(vprev_sc)

    x = x_ref[0]
    h = _rmsnorm(x, gpre_ref[...]).astype(bf16)
    z_sc[...] = jnp.dot(h, win_ref[:, P_Z:P_XBC], preferred_element_type=f32)
    xbc_ext_sc[SUBLANES:SUBLANES + tile, :] = jnp.dot(h, win_ref[:, P_XBC:P_Q],
                                                      preferred_element_type=f32)
    q_sc[...] = jnp.dot(h, win_ref[:, P_Q:P_K], preferred_element_type=f32)
    k_sc[...] = jnp.dot(h, win_ref[:, P_K:P_V], preferred_element_type=f32)
    v_sc[...] = jnp.dot(h, win_ref[:, P_V:P_DT], preferred_element_type=f32)
    dt_raw = jnp.dot(h, win_ref[:, P_DT:P_END], preferred_element_type=f32)
    lane = lax.broadcasted_iota(jnp.int32, (1, LANES), 1)
    dt_sc[...] = jnp.where(lane < SSD_HEADS, _softplus(dt_raw + dtb_ref[...]), 0.0)

    acc = convb_ref[...]
    for j in range(SSD_CONV):
        off = SUBLANES - (SSD_CONV - 1) + j
        acc = acc + xbc_ext_sc[off:off + tile, :] * convw_ref[j:j + 1, :]
    xbc_sc[...] = _silu(acc)
    tail = xbc_ext_sc[tile:tile + SUBLANES, :]
    conv_ref[0] = tail
    xbc_ext_sc[0:SUBLANES, :] = tail

    a_row = -jnp.exp(alog_ref[...])
    expand = _head_expand_matrix()
    expand2 = jnp.concatenate([expand, expand], axis=0)
    row_i = lax.broadcasted_iota(jnp.int32, (C, C), 0)
    col_i = lax.broadcasted_iota(jnp.int32, (C, C), 1)
    lower = col_i <= row_i
    tri = lower.astype(bf16)
    tri3 = jnp.concatenate([tri, tri, tri], axis=1)
    lo_half = lane < ATT_HEAD_DIM
    half = SSD_WIDTH // SSD_GROUPS
    head_of_lane = lax.broadcasted_iota(jnp.int32, (1, half), 1) >> 6

    def chunk(c, carry):
        r0 = pl.multiple_of(c * C, C)
        rows = pl.ds(r0, C)
        xs = xbc_sc[rows, 0:SSD_WIDTH]
        bm = xbc_sc[rows, SSD_WIDTH:SSD_WIDTH + half]
        cm = xbc_sc[rows, SSD_WIDTH + half:SSD_CONV_DIM]
        dtc = dt_sc[rows, :]
        adt = dtc * a_row
        a_hi = adt.astype(bf16)
        a_r1 = adt - a_hi.astype(f32)
        a_mid = a_r1.astype(bf16)
        a_lo = (a_r1 - a_mid.astype(f32)).astype(bf16)
        cs = jnp.dot(tri3, jnp.concatenate([a_hi, a_mid, a_lo], axis=0), preferred_element_type=f32)
        cs_t = cs.T
        tot = cs[C - 1:C, :]
        fac = jnp.concatenate([dtc, jnp.exp(cs), jnp.exp(tot - cs)], axis=0)
        f_hi = fac.astype(bf16)
        f_lo = (fac - f_hi.astype(f32)).astype(bf16)
        fac_e = jnp.dot(jnp.concatenate([f_hi, f_lo], axis=1), expand2,
                        preferred_element_type=f32)
        xdt = xs * fac_e[0:C]
        ecs_e = fac_e[C:2 * C]
        w_end = xdt * fac_e[2 * C:3 * C]
        cb_all = _dot_nt(jnp.concatenate([cm[:, 0:SSD_STATE], cm[:, SSD_STATE:half]], axis=0),
                         jnp.concatenate([bm[:, 0:SSD_STATE], bm[:, SSD_STATE:half]], axis=0))
        bm_t = bm.T
        xdt_b = xdt.astype(bf16)
        y_parts = []
        for g in range(SSD_GROUPS):
            gl = slice(g * half, (g + 1) * half)
            cg = cm[:, g * SSD_STATE:(g + 1) * SSD_STATE]
            cb = cb_all[g * C:(g + 1) * C, g * C:(g + 1) * C]
            st = statet_sc[g]
            y_off = _dot(cg, st) * ecs_e[:, gl]
            m_parts, x_parts = [], []
            for r in range(SSD_HEADS // SSD_GROUPS):
                hh = g * (SSD_HEADS // SSD_GROUPS) + r
                diff = cs[:, hh:hh + 1] - cs_t[hh:hh + 1, :]
                m_parts.append((cb * jnp.exp(jnp.where(lower, diff, -jnp.inf))).astype(bf16))
                x_parts.append(jnp.where(head_of_lane == r, xdt_b[:, gl], jnp.zeros((), bf16)))
            y_d = jnp.dot(jnp.concatenate(m_parts, axis=1), jnp.concatenate(x_parts, axis=0),
                          preferred_element_type=f32)
            y_parts.append(y_d + y_off)
            statet_sc[g] = (st * ecs_e[C - 1:C, gl]
                            + _dot(bm_t[g * SSD_STATE:(g + 1) * SSD_STATE, :], w_end[:, gl]))
        y = jnp.concatenate(y_parts, axis=-1) + xs * dskip_ref[...]
        y_ssd = _gated_group_norm(y, z_sc[rows, :], gssd_ref[...])
        mix_sc[rows, 0:SSD_WIDTH] = y_ssd.astype(bf16)

        q = q_sc[rows, :].astype(bf16)
        k = k_sc[rows, :]
        v = v_sc[rows, :]
        kk = jnp.concatenate([kprev_sc[...], k], axis=0)
        vv = jnp.concatenate([vprev_sc[...], v], axis=0)
        kk_r = pltpu.roll(kk, ATT_HEAD_DIM, 1)
        vv_r = pltpu.roll(vv, ATT_HEAD_DIM, 1)
        k_lo = [jnp.where(lo_half, kk, 0.0).astype(bf16), jnp.where(lo_half, kk_r, 0.0).astype(bf16)]
        k_hi = [jnp.where(lo_half, 0.0, kk_r).astype(bf16), jnp.where(lo_half, 0.0, kk).astype(bf16)]
        v_lo = [jnp.where(lo_half, vv, 0.0).astype(bf16), jnp.where(lo_half, vv_r, 0.0).astype(bf16)]
        v_hi = [jnp.where(lo_half, 0.0, vv_r).astype(bf16), jnp.where(lo_half, 0.0, vv).astype(bf16)]
        prev_bias = jnp.where((i * nchunk + c) > 0, 0.0, -jnp.inf)
        for g in range(ATT_KV_HEADS):
            qg = jnp.concatenate([q[:, (2 * g) * LANES:(2 * g + 1) * LANES],
                                  q[:, (2 * g + 1) * LANES:(2 * g + 2) * LANES]], axis=0)
            s = _dot_nt(qg, jnp.concatenate([k_lo[g], k_hi[g]], axis=0))
            p_rows = []
            for jb in range(2):
                p_cols = []
                for sub in range(2):
                    sink = sinks_ref[(2 * g + jb) * 2 + sub]
                    s_prev = s[jb * C:(jb + 1) * C, (2 * sub) * C:(2 * sub + 1) * C]
                    s_cur = s[jb * C:(jb + 1) * C, (2 * sub + 1) * C:(2 * sub + 2) * C]
                    sc = jnp.where(lower, s_cur, s_prev + prev_bias)
                    m = jnp.maximum(jnp.max(sc, axis=-1, keepdims=True), sink)
                    e = jnp.exp(sc - m)
                    den = jnp.sum(e, axis=-1, keepdims=True) + jnp.exp(sink - m)
                    p = (e * (1.0 / den)).astype(bf16)
                    zero = jnp.zeros((), bf16)
                    p_cols += [jnp.where(lower, zero, p), jnp.where(lower, p, zero)]
                p_rows.append(jnp.concatenate(p_cols, axis=1))
            o = jnp.dot(jnp.concatenate(p_rows, axis=0),
                        jnp.concatenate([v_lo[g], v_hi[g]], axis=0),
                        preferred_element_type=f32)
            for jb in range(2):
                lo_l = SSD_WIDTH + (2 * g + jb) * LANES
                mix_sc[rows, lo_l:lo_l + LANES] = o[jb * C:(jb + 1) * C].astype(bf16)
        kprev_sc[...] = k
        vprev_sc[...] = v
        return carry

    lax.fori_loop(0, nchunk, chunk, 0)

    wk_ref[0] = k_sc[tile - WINDOW:tile, :]
    wv_ref[0] = v_sc[tile - WINDOW:tile, :]

    @pl.when(i == pl.num_programs(1) - 1)
    def _final_state():
        for g in range(SSD_GROUPS):
            ssm_ref[0, g * half:(g + 1) * half, :] = statet_sc[g].T

    mix = jnp.dot(mix_sc[...], wout_ref[...], preferred_element_type=f32)
    y_ref[0] = x + _rmsnorm(mix, gpost_ref[...])


def _const_spec(shape):
    nd = len(shape)
    return pl.BlockSpec(shape, lambda *_: (0,) * nd)


def _prompt_mixer(x, gpre, win, convw, convb, dtb, alog, dskip_e, gssd, sinks, wout, gpost, tile):
    B, L, D = x.shape
    grid = (B, L // tile)
    kern = functools.partial(_prompt_mixer_kernel, tile=tile)
    out_shape = (
        jax.ShapeDtypeStruct((B, L, D), f32),
        jax.ShapeDtypeStruct((B, SSD_WIDTH, SSD_STATE), f32),
        jax.ShapeDtypeStruct((B, SUBLANES, SSD_CONV_DIM), f32),
        jax.ShapeDtypeStruct((B, WINDOW, ATT_KV_WIDTH), f32),
        jax.ShapeDtypeStruct((B, WINDOW, ATT_KV_WIDTH), f32),
    )
    in_specs = [
        pl.BlockSpec((1, tile, D), lambda b, i: (b, i, 0)),
        _const_spec(gpre.shape), _const_spec(win.shape), _const_spec(convw.shape),
        _const_spec(convb.shape), _const_spec(dtb.shape), _const_spec(alog.shape),
        _const_spec(dskip_e.shape), _const_spec(gssd.shape),
        pl.BlockSpec(memory_space=pltpu.SMEM),
        _const_spec(wout.shape), _const_spec(gpost.shape),
    ]
    out_specs = (
        pl.BlockSpec((1, tile, D), lambda b, i: (b, i, 0)),
        pl.BlockSpec((1, SSD_WIDTH, SSD_STATE), lambda b, i: (b, 0, 0)),
        pl.BlockSpec((1, SUBLANES, SSD_CONV_DIM), lambda b, i: (b, 0, 0)),
        pl.BlockSpec((1, WINDOW, ATT_KV_WIDTH), lambda b, i: (b, 0, 0)),
        pl.BlockSpec((1, WINDOW, ATT_KV_WIDTH), lambda b, i: (b, 0, 0)),
    )
    scratch = [
        pltpu.VMEM((SSD_GROUPS, SSD_STATE, SSD_WIDTH // SSD_GROUPS), f32),
        pltpu.VMEM((tile + 2 * SUBLANES, SSD_CONV_DIM), f32),
        pltpu.VMEM((tile, SSD_CONV_DIM), f32),
        pltpu.VMEM((tile, SSD_WIDTH), f32),
        pltpu.VMEM((tile, ATT_WIDTH), f32),
        pltpu.VMEM((tile, ATT_KV_WIDTH), f32),
        pltpu.VMEM((tile, ATT_KV_WIDTH), f32),
        pltpu.VMEM((tile, LANES), f32),
        pltpu.VMEM((WINDOW, ATT_KV_WIDTH), f32),
        pltpu.VMEM((WINDOW, ATT_KV_WIDTH), f32),
        pltpu.VMEM((tile, 2 * SSD_WIDTH), bf16),
    ]
    return pl.pallas_call(
        kern, grid=grid, in_specs=in_specs, out_specs=out_specs, out_shape=out_shape,
        scratch_shapes=scratch, name="prompt_mixer",
        compiler_params=pltpu.CompilerParams(
            dimension_semantics=("arbitrary", "arbitrary"), vmem_limit_bytes=VMEM_LIMIT),
    )(x, gpre, win, convw, convb, dtb, alog, dskip_e, gssd, sinks, wout, gpost)


def _memkv_kernel(m_ref, g_ref, wk_ref, wv_ref, k_ref, v_ref):
    mn = _rmsnorm(m_ref[...], g_ref[...]).astype(bf16)
    k_ref[...] = jnp.dot(mn, wk_ref[...], preferred_element_type=f32)
    v_ref[...] = jnp.dot(mn, wv_ref[...], preferred_element_type=f32)


def _memkv(mem2d, g, wk, wv, tile):
    n, d = mem2d.shape
    row = pl.BlockSpec((tile, d), lambda i: (i, 0))
    return pl.pallas_call(
        _memkv_kernel, grid=(n // tile,),
        in_specs=[row, _const_spec(g.shape), _const_spec(wk.shape), _const_spec(wv.shape)],
        out_specs=(row, row),
        out_shape=(jax.ShapeDtypeStruct((n, d), f32), jax.ShapeDtypeStruct((n, d), f32)),
        name="memory_kv",
        compiler_params=pltpu.CompilerParams(
            dimension_semantics=("arbitrary",), vmem_limit_bytes=VMEM_LIMIT),
    )(mem2d, g, wk, wv)


def _prompt_xattn_kernel(x_ref, gpre_ref, wq_ref, mk_ref, mv_ref, wo_ref, gpost_ref, y_ref):
    x = x_ref[0]
    hn = _rmsnorm(x, gpre_ref[...]).astype(bf16)
    q = jnp.dot(hn, wq_ref[...], preferred_element_type=f32)
    outs = []
    for hd in range(X_HEADS):
        sl = slice(hd * X_HEAD_DIM, (hd + 1) * X_HEAD_DIM)
        s = _dot_nt(q[:, sl], mk_ref[0, :, sl]) * X_SCALE
        m = jnp.max(s, axis=-1, keepdims=True)
        e = jnp.exp(s - m)
        p = e * (1.0 / jnp.sum(e, axis=-1, keepdims=True))
        outs.append(_dot(p, mv_ref[0, :, sl]))
    o = jnp.concatenate(outs, axis=-1)
    c = _dot(o, wo_ref[...])
    y_ref[0] = x + _rmsnorm(c, gpost_ref[...])


def _prompt_xattn(x, gpre, wq, mk, mv, wo, gpost, tile):
    B, L, D = x.shape
    xs = pl.BlockSpec((1, tile, D), lambda b, i: (b, i, 0))
    ms = pl.BlockSpec((1, N_MEM, D), lambda b, i: (b, 0, 0))
    return pl.pallas_call(
        _prompt_xattn_kernel, grid=(B, L // tile),
        in_specs=[xs, _const_spec(gpre.shape), _const_spec(wq.shape), ms, ms,
                  _const_spec(wo.shape), _const_spec(gpost.shape)],
        out_specs=xs, out_shape=jax.ShapeDtypeStruct((B, L, D), f32),
        name="prompt_xattn",
        compiler_params=pltpu.CompilerParams(
            dimension_semantics=("arbitrary", "arbitrary"), vmem_limit_bytes=VMEM_LIMIT),
    )(x, gpre, wq, mk, mv, wo, gpost)


def _ffn_kernel(x_ref, gpre_ref, wg_ref, wu_ref, wd_ref, gpost_ref, y_ref):
    x = x_ref[...]
    hf = _rmsnorm(x, gpre_ref[...]).astype(bf16)
    gate = jnp.dot(hf, wg_ref[...], preferred_element_type=f32)
    up = jnp.dot(hf, wu_ref[...], preferred_element_type=f32)
    act = (_silu(gate) * up).astype(bf16)
    f = jnp.dot(act, wd_ref[...], preferred_element_type=f32)
    y_ref[...] = x + _rmsnorm(f, gpost_ref[...])


def _ffn(x2d, gpre, wg, wu, wd, gpost, tile):
    n, d = x2d.shape
    row = pl.BlockSpec((tile, d), lambda i: (i, 0))
    return pl.pallas_call(
        _ffn_kernel, grid=(n // tile,),
        in_specs=[row, _const_spec(gpre.shape), _const_spec(wg.shape), _const_spec(wu.shape),
                  _const_spec(wd.shape), _const_spec(gpost.shape)],
        out_specs=row, out_shape=jax.ShapeDtypeStruct((n, d), f32),
        name="ffn",
        compiler_params=pltpu.CompilerParams(
            dimension_semantics=("arbitrary",), vmem_limit_bytes=VMEM_LIMIT),
    )(x2d, gpre, wg, wu, wd, gpost)


def _pad_rows(a, rows):
    if a.shape[0] == rows:
        return a
    return jnp.concatenate([a, jnp.zeros((rows - a.shape[0], a.shape[1]), a.dtype)], axis=0)


def _sample_mixer_kernel(x_ref, cprev_ref, st_ref, ck_ref, cv_ref,
                         gpre_ref, win_ref, convw_ref, convb_ref, dtb_ref, alog_ref,
                         dskip_ref, gssd_ref, sinkrow_ref, wout_ref, gpost_ref,
                         y_ref, ssm_ref, cnew_ref, knew_ref, vnew_ref, *, bt, steps):
    R = steps * bt
    half = SSD_WIDTH // SSD_GROUPS
    x = x_ref[...].reshape(R, D_MODEL)
    h = _rmsnorm(x, gpre_ref[...]).astype(bf16)
    z = jnp.dot(h, win_ref[:, P_Z:P_XBC], preferred_element_type=f32)
    u = jnp.dot(h, win_ref[:, P_XBC:P_Q], preferred_element_type=f32)
    q = jnp.dot(h, win_ref[:, P_Q:P_K], preferred_element_type=f32)
    k_new = jnp.dot(h, win_ref[:, P_K:P_V], preferred_element_type=f32)
    v_new = jnp.dot(h, win_ref[:, P_V:P_DT], preferred_element_type=f32)
    dt_raw = jnp.dot(h, win_ref[:, P_DT:P_END], preferred_element_type=f32)
    lane = lax.broadcasted_iota(jnp.int32, (1, LANES), 1)
    dt = jnp.where(lane < SSD_HEADS, _softplus(dt_raw + dtb_ref[...]), 0.0)

    def slab(a, t):
        return a[t * bt:(t + 1) * bt]

    hist = [cprev_ref[j] for j in range(SSD_CONV - 1)] + [slab(u, t) for t in range(steps)]
    xbc_t = []
    for t in range(steps):
        acc = convb_ref[...]
        for j in range(SSD_CONV):
            acc = acc + hist[t + j] * convw_ref[j:j + 1, :]
        xbc_t.append(_silu(acc))
    for j in range(SSD_CONV - 1):
        cnew_ref[j] = hist[steps + j]
    xbc = jnp.concatenate(xbc_t, axis=0)
    xs = xbc[:, 0:SSD_WIDTH]
    bm = xbc[:, SSD_WIDTH:SSD_WIDTH + half]
    cm = xbc[:, SSD_WIDTH + half:SSD_CONV_DIM]

    a_row = -jnp.exp(alog_ref[...])
    adt = dt * a_row
    cs_t = [slab(adt, 0)]
    for t in range(1, steps):
        cs_t.append(cs_t[-1] + slab(adt, t))
    cs = jnp.concatenate(cs_t, axis=0)
    tot = cs_t[-1]
    tot_rows = jnp.concatenate([tot] * steps, axis=0)
    expand = _head_expand_matrix()
    xdt = xs * _dot_x2(dt, expand)
    ecs_e = _dot_x2(jnp.exp(cs), expand)
    w_end = xdt * _dot_x2(jnp.exp(tot_rows - cs), expand)
    dec_e = _dot_x2(jnp.exp(tot), expand)

    gr = lax.broadcasted_iota(jnp.int32, (half, SSD_WIDTH), 0)
    gc = lax.broadcasted_iota(jnp.int32, (half, SSD_WIDTH), 1)
    gsum = ((gr >> 7) == (gc >> 8)).astype(bf16)

    y_t = []
    for t in range(steps):
        acc = None
        for s in range(t + 1):
            coef = _dot_x2(slab(cm, t) * slab(bm, s), gsum)
            if s < t:
                coef = coef * _dot_x2(jnp.exp(cs_t[t] - cs_t[s]), expand)
            term = coef * slab(xdt, s)
            acc = term if acc is None else acc + term
        y_t.append(acc)
    y_intra = jnp.concatenate(y_t, axis=0)

    b_idx = lax.broadcasted_iota(jnp.int32, (bt, 1, LANES), 0)
    l_idx = lax.broadcasted_iota(jnp.int32, (bt, 1, LANES), 2)
    pair = ((l_idx & (bt - 1)) == b_idx) & (l_idx < R)
    own = (l_idx == b_idx)
    y_off_parts = []
    for g in range(SSD_GROUPS):
        gl = slice(g * half, (g + 1) * half)
        h0 = st_ref[:, gl, :]
        cg = _pad_rows(cm[:, g * SSD_STATE:(g + 1) * SSD_STATE], LANES)
        zz = _dot_nt(h0.reshape(bt * half, SSD_STATE), cg).reshape(bt, half, LANES)
        yt = jnp.sum(jnp.where(pair, zz, 0.0), axis=0)
        y_off_parts.append(yt.T[0:R, :])
        wt = _pad_rows(w_end[:, gl], LANES).T
        lhs = jnp.where(pair, wt[None], 0.0).reshape(bt * half, LANES)
        bg = _pad_rows(bm[:, g * SSD_STATE:(g + 1) * SSD_STATE], LANES)
        contrib = _dot(lhs, bg).reshape(bt, half, SSD_STATE)
        dec_t = _pad_rows(dec_e[:, gl], LANES).T
        dec = jnp.sum(jnp.where(own, dec_t[None], 0.0), axis=-1, keepdims=True)
        ssm_ref[:, gl, :] = h0 * dec + contrib
    y_off = jnp.concatenate(y_off_parts, axis=-1) * ecs_e
    y = y_intra + y_off + xs * dskip_ref[...]
    y_ssd = _gated_group_norm(y, z, gssd_ref[...])

    HD = ATT_HEAD_DIM
    ncol = ATT_HEADS * R
    nrow = bt * WINDOW * ATT_KV_HEADS
    qs = jnp.concatenate([q[:, hd * HD:(hd + 1) * HD] for hd in range(ATT_HEADS)], axis=0)
    kall = ck_ref[...].reshape(nrow, HD)
    vall = cv_ref[...].reshape(nrow, HD)
    kn = jnp.concatenate([k_new[:, 0:HD], k_new[:, HD:2 * HD]], axis=0)
    vn = jnp.concatenate([v_new[:, 0:HD], v_new[:, HD:2 * HD]], axis=0)
    z_c = _dot_nt(kall, qs)
    z_n = _dot_nt(kn, qs)
    c_i = lax.broadcasted_iota(jnp.int32, (1, ncol), 1)
    g_q = c_i // (R * (ATT_HEADS // ATT_KV_HEADS))
    t_q = (c_i % R) // bt
    b_q = c_i % bt
    r_i = lax.broadcasted_iota(jnp.int32, (nrow, 1), 0)
    keep_c = ((r_i // (WINDOW * ATT_KV_HEADS) == b_q) & (r_i % ATT_KV_HEADS == g_q)
              & ((r_i % (WINDOW * ATT_KV_HEADS)) // ATT_KV_HEADS > t_q))
    n_i = lax.broadcasted_iota(jnp.int32, (ATT_KV_HEADS * R, 1), 0)
    keep_n = (n_i % bt == b_q) & (n_i // R == g_q) & ((n_i % R) // bt <= t_q)
    z_c = jnp.where(keep_c, z_c, -jnp.inf)
    z_n = jnp.where(keep_n, z_n, -jnp.inf)
    sink = sinkrow_ref[...]
    m = jnp.maximum(jnp.maximum(jnp.max(z_c, axis=0, keepdims=True),
                                jnp.max(z_n, axis=0, keepdims=True)), sink)
    e_c = jnp.exp(z_c - m)
    e_n = jnp.exp(z_n - m)
    den = (jnp.sum(e_c, axis=0, keepdims=True) + jnp.sum(e_n, axis=0, keepdims=True)
           + jnp.exp(sink - m))
    rinv = 1.0 / den
    tn = (((0,), (0,)), ((), ()))
    o = (lax.dot_general((e_c * rinv).astype(bf16), vall.astype(bf16), tn, preferred_element_type=f32)
         + lax.dot_general((e_n * rinv).astype(bf16), vn.astype(bf16), tn, preferred_element_type=f32))

    mix = jnp.dot(y_ssd.astype(bf16), wout_ref[0:SSD_WIDTH, :], preferred_element_type=f32)
    for hd in range(ATT_HEADS):
        mix = mix + jnp.dot(o[hd * R:(hd + 1) * R].astype(bf16),
                            wout_ref[SSD_WIDTH + hd * HD:SSD_WIDTH + (hd + 1) * HD, :],
                            preferred_element_type=f32)
    y_ref[...] = (x + _rmsnorm(mix, gpost_ref[...])).reshape(steps, bt, D_MODEL)
    knew_ref[...] = k_new.reshape(steps, bt, ATT_KV_WIDTH)
    vnew_ref[...] = v_new.reshape(steps, bt, ATT_KV_WIDTH)


def _sample_mixer(x_tm, cprev_tm, st, ck, cv, gpre, win, convw, convb, dtb, alog, dskip_e, gssd,
                  sinkrow, wout, gpost, bt):
    steps, nb, D = x_tm.shape
    kern = functools.partial(_sample_mixer_kernel, bt=bt, steps=steps)
    tm = lambda w: pl.BlockSpec((steps, bt, w), lambda i: (0, i, 0))
    in_specs = [
        tm(D),
        pl.BlockSpec((SSD_CONV - 1, bt, SSD_CONV_DIM), lambda i: (0, i, 0)),
        pl.BlockSpec((bt, SSD_WIDTH, SSD_STATE), lambda i: (i, 0, 0)),
        pl.BlockSpec((bt, WINDOW, ATT_KV_HEADS, ATT_HEAD_DIM), lambda i: (i, 0, 0, 0)),
        pl.BlockSpec((bt, WINDOW, ATT_KV_HEADS, ATT_HEAD_DIM), lambda i: (i, 0, 0, 0)),
    ] + [_const_spec(a.shape) for a in (gpre, win, convw, convb, dtb, alog, dskip_e, gssd,
                                        sinkrow, wout, gpost)]
    out_specs = (
        tm(D),
        pl.BlockSpec((bt, SSD_WIDTH, SSD_STATE), lambda i: (i, 0, 0)),
        pl.BlockSpec((SSD_CONV - 1, bt, SSD_CONV_DIM), lambda i: (0, i, 0)),
        tm(ATT_KV_WIDTH), tm(ATT_KV_WIDTH),
    )
    out_shape = (
        jax.ShapeDtypeStruct((steps, nb, D), f32),
        jax.ShapeDtypeStruct((nb, SSD_WIDTH, SSD_STATE), f32),
        jax.ShapeDtypeStruct((SSD_CONV - 1, nb, SSD_CONV_DIM), f32),
        jax.ShapeDtypeStruct((steps, nb, ATT_KV_WIDTH), f32),
        jax.ShapeDtypeStruct((steps, nb, ATT_KV_WIDTH), f32),
    )
    return pl.pallas_call(
        kern, grid=(nb // bt,), in_specs=in_specs, out_specs=out_specs, out_shape=out_shape,
        name="sample_mixer",
        compiler_params=pltpu.CompilerParams(
            dimension_semantics=("arbitrary",), vmem_limit_bytes=VMEM_LIMIT),
    )(x_tm, cprev_tm, st, ck, cv, gpre, win, convw, convb, dtb, alog, dskip_e, gssd, sinkrow,
      wout, gpost)


def _sample_xattn_kernel(x_ref, mk_ref, mv_ref, gpre_ref, wq_ref, wo_ref, gpost_ref, y_ref,
                         *, bt, steps):
    R = steps * bt
    nrow = bt * N_MEM * X_HEADS
    x = x_ref[...].reshape(R, D_MODEL)
    hn = _rmsnorm(x, gpre_ref[...]).astype(bf16)
    q = jnp.dot(hn, wq_ref[...], preferred_element_type=f32)
    qs = jnp.concatenate([q[:, hd * X_HEAD_DIM:(hd + 1) * X_HEAD_DIM] for hd in range(X_HEADS)],
                         axis=0)
    kall = mk_ref[...].reshape(nrow, X_HEAD_DIM)
    vall = mv_ref[...].reshape(nrow, X_HEAD_DIM)
    z = _dot_nt(kall, qs) * X_SCALE
    ncol = X_HEADS * R
    z = z.reshape(bt, N_MEM * X_HEADS // SUBLANES, SUBLANES, ncol)
    mshape = (bt, 1, SUBLANES, ncol)
    b_i = lax.broadcasted_iota(jnp.int32, mshape, 0)
    r_i = lax.broadcasted_iota(jnp.int32, mshape, 2)
    c_i = lax.broadcasted_iota(jnp.int32, mshape, 3)
    keep = (b_i == (c_i % bt)) & ((r_i % X_HEADS) == (c_i // R))
    z = jnp.where(keep, z, -jnp.inf).reshape(nrow, ncol)
    m = jnp.max(z, axis=0, keepdims=True)
    e = jnp.exp(z - m)
    p = e * (1.0 / jnp.sum(e, axis=0, keepdims=True))
    o = lax.dot_general(p.astype(bf16), vall.astype(bf16), (((0,), (0,)), ((), ())),
                        preferred_element_type=f32)
    o = jnp.concatenate([o[hd * R:(hd + 1) * R] for hd in range(X_HEADS)], axis=-1)
    cc = _dot(o, wo_ref[...])
    y_ref[...] = (x + _rmsnorm(cc, gpost_ref[...])).reshape(steps, bt, D_MODEL)


def _sample_xattn(x_tm, mk, mv, gpre, wq, wo, gpost, bt):
    steps, nb, D = x_tm.shape
    kern = functools.partial(_sample_xattn_kernel, bt=bt, steps=steps)
    xs = pl.BlockSpec((steps, bt, D), lambda i: (0, i, 0))
    ms = pl.BlockSpec((bt, N_MEM, X_HEADS, X_HEAD_DIM), lambda i: (i, 0, 0, 0))
    return pl.pallas_call(
        kern, grid=(nb // bt,),
        in_specs=[xs, ms, ms, _const_spec(gpre.shape), _const_spec(wq.shape),
                  _const_spec(wo.shape), _const_spec(gpost.shape)],
        out_specs=xs, out_shape=jax.ShapeDtypeStruct((steps, nb, D), f32),
        name="sample_xattn",
        compiler_params=pltpu.CompilerParams(
            dimension_semantics=("arbitrary",), vmem_limit_bytes=VMEM_LIMIT),
    )(x_tm, mk, mv, gpre, wq, wo, gpost)


def _window_update_kernel(ck_ref, cv_ref, nk_ref, nv_ref, ok_ref, ov_ref, sem):
    steps = nk_ref.shape[1]
    keep = WINDOW - steps
    copies = [
        pltpu.make_async_copy(ck_ref.at[:, pl.ds(steps, keep)], ok_ref.at[:, pl.ds(0, keep)], sem.at[0]),
        pltpu.make_async_copy(nk_ref, ok_ref.at[:, pl.ds(keep, steps)], sem.at[1]),
        pltpu.make_async_copy(cv_ref.at[:, pl.ds(steps, keep)], ov_ref.at[:, pl.ds(0, keep)], sem.at[2]),
        pltpu.make_async_copy(nv_ref, ov_ref.at[:, pl.ds(keep, steps)], sem.at[3]),
    ]
    for cp in copies:
        cp.start()
    for cp in copies:
        cp.wait()


def _window_update(ck, cv, nk, nv):
    hbm = pl.BlockSpec(memory_space=pl.ANY)
    out = jax.ShapeDtypeStruct(ck.shape, ck.dtype)
    return pl.pallas_call(
        _window_update_kernel, in_specs=[hbm] * 4, out_specs=(hbm, hbm), out_shape=(out, out),
        scratch_shapes=[pltpu.SemaphoreType.DMA((4,))], name="window_update",
    )(ck, cv, nk, nv)


def _row(v, width=None):
    v = v.reshape(1, -1).astype(f32)
    if width is not None and v.shape[1] < width:
        v = jnp.pad(v, ((0, 0), (0, width - v.shape[1])))
    return v


def kernel(x_prompt, x_sample, state_ssm, state_conv, cache_win_k, cache_win_v, cache_mem_k, cache_mem_v, mem_prompt, g_mix_pre, w_in, conv_w, conv_b, dt_bias, a_log, d_skip, g_ssd_norm, sinks, w_out, g_mix_post, g_x_pre, w_xq, g_mem, w_xk, w_xv, w_xo, g_x_post, g_ffn_pre, w_gate, w_up, w_down, g_ffn_post):
    depth = w_in.shape[0]
    assert depth == 1
    B, L, D = x_prompt.shape
    NB, steps, _ = x_sample.shape
    li = 0

    w = w_in[li]
    wz, wxbc, wdt = w[:, 0:512], w[:, 512:1536], w[:, 1536:1544]
    wq, wk, wv = w[:, 1544:2056], w[:, 2056:2184], w[:, 2184:2312]
    wdt_p = jnp.pad(wdt, ((0, 0), (0, LANES - SSD_HEADS)))
    win_p = jnp.concatenate([wz, wxbc, wq * ATT_SCALE, wk, wv, wdt_p], axis=1).astype(bf16)
    wo_b = w_out[li].astype(bf16)

    gpre, gpost = _row(g_mix_pre[li]), _row(g_mix_post[li])
    convw, convb = conv_w[li].astype(f32), _row(conv_b[li])
    dtb, alog = _row(dt_bias[li], LANES), _row(a_log[li], LANES)
    dskip_e = _row(jnp.repeat(d_skip[li], SSD_HEAD_DIM))
    gssd = _row(g_ssd_norm[li])
    sk = sinks[li].astype(f32)

    mk2d, mv2d = _memkv(mem_prompt.reshape(B * N_MEM, D), _row(g_mem[li]),
                        w_xk[li].astype(bf16), w_xv[li].astype(bf16), tile=512)
    mk3, mv3 = mk2d.reshape(B, N_MEM, D), mv2d.reshape(B, N_MEM, D)
    x1, p_ssm, p_conv8, p_wk, p_wv = _prompt_mixer(
        x_prompt, gpre, win_p, convw, convb, dtb, alog, dskip_e, gssd, sk, wo_b, gpost, tile=512)
    wxq_b, wxo_b = w_xq[li].astype(bf16), w_xo[li].astype(bf16)
    gxpre, gxpost = _row(g_x_pre[li]), _row(g_x_post[li])
    x2 = _prompt_xattn(x1, gxpre, wxq_b, mk3, mv3, wxo_b, gxpost, tile=512)
    wg_b, wu_b, wd_b = w_gate[li].astype(bf16), w_up[li].astype(bf16), w_down[li].astype(bf16)
    gfpre, gfpost = _row(g_ffn_pre[li]), _row(g_ffn_post[li])
    yp = _ffn(x2.reshape(B * L, D), gfpre, wg_b, wu_b, wd_b, gfpost, tile=512).reshape(B, L, D)

    bt = 8
    x_tm = jnp.transpose(x_sample, (1, 0, 2))
    cprev_tm = jnp.transpose(state_conv[li], (1, 0, 2))
    st = state_ssm[li].reshape(NB, SSD_WIDTH, SSD_STATE)
    kv_shape = (ATT_KV_HEADS, ATT_HEAD_DIM)
    ck = cache_win_k.reshape(NB, WINDOW, *kv_shape)
    cv = cache_win_v.reshape(NB, WINDOW, *kv_shape)
    sinkrow = jnp.repeat(sk, steps * bt).reshape(1, ATT_HEADS * steps * bt)
    x1s, s_ssm, cnew_tm, knew_tm, vnew_tm = _sample_mixer(
        x_tm, cprev_tm, st, ck, cv, gpre, win_p, convw, convb, dtb, alog, dskip_e, gssd,
        sinkrow, wo_b, gpost, bt=bt)
    cmk = cache_mem_k.reshape(NB, N_MEM, X_HEADS, X_HEAD_DIM)
    cmv = cache_mem_v.reshape(NB, N_MEM, X_HEADS, X_HEAD_DIM)
    x2s = _sample_xattn(x1s, cmk, cmv, gxpre, wxq_b, wxo_b, gxpost, bt=bt)
    ys_tm = _ffn(x2s.reshape(steps * NB, D), gfpre, wg_b, wu_b, wd_b, gfpost, tile=steps * NB)
    ys = jnp.transpose(ys_tm.reshape(steps, NB, D), (1, 0, 2))

    s_conv = jnp.transpose(cnew_tm, (1, 0, 2))
    knew = jnp.transpose(knew_tm, (1, 0, 2)).reshape(NB, steps, *kv_shape)
    vnew = jnp.transpose(vnew_tm, (1, 0, 2)).reshape(NB, steps, *kv_shape)
    s_wk, s_wv = _window_update(ck, cv, knew, vnew)
    return (
        yp, ys,
        p_ssm.reshape(1, B, SSD_HEADS, SSD_HEAD_DIM, SSD_STATE),
        p_conv8[:, SUBLANES - (SSD_CONV - 1):, :][None],
        p_wk.reshape(1, B, WINDOW, *kv_shape), p_wv.reshape(1, B, WINDOW, *kv_shape),
        mk3.reshape(1, B, N_MEM, X_HEADS, X_HEAD_DIM), mv3.reshape(1, B, N_MEM, X_HEADS, X_HEAD_DIM),
        s_ssm.reshape(1, NB, SSD_HEADS, SSD_HEAD_DIM, SSD_STATE),
        s_conv[None],
        s_wk[None], s_wv[None],
    )
```

```python
import functools

import jax
import jax.numpy as jnp
from jax import lax
from jax.experimental import pallas as pl
from jax.experimental.pallas import tpu as pltpu

f32 = jnp.float32
bf16 = jnp.bfloat16

D_MODEL = 1024
EPS = 1e-6
N_MEM = 256
SSD_HEADS = 8
SSD_HEAD_DIM = 64
SSD_WIDTH = 512
SSD_GROUPS = 2
SSD_STATE = 128
SSD_CONV = 4
SSD_CHUNK = 128
SSD_CONV_DIM = 1024
ATT_HEADS = 8
ATT_KV_HEADS = 2
ATT_HEAD_DIM = 64
ATT_WIDTH = 512
ATT_KV_WIDTH = 128
WINDOW = 128
ATT_SCALE = ATT_HEAD_DIM ** -0.5
X_HEADS = 4
X_HEAD_DIM = 256
X_SCALE = X_HEAD_DIM ** -0.5
D_FF = 2816
LANES = 128
SUBLANES = 8
VMEM_LIMIT = 56 * 1024 * 1024

P_Z, P_XBC, P_Q, P_K, P_V, P_DT, P_END = 0, 512, 1536, 2048, 2176, 2304, 2432


def _dot(a, b):
    return jnp.dot(a.astype(bf16), b.astype(bf16), preferred_element_type=f32)


def _dot_nt(a, b):
    return lax.dot_general(a.astype(bf16), b.astype(bf16), (((1,), (1,)), ((), ())),
                           preferred_element_type=f32)


def _split2(x):
    hi = x.astype(bf16)
    lo = (x - hi.astype(f32)).astype(bf16)
    return hi, lo


def _dot_x2(x, m):
    hi, lo = _split2(x)
    return (jnp.dot(hi, m, preferred_element_type=f32)
            + jnp.dot(lo, m, preferred_element_type=f32))


def _rmsnorm(x, g):
    ms = jnp.mean(x * x, axis=-1, keepdims=True)
    return x * lax.rsqrt(ms + EPS) * g


def _silu(x):
    return x * jax.nn.sigmoid(x)


def _softplus(x):
    return jnp.maximum(x, 0.0) + jnp.log1p(jnp.exp(-jnp.abs(x)))


def _head_expand_matrix():
    r = lax.broadcasted_iota(jnp.int32, (LANES, SSD_WIDTH), 0)
    c = lax.broadcasted_iota(jnp.int32, (LANES, SSD_WIDTH), 1)
    return (r == (c >> 6)).astype(bf16)


def _gated_group_norm(y, z, g):
    u = y * _silu(z)
    half = SSD_WIDTH // SSD_GROUPS
    parts = []
    for gi in range(SSD_GROUPS):
        ug = u[:, gi * half:(gi + 1) * half]
        parts.append(ug * lax.rsqrt(jnp.mean(ug * ug, axis=-1, keepdims=True) + EPS))
    return jnp.concatenate(parts, axis=-1) * g


def _prompt_mixer_kernel(x_ref, gpre_ref, win_ref, convw_ref, convb_ref, dtb_ref, alog_ref,
                         dskip_ref, gssd_ref, sinks_ref, wout_ref, gpost_ref,
                         y_ref, ssm_ref, conv_ref, wk_ref, wv_ref,
                         statet_sc, xbc_ext_sc, xbc_sc, z_sc, q_sc, k_sc, v_sc, dt_sc,
                         kprev_sc, vprev_sc, mix_sc, *, tile):
    i = pl.program_id(1)
    nchunk = tile // SSD_CHUNK
    C = SSD_CHUNK

    @pl.when(i == 0)
    def _init():
        statet_sc[...] = jnp.zeros_like(statet_sc)
        xbc_ext_sc[0:SUBLANES, :] = jnp.zeros((SUBLANES, SSD_CONV_DIM), f32)
        kprev_sc[...] = jnp.zeros_like(kprev_sc)
        vprev_sc[...] = jnp.zeros_like(vprev_sc)

    x = x_ref[0]
    h = _rmsnorm(x, gpre_ref[...]).astype(bf16)
    z_sc[...] = jnp.dot(h, win_ref[:, P_Z:P_XBC], preferred_element_type=f32)
    xbc_ext_sc[SUBLANES:SUBLANES + tile, :] = jnp.dot(h, win_ref[:, P_XBC:P_Q],
                                                      preferred_element_type=f32)
    q_sc[...] = jnp.dot(h, win_ref[:, P_Q:P_K], preferred_element_type=f32)
    k_sc[...] = jnp.dot(h, win_ref[:, P_K:P_V], preferred_element_type=f32)
    v_sc[...] = jnp.dot(h, win_ref[:, P_V:P_DT], preferred_element_type=f32)
    dt_raw = jnp.dot(h, win_ref[:, P_DT:P_END], preferred_element_type=f32)
    lane = lax.broadcasted_iota(jnp.int32, (1, LANES), 1)
    dt_sc[...] = jnp.where(lane < SSD_HEADS, _softplus(dt_raw + dtb_ref[...]), 0.0)

    acc = convb_ref[...]
    for j in range(SSD_CONV):
        off = SUBLANES - (SSD_CONV - 1) + j
        acc = acc + xbc_ext_sc[off:off + tile, :] * convw_ref[j:j + 1, :]
    xbc_sc[...] = _silu(acc)
    tail = xbc_ext_sc[tile:tile + SUBLANES, :]
    conv_ref[0] = tail
    xbc_ext_sc[0:SUBLANES, :] = tail

    a_row = -jnp.exp(alog_ref[...])
    expand = _head_expand_matrix()
    expand2 = jnp.concatenate([expand, expand], axis=0)
    row_i = lax.broadcasted_iota(jnp.int32, (C, C), 0)
    col_i = lax.broadcasted_iota(jnp.int32, (C, C), 1)
    lower = col_i <= row_i
    tri = lower.astype(bf16)
    tri3 = jnp.concatenate([tri, tri, tri], axis=1)
    lo_half = lane < ATT_HEAD_DIM
    half = SSD_WIDTH // SSD_GROUPS
    head_of_lane = lax.broadcasted_iota(jnp.int32, (1, half), 1) >> 6

    def chunk(c, carry):
        r0 = pl.multiple_of(c * C, C)
        rows = pl.ds(r0, C)
        xs = xbc_sc[rows, 0:SSD_WIDTH]
        bm = xbc_sc[rows, SSD_WIDTH:SSD_WIDTH + half]
        cm = xbc_sc[rows, SSD_WIDTH + half:SSD_CONV_DIM]
        dtc = dt_sc[rows, :]
        adt = dtc * a_row
        a_hi = adt.astype(bf16)
        a_r1 = adt - a_hi.astype(f32)
        a_mid = a_r1.astype(bf16)
        a_lo = (a_r1 - a_mid.astype(f32)).astype(bf16)
        cs = jnp.dot(tri3, jnp.concatenate([a_hi, a_mid, a_lo], axis=0), preferred_element_type=f32)
        cs_t = cs.T
        tot = cs[C - 1:C, :]
        fac = jnp.concatenate([dtc, jnp.exp(cs), jnp.exp(tot - cs)], axis=0)
        f_hi = fac.astype(bf16)
        f_lo = (fac - f_hi.astype(f32)).astype(bf16)
        fac_e = jnp.dot(jnp.concatenate([f_hi, f_lo], axis=1), expand2,
                        preferred_element_type=f32)
        xdt = xs * fac_e[0:C]
        ecs_e = fac_e[C:2 * C]
        w_end = xdt * fac_e[2 * C:3 * C]
        cb_all = _dot_nt(jnp.concatenate([cm[:, 0:SSD_STATE], cm[:, SSD_STATE:half]], axis=0),
                         jnp.concatenate([bm[:, 0:SSD_STATE], bm[:, SSD_STATE:half]], axis=0))
        bm_t = bm.T
        xdt_b = xdt.astype(bf16)
        y_parts = []
        for g in range(SSD_GROUPS):
            gl = slice(g * half, (g + 1) * half)
            cg = cm[:, g * SSD_STATE:(g + 1) * SSD_STATE]
            cb = cb_all[g * C:(g + 1) * C, g * C:(g + 1) * C]
            st = statet_sc[g]
            y_off = _dot(cg, st) * ecs_e[:, gl]
            m_parts, x_parts = [], []
            for r in range(SSD_HEADS // SSD_GROUPS):
                hh = g * (SSD_HEADS // SSD_GROUPS) + r
                diff = cs[:, hh:hh + 1] - cs_t[hh:hh + 1, :]
                m_parts.append((cb * jnp.exp(jnp.where(lower, diff, -jnp.inf))).astype(bf16))
                x_parts.append(jnp.where(head_of_lane == r, xdt_b[:, gl], jnp.zeros((), bf16)))
            y_d = jnp.dot(jnp.concatenate(m_parts, axis=1), jnp.concatenate(x_parts, axis=0),
                          preferred_element_type=f32)
            y_parts.append(y_d + y_off)
            statet_sc[g] = (st * ecs_e[C - 1:C, gl]
                            + _dot(bm_t[g * SSD_STATE:(g + 1) * SSD_STATE, :], w_end[:, gl]))
        y = jnp.concatenate(y_parts, axis=-1) + xs * dskip_ref[...]
        y_ssd = _gated_group_norm(y, z_sc[rows, :], gssd_ref[...])
        mix_sc[rows, 0:SSD_WIDTH] = y_ssd.astype(bf16)

        q = q_sc[rows, :].astype(bf16)
        k = k_sc[rows, :]
        v = v_sc[rows, :]
        kk = jnp.concatenate([kprev_sc[...], k], axis=0)
        vv = jnp.concatenate([vprev_sc[...], v], axis=0)
        kk_r = pltpu.roll(kk, ATT_HEAD_DIM, 1)
        vv_r = pltpu.roll(vv, ATT_HEAD_DIM, 1)
        k_lo = [jnp.where(lo_half, kk, 0.0).astype(bf16), jnp.where(lo_half, kk_r, 0.0).astype(bf16)]
        k_hi = [jnp.where(lo_half, 0.0, kk_r).astype(bf16), jnp.where(lo_half, 0.0, kk).astype(bf16)]
        v_lo = [jnp.where(lo_half, vv, 0.0).astype(bf16), jnp.where(lo_half, vv_r, 0.0).astype(bf16)]
        v_hi = [jnp.where(lo_half, 0.0, vv_r).astype(bf16), jnp.where(lo_half, 0.0, vv).astype(bf16)]
        prev_bias = jnp.where((i * nchunk + c) > 0, 0.0, -jnp.inf)
        for g in range(ATT_KV_HEADS):
            qg = jnp.concatenate([q[:, (2 * g) * LANES:(2 * g + 1) * LANES],
                                  q[:, (2 * g + 1) * LANES:(2 * g + 2) * LANES]], axis=0)
            s = _dot_nt(qg, jnp.concatenate([k_lo[g], k_hi[g]], axis=0))
            p_rows = []
            for jb in range(2):
                p_cols = []
                for sub in range(2):
                    sink = sinks_ref[(2 * g + jb) * 2 + sub]
                    s_prev = s[jb * C:(jb + 1) * C, (2 * sub) * C:(2 * sub + 1) * C]
                    s_cur = s[jb * C:(jb + 1) * C, (2 * sub + 1) * C:(2 * sub + 2) * C]
                    sc = jnp.where(lower, s_cur, s_prev + prev_bias)
                    m = jnp.maximum(jnp.max(sc, axis=-1, keepdims=True), sink)
                    e = jnp.exp(sc - m)
                    den = jnp.sum(e, axis=-1, keepdims=True) + jnp.exp(sink - m)
                    p = (e * (1.0 / den)).astype(bf16)
                    zero = jnp.zeros((), bf16)
                    p_cols += [jnp.where(lower, zero, p), jnp.where(lower, p, zero)]
                p_rows.append(jnp.concatenate(p_cols, axis=1))
            o = jnp.dot(jnp.concatenate(p_rows, axis=0),
                        jnp.concatenate([v_lo[g], v_hi[g]], axis=0),
                        preferred_element_type=f32)
            for jb in range(2):
                lo_l = SSD_WIDTH + (2 * g + jb) * LANES
                mix_sc[rows, lo_l:lo_l + LANES] = o[jb * C:(jb + 1) * C].astype(bf16)
        kprev_sc[...] = k
        vprev_sc[...] = v
        return carry

    lax.fori_loop(0, nchunk, chunk, 0)

    @pl.when(i == pl.num_programs(1) - 1)
    def _final_state():
        for g in range(SSD_GROUPS):
            ssm_ref[0, g * half:(g + 1) * half, :] = statet_sc[g].T
        wk_ref[0] = k_sc[tile - WINDOW:tile, :].T
        wv_ref[0] = v_sc[tile - WINDOW:tile, :].T

    mix = jnp.dot(mix_sc[...], wout_ref[...], preferred_element_type=f32)
    y_ref[0] = x + _rmsnorm(mix, gpost_ref[...])


def _const_spec(shape):
    nd = len(shape)
    return pl.BlockSpec(shape, lambda *_: (0,) * nd)


def _prompt_mixer(x, gpre, win, convw, convb, dtb, alog, dskip_e, gssd, sinks, wout, gpost, tile):
    B, L, D = x.shape
    grid = (B, L // tile)
    kern = functools.partial(_prompt_mixer_kernel, tile=tile)
    out_shape = (
        jax.ShapeDtypeStruct((B, L, D), f32),
        jax.ShapeDtypeStruct((B, SSD_WIDTH, SSD_STATE), f32),
        jax.ShapeDtypeStruct((B, SUBLANES, SSD_CONV_DIM), f32),
        jax.ShapeDtypeStruct((B, WINDOW, ATT_KV_WIDTH), f32),
        jax.ShapeDtypeStruct((B, WINDOW, ATT_KV_WIDTH), f32),
    )
    in_specs = [
        pl.BlockSpec((1, tile, D), lambda b, i: (b, i, 0)),
        _const_spec(gpre.shape), _const_spec(win.shape), _const_spec(convw.shape),
        _const_spec(convb.shape), _const_spec(dtb.shape), _const_spec(alog.shape),
        _const_spec(dskip_e.shape), _const_spec(gssd.shape),
        pl.BlockSpec(memory_space=pltpu.SMEM),
        _const_spec(wout.shape), _const_spec(gpost.shape),
    ]
    out_specs = (
        pl.BlockSpec((1, tile, D), lambda b, i: (b, i, 0)),
        pl.BlockSpec((1, SSD_WIDTH, SSD_STATE), lambda b, i: (b, 0, 0)),
        pl.BlockSpec((1, SUBLANES, SSD_CONV_DIM), lambda b, i: (b, 0, 0)),
        pl.BlockSpec((1, WINDOW, ATT_KV_WIDTH), lambda b, i: (b, 0, 0)),
        pl.BlockSpec((1, WINDOW, ATT_KV_WIDTH), lambda b, i: (b, 0, 0)),
    )
    scratch = [
        pltpu.VMEM((SSD_GROUPS, SSD_STATE, SSD_WIDTH // SSD_GROUPS), f32),
        pltpu.VMEM((tile + 2 * SUBLANES, SSD_CONV_DIM), f32),
        pltpu.VMEM((tile, SSD_CONV_DIM), f32),
        pltpu.VMEM((tile, SSD_WIDTH), f32),
        pltpu.VMEM((tile, ATT_WIDTH), f32),
        pltpu.VMEM((tile, ATT_KV_WIDTH), f32),
        pltpu.VMEM((tile, ATT_KV_WIDTH), f32),
        pltpu.VMEM((tile, LANES), f32),
        pltpu.VMEM((WINDOW, ATT_KV_WIDTH), f32),
        pltpu.VMEM((WINDOW, ATT_KV_WIDTH), f32),
        pltpu.VMEM((tile, 2 * SSD_WIDTH), bf16),
    ]
    return pl.pallas_call(
        kern, grid=grid, in_specs=in_specs, out_specs=out_specs, out_shape=out_shape,
        scratch_shapes=scratch, name="prompt_mixer",
        compiler_params=pltpu.CompilerParams(
            dimension_semantics=("arbitrary", "arbitrary"), vmem_limit_bytes=VMEM_LIMIT),
    )(x, gpre, win, convw, convb, dtb, alog, dskip_e, gssd, sinks, wout, gpost)


def _memkv_kernel(m_ref, g_ref, wk_ref, wv_ref, k_ref, v_ref):
    mn = _rmsnorm(m_ref[...], g_ref[...]).astype(bf16)
    k_ref[...] = jnp.dot(mn, wk_ref[...], preferred_element_type=f32)
    v_ref[...] = jnp.dot(mn, wv_ref[...], preferred_element_type=f32)


def _memkv(mem2d, g, wk, wv, tile):
    n, d = mem2d.shape
    row = pl.BlockSpec((tile, d), lambda i: (i, 0))
    return pl.pallas_call(
        _memkv_kernel, grid=(n // tile,),
        in_specs=[row, _const_spec(g.shape), _const_spec(wk.shape), _const_spec(wv.shape)],
        out_specs=(row, row),
        out_shape=(jax.ShapeDtypeStruct((n, d), f32), jax.ShapeDtypeStruct((n, d), f32)),
        name="memory_kv",
        compiler_params=pltpu.CompilerParams(
            dimension_semantics=("arbitrary",), vmem_limit_bytes=VMEM_LIMIT),
    )(mem2d, g, wk, wv)


def _prompt_xattn_kernel(x_ref, gpre_ref, wq_ref, mk_ref, mv_ref, wo_ref, gpost_ref, y_ref):
    x = x_ref[0]
    hn = _rmsnorm(x, gpre_ref[...]).astype(bf16)
    q = jnp.dot(hn, wq_ref[...], preferred_element_type=f32)
    outs = []
    for hd in range(X_HEADS):
        sl = slice(hd * X_HEAD_DIM, (hd + 1) * X_HEAD_DIM)
        s = _dot_nt(q[:, sl], mk_ref[0, :, sl]) * X_SCALE
        m = jnp.max(s, axis=-1, keepdims=True)
        e = jnp.exp(s - m)
        p = e * (1.0 / jnp.sum(e, axis=-1, keepdims=True))
        outs.append(_dot(p, mv_ref[0, :, sl]))
    o = jnp.concatenate(outs, axis=-1)
    c = _dot(o, wo_ref[...])
    y_ref[0] = x + _rmsnorm(c, gpost_ref[...])


def _prompt_xattn(x, gpre, wq, mk, mv, wo, gpost, tile):
    B, L, D = x.shape
    xs = pl.BlockSpec((1, tile, D), lambda b, i: (b, i, 0))
    ms = pl.BlockSpec((1, N_MEM, D), lambda b, i: (b, 0, 0))
    return pl.pallas_call(
        _prompt_xattn_kernel, grid=(B, L // tile),
        in_specs=[xs, _const_spec(gpre.shape), _const_spec(wq.shape), ms, ms,
                  _const_spec(wo.shape), _const_spec(gpost.shape)],
        out_specs=xs, out_shape=jax.ShapeDtypeStruct((B, L, D), f32),
        name="prompt_xattn",
        compiler_params=pltpu.CompilerParams(
            dimension_semantics=("arbitrary", "arbitrary"), vmem_limit_bytes=VMEM_LIMIT),
    )(x, gpre, wq, mk, mv, wo, gpost)


def _ffn_kernel(x_ref, gpre_ref, wg_ref, wu_ref, wd_ref, gpost_ref, y_ref):
    x = x_ref[...]
    hf = _rmsnorm(x, gpre_ref[...]).astype(bf16)
    gate = jnp.dot(hf, wg_ref[...], preferred_element_type=f32)
    up = jnp.dot(hf, wu_ref[...], preferred_element_type=f32)
    act = (_silu(gate) * up).astype(bf16)
    f = jnp.dot(act, wd_ref[...], preferred_element_type=f32)
    y_ref[...] = x + _rmsnorm(f, gpost_ref[...])


def _ffn(x2d, gpre, wg, wu, wd, gpost, tile):
    n, d = x2d.shape
    row = pl.BlockSpec((tile, d), lambda i: (i, 0))
    return pl.pallas_call(
        _ffn_kernel, grid=(n // tile,),
        in_specs=[row, _const_spec(gpre.shape), _const_spec(wg.shape), _const_spec(wu.shape),
                  _const_spec(wd.shape), _const_spec(gpost.shape)],
        out_specs=row, out_shape=jax.ShapeDtypeStruct((n, d), f32),
        name="ffn",
        compiler_params=pltpu.CompilerParams(
            dimension_semantics=("arbitrary",), vmem_limit_bytes=VMEM_LIMIT),
    )(x2d, gpre, wg, wu, wd, gpost)


def _pad_rows(a, rows):
    if a.shape[0] == rows:
        return a
    return jnp.concatenate([a, jnp.zeros((rows - a.shape[0], a.shape[1]), a.dtype)], axis=0)


def _sample_mixer_kernel(x_ref, cprev_ref, st_ref, ck_ref, cv_ref,
                         gpre_ref, win_ref, convw_ref, convb_ref, dtb_ref, alog_ref,
                         dskip_ref, gssd_ref, sinkcol_ref, wout_ref, gpost_ref,
                         y_ref, ssm_ref, cnew_ref, wk_ref, wv_ref, *, bt, steps):
    R = steps * bt
    half = SSD_WIDTH // SSD_GROUPS
    x = x_ref[...].reshape(R, D_MODEL)
    h = _rmsnorm(x, gpre_ref[...]).astype(bf16)
    z = jnp.dot(h, win_ref[:, P_Z:P_XBC], preferred_element_type=f32)
    u = jnp.dot(h, win_ref[:, P_XBC:P_Q], preferred_element_type=f32)
    q = jnp.dot(h, win_ref[:, P_Q:P_K], preferred_element_type=f32)
    k_new = jnp.dot(h, win_ref[:, P_K:P_V], preferred_element_type=f32)
    v_new = jnp.dot(h, win_ref[:, P_V:P_DT], preferred_element_type=f32)
    dt_raw = jnp.dot(h, win_ref[:, P_DT:P_END], preferred_element_type=f32)
    lane = lax.broadcasted_iota(jnp.int32, (1, LANES), 1)
    dt = jnp.where(lane < SSD_HEADS, _softplus(dt_raw + dtb_ref[...]), 0.0)

    def slab(a, t):
        return a[t * bt:(t + 1) * bt]

    hist = [cprev_ref[j] for j in range(SSD_CONV - 1)] + [slab(u, t) for t in range(steps)]
    xbc_t = []
    for t in range(steps):
        acc = convb_ref[...]
        for j in range(SSD_CONV):
            acc = acc + hist[t + j] * convw_ref[j:j + 1, :]
        xbc_t.append(_silu(acc))
    for j in range(SSD_CONV - 1):
        cnew_ref[j] = hist[steps + j]
    xbc = jnp.concatenate(xbc_t, axis=0)
    xs = xbc[:, 0:SSD_WIDTH]
    bm = xbc[:, SSD_WIDTH:SSD_WIDTH + half]
    cm = xbc[:, SSD_WIDTH + half:SSD_CONV_DIM]

    a_row = -jnp.exp(alog_ref[...])
    adt = dt * a_row
    cs_t = [slab(adt, 0)]
    for t in range(1, steps):
        cs_t.append(cs_t[-1] + slab(adt, t))
    cs = jnp.concatenate(cs_t, axis=0)
    tot = cs_t[-1]
    tot_rows = jnp.concatenate([tot] * steps, axis=0)
    expand = _head_expand_matrix()
    xdt = xs * _dot_x2(dt, expand)
    ecs_e = _dot_x2(jnp.exp(cs), expand)
    w_end = xdt * _dot_x2(jnp.exp(tot_rows - cs), expand)
    dec_e = _dot_x2(jnp.exp(tot), expand)

    gr = lax.broadcasted_iota(jnp.int32, (half, SSD_WIDTH), 0)
    gc = lax.broadcasted_iota(jnp.int32, (half, SSD_WIDTH), 1)
    gsum = ((gr >> 7) == (gc >> 8)).astype(bf16)

    y_t = []
    for t in range(steps):
        acc = None
        for s in range(t + 1):
            coef = _dot_x2(slab(cm, t) * slab(bm, s), gsum)
            if s < t:
                coef = coef * _dot_x2(jnp.exp(cs_t[t] - cs_t[s]), expand)
            term = coef * slab(xdt, s)
            acc = term if acc is None else acc + term
        y_t.append(acc)
    y_intra = jnp.concatenate(y_t, axis=0)

    b_idx = lax.broadcasted_iota(jnp.int32, (bt, 1, LANES), 0)
    l_idx = lax.broadcasted_iota(jnp.int32, (bt, 1, LANES), 2)
    pair = ((l_idx & (bt - 1)) == b_idx) & (l_idx < R)
    own = (l_idx == b_idx)
    y_off_parts = []
    for g in range(SSD_GROUPS):
        gl = slice(g * half, (g + 1) * half)
        h0 = st_ref[:, gl, :]
        cg = _pad_rows(cm[:, g * SSD_STATE:(g + 1) * SSD_STATE], LANES)
        zz = _dot_nt(h0.reshape(bt * half, SSD_STATE), cg).reshape(bt, half, LANES)
        yt = jnp.sum(jnp.where(pair, zz, 0.0), axis=0)
        y_off_parts.append(yt.T[0:R, :])
        wt = _pad_rows(w_end[:, gl], LANES).T
        lhs = jnp.where(pair, wt[None], 0.0).reshape(bt * half, LANES)
        bg = _pad_rows(bm[:, g * SSD_STATE:(g + 1) * SSD_STATE], LANES)
        contrib = _dot(lhs, bg).reshape(bt, half, SSD_STATE)
        dec_t = _pad_rows(dec_e[:, gl], LANES).T
        dec = jnp.sum(jnp.where(own, dec_t[None], 0.0), axis=-1, keepdims=True)
        ssm_ref[:, gl, :] = h0 * dec + contrib
    y_off = jnp.concatenate(y_off_parts, axis=-1) * ecs_e
    y = y_intra + y_off + xs * dskip_ref[...]
    y_ssd = _gated_group_norm(y, z, gssd_ref[...])

    HD = ATT_HEAD_DIM
    GH = ATT_HEADS // ATT_KV_HEADS
    GR = GH * R
    reps = GR // bt
    ridx = lax.broadcasted_iota(jnp.int32, (GR, 1), 0)
    rb = ridx % bt
    rt = (ridx // bt) % steps
    jcol = lax.broadcasted_iota(jnp.int32, (GR, WINDOW), 1)
    in_window = jcol > rt
    mix = jnp.dot(y_ssd.astype(bf16), wout_ref[0:SSD_WIDTH, :], preferred_element_type=f32)
    for g in range(ATT_KV_HEADS):
        qg = jnp.concatenate([q[:, (g * GH + hl) * HD:(g * GH + hl + 1) * HD] for hl in range(GH)],
                             axis=0)
        qg_b = qg.astype(bf16)
        s_c = jnp.zeros((GR, WINDOW), f32)
        for b in range(bt):
            s_c = jnp.where(rb == b, _dot(qg_b, ck_ref[b, g]), s_c)
        s_c = jnp.where(in_window, s_c, -jnp.inf)
        sink = sinkcol_ref[g * GR:(g + 1) * GR, :]
        m = jnp.maximum(jnp.max(s_c, axis=-1, keepdims=True), sink)
        s_n = []
        for t2 in range(steps):
            kt = jnp.concatenate([slab(k_new, t2)[:, g * HD:(g + 1) * HD]] * reps, axis=0)
            sn = jnp.where(rt >= t2, jnp.sum(qg * kt, axis=-1, keepdims=True), -jnp.inf)
            s_n.append(sn)
            m = jnp.maximum(m, sn)
        e_c = jnp.exp(s_c - m)
        e_n = [jnp.exp(sn - m) for sn in s_n]
        den = jnp.sum(e_c, axis=-1, keepdims=True) + jnp.exp(sink - m)
        for en in e_n:
            den = den + en
        rinv = 1.0 / den
        p_c = (e_c * rinv).astype(bf16)
        o = jnp.zeros((GR, HD), f32)
        for b in range(bt):
            o = jnp.where(rb == b, _dot_nt(p_c, cv_ref[b, g]), o)
        for t2 in range(steps):
            vt = jnp.concatenate([slab(v_new, t2)[:, g * HD:(g + 1) * HD]] * reps, axis=0)
            o = o + (e_n[t2] * rinv) * vt
        for hl in range(GH):
            hd = g * GH + hl
            mix = mix + jnp.dot(o[hl * R:(hl + 1) * R].astype(bf16),
                                wout_ref[SSD_WIDTH + hd * HD:SSD_WIDTH + (hd + 1) * HD, :],
                                preferred_element_type=f32)
    y_ref[...] = (x + _rmsnorm(mix, gpost_ref[...])).reshape(steps, bt, D_MODEL)

    keep = WINDOW - steps
    kn_t = _pad_rows(k_new, LANES).T
    vn_t = _pad_rows(v_new, LANES).T
    sel_r = lax.broadcasted_iota(jnp.int32, (LANES, WINDOW), 0)
    sel_l = lax.broadcasted_iota(jnp.int32, (LANES, WINDOW), 1)
    lane_w = lax.broadcasted_iota(jnp.int32, (1, WINDOW), 1)
    for b in range(bt):
        sel = ((sel_r % bt == b) & (sel_r < R) & (sel_l - keep == sel_r // bt)).astype(bf16)
        new_k = _dot_x2(kn_t, sel)
        new_v = _dot_x2(vn_t, sel)
        for g in range(ATT_KV_HEADS):
            gs = slice(g * HD, (g + 1) * HD)
            wk_ref[b, g] = jnp.where(lane_w < keep, pltpu.roll(ck_ref[b, g], keep, 1), new_k[gs])
            wv_ref[b, g] = jnp.where(lane_w < keep, pltpu.roll(cv_ref[b, g], keep, 1), new_v[gs])


def _sample_mixer(x_tm, cprev_tm, st, ck, cv, gpre, win, convw, convb, dtb, alog, dskip_e, gssd,
                  sinkcol, wout, gpost, bt):
    steps, nb, D = x_tm.shape
    kern = functools.partial(_sample_mixer_kernel, bt=bt, steps=steps)
    tm = lambda w: pl.BlockSpec((steps, bt, w), lambda i: (0, i, 0))
    win_spec = pl.BlockSpec((bt, ATT_KV_HEADS, ATT_HEAD_DIM, WINDOW), lambda i: (i, 0, 0, 0))
    in_specs = [
        tm(D),
        pl.BlockSpec((SSD_CONV - 1, bt, SSD_CONV_DIM), lambda i: (0, i, 0)),
        pl.BlockSpec((bt, SSD_WIDTH, SSD_STATE), lambda i: (i, 0, 0)),
        win_spec, win_spec,
    ] + [_const_spec(a.shape) for a in (gpre, win, convw, convb, dtb, alog, dskip_e, gssd,
                                        sinkcol, wout, gpost)]
    out_specs = (
        tm(D),
        pl.BlockSpec((bt, SSD_WIDTH, SSD_STATE), lambda i: (i, 0, 0)),
        pl.BlockSpec((SSD_CONV - 1, bt, SSD_CONV_DIM), lambda i: (0, i, 0)),
        win_spec, win_spec,
    )
    out_shape = (
        jax.ShapeDtypeStruct((steps, nb, D), f32),
        jax.ShapeDtypeStruct((nb, SSD_WIDTH, SSD_STATE), f32),
        jax.ShapeDtypeStruct((SSD_CONV - 1, nb, SSD_CONV_DIM), f32),
        jax.ShapeDtypeStruct(ck.shape, f32),
        jax.ShapeDtypeStruct(cv.shape, f32),
    )
    return pl.pallas_call(
        kern, grid=(nb // bt,), in_specs=in_specs, out_specs=out_specs, out_shape=out_shape,
        name="sample_mixer",
        compiler_params=pltpu.CompilerParams(
            dimension_semantics=("arbitrary",), vmem_limit_bytes=VMEM_LIMIT),
    )(x_tm, cprev_tm, st, ck, cv, gpre, win, convw, convb, dtb, alog, dskip_e, gssd, sinkcol,
      wout, gpost)


def _sample_xattn_kernel(x_ref, mk_ref, mv_ref, gpre_ref, wq_ref, wo_ref, gpost_ref, y_ref,
                         *, bt, steps):
    R = steps * bt
    nrow = bt * N_MEM * X_HEADS
    x = x_ref[...].reshape(R, D_MODEL)
    hn = _rmsnorm(x, gpre_ref[...]).astype(bf16)
    q = jnp.dot(hn, wq_ref[...], preferred_element_type=f32)
    qs = jnp.concatenate([q[:, hd * X_HEAD_DIM:(hd + 1) * X_HEAD_DIM] for hd in range(X_HEADS)],
                         axis=0)
    kall = mk_ref[...].reshape(nrow, X_HEAD_DIM)
    vall = mv_ref[...].reshape(nrow, X_HEAD_DIM)
    z = _dot_nt(kall, qs) * X_SCALE
    ncol = X_HEADS * R
    z = z.reshape(bt, N_MEM * X_HEADS // SUBLANES, SUBLANES, ncol)
    mshape = (bt, 1, SUBLANES, ncol)
    b_i = lax.broadcasted_iota(jnp.int32, mshape, 0)
    r_i = lax.broadcasted_iota(jnp.int32, mshape, 2)
    c_i = lax.broadcasted_iota(jnp.int32, mshape, 3)
    keep = (b_i == (c_i % bt)) & ((r_i % X_HEADS) == (c_i // R))
    z = jnp.where(keep, z, -jnp.inf).reshape(nrow, ncol)
    m = jnp.max(z, axis=0, keepdims=True)
    e = jnp.exp(z - m)
    p = e * (1.0 / jnp.sum(e, axis=0, keepdims=True))
    o = lax.dot_general(p.astype(bf16), vall.astype(bf16), (((0,), (0,)), ((), ())),
                        preferred_element_type=f32)
    o = jnp.concatenate([o[hd * R:(hd + 1) * R] for hd in range(X_HEADS)], axis=-1)
    cc = _dot(o, wo_ref[...])
    y_ref[...] = (x + _rmsnorm(cc, gpost_ref[...])).reshape(steps, bt, D_MODEL)


def _sample_xattn(x_tm, mk, mv, gpre, wq, wo, gpost, bt):
    steps, nb, D = x_tm.shape
    kern = functools.partial(_sample_xattn_kernel, bt=bt, steps=steps)
    xs = pl.BlockSpec((steps, bt, D), lambda i: (0, i, 0))
    ms = pl.BlockSpec((bt, N_MEM, X_HEADS, X_HEAD_DIM), lambda i: (i, 0, 0, 0))
    return pl.pallas_call(
        kern, grid=(nb // bt,),
        in_specs=[xs, ms, ms, _const_spec(gpre.shape), _const_spec(wq.shape),
                  _const_spec(wo.shape), _const_spec(gpost.shape)],
        out_specs=xs, out_shape=jax.ShapeDtypeStruct((steps, nb, D), f32),
        name="sample_xattn",
        compiler_params=pltpu.CompilerParams(
            dimension_semantics=("arbitrary",), vmem_limit_bytes=VMEM_LIMIT),
    )(x_tm, mk, mv, gpre, wq, wo, gpost)


def _row(v, width=None):
    v = v.reshape(1, -1).astype(f32)
    if width is not None and v.shape[1] < width:
        v = jnp.pad(v, ((0, 0), (0, width - v.shape[1])))
    return v


def kernel(x_prompt, x_sample, state_ssm, state_conv, cache_win_k, cache_win_v, cache_mem_k, cache_mem_v, mem_prompt, g_mix_pre, w_in, conv_w, conv_b, dt_bias, a_log, d_skip, g_ssd_norm, sinks, w_out, g_mix_post, g_x_pre, w_xq, g_mem, w_xk, w_xv, w_xo, g_x_post, g_ffn_pre, w_gate, w_up, w_down, g_ffn_post):
    depth = w_in.shape[0]
    assert depth == 1
    B, L, D = x_prompt.shape
    NB, steps, _ = x_sample.shape
    li = 0

    w = w_in[li]
    wz, wxbc, wdt = w[:, 0:512], w[:, 512:1536], w[:, 1536:1544]
    wq, wk, wv = w[:, 1544:2056], w[:, 2056:2184], w[:, 2184:2312]
    wdt_p = jnp.pad(wdt, ((0, 0), (0, LANES - SSD_HEADS)))
    win_p = jnp.concatenate([wz, wxbc, wq * ATT_SCALE, wk, wv, wdt_p], axis=1).astype(bf16)
    wo_b = w_out[li].astype(bf16)

    gpre, gpost = _row(g_mix_pre[li]), _row(g_mix_post[li])
    convw, convb = conv_w[li].astype(f32), _row(conv_b[li])
    dtb, alog = _row(dt_bias[li], LANES), _row(a_log[li], LANES)
    dskip_e = _row(jnp.repeat(d_skip[li], SSD_HEAD_DIM))
    gssd = _row(g_ssd_norm[li])
    sk = sinks[li].astype(f32)

    mk2d, mv2d = _memkv(mem_prompt.reshape(B * N_MEM, D), _row(g_mem[li]),
                        w_xk[li].astype(bf16), w_xv[li].astype(bf16), tile=512)
    mk3, mv3 = mk2d.reshape(B, N_MEM, D), mv2d.reshape(B, N_MEM, D)
    x1, p_ssm, p_conv8, p_wk, p_wv = _prompt_mixer(
        x_prompt, gpre, win_p, convw, convb, dtb, alog, dskip_e, gssd, sk, wo_b, gpost, tile=512)
    wxq_b, wxo_b = w_xq[li].astype(bf16), w_xo[li].astype(bf16)
    gxpre, gxpost = _row(g_x_pre[li]), _row(g_x_post[li])
    x2 = _prompt_xattn(x1, gxpre, wxq_b, mk3, mv3, wxo_b, gxpost, tile=512)
    wg_b, wu_b, wd_b = w_gate[li].astype(bf16), w_up[li].astype(bf16), w_down[li].astype(bf16)
    gfpre, gfpost = _row(g_ffn_pre[li]), _row(g_ffn_post[li])
    yp = _ffn(x2.reshape(B * L, D), gfpre, wg_b, wu_b, wd_b, gfpost, tile=512).reshape(B, L, D)

    bt = 8
    x_tm = jnp.transpose(x_sample, (1, 0, 2))
    cprev_tm = jnp.transpose(state_conv[li], (1, 0, 2))
    st = state_ssm[li].reshape(NB, SSD_WIDTH, SSD_STATE)
    ck = jnp.transpose(cache_win_k[li], (0, 2, 3, 1))
    cv = jnp.transpose(cache_win_v[li], (0, 2, 3, 1))
    sinkcol = jnp.repeat(sk, steps * bt).reshape(ATT_HEADS * steps * bt, 1)
    x1s, s_ssm, cnew_tm, s_wk, s_wv = _sample_mixer(
        x_tm, cprev_tm, st, ck, cv, gpre, win_p, convw, convb, dtb, alog, dskip_e, gssd,
        sinkcol, wo_b, gpost, bt=bt)
    cmk = cache_mem_k.reshape(NB, N_MEM, X_HEADS, X_HEAD_DIM)
    cmv = cache_mem_v.reshape(NB, N_MEM, X_HEADS, X_HEAD_DIM)
    x2s = _sample_xattn(x1s, cmk, cmv, gxpre, wxq_b, wxo_b, gxpost, bt=bt)
    ys_tm = _ffn(x2s.reshape(steps * NB, D), gfpre, wg_b, wu_b, wd_b, gfpost, tile=steps * NB)
    ys = jnp.transpose(ys_tm.reshape(steps, NB, D), (1, 0, 2))

    s_conv = jnp.transpose(cnew_tm, (1, 0, 2))
    kv_shape = (ATT_KV_HEADS, ATT_HEAD_DIM)
    return (
        yp, ys,
        p_ssm.reshape(1, B, SSD_HEADS, SSD_HEAD_DIM, SSD_STATE),
        p_conv8[:, SUBLANES - (SSD_CONV - 1):, :][None],
        jnp.transpose(p_wk.reshape(B, *kv_shape, WINDOW), (0, 3, 1, 2))[None],
        jnp.transpose(p_wv.reshape(B, *kv_shape, WINDOW), (0, 3, 1, 2))[None],
        mk3.reshape(1, B, N_MEM, X_HEADS, X_HEAD_DIM), mv3.reshape(1, B, N_MEM, X_HEADS, X_HEAD_DIM),
        s_ssm.reshape(1, NB, SSD_HEADS, SSD_HEAD_DIM, SSD_STATE),
        s_conv[None],
        jnp.transpose(s_wk, (0, 3, 1, 2))[None], jnp.transpose(s_wv, (0, 3, 1, 2))[None],
    )
```

```python
import functools

import jax
import jax.numpy as jnp
from jax import lax
from jax.experimental import pallas as pl
from jax.experimental.pallas import tpu as pltpu

f32 = jnp.float32
bf16 = jnp.bfloat16

D_MODEL = 1024
EPS = 1e-6
N_MEM = 256
SSD_HEADS = 8
SSD_HEAD_DIM = 64
SSD_WIDTH = 512
SSD_GROUPS = 2
SSD_STATE = 128
SSD_CONV = 4
SSD_CHUNK = 128
SSD_CONV_DIM = 1024
ATT_HEADS = 8
ATT_KV_HEADS = 2
ATT_HEAD_DIM = 64
ATT_WIDTH = 512
ATT_KV_WIDTH = 128
WINDOW = 128
ATT_SCALE = ATT_HEAD_DIM ** -0.5
X_HEADS = 4
X_HEAD_DIM = 256
X_SCALE = X_HEAD_DIM ** -0.5
D_FF = 2816
LANES = 128
SUBLANES = 8
VMEM_LIMIT = 56 * 1024 * 1024

P_Z, P_XBC, P_Q, P_K, P_V, P_DT, P_END = 0, 512, 1536, 2048, 2176, 2304, 2432


def _dot(a, b):
    return jnp.dot(a.astype(bf16), b.astype(bf16), preferred_element_type=f32)


def _dot_nt(a, b):
    return lax.dot_general(a.astype(bf16), b.astype(bf16), (((1,), (1,)), ((), ())),
                           preferred_element_type=f32)


def _split2(x):
    hi = x.astype(bf16)
    lo = (x - hi.astype(f32)).astype(bf16)
    return hi, lo


def _dot_x2(x, m):
    hi, lo = _split2(x)
    return (jnp.dot(hi, m, preferred_element_type=f32)
            + jnp.dot(lo, m, preferred_element_type=f32))


def _rmsnorm(x, g):
    ms = jnp.mean(x * x, axis=-1, keepdims=True)
    return x * lax.rsqrt(ms + EPS) * g


def _silu(x):
    return x * jax.nn.sigmoid(x)


def _softplus(x):
    return jnp.maximum(x, 0.0) + jnp.log1p(jnp.exp(-jnp.abs(x)))


def _head_expand_matrix():
    r = lax.broadcasted_iota(jnp.int32, (LANES, SSD_WIDTH), 0)
    c = lax.broadcasted_iota(jnp.int32, (LANES, SSD_WIDTH), 1)
    return (r == (c >> 6)).astype(bf16)


def _gated_group_norm(y, z, g):
    u = y * _silu(z)
    half = SSD_WIDTH // SSD_GROUPS
    parts = []
    for gi in range(SSD_GROUPS):
        ug = u[:, gi * half:(gi + 1) * half]
        parts.append(ug * lax.rsqrt(jnp.mean(ug * ug, axis=-1, keepdims=True) + EPS))
    return jnp.concatenate(parts, axis=-1) * g


def _prompt_mixer_kernel(x_ref, gpre_ref, win_ref, convw_ref, convb_ref, dtb_ref, alog_ref,
                         dskip_ref, gssd_ref, sinks_ref, wout_ref, gpost_ref,
                         y_ref, ssm_ref, conv_ref, wk_ref, wv_ref,
                         statet_sc, xbc_ext_sc, xbc_sc, z_sc, q_sc, k_sc, v_sc, dt_sc,
                         kprev_sc, vprev_sc, mix_sc, *, tile):
    i = pl.program_id(1)
    nchunk = tile // SSD_CHUNK
    C = SSD_CHUNK

    @pl.when(i == 0)
    def _init():
        statet_sc[...] = jnp.zeros_like(statet_sc)
        xbc_ext_sc[0:SUBLANES, :] = jnp.zeros((SUBLANES, SSD_CONV_DIM), f32)
        kprev_sc[...] = jnp.zeros_like(kprev_sc)
        vprev_sc[...] = jnp.zeros_like(vprev_sc)

    x = x_ref[0]
    h = _rmsnorm(x, gpre_ref[...]).astype(bf16)
    z_sc[...] = jnp.dot(h, win_ref[:, P_Z:P_XBC], preferred_element_type=f32)
    xbc_ext_sc[SUBLANES:SUBLANES + tile, :] = jnp.dot(h, win_ref[:, P_XBC:P_Q],
                                                      preferred_element_type=f32)
    q_sc[...] = jnp.dot(h, win_ref[:, P_Q:P_K], preferred_element_type=f32)
    k_sc[...] = jnp.dot(h, win_ref[:, P_K:P_V], preferred_element_type=f32)
    v_sc[...] = jnp.dot(h, win_ref[:, P_V:P_DT], preferred_element_type=f32)
    dt_raw = jnp.dot(h, win_ref[:, P_DT:P_END], preferred_element_type=f32)
    lane = lax.broadcasted_iota(jnp.int32, (1, LANES), 1)
    dt_sc[...] = jnp.where(lane < SSD_HEADS, _softplus(dt_raw + dtb_ref[...]), 0.0)

    acc = convb_ref[...]
    for j in range(SSD_CONV):
        off = SUBLANES - (SSD_CONV - 1) + j
        acc = acc + xbc_ext_sc[off:off + tile, :] * convw_ref[j:j + 1, :]
    xbc_sc[...] = _silu(acc)
    tail = xbc_ext_sc[tile:tile + SUBLANES, :]
    conv_ref[0] = tail
    xbc_ext_sc[0:SUBLANES, :] = tail

    a_row = -jnp.exp(alog_ref[...])
    expand = _head_expand_matrix()
    expand2 = jnp.concatenate([expand, expand], axis=0)
    row_i = lax.broadcasted_iota(jnp.int32, (C, C), 0)
    col_i = lax.broadcasted_iota(jnp.int32, (C, C), 1)
    lower = col_i <= row_i
    tri = lower.astype(bf16)
    tri3 = jnp.concatenate([tri, tri, tri], axis=1)
    lo_half = lane < ATT_HEAD_DIM
    half = SSD_WIDTH // SSD_GROUPS
    head_of_lane = lax.broadcasted_iota(jnp.int32, (1, half), 1) >> 6

    def chunk(c, carry):
        r0 = pl.multiple_of(c * C, C)
        rows = pl.ds(r0, C)
        GH = SSD_HEADS // SSD_GROUPS
        heads = range(ATT_HEADS)
        zero_b = jnp.zeros((), bf16)

        xs = xbc_sc[rows, 0:SSD_WIDTH]
        bm = xbc_sc[rows, SSD_WIDTH:SSD_WIDTH + half]
        cm = xbc_sc[rows, SSD_WIDTH + half:SSD_CONV_DIM]
        dtc = dt_sc[rows, :]
        adt = dtc * a_row
        a_hi = adt.astype(bf16)
        a_r1 = adt - a_hi.astype(f32)
        a_mid = a_r1.astype(bf16)
        a_lo = (a_r1 - a_mid.astype(f32)).astype(bf16)
        cs = jnp.dot(tri3, jnp.concatenate([a_hi, a_mid, a_lo], axis=0), preferred_element_type=f32)

        q = q_sc[rows, :].astype(bf16)
        k = k_sc[rows, :]
        v = v_sc[rows, :]
        kk = jnp.concatenate([kprev_sc[...], k], axis=0)
        vv = jnp.concatenate([vprev_sc[...], v], axis=0)
        kk_r = pltpu.roll(kk, ATT_HEAD_DIM, 1)
        vv_r = pltpu.roll(vv, ATT_HEAD_DIM, 1)
        k_lo = [jnp.where(lo_half, kk, 0.0).astype(bf16), jnp.where(lo_half, kk_r, 0.0).astype(bf16)]
        k_hi = [jnp.where(lo_half, 0.0, kk_r).astype(bf16), jnp.where(lo_half, 0.0, kk).astype(bf16)]
        v_lo = [jnp.where(lo_half, vv, 0.0).astype(bf16), jnp.where(lo_half, vv_r, 0.0).astype(bf16)]
        v_hi = [jnp.where(lo_half, 0.0, vv_r).astype(bf16), jnp.where(lo_half, 0.0, vv).astype(bf16)]
        kprev_sc[...] = k
        vprev_sc[...] = v
        prev_bias = jnp.where((i * nchunk + c) > 0, 0.0, -jnp.inf)
        s_g = [_dot_nt(jnp.concatenate([q[:, (2 * g) * LANES:(2 * g + 1) * LANES],
                                        q[:, (2 * g + 1) * LANES:(2 * g + 2) * LANES]], axis=0),
                       jnp.concatenate([k_lo[g], k_hi[g]], axis=0))
               for g in range(ATT_KV_HEADS)]

        cs_t = cs.T
        tot = cs[C - 1:C, :]
        fac = jnp.concatenate([dtc, jnp.exp(cs), jnp.exp(tot - cs)], axis=0)
        f_hi = fac.astype(bf16)
        f_lo = (fac - f_hi.astype(f32)).astype(bf16)
        fac_e = jnp.dot(jnp.concatenate([f_hi, f_lo], axis=1), expand2,
                        preferred_element_type=f32)
        cb_all = _dot_nt(jnp.concatenate([cm[:, 0:SSD_STATE], cm[:, SSD_STATE:half]], axis=0),
                         jnp.concatenate([bm[:, 0:SSD_STATE], bm[:, SSD_STATE:half]], axis=0))
        bm_t = bm.T

        def head_scores(hd):
            g, jb, sub = hd // 4, (hd // 2) % 2, hd % 2
            s_prev = s_g[g][jb * C:(jb + 1) * C, (2 * sub) * C:(2 * sub + 1) * C]
            s_cur = s_g[g][jb * C:(jb + 1) * C, (2 * sub + 1) * C:(2 * sub + 2) * C]
            return jnp.where(lower, s_cur, s_prev + prev_bias)
        sc = [head_scores(hd) for hd in heads]
        sink = [sinks_ref[hd] for hd in heads]
        m = [jnp.maximum(jnp.max(sc[hd], axis=-1, keepdims=True), sink[hd]) for hd in heads]

        xdt = xs * fac_e[0:C]
        ecs_e = fac_e[C:2 * C]
        w_end = xdt * fac_e[2 * C:3 * C]
        xdt_b = xdt.astype(bf16)
        st = [statet_sc[g] for g in range(SSD_GROUPS)]
        y_off = [_dot(cm[:, g * SSD_STATE:(g + 1) * SSD_STATE], st[g]) for g in range(SSD_GROUPS)]
        decay = [jnp.exp(jnp.where(lower, cs[:, hh:hh + 1] - cs_t[hh:hh + 1, :], -jnp.inf))
                 for hh in range(SSD_HEADS)]

        e = [jnp.exp(sc[hd] - m[hd]) for hd in heads]
        den = [jnp.sum(e[hd], axis=-1, keepdims=True) + jnp.exp(sink[hd] - m[hd]) for hd in heads]

        y_d, new_st = [], []
        for g in range(SSD_GROUPS):
            gl = slice(g * half, (g + 1) * half)
            cb = cb_all[g * C:(g + 1) * C, g * C:(g + 1) * C]
            m_parts = [(cb * decay[g * GH + r]).astype(bf16) for r in range(GH)]
            x_parts = [jnp.where(head_of_lane == r, xdt_b[:, gl], zero_b) for r in range(GH)]
            y_d.append(jnp.dot(jnp.concatenate(m_parts, axis=1), jnp.concatenate(x_parts, axis=0),
                               preferred_element_type=f32))
            new_st.append(st[g] * ecs_e[C - 1:C, gl]
                          + _dot(bm_t[g * SSD_STATE:(g + 1) * SSD_STATE, :], w_end[:, gl]))
        for g in range(SSD_GROUPS):
            statet_sc[g] = new_st[g]

        p = [(e[hd] * (1.0 / den[hd])).astype(bf16) for hd in heads]
        o_g = []
        for g in range(ATT_KV_HEADS):
            p_rows = []
            for jb in range(2):
                p_cols = []
                for sub in range(2):
                    ph = p[g * 4 + jb * 2 + sub]
                    p_cols += [jnp.where(lower, zero_b, ph), jnp.where(lower, ph, zero_b)]
                p_rows.append(jnp.concatenate(p_cols, axis=1))
            o_g.append(jnp.dot(jnp.concatenate(p_rows, axis=0),
                               jnp.concatenate([v_lo[g], v_hi[g]], axis=0),
                               preferred_element_type=f32))

        y = (jnp.concatenate([y_d[g] + y_off[g] * ecs_e[:, g * half:(g + 1) * half]
                              for g in range(SSD_GROUPS)], axis=-1)
             + xs * dskip_ref[...])
        y_ssd = _gated_group_norm(y, z_sc[rows, :], gssd_ref[...])
        mix_sc[rows, 0:SSD_WIDTH] = y_ssd.astype(bf16)
        for g in range(ATT_KV_HEADS):
            for jb in range(2):
                lo_l = SSD_WIDTH + (2 * g + jb) * LANES
                mix_sc[rows, lo_l:lo_l + LANES] = o_g[g][jb * C:(jb + 1) * C].astype(bf16)
        return carry

    lax.fori_loop(0, nchunk, chunk, 0)

    @pl.when(i == pl.num_programs(1) - 1)
    def _final_state():
        for g in range(SSD_GROUPS):
            ssm_ref[0, g * half:(g + 1) * half, :] = statet_sc[g].T
        wk_ref[0] = k_sc[tile - WINDOW:tile, :].T
        wv_ref[0] = v_sc[tile - WINDOW:tile, :].T

    mix = jnp.dot(mix_sc[...], wout_ref[...], preferred_element_type=f32)
    y_ref[0] = x + _rmsnorm(mix, gpost_ref[...])


def _const_spec(shape):
    nd = len(shape)
    return pl.BlockSpec(shape, lambda *_: (0,) * nd)


def _prompt_mixer(x, gpre, win, convw, convb, dtb, alog, dskip_e, gssd, sinks, wout, gpost, tile):
    B, L, D = x.shape
    grid = (B, L // tile)
    kern = functools.partial(_prompt_mixer_kernel, tile=tile)
    out_shape = (
        jax.ShapeDtypeStruct((B, L, D), f32),
        jax.ShapeDtypeStruct((B, SSD_WIDTH, SSD_STATE), f32),
        jax.ShapeDtypeStruct((B, SUBLANES, SSD_CONV_DIM), f32),
        jax.ShapeDtypeStruct((B, WINDOW, ATT_KV_WIDTH), f32),
        jax.ShapeDtypeStruct((B, WINDOW, ATT_KV_WIDTH), f32),
    )
    in_specs = [
        pl.BlockSpec((1, tile, D), lambda b, i: (b, i, 0)),
        _const_spec(gpre.shape), _const_spec(win.shape), _const_spec(convw.shape),
        _const_spec(convb.shape), _const_spec(dtb.shape), _const_spec(alog.shape),
        _const_spec(dskip_e.shape), _const_spec(gssd.shape),
        pl.BlockSpec(memory_space=pltpu.SMEM),
        _const_spec(wout.shape), _const_spec(gpost.shape),
    ]
    out_specs = (
        pl.BlockSpec((1, tile, D), lambda b, i: (b, i, 0)),
        pl.BlockSpec((1, SSD_WIDTH, SSD_STATE), lambda b, i: (b, 0, 0)),
        pl.BlockSpec((1, SUBLANES, SSD_CONV_DIM), lambda b, i: (b, 0, 0)),
        pl.BlockSpec((1, WINDOW, ATT_KV_WIDTH), lambda b, i: (b, 0, 0)),
        pl.BlockSpec((1, WINDOW, ATT_KV_WIDTH), lambda b, i: (b, 0, 0)),
    )
    scratch = [
        pltpu.VMEM((SSD_GROUPS, SSD_STATE, SSD_WIDTH // SSD_GROUPS), f32),
        pltpu.VMEM((tile + 2 * SUBLANES, SSD_CONV_DIM), f32),
        pltpu.VMEM((tile, SSD_CONV_DIM), f32),
        pltpu.VMEM((tile, SSD_WIDTH), f32),
        pltpu.VMEM((tile, ATT_WIDTH), f32),
        pltpu.VMEM((tile, ATT_KV_WIDTH), f32),
        pltpu.VMEM((tile, ATT_KV_WIDTH), f32),
        pltpu.VMEM((tile, LANES), f32),
        pltpu.VMEM((WINDOW, ATT_KV_WIDTH), f32),
        pltpu.VMEM((WINDOW, ATT_KV_WIDTH), f32),
        pltpu.VMEM((tile, 2 * SSD_WIDTH), bf16),
    ]
    return pl.pallas_call(
        kern, grid=grid, in_specs=in_specs, out_specs=out_specs, out_shape=out_shape,
        scratch_shapes=scratch, name="prompt_mixer",
        compiler_params=pltpu.CompilerParams(
            dimension_semantics=("arbitrary", "arbitrary"), vmem_limit_bytes=VMEM_LIMIT),
    )(x, gpre, win, convw, convb, dtb, alog, dskip_e, gssd, sinks, wout, gpost)


def _memkv_kernel(m_ref, g_ref, wk_ref, wv_ref, k_ref, v_ref):
    mn = _rmsnorm(m_ref[...], g_ref[...]).astype(bf16)
    k_ref[...] = jnp.dot(mn, wk_ref[...], preferred_element_type=f32)
    v_ref[...] = jnp.dot(mn, wv_ref[...], preferred_element_type=f32)


def _memkv(mem2d, g, wk, wv, tile):
    n, d = mem2d.shape
    row = pl.BlockSpec((tile, d), lambda i: (i, 0))
    return pl.pallas_call(
        _memkv_kernel, grid=(n // tile,),
        in_specs=[row, _const_spec(g.shape), _const_spec(wk.shape), _const_spec(wv.shape)],
        out_specs=(row, row),
        out_shape=(jax.ShapeDtypeStruct((n, d), f32), jax.ShapeDtypeStruct((n, d), f32)),
        name="memory_kv",
        compiler_params=pltpu.CompilerParams(
            dimension_semantics=("arbitrary",), vmem_limit_bytes=VMEM_LIMIT),
    )(mem2d, g, wk, wv)


def _prompt_xattn_kernel(x_ref, gpre_ref, wq_ref, mk_ref, mv_ref, wo_ref, gpost_ref, y_ref):
    x = x_ref[0]
    hn = _rmsnorm(x, gpre_ref[...]).astype(bf16)
    q = jnp.dot(hn, wq_ref[...], preferred_element_type=f32)
    hs = range(X_HEADS)
    sl = [slice(hd * X_HEAD_DIM, (hd + 1) * X_HEAD_DIM) for hd in hs]
    s = [_dot_nt(q[:, sl[hd]], mk_ref[0, :, sl[hd]]) for hd in hs]
    m = [jnp.max(s[hd], axis=-1, keepdims=True) for hd in hs]
    e = [jnp.exp(s[hd] - m[hd]) for hd in hs]
    r = [1.0 / jnp.sum(e[hd], axis=-1, keepdims=True) for hd in hs]
    o = jnp.concatenate([_dot(e[hd] * r[hd], mv_ref[0, :, sl[hd]]) for hd in hs], axis=-1)
    c = _dot(o, wo_ref[...])
    y_ref[0] = x + _rmsnorm(c, gpost_ref[...])


def _prompt_xattn(x, gpre, wq, mk, mv, wo, gpost, tile):
    B, L, D = x.shape
    xs = pl.BlockSpec((1, tile, D), lambda b, i: (b, i, 0))
    ms = pl.BlockSpec((1, N_MEM, D), lambda b, i: (b, 0, 0))
    return pl.pallas_call(
        _prompt_xattn_kernel, grid=(B, L // tile),
        in_specs=[xs, _const_spec(gpre.shape), _const_spec(wq.shape), ms, ms,
                  _const_spec(wo.shape), _const_spec(gpost.shape)],
        out_specs=xs, out_shape=jax.ShapeDtypeStruct((B, L, D), f32),
        name="prompt_xattn",
        compiler_params=pltpu.CompilerParams(
            dimension_semantics=("arbitrary", "arbitrary"), vmem_limit_bytes=VMEM_LIMIT),
    )(x, gpre, wq, mk, mv, wo, gpost)


def _ffn_kernel(x_ref, gpre_ref, wg_ref, wu_ref, wd_ref, gpost_ref, y_ref):
    x = x_ref[...]
    hf = _rmsnorm(x, gpre_ref[...]).astype(bf16)
    gate = jnp.dot(hf, wg_ref[...], preferred_element_type=f32)
    up = jnp.dot(hf, wu_ref[...], preferred_element_type=f32)
    act = (_silu(gate) * up).astype(bf16)
    f = jnp.dot(act, wd_ref[...], preferred_element_type=f32)
    y_ref[...] = x + _rmsnorm(f, gpost_ref[...])


def _ffn(x2d, gpre, wg, wu, wd, gpost, tile):
    n, d = x2d.shape
    row = pl.BlockSpec((tile, d), lambda i: (i, 0))
    return pl.pallas_call(
        _ffn_kernel, grid=(n // tile,),
        in_specs=[row, _const_spec(gpre.shape), _const_spec(wg.shape), _const_spec(wu.shape),
                  _const_spec(wd.shape), _const_spec(gpost.shape)],
        out_specs=row, out_shape=jax.ShapeDtypeStruct((n, d), f32),
        name="ffn",
        compiler_params=pltpu.CompilerParams(
            dimension_semantics=("arbitrary",), vmem_limit_bytes=VMEM_LIMIT),
    )(x2d, gpre, wg, wu, wd, gpost)


def _pad_rows(a, rows):
    if a.shape[0] == rows:
        return a
    return jnp.concatenate([a, jnp.zeros((rows - a.shape[0], a.shape[1]), a.dtype)], axis=0)


def _sample_mixer_kernel(x_ref, cprev_ref, st_ref, ck_ref, cv_ref,
                         gpre_ref, win_ref, convw_ref, convb_ref, dtb_ref, alog_ref,
                         dskip_ref, gssd_ref, sinkcol_ref, wout_ref, gpost_ref,
                         y_ref, ssm_ref, cnew_ref, wk_ref, wv_ref, *, bt, steps):
    R = steps * bt
    half = SSD_WIDTH // SSD_GROUPS
    x = x_ref[...].reshape(R, D_MODEL)
    h = _rmsnorm(x, gpre_ref[...]).astype(bf16)
    z = jnp.dot(h, win_ref[:, P_Z:P_XBC], preferred_element_type=f32)
    u = jnp.dot(h, win_ref[:, P_XBC:P_Q], preferred_element_type=f32)
    q = jnp.dot(h, win_ref[:, P_Q:P_K], preferred_element_type=f32)
    k_new = jnp.dot(h, win_ref[:, P_K:P_V], preferred_element_type=f32)
    v_new = jnp.dot(h, win_ref[:, P_V:P_DT], preferred_element_type=f32)
    dt_raw = jnp.dot(h, win_ref[:, P_DT:P_END], preferred_element_type=f32)
    lane = lax.broadcasted_iota(jnp.int32, (1, LANES), 1)
    dt = jnp.where(lane < SSD_HEADS, _softplus(dt_raw + dtb_ref[...]), 0.0)

    def slab(a, t):
        return a[t * bt:(t + 1) * bt]

    hist = [cprev_ref[j] for j in range(SSD_CONV - 1)] + [slab(u, t) for t in range(steps)]
    xbc_t = []
    for t in range(steps):
        acc = convb_ref[...]
        for j in range(SSD_CONV):
            acc = acc + hist[t + j] * convw_ref[j:j + 1, :]
        xbc_t.append(_silu(acc))
    for j in range(SSD_CONV - 1):
        cnew_ref[j] = hist[steps + j]
    xbc = jnp.concatenate(xbc_t, axis=0)
    xs = xbc[:, 0:SSD_WIDTH]
    bm = xbc[:, SSD_WIDTH:SSD_WIDTH + half]
    cm = xbc[:, SSD_WIDTH + half:SSD_CONV_DIM]

    a_row = -jnp.exp(alog_ref[...])
    adt = dt * a_row
    cs_t = [slab(adt, 0)]
    for t in range(1, steps):
        cs_t.append(cs_t[-1] + slab(adt, t))
    cs = jnp.concatenate(cs_t, axis=0)
    tot = cs_t[-1]
    tot_rows = jnp.concatenate([tot] * steps, axis=0)
    expand = _head_expand_matrix()
    expand2 = jnp.concatenate([expand, expand], axis=0)
    pairs = [(t, s2) for t in range(steps) for s2 in range(t)]
    fac = jnp.concatenate([dt, jnp.exp(cs), jnp.exp(tot_rows - cs), jnp.exp(tot)]
                          + [jnp.exp(cs_t[t] - cs_t[s2]) for t, s2 in pairs], axis=0)
    f_hi, f_lo = _split2(fac)
    fac_e = jnp.dot(jnp.concatenate([f_hi, f_lo], axis=1), expand2, preferred_element_type=f32)
    xdt = xs * fac_e[0:R]
    ecs_e = fac_e[R:2 * R]
    w_end = xdt * fac_e[2 * R:3 * R]
    dec_e = fac_e[3 * R:3 * R + bt]
    pair_decay = {pr: fac_e[3 * R + (n + 1) * bt:3 * R + (n + 2) * bt] for n, pr in enumerate(pairs)}

    gr = lax.broadcasted_iota(jnp.int32, (half, SSD_WIDTH), 0)
    gc = lax.broadcasted_iota(jnp.int32, (half, SSD_WIDTH), 1)
    gsum = ((gr >> 7) == (gc >> 8)).astype(bf16)
    gsum2 = jnp.concatenate([gsum, gsum], axis=0)
    cb_pairs = [(t, s2) for t in range(steps) for s2 in range(t + 1)]
    prod = jnp.concatenate([slab(cm, t) * slab(bm, s2) for t, s2 in cb_pairs], axis=0)
    c_hi, c_lo = _split2(prod)
    cb_e = jnp.dot(jnp.concatenate([c_hi, c_lo], axis=1), gsum2, preferred_element_type=f32)

    y_t = []
    for t in range(steps):
        acc = None
        for s2 in range(t + 1):
            n = cb_pairs.index((t, s2))
            coef = cb_e[n * bt:(n + 1) * bt]
            if s2 < t:
                coef = coef * pair_decay[(t, s2)]
            term = coef * slab(xdt, s2)
            acc = term if acc is None else acc + term
        y_t.append(acc)
    y_intra = jnp.concatenate(y_t, axis=0)

    b_idx = lax.broadcasted_iota(jnp.int32, (bt, 1, LANES), 0)
    l_idx = lax.broadcasted_iota(jnp.int32, (bt, 1, LANES), 2)
    pair = ((l_idx & (bt - 1)) == b_idx) & (l_idx < R)
    own = (l_idx == b_idx)
    y_off_parts = []
    for g in range(SSD_GROUPS):
        gl = slice(g * half, (g + 1) * half)
        h0 = st_ref[:, gl, :]
        cg = _pad_rows(cm[:, g * SSD_STATE:(g + 1) * SSD_STATE], LANES)
        zz = _dot_nt(h0.reshape(bt * half, SSD_STATE), cg).reshape(bt, half, LANES)
        yt = jnp.sum(jnp.where(pair, zz, 0.0), axis=0)
        y_off_parts.append(yt.T[0:R, :])
        wt = _pad_rows(w_end[:, gl], LANES).T
        lhs = jnp.where(pair, wt[None], 0.0).reshape(bt * half, LANES)
        bg = _pad_rows(bm[:, g * SSD_STATE:(g + 1) * SSD_STATE], LANES)
        contrib = _dot(lhs, bg).reshape(bt, half, SSD_STATE)
        dec_t = _pad_rows(dec_e[:, gl], LANES).T
        dec = jnp.sum(jnp.where(own, dec_t[None], 0.0), axis=-1, keepdims=True)
        ssm_ref[:, gl, :] = h0 * dec + contrib
    y_off = jnp.concatenate(y_off_parts, axis=-1) * ecs_e
    y = y_intra + y_off + xs * dskip_ref[...]
    y_ssd = _gated_group_norm(y, z, gssd_ref[...])

    HD = ATT_HEAD_DIM
    GH = ATT_HEADS // ATT_KV_HEADS
    GR = GH * R
    reps = GR // bt
    ridx = lax.broadcasted_iota(jnp.int32, (GR, 1), 0)
    rb = ridx % bt
    rt = (ridx // bt) % steps
    jcol = lax.broadcasted_iota(jnp.int32, (GR, WINDOW), 1)
    in_window = jcol > rt
    mix = jnp.dot(y_ssd.astype(bf16), wout_ref[0:SSD_WIDTH, :], preferred_element_type=f32)
    for g in range(ATT_KV_HEADS):
        qg = jnp.concatenate([q[:, (g * GH + hl) * HD:(g * GH + hl + 1) * HD] for hl in range(GH)],
                             axis=0)
        qg_b = qg.astype(bf16)
        s_c = jnp.zeros((GR, WINDOW), f32)
        for b in range(bt):
            s_c = jnp.where(rb == b, _dot(qg_b, ck_ref[b, g]), s_c)
        s_c = jnp.where(in_window, s_c, -jnp.inf)
        sink = sinkcol_ref[g * GR:(g + 1) * GR, :]
        m = jnp.maximum(jnp.max(s_c, axis=-1, keepdims=True), sink)
        s_n = []
        for t2 in range(steps):
            kt = jnp.concatenate([slab(k_new, t2)[:, g * HD:(g + 1) * HD]] * reps, axis=0)
            sn = jnp.where(rt >= t2, jnp.sum(qg * kt, axis=-1, keepdims=True), -jnp.inf)
            s_n.append(sn)
            m = jnp.maximum(m, sn)
        e_c = jnp.exp(s_c - m)
        e_n = [jnp.exp(sn - m) for sn in s_n]
        den = jnp.sum(e_c, axis=-1, keepdims=True) + jnp.exp(sink - m)
        for en in e_n:
            den = den + en
        rinv = 1.0 / den
        p_c = (e_c * rinv).astype(bf16)
        o = jnp.zeros((GR, HD), f32)
        for b in range(bt):
            o = jnp.where(rb == b, _dot_nt(p_c, cv_ref[b, g]), o)
        for t2 in range(steps):
            vt = jnp.concatenate([slab(v_new, t2)[:, g * HD:(g + 1) * HD]] * reps, axis=0)
            o = o + (e_n[t2] * rinv) * vt
        for hl in range(GH):
            hd = g * GH + hl
            mix = mix + jnp.dot(o[hl * R:(hl + 1) * R].astype(bf16),
                                wout_ref[SSD_WIDTH + hd * HD:SSD_WIDTH + (hd + 1) * HD, :],
                                preferred_element_type=f32)
    y_ref[...] = (x + _rmsnorm(mix, gpost_ref[...])).reshape(steps, bt, D_MODEL)

    keep = WINDOW - steps
    kn_t = _pad_rows(k_new, LANES).T
    vn_t = _pad_rows(v_new, LANES).T
    sel_r = lax.broadcasted_iota(jnp.int32, (LANES, WINDOW), 0)
    sel_l = lax.broadcasted_iota(jnp.int32, (LANES, WINDOW), 1)
    lane_w = lax.broadcasted_iota(jnp.int32, (1, WINDOW), 1)
    for b in range(bt):
        sel = ((sel_r % bt == b) & (sel_r < R) & (sel_l - keep == sel_r // bt)).astype(bf16)
        new_k = _dot_x2(kn_t, sel)
        new_v = _dot_x2(vn_t, sel)
        for g in range(ATT_KV_HEADS):
            gs = slice(g * HD, (g + 1) * HD)
            wk_ref[b, g] = jnp.where(lane_w < keep, pltpu.roll(ck_ref[b, g], keep, 1), new_k[gs])
            wv_ref[b, g] = jnp.where(lane_w < keep, pltpu.roll(cv_ref[b, g], keep, 1), new_v[gs])


def _sample_mixer(x_tm, cprev_tm, st, ck, cv, gpre, win, convw, convb, dtb, alog, dskip_e, gssd,
                  sinkcol, wout, gpost, bt):
    steps, nb, D = x_tm.shape
    kern = functools.partial(_sample_mixer_kernel, bt=bt, steps=steps)
    tm = lambda w: pl.BlockSpec((steps, bt, w), lambda i: (0, i, 0))
    win_spec = pl.BlockSpec((bt, ATT_KV_HEADS, ATT_HEAD_DIM, WINDOW), lambda i: (i, 0, 0, 0))
    in_specs = [
        tm(D),
        pl.BlockSpec((SSD_CONV - 1, bt, SSD_CONV_DIM), lambda i: (0, i, 0)),
        pl.BlockSpec((bt, SSD_WIDTH, SSD_STATE), lambda i: (i, 0, 0)),
        win_spec, win_spec,
    ] + [_const_spec(a.shape) for a in (gpre, win, convw, convb, dtb, alog, dskip_e, gssd,
                                        sinkcol, wout, gpost)]
    out_specs = (
        tm(D),
        pl.BlockSpec((bt, SSD_WIDTH, SSD_STATE), lambda i: (i, 0, 0)),
        pl.BlockSpec((SSD_CONV - 1, bt, SSD_CONV_DIM), lambda i: (0, i, 0)),
        win_spec, win_spec,
    )
    out_shape = (
        jax.ShapeDtypeStruct((steps, nb, D), f32),
        jax.ShapeDtypeStruct((nb, SSD_WIDTH, SSD_STATE), f32),
        jax.ShapeDtypeStruct((SSD_CONV - 1, nb, SSD_CONV_DIM), f32),
        jax.ShapeDtypeStruct(ck.shape, f32),
        jax.ShapeDtypeStruct(cv.shape, f32),
    )
    return pl.pallas_call(
        kern, grid=(nb // bt,), in_specs=in_specs, out_specs=out_specs, out_shape=out_shape,
        name="sample_mixer",
        compiler_params=pltpu.CompilerParams(
            dimension_semantics=("arbitrary",), vmem_limit_bytes=VMEM_LIMIT),
    )(x_tm, cprev_tm, st, ck, cv, gpre, win, convw, convb, dtb, alog, dskip_e, gssd, sinkcol,
      wout, gpost)


def _sample_xattn_kernel(x_ref, mk_ref, mv_ref, gpre_ref, wq_ref, wo_ref, gpost_ref, y_ref,
                         *, bt, steps):
    R = steps * bt
    nrow = bt * N_MEM * X_HEADS
    x = x_ref[...].reshape(R, D_MODEL)
    hn = _rmsnorm(x, gpre_ref[...]).astype(bf16)
    q = jnp.dot(hn, wq_ref[...], preferred_element_type=f32)
    qs = jnp.concatenate([q[:, hd * X_HEAD_DIM:(hd + 1) * X_HEAD_DIM] for hd in range(X_HEADS)],
                         axis=0)
    kall = mk_ref[...].reshape(nrow, X_HEAD_DIM)
    vall = mv_ref[...].reshape(nrow, X_HEAD_DIM)
    ncol = X_HEADS * R
    seq_rows = N_MEM * X_HEADS
    z = _dot_nt(kall, qs).reshape(bt, seq_rows, ncol)
    b_i = lax.broadcasted_iota(jnp.int32, (bt, 1, ncol), 0)
    c_i = lax.broadcasted_iota(jnp.int32, (bt, 1, ncol), 2)
    zc = jnp.sum(jnp.where(c_i % bt == b_i, z, 0.0), axis=0)
    zc = zc.reshape(seq_rows // SUBLANES, SUBLANES, ncol)
    r_h = lax.broadcasted_iota(jnp.int32, (1, SUBLANES, ncol), 1) % X_HEADS
    c_h = lax.broadcasted_iota(jnp.int32, (1, SUBLANES, ncol), 2) // R
    zc = jnp.where(r_h == c_h, zc, -jnp.inf).reshape(seq_rows, ncol)
    m = jnp.max(zc, axis=0, keepdims=True)
    e = jnp.exp(zc - m)
    p = e * (1.0 / jnp.sum(e, axis=0, keepdims=True))
    col_b = lax.broadcasted_iota(jnp.int32, (1, ncol), 1) % bt
    tn = (((0,), (0,)), ((), ()))
    o = None
    for b in range(bt):
        p_b = jnp.where(col_b == b, p, 0.0).astype(bf16)
        o_b = lax.dot_general(p_b, vall[b * seq_rows:(b + 1) * seq_rows].astype(bf16), tn,
                              preferred_element_type=f32)
        o = o_b if o is None else o + o_b
    o = jnp.concatenate([o[hd * R:(hd + 1) * R] for hd in range(X_HEADS)], axis=-1)
    cc = _dot(o, wo_ref[...])
    y_ref[...] = (x + _rmsnorm(cc, gpost_ref[...])).reshape(steps, bt, D_MODEL)


def _sample_xattn(x_tm, mk, mv, gpre, wq, wo, gpost, bt):
    steps, nb, D = x_tm.shape
    kern = functools.partial(_sample_xattn_kernel, bt=bt, steps=steps)
    xs = pl.BlockSpec((steps, bt, D), lambda i: (0, i, 0))
    ms = pl.BlockSpec((bt, N_MEM, X_HEADS, X_HEAD_DIM), lambda i: (i, 0, 0, 0))
    return pl.pallas_call(
        kern, grid=(nb // bt,),
        in_specs=[xs, ms, ms, _const_spec(gpre.shape), _const_spec(wq.shape),
                  _const_spec(wo.shape), _const_spec(gpost.shape)],
        out_specs=xs, out_shape=jax.ShapeDtypeStruct((steps, nb, D), f32),
        name="sample_xattn",
        compiler_params=pltpu.CompilerParams(
            dimension_semantics=("arbitrary",), vmem_limit_bytes=VMEM_LIMIT),
    )(x_tm, mk, mv, gpre, wq, wo, gpost)


def _row(v, width=None):
    v = v.reshape(1, -1).astype(f32)
    if width is not None and v.shape[1] < width:
        v = jnp.pad(v, ((0, 0), (0, width - v.shape[1])))
    return v


def kernel(x_prompt, x_sample, state_ssm, state_conv, cache_win_k, cache_win_v, cache_mem_k, cache_mem_v, mem_prompt, g_mix_pre, w_in, conv_w, conv_b, dt_bias, a_log, d_skip, g_ssd_norm, sinks, w_out, g_mix_post, g_x_pre, w_xq, g_mem, w_xk, w_xv, w_xo, g_x_post, g_ffn_pre, w_gate, w_up, w_down, g_ffn_post):
    depth = w_in.shape[0]
    assert depth == 1
    B, L, D = x_prompt.shape
    NB, steps, _ = x_sample.shape
    li = 0

    w = w_in[li]
    wz, wxbc, wdt = w[:, 0:512], w[:, 512:1536], w[:, 1536:1544]
    wq, wk, wv = w[:, 1544:2056], w[:, 2056:2184], w[:, 2184:2312]
    wdt_p = jnp.pad(wdt, ((0, 0), (0, LANES - SSD_HEADS)))
    win_p = jnp.concatenate([wz, wxbc, wq * ATT_SCALE, wk, wv, wdt_p], axis=1).astype(bf16)
    wo_b = w_out[li].astype(bf16)

    gpre, gpost = _row(g_mix_pre[li]), _row(g_mix_post[li])
    convw, convb = conv_w[li].astype(f32), _row(conv_b[li])
    dtb, alog = _row(dt_bias[li], LANES), _row(a_log[li], LANES)
    dskip_e = _row(jnp.repeat(d_skip[li], SSD_HEAD_DIM))
    gssd = _row(g_ssd_norm[li])
    sk = sinks[li].astype(f32)

    mk2d, mv2d = _memkv(mem_prompt.reshape(B * N_MEM, D), _row(g_mem[li]),
                        w_xk[li].astype(bf16), w_xv[li].astype(bf16), tile=512)
    mk3, mv3 = mk2d.reshape(B, N_MEM, D), mv2d.reshape(B, N_MEM, D)
    x1, p_ssm, p_conv8, p_wk, p_wv = _prompt_mixer(
        x_prompt, gpre, win_p, convw, convb, dtb, alog, dskip_e, gssd, sk, wo_b, gpost, tile=512)
    wxq_b, wxo_b = (w_xq[li] * X_SCALE).astype(bf16), w_xo[li].astype(bf16)
    gxpre, gxpost = _row(g_x_pre[li]), _row(g_x_post[li])
    x2 = _prompt_xattn(x1, gxpre, wxq_b, mk3, mv3, wxo_b, gxpost, tile=512)
    wg_b, wu_b, wd_b = w_gate[li].astype(bf16), w_up[li].astype(bf16), w_down[li].astype(bf16)
    gfpre, gfpost = _row(g_ffn_pre[li]), _row(g_ffn_post[li])
    yp = _ffn(x2.reshape(B * L, D), gfpre, wg_b, wu_b, wd_b, gfpost, tile=512).reshape(B, L, D)

    bt = 8
    x_tm = jnp.transpose(x_sample, (1, 0, 2))
    cprev_tm = jnp.transpose(state_conv[li], (1, 0, 2))
    st = state_ssm[li].reshape(NB, SSD_WIDTH, SSD_STATE)
    ck = jnp.transpose(cache_win_k[li], (0, 2, 3, 1))
    cv = jnp.transpose(cache_win_v[li], (0, 2, 3, 1))
    sinkcol = jnp.repeat(sk, steps * bt).reshape(ATT_HEADS * steps * bt, 1)
    x1s, s_ssm, cnew_tm, s_wk, s_wv = _sample_mixer(
        x_tm, cprev_tm, st, ck, cv, gpre, win_p, convw, convb, dtb, alog, dskip_e, gssd,
        sinkcol, wo_b, gpost, bt=bt)
    cmk = cache_mem_k.reshape(NB, N_MEM, X_HEADS, X_HEAD_DIM)
    cmv = cache_mem_v.reshape(NB, N_MEM, X_HEADS, X_HEAD_DIM)
    x2s = _sample_xattn(x1s, cmk, cmv, gxpre, wxq_b, wxo_b, gxpost, bt=bt)
    ys_tm = _ffn(x2s.reshape(steps * NB, D), gfpre, wg_b, wu_b, wd_b, gfpost, tile=steps * NB)
    ys = jnp.transpose(ys_tm.reshape(steps, NB, D), (1, 0, 2))

    s_conv = jnp.transpose(cnew_tm, (1, 0, 2))
    kv_shape = (ATT_KV_HEADS, ATT_HEAD_DIM)
    return (
        yp, ys,
        p_ssm.reshape(1, B, SSD_HEADS, SSD_HEAD_DIM, SSD_STATE),
        p_conv8[:, SUBLANES - (SSD_CONV - 1):, :][None],
        jnp.transpose(p_wk.reshape(B, *kv_shape, WINDOW), (0, 3, 1, 2))[None],
        jnp.transpose(p_wv.reshape(B, *kv_shape, WINDOW), (0, 3, 1, 2))[None],
        mk3.reshape(1, B, N_MEM, X_HEADS, X_HEAD_DIM), mv3.reshape(1, B, N_MEM, X_HEADS, X_HEAD_DIM),
        s_ssm.reshape(1, NB, SSD_HEADS, SSD_HEAD_DIM, SSD_STATE),
        s_conv[None],
        jnp.transpose(s_wk, (0, 3, 1, 2))[None], jnp.transpose(s_wv, (0, 3, 1, 2))[None],
    )
```

```python
import functools

import jax
import jax.numpy as jnp
from jax import lax
from jax.experimental import pallas as pl
from jax.experimental.pallas import tpu as pltpu

f32 = jnp.float32
bf16 = jnp.bfloat16

D_MODEL = 1024
EPS = 1e-6
N_MEM = 256
SSD_HEADS = 8
SSD_HEAD_DIM = 64
SSD_WIDTH = 512
SSD_GROUPS = 2
SSD_STATE = 128
SSD_CONV = 4
SSD_CHUNK = 128
SSD_CONV_DIM = 1024
ATT_HEADS = 8
ATT_KV_HEADS = 2
ATT_HEAD_DIM = 64
ATT_WIDTH = 512
ATT_KV_WIDTH = 128
WINDOW = 128
ATT_SCALE = ATT_HEAD_DIM ** -0.5
X_HEADS = 4
X_HEAD_DIM = 256
X_SCALE = X_HEAD_DIM ** -0.5
D_FF = 2816
LANES = 128
SUBLANES = 8
VMEM_LIMIT = 56 * 1024 * 1024
CHUNKS_PER_STEP = 4

P_Z, P_XBC, P_Q, P_K, P_V, P_DT, P_END = 0, 512, 1536, 2048, 2176, 2304, 2432


def _dot(a, b):
    return jnp.dot(a.astype(bf16), b.astype(bf16), preferred_element_type=f32)


def _dot_nt(a, b):
    return lax.dot_general(a.astype(bf16), b.astype(bf16), (((1,), (1,)), ((), ())),
                           preferred_element_type=f32)


def _split2(x):
    hi = x.astype(bf16)
    lo = (x - hi.astype(f32)).astype(bf16)
    return hi, lo


def _dot_x2(x, m):
    hi, lo = _split2(x)
    return (jnp.dot(hi, m, preferred_element_type=f32)
            + jnp.dot(lo, m, preferred_element_type=f32))


def _rmsnorm(x, g):
    ms = jnp.mean(x * x, axis=-1, keepdims=True)
    return x * lax.rsqrt(ms + EPS) * g


def _silu(x):
    return x * jax.nn.sigmoid(x)


def _softplus(x):
    return jnp.maximum(x, 0.0) + jnp.log1p(jnp.exp(-jnp.abs(x)))


def _head_expand_matrix():
    r = lax.broadcasted_iota(jnp.int32, (LANES, SSD_WIDTH), 0)
    c = lax.broadcasted_iota(jnp.int32, (LANES, SSD_WIDTH), 1)
    return (r == (c >> 6)).astype(bf16)


def _gated_group_norm(y, z, g):
    u = y * _silu(z)
    half = SSD_WIDTH // SSD_GROUPS
    parts = []
    for gi in range(SSD_GROUPS):
        ug = u[:, gi * half:(gi + 1) * half]
        parts.append(ug * lax.rsqrt(jnp.mean(ug * ug, axis=-1, keepdims=True) + EPS))
    return jnp.concatenate(parts, axis=-1) * g


def _prompt_mixer_kernel(x_ref, gpre_ref, win_ref, convw_ref, convb_ref, dtb_ref, alog_ref,
                         dskip_ref, gssd_ref, sinks_ref, wout_ref, gpost_ref,
                         y_ref, ssm_ref, conv_ref, wk_ref, wv_ref,
                         statet_sc, xbc_ext_sc, xbc_sc, z_sc, q_sc, k_sc, v_sc, dt_sc,
                         kprev_sc, vprev_sc, mix_sc, *, tile):
    i = pl.program_id(1)
    nchunk = tile // SSD_CHUNK
    C = SSD_CHUNK

    @pl.when(i == 0)
    def _init():
        statet_sc[...] = jnp.zeros_like(statet_sc)
        xbc_ext_sc[0:SUBLANES, :] = jnp.zeros((SUBLANES, SSD_CONV_DIM), f32)
        kprev_sc[...] = jnp.zeros_like(kprev_sc)
        vprev_sc[...] = jnp.zeros_like(vprev_sc)

    x = x_ref[0]
    h = _rmsnorm(x, gpre_ref[...]).astype(bf16)
    z_sc[...] = jnp.dot(h, win_ref[:, P_Z:P_XBC], preferred_element_type=f32)
    xbc_ext_sc[SUBLANES:SUBLANES + tile, :] = jnp.dot(h, win_ref[:, P_XBC:P_Q],
                                                      preferred_element_type=f32)
    q_sc[...] = jnp.dot(h, win_ref[:, P_Q:P_K], preferred_element_type=f32)
    k_sc[...] = jnp.dot(h, win_ref[:, P_K:P_V], preferred_element_type=f32)
    v_sc[...] = jnp.dot(h, win_ref[:, P_V:P_DT], preferred_element_type=f32)
    dt_raw = jnp.dot(h, win_ref[:, P_DT:P_END], preferred_element_type=f32)
    lane = lax.broadcasted_iota(jnp.int32, (1, LANES), 1)
    dt_sc[...] = jnp.where(lane < SSD_HEADS, _softplus(dt_raw + dtb_ref[...]), 0.0)

    acc = convb_ref[...]
    for j in range(SSD_CONV):
        off = SUBLANES - (SSD_CONV - 1) + j
        acc = acc + xbc_ext_sc[off:off + tile, :] * convw_ref[j:j + 1, :]
    xbc_sc[...] = _silu(acc)
    tail = xbc_ext_sc[tile:tile + SUBLANES, :]
    conv_ref[0] = tail
    xbc_ext_sc[0:SUBLANES, :] = tail

    a_row = -jnp.exp(alog_ref[...])
    expand = _head_expand_matrix()
    expand2 = jnp.concatenate([expand, expand], axis=0)
    row_i = lax.broadcasted_iota(jnp.int32, (C, C), 0)
    col_i = lax.broadcasted_iota(jnp.int32, (C, C), 1)
    lower = col_i <= row_i
    tri = lower.astype(bf16)
    tri3 = jnp.concatenate([tri, tri, tri], axis=1)
    lo_half = lane < ATT_HEAD_DIM
    half = SSD_WIDTH // SSD_GROUPS
    head_of_lane = lax.broadcasted_iota(jnp.int32, (1, half), 1) >> 6

    def chunk_group(cg, carry):
        NC = CHUNKS_PER_STEP
        ns = range(NC)
        rows = [pl.ds(pl.multiple_of((cg * NC + n) * C, C), C) for n in ns]
        GH = SSD_HEADS // SSD_GROUPS
        heads = range(ATT_HEADS)
        gs = range(SSD_GROUPS)
        zero_b = jnp.zeros((), bf16)

        xs = [xbc_sc[rows[n], 0:SSD_WIDTH] for n in ns]
        bm = [xbc_sc[rows[n], SSD_WIDTH:SSD_WIDTH + half] for n in ns]
        cm = [xbc_sc[rows[n], SSD_WIDTH + half:SSD_CONV_DIM] for n in ns]
        dtc = [dt_sc[rows[n], :] for n in ns]
        cs = []
        for n in ns:
            adt = dtc[n] * a_row
            a_hi = adt.astype(bf16)
            a_r1 = adt - a_hi.astype(f32)
            a_mid = a_r1.astype(bf16)
            a_lo = (a_r1 - a_mid.astype(f32)).astype(bf16)
            cs.append(jnp.dot(tri3, jnp.concatenate([a_hi, a_mid, a_lo], axis=0),
                              preferred_element_type=f32))

        q = [q_sc[rows[n], :].astype(bf16) for n in ns]
        k = [k_sc[rows[n], :] for n in ns]
        v = [v_sc[rows[n], :] for n in ns]
        k_prev = [kprev_sc[...]] + k[:-1]
        v_prev = [vprev_sc[...]] + v[:-1]
        kprev_sc[...] = k[-1]
        vprev_sc[...] = v[-1]
        first_bias = jnp.where((i * nchunk + cg * NC) > 0, 0.0, -jnp.inf)
        k_lo, k_hi, v_lo, v_hi = [], [], [], []
        for n in ns:
            kk = jnp.concatenate([k_prev[n], k[n]], axis=0)
            vv = jnp.concatenate([v_prev[n], v[n]], axis=0)
            kk_r = pltpu.roll(kk, ATT_HEAD_DIM, 1)
            vv_r = pltpu.roll(vv, ATT_HEAD_DIM, 1)
            k_lo.append([jnp.where(lo_half, kk, 0.0).astype(bf16), jnp.where(lo_half, kk_r, 0.0).astype(bf16)])
            k_hi.append([jnp.where(lo_half, 0.0, kk_r).astype(bf16), jnp.where(lo_half, 0.0, kk).astype(bf16)])
            v_lo.append([jnp.where(lo_half, vv, 0.0).astype(bf16), jnp.where(lo_half, vv_r, 0.0).astype(bf16)])
            v_hi.append([jnp.where(lo_half, 0.0, vv_r).astype(bf16), jnp.where(lo_half, 0.0, vv).astype(bf16)])
        s_g = [[_dot_nt(jnp.concatenate([q[n][:, (2 * g) * LANES:(2 * g + 1) * LANES],
                                         q[n][:, (2 * g + 1) * LANES:(2 * g + 2) * LANES]], axis=0),
                        jnp.concatenate([k_lo[n][g], k_hi[n][g]], axis=0))
                for g in range(ATT_KV_HEADS)] for n in ns]

        cs_t = [cs[n].T for n in ns]
        fac_e, cb_all, bm_t = [], [], []
        for n in ns:
            tot = cs[n][C - 1:C, :]
            fac = jnp.concatenate([dtc[n], jnp.exp(cs[n]), jnp.exp(tot - cs[n])], axis=0)
            f_hi = fac.astype(bf16)
            f_lo = (fac - f_hi.astype(f32)).astype(bf16)
            fac_e.append(jnp.dot(jnp.concatenate([f_hi, f_lo], axis=1), expand2,
                                 preferred_element_type=f32))
            cb_all.append(_dot_nt(
                jnp.concatenate([cm[n][:, 0:SSD_STATE], cm[n][:, SSD_STATE:half]], axis=0),
                jnp.concatenate([bm[n][:, 0:SSD_STATE], bm[n][:, SSD_STATE:half]], axis=0)))
            bm_t.append(bm[n].T)

        def head_scores(n, hd):
            g, jb, sub = hd // 4, (hd // 2) % 2, hd % 2
            s_prev = s_g[n][g][jb * C:(jb + 1) * C, (2 * sub) * C:(2 * sub + 1) * C]
            s_cur = s_g[n][g][jb * C:(jb + 1) * C, (2 * sub + 1) * C:(2 * sub + 2) * C]
            return jnp.where(lower, s_cur, s_prev + first_bias if n == 0 else s_prev)
        sc = [[head_scores(n, hd) for hd in heads] for n in ns]
        sink = [sinks_ref[hd] for hd in heads]
        m = [[jnp.maximum(jnp.max(sc[n][hd], axis=-1, keepdims=True), sink[hd]) for hd in heads]
             for n in ns]

        xdt = [xs[n] * fac_e[n][0:C] for n in ns]
        ecs_e = [fac_e[n][C:2 * C] for n in ns]
        w_end = [xdt[n] * fac_e[n][2 * C:3 * C] for n in ns]
        xdt_b = [xdt[n].astype(bf16) for n in ns]
        contrib = [[_dot(bm_t[n][g * SSD_STATE:(g + 1) * SSD_STATE, :],
                         w_end[n][:, g * half:(g + 1) * half]) for g in gs] for n in ns]
        decay = [[jnp.exp(jnp.where(lower, cs[n][:, hh:hh + 1] - cs_t[n][hh:hh + 1, :], -jnp.inf))
                  for hh in range(SSD_HEADS)] for n in ns]

        e = [[jnp.exp(sc[n][hd] - m[n][hd]) for hd in heads] for n in ns]
        den = [[jnp.sum(e[n][hd], axis=-1, keepdims=True) + jnp.exp(sink[hd] - m[n][hd])
                for hd in heads] for n in ns]

        y_d = []
        for n in ns:
            y_n = []
            for g in gs:
                gl = slice(g * half, (g + 1) * half)
                cb = cb_all[n][g * C:(g + 1) * C, g * C:(g + 1) * C]
                m_parts = [(cb * decay[n][g * GH + r]).astype(bf16) for r in range(GH)]
                x_parts = [jnp.where(head_of_lane == r, xdt_b[n][:, gl], zero_b) for r in range(GH)]
                y_n.append(jnp.dot(jnp.concatenate(m_parts, axis=1), jnp.concatenate(x_parts, axis=0),
                                   preferred_element_type=f32))
            y_d.append(y_n)
        st = [statet_sc[g] for g in gs]
        y_off = []
        for n in ns:
            y_off.append([_dot(cm[n][:, g * SSD_STATE:(g + 1) * SSD_STATE], st[g]) for g in gs])
            st = [st[g] * ecs_e[n][C - 1:C, g * half:(g + 1) * half] + contrib[n][g] for g in gs]
        for g in gs:
            statet_sc[g] = st[g]

        p = [[(e[n][hd] * (1.0 / den[n][hd])).astype(bf16) for hd in heads] for n in ns]
        o_g = []
        for n in ns:
            o_n = []
            for g in range(ATT_KV_HEADS):
                p_rows = []
                for jb in range(2):
                    p_cols = []
                    for sub in range(2):
                        ph = p[n][g * 4 + jb * 2 + sub]
                        p_cols += [jnp.where(lower, zero_b, ph), jnp.where(lower, ph, zero_b)]
                    p_rows.append(jnp.concatenate(p_cols, axis=1))
                o_n.append(jnp.dot(jnp.concatenate(p_rows, axis=0),
                                   jnp.concatenate([v_lo[n][g], v_hi[n][g]], axis=0),
                                   preferred_element_type=f32))
            o_g.append(o_n)

        for n in ns:
            y = (jnp.concatenate([y_d[n][g] + y_off[n][g] * ecs_e[n][:, g * half:(g + 1) * half]
                                  for g in gs], axis=-1)
                 + xs[n] * dskip_ref[...])
            y_ssd = _gated_group_norm(y, z_sc[rows[n], :], gssd_ref[...])
            mix_sc[rows[n], 0:SSD_WIDTH] = y_ssd.astype(bf16)
        for n in ns:
            for g in range(ATT_KV_HEADS):
                for jb in range(2):
                    lo_l = SSD_WIDTH + (2 * g + jb) * LANES
                    mix_sc[rows[n], lo_l:lo_l + LANES] = o_g[n][g][jb * C:(jb + 1) * C].astype(bf16)
        return carry

    lax.fori_loop(0, nchunk // CHUNKS_PER_STEP, chunk_group, 0)

    @pl.when(i == pl.num_programs(1) - 1)
    def _final_state():
        for g in range(SSD_GROUPS):
            ssm_ref[0, g * half:(g + 1) * half, :] = statet_sc[g].T
        wk_ref[0] = k_sc[tile - WINDOW:tile, :].T
        wv_ref[0] = v_sc[tile - WINDOW:tile, :].T

    mix = jnp.dot(mix_sc[...], wout_ref[...], preferred_element_type=f32)
    y_ref[0] = x + _rmsnorm(mix, gpost_ref[...])


def _const_spec(shape):
    nd = len(shape)
    return pl.BlockSpec(shape, lambda *_: (0,) * nd)


def _prompt_mixer(x, gpre, win, convw, convb, dtb, alog, dskip_e, gssd, sinks, wout, gpost, tile):
    B, L, D = x.shape
    grid = (B, L // tile)
    kern = functools.partial(_prompt_mixer_kernel, tile=tile)
    out_shape = (
        jax.ShapeDtypeStruct((B, L, D), f32),
        jax.ShapeDtypeStruct((B, SSD_WIDTH, SSD_STATE), f32),
        jax.ShapeDtypeStruct((B, SUBLANES, SSD_CONV_DIM), f32),
        jax.ShapeDtypeStruct((B, WINDOW, ATT_KV_WIDTH), f32),
        jax.ShapeDtypeStruct((B, WINDOW, ATT_KV_WIDTH), f32),
    )
    in_specs = [
        pl.BlockSpec((1, tile, D), lambda b, i: (b, i, 0)),
        _const_spec(gpre.shape), _const_spec(win.shape), _const_spec(convw.shape),
        _const_spec(convb.shape), _const_spec(dtb.shape), _const_spec(alog.shape),
        _const_spec(dskip_e.shape), _const_spec(gssd.shape),
        pl.BlockSpec(memory_space=pltpu.SMEM),
        _const_spec(wout.shape), _const_spec(gpost.shape),
    ]
    out_specs = (
        pl.BlockSpec((1, tile, D), lambda b, i: (b, i, 0)),
        pl.BlockSpec((1, SSD_WIDTH, SSD_STATE), lambda b, i: (b, 0, 0)),
        pl.BlockSpec((1, SUBLANES, SSD_CONV_DIM), lambda b, i: (b, 0, 0)),
        pl.BlockSpec((1, WINDOW, ATT_KV_WIDTH), lambda b, i: (b, 0, 0)),
        pl.BlockSpec((1, WINDOW, ATT_KV_WIDTH), lambda b, i: (b, 0, 0)),
    )
    scratch = [
        pltpu.VMEM((SSD_GROUPS, SSD_STATE, SSD_WIDTH // SSD_GROUPS), f32),
        pltpu.VMEM((tile + 2 * SUBLANES, SSD_CONV_DIM), f32),
        pltpu.VMEM((tile, SSD_CONV_DIM), f32),
        pltpu.VMEM((tile, SSD_WIDTH), f32),
        pltpu.VMEM((tile, ATT_WIDTH), f32),
        pltpu.VMEM((tile, ATT_KV_WIDTH), f32),
        pltpu.VMEM((tile, ATT_KV_WIDTH), f32),
        pltpu.VMEM((tile, LANES), f32),
        pltpu.VMEM((WINDOW, ATT_KV_WIDTH), f32),
        pltpu.VMEM((WINDOW, ATT_KV_WIDTH), f32),
        pltpu.VMEM((tile, 2 * SSD_WIDTH), bf16),
    ]
    return pl.pallas_call(
        kern, grid=grid, in_specs=in_specs, out_specs=out_specs, out_shape=out_shape,
        scratch_shapes=scratch, name="prompt_mixer",
        compiler_params=pltpu.CompilerParams(
            dimension_semantics=("arbitrary", "arbitrary"), vmem_limit_bytes=VMEM_LIMIT),
    )(x, gpre, win, convw, convb, dtb, alog, dskip_e, gssd, sinks, wout, gpost)


def _memkv_kernel(m_ref, g_ref, wk_ref, wv_ref, k_ref, v_ref, kh_ref, vh_ref):
    mn = _rmsnorm(m_ref[...], g_ref[...]).astype(bf16)
    k = jnp.dot(mn, wk_ref[...], preferred_element_type=f32)
    v = jnp.dot(mn, wv_ref[...], preferred_element_type=f32)
    k_ref[...] = k
    v_ref[...] = v
    for hd in range(X_HEADS):
        kh_ref[:, hd, :] = k[:, hd * X_HEAD_DIM:(hd + 1) * X_HEAD_DIM]
        vh_ref[:, hd, :] = v[:, hd * X_HEAD_DIM:(hd + 1) * X_HEAD_DIM]


def _memkv(mem2d, g, wk, wv, tile):
    n, d = mem2d.shape
    row = pl.BlockSpec((tile, d), lambda i: (i, 0))
    hrow = pl.BlockSpec((tile, X_HEADS, X_HEAD_DIM), lambda i: (i, 0, 0))
    flat = jax.ShapeDtypeStruct((n, d), f32)
    heads = jax.ShapeDtypeStruct((n, X_HEADS, X_HEAD_DIM), f32)
    return pl.pallas_call(
        _memkv_kernel, grid=(n // tile,),
        in_specs=[row, _const_spec(g.shape), _const_spec(wk.shape), _const_spec(wv.shape)],
        out_specs=(row, row, hrow, hrow),
        out_shape=(flat, flat, heads, heads),
        name="memory_kv",
        compiler_params=pltpu.CompilerParams(
            dimension_semantics=("arbitrary",), vmem_limit_bytes=VMEM_LIMIT),
    )(mem2d, g, wk, wv)


def _prompt_xattn_kernel(x_ref, gpre_ref, wq_ref, mk_ref, mv_ref, wo_ref, gpost_ref, y_ref):
    x = x_ref[0]
    hn = _rmsnorm(x, gpre_ref[...]).astype(bf16)
    q = jnp.dot(hn, wq_ref[...], preferred_element_type=f32)
    hs = range(X_HEADS)
    sl = [slice(hd * X_HEAD_DIM, (hd + 1) * X_HEAD_DIM) for hd in hs]
    s = [_dot_nt(q[:, sl[hd]], mk_ref[0, :, sl[hd]]) for hd in hs]
    m = [jnp.max(s[hd], axis=-1, keepdims=True) for hd in hs]
    e = [jnp.exp(s[hd] - m[hd]) for hd in hs]
    r = [1.0 / jnp.sum(e[hd], axis=-1, keepdims=True) for hd in hs]
    o = jnp.concatenate([_dot(e[hd] * r[hd], mv_ref[0, :, sl[hd]]) for hd in hs], axis=-1)
    c = _dot(o, wo_ref[...])
    y_ref[0] = x + _rmsnorm(c, gpost_ref[...])


def _prompt_xattn(x, gpre, wq, mk, mv, wo, gpost, tile):
    B, L, D = x.shape
    xs = pl.BlockSpec((1, tile, D), lambda b, i: (b, i, 0))
    ms = pl.BlockSpec((1, N_MEM, D), lambda b, i: (b, 0, 0))
    return pl.pallas_call(
        _prompt_xattn_kernel, grid=(B, L // tile),
        in_specs=[xs, _const_spec(gpre.shape), _const_spec(wq.shape), ms, ms,
                  _const_spec(wo.shape), _const_spec(gpost.shape)],
        out_specs=xs, out_shape=jax.ShapeDtypeStruct((B, L, D), f32),
        name="prompt_xattn",
        compiler_params=pltpu.CompilerParams(
            dimension_semantics=("arbitrary", "arbitrary"), vmem_limit_bytes=VMEM_LIMIT),
    )(x, gpre, wq, mk, mv, wo, gpost)


def _ffn_kernel(x_ref, gpre_ref, wg_ref, wu_ref, wd_ref, gpost_ref, y_ref):
    x = x_ref[...]
    hf = _rmsnorm(x, gpre_ref[...]).astype(bf16)
    gate = jnp.dot(hf, wg_ref[...], preferred_element_type=f32)
    up = jnp.dot(hf, wu_ref[...], preferred_element_type=f32)
    act = (_silu(gate) * up).astype(bf16)
    f = jnp.dot(act, wd_ref[...], preferred_element_type=f32)
    y_ref[...] = x + _rmsnorm(f, gpost_ref[...])


def _ffn(x2d, gpre, wg, wu, wd, gpost, tile):
    n, d = x2d.shape
    row = pl.BlockSpec((tile, d), lambda i: (i, 0))
    return pl.pallas_call(
        _ffn_kernel, grid=(n // tile,),
        in_specs=[row, _const_spec(gpre.shape), _const_spec(wg.shape), _const_spec(wu.shape),
                  _const_spec(wd.shape), _const_spec(gpost.shape)],
        out_specs=row, out_shape=jax.ShapeDtypeStruct((n, d), f32),
        name="ffn",
        compiler_params=pltpu.CompilerParams(
            dimension_semantics=("arbitrary",), vmem_limit_bytes=VMEM_LIMIT),
    )(x2d, gpre, wg, wu, wd, gpost)


def _pad_rows(a, rows):
    if a.shape[0] == rows:
        return a
    return jnp.concatenate([a, jnp.zeros((rows - a.shape[0], a.shape[1]), a.dtype)], axis=0)


def _sample_mixer_kernel(x_ref, cprev_ref, st_ref, ck_ref, cv_ref,
                         gpre_ref, win_ref, convw_ref, convb_ref, dtb_ref, alog_ref,
                         dskip_ref, gssd_ref, sinkcol_ref, wout_ref, gpost_ref,
                         y_ref, ssm_ref, cnew_ref, wk_ref, wv_ref, *, bt, steps):
    R = steps * bt
    half = SSD_WIDTH // SSD_GROUPS
    x = x_ref[...].reshape(R, D_MODEL)
    h = _rmsnorm(x, gpre_ref[...]).astype(bf16)
    z = jnp.dot(h, win_ref[:, P_Z:P_XBC], preferred_element_type=f32)
    u = jnp.dot(h, win_ref[:, P_XBC:P_Q], preferred_element_type=f32)
    q = jnp.dot(h, win_ref[:, P_Q:P_K], preferred_element_type=f32)
    k_new = jnp.dot(h, win_ref[:, P_K:P_V], preferred_element_type=f32)
    v_new = jnp.dot(h, win_ref[:, P_V:P_DT], preferred_element_type=f32)
    dt_raw = jnp.dot(h, win_ref[:, P_DT:P_END], preferred_element_type=f32)
    lane = lax.broadcasted_iota(jnp.int32, (1, LANES), 1)
    dt = jnp.where(lane < SSD_HEADS, _softplus(dt_raw + dtb_ref[...]), 0.0)

    def slab(a, t):
        return a[t * bt:(t + 1) * bt]

    hist = [cprev_ref[j] for j in range(SSD_CONV - 1)] + [slab(u, t) for t in range(steps)]
    xbc_t = []
    for t in range(steps):
        acc = convb_ref[...]
        for j in range(SSD_CONV):
            acc = acc + hist[t + j] * convw_ref[j:j + 1, :]
        xbc_t.append(_silu(acc))
    for j in range(SSD_CONV - 1):
        cnew_ref[j] = hist[steps + j]
    xbc = jnp.concatenate(xbc_t, axis=0)
    xs = xbc[:, 0:SSD_WIDTH]
    bm = xbc[:, SSD_WIDTH:SSD_WIDTH + half]
    cm = xbc[:, SSD_WIDTH + half:SSD_CONV_DIM]

    a_row = -jnp.exp(alog_ref[...])
    adt = dt * a_row
    cs_t = [slab(adt, 0)]
    for t in range(1, steps):
        cs_t.append(cs_t[-1] + slab(adt, t))
    cs = jnp.concatenate(cs_t, axis=0)
    tot = cs_t[-1]
    tot_rows = jnp.concatenate([tot] * steps, axis=0)
    expand = _head_expand_matrix()
    expand2 = jnp.concatenate([expand, expand], axis=0)
    pairs = [(t, s2) for t in range(steps) for s2 in range(t)]
    fac = jnp.concatenate([dt, jnp.exp(cs), jnp.exp(tot_rows - cs), jnp.exp(tot)]
                          + [jnp.exp(cs_t[t] - cs_t[s2]) for t, s2 in pairs], axis=0)
    f_hi, f_lo = _split2(fac)
    fac_e = jnp.dot(jnp.concatenate([f_hi, f_lo], axis=1), expand2, preferred_element_type=f32)
    xdt = xs * fac_e[0:R]
    ecs_e = fac_e[R:2 * R]
    w_end = xdt * fac_e[2 * R:3 * R]
    dec_e = fac_e[3 * R:3 * R + bt]
    pair_decay = {pr: fac_e[3 * R + (n + 1) * bt:3 * R + (n + 2) * bt] for n, pr in enumerate(pairs)}

    gr = lax.broadcasted_iota(jnp.int32, (half, SSD_WIDTH), 0)
    gc = lax.broadcasted_iota(jnp.int32, (half, SSD_WIDTH), 1)
    gsum = ((gr >> 7) == (gc >> 8)).astype(bf16)
    gsum2 = jnp.concatenate([gsum, gsum], axis=0)
    cb_pairs = [(t, s2) for t in range(steps) for s2 in range(t + 1)]
    prod = jnp.concatenate([slab(cm, t) * slab(bm, s2) for t, s2 in cb_pairs], axis=0)
    c_hi, c_lo = _split2(prod)
    cb_e = jnp.dot(jnp.concatenate([c_hi, c_lo], axis=1), gsum2, preferred_element_type=f32)

    y_t = []
    for t in range(steps):
        acc = None
        for s2 in range(t + 1):
            n = cb_pairs.index((t, s2))
            coef = cb_e[n * bt:(n + 1) * bt]
            if s2 < t:
                coef = coef * pair_decay[(t, s2)]
            term = coef * slab(xdt, s2)
            acc = term if acc is None else acc + term
        y_t.append(acc)
    y_intra = jnp.concatenate(y_t, axis=0)

    b_idx = lax.broadcasted_iota(jnp.int32, (bt, 1, LANES), 0)
    l_idx = lax.broadcasted_iota(jnp.int32, (bt, 1, LANES), 2)
    pair = ((l_idx & (bt - 1)) == b_idx) & (l_idx < R)
    own = (l_idx == b_idx)
    y_off_parts = []
    for g in range(SSD_GROUPS):
        gl = slice(g * half, (g + 1) * half)
        h0 = st_ref[:, gl, :]
        cg = _pad_rows(cm[:, g * SSD_STATE:(g + 1) * SSD_STATE], LANES)
        zz = _dot_nt(h0.reshape(bt * half, SSD_STATE), cg).reshape(bt, half, LANES)
        yt = jnp.sum(jnp.where(pair, zz, 0.0), axis=0)
        y_off_parts.append(yt.T[0:R, :])
        wt = _pad_rows(w_end[:, gl], LANES).T
        lhs = jnp.where(pair, wt[None], 0.0).reshape(bt * half, LANES)
        bg = _pad_rows(bm[:, g * SSD_STATE:(g + 1) * SSD_STATE], LANES)
        contrib = _dot(lhs, bg).reshape(bt, half, SSD_STATE)
        dec_t = _pad_rows(dec_e[:, gl], LANES).T
        dec = jnp.sum(jnp.where(own, dec_t[None], 0.0), axis=-1, keepdims=True)
        ssm_ref[:, gl, :] = h0 * dec + contrib
    y_off = jnp.concatenate(y_off_parts, axis=-1) * ecs_e
    y = y_intra + y_off + xs * dskip_ref[...]
    y_ssd = _gated_group_norm(y, z, gssd_ref[...])

    HD = ATT_HEAD_DIM
    GH = ATT_HEADS // ATT_KV_HEADS
    GR = GH * R
    reps = GR // bt
    ridx = lax.broadcasted_iota(jnp.int32, (GR, 1), 0)
    rb = ridx % bt
    rt = (ridx // bt) % steps
    jcol = lax.broadcasted_iota(jnp.int32, (GR, WINDOW), 1)
    in_window = jcol > rt
    mix = jnp.dot(y_ssd.astype(bf16), wout_ref[0:SSD_WIDTH, :], preferred_element_type=f32)
    for g in range(ATT_KV_HEADS):
        qg = jnp.concatenate([q[:, (g * GH + hl) * HD:(g * GH + hl + 1) * HD] for hl in range(GH)],
                             axis=0)
        qg_b = qg.astype(bf16)
        s_c = jnp.zeros((GR, WINDOW), f32)
        for b in range(bt):
            s_c = jnp.where(rb == b, _dot(qg_b, ck_ref[b, g]), s_c)
        s_c = jnp.where(in_window, s_c, -jnp.inf)
        sink = sinkcol_ref[g * GR:(g + 1) * GR, :]
        m = jnp.maximum(jnp.max(s_c, axis=-1, keepdims=True), sink)
        s_n = []
        for t2 in range(steps):
            kt = jnp.concatenate([slab(k_new, t2)[:, g * HD:(g + 1) * HD]] * reps, axis=0)
            sn = jnp.where(rt >= t2, jnp.sum(qg * kt, axis=-1, keepdims=True), -jnp.inf)
            s_n.append(sn)
            m = jnp.maximum(m, sn)
        e_c = jnp.exp(s_c - m)
        e_n = [jnp.exp(sn - m) for sn in s_n]
        den = jnp.sum(e_c, axis=-1, keepdims=True) + jnp.exp(sink - m)
        for en in e_n:
            den = den + en
        rinv = 1.0 / den
        p_c = (e_c * rinv).astype(bf16)
        o = jnp.zeros((GR, HD), f32)
        for b in range(bt):
            o = jnp.where(rb == b, _dot_nt(p_c, cv_ref[b, g]), o)
        for t2 in range(steps):
            vt = jnp.concatenate([slab(v_new, t2)[:, g * HD:(g + 1) * HD]] * reps, axis=0)
            o = o + (e_n[t2] * rinv) * vt
        for hl in range(GH):
            hd = g * GH + hl
            mix = mix + jnp.dot(o[hl * R:(hl + 1) * R].astype(bf16),
                                wout_ref[SSD_WIDTH + hd * HD:SSD_WIDTH + (hd + 1) * HD, :],
                                preferred_element_type=f32)
    y_ref[...] = (x + _rmsnorm(mix, gpost_ref[...])).reshape(steps, bt, D_MODEL)

    keep = WINDOW - steps
    kn_t = _pad_rows(k_new, LANES).T
    vn_t = _pad_rows(v_new, LANES).T
    sel_r = lax.broadcasted_iota(jnp.int32, (LANES, WINDOW), 0)
    sel_l = lax.broadcasted_iota(jnp.int32, (LANES, WINDOW), 1)
    lane_w = lax.broadcasted_iota(jnp.int32, (1, WINDOW), 1)
    for b in range(bt):
        sel = ((sel_r % bt == b) & (sel_r < R) & (sel_l - keep == sel_r // bt)).astype(bf16)
        new_k = _dot_x2(kn_t, sel)
        new_v = _dot_x2(vn_t, sel)
        for g in range(ATT_KV_HEADS):
            gs = slice(g * HD, (g + 1) * HD)
            wk_ref[b, g] = jnp.where(lane_w < keep, pltpu.roll(ck_ref[b, g], keep, 1), new_k[gs])
            wv_ref[b, g] = jnp.where(lane_w < keep, pltpu.roll(cv_ref[b, g], keep, 1), new_v[gs])


def _sample_mixer(x_tm, cprev_tm, st, ck, cv, gpre, win, convw, convb, dtb, alog, dskip_e, gssd,
                  sinkcol, wout, gpost, bt):
    steps, nb, D = x_tm.shape
    kern = functools.partial(_sample_mixer_kernel, bt=bt, steps=steps)
    tm = lambda w: pl.BlockSpec((steps, bt, w), lambda i: (0, i, 0))
    win_spec = pl.BlockSpec((bt, ATT_KV_HEADS, ATT_HEAD_DIM, WINDOW), lambda i: (i, 0, 0, 0))
    in_specs = [
        tm(D),
        pl.BlockSpec((SSD_CONV - 1, bt, SSD_CONV_DIM), lambda i: (0, i, 0)),
        pl.BlockSpec((bt, SSD_WIDTH, SSD_STATE), lambda i: (i, 0, 0)),
        win_spec, win_spec,
    ] + [_const_spec(a.shape) for a in (gpre, win, convw, convb, dtb, alog, dskip_e, gssd,
                                        sinkcol, wout, gpost)]
    out_specs = (
        tm(D),
        pl.BlockSpec((bt, SSD_WIDTH, SSD_STATE), lambda i: (i, 0, 0)),
        pl.BlockSpec((SSD_CONV - 1, bt, SSD_CONV_DIM), lambda i: (0, i, 0)),
        win_spec, win_spec,
    )
    out_shape = (
        jax.ShapeDtypeStruct((steps, nb, D), f32),
        jax.ShapeDtypeStruct((nb, SSD_WIDTH, SSD_STATE), f32),
        jax.ShapeDtypeStruct((SSD_CONV - 1, nb, SSD_CONV_DIM), f32),
        jax.ShapeDtypeStruct(ck.shape, f32),
        jax.ShapeDtypeStruct(cv.shape, f32),
    )
    return pl.pallas_call(
        kern, grid=(nb // bt,), in_specs=in_specs, out_specs=out_specs, out_shape=out_shape,
        name="sample_mixer",
        compiler_params=pltpu.CompilerParams(
            dimension_semantics=("arbitrary",), vmem_limit_bytes=VMEM_LIMIT),
    )(x_tm, cprev_tm, st, ck, cv, gpre, win, convw, convb, dtb, alog, dskip_e, gssd, sinkcol,
      wout, gpost)


def _sample_xattn_kernel(x_ref, mk_ref, mv_ref, gpre_ref, wq_ref, wo_ref, gpost_ref, y_ref,
                         *, bt, steps):
    R = steps * bt
    nrow = bt * N_MEM * X_HEADS
    x = x_ref[...].reshape(R, D_MODEL)
    hn = _rmsnorm(x, gpre_ref[...]).astype(bf16)
    q = jnp.dot(hn, wq_ref[...], preferred_element_type=f32)
    qs = jnp.concatenate([q[:, hd * X_HEAD_DIM:(hd + 1) * X_HEAD_DIM] for hd in range(X_HEADS)],
                         axis=0)
    kall = mk_ref[...].reshape(nrow, X_HEAD_DIM)
    vall = mv_ref[...].reshape(nrow, X_HEAD_DIM)
    ncol = X_HEADS * R
    seq_rows = N_MEM * X_HEADS
    z = _dot_nt(kall, qs).reshape(bt, seq_rows, ncol)
    b_i = lax.broadcasted_iota(jnp.int32, (bt, 1, ncol), 0)
    c_i = lax.broadcasted_iota(jnp.int32, (bt, 1, ncol), 2)
    zc = jnp.sum(jnp.where(c_i % bt == b_i, z, 0.0), axis=0)
    zc = zc.reshape(seq_rows // SUBLANES, SUBLANES, ncol)
    r_h = lax.broadcasted_iota(jnp.int32, (1, SUBLANES, ncol), 1) % X_HEADS
    c_h = lax.broadcasted_iota(jnp.int32, (1, SUBLANES, ncol), 2) // R
    zc = jnp.where(r_h == c_h, zc, -jnp.inf).reshape(seq_rows, ncol)
    m = jnp.max(zc, axis=0, keepdims=True)
    e = jnp.exp(zc - m)
    p = e * (1.0 / jnp.sum(e, axis=0, keepdims=True))
    col_b = lax.broadcasted_iota(jnp.int32, (1, ncol), 1) % bt
    tn = (((0,), (0,)), ((), ()))
    o = None
    for b in range(bt):
        p_b = jnp.where(col_b == b, p, 0.0).astype(bf16)
        o_b = lax.dot_general(p_b, vall[b * seq_rows:(b + 1) * seq_rows].astype(bf16), tn,
                              preferred_element_type=f32)
        o = o_b if o is None else o + o_b
    o = jnp.concatenate([o[hd * R:(hd + 1) * R] for hd in range(X_HEADS)], axis=-1)
    cc = _dot(o, wo_ref[...])
    y_ref[...] = (x + _rmsnorm(cc, gpost_ref[...])).reshape(steps, bt, D_MODEL)


def _sample_xattn(x_tm, mk, mv, gpre, wq, wo, gpost, bt):
    steps, nb, D = x_tm.shape
    kern = functools.partial(_sample_xattn_kernel, bt=bt, steps=steps)
    xs = pl.BlockSpec((steps, bt, D), lambda i: (0, i, 0))
    ms = pl.BlockSpec((bt, N_MEM, X_HEADS, X_HEAD_DIM), lambda i: (i, 0, 0, 0))
    return pl.pallas_call(
        kern, grid=(nb // bt,),
        in_specs=[xs, ms, ms, _const_spec(gpre.shape), _const_spec(wq.shape),
                  _const_spec(wo.shape), _const_spec(gpost.shape)],
        out_specs=xs, out_shape=jax.ShapeDtypeStruct((steps, nb, D), f32),
        name="sample_xattn",
        compiler_params=pltpu.CompilerParams(
            dimension_semantics=("arbitrary",), vmem_limit_bytes=VMEM_LIMIT),
    )(x_tm, mk, mv, gpre, wq, wo, gpost)


def _row(v, width=None):
    v = v.reshape(1, -1).astype(f32)
    if width is not None and v.shape[1] < width:
        v = jnp.pad(v, ((0, 0), (0, width - v.shape[1])))
    return v


def kernel(x_prompt, x_sample, state_ssm, state_conv, cache_win_k, cache_win_v, cache_mem_k, cache_mem_v, mem_prompt, g_mix_pre, w_in, conv_w, conv_b, dt_bias, a_log, d_skip, g_ssd_norm, sinks, w_out, g_mix_post, g_x_pre, w_xq, g_mem, w_xk, w_xv, w_xo, g_x_post, g_ffn_pre, w_gate, w_up, w_down, g_ffn_post):
    depth = w_in.shape[0]
    assert depth == 1
    B, L, D = x_prompt.shape
    NB, steps, _ = x_sample.shape
    li = 0

    w = w_in[li]
    wz, wxbc, wdt = w[:, 0:512], w[:, 512:1536], w[:, 1536:1544]
    wq, wk, wv = w[:, 1544:2056], w[:, 2056:2184], w[:, 2184:2312]
    wdt_p = jnp.pad(wdt, ((0, 0), (0, LANES - SSD_HEADS)))
    win_p = jnp.concatenate([wz, wxbc, wq * ATT_SCALE, wk, wv, wdt_p], axis=1).astype(bf16)
    wo_b = w_out[li].astype(bf16)

    gpre, gpost = _row(g_mix_pre[li]), _row(g_mix_post[li])
    convw, convb = conv_w[li].astype(f32), _row(conv_b[li])
    dtb, alog = _row(dt_bias[li], LANES), _row(a_log[li], LANES)
    dskip_e = _row(jnp.repeat(d_skip[li], SSD_HEAD_DIM))
    gssd = _row(g_ssd_norm[li])
    sk = sinks[li].astype(f32)

    mk2d, mv2d, mk4, mv4 = _memkv(mem_prompt.reshape(B * N_MEM, D), _row(g_mem[li]),
                                  w_xk[li].astype(bf16), w_xv[li].astype(bf16), tile=512)
    mk3, mv3 = mk2d.reshape(B, N_MEM, D), mv2d.reshape(B, N_MEM, D)
    x1, p_ssm, p_conv8, p_wk, p_wv = _prompt_mixer(
        x_prompt, gpre, win_p, convw, convb, dtb, alog, dskip_e, gssd, sk, wo_b, gpost, tile=512)
    wxq_b, wxo_b = (w_xq[li] * X_SCALE).astype(bf16), w_xo[li].astype(bf16)
    gxpre, gxpost = _row(g_x_pre[li]), _row(g_x_post[li])
    x2 = _prompt_xattn(x1, gxpre, wxq_b, mk3, mv3, wxo_b, gxpost, tile=512)
    wg_b, wu_b, wd_b = w_gate[li].astype(bf16), w_up[li].astype(bf16), w_down[li].astype(bf16)
    gfpre, gfpost = _row(g_ffn_pre[li]), _row(g_ffn_post[li])
    yp = _ffn(x2.reshape(B * L, D), gfpre, wg_b, wu_b, wd_b, gfpost, tile=512).reshape(B, L, D)

    bt = 8
    x_tm = jnp.transpose(x_sample, (1, 0, 2))
    cprev_tm = jnp.transpose(state_conv[li], (1, 0, 2))
    st = state_ssm[li].reshape(NB, SSD_WIDTH, SSD_STATE)
    ck = jnp.transpose(cache_win_k[li], (0, 2, 3, 1))
    cv = jnp.transpose(cache_win_v[li], (0, 2, 3, 1))
    sinkcol = jnp.repeat(sk, steps * bt).reshape(ATT_HEADS * steps * bt, 1)
    x1s, s_ssm, cnew_tm, s_wk, s_wv = _sample_mixer(
        x_tm, cprev_tm, st, ck, cv, gpre, win_p, convw, convb, dtb, alog, dskip_e, gssd,
        sinkcol, wo_b, gpost, bt=bt)
    cmk = cache_mem_k.reshape(NB, N_MEM, X_HEADS, X_HEAD_DIM)
    cmv = cache_mem_v.reshape(NB, N_MEM, X_HEADS, X_HEAD_DIM)
    x2s = _sample_xattn(x1s, cmk, cmv, gxpre, wxq_b, wxo_b, gxpost, bt=bt)
    ys_tm = _ffn(x2s.reshape(steps * NB, D), gfpre, wg_b, wu_b, wd_b, gfpost, tile=steps * NB)
    ys = jnp.transpose(ys_tm.reshape(steps, NB, D), (1, 0, 2))

    s_conv = jnp.transpose(cnew_tm, (1, 0, 2))
    kv_shape = (ATT_KV_HEADS, ATT_HEAD_DIM)
    return (
        yp, ys,
        p_ssm.reshape(1, B, SSD_HEADS, SSD_HEAD_DIM, SSD_STATE),
        p_conv8[:, SUBLANES - (SSD_CONV - 1):, :][None],
        jnp.transpose(p_wk.reshape(B, *kv_shape, WINDOW), (0, 3, 1, 2))[None],
        jnp.transpose(p_wv.reshape(B, *kv_shape, WINDOW), (0, 3, 1, 2))[None],
        mk4.reshape(1, B, N_MEM, X_HEADS, X_HEAD_DIM), mv4.reshape(1, B, N_MEM, X_HEADS, X_HEAD_DIM),
        s_ssm.reshape(1, NB, SSD_HEADS, SSD_HEAD_DIM, SSD_STATE),
        s_conv[None],
        jnp.transpose(s_wk, (0, 3, 1, 2))[None], jnp.transpose(s_wv, (0, 3, 1, 2))[None],
    )
```

```python
import functools

import jax
import jax.numpy as jnp
from jax import lax
from jax.experimental import pallas as pl
from jax.experimental.pallas import tpu as pltpu

f32 = jnp.float32
bf16 = jnp.bfloat16

D_MODEL = 1024
EPS = 1e-6
N_MEM = 256
SSD_HEADS = 8
SSD_HEAD_DIM = 64
SSD_WIDTH = 512
SSD_GROUPS = 2
SSD_STATE = 128
SSD_CONV = 4
SSD_CHUNK = 128
SSD_CONV_DIM = 1024
ATT_HEADS = 8
ATT_KV_HEADS = 2
ATT_HEAD_DIM = 64
ATT_WIDTH = 512
ATT_KV_WIDTH = 128
WINDOW = 128
ATT_SCALE = ATT_HEAD_DIM ** -0.5
X_HEADS = 4
X_HEAD_DIM = 256
X_SCALE = X_HEAD_DIM ** -0.5
D_FF = 2816
LANES = 128
SUBLANES = 8
VMEM_LIMIT = 56 * 1024 * 1024
STREAMS = 2
PROJ_PIECES, SCAN_PIECES, OUT_PIECES = 19, 10, 5

P_Z, P_XBC, P_Q, P_K, P_V, P_DT, P_END = 0, 512, 1536, 2048, 2176, 2304, 2432


def _dot(a, b):
    return jnp.dot(a.astype(bf16), b.astype(bf16), preferred_element_type=f32)


def _dot_nt(a, b):
    return lax.dot_general(a.astype(bf16), b.astype(bf16), (((1,), (1,)), ((), ())),
                           preferred_element_type=f32)


def _split2(x):
    hi = x.astype(bf16)
    lo = (x - hi.astype(f32)).astype(bf16)
    return hi, lo


def _dot_x2(x, m):
    hi, lo = _split2(x)
    return (jnp.dot(hi, m, preferred_element_type=f32)
            + jnp.dot(lo, m, preferred_element_type=f32))


def _rmsnorm(x, g):
    ms = jnp.mean(x * x, axis=-1, keepdims=True)
    return x * lax.rsqrt(ms + EPS) * g


def _silu(x):
    return x * jax.nn.sigmoid(x)


def _softplus(x):
    return jnp.maximum(x, 0.0) + jnp.log1p(jnp.exp(-jnp.abs(x)))


def _head_expand_matrix():
    r = lax.broadcasted_iota(jnp.int32, (LANES, SSD_WIDTH), 0)
    c = lax.broadcasted_iota(jnp.int32, (LANES, SSD_WIDTH), 1)
    return (r == (c >> 6)).astype(bf16)


def _gated_group_norm(y, z, g):
    u = y * _silu(z)
    half = SSD_WIDTH // SSD_GROUPS
    parts = []
    for gi in range(SSD_GROUPS):
        ug = u[:, gi * half:(gi + 1) * half]
        parts.append(ug * lax.rsqrt(jnp.mean(ug * ug, axis=-1, keepdims=True) + EPS))
    return jnp.concatenate(parts, axis=-1) * g


def _prompt_mixer_kernel(x_ref, gpre_ref, win_ref, convw_ref, convb_ref, dtb_ref, alog_ref,
                         dskip_ref, gssd_ref, sinks_ref, wout_ref, gpost_ref,
                         y_ref, ssm_ref, conv_ref, wk_ref, wv_ref,
                         statet_sc, xbc_ext_sc, xbc_sc, z_sc, q_sc, k_sc, v_sc, dt_sc,
                         kprev_sc, vprev_sc, mix_sc, *, tile):
    i = pl.program_id(1)
    NC = tile // SSD_CHUNK
    ns = range(NC)
    C = SSD_CHUNK
    PW = 2 * LANES

    @pl.when(i == 0)
    def _init():
        statet_sc[...] = jnp.zeros_like(statet_sc)
        xbc_ext_sc[:, 0:SUBLANES, :] = jnp.zeros((STREAMS, SUBLANES, SSD_CONV_DIM), f32)
        kprev_sc[...] = jnp.zeros_like(kprev_sc)
        vprev_sc[...] = jnp.zeros_like(vprev_sc)

    lane = lax.broadcasted_iota(jnp.int32, (1, LANES), 1)
    a_row = -jnp.exp(alog_ref[...])
    expand = _head_expand_matrix()
    expand2 = jnp.concatenate([expand, expand], axis=0)
    row_i = lax.broadcasted_iota(jnp.int32, (C, C), 0)
    col_i = lax.broadcasted_iota(jnp.int32, (C, C), 1)
    lower = col_i <= row_i
    tri = lower.astype(bf16)
    tri3 = jnp.concatenate([tri, tri, tri], axis=1)
    lo_half = lane < ATT_HEAD_DIM
    half = SSD_WIDTH // SSD_GROUPS
    head_of_lane = lax.broadcasted_iota(jnp.int32, (1, half), 1) >> 6
    rows = [slice(n * C, (n + 1) * C) for n in ns]

    def stream(s):
        statet_s, xbc_ext_s, xbc_s, z_s = statet_sc.at[s], xbc_ext_sc.at[s], xbc_sc.at[s], z_sc.at[s]
        q_s, k_s, v_s, dt_s = q_sc.at[s], k_sc.at[s], v_sc.at[s], dt_sc.at[s]
        kprev_s, vprev_s, mix_s = kprev_sc.at[s], vprev_sc.at[s], mix_sc.at[s]

        x = x_ref[s, 0]
        h = _rmsnorm(x, gpre_ref[...]).astype(bf16)
        yield

        def proj(col):
            return jnp.dot(h, win_ref[:, col:col + PW], preferred_element_type=f32)

        def conv_cols(cb):
            cols = slice(cb * LANES, (cb + 1) * LANES)
            acc = convb_ref[:, cols]
            for j in range(SSD_CONV):
                off = SUBLANES - (SSD_CONV - 1) + j
                acc = acc + xbc_ext_s[off:off + tile, cols] * convw_ref[j:j + 1, cols]
            xbc_s[:, cols] = _silu(acc)

        def xbc_piece(pc):
            xbc_ext_s[SUBLANES:SUBLANES + tile, pc * PW:(pc + 1) * PW] = proj(P_XBC + pc * PW)

        xbc_piece(0)
        yield
        xbc_piece(1)
        yield
        conv_cols(0)
        yield
        xbc_piece(2)
        yield
        conv_cols(1)
        yield
        xbc_piece(3)
        yield
        conv_cols(2)
        yield
        z_s[:, 0:PW] = proj(P_Z)
        yield
        conv_cols(3)
        yield
        z_s[:, PW:2 * PW] = proj(P_Z + PW)
        yield
        conv_cols(4)
        yield
        q_s[:, 0:PW] = proj(P_Q)
        yield
        conv_cols(5)
        yield
        q_s[:, PW:2 * PW] = proj(P_Q + PW)
        yield
        conv_cols(6)
        yield
        kv = proj(P_K)
        k_s[...] = kv[:, 0:LANES]
        v_s[...] = kv[:, LANES:PW]
        yield
        conv_cols(7)
        yield
        dt_raw = jnp.dot(h, win_ref[:, P_DT:P_END], preferred_element_type=f32)
        dt_s[...] = jnp.where(lane < SSD_HEADS, _softplus(dt_raw + dtb_ref[...]), 0.0)
        tail = xbc_ext_s[tile:tile + SUBLANES, :]
        conv_ref[s, 0] = tail
        xbc_ext_s[0:SUBLANES, :] = tail
        yield

        GH = SSD_HEADS // SSD_GROUPS
        heads = range(ATT_HEADS)
        gs = range(SSD_GROUPS)
        zero_b = jnp.zeros((), bf16)

        yield
        xs = [xbc_s[rows[n], 0:SSD_WIDTH] for n in ns]
        bm = [xbc_s[rows[n], SSD_WIDTH:SSD_WIDTH + half] for n in ns]
        cm = [xbc_s[rows[n], SSD_WIDTH + half:SSD_CONV_DIM] for n in ns]
        dtc = [dt_s[rows[n], :] for n in ns]
        cs = []
        for n in ns:
            adt = dtc[n] * a_row
            a_hi = adt.astype(bf16)
            a_r1 = adt - a_hi.astype(f32)
            a_mid = a_r1.astype(bf16)
            a_lo = (a_r1 - a_mid.astype(f32)).astype(bf16)
            cs.append(jnp.dot(tri3, jnp.concatenate([a_hi, a_mid, a_lo], axis=0),
                              preferred_element_type=f32))

        yield
        q = [q_s[rows[n], :].astype(bf16) for n in ns]
        k = [k_s[rows[n], :] for n in ns]
        v = [v_s[rows[n], :] for n in ns]
        k_prev = [kprev_s[...]] + k[:-1]
        v_prev = [vprev_s[...]] + v[:-1]
        kprev_s[...] = k[-1]
        vprev_s[...] = v[-1]
        first_bias = jnp.where(i > 0, 0.0, -jnp.inf)
        k_lo, k_hi, v_lo, v_hi = [], [], [], []
        for n in ns:
            kk = jnp.concatenate([k_prev[n], k[n]], axis=0)
            vv = jnp.concatenate([v_prev[n], v[n]], axis=0)
            kk_r = pltpu.roll(kk, ATT_HEAD_DIM, 1)
            vv_r = pltpu.roll(vv, ATT_HEAD_DIM, 1)
            k_lo.append([jnp.where(lo_half, kk, 0.0).astype(bf16), jnp.where(lo_half, kk_r, 0.0).astype(bf16)])
            k_hi.append([jnp.where(lo_half, 0.0, kk_r).astype(bf16), jnp.where(lo_half, 0.0, kk).astype(bf16)])
            v_lo.append([jnp.where(lo_half, vv, 0.0).astype(bf16), jnp.where(lo_half, vv_r, 0.0).astype(bf16)])
            v_hi.append([jnp.where(lo_half, 0.0, vv_r).astype(bf16), jnp.where(lo_half, 0.0, vv).astype(bf16)])
        s_g = [[_dot_nt(jnp.concatenate([q[n][:, (2 * g) * LANES:(2 * g + 1) * LANES],
                                         q[n][:, (2 * g + 1) * LANES:(2 * g + 2) * LANES]], axis=0),
                        jnp.concatenate([k_lo[n][g], k_hi[n][g]], axis=0))
                for g in range(ATT_KV_HEADS)] for n in ns]

        yield
        cs_t = [cs[n].T for n in ns]
        fac_e, cb_all, bm_t = [], [], []
        for n in ns:
            tot = cs[n][C - 1:C, :]
            fac = jnp.concatenate([dtc[n], jnp.exp(cs[n]), jnp.exp(tot - cs[n])], axis=0)
            f_hi = fac.astype(bf16)
            f_lo = (fac - f_hi.astype(f32)).astype(bf16)
            fac_e.append(jnp.dot(jnp.concatenate([f_hi, f_lo], axis=1), expand2,
                                 preferred_element_type=f32))
            cb_all.append(_dot_nt(
                jnp.concatenate([cm[n][:, 0:SSD_STATE], cm[n][:, SSD_STATE:half]], axis=0),
                jnp.concatenate([bm[n][:, 0:SSD_STATE], bm[n][:, SSD_STATE:half]], axis=0)))
            bm_t.append(bm[n].T)

        yield
        def head_scores(n, hd):
            g, jb, sub = hd // 4, (hd // 2) % 2, hd % 2
            s_prev = s_g[n][g][jb * C:(jb + 1) * C, (2 * sub) * C:(2 * sub + 1) * C]
            s_cur = s_g[n][g][jb * C:(jb + 1) * C, (2 * sub + 1) * C:(2 * sub + 2) * C]
            return jnp.where(lower, s_cur, s_prev + first_bias if n == 0 else s_prev)
        sc = [[head_scores(n, hd) for hd in heads] for n in ns]
        sink = [sinks_ref[hd] for hd in heads]
        m = [[jnp.maximum(jnp.max(sc[n][hd], axis=-1, keepdims=True), sink[hd]) for hd in heads]
             for n in ns]

        yield
        xdt = [xs[n] * fac_e[n][0:C] for n in ns]
        ecs_e = [fac_e[n][C:2 * C] for n in ns]
        w_end = [xdt[n] * fac_e[n][2 * C:3 * C] for n in ns]
        xdt_b = [xdt[n].astype(bf16) for n in ns]
        contrib = [[_dot(bm_t[n][g * SSD_STATE:(g + 1) * SSD_STATE, :],
                         w_end[n][:, g * half:(g + 1) * half]) for g in gs] for n in ns]
        decay = [[jnp.exp(jnp.where(lower, cs[n][:, hh:hh + 1] - cs_t[n][hh:hh + 1, :], -jnp.inf))
                  for hh in range(SSD_HEADS)] for n in ns]

        yield
        e = [[jnp.exp(sc[n][hd] - m[n][hd]) for hd in heads] for n in ns]
        den = [[jnp.sum(e[n][hd], axis=-1, keepdims=True) + jnp.exp(sink[hd] - m[n][hd])
                for hd in heads] for n in ns]

        yield
        y_d = []
        for n in ns:
            y_n = []
            for g in gs:
                gl = slice(g * half, (g + 1) * half)
                cb = cb_all[n][g * C:(g + 1) * C, g * C:(g + 1) * C]
                m_parts = [(cb * decay[n][g * GH + r]).astype(bf16) for r in range(GH)]
                x_parts = [jnp.where(head_of_lane == r, xdt_b[n][:, gl], zero_b) for r in range(GH)]
                y_n.append(jnp.dot(jnp.concatenate(m_parts, axis=1), jnp.concatenate(x_parts, axis=0),
                                   preferred_element_type=f32))
            y_d.append(y_n)
        st = [statet_s[g] for g in gs]
        y_off = []
        for n in ns:
            y_off.append([_dot(cm[n][:, g * SSD_STATE:(g + 1) * SSD_STATE], st[g]) for g in gs])
            st = [st[g] * ecs_e[n][C - 1:C, g * half:(g + 1) * half] + contrib[n][g] for g in gs]
        for g in gs:
            statet_s[g] = st[g]

        yield
        p = [[(e[n][hd] * (1.0 / den[n][hd])).astype(bf16) for hd in heads] for n in ns]
        o_g = []
        for n in ns:
            o_n = []
            for g in range(ATT_KV_HEADS):
                p_rows = []
                for jb in range(2):
                    p_cols = []
                    for sub in range(2):
                        ph = p[n][g * 4 + jb * 2 + sub]
                        p_cols += [jnp.where(lower, zero_b, ph), jnp.where(lower, ph, zero_b)]
                    p_rows.append(jnp.concatenate(p_cols, axis=1))
                o_n.append(jnp.dot(jnp.concatenate(p_rows, axis=0),
                                   jnp.concatenate([v_lo[n][g], v_hi[n][g]], axis=0),
                                   preferred_element_type=f32))
            o_g.append(o_n)

        yield
        for n in ns:
            y = (jnp.concatenate([y_d[n][g] + y_off[n][g] * ecs_e[n][:, g * half:(g + 1) * half]
                                  for g in gs], axis=-1)
                 + xs[n] * dskip_ref[...])
            y_ssd = _gated_group_norm(y, z_s[rows[n], :], gssd_ref[...])
            mix_s[rows[n], 0:SSD_WIDTH] = y_ssd.astype(bf16)
        for n in ns:
            for g in range(ATT_KV_HEADS):
                for jb in range(2):
                    lo_l = SSD_WIDTH + (2 * g + jb) * LANES
                    mix_s[rows[n], lo_l:lo_l + LANES] = o_g[n][g][jb * C:(jb + 1) * C].astype(bf16)

        yield

        mix_in = mix_s[...]
        mix = []
        for pc in range(D_MODEL // PW):
            mix.append(jnp.dot(mix_in, wout_ref[:, pc * PW:(pc + 1) * PW], preferred_element_type=f32))
            yield
        y_ref[s, 0] = x + _rmsnorm(jnp.concatenate(mix, axis=-1), gpost_ref[...])

    def mixed(ga, na, gb, nb):
        done_b = 0
        for ka in range(na):
            next(ga, None)
            want_b = ((ka + 1) * nb) // na
            for _ in range(want_b - done_b):
                next(gb, None)
            done_b = want_b

    g0, g1 = [stream(s) for s in range(STREAMS)]
    for _ in range(PROJ_PIECES):
        next(g0, None)
    mixed(g0, SCAN_PIECES, g1, PROJ_PIECES)
    mixed(g1, SCAN_PIECES, g0, OUT_PIECES)
    for g in (g0, g1):
        for _ in g:
            pass

    @pl.when(i == pl.num_programs(1) - 1)
    def _final_state():
        for s in range(STREAMS):
            for g in range(SSD_GROUPS):
                ssm_ref[s, 0, g * half:(g + 1) * half, :] = statet_sc[s, g].T
            wk_ref[s, 0] = k_sc[s, tile - WINDOW:tile, :].T
            wv_ref[s, 0] = v_sc[s, tile - WINDOW:tile, :].T


def _const_spec(shape):
    nd = len(shape)
    return pl.BlockSpec(shape, lambda *_: (0,) * nd)


def _prompt_mixer(x, gpre, win, convw, convb, dtb, alog, dskip_e, gssd, sinks, wout, gpost, tile):
    B, L, D = x.shape
    S = STREAMS
    G = B // S
    kern = functools.partial(_prompt_mixer_kernel, tile=tile)

    def per_seq(rows, width):
        return pl.BlockSpec((S, 1, rows, width), lambda b, i: (0, b, 0, 0))

    tile_spec = pl.BlockSpec((S, 1, tile, D), lambda b, i: (0, b, i, 0))
    out_shape = (
        jax.ShapeDtypeStruct((S, G, L, D), f32),
        jax.ShapeDtypeStruct((S, G, SSD_WIDTH, SSD_STATE), f32),
        jax.ShapeDtypeStruct((S, G, SUBLANES, SSD_CONV_DIM), f32),
        jax.ShapeDtypeStruct((S, G, ATT_KV_WIDTH, WINDOW), f32),
        jax.ShapeDtypeStruct((S, G, ATT_KV_WIDTH, WINDOW), f32),
    )
    in_specs = [
        tile_spec,
        _const_spec(gpre.shape), _const_spec(win.shape), _const_spec(convw.shape),
        _const_spec(convb.shape), _const_spec(dtb.shape), _const_spec(alog.shape),
        _const_spec(dskip_e.shape), _const_spec(gssd.shape),
        pl.BlockSpec(memory_space=pltpu.SMEM),
        _const_spec(wout.shape), _const_spec(gpost.shape),
    ]
    out_specs = (
        tile_spec,
        per_seq(SSD_WIDTH, SSD_STATE),
        per_seq(SUBLANES, SSD_CONV_DIM),
        per_seq(ATT_KV_WIDTH, WINDOW),
        per_seq(ATT_KV_WIDTH, WINDOW),
    )
    scratch = [
        pltpu.VMEM((S, SSD_GROUPS, SSD_STATE, SSD_WIDTH // SSD_GROUPS), f32),
        pltpu.VMEM((S, tile + 2 * SUBLANES, SSD_CONV_DIM), f32),
        pltpu.VMEM((S, tile, SSD_CONV_DIM), f32),
        pltpu.VMEM((S, tile, SSD_WIDTH), f32),
        pltpu.VMEM((S, tile, ATT_WIDTH), f32),
        pltpu.VMEM((S, tile, ATT_KV_WIDTH), f32),
        pltpu.VMEM((S, tile, ATT_KV_WIDTH), f32),
        pltpu.VMEM((S, tile, LANES), f32),
        pltpu.VMEM((S, WINDOW, ATT_KV_WIDTH), f32),
        pltpu.VMEM((S, WINDOW, ATT_KV_WIDTH), f32),
        pltpu.VMEM((S, tile, 2 * SSD_WIDTH), bf16),
    ]
    outs = pl.pallas_call(
        kern, grid=(G, L // tile), in_specs=in_specs, out_specs=out_specs, out_shape=out_shape,
        scratch_shapes=scratch, name="prompt_mixer",
        compiler_params=pltpu.CompilerParams(
            dimension_semantics=("arbitrary", "arbitrary"), vmem_limit_bytes=VMEM_LIMIT),
    )(x.reshape(S, G, L, D), gpre, win, convw, convb, dtb, alog, dskip_e, gssd, sinks, wout, gpost)
    return tuple(o.reshape(B, *o.shape[2:]) for o in outs)


def _memkv_kernel(m_ref, g_ref, wk_ref, wv_ref, k_ref, v_ref, kh_ref, vh_ref):
    mn = _rmsnorm(m_ref[...], g_ref[...]).astype(bf16)
    k = jnp.dot(mn, wk_ref[...], preferred_element_type=f32)
    v = jnp.dot(mn, wv_ref[...], preferred_element_type=f32)
    k_ref[...] = k
    v_ref[...] = v
    for hd in range(X_HEADS):
        kh_ref[:, hd, :] = k[:, hd * X_HEAD_DIM:(hd + 1) * X_HEAD_DIM]
        vh_ref[:, hd, :] = v[:, hd * X_HEAD_DIM:(hd + 1) * X_HEAD_DIM]


def _memkv(mem2d, g, wk, wv, tile):
    n, d = mem2d.shape
    row = pl.BlockSpec((tile, d), lambda i: (i, 0))
    hrow = pl.BlockSpec((tile, X_HEADS, X_HEAD_DIM), lambda i: (i, 0, 0))
    flat = jax.ShapeDtypeStruct((n, d), f32)
    heads = jax.ShapeDtypeStruct((n, X_HEADS, X_HEAD_DIM), f32)
    return pl.pallas_call(
        _memkv_kernel, grid=(n // tile,),
        in_specs=[row, _const_spec(g.shape), _const_spec(wk.shape), _const_spec(wv.shape)],
        out_specs=(row, row, hrow, hrow),
        out_shape=(flat, flat, heads, heads),
        name="memory_kv",
        compiler_params=pltpu.CompilerParams(
            dimension_semantics=("arbitrary",), vmem_limit_bytes=VMEM_LIMIT),
    )(mem2d, g, wk, wv)


def _prompt_xattn_kernel(x_ref, gpre_ref, wq_ref, mk_ref, mv_ref, wo_ref, gpost_ref, y_ref):
    x = x_ref[0]
    hn = _rmsnorm(x, gpre_ref[...]).astype(bf16)
    q = jnp.dot(hn, wq_ref[...], preferred_element_type=f32)
    hs = range(X_HEADS)
    sl = [slice(hd * X_HEAD_DIM, (hd + 1) * X_HEAD_DIM) for hd in hs]
    s = [_dot_nt(q[:, sl[hd]], mk_ref[0, :, sl[hd]]) for hd in hs]
    m = [jnp.max(s[hd], axis=-1, keepdims=True) for hd in hs]
    e = [jnp.exp(s[hd] - m[hd]) for hd in hs]
    r = [1.0 / jnp.sum(e[hd], axis=-1, keepdims=True) for hd in hs]
    o = jnp.concatenate([_dot(e[hd] * r[hd], mv_ref[0, :, sl[hd]]) for hd in hs], axis=-1)
    c = _dot(o, wo_ref[...])
    y_ref[0] = x + _rmsnorm(c, gpost_ref[...])


def _prompt_xattn(x, gpre, wq, mk, mv, wo, gpost, tile):
    B, L, D = x.shape
    xs = pl.BlockSpec((1, tile, D), lambda b, i: (b, i, 0))
    ms = pl.BlockSpec((1, N_MEM, D), lambda b, i: (b, 0, 0))
    return pl.pallas_call(
        _prompt_xattn_kernel, grid=(B, L // tile),
        in_specs=[xs, _const_spec(gpre.shape), _const_spec(wq.shape), ms, ms,
                  _const_spec(wo.shape), _const_spec(gpost.shape)],
        out_specs=xs, out_shape=jax.ShapeDtypeStruct((B, L, D), f32),
        name="prompt_xattn",
        compiler_params=pltpu.CompilerParams(
            dimension_semantics=("arbitrary", "arbitrary"), vmem_limit_bytes=VMEM_LIMIT),
    )(x, gpre, wq, mk, mv, wo, gpost)


def _ffn_kernel(x_ref, gpre_ref, wg_ref, wu_ref, wd_ref, gpost_ref, y_ref):
    x = x_ref[...]
    hf = _rmsnorm(x, gpre_ref[...]).astype(bf16)
    gate = jnp.dot(hf, wg_ref[...], preferred_element_type=f32)
    up = jnp.dot(hf, wu_ref[...], preferred_element_type=f32)
    act = (_silu(gate) * up).astype(bf16)
    f = jnp.dot(act, wd_ref[...], preferred_element_type=f32)
    y_ref[...] = x + _rmsnorm(f, gpost_ref[...])


def _ffn(x2d, gpre, wg, wu, wd, gpost, tile):
    n, d = x2d.shape
    row = pl.BlockSpec((tile, d), lambda i: (i, 0))
    return pl.pallas_call(
        _ffn_kernel, grid=(n // tile,),
        in_specs=[row, _const_spec(gpre.shape), _const_spec(wg.shape), _const_spec(wu.shape),
                  _const_spec(wd.shape), _const_spec(gpost.shape)],
        out_specs=row, out_shape=jax.ShapeDtypeStruct((n, d), f32),
        name="ffn",
        compiler_params=pltpu.CompilerParams(
            dimension_semantics=("arbitrary",), vmem_limit_bytes=VMEM_LIMIT),
    )(x2d, gpre, wg, wu, wd, gpost)


def _pad_rows(a, rows):
    if a.shape[0] == rows:
        return a
    return jnp.concatenate([a, jnp.zeros((rows - a.shape[0], a.shape[1]), a.dtype)], axis=0)


def _sample_mixer_kernel(x_ref, cprev_ref, st_ref, ck_ref, cv_ref,
                         gpre_ref, win_ref, convw_ref, convb_ref, dtb_ref, alog_ref,
                         dskip_ref, gssd_ref, sinkcol_ref, wout_ref, gpost_ref,
                         y_ref, ssm_ref, cnew_ref, wk_ref, wv_ref, *, bt, steps):
    R = steps * bt
    half = SSD_WIDTH // SSD_GROUPS
    x = x_ref[...].reshape(R, D_MODEL)
    h = _rmsnorm(x, gpre_ref[...]).astype(bf16)
    z = jnp.dot(h, win_ref[:, P_Z:P_XBC], preferred_element_type=f32)
    u = jnp.dot(h, win_ref[:, P_XBC:P_Q], preferred_element_type=f32)
    q = jnp.dot(h, win_ref[:, P_Q:P_K], preferred_element_type=f32)
    k_new = jnp.dot(h, win_ref[:, P_K:P_V], preferred_element_type=f32)
    v_new = jnp.dot(h, win_ref[:, P_V:P_DT], preferred_element_type=f32)
    dt_raw = jnp.dot(h, win_ref[:, P_DT:P_END], preferred_element_type=f32)
    lane = lax.broadcasted_iota(jnp.int32, (1, LANES), 1)
    dt = jnp.where(lane < SSD_HEADS, _softplus(dt_raw + dtb_ref[...]), 0.0)

    def slab(a, t):
        return a[t * bt:(t + 1) * bt]

    hist = [cprev_ref[j] for j in range(SSD_CONV - 1)] + [slab(u, t) for t in range(steps)]
    xbc_t = []
    for t in range(steps):
        acc = convb_ref[...]
        for j in range(SSD_CONV):
            acc = acc + hist[t + j] * convw_ref[j:j + 1, :]
        xbc_t.append(_silu(acc))
    for j in range(SSD_CONV - 1):
        cnew_ref[j] = hist[steps + j]
    xbc = jnp.concatenate(xbc_t, axis=0)
    xs = xbc[:, 0:SSD_WIDTH]
    bm = xbc[:, SSD_WIDTH:SSD_WIDTH + half]
    cm = xbc[:, SSD_WIDTH + half:SSD_CONV_DIM]

    a_row = -jnp.exp(alog_ref[...])
    adt = dt * a_row
    cs_t = [slab(adt, 0)]
    for t in range(1, steps):
        cs_t.append(cs_t[-1] + slab(adt, t))
    cs = jnp.concatenate(cs_t, axis=0)
    tot = cs_t[-1]
    tot_rows = jnp.concatenate([tot] * steps, axis=0)
    expand = _head_expand_matrix()
    expand2 = jnp.concatenate([expand, expand], axis=0)
    pairs = [(t, s2) for t in range(steps) for s2 in range(t)]
    fac = jnp.concatenate([dt, jnp.exp(cs), jnp.exp(tot_rows - cs), jnp.exp(tot)]
                          + [jnp.exp(cs_t[t] - cs_t[s2]) for t, s2 in pairs], axis=0)
    f_hi, f_lo = _split2(fac)
    fac_e = jnp.dot(jnp.concatenate([f_hi, f_lo], axis=1), expand2, preferred_element_type=f32)
    xdt = xs * fac_e[0:R]
    ecs_e = fac_e[R:2 * R]
    w_end = xdt * fac_e[2 * R:3 * R]
    dec_e = fac_e[3 * R:3 * R + bt]
    pair_decay = {pr: fac_e[3 * R + (n + 1) * bt:3 * R + (n + 2) * bt] for n, pr in enumerate(pairs)}

    gr = lax.broadcasted_iota(jnp.int32, (half, SSD_WIDTH), 0)
    gc = lax.broadcasted_iota(jnp.int32, (half, SSD_WIDTH), 1)
    gsum = ((gr >> 7) == (gc >> 8)).astype(bf16)
    gsum2 = jnp.concatenate([gsum, gsum], axis=0)
    cb_pairs = [(t, s2) for t in range(steps) for s2 in range(t + 1)]
    prod = jnp.concatenate([slab(cm, t) * slab(bm, s2) for t, s2 in cb_pairs], axis=0)
    c_hi, c_lo = _split2(prod)
    cb_e = jnp.dot(jnp.concatenate([c_hi, c_lo], axis=1), gsum2, preferred_element_type=f32)

    y_t = []
    for t in range(steps):
        acc = None
        for s2 in range(t + 1):
            n = cb_pairs.index((t, s2))
            coef = cb_e[n * bt:(n + 1) * bt]
            if s2 < t:
                coef = coef * pair_decay[(t, s2)]
            term = coef * slab(xdt, s2)
            acc = term if acc is None else acc + term
        y_t.append(acc)
    y_intra = jnp.concatenate(y_t, axis=0)

    b_idx = lax.broadcasted_iota(jnp.int32, (bt, 1, LANES), 0)
    l_idx = lax.broadcasted_iota(jnp.int32, (bt, 1, LANES), 2)
    pair = ((l_idx & (bt - 1)) == b_idx) & (l_idx < R)
    own = (l_idx == b_idx)
    y_off_parts = []
    for g in range(SSD_GROUPS):
        gl = slice(g * half, (g + 1) * half)
        h0 = st_ref[:, gl, :]
        cg = _pad_rows(cm[:, g * SSD_STATE:(g + 1) * SSD_STATE], LANES)
        zz = _dot_nt(h0.reshape(bt * half, SSD_STATE), cg).reshape(bt, half, LANES)
        yt = jnp.sum(jnp.where(pair, zz, 0.0), axis=0)
        y_off_parts.append(yt.T[0:R, :])
        wt = _pad_rows(w_end[:, gl], LANES).T
        lhs = jnp.where(pair, wt[None], 0.0).reshape(bt * half, LANES)
        bg = _pad_rows(bm[:, g * SSD_STATE:(g + 1) * SSD_STATE], LANES)
        contrib = _dot(lhs, bg).reshape(bt, half, SSD_STATE)
        dec_t = _pad_rows(dec_e[:, gl], LANES).T
        dec = jnp.sum(jnp.where(own, dec_t[None], 0.0), axis=-1, keepdims=True)
        ssm_ref[:, gl, :] = h0 * dec + contrib
    y_off = jnp.concatenate(y_off_parts, axis=-1) * ecs_e
    y = y_intra + y_off + xs * dskip_ref[...]
    y_ssd = _gated_group_norm(y, z, gssd_ref[...])

    HD = ATT_HEAD_DIM
    GH = ATT_HEADS // ATT_KV_HEADS
    GR = GH * R
    reps = GR // bt
    ridx = lax.broadcasted_iota(jnp.int32, (GR, 1), 0)
    rb = ridx % bt
    rt = (ridx // bt) % steps
    jcol = lax.broadcasted_iota(jnp.int32, (GR, WINDOW), 1)
    in_window = jcol > rt
    mix = jnp.dot(y_ssd.astype(bf16), wout_ref[0:SSD_WIDTH, :], preferred_element_type=f32)
    for g in range(ATT_KV_HEADS):
        qg = jnp.concatenate([q[:, (g * GH + hl) * HD:(g * GH + hl + 1) * HD] for hl in range(GH)],
                             axis=0)
        qg_b = qg.astype(bf16)
        s_c = jnp.zeros((GR, WINDOW), f32)
        for b in range(bt):
            s_c = jnp.where(rb == b, _dot(qg_b, ck_ref[b, g]), s_c)
        s_c = jnp.where(in_window, s_c, -jnp.inf)
        sink = sinkcol_ref[g * GR:(g + 1) * GR, :]
        m = jnp.maximum(jnp.max(s_c, axis=-1, keepdims=True), sink)
        s_n = []
        for t2 in range(steps):
            kt = jnp.concatenate([slab(k_new, t2)[:, g * HD:(g + 1) * HD]] * reps, axis=0)
            sn = jnp.where(rt >= t2, jnp.sum(qg * kt, axis=-1, keepdims=True), -jnp.inf)
            s_n.append(sn)
            m = jnp.maximum(m, sn)
        e_c = jnp.exp(s_c - m)
        e_n = [jnp.exp(sn - m) for sn in s_n]
        den = jnp.sum(e_c, axis=-1, keepdims=True) + jnp.exp(sink - m)
        for en in e_n:
            den = den + en
        rinv = 1.0 / den
        p_c = (e_c * rinv).astype(bf16)
        o = jnp.zeros((GR, HD), f32)
        for b in range(bt):
            o = jnp.where(rb == b, _dot_nt(p_c, cv_ref[b, g]), o)
        for t2 in range(steps):
            vt = jnp.concatenate([slab(v_new, t2)[:, g * HD:(g + 1) * HD]] * reps, axis=0)
            o = o + (e_n[t2] * rinv) * vt
        for hl in range(GH):
            hd = g * GH + hl
            mix = mix + jnp.dot(o[hl * R:(hl + 1) * R].astype(bf16),
                                wout_ref[SSD_WIDTH + hd * HD:SSD_WIDTH + (hd + 1) * HD, :],
                                preferred_element_type=f32)
    y_ref[...] = (x + _rmsnorm(mix, gpost_ref[...])).reshape(steps, bt, D_MODEL)

    keep = WINDOW - steps
    kn_t = _pad_rows(k_new, LANES).T
    vn_t = _pad_rows(v_new, LANES).T
    sel_r = lax.broadcasted_iota(jnp.int32, (LANES, WINDOW), 0)
    sel_l = lax.broadcasted_iota(jnp.int32, (LANES, WINDOW), 1)
    lane_w = lax.broadcasted_iota(jnp.int32, (1, WINDOW), 1)
    for b in range(bt):
        sel = ((sel_r % bt == b) & (sel_r < R) & (sel_l - keep == sel_r // bt)).astype(bf16)
        new_k = _dot_x2(kn_t, sel)
        new_v = _dot_x2(vn_t, sel)
        for g in range(ATT_KV_HEADS):
            gs = slice(g * HD, (g + 1) * HD)
            wk_ref[b, g] = jnp.where(lane_w < keep, pltpu.roll(ck_ref[b, g], keep, 1), new_k[gs])
            wv_ref[b, g] = jnp.where(lane_w < keep, pltpu.roll(cv_ref[b, g], keep, 1), new_v[gs])


def _sample_mixer(x_tm, cprev_tm, st, ck, cv, gpre, win, convw, convb, dtb, alog, dskip_e, gssd,
                  sinkcol, wout, gpost, bt):
    steps, nb, D = x_tm.shape
    kern = functools.partial(_sample_mixer_kernel, bt=bt, steps=steps)
    tm = lambda w: pl.BlockSpec((steps, bt, w), lambda i: (0, i, 0))
    win_spec = pl.BlockSpec((bt, ATT_KV_HEADS, ATT_HEAD_DIM, WINDOW), lambda i: (i, 0, 0, 0))
    in_specs = [
        tm(D),
        pl.BlockSpec((SSD_CONV - 1, bt, SSD_CONV_DIM), lambda i: (0, i, 0)),
        pl.BlockSpec((bt, SSD_WIDTH, SSD_STATE), lambda i: (i, 0, 0)),
        win_spec, win_spec,
    ] + [_const_spec(a.shape) for a in (gpre, win, convw, convb, dtb, alog, dskip_e, gssd,
                                        sinkcol, wout, gpost)]
    out_specs = (
        tm(D),
        pl.BlockSpec((bt, SSD_WIDTH, SSD_STATE), lambda i: (i, 0, 0)),
        pl.BlockSpec((SSD_CONV - 1, bt, SSD_CONV_DIM), lambda i: (0, i, 0)),
        win_spec, win_spec,
    )
    out_shape = (
        jax.ShapeDtypeStruct((steps, nb, D), f32),
        jax.ShapeDtypeStruct((nb, SSD_WIDTH, SSD_STATE), f32),
        jax.ShapeDtypeStruct((SSD_CONV - 1, nb, SSD_CONV_DIM), f32),
        jax.ShapeDtypeStruct(ck.shape, f32),
        jax.ShapeDtypeStruct(cv.shape, f32),
    )
    return pl.pallas_call(
        kern, grid=(nb // bt,), in_specs=in_specs, out_specs=out_specs, out_shape=out_shape,
        name="sample_mixer",
        compiler_params=pltpu.CompilerParams(
            dimension_semantics=("arbitrary",), vmem_limit_bytes=VMEM_LIMIT),
    )(x_tm, cprev_tm, st, ck, cv, gpre, win, convw, convb, dtb, alog, dskip_e, gssd, sinkcol,
      wout, gpost)


def _sample_xattn_kernel(x_ref, mk_ref, mv_ref, gpre_ref, wq_ref, wo_ref, gpost_ref, y_ref,
                         *, bt, steps):
    R = steps * bt
    nrow = bt * N_MEM * X_HEADS
    x = x_ref[...].reshape(R, D_MODEL)
    hn = _rmsnorm(x, gpre_ref[...]).astype(bf16)
    q = jnp.dot(hn, wq_ref[...], preferred_element_type=f32)
    qs = jnp.concatenate([q[:, hd * X_HEAD_DIM:(hd + 1) * X_HEAD_DIM] for hd in range(X_HEADS)],
                         axis=0)
    kall = mk_ref[...].reshape(nrow, X_HEAD_DIM)
    vall = mv_ref[...].reshape(nrow, X_HEAD_DIM)
    ncol = X_HEADS * R
    seq_rows = N_MEM * X_HEADS
    z = _dot_nt(kall, qs).reshape(bt, seq_rows, ncol)
    b_i = lax.broadcasted_iota(jnp.int32, (bt, 1, ncol), 0)
    c_i = lax.broadcasted_iota(jnp.int32, (bt, 1, ncol), 2)
    zc = jnp.sum(jnp.where(c_i % bt == b_i, z, 0.0), axis=0)
    zc = zc.reshape(seq_rows // SUBLANES, SUBLANES, ncol)
    r_h = lax.broadcasted_iota(jnp.int32, (1, SUBLANES, ncol), 1) % X_HEADS
    c_h = lax.broadcasted_iota(jnp.int32, (1, SUBLANES, ncol), 2) // R
    zc = jnp.where(r_h == c_h, zc, -jnp.inf).reshape(seq_rows, ncol)
    m = jnp.max(zc, axis=0, keepdims=True)
    e = jnp.exp(zc - m)
    p = e * (1.0 / jnp.sum(e, axis=0, keepdims=True))
    col_b = lax.broadcasted_iota(jnp.int32, (1, ncol), 1) % bt
    tn = (((0,), (0,)), ((), ()))
    o = None
    for b in range(bt):
        p_b = jnp.where(col_b == b, p, 0.0).astype(bf16)
        o_b = lax.dot_general(p_b, vall[b * seq_rows:(b + 1) * seq_rows].astype(bf16), tn,
                              preferred_element_type=f32)
        o = o_b if o is None else o + o_b
    o = jnp.concatenate([o[hd * R:(hd + 1) * R] for hd in range(X_HEADS)], axis=-1)
    cc = _dot(o, wo_ref[...])
    y_ref[...] = (x + _rmsnorm(cc, gpost_ref[...])).reshape(steps, bt, D_MODEL)


def _sample_xattn(x_tm, mk, mv, gpre, wq, wo, gpost, bt):
    steps, nb, D = x_tm.shape
    kern = functools.partial(_sample_xattn_kernel, bt=bt, steps=steps)
    xs = pl.BlockSpec((steps, bt, D), lambda i: (0, i, 0))
    ms = pl.BlockSpec((bt, N_MEM, X_HEADS, X_HEAD_DIM), lambda i: (i, 0, 0, 0))
    return pl.pallas_call(
        kern, grid=(nb // bt,),
        in_specs=[xs, ms, ms, _const_spec(gpre.shape), _const_spec(wq.shape),
                  _const_spec(wo.shape), _const_spec(gpost.shape)],
        out_specs=xs, out_shape=jax.ShapeDtypeStruct((steps, nb, D), f32),
        name="sample_xattn",
        compiler_params=pltpu.CompilerParams(
            dimension_semantics=("arbitrary",), vmem_limit_bytes=VMEM_LIMIT),
    )(x_tm, mk, mv, gpre, wq, wo, gpost)


def _row(v, width=None):
    v = v.reshape(1, -1).astype(f32)
    if width is not None and v.shape[1] < width:
        v = jnp.pad(v, ((0, 0), (0, width - v.shape[1])))
    return v


def kernel(x_prompt, x_sample, state_ssm, state_conv, cache_win_k, cache_win_v, cache_mem_k, cache_mem_v, mem_prompt, g_mix_pre, w_in, conv_w, conv_b, dt_bias, a_log, d_skip, g_ssd_norm, sinks, w_out, g_mix_post, g_x_pre, w_xq, g_mem, w_xk, w_xv, w_xo, g_x_post, g_ffn_pre, w_gate, w_up, w_down, g_ffn_post):
    depth = w_in.shape[0]
    assert depth == 1
    B, L, D = x_prompt.shape
    NB, steps, _ = x_sample.shape
    li = 0

    w = w_in[li]
    wz, wxbc, wdt = w[:, 0:512], w[:, 512:1536], w[:, 1536:1544]
    wq, wk, wv = w[:, 1544:2056], w[:, 2056:2184], w[:, 2184:2312]
    wdt_p = jnp.pad(wdt, ((0, 0), (0, LANES - SSD_HEADS)))
    win_p = jnp.concatenate([wz, wxbc, wq * ATT_SCALE, wk, wv, wdt_p], axis=1).astype(bf16)
    wo_b = w_out[li].astype(bf16)

    gpre, gpost = _row(g_mix_pre[li]), _row(g_mix_post[li])
    convw, convb = conv_w[li].astype(f32), _row(conv_b[li])
    dtb, alog = _row(dt_bias[li], LANES), _row(a_log[li], LANES)
    dskip_e = _row(jnp.repeat(d_skip[li], SSD_HEAD_DIM))
    gssd = _row(g_ssd_norm[li])
    sk = sinks[li].astype(f32)

    mk2d, mv2d, mk4, mv4 = _memkv(mem_prompt.reshape(B * N_MEM, D), _row(g_mem[li]),
                                  w_xk[li].astype(bf16), w_xv[li].astype(bf16), tile=512)
    mk3, mv3 = mk2d.reshape(B, N_MEM, D), mv2d.reshape(B, N_MEM, D)
    x1, p_ssm, p_conv8, p_wk, p_wv = _prompt_mixer(
        x_prompt, gpre, win_p, convw, convb, dtb, alog, dskip_e, gssd, sk, wo_b, gpost, tile=512)
    wxq_b, wxo_b = (w_xq[li] * X_SCALE).astype(bf16), w_xo[li].astype(bf16)
    gxpre, gxpost = _row(g_x_pre[li]), _row(g_x_post[li])
    x2 = _prompt_xattn(x1, gxpre, wxq_b, mk3, mv3, wxo_b, gxpost, tile=512)
    wg_b, wu_b, wd_b = w_gate[li].astype(bf16), w_up[li].astype(bf16), w_down[li].astype(bf16)
    gfpre, gfpost = _row(g_ffn_pre[li]), _row(g_ffn_post[li])
    yp = _ffn(x2.reshape(B * L, D), gfpre, wg_b, wu_b, wd_b, gfpost, tile=512).reshape(B, L, D)

    bt = 8
    x_tm = jnp.transpose(x_sample, (1, 0, 2))
    cprev_tm = jnp.transpose(state_conv[li], (1, 0, 2))
    st = state_ssm[li].reshape(NB, SSD_WIDTH, SSD_STATE)
    ck = jnp.transpose(cache_win_k[li], (0, 2, 3, 1))
    cv = jnp.transpose(cache_win_v[li], (0, 2, 3, 1))
    sinkcol = jnp.repeat(sk, steps * bt).reshape(ATT_HEADS * steps * bt, 1)
    x1s, s_ssm, cnew_tm, s_wk, s_wv = _sample_mixer(
        x_tm, cprev_tm, st, ck, cv, gpre, win_p, convw, convb, dtb, alog, dskip_e, gssd,
        sinkcol, wo_b, gpost, bt=bt)
    cmk = cache_mem_k.reshape(NB, N_MEM, X_HEADS, X_HEAD_DIM)
    cmv = cache_mem_v.reshape(NB, N_MEM, X_HEADS, X_HEAD_DIM)
    x2s = _sample_xattn(x1s, cmk, cmv, gxpre, wxq_b, wxo_b, gxpost, bt=bt)
    ys_tm = _ffn(x2s.reshape(steps * NB, D), gfpre, wg_b, wu_b, wd_b, gfpost, tile=steps * NB)
    ys = jnp.transpose(ys_tm.reshape(steps, NB, D), (1, 0, 2))

    s_conv = jnp.transpose(cnew_tm, (1, 0, 2))
    kv_shape = (ATT_KV_HEADS, ATT_HEAD_DIM)
    return (
        yp, ys,
        p_ssm.reshape(1, B, SSD_HEADS, SSD_HEAD_DIM, SSD_STATE),
        p_conv8[:, SUBLANES - (SSD_CONV - 1):, :][None],
        jnp.transpose(p_wk.reshape(B, *kv_shape, WINDOW), (0, 3, 1, 2))[None],
        jnp.transpose(p_wv.reshape(B, *kv_shape, WINDOW), (0, 3, 1, 2))[None],
        mk4.reshape(1, B, N_MEM, X_HEADS, X_HEAD_DIM), mv4.reshape(1, B, N_MEM, X_HEADS, X_HEAD_DIM),
        s_ssm.reshape(1, NB, SSD_HEADS, SSD_HEAD_DIM, SSD_STATE),
        s_conv[None],
        jnp.transpose(s_wk, (0, 3, 1, 2))[None], jnp.transpose(s_wv, (0, 3, 1, 2))[None],
    )
```

```python
import functools

import jax
import jax.numpy as jnp
from jax import lax
from jax.experimental import pallas as pl
from jax.experimental.pallas import tpu as pltpu

f32 = jnp.float32
bf16 = jnp.bfloat16

D_MODEL = 1024
EPS = 1e-6
N_MEM = 256
SSD_HEADS = 8
SSD_HEAD_DIM = 64
SSD_WIDTH = 512
SSD_GROUPS = 2
SSD_STATE = 128
SSD_CONV = 4
SSD_CHUNK = 128
SSD_CONV_DIM = 1024
ATT_HEADS = 8
ATT_KV_HEADS = 2
ATT_HEAD_DIM = 64
ATT_WIDTH = 512
ATT_KV_WIDTH = 128
WINDOW = 128
ATT_SCALE = ATT_HEAD_DIM ** -0.5
X_HEADS = 4
X_HEAD_DIM = 256
X_SCALE = X_HEAD_DIM ** -0.5
D_FF = 2816
LANES = 128
SUBLANES = 8
VMEM_LIMIT = 56 * 1024 * 1024
STREAMS = 2
PROJ_PIECES, SCAN_PIECES, OUT_PIECES = 19, 10, 5
XATTN_LEAD = 5

P_Z, P_XBC, P_Q, P_K, P_V, P_DT, P_END = 0, 512, 1536, 2048, 2176, 2304, 2432


def _dot(a, b):
    return jnp.dot(a.astype(bf16), b.astype(bf16), preferred_element_type=f32)


def _dot_nt(a, b):
    return lax.dot_general(a.astype(bf16), b.astype(bf16), (((1,), (1,)), ((), ())),
                           preferred_element_type=f32)


def _split2(x):
    hi = x.astype(bf16)
    lo = (x - hi.astype(f32)).astype(bf16)
    return hi, lo


def _dot_x2(x, m):
    hi, lo = _split2(x)
    return (jnp.dot(hi, m, preferred_element_type=f32)
            + jnp.dot(lo, m, preferred_element_type=f32))


def _rmsnorm(x, g):
    ms = jnp.mean(x * x, axis=-1, keepdims=True)
    return x * lax.rsqrt(ms + EPS) * g


def _silu(x):
    return x * jax.nn.sigmoid(x)


def _softplus(x):
    return jnp.maximum(x, 0.0) + jnp.log1p(jnp.exp(-jnp.abs(x)))


def _head_expand_matrix():
    r = lax.broadcasted_iota(jnp.int32, (LANES, SSD_WIDTH), 0)
    c = lax.broadcasted_iota(jnp.int32, (LANES, SSD_WIDTH), 1)
    return (r == (c >> 6)).astype(bf16)


def _gated_group_norm(y, z, g):
    u = y * _silu(z)
    half = SSD_WIDTH // SSD_GROUPS
    parts = []
    for gi in range(SSD_GROUPS):
        ug = u[:, gi * half:(gi + 1) * half]
        parts.append(ug * lax.rsqrt(jnp.mean(ug * ug, axis=-1, keepdims=True) + EPS))
    return jnp.concatenate(parts, axis=-1) * g


def _prompt_mixer_kernel(x_ref, gpre_ref, win_ref, convw_ref, convb_ref, dtb_ref, alog_ref,
                         dskip_ref, gssd_ref, sinks_ref, wout_ref, gpost_ref,
                         y_ref, ssm_ref, conv_ref, wk_ref, wv_ref,
                         statet_sc, xbc_ext_sc, xbc_sc, z_sc, q_sc, k_sc, v_sc, dt_sc,
                         kprev_sc, vprev_sc, mix_sc, *, tile):
    i = pl.program_id(1)
    NC = tile // SSD_CHUNK
    ns = range(NC)
    C = SSD_CHUNK
    PW = 2 * LANES

    @pl.when(i == 0)
    def _init():
        statet_sc[...] = jnp.zeros_like(statet_sc)
        xbc_ext_sc[:, 0:SUBLANES, :] = jnp.zeros((STREAMS, SUBLANES, SSD_CONV_DIM), f32)
        kprev_sc[...] = jnp.zeros_like(kprev_sc)
        vprev_sc[...] = jnp.zeros_like(vprev_sc)

    lane = lax.broadcasted_iota(jnp.int32, (1, LANES), 1)
    a_row = -jnp.exp(alog_ref[...])
    expand = _head_expand_matrix()
    expand2 = jnp.concatenate([expand, expand], axis=0)
    row_i = lax.broadcasted_iota(jnp.int32, (C, C), 0)
    col_i = lax.broadcasted_iota(jnp.int32, (C, C), 1)
    lower = col_i <= row_i
    tri = lower.astype(bf16)
    tri3 = jnp.concatenate([tri, tri, tri], axis=1)
    lo_half = lane < ATT_HEAD_DIM
    half = SSD_WIDTH // SSD_GROUPS
    head_of_lane = lax.broadcasted_iota(jnp.int32, (1, half), 1) >> 6
    rows = [slice(n * C, (n + 1) * C) for n in ns]

    def stream(s):
        statet_s, xbc_ext_s, xbc_s, z_s = statet_sc.at[s], xbc_ext_sc.at[s], xbc_sc.at[s], z_sc.at[s]
        q_s, k_s, v_s, dt_s = q_sc.at[s], k_sc.at[s], v_sc.at[s], dt_sc.at[s]
        kprev_s, vprev_s, mix_s = kprev_sc.at[s], vprev_sc.at[s], mix_sc.at[s]

        x = x_ref[s, 0]
        h = _rmsnorm(x, gpre_ref[...]).astype(bf16)
        yield

        def proj(col):
            return jnp.dot(h, win_ref[:, col:col + PW], preferred_element_type=f32)

        def conv_cols(cb):
            cols = slice(cb * LANES, (cb + 1) * LANES)
            acc = convb_ref[:, cols]
            for j in range(SSD_CONV):
                off = SUBLANES - (SSD_CONV - 1) + j
                acc = acc + xbc_ext_s[off:off + tile, cols] * convw_ref[j:j + 1, cols]
            xbc_s[:, cols] = _silu(acc)

        def xbc_piece(pc):
            xbc_ext_s[SUBLANES:SUBLANES + tile, pc * PW:(pc + 1) * PW] = proj(P_XBC + pc * PW)

        xbc_piece(0)
        yield
        xbc_piece(1)
        yield
        conv_cols(0)
        yield
        xbc_piece(2)
        yield
        conv_cols(1)
        yield
        xbc_piece(3)
        yield
        conv_cols(2)
        yield
        z_s[:, 0:PW] = proj(P_Z)
        yield
        conv_cols(3)
        yield
        z_s[:, PW:2 * PW] = proj(P_Z + PW)
        yield
        conv_cols(4)
        yield
        q_s[:, 0:PW] = proj(P_Q)
        yield
        conv_cols(5)
        yield
        q_s[:, PW:2 * PW] = proj(P_Q + PW)
        yield
        conv_cols(6)
        yield
        kv = proj(P_K)
        k_s[...] = kv[:, 0:LANES]
        v_s[...] = kv[:, LANES:PW]
        yield
        conv_cols(7)
        yield
        dt_raw = jnp.dot(h, win_ref[:, P_DT:P_END], preferred_element_type=f32)
        dt_s[...] = jnp.where(lane < SSD_HEADS, _softplus(dt_raw + dtb_ref[...]), 0.0)
        tail = xbc_ext_s[tile:tile + SUBLANES, :]
        conv_ref[s, 0] = tail
        xbc_ext_s[0:SUBLANES, :] = tail
        yield

        GH = SSD_HEADS // SSD_GROUPS
        heads = range(ATT_HEADS)
        gs = range(SSD_GROUPS)
        zero_b = jnp.zeros((), bf16)

        yield
        xs = [xbc_s[rows[n], 0:SSD_WIDTH] for n in ns]
        bm = [xbc_s[rows[n], SSD_WIDTH:SSD_WIDTH + half] for n in ns]
        cm = [xbc_s[rows[n], SSD_WIDTH + half:SSD_CONV_DIM] for n in ns]
        dtc = [dt_s[rows[n], :] for n in ns]
        cs = []
        for n in ns:
            adt = dtc[n] * a_row
            a_hi = adt.astype(bf16)
            a_r1 = adt - a_hi.astype(f32)
            a_mid = a_r1.astype(bf16)
            a_lo = (a_r1 - a_mid.astype(f32)).astype(bf16)
            cs.append(jnp.dot(tri3, jnp.concatenate([a_hi, a_mid, a_lo], axis=0),
                              preferred_element_type=f32))

        yield
        q = [q_s[rows[n], :].astype(bf16) for n in ns]
        k = [k_s[rows[n], :] for n in ns]
        v = [v_s[rows[n], :] for n in ns]
        k_prev = [kprev_s[...]] + k[:-1]
        v_prev = [vprev_s[...]] + v[:-1]
        kprev_s[...] = k[-1]
        vprev_s[...] = v[-1]
        first_bias = jnp.where(i > 0, 0.0, -jnp.inf)
        k_lo, k_hi, v_lo, v_hi = [], [], [], []
        for n in ns:
            kk = jnp.concatenate([k_prev[n], k[n]], axis=0)
            vv = jnp.concatenate([v_prev[n], v[n]], axis=0)
            kk_r = pltpu.roll(kk, ATT_HEAD_DIM, 1)
            vv_r = pltpu.roll(vv, ATT_HEAD_DIM, 1)
            k_lo.append([jnp.where(lo_half, kk, 0.0).astype(bf16), jnp.where(lo_half, kk_r, 0.0).astype(bf16)])
            k_hi.append([jnp.where(lo_half, 0.0, kk_r).astype(bf16), jnp.where(lo_half, 0.0, kk).astype(bf16)])
            v_lo.append([jnp.where(lo_half, vv, 0.0).astype(bf16), jnp.where(lo_half, vv_r, 0.0).astype(bf16)])
            v_hi.append([jnp.where(lo_half, 0.0, vv_r).astype(bf16), jnp.where(lo_half, 0.0, vv).astype(bf16)])
        s_g = [[_dot_nt(jnp.concatenate([q[n][:, (2 * g) * LANES:(2 * g + 1) * LANES],
                                         q[n][:, (2 * g + 1) * LANES:(2 * g + 2) * LANES]], axis=0),
                        jnp.concatenate([k_lo[n][g], k_hi[n][g]], axis=0))
                for g in range(ATT_KV_HEADS)] for n in ns]

        yield
        cs_t = [cs[n].T for n in ns]
        fac_e, cb_all, bm_t = [], [], []
        for n in ns:
            tot = cs[n][C - 1:C, :]
            fac = jnp.concatenate([dtc[n], jnp.exp(cs[n]), jnp.exp(tot - cs[n])], axis=0)
            f_hi = fac.astype(bf16)
            f_lo = (fac - f_hi.astype(f32)).astype(bf16)
            fac_e.append(jnp.dot(jnp.concatenate([f_hi, f_lo], axis=1), expand2,
                                 preferred_element_type=f32))
            cb_all.append(_dot_nt(
                jnp.concatenate([cm[n][:, 0:SSD_STATE], cm[n][:, SSD_STATE:half]], axis=0),
                jnp.concatenate([bm[n][:, 0:SSD_STATE], bm[n][:, SSD_STATE:half]], axis=0)))
            bm_t.append(bm[n].T)

        yield
        def head_scores(n, hd):
            g, jb, sub = hd // 4, (hd // 2) % 2, hd % 2
            s_prev = s_g[n][g][jb * C:(jb + 1) * C, (2 * sub) * C:(2 * sub + 1) * C]
            s_cur = s_g[n][g][jb * C:(jb + 1) * C, (2 * sub + 1) * C:(2 * sub + 2) * C]
            return jnp.where(lower, s_cur, s_prev + first_bias if n == 0 else s_prev)
        sc = [[head_scores(n, hd) for hd in heads] for n in ns]
        sink = [sinks_ref[hd] for hd in heads]
        m = [[jnp.maximum(jnp.max(sc[n][hd], axis=-1, keepdims=True), sink[hd]) for hd in heads]
             for n in ns]

        yield
        xdt = [xs[n] * fac_e[n][0:C] for n in ns]
        ecs_e = [fac_e[n][C:2 * C] for n in ns]
        w_end = [xdt[n] * fac_e[n][2 * C:3 * C] for n in ns]
        xdt_b = [xdt[n].astype(bf16) for n in ns]
        contrib = [[_dot(bm_t[n][g * SSD_STATE:(g + 1) * SSD_STATE, :],
                         w_end[n][:, g * half:(g + 1) * half]) for g in gs] for n in ns]
        decay = [[jnp.exp(jnp.where(lower, cs[n][:, hh:hh + 1] - cs_t[n][hh:hh + 1, :], -jnp.inf))
                  for hh in range(SSD_HEADS)] for n in ns]

        yield
        e = [[jnp.exp(sc[n][hd] - m[n][hd]) for hd in heads] for n in ns]
        den = [[jnp.sum(e[n][hd], axis=-1, keepdims=True) + jnp.exp(sink[hd] - m[n][hd])
                for hd in heads] for n in ns]

        yield
        y_d = []
        for n in ns:
            y_n = []
            for g in gs:
                gl = slice(g * half, (g + 1) * half)
                cb = cb_all[n][g * C:(g + 1) * C, g * C:(g + 1) * C]
                m_parts = [(cb * decay[n][g * GH + r]).astype(bf16) for r in range(GH)]
                x_parts = [jnp.where(head_of_lane == r, xdt_b[n][:, gl], zero_b) for r in range(GH)]
                y_n.append(jnp.dot(jnp.concatenate(m_parts, axis=1), jnp.concatenate(x_parts, axis=0),
                                   preferred_element_type=f32))
            y_d.append(y_n)
        st = [statet_s[g] for g in gs]
        y_off = []
        for n in ns:
            y_off.append([_dot(cm[n][:, g * SSD_STATE:(g + 1) * SSD_STATE], st[g]) for g in gs])
            st = [st[g] * ecs_e[n][C - 1:C, g * half:(g + 1) * half] + contrib[n][g] for g in gs]
        for g in gs:
            statet_s[g] = st[g]

        yield
        p = [[(e[n][hd] * (1.0 / den[n][hd])).astype(bf16) for hd in heads] for n in ns]
        o_g = []
        for n in ns:
            o_n = []
            for g in range(ATT_KV_HEADS):
                p_rows = []
                for jb in range(2):
                    p_cols = []
                    for sub in range(2):
                        ph = p[n][g * 4 + jb * 2 + sub]
                        p_cols += [jnp.where(lower, zero_b, ph), jnp.where(lower, ph, zero_b)]
                    p_rows.append(jnp.concatenate(p_cols, axis=1))
                o_n.append(jnp.dot(jnp.concatenate(p_rows, axis=0),
                                   jnp.concatenate([v_lo[n][g], v_hi[n][g]], axis=0),
                                   preferred_element_type=f32))
            o_g.append(o_n)

        yield
        for n in ns:
            y = (jnp.concatenate([y_d[n][g] + y_off[n][g] * ecs_e[n][:, g * half:(g + 1) * half]
                                  for g in gs], axis=-1)
                 + xs[n] * dskip_ref[...])
            y_ssd = _gated_group_norm(y, z_s[rows[n], :], gssd_ref[...])
            mix_s[rows[n], 0:SSD_WIDTH] = y_ssd.astype(bf16)
        for n in ns:
            for g in range(ATT_KV_HEADS):
                for jb in range(2):
                    lo_l = SSD_WIDTH + (2 * g + jb) * LANES
                    mix_s[rows[n], lo_l:lo_l + LANES] = o_g[n][g][jb * C:(jb + 1) * C].astype(bf16)

        yield

        mix_in = mix_s[...]
        mix = []
        for pc in range(D_MODEL // PW):
            mix.append(jnp.dot(mix_in, wout_ref[:, pc * PW:(pc + 1) * PW], preferred_element_type=f32))
            yield
        y_ref[s, 0] = x + _rmsnorm(jnp.concatenate(mix, axis=-1), gpost_ref[...])

    def mixed(ga, na, gb, nb):
        done_b = 0
        for ka in range(na):
            next(ga, None)
            want_b = ((ka + 1) * nb) // na
            for _ in range(want_b - done_b):
                next(gb, None)
            done_b = want_b

    g0, g1 = [stream(s) for s in range(STREAMS)]
    for _ in range(PROJ_PIECES):
        next(g0, None)
    mixed(g0, SCAN_PIECES, g1, PROJ_PIECES)
    mixed(g1, SCAN_PIECES, g0, OUT_PIECES)
    for g in (g0, g1):
        for _ in g:
            pass

    @pl.when(i == pl.num_programs(1) - 1)
    def _final_state():
        for s in range(STREAMS):
            for g in range(SSD_GROUPS):
                ssm_ref[s, 0, g * half:(g + 1) * half, :] = statet_sc[s, g].T
            wk_ref[s, 0] = k_sc[s, tile - WINDOW:tile, :].T
            wv_ref[s, 0] = v_sc[s, tile - WINDOW:tile, :].T


def _const_spec(shape):
    nd = len(shape)
    return pl.BlockSpec(shape, lambda *_: (0,) * nd)


def _prompt_mixer(x, gpre, win, convw, convb, dtb, alog, dskip_e, gssd, sinks, wout, gpost, tile):
    B, L, D = x.shape
    S = STREAMS
    G = B // S
    kern = functools.partial(_prompt_mixer_kernel, tile=tile)

    def per_seq(rows, width):
        return pl.BlockSpec((S, 1, rows, width), lambda b, i: (0, b, 0, 0))

    tile_spec = pl.BlockSpec((S, 1, tile, D), lambda b, i: (0, b, i, 0))
    out_shape = (
        jax.ShapeDtypeStruct((S, G, L, D), f32),
        jax.ShapeDtypeStruct((S, G, SSD_WIDTH, SSD_STATE), f32),
        jax.ShapeDtypeStruct((S, G, SUBLANES, SSD_CONV_DIM), f32),
        jax.ShapeDtypeStruct((S, G, ATT_KV_WIDTH, WINDOW), f32),
        jax.ShapeDtypeStruct((S, G, ATT_KV_WIDTH, WINDOW), f32),
    )
    in_specs = [
        tile_spec,
        _const_spec(gpre.shape), _const_spec(win.shape), _const_spec(convw.shape),
        _const_spec(convb.shape), _const_spec(dtb.shape), _const_spec(alog.shape),
        _const_spec(dskip_e.shape), _const_spec(gssd.shape),
        pl.BlockSpec(memory_space=pltpu.SMEM),
        _const_spec(wout.shape), _const_spec(gpost.shape),
    ]
    out_specs = (
        tile_spec,
        per_seq(SSD_WIDTH, SSD_STATE),
        per_seq(SUBLANES, SSD_CONV_DIM),
        per_seq(ATT_KV_WIDTH, WINDOW),
        per_seq(ATT_KV_WIDTH, WINDOW),
    )
    scratch = [
        pltpu.VMEM((S, SSD_GROUPS, SSD_STATE, SSD_WIDTH // SSD_GROUPS), f32),
        pltpu.VMEM((S, tile + 2 * SUBLANES, SSD_CONV_DIM), f32),
        pltpu.VMEM((S, tile, SSD_CONV_DIM), f32),
        pltpu.VMEM((S, tile, SSD_WIDTH), f32),
        pltpu.VMEM((S, tile, ATT_WIDTH), f32),
        pltpu.VMEM((S, tile, ATT_KV_WIDTH), f32),
        pltpu.VMEM((S, tile, ATT_KV_WIDTH), f32),
        pltpu.VMEM((S, tile, LANES), f32),
        pltpu.VMEM((S, WINDOW, ATT_KV_WIDTH), f32),
        pltpu.VMEM((S, WINDOW, ATT_KV_WIDTH), f32),
        pltpu.VMEM((S, tile, 2 * SSD_WIDTH), bf16),
    ]
    outs = pl.pallas_call(
        kern, grid=(G, L // tile), in_specs=in_specs, out_specs=out_specs, out_shape=out_shape,
        scratch_shapes=scratch, name="prompt_mixer",
        compiler_params=pltpu.CompilerParams(
            dimension_semantics=("arbitrary", "arbitrary"), vmem_limit_bytes=VMEM_LIMIT),
    )(x.reshape(S, G, L, D), gpre, win, convw, convb, dtb, alog, dskip_e, gssd, sinks, wout, gpost)
    return tuple(o.reshape(B, *o.shape[2:]) for o in outs)


def _memkv_kernel(m_ref, g_ref, wk_ref, wv_ref, k_ref, v_ref, kh_ref, vh_ref):
    mn = _rmsnorm(m_ref[...], g_ref[...]).astype(bf16)
    k = jnp.dot(mn, wk_ref[...], preferred_element_type=f32)
    v = jnp.dot(mn, wv_ref[...], preferred_element_type=f32)
    k_ref[...] = k
    v_ref[...] = v
    for hd in range(X_HEADS):
        kh_ref[:, hd, :] = k[:, hd * X_HEAD_DIM:(hd + 1) * X_HEAD_DIM]
        vh_ref[:, hd, :] = v[:, hd * X_HEAD_DIM:(hd + 1) * X_HEAD_DIM]


def _memkv(mem2d, g, wk, wv, tile):
    n, d = mem2d.shape
    row = pl.BlockSpec((tile, d), lambda i: (i, 0))
    hrow = pl.BlockSpec((tile, X_HEADS, X_HEAD_DIM), lambda i: (i, 0, 0))
    flat = jax.ShapeDtypeStruct((n, d), f32)
    heads = jax.ShapeDtypeStruct((n, X_HEADS, X_HEAD_DIM), f32)
    return pl.pallas_call(
        _memkv_kernel, grid=(n // tile,),
        in_specs=[row, _const_spec(g.shape), _const_spec(wk.shape), _const_spec(wv.shape)],
        out_specs=(row, row, hrow, hrow),
        out_shape=(flat, flat, heads, heads),
        name="memory_kv",
        compiler_params=pltpu.CompilerParams(
            dimension_semantics=("arbitrary",), vmem_limit_bytes=VMEM_LIMIT),
    )(mem2d, g, wk, wv)


def _prompt_xattn_kernel(x_ref, gpre_ref, wq_ref, mk_ref, mv_ref, wo_ref, gpost_ref, y_ref):
    hs = range(X_HEADS)
    sl = [slice(hd * X_HEAD_DIM, (hd + 1) * X_HEAD_DIM) for hd in hs]

    def stream(s):
        x = x_ref[s, 0]
        hn = _rmsnorm(x, gpre_ref[...]).astype(bf16)
        yield
        q = []
        for hd in hs:
            q.append(jnp.dot(hn, wq_ref[:, sl[hd]], preferred_element_type=f32))
            yield
        sc = [_dot_nt(q[hd], mk_ref[s, 0, :, sl[hd]]) for hd in hs]
        yield
        m = [jnp.max(sc[hd], axis=-1, keepdims=True) for hd in hs]
        e = [jnp.exp(sc[hd] - m[hd]) for hd in hs]
        yield
        r = [1.0 / jnp.sum(e[hd], axis=-1, keepdims=True) for hd in hs]
        p = [(e[hd] * r[hd]).astype(bf16) for hd in hs]
        yield
        o = jnp.concatenate([_dot(p[hd], mv_ref[s, 0, :, sl[hd]]) for hd in hs], axis=-1).astype(bf16)
        yield
        c = []
        for hd in hs:
            c.append(jnp.dot(o, wo_ref[:, sl[hd]], preferred_element_type=f32))
            yield
        y_ref[s, 0] = x + _rmsnorm(jnp.concatenate(c, axis=-1), gpost_ref[...])

    live = [stream(s) for s in range(STREAMS)]
    for _ in range(XATTN_LEAD):
        next(live[0], None)
    while live:
        live = [g for g in live if next(g, True) is None]


def _prompt_xattn(x, gpre, wq, mk, mv, wo, gpost, tile):
    B, L, D = x.shape
    S = STREAMS
    G = B // S
    xs = pl.BlockSpec((S, 1, tile, D), lambda b, i: (0, b, i, 0))
    ms = pl.BlockSpec((S, 1, N_MEM, D), lambda b, i: (0, b, 0, 0))
    return pl.pallas_call(
        _prompt_xattn_kernel, grid=(G, L // tile),
        in_specs=[xs, _const_spec(gpre.shape), _const_spec(wq.shape), ms, ms,
                  _const_spec(wo.shape), _const_spec(gpost.shape)],
        out_specs=xs, out_shape=jax.ShapeDtypeStruct((S, G, L, D), f32),
        name="prompt_xattn",
        compiler_params=pltpu.CompilerParams(
            dimension_semantics=("arbitrary", "arbitrary"), vmem_limit_bytes=VMEM_LIMIT),
    )(x.reshape(S, G, L, D), gpre, wq, mk.reshape(S, G, N_MEM, D), mv.reshape(S, G, N_MEM, D),
      wo, gpost).reshape(B, L, D)


def _ffn_kernel(x_ref, gpre_ref, wg_ref, wu_ref, wd_ref, gpost_ref, y_ref):
    x = x_ref[...]
    hf = _rmsnorm(x, gpre_ref[...]).astype(bf16)
    gate = jnp.dot(hf, wg_ref[...], preferred_element_type=f32)
    up = jnp.dot(hf, wu_ref[...], preferred_element_type=f32)
    act = (_silu(gate) * up).astype(bf16)
    f = jnp.dot(act, wd_ref[...], preferred_element_type=f32)
    y_ref[...] = x + _rmsnorm(f, gpost_ref[...])


def _ffn(x2d, gpre, wg, wu, wd, gpost, tile):
    n, d = x2d.shape
    row = pl.BlockSpec((tile, d), lambda i: (i, 0))
    return pl.pallas_call(
        _ffn_kernel, grid=(n // tile,),
        in_specs=[row, _const_spec(gpre.shape), _const_spec(wg.shape), _const_spec(wu.shape),
                  _const_spec(wd.shape), _const_spec(gpost.shape)],
        out_specs=row, out_shape=jax.ShapeDtypeStruct((n, d), f32),
        name="ffn",
        compiler_params=pltpu.CompilerParams(
            dimension_semantics=("arbitrary",), vmem_limit_bytes=VMEM_LIMIT),
    )(x2d, gpre, wg, wu, wd, gpost)


def _pad_rows(a, rows):
    if a.shape[0] == rows:
        return a
    return jnp.concatenate([a, jnp.zeros((rows - a.shape[0], a.shape[1]), a.dtype)], axis=0)


def _sample_mixer_kernel(x_ref, cprev_ref, st_ref, ck_ref, cv_ref,
                         gpre_ref, win_ref, convw_ref, convb_ref, dtb_ref, alog_ref,
                         dskip_ref, gssd_ref, sinkcol_ref, wout_ref, gpost_ref,
                         y_ref, ssm_ref, cnew_ref, wk_ref, wv_ref, *, bt, steps):
    R = steps * bt
    half = SSD_WIDTH // SSD_GROUPS
    x = x_ref[...].reshape(R, D_MODEL)
    h = _rmsnorm(x, gpre_ref[...]).astype(bf16)
    z = jnp.dot(h, win_ref[:, P_Z:P_XBC], preferred_element_type=f32)
    u = jnp.dot(h, win_ref[:, P_XBC:P_Q], preferred_element_type=f32)
    q = jnp.dot(h, win_ref[:, P_Q:P_K], preferred_element_type=f32)
    k_new = jnp.dot(h, win_ref[:, P_K:P_V], preferred_element_type=f32)
    v_new = jnp.dot(h, win_ref[:, P_V:P_DT], preferred_element_type=f32)
    dt_raw = jnp.dot(h, win_ref[:, P_DT:P_END], preferred_element_type=f32)
    lane = lax.broadcasted_iota(jnp.int32, (1, LANES), 1)
    dt = jnp.where(lane < SSD_HEADS, _softplus(dt_raw + dtb_ref[...]), 0.0)

    def slab(a, t):
        return a[t * bt:(t + 1) * bt]

    HD = ATT_HEAD_DIM
    GH = ATT_HEADS // ATT_KV_HEADS
    GR = GH * R
    reps = GR // bt
    kvg = range(ATT_KV_HEADS)
    keep = WINDOW - steps

    kn_t = _pad_rows(k_new, LANES).T
    vn_t = _pad_rows(v_new, LANES).T

    ridx = lax.broadcasted_iota(jnp.int32, (GR, 1), 0)
    rb = ridx % bt
    rt = (ridx // bt) % steps
    qg = [jnp.concatenate([q[:, (g * GH + hl) * HD:(g * GH + hl + 1) * HD] for hl in range(GH)],
                          axis=0) for g in kvg]
    qg_b = [qg[g].astype(bf16) for g in kvg]
    s_cb = [[_dot(qg_b[g], ck_ref[b, g]) for b in range(bt)] for g in kvg]

    hist = [cprev_ref[j] for j in range(SSD_CONV - 1)] + [slab(u, t) for t in range(steps)]
    xbc_t = []
    for t in range(steps):
        acc = convb_ref[...]
        for j in range(SSD_CONV):
            acc = acc + hist[t + j] * convw_ref[j:j + 1, :]
        xbc_t.append(_silu(acc))
    for j in range(SSD_CONV - 1):
        cnew_ref[j] = hist[steps + j]
    xbc = jnp.concatenate(xbc_t, axis=0)
    xs = xbc[:, 0:SSD_WIDTH]
    bm = xbc[:, SSD_WIDTH:SSD_WIDTH + half]
    cm = xbc[:, SSD_WIDTH + half:SSD_CONV_DIM]
    a_row = -jnp.exp(alog_ref[...])
    adt = dt * a_row
    cs_t = [slab(adt, 0)]
    for t in range(1, steps):
        cs_t.append(cs_t[-1] + slab(adt, t))
    cs = jnp.concatenate(cs_t, axis=0)
    tot = cs_t[-1]
    tot_rows = jnp.concatenate([tot] * steps, axis=0)
    expand = _head_expand_matrix()
    expand2 = jnp.concatenate([expand, expand], axis=0)
    pairs = [(t, s2) for t in range(steps) for s2 in range(t)]
    fac = jnp.concatenate([dt, jnp.exp(cs), jnp.exp(tot_rows - cs), jnp.exp(tot)]
                          + [jnp.exp(cs_t[t] - cs_t[s2]) for t, s2 in pairs], axis=0)
    f_hi, f_lo = _split2(fac)
    fac_e = jnp.dot(jnp.concatenate([f_hi, f_lo], axis=1), expand2, preferred_element_type=f32)
    gr = lax.broadcasted_iota(jnp.int32, (half, SSD_WIDTH), 0)
    gc = lax.broadcasted_iota(jnp.int32, (half, SSD_WIDTH), 1)
    gsum = ((gr >> 7) == (gc >> 8)).astype(bf16)
    gsum2 = jnp.concatenate([gsum, gsum], axis=0)
    cb_pairs = [(t, s2) for t in range(steps) for s2 in range(t + 1)]
    prod = jnp.concatenate([slab(cm, t) * slab(bm, s2) for t, s2 in cb_pairs], axis=0)
    c_hi, c_lo = _split2(prod)
    cb_e = jnp.dot(jnp.concatenate([c_hi, c_lo], axis=1), gsum2, preferred_element_type=f32)

    sel_r = lax.broadcasted_iota(jnp.int32, (LANES, WINDOW), 0)
    sel_l = lax.broadcasted_iota(jnp.int32, (LANES, WINDOW), 1)
    sel = [((sel_r % bt == b) & (sel_r < R) & (sel_l - keep == sel_r // bt)).astype(bf16)
           for b in range(bt)]
    new_k = [_dot_x2(kn_t, sel[b]) for b in range(bt)]
    new_v = [_dot_x2(vn_t, sel[b]) for b in range(bt)]

    jcol = lax.broadcasted_iota(jnp.int32, (GR, WINDOW), 1)
    in_window = jcol > rt
    s_c, m, s_n = [], [], []
    for g in kvg:
        acc = jnp.zeros((GR, WINDOW), f32)
        for b in range(bt):
            acc = jnp.where(rb == b, s_cb[g][b], acc)
        s_c.append(jnp.where(in_window, acc, -jnp.inf))
    sink = [sinkcol_ref[g * GR:(g + 1) * GR, :] for g in kvg]
    for g in kvg:
        mg = jnp.maximum(jnp.max(s_c[g], axis=-1, keepdims=True), sink[g])
        sn_g = []
        for t2 in range(steps):
            kt = jnp.concatenate([slab(k_new, t2)[:, g * HD:(g + 1) * HD]] * reps, axis=0)
            sn = jnp.where(rt >= t2, jnp.sum(qg[g] * kt, axis=-1, keepdims=True), -jnp.inf)
            sn_g.append(sn)
            mg = jnp.maximum(mg, sn)
        m.append(mg)
        s_n.append(sn_g)
    e_c = [jnp.exp(s_c[g] - m[g]) for g in kvg]
    e_n = [[jnp.exp(sn - m[g]) for sn in s_n[g]] for g in kvg]
    rinv = []
    for g in kvg:
        den = jnp.sum(e_c[g], axis=-1, keepdims=True) + jnp.exp(sink[g] - m[g])
        for en in e_n[g]:
            den = den + en
        rinv.append(1.0 / den)
    p_c = [(e_c[g] * rinv[g]).astype(bf16) for g in kvg]

    xdt = xs * fac_e[0:R]
    ecs_e = fac_e[R:2 * R]
    w_end = xdt * fac_e[2 * R:3 * R]
    dec_e = fac_e[3 * R:3 * R + bt]
    pair_decay = {pr: fac_e[3 * R + (n + 1) * bt:3 * R + (n + 2) * bt] for n, pr in enumerate(pairs)}
    y_t = []
    for t in range(steps):
        acc = None
        for s2 in range(t + 1):
            n = cb_pairs.index((t, s2))
            coef = cb_e[n * bt:(n + 1) * bt]
            if s2 < t:
                coef = coef * pair_decay[(t, s2)]
            term = coef * slab(xdt, s2)
            acc = term if acc is None else acc + term
        y_t.append(acc)
    y_intra = jnp.concatenate(y_t, axis=0)
    b_idx = lax.broadcasted_iota(jnp.int32, (bt, 1, LANES), 0)
    l_idx = lax.broadcasted_iota(jnp.int32, (bt, 1, LANES), 2)
    pair = ((l_idx & (bt - 1)) == b_idx) & (l_idx < R)
    own = (l_idx == b_idx)
    gsl = [slice(g * half, (g + 1) * half) for g in range(SSD_GROUPS)]
    h0 = [st_ref[:, gsl[g], :] for g in range(SSD_GROUPS)]
    zz = [_dot_nt(h0[g].reshape(bt * half, SSD_STATE),
                  _pad_rows(cm[:, g * SSD_STATE:(g + 1) * SSD_STATE], LANES)).reshape(bt, half, LANES)
          for g in range(SSD_GROUPS)]
    wt = [_pad_rows(w_end[:, gsl[g]], LANES).T for g in range(SSD_GROUPS)]
    contrib = [_dot(jnp.where(pair, wt[g][None], 0.0).reshape(bt * half, LANES),
                    _pad_rows(bm[:, g * SSD_STATE:(g + 1) * SSD_STATE], LANES)
                    ).reshape(bt, half, SSD_STATE) for g in range(SSD_GROUPS)]
    dec_t = [_pad_rows(dec_e[:, gsl[g]], LANES).T for g in range(SSD_GROUPS)]

    o = []
    for g in kvg:
        pv = [_dot_nt(p_c[g], cv_ref[b, g]) for b in range(bt)]
        og = jnp.zeros((GR, HD), f32)
        for b in range(bt):
            og = jnp.where(rb == b, pv[b], og)
        for t2 in range(steps):
            vt = jnp.concatenate([slab(v_new, t2)[:, g * HD:(g + 1) * HD]] * reps, axis=0)
            og = og + (e_n[g][t2] * rinv[g]) * vt
        o.append(og)

    lane_w = lax.broadcasted_iota(jnp.int32, (1, WINDOW), 1)
    for b in range(bt):
        for g in kvg:
            gs = slice(g * HD, (g + 1) * HD)
            wk_ref[b, g] = jnp.where(lane_w < keep, pltpu.roll(ck_ref[b, g], keep, 1), new_k[b][gs])
            wv_ref[b, g] = jnp.where(lane_w < keep, pltpu.roll(cv_ref[b, g], keep, 1), new_v[b][gs])

    y_off_parts = []
    for g in range(SSD_GROUPS):
        yt = jnp.sum(jnp.where(pair, zz[g], 0.0), axis=0)
        y_off_parts.append(yt.T[0:R, :])
        dec = jnp.sum(jnp.where(own, dec_t[g][None], 0.0), axis=-1, keepdims=True)
        ssm_ref[:, gsl[g], :] = h0[g] * dec + contrib[g]
    y_off = jnp.concatenate(y_off_parts, axis=-1) * ecs_e
    y = y_intra + y_off + xs * dskip_ref[...]
    y_ssd = _gated_group_norm(y, z, gssd_ref[...])

    mix = jnp.dot(y_ssd.astype(bf16), wout_ref[0:SSD_WIDTH, :], preferred_element_type=f32)
    for g in kvg:
        for hl in range(GH):
            hd = g * GH + hl
            mix = mix + jnp.dot(o[g][hl * R:(hl + 1) * R].astype(bf16),
                                wout_ref[SSD_WIDTH + hd * HD:SSD_WIDTH + (hd + 1) * HD, :],
                                preferred_element_type=f32)
    y_ref[...] = (x + _rmsnorm(mix, gpost_ref[...])).reshape(steps, bt, D_MODEL)


def _sample_mixer(x_tm, cprev_tm, st, ck, cv, gpre, win, convw, convb, dtb, alog, dskip_e, gssd,
                  sinkcol, wout, gpost, bt):
    steps, nb, D = x_tm.shape
    kern = functools.partial(_sample_mixer_kernel, bt=bt, steps=steps)
    tm = lambda w: pl.BlockSpec((steps, bt, w), lambda i: (0, i, 0))
    win_spec = pl.BlockSpec((bt, ATT_KV_HEADS, ATT_HEAD_DIM, WINDOW), lambda i: (i, 0, 0, 0))
    in_specs = [
        tm(D),
        pl.BlockSpec((SSD_CONV - 1, bt, SSD_CONV_DIM), lambda i: (0, i, 0)),
        pl.BlockSpec((bt, SSD_WIDTH, SSD_STATE), lambda i: (i, 0, 0)),
        win_spec, win_spec,
    ] + [_const_spec(a.shape) for a in (gpre, win, convw, convb, dtb, alog, dskip_e, gssd,
                                        sinkcol, wout, gpost)]
    out_specs = (
        tm(D),
        pl.BlockSpec((bt, SSD_WIDTH, SSD_STATE), lambda i: (i, 0, 0)),
        pl.BlockSpec((SSD_CONV - 1, bt, SSD_CONV_DIM), lambda i: (0, i, 0)),
        win_spec, win_spec,
    )
    out_shape = (
        jax.ShapeDtypeStruct((steps, nb, D), f32),
        jax.ShapeDtypeStruct((nb, SSD_WIDTH, SSD_STATE), f32),
        jax.ShapeDtypeStruct((SSD_CONV - 1, nb, SSD_CONV_DIM), f32),
        jax.ShapeDtypeStruct(ck.shape, f32),
        jax.ShapeDtypeStruct(cv.shape, f32),
    )
    return pl.pallas_call(
        kern, grid=(nb // bt,), in_specs=in_specs, out_specs=out_specs, out_shape=out_shape,
        name="sample_mixer",
        compiler_params=pltpu.CompilerParams(
            dimension_semantics=("arbitrary",), vmem_limit_bytes=VMEM_LIMIT),
    )(x_tm, cprev_tm, st, ck, cv, gpre, win, convw, convb, dtb, alog, dskip_e, gssd, sinkcol,
      wout, gpost)


def _sample_xattn_kernel(x_ref, mk_ref, mv_ref, gpre_ref, wq_ref, wo_ref, gpost_ref, y_ref,
                         *, bt, steps):
    R = steps * bt
    nrow = bt * N_MEM * X_HEADS
    x = x_ref[...].reshape(R, D_MODEL)
    hn = _rmsnorm(x, gpre_ref[...]).astype(bf16)
    q = jnp.dot(hn, wq_ref[...], preferred_element_type=f32)
    qs = jnp.concatenate([q[:, hd * X_HEAD_DIM:(hd + 1) * X_HEAD_DIM] for hd in range(X_HEADS)],
                         axis=0)
    kall = mk_ref[...].reshape(nrow, X_HEAD_DIM)
    vall = mv_ref[...].reshape(nrow, X_HEAD_DIM)
    ncol = X_HEADS * R
    seq_rows = N_MEM * X_HEADS
    z = _dot_nt(kall, qs).reshape(bt, seq_rows, ncol)
    b_i = lax.broadcasted_iota(jnp.int32, (bt, 1, ncol), 0)
    c_i = lax.broadcasted_iota(jnp.int32, (bt, 1, ncol), 2)
    zc = jnp.sum(jnp.where(c_i % bt == b_i, z, 0.0), axis=0)
    zc = zc.reshape(seq_rows // SUBLANES, SUBLANES, ncol)
    r_h = lax.broadcasted_iota(jnp.int32, (1, SUBLANES, ncol), 1) % X_HEADS
    c_h = lax.broadcasted_iota(jnp.int32, (1, SUBLANES, ncol), 2) // R
    zc = jnp.where(r_h == c_h, zc, -jnp.inf).reshape(seq_rows, ncol)
    m = jnp.max(zc, axis=0, keepdims=True)
    e = jnp.exp(zc - m)
    p = e * (1.0 / jnp.sum(e, axis=0, keepdims=True))
    col_b = lax.broadcasted_iota(jnp.int32, (1, ncol), 1) % bt
    tn = (((0,), (0,)), ((), ()))
    o = None
    for b in range(bt):
        p_b = jnp.where(col_b == b, p, 0.0).astype(bf16)
        o_b = lax.dot_general(p_b, vall[b * seq_rows:(b + 1) * seq_rows].astype(bf16), tn,
                              preferred_element_type=f32)
        o = o_b if o is None else o + o_b
    o = jnp.concatenate([o[hd * R:(hd + 1) * R] for hd in range(X_HEADS)], axis=-1)
    cc = _dot(o, wo_ref[...])
    y_ref[...] = (x + _rmsnorm(cc, gpost_ref[...])).reshape(steps, bt, D_MODEL)


def _sample_xattn(x_tm, mk, mv, gpre, wq, wo, gpost, bt):
    steps, nb, D = x_tm.shape
    kern = functools.partial(_sample_xattn_kernel, bt=bt, steps=steps)
    xs = pl.BlockSpec((steps, bt, D), lambda i: (0, i, 0))
    ms = pl.BlockSpec((bt, N_MEM, X_HEADS, X_HEAD_DIM), lambda i: (i, 0, 0, 0))
    return pl.pallas_call(
        kern, grid=(nb // bt,),
        in_specs=[xs, ms, ms, _const_spec(gpre.shape), _const_spec(wq.shape),
                  _const_spec(wo.shape), _const_spec(gpost.shape)],
        out_specs=xs, out_shape=jax.ShapeDtypeStruct((steps, nb, D), f32),
        name="sample_xattn",
        compiler_params=pltpu.CompilerParams(
            dimension_semantics=("arbitrary",), vmem_limit_bytes=VMEM_LIMIT),
    )(x_tm, mk, mv, gpre, wq, wo, gpost)


def _row(v, width=None):
    v = v.reshape(1, -1).astype(f32)
    if width is not None and v.shape[1] < width:
        v = jnp.pad(v, ((0, 0), (0, width - v.shape[1])))
    return v


def kernel(x_prompt, x_sample, state_ssm, state_conv, cache_win_k, cache_win_v, cache_mem_k, cache_mem_v, mem_prompt, g_mix_pre, w_in, conv_w, conv_b, dt_bias, a_log, d_skip, g_ssd_norm, sinks, w_out, g_mix_post, g_x_pre, w_xq, g_mem, w_xk, w_xv, w_xo, g_x_post, g_ffn_pre, w_gate, w_up, w_down, g_ffn_post):
    depth = w_in.shape[0]
    assert depth == 1
    B, L, D = x_prompt.shape
    NB, steps, _ = x_sample.shape
    li = 0

    w = w_in[li]
    wz, wxbc, wdt = w[:, 0:512], w[:, 512:1536], w[:, 1536:1544]
    wq, wk, wv = w[:, 1544:2056], w[:, 2056:2184], w[:, 2184:2312]
    wdt_p = jnp.pad(wdt, ((0, 0), (0, LANES - SSD_HEADS)))
    win_p = jnp.concatenate([wz, wxbc, wq * ATT_SCALE, wk, wv, wdt_p], axis=1).astype(bf16)
    wo_b = w_out[li].astype(bf16)

    gpre, gpost = _row(g_mix_pre[li]), _row(g_mix_post[li])
    convw, convb = conv_w[li].astype(f32), _row(conv_b[li])
    dtb, alog = _row(dt_bias[li], LANES), _row(a_log[li], LANES)
    dskip_e = _row(jnp.repeat(d_skip[li], SSD_HEAD_DIM))
    gssd = _row(g_ssd_norm[li])
    sk = sinks[li].astype(f32)

    mk2d, mv2d, mk4, mv4 = _memkv(mem_prompt.reshape(B * N_MEM, D), _row(g_mem[li]),
                                  w_xk[li].astype(bf16), w_xv[li].astype(bf16), tile=512)
    mk3, mv3 = mk2d.reshape(B, N_MEM, D), mv2d.reshape(B, N_MEM, D)
    x1, p_ssm, p_conv8, p_wk, p_wv = _prompt_mixer(
        x_prompt, gpre, win_p, convw, convb, dtb, alog, dskip_e, gssd, sk, wo_b, gpost, tile=512)
    wxq_b, wxo_b = (w_xq[li] * X_SCALE).astype(bf16), w_xo[li].astype(bf16)
    gxpre, gxpost = _row(g_x_pre[li]), _row(g_x_post[li])
    x2 = _prompt_xattn(x1, gxpre, wxq_b, mk3, mv3, wxo_b, gxpost, tile=512)
    wg_b, wu_b, wd_b = w_gate[li].astype(bf16), w_up[li].astype(bf16), w_down[li].astype(bf16)
    gfpre, gfpost = _row(g_ffn_pre[li]), _row(g_ffn_post[li])
    yp = _ffn(x2.reshape(B * L, D), gfpre, wg_b, wu_b, wd_b, gfpost, tile=512).reshape(B, L, D)

    bt = 8
    x_tm = jnp.transpose(x_sample, (1, 0, 2))
    cprev_tm = jnp.transpose(state_conv[li], (1, 0, 2))
    st = state_ssm[li].reshape(NB, SSD_WIDTH, SSD_STATE)
    ck = jnp.transpose(cache_win_k[li], (0, 2, 3, 1))
    cv = jnp.transpose(cache_win_v[li], (0, 2, 3, 1))
    sinkcol = jnp.repeat(sk, steps * bt).reshape(ATT_HEADS * steps * bt, 1)
    x1s, s_ssm, cnew_tm, s_wk, s_wv = _sample_mixer(
        x_tm, cprev_tm, st, ck, cv, gpre, win_p, convw, convb, dtb, alog, dskip_e, gssd,
        sinkcol, wo_b, gpost, bt=bt)
    cmk = cache_mem_k.reshape(NB, N_MEM, X_HEADS, X_HEAD_DIM)
    cmv = cache_mem_v.reshape(NB, N_MEM, X_HEADS, X_HEAD_DIM)
    x2s = _sample_xattn(x1s, cmk, cmv, gxpre, wxq_b, wxo_b, gxpost, bt=bt)
    ys_tm = _ffn(x2s.reshape(steps * NB, D), gfpre, wg_b, wu_b, wd_b, gfpost, tile=steps * NB)
    ys = jnp.transpose(ys_tm.reshape(steps, NB, D), (1, 0, 2))

    s_conv = jnp.transpose(cnew_tm, (1, 0, 2))
    kv_shape = (ATT_KV_HEADS, ATT_HEAD_DIM)
    return (
        yp, ys,
        p_ssm.reshape(1, B, SSD_HEADS, SSD_HEAD_DIM, SSD_STATE),
        p_conv8[:, SUBLANES - (SSD_CONV - 1):, :][None],
        jnp.transpose(p_wk.reshape(B, *kv_shape, WINDOW), (0, 3, 1, 2))[None],
        jnp.transpose(p_wv.reshape(B, *kv_shape, WINDOW), (0, 3, 1, 2))[None],
        mk4.reshape(1, B, N_MEM, X_HEADS, X_HEAD_DIM), mv4.reshape(1, B, N_MEM, X_HEADS, X_HEAD_DIM),
        s_ssm.reshape(1, NB, SSD_HEADS, SSD_HEAD_DIM, SSD_STATE),
        s_conv[None],
        jnp.transpose(s_wk, (0, 3, 1, 2))[None], jnp.transpose(s_wv, (0, 3, 1, 2))[None],
    )
```

```python
import functools

import jax
import jax.numpy as jnp
from jax import lax
from jax.experimental import pallas as pl
from jax.experimental.pallas import tpu as pltpu

f32 = jnp.float32
bf16 = jnp.bfloat16

D_MODEL = 1024
EPS = 1e-6
N_MEM = 256
SSD_HEADS = 8
SSD_HEAD_DIM = 64
SSD_WIDTH = 512
SSD_GROUPS = 2
SSD_STATE = 128
SSD_CONV = 4
SSD_CHUNK = 128
SSD_CONV_DIM = 1024
ATT_HEADS = 8
ATT_KV_HEADS = 2
ATT_HEAD_DIM = 64
ATT_WIDTH = 512
ATT_KV_WIDTH = 128
WINDOW = 128
ATT_SCALE = ATT_HEAD_DIM ** -0.5
X_HEADS = 4
X_HEAD_DIM = 256
X_SCALE = X_HEAD_DIM ** -0.5
D_FF = 2816
LANES = 128
SUBLANES = 8
VMEM_LIMIT = 56 * 1024 * 1024
STREAMS = 2
PROJ_PIECES, SCAN_PIECES, OUT_PIECES = 19, 10, 5
CONV_BLOCKS = SSD_CONV_DIM // LANES
XATTN_LEAD = 5

P_Z, P_XBC, P_Q, P_K, P_V, P_DT, P_END = 0, 512, 1536, 2048, 2176, 2304, 2432


def _dot(a, b):
    return jnp.dot(a.astype(bf16), b.astype(bf16), preferred_element_type=f32)


def _dot_nt(a, b):
    return lax.dot_general(a.astype(bf16), b.astype(bf16), (((1,), (1,)), ((), ())),
                           preferred_element_type=f32)


def _split2(x):
    hi = x.astype(bf16)
    lo = (x - hi.astype(f32)).astype(bf16)
    return hi, lo


def _dot_x2(x, m):
    hi, lo = _split2(x)
    return (jnp.dot(hi, m, preferred_element_type=f32)
            + jnp.dot(lo, m, preferred_element_type=f32))


def _rmsnorm(x, g):
    ms = jnp.mean(x * x, axis=-1, keepdims=True)
    return x * lax.rsqrt(ms + EPS) * g


def _silu(x):
    return x * jax.nn.sigmoid(x)


def _softplus(x):
    return jnp.maximum(x, 0.0) + jnp.log1p(jnp.exp(-jnp.abs(x)))


def _head_expand_matrix():
    r = lax.broadcasted_iota(jnp.int32, (LANES, SSD_WIDTH), 0)
    c = lax.broadcasted_iota(jnp.int32, (LANES, SSD_WIDTH), 1)
    return (r == (c >> 6)).astype(bf16)


def _gated_group_norm(y, z, g):
    u = y * _silu(z)
    half = SSD_WIDTH // SSD_GROUPS
    parts = []
    for gi in range(SSD_GROUPS):
        ug = u[:, gi * half:(gi + 1) * half]
        parts.append(ug * lax.rsqrt(jnp.mean(ug * ug, axis=-1, keepdims=True) + EPS))
    return jnp.concatenate(parts, axis=-1) * g


def _prompt_mixer_kernel(x_ref, gpre_ref, win_ref, convw_ref, convb_ref, dtb_ref, alog_ref,
                         dskip_ref, gssd_ref, sinks_ref, wout_ref, gpost_ref,
                         y_ref, ssm_ref, conv_ref, wk_ref, wv_ref,
                         statet_sc, xbc_ext_sc, xbc_sc, z_sc, q_sc, k_sc, v_sc, dt_sc,
                         kprev_sc, vprev_sc, mix_sc, *, tile):
    i = pl.program_id(1)
    NC = tile // SSD_CHUNK
    ns = range(NC)
    C = SSD_CHUNK
    PW = 2 * LANES

    @pl.when(i == 0)
    def _init():
        statet_sc[...] = jnp.zeros_like(statet_sc)
        xbc_ext_sc[:, :, 0:SUBLANES, :] = jnp.zeros((STREAMS, CONV_BLOCKS, SUBLANES, LANES), f32)
        kprev_sc[...] = jnp.zeros_like(kprev_sc)
        vprev_sc[...] = jnp.zeros_like(vprev_sc)

    lane = lax.broadcasted_iota(jnp.int32, (1, LANES), 1)
    a_row = -jnp.exp(alog_ref[...])
    expand = _head_expand_matrix()
    expand2 = jnp.concatenate([expand, expand], axis=0)
    row_i = lax.broadcasted_iota(jnp.int32, (C, C), 0)
    col_i = lax.broadcasted_iota(jnp.int32, (C, C), 1)
    lower = col_i <= row_i
    tri = lower.astype(bf16)
    tri3 = jnp.concatenate([tri, tri, tri], axis=1)
    lo_half = lane < ATT_HEAD_DIM
    half = SSD_WIDTH // SSD_GROUPS
    head_of_lane = lax.broadcasted_iota(jnp.int32, (1, half), 1) >> 6
    rows = [slice(n * C, (n + 1) * C) for n in ns]

    def stream(s):
        statet_s, xbc_ext_s, xbc_s, z_s = statet_sc.at[s], xbc_ext_sc.at[s], xbc_sc.at[s], z_sc.at[s]
        q_s, k_s, v_s, dt_s = q_sc.at[s], k_sc.at[s], v_sc.at[s], dt_sc.at[s]
        kprev_s, vprev_s, mix_s = kprev_sc.at[s], vprev_sc.at[s], mix_sc.at[s]

        x = x_ref[s, 0]
        h = _rmsnorm(x, gpre_ref[...]).astype(bf16)
        yield

        def proj(col):
            return jnp.dot(h, win_ref[:, col:col + PW], preferred_element_type=f32)

        def conv_cols(cb):
            cols = slice(cb * LANES, (cb + 1) * LANES)
            acc = convb_ref[:, cols]
            for j in range(SSD_CONV):
                off = SUBLANES - (SSD_CONV - 1) + j
                acc = acc + xbc_ext_s[cb, off:off + tile, :] * convw_ref[j:j + 1, cols]
            xbc_s[cb] = _silu(acc)

        def xbc_piece(pc):
            res = proj(P_XBC + pc * PW)
            for half_pc in range(PW // LANES):
                xbc_ext_s[(PW // LANES) * pc + half_pc, SUBLANES:SUBLANES + tile, :] = (
                    res[:, half_pc * LANES:(half_pc + 1) * LANES])

        xbc_piece(0)
        yield
        xbc_piece(1)
        yield
        conv_cols(0)
        yield
        xbc_piece(2)
        yield
        conv_cols(1)
        yield
        xbc_piece(3)
        yield
        conv_cols(2)
        yield
        z_s[:, 0:PW] = proj(P_Z)
        yield
        conv_cols(3)
        yield
        z_s[:, PW:2 * PW] = proj(P_Z + PW)
        yield
        conv_cols(4)
        yield
        q_s[:, 0:PW] = proj(P_Q)
        yield
        conv_cols(5)
        yield
        q_s[:, PW:2 * PW] = proj(P_Q + PW)
        yield
        conv_cols(6)
        yield
        kv = proj(P_K)
        k_s[...] = kv[:, 0:LANES]
        v_s[...] = kv[:, LANES:PW]
        yield
        conv_cols(7)
        yield
        dt_raw = jnp.dot(h, win_ref[:, P_DT:P_END], preferred_element_type=f32)
        dt_s[...] = jnp.where(lane < SSD_HEADS, _softplus(dt_raw + dtb_ref[...]), 0.0)
        tail = xbc_ext_s[:, tile:tile + SUBLANES, :]
        conv_ref[s, 0] = jnp.concatenate([tail[cb] for cb in range(CONV_BLOCKS)], axis=-1)
        xbc_ext_s[:, 0:SUBLANES, :] = tail
        yield

        GH = SSD_HEADS // SSD_GROUPS
        heads = range(ATT_HEADS)
        gs = range(SSD_GROUPS)
        zero_b = jnp.zeros((), bf16)

        yield
        def xbc_cols(n, lo, hi):
            return jnp.concatenate([xbc_s[cb, rows[n], :] for cb in range(lo // LANES, hi // LANES)],
                                   axis=-1)
        xs = [xbc_cols(n, 0, SSD_WIDTH) for n in ns]
        bm = [xbc_cols(n, SSD_WIDTH, SSD_WIDTH + half) for n in ns]
        cm = [xbc_cols(n, SSD_WIDTH + half, SSD_CONV_DIM) for n in ns]
        dtc = [dt_s[rows[n], :] for n in ns]
        cs = []
        for n in ns:
            adt = dtc[n] * a_row
            a_hi = adt.astype(bf16)
            a_r1 = adt - a_hi.astype(f32)
            a_mid = a_r1.astype(bf16)
            a_lo = (a_r1 - a_mid.astype(f32)).astype(bf16)
            cs.append(jnp.dot(tri3, jnp.concatenate([a_hi, a_mid, a_lo], axis=0),
                              preferred_element_type=f32))

        yield
        q = [q_s[rows[n], :].astype(bf16) for n in ns]
        k = [k_s[rows[n], :] for n in ns]
        v = [v_s[rows[n], :] for n in ns]
        k_prev = [kprev_s[...]] + k[:-1]
        v_prev = [vprev_s[...]] + v[:-1]
        kprev_s[...] = k[-1]
        vprev_s[...] = v[-1]
        first_bias = jnp.where(i > 0, 0.0, -jnp.inf)
        k_lo, k_hi, v_lo, v_hi = [], [], [], []
        for n in ns:
            kk = jnp.concatenate([k_prev[n], k[n]], axis=0)
            vv = jnp.concatenate([v_prev[n], v[n]], axis=0)
            kk_r = pltpu.roll(kk, ATT_HEAD_DIM, 1)
            vv_r = pltpu.roll(vv, ATT_HEAD_DIM, 1)
            k_lo.append([jnp.where(lo_half, kk, 0.0).astype(bf16), jnp.where(lo_half, kk_r, 0.0).astype(bf16)])
            k_hi.append([jnp.where(lo_half, 0.0, kk_r).astype(bf16), jnp.where(lo_half, 0.0, kk).astype(bf16)])
            v_lo.append([jnp.where(lo_half, vv, 0.0).astype(bf16), jnp.where(lo_half, vv_r, 0.0).astype(bf16)])
            v_hi.append([jnp.where(lo_half, 0.0, vv_r).astype(bf16), jnp.where(lo_half, 0.0, vv).astype(bf16)])
        s_g = [[_dot_nt(jnp.concatenate([q[n][:, (2 * g) * LANES:(2 * g + 1) * LANES],
                                         q[n][:, (2 * g + 1) * LANES:(2 * g + 2) * LANES]], axis=0),
                        jnp.concatenate([k_lo[n][g], k_hi[n][g]], axis=0))
                for g in range(ATT_KV_HEADS)] for n in ns]

        yield
        cs_t = [cs[n].T for n in ns]
        fac_e, cb_all, bm_t = [], [], []
        for n in ns:
            tot = cs[n][C - 1:C, :]
            fac = jnp.concatenate([dtc[n], jnp.exp(cs[n]), jnp.exp(tot - cs[n])], axis=0)
            f_hi = fac.astype(bf16)
            f_lo = (fac - f_hi.astype(f32)).astype(bf16)
            fac_e.append(jnp.dot(jnp.concatenate([f_hi, f_lo], axis=1), expand2,
                                 preferred_element_type=f32))
            cb_all.append(_dot_nt(
                jnp.concatenate([cm[n][:, 0:SSD_STATE], cm[n][:, SSD_STATE:half]], axis=0),
                jnp.concatenate([bm[n][:, 0:SSD_STATE], bm[n][:, SSD_STATE:half]], axis=0)))
            bm_t.append(bm[n].T)

        yield
        def head_scores(n, hd):
            g, jb, sub = hd // 4, (hd // 2) % 2, hd % 2
            s_prev = s_g[n][g][jb * C:(jb + 1) * C, (2 * sub) * C:(2 * sub + 1) * C]
            s_cur = s_g[n][g][jb * C:(jb + 1) * C, (2 * sub + 1) * C:(2 * sub + 2) * C]
            return jnp.where(lower, s_cur, s_prev + first_bias if n == 0 else s_prev)
        sc = [[head_scores(n, hd) for hd in heads] for n in ns]
        sink = [sinks_ref[hd] for hd in heads]
        m = [[jnp.maximum(jnp.max(sc[n][hd], axis=-1, keepdims=True), sink[hd]) for hd in heads]
             for n in ns]

        yield
        xdt = [xs[n] * fac_e[n][0:C] for n in ns]
        ecs_e = [fac_e[n][C:2 * C] for n in ns]
        w_end = [xdt[n] * fac_e[n][2 * C:3 * C] for n in ns]
        xdt_b = [xdt[n].astype(bf16) for n in ns]
        contrib = [[_dot(bm_t[n][g * SSD_STATE:(g + 1) * SSD_STATE, :],
                         w_end[n][:, g * half:(g + 1) * half]) for g in gs] for n in ns]
        decay = [[jnp.exp(jnp.where(lower, cs[n][:, hh:hh + 1] - cs_t[n][hh:hh + 1, :], -jnp.inf))
                  for hh in range(SSD_HEADS)] for n in ns]

        yield
        e = [[jnp.exp(sc[n][hd] - m[n][hd]) for hd in heads] for n in ns]
        den = [[jnp.sum(e[n][hd], axis=-1, keepdims=True) + jnp.exp(sink[hd] - m[n][hd])
                for hd in heads] for n in ns]

        yield
        y_d = []
        for n in ns:
            y_n = []
            for g in gs:
                gl = slice(g * half, (g + 1) * half)
                cb = cb_all[n][g * C:(g + 1) * C, g * C:(g + 1) * C]
                m_parts = [(cb * decay[n][g * GH + r]).astype(bf16) for r in range(GH)]
                x_parts = [jnp.where(head_of_lane == r, xdt_b[n][:, gl], zero_b) for r in range(GH)]
                y_n.append(jnp.dot(jnp.concatenate(m_parts, axis=1), jnp.concatenate(x_parts, axis=0),
                                   preferred_element_type=f32))
            y_d.append(y_n)
        st = [statet_s[g] for g in gs]
        y_off = []
        for n in ns:
            y_off.append([_dot(cm[n][:, g * SSD_STATE:(g + 1) * SSD_STATE], st[g]) for g in gs])
            st = [st[g] * ecs_e[n][C - 1:C, g * half:(g + 1) * half] + contrib[n][g] for g in gs]
        for g in gs:
            statet_s[g] = st[g]

        yield
        p = [[(e[n][hd] * (1.0 / den[n][hd])).astype(bf16) for hd in heads] for n in ns]
        o_g = []
        for n in ns:
            o_n = []
            for g in range(ATT_KV_HEADS):
                p_rows = []
                for jb in range(2):
                    p_cols = []
                    for sub in range(2):
                        ph = p[n][g * 4 + jb * 2 + sub]
                        p_cols += [jnp.where(lower, zero_b, ph), jnp.where(lower, ph, zero_b)]
                    p_rows.append(jnp.concatenate(p_cols, axis=1))
                o_n.append(jnp.dot(jnp.concatenate(p_rows, axis=0),
                                   jnp.concatenate([v_lo[n][g], v_hi[n][g]], axis=0),
                                   preferred_element_type=f32))
            o_g.append(o_n)

        yield
        for n in ns:
            y = (jnp.concatenate([y_d[n][g] + y_off[n][g] * ecs_e[n][:, g * half:(g + 1) * half]
                                  for g in gs], axis=-1)
                 + xs[n] * dskip_ref[...])
            y_ssd = _gated_group_norm(y, z_s[rows[n], :], gssd_ref[...])
            mix_s[rows[n], 0:SSD_WIDTH] = y_ssd.astype(bf16)
        for n in ns:
            for g in range(ATT_KV_HEADS):
                for jb in range(2):
                    lo_l = SSD_WIDTH + (2 * g + jb) * LANES
                    mix_s[rows[n], lo_l:lo_l + LANES] = o_g[n][g][jb * C:(jb + 1) * C].astype(bf16)

        yield

        mix_in = mix_s[...]
        mix = []
        for pc in range(D_MODEL // PW):
            mix.append(jnp.dot(mix_in, wout_ref[:, pc * PW:(pc + 1) * PW], preferred_element_type=f32))
            yield
        y_ref[s, 0] = x + _rmsnorm(jnp.concatenate(mix, axis=-1), gpost_ref[...])

    def mixed(ga, na, gb, nb):
        done_b = 0
        for ka in range(na):
            next(ga, None)
            want_b = ((ka + 1) * nb) // na
            for _ in range(want_b - done_b):
                next(gb, None)
            done_b = want_b

    g0, g1 = [stream(s) for s in range(STREAMS)]
    for _ in range(PROJ_PIECES):
        next(g0, None)
    mixed(g0, SCAN_PIECES, g1, PROJ_PIECES)
    mixed(g1, SCAN_PIECES, g0, OUT_PIECES)
    for g in (g0, g1):
        for _ in g:
            pass

    @pl.when(i == pl.num_programs(1) - 1)
    def _final_state():
        for s in range(STREAMS):
            for g in range(SSD_GROUPS):
                ssm_ref[s, 0, g * half:(g + 1) * half, :] = statet_sc[s, g].T
            wk_ref[s, 0] = k_sc[s, tile - WINDOW:tile, :].T
            wv_ref[s, 0] = v_sc[s, tile - WINDOW:tile, :].T


def _const_spec(shape):
    nd = len(shape)
    return pl.BlockSpec(shape, lambda *_: (0,) * nd)


def _prompt_mixer(x, gpre, win, convw, convb, dtb, alog, dskip_e, gssd, sinks, wout, gpost, tile):
    B, L, D = x.shape
    S = STREAMS
    G = B // S
    kern = functools.partial(_prompt_mixer_kernel, tile=tile)

    def per_seq(rows, width):
        return pl.BlockSpec((S, 1, rows, width), lambda b, i: (0, b, 0, 0))

    tile_spec = pl.BlockSpec((S, 1, tile, D), lambda b, i: (0, b, i, 0))
    out_shape = (
        jax.ShapeDtypeStruct((S, G, L, D), f32),
        jax.ShapeDtypeStruct((S, G, SSD_WIDTH, SSD_STATE), f32),
        jax.ShapeDtypeStruct((S, G, SUBLANES, SSD_CONV_DIM), f32),
        jax.ShapeDtypeStruct((S, G, ATT_KV_WIDTH, WINDOW), f32),
        jax.ShapeDtypeStruct((S, G, ATT_KV_WIDTH, WINDOW), f32),
    )
    in_specs = [
        tile_spec,
        _const_spec(gpre.shape), _const_spec(win.shape), _const_spec(convw.shape),
        _const_spec(convb.shape), _const_spec(dtb.shape), _const_spec(alog.shape),
        _const_spec(dskip_e.shape), _const_spec(gssd.shape),
        pl.BlockSpec(memory_space=pltpu.SMEM),
        _const_spec(wout.shape), _const_spec(gpost.shape),
    ]
    out_specs = (
        tile_spec,
        per_seq(SSD_WIDTH, SSD_STATE),
        per_seq(SUBLANES, SSD_CONV_DIM),
        per_seq(ATT_KV_WIDTH, WINDOW),
        per_seq(ATT_KV_WIDTH, WINDOW),
    )
    scratch = [
        pltpu.VMEM((S, SSD_GROUPS, SSD_STATE, SSD_WIDTH // SSD_GROUPS), f32),
        pltpu.VMEM((S, CONV_BLOCKS, tile + 2 * SUBLANES, LANES), f32),
        pltpu.VMEM((S, CONV_BLOCKS, tile, LANES), f32),
        pltpu.VMEM((S, tile, SSD_WIDTH), f32),
        pltpu.VMEM((S, tile, ATT_WIDTH), f32),
        pltpu.VMEM((S, tile, ATT_KV_WIDTH), f32),
        pltpu.VMEM((S, tile, ATT_KV_WIDTH), f32),
        pltpu.VMEM((S, tile, LANES), f32),
        pltpu.VMEM((S, WINDOW, ATT_KV_WIDTH), f32),
        pltpu.VMEM((S, WINDOW, ATT_KV_WIDTH), f32),
        pltpu.VMEM((S, tile, 2 * SSD_WIDTH), bf16),
    ]
    outs = pl.pallas_call(
        kern, grid=(G, L // tile), in_specs=in_specs, out_specs=out_specs, out_shape=out_shape,
        scratch_shapes=scratch, name="prompt_mixer",
        compiler_params=pltpu.CompilerParams(
            dimension_semantics=("arbitrary", "arbitrary"), vmem_limit_bytes=VMEM_LIMIT),
    )(x.reshape(S, G, L, D), gpre, win, convw, convb, dtb, alog, dskip_e, gssd, sinks, wout, gpost)
    return tuple(o.reshape(B, *o.shape[2:]) for o in outs)


def _memkv_kernel(m_ref, g_ref, wk_ref, wv_ref, k_ref, v_ref, kh_ref, vh_ref):
    mn = _rmsnorm(m_ref[...], g_ref[...]).astype(bf16)
    k = jnp.dot(mn, wk_ref[...], preferred_element_type=f32)
    v = jnp.dot(mn, wv_ref[...], preferred_element_type=f32)
    k_ref[...] = k
    v_ref[...] = v
    for hd in range(X_HEADS):
        kh_ref[:, hd, :] = k[:, hd * X_HEAD_DIM:(hd + 1) * X_HEAD_DIM]
        vh_ref[:, hd, :] = v[:, hd * X_HEAD_DIM:(hd + 1) * X_HEAD_DIM]


def _memkv(mem2d, g, wk, wv, tile):
    n, d = mem2d.shape
    row = pl.BlockSpec((tile, d), lambda i: (i, 0))
    hrow = pl.BlockSpec((tile, X_HEADS, X_HEAD_DIM), lambda i: (i, 0, 0))
    flat = jax.ShapeDtypeStruct((n, d), f32)
    heads = jax.ShapeDtypeStruct((n, X_HEADS, X_HEAD_DIM), f32)
    return pl.pallas_call(
        _memkv_kernel, grid=(n // tile,),
        in_specs=[row, _const_spec(g.shape), _const_spec(wk.shape), _const_spec(wv.shape)],
        out_specs=(row, row, hrow, hrow),
        out_shape=(flat, flat, heads, heads),
        name="memory_kv",
        compiler_params=pltpu.CompilerParams(
            dimension_semantics=("arbitrary",), vmem_limit_bytes=VMEM_LIMIT),
    )(mem2d, g, wk, wv)


def _prompt_xattn_kernel(x_ref, gpre_ref, wq_ref, mk_ref, mv_ref, wo_ref, gpost_ref, y_ref):
    hs = range(X_HEADS)
    sl = [slice(hd * X_HEAD_DIM, (hd + 1) * X_HEAD_DIM) for hd in hs]

    def stream(s):
        x = x_ref[s, 0]
        hn = _rmsnorm(x, gpre_ref[...]).astype(bf16)
        yield
        q = []
        for hd in hs:
            q.append(jnp.dot(hn, wq_ref[:, sl[hd]], preferred_element_type=f32))
            yield
        sc = [_dot_nt(q[hd], mk_ref[s, 0, :, sl[hd]]) for hd in hs]
        yield
        m = [jnp.max(sc[hd], axis=-1, keepdims=True) for hd in hs]
        e = [jnp.exp(sc[hd] - m[hd]) for hd in hs]
        yield
        r = [1.0 / jnp.sum(e[hd], axis=-1, keepdims=True) for hd in hs]
        p = [(e[hd] * r[hd]).astype(bf16) for hd in hs]
        yield
        o = jnp.concatenate([_dot(p[hd], mv_ref[s, 0, :, sl[hd]]) for hd in hs], axis=-1).astype(bf16)
        yield
        c = []
        for hd in hs:
            c.append(jnp.dot(o, wo_ref[:, sl[hd]], preferred_element_type=f32))
            yield
        y_ref[s, 0] = x + _rmsnorm(jnp.concatenate(c, axis=-1), gpost_ref[...])

    live = [stream(s) for s in range(STREAMS)]
    for _ in range(XATTN_LEAD):
        next(live[0], None)
    while live:
        live = [g for g in live if next(g, True) is None]


def _prompt_xattn(x, gpre, wq, mk, mv, wo, gpost, tile):
    B, L, D = x.shape
    S = STREAMS
    G = B // S
    xs = pl.BlockSpec((S, 1, tile, D), lambda b, i: (0, b, i, 0))
    ms = pl.BlockSpec((S, 1, N_MEM, D), lambda b, i: (0, b, 0, 0))
    return pl.pallas_call(
        _prompt_xattn_kernel, grid=(G, L // tile),
        in_specs=[xs, _const_spec(gpre.shape), _const_spec(wq.shape), ms, ms,
                  _const_spec(wo.shape), _const_spec(gpost.shape)],
        out_specs=xs, out_shape=jax.ShapeDtypeStruct((S, G, L, D), f32),
        name="prompt_xattn",
        compiler_params=pltpu.CompilerParams(
            dimension_semantics=("arbitrary", "arbitrary"), vmem_limit_bytes=VMEM_LIMIT),
    )(x.reshape(S, G, L, D), gpre, wq, mk.reshape(S, G, N_MEM, D), mv.reshape(S, G, N_MEM, D),
      wo, gpost).reshape(B, L, D)


def _ffn_kernel(x_ref, gpre_ref, wg_ref, wu_ref, wd_ref, gpost_ref, y_ref):
    x = x_ref[...]
    hf = _rmsnorm(x, gpre_ref[...]).astype(bf16)
    gate = jnp.dot(hf, wg_ref[...], preferred_element_type=f32)
    up = jnp.dot(hf, wu_ref[...], preferred_element_type=f32)
    act = (_silu(gate) * up).astype(bf16)
    f = jnp.dot(act, wd_ref[...], preferred_element_type=f32)
    y_ref[...] = x + _rmsnorm(f, gpost_ref[...])


def _ffn(x2d, gpre, wg, wu, wd, gpost, tile):
    n, d = x2d.shape
    row = pl.BlockSpec((tile, d), lambda i: (i, 0))
    return pl.pallas_call(
        _ffn_kernel, grid=(n // tile,),
        in_specs=[row, _const_spec(gpre.shape), _const_spec(wg.shape), _const_spec(wu.shape),
                  _const_spec(wd.shape), _const_spec(gpost.shape)],
        out_specs=row, out_shape=jax.ShapeDtypeStruct((n, d), f32),
        name="ffn",
        compiler_params=pltpu.CompilerParams(
            dimension_semantics=("arbitrary",), vmem_limit_bytes=VMEM_LIMIT),
    )(x2d, gpre, wg, wu, wd, gpost)


def _pad_rows(a, rows):
    if a.shape[0] == rows:
        return a
    return jnp.concatenate([a, jnp.zeros((rows - a.shape[0], a.shape[1]), a.dtype)], axis=0)


def _sample_mixer_kernel(x_ref, cprev_ref, st_ref, ck_ref, cv_ref,
                         gpre_ref, win_ref, convw_ref, convb_ref, dtb_ref, alog_ref,
                         dskip_ref, gssd_ref, sinkcol_ref, wout_ref, gpost_ref,
                         y_ref, ssm_ref, cnew_ref, wk_ref, wv_ref, *, bt, steps):
    R = steps * bt
    half = SSD_WIDTH // SSD_GROUPS
    x = x_ref[...].reshape(R, D_MODEL)
    h = _rmsnorm(x, gpre_ref[...]).astype(bf16)
    z = jnp.dot(h, win_ref[:, P_Z:P_XBC], preferred_element_type=f32)
    u = jnp.dot(h, win_ref[:, P_XBC:P_Q], preferred_element_type=f32)
    q = jnp.dot(h, win_ref[:, P_Q:P_K], preferred_element_type=f32)
    k_new = jnp.dot(h, win_ref[:, P_K:P_V], preferred_element_type=f32)
    v_new = jnp.dot(h, win_ref[:, P_V:P_DT], preferred_element_type=f32)
    dt_raw = jnp.dot(h, win_ref[:, P_DT:P_END], preferred_element_type=f32)
    lane = lax.broadcasted_iota(jnp.int32, (1, LANES), 1)
    dt = jnp.where(lane < SSD_HEADS, _softplus(dt_raw + dtb_ref[...]), 0.0)

    def slab(a, t):
        return a[t * bt:(t + 1) * bt]

    HD = ATT_HEAD_DIM
    GH = ATT_HEADS // ATT_KV_HEADS
    GR = GH * R
    reps = GR // bt
    kvg = range(ATT_KV_HEADS)
    keep = WINDOW - steps

    kn_t = _pad_rows(k_new, LANES).T
    vn_t = _pad_rows(v_new, LANES).T

    ridx = lax.broadcasted_iota(jnp.int32, (GR, 1), 0)
    rb = ridx % bt
    rt = (ridx // bt) % steps
    qg = [jnp.concatenate([q[:, (g * GH + hl) * HD:(g * GH + hl + 1) * HD] for hl in range(GH)],
                          axis=0) for g in kvg]
    qg_b = [qg[g].astype(bf16) for g in kvg]
    s_cb = [[_dot(qg_b[g], ck_ref[b, g]) for b in range(bt)] for g in kvg]

    hist = [cprev_ref[j] for j in range(SSD_CONV - 1)] + [slab(u, t) for t in range(steps)]
    xbc_t = []
    for t in range(steps):
        acc = convb_ref[...]
        for j in range(SSD_CONV):
            acc = acc + hist[t + j] * convw_ref[j:j + 1, :]
        xbc_t.append(_silu(acc))
    for j in range(SSD_CONV - 1):
        cnew_ref[j] = hist[steps + j]
    xbc = jnp.concatenate(xbc_t, axis=0)
    xs = xbc[:, 0:SSD_WIDTH]
    bm = xbc[:, SSD_WIDTH:SSD_WIDTH + half]
    cm = xbc[:, SSD_WIDTH + half:SSD_CONV_DIM]
    a_row = -jnp.exp(alog_ref[...])
    adt = dt * a_row
    cs_t = [slab(adt, 0)]
    for t in range(1, steps):
        cs_t.append(cs_t[-1] + slab(adt, t))
    cs = jnp.concatenate(cs_t, axis=0)
    tot = cs_t[-1]
    tot_rows = jnp.concatenate([tot] * steps, axis=0)
    expand = _head_expand_matrix()
    expand2 = jnp.concatenate([expand, expand], axis=0)
    pairs = [(t, s2) for t in range(steps) for s2 in range(t)]
    fac = jnp.concatenate([dt, jnp.exp(cs), jnp.exp(tot_rows - cs), jnp.exp(tot)]
                          + [jnp.exp(cs_t[t] - cs_t[s2]) for t, s2 in pairs], axis=0)
    f_hi, f_lo = _split2(fac)
    fac_e = jnp.dot(jnp.concatenate([f_hi, f_lo], axis=1), expand2, preferred_element_type=f32)
    gr = lax.broadcasted_iota(jnp.int32, (half, SSD_WIDTH), 0)
    gc = lax.broadcasted_iota(jnp.int32, (half, SSD_WIDTH), 1)
    gsum = ((gr >> 7) == (gc >> 8)).astype(bf16)
    gsum2 = jnp.concatenate([gsum, gsum], axis=0)
    cb_pairs = [(t, s2) for t in range(steps) for s2 in range(t + 1)]
    prod = jnp.concatenate([slab(cm, t) * slab(bm, s2) for t, s2 in cb_pairs], axis=0)
    c_hi, c_lo = _split2(prod)
    cb_e = jnp.dot(jnp.concatenate([c_hi, c_lo], axis=1), gsum2, preferred_element_type=f32)

    sel_r = lax.broadcasted_iota(jnp.int32, (LANES, WINDOW), 0)
    sel_l = lax.broadcasted_iota(jnp.int32, (LANES, WINDOW), 1)
    sel = [((sel_r % bt == b) & (sel_r < R) & (sel_l - keep == sel_r // bt)).astype(bf16)
           for b in range(bt)]
    new_k = [_dot_x2(kn_t, sel[b]) for b in range(bt)]
    new_v = [_dot_x2(vn_t, sel[b]) for b in range(bt)]

    jcol = lax.broadcasted_iota(jnp.int32, (GR, WINDOW), 1)
    in_window = jcol > rt
    s_c, m, s_n = [], [], []
    for g in kvg:
        acc = jnp.zeros((GR, WINDOW), f32)
        for b in range(bt):
            acc = jnp.where(rb == b, s_cb[g][b], acc)
        s_c.append(jnp.where(in_window, acc, -jnp.inf))
    sink = [sinkcol_ref[g * GR:(g + 1) * GR, :] for g in kvg]
    for g in kvg:
        mg = jnp.maximum(jnp.max(s_c[g], axis=-1, keepdims=True), sink[g])
        sn_g = []
        for t2 in range(steps):
            kt = jnp.concatenate([slab(k_new, t2)[:, g * HD:(g + 1) * HD]] * reps, axis=0)
            sn = jnp.where(rt >= t2, jnp.sum(qg[g] * kt, axis=-1, keepdims=True), -jnp.inf)
            sn_g.append(sn)
            mg = jnp.maximum(mg, sn)
        m.append(mg)
        s_n.append(sn_g)
    e_c = [jnp.exp(s_c[g] - m[g]) for g in kvg]
    e_n = [[jnp.exp(sn - m[g]) for sn in s_n[g]] for g in kvg]
    rinv = []
    for g in kvg:
        den = jnp.sum(e_c[g], axis=-1, keepdims=True) + jnp.exp(sink[g] - m[g])
        for en in e_n[g]:
            den = den + en
        rinv.append(1.0 / den)
    p_c = [(e_c[g] * rinv[g]).astype(bf16) for g in kvg]

    xdt = xs * fac_e[0:R]
    ecs_e = fac_e[R:2 * R]
    w_end = xdt * fac_e[2 * R:3 * R]
    dec_e = fac_e[3 * R:3 * R + bt]
    pair_decay = {pr: fac_e[3 * R + (n + 1) * bt:3 * R + (n + 2) * bt] for n, pr in enumerate(pairs)}
    y_t = []
    for t in range(steps):
        acc = None
        for s2 in range(t + 1):
            n = cb_pairs.index((t, s2))
            coef = cb_e[n * bt:(n + 1) * bt]
            if s2 < t:
                coef = coef * pair_decay[(t, s2)]
            term = coef * slab(xdt, s2)
            acc = term if acc is None else acc + term
        y_t.append(acc)
    y_intra = jnp.concatenate(y_t, axis=0)
    b_idx = lax.broadcasted_iota(jnp.int32, (bt, 1, LANES), 0)
    l_idx = lax.broadcasted_iota(jnp.int32, (bt, 1, LANES), 2)
    pair = ((l_idx & (bt - 1)) == b_idx) & (l_idx < R)
    own = (l_idx == b_idx)
    gsl = [slice(g * half, (g + 1) * half) for g in range(SSD_GROUPS)]
    h0 = [st_ref[:, gsl[g], :] for g in range(SSD_GROUPS)]
    zz = [_dot_nt(h0[g].reshape(bt * half, SSD_STATE),
                  _pad_rows(cm[:, g * SSD_STATE:(g + 1) * SSD_STATE], LANES)).reshape(bt, half, LANES)
          for g in range(SSD_GROUPS)]
    wt = [_pad_rows(w_end[:, gsl[g]], LANES).T for g in range(SSD_GROUPS)]
    contrib = [_dot(jnp.where(pair, wt[g][None], 0.0).reshape(bt * half, LANES),
                    _pad_rows(bm[:, g * SSD_STATE:(g + 1) * SSD_STATE], LANES)
                    ).reshape(bt, half, SSD_STATE) for g in range(SSD_GROUPS)]
    dec_t = [_pad_rows(dec_e[:, gsl[g]], LANES).T for g in range(SSD_GROUPS)]

    o = []
    for g in kvg:
        pv = [_dot_nt(p_c[g], cv_ref[b, g]) for b in range(bt)]
        og = jnp.zeros((GR, HD), f32)
        for b in range(bt):
            og = jnp.where(rb == b, pv[b], og)
        for t2 in range(steps):
            vt = jnp.concatenate([slab(v_new, t2)[:, g * HD:(g + 1) * HD]] * reps, axis=0)
            og = og + (e_n[g][t2] * rinv[g]) * vt
        o.append(og)

    lane_w = lax.broadcasted_iota(jnp.int32, (1, WINDOW), 1)
    for b in range(bt):
        for g in kvg:
            gs = slice(g * HD, (g + 1) * HD)
            wk_ref[b, g] = jnp.where(lane_w < keep, pltpu.roll(ck_ref[b, g], keep, 1), new_k[b][gs])
            wv_ref[b, g] = jnp.where(lane_w < keep, pltpu.roll(cv_ref[b, g], keep, 1), new_v[b][gs])

    y_off_parts = []
    for g in range(SSD_GROUPS):
        yt = jnp.sum(jnp.where(pair, zz[g], 0.0), axis=0)
        y_off_parts.append(yt.T[0:R, :])
        dec = jnp.sum(jnp.where(own, dec_t[g][None], 0.0), axis=-1, keepdims=True)
        ssm_ref[:, gsl[g], :] = h0[g] * dec + contrib[g]
    y_off = jnp.concatenate(y_off_parts, axis=-1) * ecs_e
    y = y_intra + y_off + xs * dskip_ref[...]
    y_ssd = _gated_group_norm(y, z, gssd_ref[...])

    mix = jnp.dot(y_ssd.astype(bf16), wout_ref[0:SSD_WIDTH, :], preferred_element_type=f32)
    for g in kvg:
        for hl in range(GH):
            hd = g * GH + hl
            mix = mix + jnp.dot(o[g][hl * R:(hl + 1) * R].astype(bf16),
                                wout_ref[SSD_WIDTH + hd * HD:SSD_WIDTH + (hd + 1) * HD, :],
                                preferred_element_type=f32)
    y_ref[...] = (x + _rmsnorm(mix, gpost_ref[...])).reshape(steps, bt, D_MODEL)


def _sample_mixer(x_tm, cprev_tm, st, ck, cv, gpre, win, convw, convb, dtb, alog, dskip_e, gssd,
                  sinkcol, wout, gpost, bt):
    steps, nb, D = x_tm.shape
    kern = functools.partial(_sample_mixer_kernel, bt=bt, steps=steps)
    tm = lambda w: pl.BlockSpec((steps, bt, w), lambda i: (0, i, 0))
    win_spec = pl.BlockSpec((bt, ATT_KV_HEADS, ATT_HEAD_DIM, WINDOW), lambda i: (i, 0, 0, 0))
    in_specs = [
        tm(D),
        pl.BlockSpec((SSD_CONV - 1, bt, SSD_CONV_DIM), lambda i: (0, i, 0)),
        pl.BlockSpec((bt, SSD_WIDTH, SSD_STATE), lambda i: (i, 0, 0)),
        win_spec, win_spec,
    ] + [_const_spec(a.shape) for a in (gpre, win, convw, convb, dtb, alog, dskip_e, gssd,
                                        sinkcol, wout, gpost)]
    out_specs = (
        tm(D),
        pl.BlockSpec((bt, SSD_WIDTH, SSD_STATE), lambda i: (i, 0, 0)),
        pl.BlockSpec((SSD_CONV - 1, bt, SSD_CONV_DIM), lambda i: (0, i, 0)),
        win_spec, win_spec,
    )
    out_shape = (
        jax.ShapeDtypeStruct((steps, nb, D), f32),
        jax.ShapeDtypeStruct((nb, SSD_WIDTH, SSD_STATE), f32),
        jax.ShapeDtypeStruct((SSD_CONV - 1, nb, SSD_CONV_DIM), f32),
        jax.ShapeDtypeStruct(ck.shape, f32),
        jax.ShapeDtypeStruct(cv.shape, f32),
    )
    return pl.pallas_call(
        kern, grid=(nb // bt,), in_specs=in_specs, out_specs=out_specs, out_shape=out_shape,
        name="sample_mixer",
        compiler_params=pltpu.CompilerParams(
            dimension_semantics=("arbitrary",), vmem_limit_bytes=VMEM_LIMIT),
    )(x_tm, cprev_tm, st, ck, cv, gpre, win, convw, convb, dtb, alog, dskip_e, gssd, sinkcol,
      wout, gpost)


def _sample_xattn_kernel(x_ref, mk_ref, mv_ref, gpre_ref, wq_ref, wo_ref, gpost_ref, y_ref,
                         *, bt, steps):
    R = steps * bt
    nrow = bt * N_MEM * X_HEADS
    x = x_ref[...].reshape(R, D_MODEL)
    hn = _rmsnorm(x, gpre_ref[...]).astype(bf16)
    q = jnp.dot(hn, wq_ref[...], preferred_element_type=f32)
    qs = jnp.concatenate([q[:, hd * X_HEAD_DIM:(hd + 1) * X_HEAD_DIM] for hd in range(X_HEADS)],
                         axis=0)
    kall = mk_ref[...].reshape(nrow, X_HEAD_DIM)
    vall = mv_ref[...].reshape(nrow, X_HEAD_DIM)
    ncol = X_HEADS * R
    seq_rows = N_MEM * X_HEADS
    z = _dot_nt(kall, qs).reshape(bt, seq_rows, ncol)
    v_b = [vall[b * seq_rows:(b + 1) * seq_rows].astype(bf16) for b in range(bt)]
    b_i = lax.broadcasted_iota(jnp.int32, (bt, 1, ncol), 0)
    c_i = lax.broadcasted_iota(jnp.int32, (bt, 1, ncol), 2)
    zc = jnp.sum(jnp.where(c_i % bt == b_i, z, 0.0), axis=0)
    zc = zc.reshape(seq_rows // SUBLANES, SUBLANES, ncol)
    r_h = lax.broadcasted_iota(jnp.int32, (1, SUBLANES, ncol), 1) % X_HEADS
    c_h = lax.broadcasted_iota(jnp.int32, (1, SUBLANES, ncol), 2) // R
    zc = jnp.where(r_h == c_h, zc, -jnp.inf).reshape(seq_rows, ncol)
    m = jnp.max(zc, axis=0, keepdims=True)
    e = jnp.exp(zc - m)
    p = e * (1.0 / jnp.sum(e, axis=0, keepdims=True))
    col_b = lax.broadcasted_iota(jnp.int32, (1, ncol), 1) % bt
    tn = (((0,), (0,)), ((), ()))
    o = None
    for b in range(bt):
        p_b = jnp.where(col_b == b, p, 0.0).astype(bf16)
        o_b = lax.dot_general(p_b, v_b[b], tn, preferred_element_type=f32)
        o = o_b if o is None else o + o_b
    o = jnp.concatenate([o[hd * R:(hd + 1) * R] for hd in range(X_HEADS)], axis=-1)
    cc = _dot(o, wo_ref[...])
    y_ref[...] = (x + _rmsnorm(cc, gpost_ref[...])).reshape(steps, bt, D_MODEL)


def _sample_xattn(x_tm, mk, mv, gpre, wq, wo, gpost, bt):
    steps, nb, D = x_tm.shape
    kern = functools.partial(_sample_xattn_kernel, bt=bt, steps=steps)
    xs = pl.BlockSpec((steps, bt, D), lambda i: (0, i, 0))
    ms = pl.BlockSpec((bt, N_MEM, X_HEADS, X_HEAD_DIM), lambda i: (i, 0, 0, 0))
    return pl.pallas_call(
        kern, grid=(nb // bt,),
        in_specs=[xs, ms, ms, _const_spec(gpre.shape), _const_spec(wq.shape),
                  _const_spec(wo.shape), _const_spec(gpost.shape)],
        out_specs=xs, out_shape=jax.ShapeDtypeStruct((steps, nb, D), f32),
        name="sample_xattn",
        compiler_params=pltpu.CompilerParams(
            dimension_semantics=("arbitrary",), vmem_limit_bytes=VMEM_LIMIT),
    )(x_tm, mk, mv, gpre, wq, wo, gpost)


def _row(v, width=None):
    v = v.reshape(1, -1).astype(f32)
    if width is not None and v.shape[1] < width:
        v = jnp.pad(v, ((0, 0), (0, width - v.shape[1])))
    return v


def kernel(x_prompt, x_sample, state_ssm, state_conv, cache_win_k, cache_win_v, cache_mem_k, cache_mem_v, mem_prompt, g_mix_pre, w_in, conv_w, conv_b, dt_bias, a_log, d_skip, g_ssd_norm, sinks, w_out, g_mix_post, g_x_pre, w_xq, g_mem, w_xk, w_xv, w_xo, g_x_post, g_ffn_pre, w_gate, w_up, w_down, g_ffn_post):
    depth = w_in.shape[0]
    assert depth == 1
    B, L, D = x_prompt.shape
    NB, steps, _ = x_sample.shape
    li = 0

    w = w_in[li]
    wz, wxbc, wdt = w[:, 0:512], w[:, 512:1536], w[:, 1536:1544]
    wq, wk, wv = w[:, 1544:2056], w[:, 2056:2184], w[:, 2184:2312]
    wdt_p = jnp.pad(wdt, ((0, 0), (0, LANES - SSD_HEADS)))
    win_p = jnp.concatenate([wz, wxbc, wq * ATT_SCALE, wk, wv, wdt_p], axis=1).astype(bf16)
    wo_b = w_out[li].astype(bf16)

    gpre, gpost = _row(g_mix_pre[li]), _row(g_mix_post[li])
    convw, convb = conv_w[li].astype(f32), _row(conv_b[li])
    dtb, alog = _row(dt_bias[li], LANES), _row(a_log[li], LANES)
    dskip_e = _row(jnp.repeat(d_skip[li], SSD_HEAD_DIM))
    gssd = _row(g_ssd_norm[li])
    sk = sinks[li].astype(f32)

    mk2d, mv2d, mk4, mv4 = _memkv(mem_prompt.reshape(B * N_MEM, D), _row(g_mem[li]),
                                  w_xk[li].astype(bf16), w_xv[li].astype(bf16), tile=512)
    mk3, mv3 = mk2d.reshape(B, N_MEM, D), mv2d.reshape(B, N_MEM, D)
    x1, p_ssm, p_conv8, p_wk, p_wv = _prompt_mixer(
        x_prompt, gpre, win_p, convw, convb, dtb, alog, dskip_e, gssd, sk, wo_b, gpost, tile=512)
    wxq_b, wxo_b = (w_xq[li] * X_SCALE).astype(bf16), w_xo[li].astype(bf16)
    gxpre, gxpost = _row(g_x_pre[li]), _row(g_x_post[li])
    x2 = _prompt_xattn(x1, gxpre, wxq_b, mk3, mv3, wxo_b, gxpost, tile=512)
    wg_b, wu_b, wd_b = w_gate[li].astype(bf16), w_up[li].astype(bf16), w_down[li].astype(bf16)
    gfpre, gfpost = _row(g_ffn_pre[li]), _row(g_ffn_post[li])
    yp = _ffn(x2.reshape(B * L, D), gfpre, wg_b, wu_b, wd_b, gfpost, tile=512).reshape(B, L, D)

    bt = 8
    x_tm = jnp.transpose(x_sample, (1, 0, 2))
    cprev_tm = jnp.transpose(state_conv[li], (1, 0, 2))
    st = state_ssm[li].reshape(NB, SSD_WIDTH, SSD_STATE)
    ck = jnp.transpose(cache_win_k[li], (0, 2, 3, 1))
    cv = jnp.transpose(cache_win_v[li], (0, 2, 3, 1))
    sinkcol = jnp.repeat(sk, steps * bt).reshape(ATT_HEADS * steps * bt, 1)
    x1s, s_ssm, cnew_tm, s_wk, s_wv = _sample_mixer(
        x_tm, cprev_tm, st, ck, cv, gpre, win_p, convw, convb, dtb, alog, dskip_e, gssd,
        sinkcol, wo_b, gpost, bt=bt)
    cmk = cache_mem_k.reshape(NB, N_MEM, X_HEADS, X_HEAD_DIM)
    cmv = cache_mem_v.reshape(NB, N_MEM, X_HEADS, X_HEAD_DIM)
    x2s = _sample_xattn(x1s, cmk, cmv, gxpre, wxq_b, wxo_b, gxpost, bt=bt)
    ys_tm = _ffn(x2s.reshape(steps * NB, D), gfpre, wg_b, wu_b, wd_b, gfpost, tile=steps * NB)
    ys = jnp.transpose(ys_tm.reshape(steps, NB, D), (1, 0, 2))

    s_conv = jnp.transpose(cnew_tm, (1, 0, 2))
    kv_shape = (ATT_KV_HEADS, ATT_HEAD_DIM)
    return (
        yp, ys,
        p_ssm.reshape(1, B, SSD_HEADS, SSD_HEAD_DIM, SSD_STATE),
        p_conv8[:, SUBLANES - (SSD_CONV - 1):, :][None],
        jnp.transpose(p_wk.reshape(B, *kv_shape, WINDOW), (0, 3, 1, 2))[None],
        jnp.transpose(p_wv.reshape(B, *kv_shape, WINDOW), (0, 3, 1, 2))[None],
        mk4.reshape(1, B, N_MEM, X_HEADS, X_HEAD_DIM), mv4.reshape(1, B, N_MEM, X_HEADS, X_HEAD_DIM),
        s_ssm.reshape(1, NB, SSD_HEADS, SSD_HEAD_DIM, SSD_STATE),
        s_conv[None],
        jnp.transpose(s_wk, (0, 3, 1, 2))[None], jnp.transpose(s_wv, (0, 3, 1, 2))[None],
    )
```

```python
import functools

import jax
import jax.numpy as jnp
from jax import lax
from jax.experimental import pallas as pl
from jax.experimental.pallas import tpu as pltpu

f32 = jnp.float32
bf16 = jnp.bfloat16

D_MODEL = 1024
EPS = 1e-6
N_MEM = 256
SSD_HEADS = 8
SSD_HEAD_DIM = 64
SSD_WIDTH = 512
SSD_GROUPS = 2
SSD_STATE = 128
SSD_CONV = 4
SSD_CHUNK = 128
SSD_CONV_DIM = 1024
ATT_HEADS = 8
ATT_KV_HEADS = 2
ATT_HEAD_DIM = 64
ATT_WIDTH = 512
ATT_KV_WIDTH = 128
WINDOW = 128
ATT_SCALE = ATT_HEAD_DIM ** -0.5
X_HEADS = 4
X_HEAD_DIM = 256
X_SCALE = X_HEAD_DIM ** -0.5
D_FF = 2816
LANES = 128
SUBLANES = 8
VMEM_LIMIT = 56 * 1024 * 1024
STREAMS = 2
PROJ_PIECES, SCAN_PIECES, OUT_PIECES = 19, 10, 5
CONV_BLOCKS = SSD_CONV_DIM // LANES
XATTN_LEAD = 5

P_Z, P_XBC, P_Q, P_K, P_V, P_DT, P_END = 0, 512, 1536, 2048, 2176, 2304, 2432
W_Z, W_DT, W_Q, W_K = 0, 1536, 1544, 2056


def _dot(a, b):
    return jnp.dot(a.astype(bf16), b.astype(bf16), preferred_element_type=f32)


def _dot_nt(a, b):
    return lax.dot_general(a.astype(bf16), b.astype(bf16), (((1,), (1,)), ((), ())),
                           preferred_element_type=f32)


def _split2(x):
    hi = x.astype(bf16)
    lo = (x - hi.astype(f32)).astype(bf16)
    return hi, lo


def _dot_x2(x, m):
    hi, lo = _split2(x)
    return (jnp.dot(hi, m, preferred_element_type=f32)
            + jnp.dot(lo, m, preferred_element_type=f32))


def _rmsnorm(x, g):
    ms = jnp.mean(x * x, axis=-1, keepdims=True)
    return x * lax.rsqrt(ms + EPS) * g


def _silu(x):
    return x * jax.nn.sigmoid(x)


def _softplus(x):
    return jnp.maximum(x, 0.0) + jnp.log1p(jnp.exp(-jnp.abs(x)))


def _head_expand_matrix():
    r = lax.broadcasted_iota(jnp.int32, (LANES, SSD_WIDTH), 0)
    c = lax.broadcasted_iota(jnp.int32, (LANES, SSD_WIDTH), 1)
    return (r == (c >> 6)).astype(bf16)


def _gated_group_norm(y, z, g):
    u = y * _silu(z)
    half = SSD_WIDTH // SSD_GROUPS
    parts = []
    for gi in range(SSD_GROUPS):
        ug = u[:, gi * half:(gi + 1) * half]
        parts.append(ug * lax.rsqrt(jnp.mean(ug * ug, axis=-1, keepdims=True) + EPS))
    return jnp.concatenate(parts, axis=-1) * g


def _prompt_mixer_kernel(x_ref, gpre_ref, win_ref, convw_ref, convb_ref, dtb_ref, alog_ref,
                         dskip_ref, gssd_ref, sinks_ref, wout_ref, gpost_ref,
                         y_ref, ssm_ref, conv_ref, wk_ref, wv_ref,
                         statet_sc, xbc_ext_sc, xbc_sc, z_sc, q_sc, k_sc, v_sc, dt_sc,
                         kprev_sc, vprev_sc, mix_sc, *, tile):
    i = pl.program_id(1)
    NC = tile // SSD_CHUNK
    ns = range(NC)
    C = SSD_CHUNK
    PW = 2 * LANES

    @pl.when(i == 0)
    def _init():
        statet_sc[...] = jnp.zeros_like(statet_sc)
        xbc_ext_sc[:, :, 0:SUBLANES, :] = jnp.zeros((STREAMS, CONV_BLOCKS, SUBLANES, LANES), f32)
        kprev_sc[...] = jnp.zeros_like(kprev_sc)
        vprev_sc[...] = jnp.zeros_like(vprev_sc)

    lane = lax.broadcasted_iota(jnp.int32, (1, LANES), 1)
    a_row = -jnp.exp(alog_ref[...])
    expand = _head_expand_matrix()
    expand2 = jnp.concatenate([expand, expand], axis=0)
    row_i = lax.broadcasted_iota(jnp.int32, (C, C), 0)
    col_i = lax.broadcasted_iota(jnp.int32, (C, C), 1)
    lower = col_i <= row_i
    tri = lower.astype(bf16)
    tri3 = jnp.concatenate([tri, tri, tri], axis=1)
    lo_half = lane < ATT_HEAD_DIM
    half = SSD_WIDTH // SSD_GROUPS
    head_of_lane = lax.broadcasted_iota(jnp.int32, (1, half), 1) >> 6
    rows = [slice(n * C, (n + 1) * C) for n in ns]

    def stream(s):
        statet_s, xbc_ext_s, xbc_s, z_s = statet_sc.at[s], xbc_ext_sc.at[s], xbc_sc.at[s], z_sc.at[s]
        q_s, k_s, v_s, dt_s = q_sc.at[s], k_sc.at[s], v_sc.at[s], dt_sc.at[s]
        kprev_s, vprev_s, mix_s = kprev_sc.at[s], vprev_sc.at[s], mix_sc.at[s]

        x = x_ref[s, 0]
        h = _rmsnorm(x, gpre_ref[...]).astype(bf16)
        yield

        def proj(col):
            return jnp.dot(h, win_ref[:, col:col + PW], preferred_element_type=f32)

        def conv_cols(cb):
            cols = slice(cb * LANES, (cb + 1) * LANES)
            acc = convb_ref[:, cols]
            for j in range(SSD_CONV):
                off = SUBLANES - (SSD_CONV - 1) + j
                acc = acc + xbc_ext_s[cb, off:off + tile, :] * convw_ref[j:j + 1, cols]
            xbc_s[cb] = _silu(acc)

        def xbc_piece(pc):
            res = proj(P_XBC + pc * PW)
            for half_pc in range(PW // LANES):
                xbc_ext_s[(PW // LANES) * pc + half_pc, SUBLANES:SUBLANES + tile, :] = (
                    res[:, half_pc * LANES:(half_pc + 1) * LANES])

        xbc_piece(0)
        yield
        xbc_piece(1)
        yield
        conv_cols(0)
        yield
        xbc_piece(2)
        yield
        conv_cols(1)
        yield
        xbc_piece(3)
        yield
        conv_cols(2)
        yield
        z_s[:, 0:PW] = proj(P_Z)
        yield
        conv_cols(3)
        yield
        z_s[:, PW:2 * PW] = proj(P_Z + PW)
        yield
        conv_cols(4)
        yield
        q_s[:, 0:PW] = proj(P_Q)
        yield
        conv_cols(5)
        yield
        q_s[:, PW:2 * PW] = proj(P_Q + PW)
        yield
        conv_cols(6)
        yield
        kv = proj(P_K)
        k_s[...] = kv[:, 0:LANES]
        v_s[...] = kv[:, LANES:PW]
        yield
        conv_cols(7)
        yield
        dt_raw = jnp.dot(h, win_ref[:, P_DT:P_END], preferred_element_type=f32)
        dt_s[...] = jnp.where(lane < SSD_HEADS, _softplus(dt_raw + dtb_ref[...]), 0.0)
        tail = xbc_ext_s[:, tile:tile + SUBLANES, :]
        conv_ref[s, 0] = jnp.concatenate([tail[cb] for cb in range(CONV_BLOCKS)], axis=-1)
        xbc_ext_s[:, 0:SUBLANES, :] = tail
        yield

        GH = SSD_HEADS // SSD_GROUPS
        heads = range(ATT_HEADS)
        gs = range(SSD_GROUPS)
        zero_b = jnp.zeros((), bf16)

        yield
        def xbc_cols(n, lo, hi):
            return jnp.concatenate([xbc_s[cb, rows[n], :] for cb in range(lo // LANES, hi // LANES)],
                                   axis=-1)
        xs = [xbc_cols(n, 0, SSD_WIDTH) for n in ns]
        bm = [xbc_cols(n, SSD_WIDTH, SSD_WIDTH + half) for n in ns]
        cm = [xbc_cols(n, SSD_WIDTH + half, SSD_CONV_DIM) for n in ns]
        dtc = [dt_s[rows[n], :] for n in ns]
        cs = []
        for n in ns:
            adt = dtc[n] * a_row
            a_hi = adt.astype(bf16)
            a_r1 = adt - a_hi.astype(f32)
            a_mid = a_r1.astype(bf16)
            a_lo = (a_r1 - a_mid.astype(f32)).astype(bf16)
            cs.append(jnp.dot(tri3, jnp.concatenate([a_hi, a_mid, a_lo], axis=0),
                              preferred_element_type=f32))

        yield
        q = [q_s[rows[n], :].astype(bf16) for n in ns]
        k = [k_s[rows[n], :] for n in ns]
        v = [v_s[rows[n], :] for n in ns]
        k_prev = [kprev_s[...]] + k[:-1]
        v_prev = [vprev_s[...]] + v[:-1]
        kprev_s[...] = k[-1]
        vprev_s[...] = v[-1]
        first_bias = jnp.where(i > 0, 0.0, -jnp.inf)
        k_lo, k_hi, v_lo, v_hi = [], [], [], []
        for n in ns:
            kk = jnp.concatenate([k_prev[n], k[n]], axis=0)
            vv = jnp.concatenate([v_prev[n], v[n]], axis=0)
            kk_r = pltpu.roll(kk, ATT_HEAD_DIM, 1)
            vv_r = pltpu.roll(vv, ATT_HEAD_DIM, 1)
            k_lo.append([jnp.where(lo_half, kk, 0.0).astype(bf16), jnp.where(lo_half, kk_r, 0.0).astype(bf16)])
            k_hi.append([jnp.where(lo_half, 0.0, kk_r).astype(bf16), jnp.where(lo_half, 0.0, kk).astype(bf16)])
            v_lo.append([jnp.where(lo_half, vv, 0.0).astype(bf16), jnp.where(lo_half, vv_r, 0.0).astype(bf16)])
            v_hi.append([jnp.where(lo_half, 0.0, vv_r).astype(bf16), jnp.where(lo_half, 0.0, vv).astype(bf16)])
        s_g = [[_dot_nt(jnp.concatenate([q[n][:, (2 * g) * LANES:(2 * g + 1) * LANES],
                                         q[n][:, (2 * g + 1) * LANES:(2 * g + 2) * LANES]], axis=0),
                        jnp.concatenate([k_lo[n][g], k_hi[n][g]], axis=0))
                for g in range(ATT_KV_HEADS)] for n in ns]

        yield
        cs_t = [cs[n].T for n in ns]
        fac_e, cb_all, bm_t = [], [], []
        for n in ns:
            tot = cs[n][C - 1:C, :]
            fac = jnp.concatenate([dtc[n], jnp.exp(cs[n]), jnp.exp(tot - cs[n])], axis=0)
            f_hi = fac.astype(bf16)
            f_lo = (fac - f_hi.astype(f32)).astype(bf16)
            fac_e.append(jnp.dot(jnp.concatenate([f_hi, f_lo], axis=1), expand2,
                                 preferred_element_type=f32))
            cb_all.append(_dot_nt(
                jnp.concatenate([cm[n][:, 0:SSD_STATE], cm[n][:, SSD_STATE:half]], axis=0),
                jnp.concatenate([bm[n][:, 0:SSD_STATE], bm[n][:, SSD_STATE:half]], axis=0)))
            bm_t.append(bm[n].T)

        yield
        def head_scores(n, hd):
            g, jb, sub = hd // 4, (hd // 2) % 2, hd % 2
            s_prev = s_g[n][g][jb * C:(jb + 1) * C, (2 * sub) * C:(2 * sub + 1) * C]
            s_cur = s_g[n][g][jb * C:(jb + 1) * C, (2 * sub + 1) * C:(2 * sub + 2) * C]
            return jnp.where(lower, s_cur, s_prev + first_bias if n == 0 else s_prev)
        sc = [[head_scores(n, hd) for hd in heads] for n in ns]
        sink = [sinks_ref[hd] for hd in heads]
        m = [[jnp.maximum(jnp.max(sc[n][hd], axis=-1, keepdims=True), sink[hd]) for hd in heads]
             for n in ns]

        yield
        xdt = [xs[n] * fac_e[n][0:C] for n in ns]
        ecs_e = [fac_e[n][C:2 * C] for n in ns]
        w_end = [xdt[n] * fac_e[n][2 * C:3 * C] for n in ns]
        xdt_b = [xdt[n].astype(bf16) for n in ns]
        contrib = [[_dot(bm_t[n][g * SSD_STATE:(g + 1) * SSD_STATE, :],
                         w_end[n][:, g * half:(g + 1) * half]) for g in gs] for n in ns]
        decay = [[jnp.exp(jnp.where(lower, cs[n][:, hh:hh + 1] - cs_t[n][hh:hh + 1, :], -jnp.inf))
                  for hh in range(SSD_HEADS)] for n in ns]

        yield
        e = [[jnp.exp(sc[n][hd] - m[n][hd]) for hd in heads] for n in ns]
        den = [[jnp.sum(e[n][hd], axis=-1, keepdims=True) + jnp.exp(sink[hd] - m[n][hd])
                for hd in heads] for n in ns]

        yield
        y_d = []
        for n in ns:
            y_n = []
            for g in gs:
                gl = slice(g * half, (g + 1) * half)
                cb = cb_all[n][g * C:(g + 1) * C, g * C:(g + 1) * C]
                m_parts = [(cb * decay[n][g * GH + r]).astype(bf16) for r in range(GH)]
                x_parts = [jnp.where(head_of_lane == r, xdt_b[n][:, gl], zero_b) for r in range(GH)]
                y_n.append(jnp.dot(jnp.concatenate(m_parts, axis=1), jnp.concatenate(x_parts, axis=0),
                                   preferred_element_type=f32))
            y_d.append(y_n)
        st = [statet_s[g] for g in gs]
        y_off = []
        for n in ns:
            y_off.append([_dot(cm[n][:, g * SSD_STATE:(g + 1) * SSD_STATE], st[g]) for g in gs])
            st = [st[g] * ecs_e[n][C - 1:C, g * half:(g + 1) * half] + contrib[n][g] for g in gs]
        for g in gs:
            statet_s[g] = st[g]

        yield
        p = [[(e[n][hd] * (1.0 / den[n][hd])).astype(bf16) for hd in heads] for n in ns]
        o_g = []
        for n in ns:
            o_n = []
            for g in range(ATT_KV_HEADS):
                p_rows = []
                for jb in range(2):
                    p_cols = []
                    for sub in range(2):
                        ph = p[n][g * 4 + jb * 2 + sub]
                        p_cols += [jnp.where(lower, zero_b, ph), jnp.where(lower, ph, zero_b)]
                    p_rows.append(jnp.concatenate(p_cols, axis=1))
                o_n.append(jnp.dot(jnp.concatenate(p_rows, axis=0),
                                   jnp.concatenate([v_lo[n][g], v_hi[n][g]], axis=0),
                                   preferred_element_type=f32))
            o_g.append(o_n)

        yield
        for n in ns:
            y = (jnp.concatenate([y_d[n][g] + y_off[n][g] * ecs_e[n][:, g * half:(g + 1) * half]
                                  for g in gs], axis=-1)
                 + xs[n] * dskip_ref[...])
            y_ssd = _gated_group_norm(y, z_s[rows[n], :], gssd_ref[...])
            mix_s[rows[n], 0:SSD_WIDTH] = y_ssd.astype(bf16)
        for n in ns:
            for g in range(ATT_KV_HEADS):
                for jb in range(2):
                    lo_l = SSD_WIDTH + (2 * g + jb) * LANES
                    mix_s[rows[n], lo_l:lo_l + LANES] = o_g[n][g][jb * C:(jb + 1) * C].astype(bf16)

        yield

        mix_in = mix_s[...]
        mix = []
        for pc in range(D_MODEL // PW):
            mix.append(jnp.dot(mix_in, wout_ref[:, pc * PW:(pc + 1) * PW], preferred_element_type=f32))
            yield
        y_ref[s, 0] = x + _rmsnorm(jnp.concatenate(mix, axis=-1), gpost_ref[...])

    def mixed(ga, na, gb, nb):
        done_b = 0
        for ka in range(na):
            next(ga, None)
            want_b = ((ka + 1) * nb) // na
            for _ in range(want_b - done_b):
                next(gb, None)
            done_b = want_b

    g0, g1 = [stream(s) for s in range(STREAMS)]
    for _ in range(PROJ_PIECES):
        next(g0, None)
    mixed(g0, SCAN_PIECES, g1, PROJ_PIECES)
    mixed(g1, SCAN_PIECES, g0, OUT_PIECES)
    for g in (g0, g1):
        for _ in g:
            pass

    @pl.when(i == pl.num_programs(1) - 1)
    def _final_state():
        for s in range(STREAMS):
            for g in range(SSD_GROUPS):
                ssm_ref[s, 0, g * half:(g + 1) * half, :] = statet_sc[s, g].T
            wk_ref[s, 0] = k_sc[s, tile - WINDOW:tile, :].T
            wv_ref[s, 0] = v_sc[s, tile - WINDOW:tile, :].T


def _const_spec(shape):
    nd = len(shape)
    return pl.BlockSpec(shape, lambda *_: (0,) * nd)


def _prompt_mixer(x, gpre, win, convw, convb, dtb, alog, dskip_e, gssd, sinks, wout, gpost, tile):
    B, L, D = x.shape
    S = STREAMS
    G = B // S
    kern = functools.partial(_prompt_mixer_kernel, tile=tile)

    def per_seq(rows, width):
        return pl.BlockSpec((S, 1, rows, width), lambda b, i: (0, b, 0, 0))

    tile_spec = pl.BlockSpec((S, 1, tile, D), lambda b, i: (0, b, i, 0))
    out_shape = (
        jax.ShapeDtypeStruct((S, G, L, D), f32),
        jax.ShapeDtypeStruct((S, G, SSD_WIDTH, SSD_STATE), f32),
        jax.ShapeDtypeStruct((S, G, SUBLANES, SSD_CONV_DIM), f32),
        jax.ShapeDtypeStruct((S, G, ATT_KV_WIDTH, WINDOW), f32),
        jax.ShapeDtypeStruct((S, G, ATT_KV_WIDTH, WINDOW), f32),
    )
    in_specs = [
        tile_spec,
        _const_spec(gpre.shape), _const_spec(win.shape), _const_spec(convw.shape),
        _const_spec(convb.shape), _const_spec(dtb.shape), _const_spec(alog.shape),
        _const_spec(dskip_e.shape), _const_spec(gssd.shape),
        pl.BlockSpec(memory_space=pltpu.SMEM),
        _const_spec(wout.shape), _const_spec(gpost.shape),
    ]
    out_specs = (
        tile_spec,
        per_seq(SSD_WIDTH, SSD_STATE),
        per_seq(SUBLANES, SSD_CONV_DIM),
        per_seq(ATT_KV_WIDTH, WINDOW),
        per_seq(ATT_KV_WIDTH, WINDOW),
    )
    scratch = [
        pltpu.VMEM((S, SSD_GROUPS, SSD_STATE, SSD_WIDTH // SSD_GROUPS), f32),
        pltpu.VMEM((S, CONV_BLOCKS, tile + 2 * SUBLANES, LANES), f32),
        pltpu.VMEM((S, CONV_BLOCKS, tile, LANES), f32),
        pltpu.VMEM((S, tile, SSD_WIDTH), f32),
        pltpu.VMEM((S, tile, ATT_WIDTH), f32),
        pltpu.VMEM((S, tile, ATT_KV_WIDTH), f32),
        pltpu.VMEM((S, tile, ATT_KV_WIDTH), f32),
        pltpu.VMEM((S, tile, LANES), f32),
        pltpu.VMEM((S, WINDOW, ATT_KV_WIDTH), f32),
        pltpu.VMEM((S, WINDOW, ATT_KV_WIDTH), f32),
        pltpu.VMEM((S, tile, 2 * SSD_WIDTH), bf16),
    ]
    outs = pl.pallas_call(
        kern, grid=(G, L // tile), in_specs=in_specs, out_specs=out_specs, out_shape=out_shape,
        scratch_shapes=scratch, name="prompt_mixer",
        compiler_params=pltpu.CompilerParams(
            dimension_semantics=("arbitrary", "arbitrary"), vmem_limit_bytes=VMEM_LIMIT),
    )(x.reshape(S, G, L, D), gpre, win, convw, convb, dtb, alog, dskip_e, gssd, sinks, wout, gpost)
    return tuple(o.reshape(B, *o.shape[2:]) for o in outs)


def _memkv_kernel(m_ref, g_ref, wk_ref, wv_ref, k_ref, v_ref, kh_ref, vh_ref):
    mn = _rmsnorm(m_ref[...], g_ref[...]).astype(bf16)
    k = jnp.dot(mn, wk_ref[...], preferred_element_type=f32)
    v = jnp.dot(mn, wv_ref[...], preferred_element_type=f32)
    k_ref[...] = k
    v_ref[...] = v
    for hd in range(X_HEADS):
        kh_ref[:, hd, :] = k[:, hd * X_HEAD_DIM:(hd + 1) * X_HEAD_DIM]
        vh_ref[:, hd, :] = v[:, hd * X_HEAD_DIM:(hd + 1) * X_HEAD_DIM]


def _memkv(mem2d, g, wk, wv, tile):
    n, d = mem2d.shape
    row = pl.BlockSpec((tile, d), lambda i: (i, 0))
    hrow = pl.BlockSpec((tile, X_HEADS, X_HEAD_DIM), lambda i: (i, 0, 0))
    flat = jax.ShapeDtypeStruct((n, d), f32)
    heads = jax.ShapeDtypeStruct((n, X_HEADS, X_HEAD_DIM), f32)
    return pl.pallas_call(
        _memkv_kernel, grid=(n // tile,),
        in_specs=[row, _const_spec(g.shape), _const_spec(wk.shape), _const_spec(wv.shape)],
        out_specs=(row, row, hrow, hrow),
        out_shape=(flat, flat, heads, heads),
        name="memory_kv",
        compiler_params=pltpu.CompilerParams(
            dimension_semantics=("arbitrary",), vmem_limit_bytes=VMEM_LIMIT),
    )(mem2d, g, wk, wv)


def _prompt_xattn_kernel(x_ref, gpre_ref, wq_ref, mk_ref, mv_ref, wo_ref, gpost_ref, y_ref):
    hs = range(X_HEADS)
    sl = [slice(hd * X_HEAD_DIM, (hd + 1) * X_HEAD_DIM) for hd in hs]

    def stream(s):
        x = x_ref[s, 0]
        hn = _rmsnorm(x, gpre_ref[...]).astype(bf16)
        yield
        q = []
        for hd in hs:
            q.append(jnp.dot(hn, wq_ref[:, sl[hd]], preferred_element_type=f32))
            yield
        sc = [_dot_nt(q[hd], mk_ref[s, 0, :, sl[hd]]) for hd in hs]
        yield
        m = [jnp.max(sc[hd], axis=-1, keepdims=True) for hd in hs]
        e = [jnp.exp(sc[hd] - m[hd]) for hd in hs]
        yield
        r = [1.0 / jnp.sum(e[hd], axis=-1, keepdims=True) for hd in hs]
        p = [(e[hd] * r[hd]).astype(bf16) for hd in hs]
        yield
        o = jnp.concatenate([_dot(p[hd], mv_ref[s, 0, :, sl[hd]]) for hd in hs], axis=-1).astype(bf16)
        yield
        c = []
        for hd in hs:
            c.append(jnp.dot(o, wo_ref[:, sl[hd]], preferred_element_type=f32))
            yield
        y_ref[s, 0] = x + _rmsnorm(jnp.concatenate(c, axis=-1), gpost_ref[...])

    live = [stream(s) for s in range(STREAMS)]
    for _ in range(XATTN_LEAD):
        next(live[0], None)
    while live:
        live = [g for g in live if next(g, True) is None]


def _prompt_xattn(x, gpre, wq, mk, mv, wo, gpost, tile):
    B, L, D = x.shape
    S = STREAMS
    G = B // S
    xs = pl.BlockSpec((S, 1, tile, D), lambda b, i: (0, b, i, 0))
    ms = pl.BlockSpec((S, 1, N_MEM, D), lambda b, i: (0, b, 0, 0))
    return pl.pallas_call(
        _prompt_xattn_kernel, grid=(G, L // tile),
        in_specs=[xs, _const_spec(gpre.shape), _const_spec(wq.shape), ms, ms,
                  _const_spec(wo.shape), _const_spec(gpost.shape)],
        out_specs=xs, out_shape=jax.ShapeDtypeStruct((S, G, L, D), f32),
        name="prompt_xattn",
        compiler_params=pltpu.CompilerParams(
            dimension_semantics=("arbitrary", "arbitrary"), vmem_limit_bytes=VMEM_LIMIT),
    )(x.reshape(S, G, L, D), gpre, wq, mk.reshape(S, G, N_MEM, D), mv.reshape(S, G, N_MEM, D),
      wo, gpost).reshape(B, L, D)


def _ffn_kernel(xp_ref, xs_ref, gpre_ref, wg_ref, wu_ref, wd_ref, gpost_ref, yp_ref, ys_ref):
    i = pl.program_id(0)
    last = pl.num_programs(0) - 1
    x = jnp.where(i < last, xp_ref[...], xs_ref[...])
    hf = _rmsnorm(x, gpre_ref[...]).astype(bf16)
    gate = jnp.dot(hf, wg_ref[...], preferred_element_type=f32)
    up = jnp.dot(hf, wu_ref[...], preferred_element_type=f32)
    act = (_silu(gate) * up).astype(bf16)
    f = jnp.dot(act, wd_ref[...], preferred_element_type=f32)
    y = x + _rmsnorm(f, gpost_ref[...])

    @pl.when(i < last)
    def _prompt_rows():
        yp_ref[...] = y

    @pl.when(i == last)
    def _sample_rows():
        ys_ref[...] = y


def _ffn(xp2d, xs2d, gpre, wg, wu, wd, gpost, tile):
    n, d = xp2d.shape
    assert xs2d.shape == (tile, d)
    steps_p = n // tile
    prow = pl.BlockSpec((tile, d), lambda i: (jnp.minimum(i, steps_p - 1), 0))
    srow = pl.BlockSpec((tile, d), lambda i: (0, 0))
    return pl.pallas_call(
        _ffn_kernel, grid=(steps_p + 1,),
        in_specs=[prow, srow, _const_spec(gpre.shape), _const_spec(wg.shape), _const_spec(wu.shape),
                  _const_spec(wd.shape), _const_spec(gpost.shape)],
        out_specs=(prow, srow),
        out_shape=(jax.ShapeDtypeStruct((n, d), f32), jax.ShapeDtypeStruct((tile, d), f32)),
        name="ffn",
        compiler_params=pltpu.CompilerParams(
            dimension_semantics=("arbitrary",), vmem_limit_bytes=VMEM_LIMIT),
    )(xp2d, xs2d, gpre, wg, wu, wd, gpost)


def _pad_rows(a, rows):
    if a.shape[0] == rows:
        return a
    return jnp.concatenate([a, jnp.zeros((rows - a.shape[0], a.shape[1]), a.dtype)], axis=0)


def _sample_mixer_kernel(x_ref, cprev_ref, st_ref, ck_ref, cv_ref,
                         gpre_ref, win_ref, convw_ref, convb_ref, dtb_ref, alog_ref,
                         dskip_ref, gssd_ref, sinkcol_ref, wout_ref, gpost_ref,
                         y_ref, ssm_ref, cnew_ref, wk_ref, wv_ref, *, bt, steps):
    R = steps * bt
    half = SSD_WIDTH // SSD_GROUPS
    x = x_ref[...].reshape(R, D_MODEL)
    h = _rmsnorm(x, gpre_ref[...]).astype(bf16)
    z = jnp.dot(h, win_ref[:, P_Z:P_XBC], preferred_element_type=f32)
    u = jnp.dot(h, win_ref[:, P_XBC:P_Q], preferred_element_type=f32)
    q = jnp.dot(h, win_ref[:, P_Q:P_K], preferred_element_type=f32)
    k_new = jnp.dot(h, win_ref[:, P_K:P_V], preferred_element_type=f32)
    v_new = jnp.dot(h, win_ref[:, P_V:P_DT], preferred_element_type=f32)
    dt_raw = jnp.dot(h, win_ref[:, P_DT:P_END], preferred_element_type=f32)
    lane = lax.broadcasted_iota(jnp.int32, (1, LANES), 1)
    dt = jnp.where(lane < SSD_HEADS, _softplus(dt_raw + dtb_ref[...]), 0.0)

    def slab(a, t):
        return a[t * bt:(t + 1) * bt]

    HD = ATT_HEAD_DIM
    GH = ATT_HEADS // ATT_KV_HEADS
    GR = GH * R
    reps = GR // bt
    kvg = range(ATT_KV_HEADS)
    keep = WINDOW - steps

    kn_t = _pad_rows(k_new, LANES).T
    vn_t = _pad_rows(v_new, LANES).T

    ridx = lax.broadcasted_iota(jnp.int32, (GR, 1), 0)
    rb = ridx % bt
    rt = (ridx // bt) % steps
    qg = [jnp.concatenate([q[:, (g * GH + hl) * HD:(g * GH + hl + 1) * HD] for hl in range(GH)],
                          axis=0) for g in kvg]
    qg_b = [qg[g].astype(bf16) for g in kvg]
    s_cb = [[_dot(qg_b[g], ck_ref[b, g]) for b in range(bt)] for g in kvg]

    hist = [cprev_ref[j] for j in range(SSD_CONV - 1)] + [slab(u, t) for t in range(steps)]
    xbc_t = []
    for t in range(steps):
        acc = convb_ref[...]
        for j in range(SSD_CONV):
            acc = acc + hist[t + j] * convw_ref[j:j + 1, :]
        xbc_t.append(_silu(acc))
    for j in range(SSD_CONV - 1):
        cnew_ref[j] = hist[steps + j]
    xbc = jnp.concatenate(xbc_t, axis=0)
    xs = xbc[:, 0:SSD_WIDTH]
    bm = xbc[:, SSD_WIDTH:SSD_WIDTH + half]
    cm = xbc[:, SSD_WIDTH + half:SSD_CONV_DIM]
    a_row = -jnp.exp(alog_ref[...])
    adt = dt * a_row
    cs_t = [slab(adt, 0)]
    for t in range(1, steps):
        cs_t.append(cs_t[-1] + slab(adt, t))
    cs = jnp.concatenate(cs_t, axis=0)
    tot = cs_t[-1]
    tot_rows = jnp.concatenate([tot] * steps, axis=0)
    expand = _head_expand_matrix()
    expand2 = jnp.concatenate([expand, expand], axis=0)
    pairs = [(t, s2) for t in range(steps) for s2 in range(t)]
    fac = jnp.concatenate([dt, jnp.exp(cs), jnp.exp(tot_rows - cs), jnp.exp(tot)]
                          + [jnp.exp(cs_t[t] - cs_t[s2]) for t, s2 in pairs], axis=0)
    f_hi, f_lo = _split2(fac)
    fac_e = jnp.dot(jnp.concatenate([f_hi, f_lo], axis=1), expand2, preferred_element_type=f32)
    gr = lax.broadcasted_iota(jnp.int32, (half, SSD_WIDTH), 0)
    gc = lax.broadcasted_iota(jnp.int32, (half, SSD_WIDTH), 1)
    gsum = ((gr >> 7) == (gc >> 8)).astype(bf16)
    gsum2 = jnp.concatenate([gsum, gsum], axis=0)
    cb_pairs = [(t, s2) for t in range(steps) for s2 in range(t + 1)]
    prod = jnp.concatenate([slab(cm, t) * slab(bm, s2) for t, s2 in cb_pairs], axis=0)
    c_hi, c_lo = _split2(prod)
    cb_e = jnp.dot(jnp.concatenate([c_hi, c_lo], axis=1), gsum2, preferred_element_type=f32)

    sel_r = lax.broadcasted_iota(jnp.int32, (LANES, WINDOW), 0)
    sel_l = lax.broadcasted_iota(jnp.int32, (LANES, WINDOW), 1)
    sel = [((sel_r % bt == b) & (sel_r < R) & (sel_l - keep == sel_r // bt)).astype(bf16)
           for b in range(bt)]
    new_k = [_dot_x2(kn_t, sel[b]) for b in range(bt)]
    new_v = [_dot_x2(vn_t, sel[b]) for b in range(bt)]

    jcol = lax.broadcasted_iota(jnp.int32, (GR, WINDOW), 1)
    in_window = jcol > rt
    s_c, m, s_n = [], [], []
    for g in kvg:
        acc = jnp.zeros((GR, WINDOW), f32)
        for b in range(bt):
            acc = jnp.where(rb == b, s_cb[g][b], acc)
        s_c.append(jnp.where(in_window, acc, -jnp.inf))
    sink = [sinkcol_ref[g * GR:(g + 1) * GR, :] for g in kvg]
    for g in kvg:
        mg = jnp.maximum(jnp.max(s_c[g], axis=-1, keepdims=True), sink[g])
        sn_g = []
        for t2 in range(steps):
            kt = jnp.concatenate([slab(k_new, t2)[:, g * HD:(g + 1) * HD]] * reps, axis=0)
            sn = jnp.where(rt >= t2, jnp.sum(qg[g] * kt, axis=-1, keepdims=True), -jnp.inf)
            sn_g.append(sn)
            mg = jnp.maximum(mg, sn)
        m.append(mg)
        s_n.append(sn_g)
    e_c = [jnp.exp(s_c[g] - m[g]) for g in kvg]
    e_n = [[jnp.exp(sn - m[g]) for sn in s_n[g]] for g in kvg]
    rinv = []
    for g in kvg:
        den = jnp.sum(e_c[g], axis=-1, keepdims=True) + jnp.exp(sink[g] - m[g])
        for en in e_n[g]:
            den = den + en
        rinv.append(1.0 / den)
    p_c = [(e_c[g] * rinv[g]).astype(bf16) for g in kvg]

    xdt = xs * fac_e[0:R]
    ecs_e = fac_e[R:2 * R]
    w_end = xdt * fac_e[2 * R:3 * R]
    dec_e = fac_e[3 * R:3 * R + bt]
    pair_decay = {pr: fac_e[3 * R + (n + 1) * bt:3 * R + (n + 2) * bt] for n, pr in enumerate(pairs)}
    y_t = []
    for t in range(steps):
        acc = None
        for s2 in range(t + 1):
            n = cb_pairs.index((t, s2))
            coef = cb_e[n * bt:(n + 1) * bt]
            if s2 < t:
                coef = coef * pair_decay[(t, s2)]
            term = coef * slab(xdt, s2)
            acc = term if acc is None else acc + term
        y_t.append(acc)
    y_intra = jnp.concatenate(y_t, axis=0)
    b_idx = lax.broadcasted_iota(jnp.int32, (bt, 1, LANES), 0)
    l_idx = lax.broadcasted_iota(jnp.int32, (bt, 1, LANES), 2)
    pair = ((l_idx & (bt - 1)) == b_idx) & (l_idx < R)
    own = (l_idx == b_idx)
    gsl = [slice(g * half, (g + 1) * half) for g in range(SSD_GROUPS)]
    h0 = [st_ref[:, gsl[g], :] for g in range(SSD_GROUPS)]
    zz = [_dot_nt(h0[g].reshape(bt * half, SSD_STATE),
                  _pad_rows(cm[:, g * SSD_STATE:(g + 1) * SSD_STATE], LANES)).reshape(bt, half, LANES)
          for g in range(SSD_GROUPS)]
    wt = [_pad_rows(w_end[:, gsl[g]], LANES).T for g in range(SSD_GROUPS)]
    contrib = [_dot(jnp.where(pair, wt[g][None], 0.0).reshape(bt * half, LANES),
                    _pad_rows(bm[:, g * SSD_STATE:(g + 1) * SSD_STATE], LANES)
                    ).reshape(bt, half, SSD_STATE) for g in range(SSD_GROUPS)]
    dec_t = [_pad_rows(dec_e[:, gsl[g]], LANES).T for g in range(SSD_GROUPS)]

    o = []
    for g in kvg:
        pv = [_dot_nt(p_c[g], cv_ref[b, g]) for b in range(bt)]
        og = jnp.zeros((GR, HD), f32)
        for b in range(bt):
            og = jnp.where(rb == b, pv[b], og)
        for t2 in range(steps):
            vt = jnp.concatenate([slab(v_new, t2)[:, g * HD:(g + 1) * HD]] * reps, axis=0)
            og = og + (e_n[g][t2] * rinv[g]) * vt
        o.append(og)

    lane_w = lax.broadcasted_iota(jnp.int32, (1, WINDOW), 1)
    for b in range(bt):
        for g in kvg:
            gs = slice(g * HD, (g + 1) * HD)
            wk_ref[b, g] = jnp.where(lane_w < keep, pltpu.roll(ck_ref[b, g], keep, 1), new_k[b][gs])
            wv_ref[b, g] = jnp.where(lane_w < keep, pltpu.roll(cv_ref[b, g], keep, 1), new_v[b][gs])

    y_off_parts = []
    for g in range(SSD_GROUPS):
        yt = jnp.sum(jnp.where(pair, zz[g], 0.0), axis=0)
        y_off_parts.append(yt.T[0:R, :])
        dec = jnp.sum(jnp.where(own, dec_t[g][None], 0.0), axis=-1, keepdims=True)
        ssm_ref[:, gsl[g], :] = h0[g] * dec + contrib[g]
    y_off = jnp.concatenate(y_off_parts, axis=-1) * ecs_e
    y = y_intra + y_off + xs * dskip_ref[...]
    y_ssd = _gated_group_norm(y, z, gssd_ref[...])

    mix = jnp.dot(y_ssd.astype(bf16), wout_ref[0:SSD_WIDTH, :], preferred_element_type=f32)
    for g in kvg:
        for hl in range(GH):
            hd = g * GH + hl
            mix = mix + jnp.dot(o[g][hl * R:(hl + 1) * R].astype(bf16),
                                wout_ref[SSD_WIDTH + hd * HD:SSD_WIDTH + (hd + 1) * HD, :],
                                preferred_element_type=f32)
    y_ref[...] = (x + _rmsnorm(mix, gpost_ref[...])).reshape(steps, bt, D_MODEL)


def _sample_mixer(x_tm, cprev_tm, st, ck, cv, gpre, win, convw, convb, dtb, alog, dskip_e, gssd,
                  sinkcol, wout, gpost, bt):
    steps, nb, D = x_tm.shape
    kern = functools.partial(_sample_mixer_kernel, bt=bt, steps=steps)
    tm = lambda w: pl.BlockSpec((steps, bt, w), lambda i: (0, i, 0))
    win_spec = pl.BlockSpec((bt, ATT_KV_HEADS, ATT_HEAD_DIM, WINDOW), lambda i: (i, 0, 0, 0))
    in_specs = [
        tm(D),
        pl.BlockSpec((SSD_CONV - 1, bt, SSD_CONV_DIM), lambda i: (0, i, 0)),
        pl.BlockSpec((bt, SSD_WIDTH, SSD_STATE), lambda i: (i, 0, 0)),
        win_spec, win_spec,
    ] + [_const_spec(a.shape) for a in (gpre, win, convw, convb, dtb, alog, dskip_e, gssd,
                                        sinkcol, wout, gpost)]
    out_specs = (
        tm(D),
        pl.BlockSpec((bt, SSD_WIDTH, SSD_STATE), lambda i: (i, 0, 0)),
        pl.BlockSpec((SSD_CONV - 1, bt, SSD_CONV_DIM), lambda i: (0, i, 0)),
        win_spec, win_spec,
    )
    out_shape = (
        jax.ShapeDtypeStruct((steps, nb, D), f32),
        jax.ShapeDtypeStruct((nb, SSD_WIDTH, SSD_STATE), f32),
        jax.ShapeDtypeStruct((SSD_CONV - 1, nb, SSD_CONV_DIM), f32),
        jax.ShapeDtypeStruct(ck.shape, f32),
        jax.ShapeDtypeStruct(cv.shape, f32),
    )
    return pl.pallas_call(
        kern, grid=(nb // bt,), in_specs=in_specs, out_specs=out_specs, out_shape=out_shape,
        name="sample_mixer",
        compiler_params=pltpu.CompilerParams(
            dimension_semantics=("arbitrary",), vmem_limit_bytes=VMEM_LIMIT),
    )(x_tm, cprev_tm, st, ck, cv, gpre, win, convw, convb, dtb, alog, dskip_e, gssd, sinkcol,
      wout, gpost)


def _sample_xattn_kernel(x_ref, mk_ref, mv_ref, gpre_ref, wq_ref, wo_ref, gpost_ref, y_ref,
                         *, bt, steps):
    R = steps * bt
    nrow = bt * N_MEM * X_HEADS
    x = x_ref[...].reshape(R, D_MODEL)
    hn = _rmsnorm(x, gpre_ref[...]).astype(bf16)
    q = jnp.dot(hn, wq_ref[...], preferred_element_type=f32)
    qs = jnp.concatenate([q[:, hd * X_HEAD_DIM:(hd + 1) * X_HEAD_DIM] for hd in range(X_HEADS)],
                         axis=0)
    kall = mk_ref[...].reshape(nrow, X_HEAD_DIM)
    vall = mv_ref[...].reshape(nrow, X_HEAD_DIM)
    ncol = X_HEADS * R
    seq_rows = N_MEM * X_HEADS
    z = _dot_nt(kall, qs).reshape(bt, seq_rows, ncol)
    v_b = [vall[b * seq_rows:(b + 1) * seq_rows].astype(bf16) for b in range(bt)]
    b_i = lax.broadcasted_iota(jnp.int32, (bt, 1, ncol), 0)
    c_i = lax.broadcasted_iota(jnp.int32, (bt, 1, ncol), 2)
    zc = jnp.sum(jnp.where(c_i % bt == b_i, z, 0.0), axis=0)
    zc = zc.reshape(seq_rows // SUBLANES, SUBLANES, ncol)
    r_h = lax.broadcasted_iota(jnp.int32, (1, SUBLANES, ncol), 1) % X_HEADS
    c_h = lax.broadcasted_iota(jnp.int32, (1, SUBLANES, ncol), 2) // R
    zc = jnp.where(r_h == c_h, zc, -jnp.inf).reshape(seq_rows, ncol)
    m = jnp.max(zc, axis=0, keepdims=True)
    e = jnp.exp(zc - m)
    p = e * (1.0 / jnp.sum(e, axis=0, keepdims=True))
    col_b = lax.broadcasted_iota(jnp.int32, (1, ncol), 1) % bt
    tn = (((0,), (0,)), ((), ()))
    o = None
    for b in range(bt):
        p_b = jnp.where(col_b == b, p, 0.0).astype(bf16)
        o_b = lax.dot_general(p_b, v_b[b], tn, preferred_element_type=f32)
        o = o_b if o is None else o + o_b
    o = jnp.concatenate([o[hd * R:(hd + 1) * R] for hd in range(X_HEADS)], axis=-1)
    cc = _dot(o, wo_ref[...])
    y_ref[...] = (x + _rmsnorm(cc, gpost_ref[...])).reshape(steps, bt, D_MODEL)


def _sample_xattn(x_tm, mk, mv, gpre, wq, wo, gpost, bt):
    steps, nb, D = x_tm.shape
    kern = functools.partial(_sample_xattn_kernel, bt=bt, steps=steps)
    xs = pl.BlockSpec((steps, bt, D), lambda i: (0, i, 0))
    ms = pl.BlockSpec((bt, N_MEM, X_HEADS, X_HEAD_DIM), lambda i: (i, 0, 0, 0))
    return pl.pallas_call(
        kern, grid=(nb // bt,),
        in_specs=[xs, ms, ms, _const_spec(gpre.shape), _const_spec(wq.shape),
                  _const_spec(wo.shape), _const_spec(gpost.shape)],
        out_specs=xs, out_shape=jax.ShapeDtypeStruct((steps, nb, D), f32),
        name="sample_xattn",
        compiler_params=pltpu.CompilerParams(
            dimension_semantics=("arbitrary",), vmem_limit_bytes=VMEM_LIMIT),
    )(x_tm, mk, mv, gpre, wq, wo, gpost)


def _win_prep_kernel(wt_ref, o_ref):
    piece = 2 * LANES

    def put(src_lo, n, dst_lo, scale=None):
        for c in range(0, n, piece):
            t = wt_ref[src_lo + c:src_lo + c + piece, :].T
            if scale is not None:
                t = t * scale
            o_ref[:, dst_lo + c:dst_lo + c + piece] = t.astype(bf16)

    put(W_Z, P_Q - P_Z, P_Z)
    put(W_Q, P_K - P_Q, P_Q, ATT_SCALE)
    put(W_K, P_DT - P_K, P_K)
    dt_rows = jnp.concatenate([wt_ref[W_DT:W_DT + SSD_HEADS, :],
                               jnp.zeros((LANES - SSD_HEADS, D_MODEL), f32)], axis=0)
    o_ref[:, P_DT:P_END] = dt_rows.T.astype(bf16)


def _win_prep(w_t):
    return pl.pallas_call(
        _win_prep_kernel, out_shape=jax.ShapeDtypeStruct((D_MODEL, P_END), bf16),
        name="win_prep",
        compiler_params=pltpu.CompilerParams(vmem_limit_bytes=VMEM_LIMIT),
    )(w_t)


def _row(v, width=None):
    v = v.reshape(1, -1).astype(f32)
    if width is not None and v.shape[1] < width:
        v = jnp.pad(v, ((0, 0), (0, width - v.shape[1])))
    return v


def kernel(x_prompt, x_sample, state_ssm, state_conv, cache_win_k, cache_win_v, cache_mem_k, cache_mem_v, mem_prompt, g_mix_pre, w_in, conv_w, conv_b, dt_bias, a_log, d_skip, g_ssd_norm, sinks, w_out, g_mix_post, g_x_pre, w_xq, g_mem, w_xk, w_xv, w_xo, g_x_post, g_ffn_pre, w_gate, w_up, w_down, g_ffn_post):
    depth = w_in.shape[0]
    assert depth == 1
    B, L, D = x_prompt.shape
    NB, steps, _ = x_sample.shape
    li = 0

    win_p = _win_prep(jnp.transpose(w_in[li]))
    wo_b = w_out[li].astype(bf16)

    gpre, gpost = _row(g_mix_pre[li]), _row(g_mix_post[li])
    convw, convb = conv_w[li].astype(f32), _row(conv_b[li])
    dtb, alog = _row(dt_bias[li], LANES), _row(a_log[li], LANES)
    dskip_e = _row(jnp.repeat(d_skip[li], SSD_HEAD_DIM))
    gssd = _row(g_ssd_norm[li])
    sk = sinks[li].astype(f32)

    mk2d, mv2d, mk4, mv4 = _memkv(mem_prompt.reshape(B * N_MEM, D), _row(g_mem[li]),
                                  w_xk[li].astype(bf16), w_xv[li].astype(bf16), tile=512)
    mk3, mv3 = mk2d.reshape(B, N_MEM, D), mv2d.reshape(B, N_MEM, D)
    x1, p_ssm, p_conv8, p_wk, p_wv = _prompt_mixer(
        x_prompt, gpre, win_p, convw, convb, dtb, alog, dskip_e, gssd, sk, wo_b, gpost, tile=512)
    wxq_b, wxo_b = (w_xq[li] * X_SCALE).astype(bf16), w_xo[li].astype(bf16)
    gxpre, gxpost = _row(g_x_pre[li]), _row(g_x_post[li])
    x2 = _prompt_xattn(x1, gxpre, wxq_b, mk3, mv3, wxo_b, gxpost, tile=512)
    wg_b, wu_b, wd_b = w_gate[li].astype(bf16), w_up[li].astype(bf16), w_down[li].astype(bf16)
    gfpre, gfpost = _row(g_ffn_pre[li]), _row(g_ffn_post[li])

    bt = 8
    x_tm = jnp.transpose(x_sample, (1, 0, 2))
    cprev_tm = jnp.transpose(state_conv[li], (1, 0, 2))
    st = state_ssm[li].reshape(NB, SSD_WIDTH, SSD_STATE)
    ck = jnp.transpose(cache_win_k[li], (0, 2, 3, 1))
    cv = jnp.transpose(cache_win_v[li], (0, 2, 3, 1))
    sinkcol = jnp.repeat(sk, steps * bt).reshape(ATT_HEADS * steps * bt, 1)
    x1s, s_ssm, cnew_tm, s_wk, s_wv = _sample_mixer(
        x_tm, cprev_tm, st, ck, cv, gpre, win_p, convw, convb, dtb, alog, dskip_e, gssd,
        sinkcol, wo_b, gpost, bt=bt)
    cmk = cache_mem_k.reshape(NB, N_MEM, X_HEADS, X_HEAD_DIM)
    cmv = cache_mem_v.reshape(NB, N_MEM, X_HEADS, X_HEAD_DIM)
    x2s = _sample_xattn(x1s, cmk, cmv, gxpre, wxq_b, wxo_b, gxpost, bt=bt)
    yp2d, ys_tm = _ffn(x2.reshape(B * L, D), x2s.reshape(steps * NB, D), gfpre, wg_b, wu_b, wd_b,
                       gfpost, tile=steps * NB)
    yp = yp2d.reshape(B, L, D)
    ys = jnp.transpose(ys_tm.reshape(steps, NB, D), (1, 0, 2))

    s_conv = jnp.transpose(cnew_tm, (1, 0, 2))
    kv_shape = (ATT_KV_HEADS, ATT_HEAD_DIM)
    return (
        yp, ys,
        p_ssm.reshape(1, B, SSD_HEADS, SSD_HEAD_DIM, SSD_STATE),
        p_conv8[:, SUBLANES - (SSD_CONV - 1):, :][None],
        jnp.transpose(p_wk.reshape(B, *kv_shape, WINDOW), (0, 3, 1, 2))[None],
        jnp.transpose(p_wv.reshape(B, *kv_shape, WINDOW), (0, 3, 1, 2))[None],
        mk4.reshape(1, B, N_MEM, X_HEADS, X_HEAD_DIM), mv4.reshape(1, B, N_MEM, X_HEADS, X_HEAD_DIM),
        s_ssm.reshape(1, NB, SSD_HEADS, SSD_HEAD_DIM, SSD_STATE),
        s_conv[None],
        jnp.transpose(s_wk, (0, 3, 1, 2))[None], jnp.transpose(s_wv, (0, 3, 1, 2))[None],
    )
```

```python
import functools

import jax
import jax.numpy as jnp
from jax import lax
from jax.experimental import pallas as pl
from jax.experimental.pallas import tpu as pltpu

f32 = jnp.float32
bf16 = jnp.bfloat16

D_MODEL = 1024
EPS = 1e-6
N_MEM = 256
SSD_HEADS = 8
SSD_HEAD_DIM = 64
SSD_WIDTH = 512
SSD_GROUPS = 2
SSD_STATE = 128
SSD_CONV = 4
SSD_CHUNK = 128
SSD_CONV_DIM = 1024
ATT_HEADS = 8
ATT_KV_HEADS = 2
ATT_HEAD_DIM = 64
ATT_WIDTH = 512
ATT_KV_WIDTH = 128
WINDOW = 128
ATT_SCALE = ATT_HEAD_DIM ** -0.5
X_HEADS = 4
X_HEAD_DIM = 256
X_SCALE = X_HEAD_DIM ** -0.5
D_FF = 2816
LANES = 128
SUBLANES = 8
VMEM_LIMIT = 56 * 1024 * 1024
STREAMS = 2
PROJ_PIECES, SCAN_PIECES, OUT_PIECES = 19, 10, 5
CONV_BLOCKS = SSD_CONV_DIM // LANES
XATTN_LEAD = 5

P_Z, P_XBC, P_Q, P_K, P_V, P_DT, P_END = 0, 512, 1536, 2048, 2176, 2304, 2432
W_Z, W_DT, W_Q, W_K = 0, 1536, 1544, 2056


def _dot(a, b):
    return jnp.dot(a.astype(bf16), b.astype(bf16), preferred_element_type=f32)


def _dot_nt(a, b):
    return lax.dot_general(a.astype(bf16), b.astype(bf16), (((1,), (1,)), ((), ())),
                           preferred_element_type=f32)


def _split2(x):
    hi = x.astype(bf16)
    lo = (x - hi.astype(f32)).astype(bf16)
    return hi, lo


def _dot_x2(x, m):
    hi, lo = _split2(x)
    return (jnp.dot(hi, m, preferred_element_type=f32)
            + jnp.dot(lo, m, preferred_element_type=f32))


def _rmsnorm(x, g):
    ms = jnp.mean(x * x, axis=-1, keepdims=True)
    return x * lax.rsqrt(ms + EPS) * g


def _silu(x):
    return x * jax.nn.sigmoid(x)


def _softplus(x):
    return jnp.maximum(x, 0.0) + jnp.log1p(jnp.exp(-jnp.abs(x)))


def _head_expand_matrix():
    r = lax.broadcasted_iota(jnp.int32, (LANES, SSD_WIDTH), 0)
    c = lax.broadcasted_iota(jnp.int32, (LANES, SSD_WIDTH), 1)
    return (r == (c >> 6)).astype(bf16)


def _gated_group_norm(y, z, g):
    u = y * _silu(z)
    half = SSD_WIDTH // SSD_GROUPS
    parts = []
    for gi in range(SSD_GROUPS):
        ug = u[:, gi * half:(gi + 1) * half]
        parts.append(ug * lax.rsqrt(jnp.mean(ug * ug, axis=-1, keepdims=True) + EPS))
    return jnp.concatenate(parts, axis=-1) * g


def _prompt_mixer_kernel(x_ref, gpre_ref, win_ref, convw_ref, convb_ref, dtb_ref, alog_ref,
                         dskip_ref, gssd_ref, sinks_ref, wout_ref, gpost_ref,
                         y_ref, ssm_ref, conv_ref, wk_ref, wv_ref,
                         statet_sc, xbc_ext_sc, xbc_sc, z_sc, q_sc, k_sc, v_sc, dt_sc,
                         kprev_sc, vprev_sc, mix_sc, *, tile):
    i = pl.program_id(1)
    NC = tile // SSD_CHUNK
    ns = range(NC)
    C = SSD_CHUNK
    PW = 2 * LANES

    @pl.when(i == 0)
    def _init():
        statet_sc[...] = jnp.zeros_like(statet_sc)
        xbc_ext_sc[:, :, 0:SUBLANES, :] = jnp.zeros((STREAMS, CONV_BLOCKS, SUBLANES, LANES), f32)
        kprev_sc[...] = jnp.zeros_like(kprev_sc)
        vprev_sc[...] = jnp.zeros_like(vprev_sc)

    lane = lax.broadcasted_iota(jnp.int32, (1, LANES), 1)
    a_row = -jnp.exp(alog_ref[...])
    expand = _head_expand_matrix()
    expand2 = jnp.concatenate([expand, expand], axis=0)
    row_i = lax.broadcasted_iota(jnp.int32, (C, C), 0)
    col_i = lax.broadcasted_iota(jnp.int32, (C, C), 1)
    lower = col_i <= row_i
    tri = lower.astype(bf16)
    tri3 = jnp.concatenate([tri, tri, tri], axis=1)
    lo_half = lane < ATT_HEAD_DIM
    half = SSD_WIDTH // SSD_GROUPS
    head_of_lane = lax.broadcasted_iota(jnp.int32, (1, half), 1) >> 6
    rows = [slice(n * C, (n + 1) * C) for n in ns]

    def stream(s):
        statet_s, xbc_ext_s, xbc_s, z_s = statet_sc.at[s], xbc_ext_sc.at[s], xbc_sc.at[s], z_sc.at[s]
        q_s, k_s, v_s, dt_s = q_sc.at[s], k_sc.at[s], v_sc.at[s], dt_sc.at[s]
        kprev_s, vprev_s, mix_s = kprev_sc.at[s], vprev_sc.at[s], mix_sc.at[s]

        x = x_ref[s, 0]
        h = _rmsnorm(x, gpre_ref[...]).astype(bf16)
        yield

        def proj(col):
            return jnp.dot(h, win_ref[:, col:col + PW], preferred_element_type=f32)

        def conv_cols(cb):
            cols = slice(cb * LANES, (cb + 1) * LANES)
            acc = convb_ref[:, cols]
            for j in range(SSD_CONV):
                off = SUBLANES - (SSD_CONV - 1) + j
                acc = acc + xbc_ext_s[cb, off:off + tile, :] * convw_ref[j:j + 1, cols]
            xbc_s[cb] = _silu(acc)

        def xbc_piece(pc):
            res = proj(P_XBC + pc * PW)
            for half_pc in range(PW // LANES):
                xbc_ext_s[(PW // LANES) * pc + half_pc, SUBLANES:SUBLANES + tile, :] = (
                    res[:, half_pc * LANES:(half_pc + 1) * LANES])

        xbc_piece(0)
        yield
        xbc_piece(1)
        yield
        conv_cols(0)
        yield
        xbc_piece(2)
        yield
        conv_cols(1)
        yield
        xbc_piece(3)
        yield
        conv_cols(2)
        yield
        z_s[:, 0:PW] = proj(P_Z)
        yield
        conv_cols(3)
        yield
        z_s[:, PW:2 * PW] = proj(P_Z + PW)
        yield
        conv_cols(4)
        yield
        q_s[:, 0:PW] = proj(P_Q)
        yield
        conv_cols(5)
        yield
        q_s[:, PW:2 * PW] = proj(P_Q + PW)
        yield
        conv_cols(6)
        yield
        kv = proj(P_K)
        k_s[...] = kv[:, 0:LANES]
        v_s[...] = kv[:, LANES:PW]
        yield
        conv_cols(7)
        yield
        dt_raw = jnp.dot(h, win_ref[:, P_DT:P_END], preferred_element_type=f32)
        dt_s[...] = jnp.where(lane < SSD_HEADS, _softplus(dt_raw + dtb_ref[...]), 0.0)
        tail = xbc_ext_s[:, tile:tile + SUBLANES, :]
        conv_ref[s, 0] = jnp.concatenate([tail[cb] for cb in range(CONV_BLOCKS)], axis=-1)
        xbc_ext_s[:, 0:SUBLANES, :] = tail
        yield

        GH = SSD_HEADS // SSD_GROUPS
        heads = range(ATT_HEADS)
        gs = range(SSD_GROUPS)
        zero_b = jnp.zeros((), bf16)

        yield
        def xbc_cols(n, lo, hi):
            return jnp.concatenate([xbc_s[cb, rows[n], :] for cb in range(lo // LANES, hi // LANES)],
                                   axis=-1)
        xs = [xbc_cols(n, 0, SSD_WIDTH) for n in ns]
        bm = [xbc_cols(n, SSD_WIDTH, SSD_WIDTH + half) for n in ns]
        cm = [xbc_cols(n, SSD_WIDTH + half, SSD_CONV_DIM) for n in ns]
        dtc = [dt_s[rows[n], :] for n in ns]
        cs = []
        for n in ns:
            adt = dtc[n] * a_row
            a_hi = adt.astype(bf16)
            a_r1 = adt - a_hi.astype(f32)
            a_mid = a_r1.astype(bf16)
            a_lo = (a_r1 - a_mid.astype(f32)).astype(bf16)
            cs.append(jnp.dot(tri3, jnp.concatenate([a_hi, a_mid, a_lo], axis=0),
                              preferred_element_type=f32))

        yield
        q = [q_s[rows[n], :].astype(bf16) for n in ns]
        k = [k_s[rows[n], :] for n in ns]
        v = [v_s[rows[n], :] for n in ns]
        k_prev = [kprev_s[...]] + k[:-1]
        v_prev = [vprev_s[...]] + v[:-1]
        kprev_s[...] = k[-1]
        vprev_s[...] = v[-1]
        first_bias = jnp.where(i > 0, 0.0, -jnp.inf)
        k_lo, k_hi, v_lo, v_hi = [], [], [], []
        for n in ns:
            kk = jnp.concatenate([k_prev[n], k[n]], axis=0)
            vv = jnp.concatenate([v_prev[n], v[n]], axis=0)
            kk_r = pltpu.roll(kk, ATT_HEAD_DIM, 1)
            vv_r = pltpu.roll(vv, ATT_HEAD_DIM, 1)
            k_lo.append([jnp.where(lo_half, kk, 0.0).astype(bf16), jnp.where(lo_half, kk_r, 0.0).astype(bf16)])
            k_hi.append([jnp.where(lo_half, 0.0, kk_r).astype(bf16), jnp.where(lo_half, 0.0, kk).astype(bf16)])
            v_lo.append([jnp.where(lo_half, vv, 0.0).astype(bf16), jnp.where(lo_half, vv_r, 0.0).astype(bf16)])
            v_hi.append([jnp.where(lo_half, 0.0, vv_r).astype(bf16), jnp.where(lo_half, 0.0, vv).astype(bf16)])
        s_g = [[_dot_nt(jnp.concatenate([q[n][:, (2 * g) * LANES:(2 * g + 1) * LANES],
                                         q[n][:, (2 * g + 1) * LANES:(2 * g + 2) * LANES]], axis=0),
                        jnp.concatenate([k_lo[n][g], k_hi[n][g]], axis=0))
                for g in range(ATT_KV_HEADS)] for n in ns]

        yield
        cs_t = [cs[n].T for n in ns]
        fac_e, cb_all, bm_t = [], [], []
        for n in ns:
            tot = cs[n][C - 1:C, :]
            fac = jnp.concatenate([dtc[n], jnp.exp(cs[n]), jnp.exp(tot - cs[n])], axis=0)
            f_hi = fac.astype(bf16)
            f_lo = (fac - f_hi.astype(f32)).astype(bf16)
            fac_e.append(jnp.dot(jnp.concatenate([f_hi, f_lo], axis=1), expand2,
                                 preferred_element_type=f32))
            cb_all.append(_dot_nt(
                jnp.concatenate([cm[n][:, 0:SSD_STATE], cm[n][:, SSD_STATE:half]], axis=0),
                jnp.concatenate([bm[n][:, 0:SSD_STATE], bm[n][:, SSD_STATE:half]], axis=0)))
            bm_t.append(bm[n].T)

        yield
        def head_scores(n, hd):
            g, jb, sub = hd // 4, (hd // 2) % 2, hd % 2
            s_prev = s_g[n][g][jb * C:(jb + 1) * C, (2 * sub) * C:(2 * sub + 1) * C]
            s_cur = s_g[n][g][jb * C:(jb + 1) * C, (2 * sub + 1) * C:(2 * sub + 2) * C]
            return jnp.where(lower, s_cur, s_prev + first_bias if n == 0 else s_prev)
        sc = [[head_scores(n, hd) for hd in heads] for n in ns]
        sink = [sinks_ref[hd] for hd in heads]
        m = [[jnp.maximum(jnp.max(sc[n][hd], axis=-1, keepdims=True), sink[hd]) for hd in heads]
             for n in ns]

        yield
        xdt = [xs[n] * fac_e[n][0:C] for n in ns]
        ecs_e = [fac_e[n][C:2 * C] for n in ns]
        w_end = [xdt[n] * fac_e[n][2 * C:3 * C] for n in ns]
        xdt_b = [xdt[n].astype(bf16) for n in ns]
        contrib = [[_dot(bm_t[n][g * SSD_STATE:(g + 1) * SSD_STATE, :],
                         w_end[n][:, g * half:(g + 1) * half]) for g in gs] for n in ns]
        decay = [[jnp.exp(jnp.where(lower, cs[n][:, hh:hh + 1] - cs_t[n][hh:hh + 1, :], -jnp.inf))
                  for hh in range(SSD_HEADS)] for n in ns]

        yield
        e = [[jnp.exp(sc[n][hd] - m[n][hd]) for hd in heads] for n in ns]
        den = [[jnp.sum(e[n][hd], axis=-1, keepdims=True) + jnp.exp(sink[hd] - m[n][hd])
                for hd in heads] for n in ns]

        yield
        y_d = []
        for n in ns:
            y_n = []
            for g in gs:
                gl = slice(g * half, (g + 1) * half)
                cb = cb_all[n][g * C:(g + 1) * C, g * C:(g + 1) * C]
                m_parts = [(cb * decay[n][g * GH + r]).astype(bf16) for r in range(GH)]
                x_parts = [jnp.where(head_of_lane == r, xdt_b[n][:, gl], zero_b) for r in range(GH)]
                y_n.append(jnp.dot(jnp.concatenate(m_parts, axis=1), jnp.concatenate(x_parts, axis=0),
                                   preferred_element_type=f32))
            y_d.append(y_n)
        st = [statet_s[g] for g in gs]
        y_off = []
        for n in ns:
            y_off.append([_dot(cm[n][:, g * SSD_STATE:(g + 1) * SSD_STATE], st[g]) for g in gs])
            st = [st[g] * ecs_e[n][C - 1:C, g * half:(g + 1) * half] + contrib[n][g] for g in gs]
        for g in gs:
            statet_s[g] = st[g]

        yield
        p = [[(e[n][hd] * (1.0 / den[n][hd])).astype(bf16) for hd in heads] for n in ns]
        o_g = []
        for n in ns:
            o_n = []
            for g in range(ATT_KV_HEADS):
                p_rows = []
                for jb in range(2):
                    p_cols = []
                    for sub in range(2):
                        ph = p[n][g * 4 + jb * 2 + sub]
                        p_cols += [jnp.where(lower, zero_b, ph), jnp.where(lower, ph, zero_b)]
                    p_rows.append(jnp.concatenate(p_cols, axis=1))
                o_n.append(jnp.dot(jnp.concatenate(p_rows, axis=0),
                                   jnp.concatenate([v_lo[n][g], v_hi[n][g]], axis=0),
                                   preferred_element_type=f32))
            o_g.append(o_n)

        yield
        for n in ns:
            y = (jnp.concatenate([y_d[n][g] + y_off[n][g] * ecs_e[n][:, g * half:(g + 1) * half]
                                  for g in gs], axis=-1)
                 + xs[n] * dskip_ref[...])
            y_ssd = _gated_group_norm(y, z_s[rows[n], :], gssd_ref[...])
            mix_s[rows[n], 0:SSD_WIDTH] = y_ssd.astype(bf16)
        for n in ns:
            for g in range(ATT_KV_HEADS):
                for jb in range(2):
                    lo_l = SSD_WIDTH + (2 * g + jb) * LANES
                    mix_s[rows[n], lo_l:lo_l + LANES] = o_g[n][g][jb * C:(jb + 1) * C].astype(bf16)

        yield

        mix_in = mix_s[...]
        mix = []
        for pc in range(D_MODEL // PW):
            mix.append(jnp.dot(mix_in, wout_ref[:, pc * PW:(pc + 1) * PW], preferred_element_type=f32))
            yield
        y_ref[s, 0] = x + _rmsnorm(jnp.concatenate(mix, axis=-1), gpost_ref[...])

    def mixed(ga, na, gb, nb):
        done_b = 0
        for ka in range(na):
            next(ga, None)
            want_b = ((ka + 1) * nb) // na
            for _ in range(want_b - done_b):
                next(gb, None)
            done_b = want_b

    g0, g1 = [stream(s) for s in range(STREAMS)]
    for _ in range(PROJ_PIECES):
        next(g0, None)
    mixed(g0, SCAN_PIECES, g1, PROJ_PIECES)
    mixed(g1, SCAN_PIECES, g0, OUT_PIECES)
    for g in (g0, g1):
        for _ in g:
            pass

    @pl.when(i == pl.num_programs(1) - 1)
    def _final_state():
        for s in range(STREAMS):
            for g in range(SSD_GROUPS):
                ssm_ref[s, 0, g * half:(g + 1) * half, :] = statet_sc[s, g].T
            wk_ref[s, 0] = k_sc[s, tile - WINDOW:tile, :].T
            wv_ref[s, 0] = v_sc[s, tile - WINDOW:tile, :].T


def _const_spec(shape):
    nd = len(shape)
    return pl.BlockSpec(shape, lambda *_: (0,) * nd)


def _prompt_mixer(x, gpre, win, convw, convb, dtb, alog, dskip_e, gssd, sinks, wout, gpost, tile):
    B, L, D = x.shape
    S = STREAMS
    G = B // S
    kern = functools.partial(_prompt_mixer_kernel, tile=tile)

    def per_seq(rows, width):
        return pl.BlockSpec((S, 1, rows, width), lambda b, i: (0, b, 0, 0))

    tile_spec = pl.BlockSpec((S, 1, tile, D), lambda b, i: (0, b, i, 0))
    out_shape = (
        jax.ShapeDtypeStruct((S, G, L, D), f32),
        jax.ShapeDtypeStruct((S, G, SSD_WIDTH, SSD_STATE), f32),
        jax.ShapeDtypeStruct((S, G, SUBLANES, SSD_CONV_DIM), f32),
        jax.ShapeDtypeStruct((S, G, ATT_KV_WIDTH, WINDOW), f32),
        jax.ShapeDtypeStruct((S, G, ATT_KV_WIDTH, WINDOW), f32),
    )
    in_specs = [
        tile_spec,
        _const_spec(gpre.shape), _const_spec(win.shape), _const_spec(convw.shape),
        _const_spec(convb.shape), _const_spec(dtb.shape), _const_spec(alog.shape),
        _const_spec(dskip_e.shape), _const_spec(gssd.shape),
        pl.BlockSpec(memory_space=pltpu.SMEM),
        _const_spec(wout.shape), _const_spec(gpost.shape),
    ]
    out_specs = (
        tile_spec,
        per_seq(SSD_WIDTH, SSD_STATE),
        per_seq(SUBLANES, SSD_CONV_DIM),
        per_seq(ATT_KV_WIDTH, WINDOW),
        per_seq(ATT_KV_WIDTH, WINDOW),
    )
    scratch = [
        pltpu.VMEM((S, SSD_GROUPS, SSD_STATE, SSD_WIDTH // SSD_GROUPS), f32),
        pltpu.VMEM((S, CONV_BLOCKS, tile + 2 * SUBLANES, LANES), f32),
        pltpu.VMEM((S, CONV_BLOCKS, tile, LANES), f32),
        pltpu.VMEM((S, tile, SSD_WIDTH), f32),
        pltpu.VMEM((S, tile, ATT_WIDTH), f32),
        pltpu.VMEM((S, tile, ATT_KV_WIDTH), f32),
        pltpu.VMEM((S, tile, ATT_KV_WIDTH), f32),
        pltpu.VMEM((S, tile, LANES), f32),
        pltpu.VMEM((S, WINDOW, ATT_KV_WIDTH), f32),
        pltpu.VMEM((S, WINDOW, ATT_KV_WIDTH), f32),
        pltpu.VMEM((S, tile, 2 * SSD_WIDTH), bf16),
    ]
    outs = pl.pallas_call(
        kern, grid=(G, L // tile), in_specs=in_specs, out_specs=out_specs, out_shape=out_shape,
        scratch_shapes=scratch, name="prompt_mixer",
        compiler_params=pltpu.CompilerParams(
            dimension_semantics=("arbitrary", "arbitrary"), vmem_limit_bytes=VMEM_LIMIT),
    )(x.reshape(S, G, L, D), gpre, win, convw, convb, dtb, alog, dskip_e, gssd, sinks, wout, gpost)
    return tuple(o.reshape(B, *o.shape[2:]) for o in outs)


def _memkv_kernel(m_ref, g_ref, wk_ref, wv_ref, k_ref, v_ref, kh_ref, vh_ref):
    mn = _rmsnorm(m_ref[...], g_ref[...]).astype(bf16)
    k = jnp.dot(mn, wk_ref[...], preferred_element_type=f32)
    v = jnp.dot(mn, wv_ref[...], preferred_element_type=f32)
    k_ref[...] = k
    v_ref[...] = v
    for hd in range(X_HEADS):
        kh_ref[:, hd, :] = k[:, hd * X_HEAD_DIM:(hd + 1) * X_HEAD_DIM]
        vh_ref[:, hd, :] = v[:, hd * X_HEAD_DIM:(hd + 1) * X_HEAD_DIM]


def _memkv(mem2d, g, wk, wv, tile):
    n, d = mem2d.shape
    row = pl.BlockSpec((tile, d), lambda i: (i, 0))
    hrow = pl.BlockSpec((tile, X_HEADS, X_HEAD_DIM), lambda i: (i, 0, 0))
    flat = jax.ShapeDtypeStruct((n, d), f32)
    heads = jax.ShapeDtypeStruct((n, X_HEADS, X_HEAD_DIM), f32)
    return pl.pallas_call(
        _memkv_kernel, grid=(n // tile,),
        in_specs=[row, _const_spec(g.shape), _const_spec(wk.shape), _const_spec(wv.shape)],
        out_specs=(row, row, hrow, hrow),
        out_shape=(flat, flat, heads, heads),
        name="memory_kv",
        compiler_params=pltpu.CompilerParams(
            dimension_semantics=("arbitrary",), vmem_limit_bytes=VMEM_LIMIT),
    )(mem2d, g, wk, wv)


def _prompt_xattn_kernel(x_ref, gpre_ref, wq_ref, mk_ref, mv_ref, wo_ref, gpost_ref,
                         wg_ref, wu_ref, wd_ref, y_ref, wg_out, wu_out, wd_out):
    hs = range(X_HEADS)
    sl = [slice(hd * X_HEAD_DIM, (hd + 1) * X_HEAD_DIM) for hd in hs]

    def stream(s):
        x = x_ref[s, 0]
        hn = _rmsnorm(x, gpre_ref[...]).astype(bf16)
        yield
        q = []
        for hd in hs:
            q.append(jnp.dot(hn, wq_ref[:, sl[hd]], preferred_element_type=f32))
            yield
        sc = [_dot_nt(q[hd], mk_ref[s, 0, :, sl[hd]]) for hd in hs]
        yield
        m = [jnp.max(sc[hd], axis=-1, keepdims=True) for hd in hs]
        e = [jnp.exp(sc[hd] - m[hd]) for hd in hs]
        yield
        r = [1.0 / jnp.sum(e[hd], axis=-1, keepdims=True) for hd in hs]
        p = [(e[hd] * r[hd]).astype(bf16) for hd in hs]
        yield
        o = jnp.concatenate([_dot(p[hd], mv_ref[s, 0, :, sl[hd]]) for hd in hs], axis=-1).astype(bf16)
        yield
        c = []
        for hd in hs:
            c.append(jnp.dot(o, wo_ref[:, sl[hd]], preferred_element_type=f32))
            yield
        y_ref[s, 0] = x + _rmsnorm(jnp.concatenate(c, axis=-1), gpost_ref[...])

    live = [stream(s) for s in range(STREAMS)]
    for _ in range(XATTN_LEAD):
        next(live[0], None)
    while live:
        live = [g for g in live if next(g, True) is None]

    wg_out[...] = wg_ref[...].astype(bf16)
    wu_out[...] = wu_ref[...].astype(bf16)
    wd_out[...] = wd_ref[...].astype(bf16)


def _prompt_xattn(x, gpre, wq, mk, mv, wo, gpost, ffn_w, tile):
    B, L, D = x.shape
    S = STREAMS
    G = B // S
    nt = L // tile
    xs = pl.BlockSpec((S, 1, tile, D), lambda b, i: (0, b, i, 0))
    ms = pl.BlockSpec((S, 1, N_MEM, D), lambda b, i: (0, b, 0, 0))

    def slab(w):
        return pl.BlockSpec((w.shape[0] // (G * nt), w.shape[1]), lambda b, i: (b * nt + i, 0))

    outs = pl.pallas_call(
        _prompt_xattn_kernel, grid=(G, nt),
        in_specs=[xs, _const_spec(gpre.shape), _const_spec(wq.shape), ms, ms,
                  _const_spec(wo.shape), _const_spec(gpost.shape)] + [slab(w) for w in ffn_w],
        out_specs=(xs,) + tuple(slab(w) for w in ffn_w),
        out_shape=(jax.ShapeDtypeStruct((S, G, L, D), f32),)
        + tuple(jax.ShapeDtypeStruct(w.shape, bf16) for w in ffn_w),
        name="prompt_xattn",
        compiler_params=pltpu.CompilerParams(
            dimension_semantics=("arbitrary", "arbitrary"), vmem_limit_bytes=VMEM_LIMIT),
    )(x.reshape(S, G, L, D), gpre, wq, mk.reshape(S, G, N_MEM, D), mv.reshape(S, G, N_MEM, D),
      wo, gpost, *ffn_w)
    return (outs[0].reshape(B, L, D),) + tuple(outs[1:])


def _ffn_kernel(xp_ref, xs_ref, gpre_ref, wg_ref, wu_ref, wd_ref, gpost_ref, yp_ref, ys_ref):
    i = pl.program_id(0)
    last = pl.num_programs(0) - 1
    x = jnp.where(i < last, xp_ref[...], xs_ref[...])
    hf = _rmsnorm(x, gpre_ref[...]).astype(bf16)
    gate = jnp.dot(hf, wg_ref[...], preferred_element_type=f32)
    up = jnp.dot(hf, wu_ref[...], preferred_element_type=f32)
    act = (_silu(gate) * up).astype(bf16)
    f = jnp.dot(act, wd_ref[...], preferred_element_type=f32)
    y = x + _rmsnorm(f, gpost_ref[...])

    @pl.when(i < last)
    def _prompt_rows():
        yp_ref[...] = y

    @pl.when(i == last)
    def _sample_rows():
        ys_ref[...] = y


def _ffn(xp2d, xs2d, gpre, wg, wu, wd, gpost, tile):
    n, d = xp2d.shape
    assert xs2d.shape == (tile, d)
    steps_p = n // tile
    prow = pl.BlockSpec((tile, d), lambda i: (jnp.minimum(i, steps_p - 1), 0))
    srow = pl.BlockSpec((tile, d), lambda i: (0, 0))
    return pl.pallas_call(
        _ffn_kernel, grid=(steps_p + 1,),
        in_specs=[prow, srow, _const_spec(gpre.shape), _const_spec(wg.shape), _const_spec(wu.shape),
                  _const_spec(wd.shape), _const_spec(gpost.shape)],
        out_specs=(prow, srow),
        out_shape=(jax.ShapeDtypeStruct((n, d), f32), jax.ShapeDtypeStruct((tile, d), f32)),
        name="ffn",
        compiler_params=pltpu.CompilerParams(
            dimension_semantics=("arbitrary",), vmem_limit_bytes=VMEM_LIMIT),
    )(xp2d, xs2d, gpre, wg, wu, wd, gpost)


def _pad_rows(a, rows):
    if a.shape[0] == rows:
        return a
    return jnp.concatenate([a, jnp.zeros((rows - a.shape[0], a.shape[1]), a.dtype)], axis=0)


def _sample_mixer_kernel(x_ref, cprev_ref, st_ref, ck_ref, cv_ref,
                         gpre_ref, win_ref, convw_ref, convb_ref, dtb_ref, alog_ref,
                         dskip_ref, gssd_ref, sinkcol_ref, wout_ref, gpost_ref,
                         y_ref, ssm_ref, cnew_ref, wk_ref, wv_ref, *, bt, steps):
    R = steps * bt
    half = SSD_WIDTH // SSD_GROUPS
    x = x_ref[...].reshape(R, D_MODEL)
    h = _rmsnorm(x, gpre_ref[...]).astype(bf16)
    z = jnp.dot(h, win_ref[:, P_Z:P_XBC], preferred_element_type=f32)
    u = jnp.dot(h, win_ref[:, P_XBC:P_Q], preferred_element_type=f32)
    q = jnp.dot(h, win_ref[:, P_Q:P_K], preferred_element_type=f32)
    k_new = jnp.dot(h, win_ref[:, P_K:P_V], preferred_element_type=f32)
    v_new = jnp.dot(h, win_ref[:, P_V:P_DT], preferred_element_type=f32)
    dt_raw = jnp.dot(h, win_ref[:, P_DT:P_END], preferred_element_type=f32)
    lane = lax.broadcasted_iota(jnp.int32, (1, LANES), 1)
    dt = jnp.where(lane < SSD_HEADS, _softplus(dt_raw + dtb_ref[...]), 0.0)

    def slab(a, t):
        return a[t * bt:(t + 1) * bt]

    HD = ATT_HEAD_DIM
    GH = ATT_HEADS // ATT_KV_HEADS
    GR = GH * R
    reps = GR // bt
    kvg = range(ATT_KV_HEADS)
    keep = WINDOW - steps

    kn_t = _pad_rows(k_new, LANES).T
    vn_t = _pad_rows(v_new, LANES).T

    ridx = lax.broadcasted_iota(jnp.int32, (GR, 1), 0)
    rb = ridx % bt
    rt = (ridx // bt) % steps
    qg = [jnp.concatenate([q[:, (g * GH + hl) * HD:(g * GH + hl + 1) * HD] for hl in range(GH)],
                          axis=0) for g in kvg]
    qg_b = [qg[g].astype(bf16) for g in kvg]
    s_cb = [[_dot(qg_b[g], ck_ref[b, g]) for b in range(bt)] for g in kvg]

    hist = [cprev_ref[j] for j in range(SSD_CONV - 1)] + [slab(u, t) for t in range(steps)]
    xbc_t = []
    for t in range(steps):
        acc = convb_ref[...]
        for j in range(SSD_CONV):
            acc = acc + hist[t + j] * convw_ref[j:j + 1, :]
        xbc_t.append(_silu(acc))
    for j in range(SSD_CONV - 1):
        cnew_ref[j] = hist[steps + j]
    xbc = jnp.concatenate(xbc_t, axis=0)
    xs = xbc[:, 0:SSD_WIDTH]
    bm = xbc[:, SSD_WIDTH:SSD_WIDTH + half]
    cm = xbc[:, SSD_WIDTH + half:SSD_CONV_DIM]
    a_row = -jnp.exp(alog_ref[...])
    adt = dt * a_row
    cs_t = [slab(adt, 0)]
    for t in range(1, steps):
        cs_t.append(cs_t[-1] + slab(adt, t))
    cs = jnp.concatenate(cs_t, axis=0)
    tot = cs_t[-1]
    tot_rows = jnp.concatenate([tot] * steps, axis=0)
    expand = _head_expand_matrix()
    expand2 = jnp.concatenate([expand, expand], axis=0)
    pairs = [(t, s2) for t in range(steps) for s2 in range(t)]
    fac = jnp.concatenate([dt, jnp.exp(cs), jnp.exp(tot_rows - cs), jnp.exp(tot)]
                          + [jnp.exp(cs_t[t] - cs_t[s2]) for t, s2 in pairs], axis=0)
    f_hi, f_lo = _split2(fac)
    fac_e = jnp.dot(jnp.concatenate([f_hi, f_lo], axis=1), expand2, preferred_element_type=f32)
    gr = lax.broadcasted_iota(jnp.int32, (half, SSD_WIDTH), 0)
    gc = lax.broadcasted_iota(jnp.int32, (half, SSD_WIDTH), 1)
    gsum = ((gr >> 7) == (gc >> 8)).astype(bf16)
    gsum2 = jnp.concatenate([gsum, gsum], axis=0)
    cb_pairs = [(t, s2) for t in range(steps) for s2 in range(t + 1)]
    prod = jnp.concatenate([slab(cm, t) * slab(bm, s2) for t, s2 in cb_pairs], axis=0)
    c_hi, c_lo = _split2(prod)
    cb_e = jnp.dot(jnp.concatenate([c_hi, c_lo], axis=1), gsum2, preferred_element_type=f32)

    sel_r = lax.broadcasted_iota(jnp.int32, (LANES, WINDOW), 0)
    sel_l = lax.broadcasted_iota(jnp.int32, (LANES, WINDOW), 1)
    sel = [((sel_r % bt == b) & (sel_r < R) & (sel_l - keep == sel_r // bt)).astype(bf16)
           for b in range(bt)]
    new_k = [_dot_x2(kn_t, sel[b]) for b in range(bt)]
    new_v = [_dot_x2(vn_t, sel[b]) for b in range(bt)]

    jcol = lax.broadcasted_iota(jnp.int32, (GR, WINDOW), 1)
    in_window = jcol > rt
    s_c, m, s_n = [], [], []
    for g in kvg:
        acc = jnp.zeros((GR, WINDOW), f32)
        for b in range(bt):
            acc = jnp.where(rb == b, s_cb[g][b], acc)
        s_c.append(jnp.where(in_window, acc, -jnp.inf))
    sink = [sinkcol_ref[g * GR:(g + 1) * GR, :] for g in kvg]
    for g in kvg:
        mg = jnp.maximum(jnp.max(s_c[g], axis=-1, keepdims=True), sink[g])
        sn_g = []
        for t2 in range(steps):
            kt = jnp.concatenate([slab(k_new, t2)[:, g * HD:(g + 1) * HD]] * reps, axis=0)
            sn = jnp.where(rt >= t2, jnp.sum(qg[g] * kt, axis=-1, keepdims=True), -jnp.inf)
            sn_g.append(sn)
            mg = jnp.maximum(mg, sn)
        m.append(mg)
        s_n.append(sn_g)
    e_c = [jnp.exp(s_c[g] - m[g]) for g in kvg]
    e_n = [[jnp.exp(sn - m[g]) for sn in s_n[g]] for g in kvg]
    rinv = []
    for g in kvg:
        den = jnp.sum(e_c[g], axis=-1, keepdims=True) + jnp.exp(sink[g] - m[g])
        for en in e_n[g]:
            den = den + en
        rinv.append(1.0 / den)
    p_c = [(e_c[g] * rinv[g]).astype(bf16) for g in kvg]

    xdt = xs * fac_e[0:R]
    ecs_e = fac_e[R:2 * R]
    w_end = xdt * fac_e[2 * R:3 * R]
    dec_e = fac_e[3 * R:3 * R + bt]
    pair_decay = {pr: fac_e[3 * R + (n + 1) * bt:3 * R + (n + 2) * bt] for n, pr in enumerate(pairs)}
    y_t = []
    for t in range(steps):
        acc = None
        for s2 in range(t + 1):
            n = cb_pairs.index((t, s2))
            coef = cb_e[n * bt:(n + 1) * bt]
            if s2 < t:
                coef = coef * pair_decay[(t, s2)]
            term = coef * slab(xdt, s2)
            acc = term if acc is None else acc + term
        y_t.append(acc)
    y_intra = jnp.concatenate(y_t, axis=0)
    b_idx = lax.broadcasted_iota(jnp.int32, (bt, 1, LANES), 0)
    l_idx = lax.broadcasted_iota(jnp.int32, (bt, 1, LANES), 2)
    pair = ((l_idx & (bt - 1)) == b_idx) & (l_idx < R)
    own = (l_idx == b_idx)
    gsl = [slice(g * half, (g + 1) * half) for g in range(SSD_GROUPS)]
    h0 = [st_ref[:, gsl[g], :] for g in range(SSD_GROUPS)]
    zz = [_dot_nt(h0[g].reshape(bt * half, SSD_STATE),
                  _pad_rows(cm[:, g * SSD_STATE:(g + 1) * SSD_STATE], LANES)).reshape(bt, half, LANES)
          for g in range(SSD_GROUPS)]
    wt = [_pad_rows(w_end[:, gsl[g]], LANES).T for g in range(SSD_GROUPS)]
    contrib = [_dot(jnp.where(pair, wt[g][None], 0.0).reshape(bt * half, LANES),
                    _pad_rows(bm[:, g * SSD_STATE:(g + 1) * SSD_STATE], LANES)
                    ).reshape(bt, half, SSD_STATE) for g in range(SSD_GROUPS)]
    dec_t = [_pad_rows(dec_e[:, gsl[g]], LANES).T for g in range(SSD_GROUPS)]

    o = []
    for g in kvg:
        pv = [_dot_nt(p_c[g], cv_ref[b, g]) for b in range(bt)]
        og = jnp.zeros((GR, HD), f32)
        for b in range(bt):
            og = jnp.where(rb == b, pv[b], og)
        for t2 in range(steps):
            vt = jnp.concatenate([slab(v_new, t2)[:, g * HD:(g + 1) * HD]] * reps, axis=0)
            og = og + (e_n[g][t2] * rinv[g]) * vt
        o.append(og)

    lane_w = lax.broadcasted_iota(jnp.int32, (1, WINDOW), 1)
    for b in range(bt):
        for g in kvg:
            gs = slice(g * HD, (g + 1) * HD)
            wk_ref[b, g] = jnp.where(lane_w < keep, pltpu.roll(ck_ref[b, g], keep, 1), new_k[b][gs])
            wv_ref[b, g] = jnp.where(lane_w < keep, pltpu.roll(cv_ref[b, g], keep, 1), new_v[b][gs])

    y_off_parts = []
    for g in range(SSD_GROUPS):
        yt = jnp.sum(jnp.where(pair, zz[g], 0.0), axis=0)
        y_off_parts.append(yt.T[0:R, :])
        dec = jnp.sum(jnp.where(own, dec_t[g][None], 0.0), axis=-1, keepdims=True)
        ssm_ref[:, gsl[g], :] = h0[g] * dec + contrib[g]
    y_off = jnp.concatenate(y_off_parts, axis=-1) * ecs_e
    y = y_intra + y_off + xs * dskip_ref[...]
    y_ssd = _gated_group_norm(y, z, gssd_ref[...])

    mix = jnp.dot(y_ssd.astype(bf16), wout_ref[0:SSD_WIDTH, :], preferred_element_type=f32)
    for g in kvg:
        for hl in range(GH):
            hd = g * GH + hl
            mix = mix + jnp.dot(o[g][hl * R:(hl + 1) * R].astype(bf16),
                                wout_ref[SSD_WIDTH + hd * HD:SSD_WIDTH + (hd + 1) * HD, :],
                                preferred_element_type=f32)
    y_ref[...] = (x + _rmsnorm(mix, gpost_ref[...])).reshape(steps, bt, D_MODEL)


def _sample_mixer(x_tm, cprev_tm, st, ck, cv, gpre, win, convw, convb, dtb, alog, dskip_e, gssd,
                  sinkcol, wout, gpost, bt):
    steps, nb, D = x_tm.shape
    kern = functools.partial(_sample_mixer_kernel, bt=bt, steps=steps)
    tm = lambda w: pl.BlockSpec((steps, bt, w), lambda i: (0, i, 0))
    win_spec = pl.BlockSpec((bt, ATT_KV_HEADS, ATT_HEAD_DIM, WINDOW), lambda i: (i, 0, 0, 0))
    in_specs = [
        tm(D),
        pl.BlockSpec((SSD_CONV - 1, bt, SSD_CONV_DIM), lambda i: (0, i, 0)),
        pl.BlockSpec((bt, SSD_WIDTH, SSD_STATE), lambda i: (i, 0, 0)),
        win_spec, win_spec,
    ] + [_const_spec(a.shape) for a in (gpre, win, convw, convb, dtb, alog, dskip_e, gssd,
                                        sinkcol, wout, gpost)]
    out_specs = (
        tm(D),
        pl.BlockSpec((bt, SSD_WIDTH, SSD_STATE), lambda i: (i, 0, 0)),
        pl.BlockSpec((SSD_CONV - 1, bt, SSD_CONV_DIM), lambda i: (0, i, 0)),
        win_spec, win_spec,
    )
    out_shape = (
        jax.ShapeDtypeStruct((steps, nb, D), f32),
        jax.ShapeDtypeStruct((nb, SSD_WIDTH, SSD_STATE), f32),
        jax.ShapeDtypeStruct((SSD_CONV - 1, nb, SSD_CONV_DIM), f32),
        jax.ShapeDtypeStruct(ck.shape, f32),
        jax.ShapeDtypeStruct(cv.shape, f32),
    )
    return pl.pallas_call(
        kern, grid=(nb // bt,), in_specs=in_specs, out_specs=out_specs, out_shape=out_shape,
        name="sample_mixer",
        compiler_params=pltpu.CompilerParams(
            dimension_semantics=("arbitrary",), vmem_limit_bytes=VMEM_LIMIT),
    )(x_tm, cprev_tm, st, ck, cv, gpre, win, convw, convb, dtb, alog, dskip_e, gssd, sinkcol,
      wout, gpost)


def _sample_xattn_kernel(x_ref, mk_ref, mv_ref, gpre_ref, wq_ref, wo_ref, gpost_ref, y_ref,
                         *, bt, steps):
    R = steps * bt
    nrow = bt * N_MEM * X_HEADS
    x = x_ref[...].reshape(R, D_MODEL)
    hn = _rmsnorm(x, gpre_ref[...]).astype(bf16)
    q = jnp.dot(hn, wq_ref[...], preferred_element_type=f32)
    qs = jnp.concatenate([q[:, hd * X_HEAD_DIM:(hd + 1) * X_HEAD_DIM] for hd in range(X_HEADS)],
                         axis=0)
    kall = mk_ref[...].reshape(nrow, X_HEAD_DIM)
    vall = mv_ref[...].reshape(nrow, X_HEAD_DIM)
    ncol = X_HEADS * R
    seq_rows = N_MEM * X_HEADS
    z = _dot_nt(kall, qs).reshape(bt, seq_rows, ncol)
    v_b = [vall[b * seq_rows:(b + 1) * seq_rows].astype(bf16) for b in range(bt)]
    b_i = lax.broadcasted_iota(jnp.int32, (bt, 1, ncol), 0)
    c_i = lax.broadcasted_iota(jnp.int32, (bt, 1, ncol), 2)
    zc = jnp.sum(jnp.where(c_i % bt == b_i, z, 0.0), axis=0)
    zc = zc.reshape(seq_rows // SUBLANES, SUBLANES, ncol)
    r_h = lax.broadcasted_iota(jnp.int32, (1, SUBLANES, ncol), 1) % X_HEADS
    c_h = lax.broadcasted_iota(jnp.int32, (1, SUBLANES, ncol), 2) // R
    zc = jnp.where(r_h == c_h, zc, -jnp.inf).reshape(seq_rows, ncol)
    m = jnp.max(zc, axis=0, keepdims=True)
    e = jnp.exp(zc - m)
    p = e * (1.0 / jnp.sum(e, axis=0, keepdims=True))
    col_b = lax.broadcasted_iota(jnp.int32, (1, ncol), 1) % bt
    tn = (((0,), (0,)), ((), ()))
    o = None
    for b in range(bt):
        p_b = jnp.where(col_b == b, p, 0.0).astype(bf16)
        o_b = lax.dot_general(p_b, v_b[b], tn, preferred_element_type=f32)
        o = o_b if o is None else o + o_b
    o = jnp.concatenate([o[hd * R:(hd + 1) * R] for hd in range(X_HEADS)], axis=-1)
    cc = _dot(o, wo_ref[...])
    y_ref[...] = (x + _rmsnorm(cc, gpost_ref[...])).reshape(steps, bt, D_MODEL)


def _sample_xattn(x_tm, mk, mv, gpre, wq, wo, gpost, bt):
    steps, nb, D = x_tm.shape
    kern = functools.partial(_sample_xattn_kernel, bt=bt, steps=steps)
    xs = pl.BlockSpec((steps, bt, D), lambda i: (0, i, 0))
    ms = pl.BlockSpec((bt, N_MEM, X_HEADS, X_HEAD_DIM), lambda i: (i, 0, 0, 0))
    return pl.pallas_call(
        kern, grid=(nb // bt,),
        in_specs=[xs, ms, ms, _const_spec(gpre.shape), _const_spec(wq.shape),
                  _const_spec(wo.shape), _const_spec(gpost.shape)],
        out_specs=xs, out_shape=jax.ShapeDtypeStruct((steps, nb, D), f32),
        name="sample_xattn",
        compiler_params=pltpu.CompilerParams(
            dimension_semantics=("arbitrary",), vmem_limit_bytes=VMEM_LIMIT),
    )(x_tm, mk, mv, gpre, wq, wo, gpost)


def _win_prep_kernel(wt_ref, o_ref):
    piece = 2 * LANES

    def put(src_lo, n, dst_lo, scale=None):
        for c in range(0, n, piece):
            t = wt_ref[src_lo + c:src_lo + c + piece, :].T
            if scale is not None:
                t = t * scale
            o_ref[:, dst_lo + c:dst_lo + c + piece] = t.astype(bf16)

    put(W_Z, P_Q - P_Z, P_Z)
    put(W_Q, P_K - P_Q, P_Q, ATT_SCALE)
    put(W_K, P_DT - P_K, P_K)
    dt_rows = jnp.concatenate([wt_ref[W_DT:W_DT + SSD_HEADS, :],
                               jnp.zeros((LANES - SSD_HEADS, D_MODEL), f32)], axis=0)
    o_ref[:, P_DT:P_END] = dt_rows.T.astype(bf16)


def _win_prep(w_t):
    return pl.pallas_call(
        _win_prep_kernel, out_shape=jax.ShapeDtypeStruct((D_MODEL, P_END), bf16),
        name="win_prep",
        compiler_params=pltpu.CompilerParams(vmem_limit_bytes=VMEM_LIMIT),
    )(w_t)


def _row(v, width=None):
    v = v.reshape(1, -1).astype(f32)
    if width is not None and v.shape[1] < width:
        v = jnp.pad(v, ((0, 0), (0, width - v.shape[1])))
    return v


def kernel(x_prompt, x_sample, state_ssm, state_conv, cache_win_k, cache_win_v, cache_mem_k, cache_mem_v, mem_prompt, g_mix_pre, w_in, conv_w, conv_b, dt_bias, a_log, d_skip, g_ssd_norm, sinks, w_out, g_mix_post, g_x_pre, w_xq, g_mem, w_xk, w_xv, w_xo, g_x_post, g_ffn_pre, w_gate, w_up, w_down, g_ffn_post):
    depth = w_in.shape[0]
    assert depth == 1
    B, L, D = x_prompt.shape
    NB, steps, _ = x_sample.shape
    li = 0

    win_p = _win_prep(jnp.transpose(w_in[li]))
    wo_b = w_out[li].astype(bf16)

    gpre, gpost = _row(g_mix_pre[li]), _row(g_mix_post[li])
    convw, convb = conv_w[li].astype(f32), _row(conv_b[li])
    dtb, alog = _row(dt_bias[li], LANES), _row(a_log[li], LANES)
    dskip_e = _row(jnp.repeat(d_skip[li], SSD_HEAD_DIM))
    gssd = _row(g_ssd_norm[li])
    sk = sinks[li].astype(f32)

    mk2d, mv2d, mk4, mv4 = _memkv(mem_prompt.reshape(B * N_MEM, D), _row(g_mem[li]),
                                  w_xk[li].astype(bf16), w_xv[li].astype(bf16), tile=512)
    mk3, mv3 = mk2d.reshape(B, N_MEM, D), mv2d.reshape(B, N_MEM, D)
    x1, p_ssm, p_conv8, p_wk, p_wv = _prompt_mixer(
        x_prompt, gpre, win_p, convw, convb, dtb, alog, dskip_e, gssd, sk, wo_b, gpost, tile=512)
    wxq_b, wxo_b = (w_xq[li] * X_SCALE).astype(bf16), w_xo[li].astype(bf16)
    gxpre, gxpost = _row(g_x_pre[li]), _row(g_x_post[li])
    x2, wg_b, wu_b, wd_b = _prompt_xattn(x1, gxpre, wxq_b, mk3, mv3, wxo_b, gxpost,
                                         (w_gate[li], w_up[li], w_down[li]), tile=512)
    gfpre, gfpost = _row(g_ffn_pre[li]), _row(g_ffn_post[li])

    bt, bt_mix = 8, 16
    x_tm = jnp.transpose(x_sample, (1, 0, 2))
    cprev_tm = jnp.transpose(state_conv[li], (1, 0, 2))
    st = state_ssm[li].reshape(NB, SSD_WIDTH, SSD_STATE)
    ck = jnp.transpose(cache_win_k[li], (0, 2, 3, 1))
    cv = jnp.transpose(cache_win_v[li], (0, 2, 3, 1))
    sinkcol = jnp.repeat(sk, steps * bt_mix).reshape(ATT_HEADS * steps * bt_mix, 1)
    x1s, s_ssm, cnew_tm, s_wk, s_wv = _sample_mixer(
        x_tm, cprev_tm, st, ck, cv, gpre, win_p, convw, convb, dtb, alog, dskip_e, gssd,
        sinkcol, wo_b, gpost, bt=bt_mix)
    cmk = cache_mem_k.reshape(NB, N_MEM, X_HEADS, X_HEAD_DIM)
    cmv = cache_mem_v.reshape(NB, N_MEM, X_HEADS, X_HEAD_DIM)
    x2s = _sample_xattn(x1s, cmk, cmv, gxpre, wxq_b, wxo_b, gxpost, bt=bt)
    yp2d, ys_tm = _ffn(x2.reshape(B * L, D), x2s.reshape(steps * NB, D), gfpre, wg_b, wu_b, wd_b,
                       gfpost, tile=steps * NB)
    yp = yp2d.reshape(B, L, D)
    ys = jnp.transpose(ys_tm.reshape(steps, NB, D), (1, 0, 2))

    s_conv = jnp.transpose(cnew_tm, (1, 0, 2))
    kv_shape = (ATT_KV_HEADS, ATT_HEAD_DIM)
    return (
        yp, ys,
        p_ssm.reshape(1, B, SSD_HEADS, SSD_HEAD_DIM, SSD_STATE),
        p_conv8[:, SUBLANES - (SSD_CONV - 1):, :][None],
        jnp.transpose(p_wk.reshape(B, *kv_shape, WINDOW), (0, 3, 1, 2))[None],
        jnp.transpose(p_wv.reshape(B, *kv_shape, WINDOW), (0, 3, 1, 2))[None],
        mk4.reshape(1, B, N_MEM, X_HEADS, X_HEAD_DIM), mv4.reshape(1, B, N_MEM, X_HEADS, X_HEAD_DIM),
        s_ssm.reshape(1, NB, SSD_HEADS, SSD_HEAD_DIM, SSD_STATE),
        s_conv[None],
        jnp.transpose(s_wk, (0, 3, 1, 2))[None], jnp.transpose(s_wv, (0, 3, 1, 2))[None],
    )
```

```python
import functools

import jax
import jax.numpy as jnp
from jax import lax
from jax.experimental import pallas as pl
from jax.experimental.pallas import tpu as pltpu

f32 = jnp.float32
bf16 = jnp.bfloat16

D_MODEL = 1024
EPS = 1e-6
N_MEM = 256
SSD_HEADS = 8
SSD_HEAD_DIM = 64
SSD_WIDTH = 512
SSD_GROUPS = 2
SSD_STATE = 128
SSD_CONV = 4
SSD_CHUNK = 128
SSD_CONV_DIM = 1024
ATT_HEADS = 8
ATT_KV_HEADS = 2
ATT_HEAD_DIM = 64
ATT_WIDTH = 512
ATT_KV_WIDTH = 128
WINDOW = 128
ATT_SCALE = ATT_HEAD_DIM ** -0.5
X_HEADS = 4
X_HEAD_DIM = 256
X_SCALE = X_HEAD_DIM ** -0.5
D_FF = 2816
LANES = 128
SUBLANES = 8
VMEM_LIMIT = 56 * 1024 * 1024
STREAMS = 2
PROJ_PIECES, SCAN_PIECES, OUT_PIECES = 19, 10, 5
CONV_BLOCKS = SSD_CONV_DIM // LANES
XATTN_LEAD = 5

P_Z, P_XBC, P_Q, P_K, P_V, P_DT, P_END = 0, 512, 1536, 2048, 2176, 2304, 2432
W_Z, W_DT, W_Q, W_K = 0, 1536, 1544, 2056


def _dot(a, b):
    return jnp.dot(a.astype(bf16), b.astype(bf16), preferred_element_type=f32)


def _dot_nt(a, b):
    return lax.dot_general(a.astype(bf16), b.astype(bf16), (((1,), (1,)), ((), ())),
                           preferred_element_type=f32)


def _split2(x):
    hi = x.astype(bf16)
    lo = (x - hi.astype(f32)).astype(bf16)
    return hi, lo


def _dot_x2(x, m):
    hi, lo = _split2(x)
    return (jnp.dot(hi, m, preferred_element_type=f32)
            + jnp.dot(lo, m, preferred_element_type=f32))


def _rmsnorm(x, g):
    ms = jnp.mean(x * x, axis=-1, keepdims=True)
    return x * lax.rsqrt(ms + EPS) * g


def _silu(x):
    return x * jax.nn.sigmoid(x)


def _softplus(x):
    return jnp.maximum(x, 0.0) + jnp.log1p(jnp.exp(-jnp.abs(x)))


def _head_expand_matrix():
    r = lax.broadcasted_iota(jnp.int32, (LANES, SSD_WIDTH), 0)
    c = lax.broadcasted_iota(jnp.int32, (LANES, SSD_WIDTH), 1)
    return (r == (c >> 6)).astype(bf16)


def _gated_group_norm(y, z, g):
    u = y * _silu(z)
    half = SSD_WIDTH // SSD_GROUPS
    parts = []
    for gi in range(SSD_GROUPS):
        ug = u[:, gi * half:(gi + 1) * half]
        parts.append(ug * lax.rsqrt(jnp.mean(ug * ug, axis=-1, keepdims=True) + EPS))
    return jnp.concatenate(parts, axis=-1) * g


def _prompt_mixer_kernel(x_ref, gpre_ref, win_ref, convw_ref, convb_ref, dtb_ref, alog_ref,
                         dskip_ref, gssd_ref, sinks_ref, wout_ref, gpost_ref,
                         y_ref, ssm_ref, conv_ref, wk_ref, wv_ref,
                         statet_sc, xbc_ext_sc, xbc_sc, z_sc, q_sc, k_sc, v_sc, dt_sc,
                         kprev_sc, vprev_sc, mix_sc, *, tile):
    i = pl.program_id(1)
    NC = tile // SSD_CHUNK
    ns = range(NC)
    C = SSD_CHUNK
    PW = 2 * LANES

    @pl.when(i == 0)
    def _init():
        statet_sc[...] = jnp.zeros_like(statet_sc)
        xbc_ext_sc[:, :, 0:SUBLANES, :] = jnp.zeros((STREAMS, CONV_BLOCKS, SUBLANES, LANES), f32)
        kprev_sc[...] = jnp.zeros_like(kprev_sc)
        vprev_sc[...] = jnp.zeros_like(vprev_sc)

    lane = lax.broadcasted_iota(jnp.int32, (1, LANES), 1)
    a_row = -jnp.exp(alog_ref[...])
    expand = _head_expand_matrix()
    expand2 = jnp.concatenate([expand, expand], axis=0)
    row_i = lax.broadcasted_iota(jnp.int32, (C, C), 0)
    col_i = lax.broadcasted_iota(jnp.int32, (C, C), 1)
    lower = col_i <= row_i
    tri = lower.astype(bf16)
    tri3 = jnp.concatenate([tri, tri, tri], axis=1)
    lo_half = lane < ATT_HEAD_DIM
    half = SSD_WIDTH // SSD_GROUPS
    head_of_lane = lax.broadcasted_iota(jnp.int32, (1, half), 1) >> 6
    rows = [slice(n * C, (n + 1) * C) for n in ns]

    def stream(s):
        statet_s, xbc_ext_s, xbc_s, z_s = statet_sc.at[s], xbc_ext_sc.at[s], xbc_sc.at[s], z_sc.at[s]
        q_s, k_s, v_s, dt_s = q_sc.at[s], k_sc.at[s], v_sc.at[s], dt_sc.at[s]
        kprev_s, vprev_s, mix_s = kprev_sc.at[s], vprev_sc.at[s], mix_sc.at[s]

        x = x_ref[s, 0]
        h = _rmsnorm(x, gpre_ref[...]).astype(bf16)
        yield

        def proj(col):
            return jnp.dot(h, win_ref[:, col:col + PW], preferred_element_type=f32)

        def conv_cols(cb):
            cols = slice(cb * LANES, (cb + 1) * LANES)
            acc = convb_ref[:, cols]
            for j in range(SSD_CONV):
                off = SUBLANES - (SSD_CONV - 1) + j
                acc = acc + xbc_ext_s[cb, off:off + tile, :] * convw_ref[j:j + 1, cols]
            xbc_s[cb] = _silu(acc)

        def xbc_piece(pc):
            res = proj(P_XBC + pc * PW)
            for half_pc in range(PW // LANES):
                xbc_ext_s[(PW // LANES) * pc + half_pc, SUBLANES:SUBLANES + tile, :] = (
                    res[:, half_pc * LANES:(half_pc + 1) * LANES])

        xbc_piece(0)
        yield
        xbc_piece(1)
        yield
        conv_cols(0)
        yield
        xbc_piece(2)
        yield
        conv_cols(1)
        yield
        xbc_piece(3)
        yield
        conv_cols(2)
        yield
        z_s[:, 0:PW] = proj(P_Z)
        yield
        conv_cols(3)
        yield
        z_s[:, PW:2 * PW] = proj(P_Z + PW)
        yield
        conv_cols(4)
        yield
        q_s[:, 0:PW] = proj(P_Q)
        yield
        conv_cols(5)
        yield
        q_s[:, PW:2 * PW] = proj(P_Q + PW)
        yield
        conv_cols(6)
        yield
        kv = proj(P_K)
        k_s[...] = kv[:, 0:LANES]
        v_s[...] = kv[:, LANES:PW]
        yield
        conv_cols(7)
        yield
        dt_raw = jnp.dot(h, win_ref[:, P_DT:P_END], preferred_element_type=f32)
        dt_s[...] = jnp.where(lane < SSD_HEADS, _softplus(dt_raw + dtb_ref[...]), 0.0)
        tail = xbc_ext_s[:, tile:tile + SUBLANES, :]
        conv_ref[s, 0] = jnp.concatenate([tail[cb] for cb in range(CONV_BLOCKS)], axis=-1)
        xbc_ext_s[:, 0:SUBLANES, :] = tail
        yield

        GH = SSD_HEADS // SSD_GROUPS
        heads = range(ATT_HEADS)
        gs = range(SSD_GROUPS)
        zero_b = jnp.zeros((), bf16)

        yield
        def xbc_cols(n, lo, hi):
            return jnp.concatenate([xbc_s[cb, rows[n], :] for cb in range(lo // LANES, hi // LANES)],
                                   axis=-1)
        xs = [xbc_cols(n, 0, SSD_WIDTH) for n in ns]
        bm = [xbc_cols(n, SSD_WIDTH, SSD_WIDTH + half) for n in ns]
        cm = [xbc_cols(n, SSD_WIDTH + half, SSD_CONV_DIM) for n in ns]
        dtc = [dt_s[rows[n], :] for n in ns]
        cs = []
        for n in ns:
            adt = dtc[n] * a_row
            a_hi = adt.astype(bf16)
            a_r1 = adt - a_hi.astype(f32)
            a_mid = a_r1.astype(bf16)
            a_lo = (a_r1 - a_mid.astype(f32)).astype(bf16)
            cs.append(jnp.dot(tri3, jnp.concatenate([a_hi, a_mid, a_lo], axis=0),
                              preferred_element_type=f32))

        yield
        q = [q_s[rows[n], :].astype(bf16) for n in ns]
        k = [k_s[rows[n], :] for n in ns]
        v = [v_s[rows[n], :] for n in ns]
        k_prev = [kprev_s[...]] + k[:-1]
        v_prev = [vprev_s[...]] + v[:-1]
        kprev_s[...] = k[-1]
        vprev_s[...] = v[-1]
        first_bias = jnp.where(i > 0, 0.0, -jnp.inf)
        k_lo, k_hi, v_lo, v_hi = [], [], [], []
        for n in ns:
            kk = jnp.concatenate([k_prev[n], k[n]], axis=0)
            vv = jnp.concatenate([v_prev[n], v[n]], axis=0)
            kk_r = pltpu.roll(kk, ATT_HEAD_DIM, 1)
            vv_r = pltpu.roll(vv, ATT_HEAD_DIM, 1)
            k_lo.append([jnp.where(lo_half, kk, 0.0).astype(bf16), jnp.where(lo_half, kk_r, 0.0).astype(bf16)])
            k_hi.append([jnp.where(lo_half, 0.0, kk_r).astype(bf16), jnp.where(lo_half, 0.0, kk).astype(bf16)])
            v_lo.append([jnp.where(lo_half, vv, 0.0).astype(bf16), jnp.where(lo_half, vv_r, 0.0).astype(bf16)])
            v_hi.append([jnp.where(lo_half, 0.0, vv_r).astype(bf16), jnp.where(lo_half, 0.0, vv).astype(bf16)])
        s_g = [[_dot_nt(jnp.concatenate([q[n][:, (2 * g) * LANES:(2 * g + 1) * LANES],
                                         q[n][:, (2 * g + 1) * LANES:(2 * g + 2) * LANES]], axis=0),
                        jnp.concatenate([k_lo[n][g], k_hi[n][g]], axis=0))
                for g in range(ATT_KV_HEADS)] for n in ns]

        yield
        cs_t = [cs[n].T for n in ns]
        fac_e, cb_all, bm_t = [], [], []
        for n in ns:
            tot = cs[n][C - 1:C, :]
            fac = jnp.concatenate([dtc[n], jnp.exp(cs[n]), jnp.exp(tot - cs[n])], axis=0)
            f_hi = fac.astype(bf16)
            f_lo = (fac - f_hi.astype(f32)).astype(bf16)
            fac_e.append(jnp.dot(jnp.concatenate([f_hi, f_lo], axis=1), expand2,
                                 preferred_element_type=f32))
            cb_all.append(_dot_nt(
                jnp.concatenate([cm[n][:, 0:SSD_STATE], cm[n][:, SSD_STATE:half]], axis=0),
                jnp.concatenate([bm[n][:, 0:SSD_STATE], bm[n][:, SSD_STATE:half]], axis=0)))
            bm_t.append(bm[n].T)

        yield
        def head_scores(n, hd):
            g, jb, sub = hd // 4, (hd // 2) % 2, hd % 2
            s_prev = s_g[n][g][jb * C:(jb + 1) * C, (2 * sub) * C:(2 * sub + 1) * C]
            s_cur = s_g[n][g][jb * C:(jb + 1) * C, (2 * sub + 1) * C:(2 * sub + 2) * C]
            return jnp.where(lower, s_cur, s_prev + first_bias if n == 0 else s_prev)
        sc = [[head_scores(n, hd) for hd in heads] for n in ns]
        sink = [sinks_ref[hd] for hd in heads]
        m = [[jnp.maximum(jnp.max(sc[n][hd], axis=-1, keepdims=True), sink[hd]) for hd in heads]
             for n in ns]

        yield
        xdt = [xs[n] * fac_e[n][0:C] for n in ns]
        ecs_e = [fac_e[n][C:2 * C] for n in ns]
        w_end = [xdt[n] * fac_e[n][2 * C:3 * C] for n in ns]
        xdt_b = [xdt[n].astype(bf16) for n in ns]
        contrib = [[_dot(bm_t[n][g * SSD_STATE:(g + 1) * SSD_STATE, :],
                         w_end[n][:, g * half:(g + 1) * half]) for g in gs] for n in ns]
        decay = [[jnp.exp(jnp.where(lower, cs[n][:, hh:hh + 1] - cs_t[n][hh:hh + 1, :], -jnp.inf))
                  for hh in range(SSD_HEADS)] for n in ns]

        yield
        e = [[jnp.exp(sc[n][hd] - m[n][hd]) for hd in heads] for n in ns]
        den = [[jnp.sum(e[n][hd], axis=-1, keepdims=True) + jnp.exp(sink[hd] - m[n][hd])
                for hd in heads] for n in ns]

        yield
        y_d = []
        for n in ns:
            y_n = []
            for g in gs:
                gl = slice(g * half, (g + 1) * half)
                cb = cb_all[n][g * C:(g + 1) * C, g * C:(g + 1) * C]
                m_parts = [(cb * decay[n][g * GH + r]).astype(bf16) for r in range(GH)]
                x_parts = [jnp.where(head_of_lane == r, xdt_b[n][:, gl], zero_b) for r in range(GH)]
                y_n.append(jnp.dot(jnp.concatenate(m_parts, axis=1), jnp.concatenate(x_parts, axis=0),
                                   preferred_element_type=f32))
            y_d.append(y_n)
        st = [statet_s[g] for g in gs]
        y_off = []
        for n in ns:
            y_off.append([_dot(cm[n][:, g * SSD_STATE:(g + 1) * SSD_STATE], st[g]) for g in gs])
            st = [st[g] * ecs_e[n][C - 1:C, g * half:(g + 1) * half] + contrib[n][g] for g in gs]
        for g in gs:
            statet_s[g] = st[g]

        yield
        p = [[(e[n][hd] * (1.0 / den[n][hd])).astype(bf16) for hd in heads] for n in ns]
        o_g = []
        for n in ns:
            o_n = []
            for g in range(ATT_KV_HEADS):
                p_rows = []
                for jb in range(2):
                    p_cols = []
                    for sub in range(2):
                        ph = p[n][g * 4 + jb * 2 + sub]
                        p_cols += [jnp.where(lower, zero_b, ph), jnp.where(lower, ph, zero_b)]
                    p_rows.append(jnp.concatenate(p_cols, axis=1))
                o_n.append(jnp.dot(jnp.concatenate(p_rows, axis=0),
                                   jnp.concatenate([v_lo[n][g], v_hi[n][g]], axis=0),
                                   preferred_element_type=f32))
            o_g.append(o_n)

        yield
        for n in ns:
            y = (jnp.concatenate([y_d[n][g] + y_off[n][g] * ecs_e[n][:, g * half:(g + 1) * half]
                                  for g in gs], axis=-1)
                 + xs[n] * dskip_ref[...])
            y_ssd = _gated_group_norm(y, z_s[rows[n], :], gssd_ref[...])
            mix_s[rows[n], 0:SSD_WIDTH] = y_ssd.astype(bf16)
        for n in ns:
            for g in range(ATT_KV_HEADS):
                for jb in range(2):
                    lo_l = SSD_WIDTH + (2 * g + jb) * LANES
                    mix_s[rows[n], lo_l:lo_l + LANES] = o_g[n][g][jb * C:(jb + 1) * C].astype(bf16)

        yield

        mix_in = mix_s[...]
        mix = []
        for pc in range(D_MODEL // PW):
            mix.append(jnp.dot(mix_in, wout_ref[:, pc * PW:(pc + 1) * PW], preferred_element_type=f32))
            yield
        y_ref[s, 0] = x + _rmsnorm(jnp.concatenate(mix, axis=-1), gpost_ref[...])

    def mixed(ga, na, gb, nb):
        done_b = 0
        for ka in range(na):
            next(ga, None)
            want_b = ((ka + 1) * nb) // na
            for _ in range(want_b - done_b):
                next(gb, None)
            done_b = want_b

    g0, g1 = [stream(s) for s in range(STREAMS)]
    for _ in range(PROJ_PIECES):
        next(g0, None)
    mixed(g0, SCAN_PIECES, g1, PROJ_PIECES)
    mixed(g1, SCAN_PIECES, g0, OUT_PIECES)
    for g in (g0, g1):
        for _ in g:
            pass

    @pl.when(i == pl.num_programs(1) - 1)
    def _final_state():
        for s in range(STREAMS):
            for g in range(SSD_GROUPS):
                ssm_ref[s, 0, g * half:(g + 1) * half, :] = statet_sc[s, g].T
            wk_ref[s, 0] = k_sc[s, tile - WINDOW:tile, :].T
            wv_ref[s, 0] = v_sc[s, tile - WINDOW:tile, :].T


def _const_spec(shape):
    nd = len(shape)
    return pl.BlockSpec(shape, lambda *_: (0,) * nd)


def _prompt_mixer(x, gpre, win, convw, convb, dtb, alog, dskip_e, gssd, sinks, wout, gpost, tile):
    B, L, D = x.shape
    S = STREAMS
    G = B // S
    kern = functools.partial(_prompt_mixer_kernel, tile=tile)

    def per_seq(rows, width):
        return pl.BlockSpec((S, 1, rows, width), lambda b, i: (0, b, 0, 0))

    tile_spec = pl.BlockSpec((S, 1, tile, D), lambda b, i: (0, b, i, 0))
    out_shape = (
        jax.ShapeDtypeStruct((S, G, L, D), f32),
        jax.ShapeDtypeStruct((S, G, SSD_WIDTH, SSD_STATE), f32),
        jax.ShapeDtypeStruct((S, G, SUBLANES, SSD_CONV_DIM), f32),
        jax.ShapeDtypeStruct((S, G, ATT_KV_WIDTH, WINDOW), f32),
        jax.ShapeDtypeStruct((S, G, ATT_KV_WIDTH, WINDOW), f32),
    )
    in_specs = [
        tile_spec,
        _const_spec(gpre.shape), _const_spec(win.shape), _const_spec(convw.shape),
        _const_spec(convb.shape), _const_spec(dtb.shape), _const_spec(alog.shape),
        _const_spec(dskip_e.shape), _const_spec(gssd.shape),
        pl.BlockSpec(memory_space=pltpu.SMEM),
        _const_spec(wout.shape), _const_spec(gpost.shape),
    ]
    out_specs = (
        tile_spec,
        per_seq(SSD_WIDTH, SSD_STATE),
        per_seq(SUBLANES, SSD_CONV_DIM),
        per_seq(ATT_KV_WIDTH, WINDOW),
        per_seq(ATT_KV_WIDTH, WINDOW),
    )
    scratch = [
        pltpu.VMEM((S, SSD_GROUPS, SSD_STATE, SSD_WIDTH // SSD_GROUPS), f32),
        pltpu.VMEM((S, CONV_BLOCKS, tile + 2 * SUBLANES, LANES), f32),
        pltpu.VMEM((S, CONV_BLOCKS, tile, LANES), f32),
        pltpu.VMEM((S, tile, SSD_WIDTH), f32),
        pltpu.VMEM((S, tile, ATT_WIDTH), f32),
        pltpu.VMEM((S, tile, ATT_KV_WIDTH), f32),
        pltpu.VMEM((S, tile, ATT_KV_WIDTH), f32),
        pltpu.VMEM((S, tile, LANES), f32),
        pltpu.VMEM((S, WINDOW, ATT_KV_WIDTH), f32),
        pltpu.VMEM((S, WINDOW, ATT_KV_WIDTH), f32),
        pltpu.VMEM((S, tile, 2 * SSD_WIDTH), bf16),
    ]
    outs = pl.pallas_call(
        kern, grid=(G, L // tile), in_specs=in_specs, out_specs=out_specs, out_shape=out_shape,
        scratch_shapes=scratch, name="prompt_mixer",
        compiler_params=pltpu.CompilerParams(
            dimension_semantics=("arbitrary", "arbitrary"), vmem_limit_bytes=VMEM_LIMIT),
    )(x.reshape(S, G, L, D), gpre, win, convw, convb, dtb, alog, dskip_e, gssd, sinks, wout, gpost)
    return tuple(o.reshape(B, *o.shape[2:]) for o in outs)


def _memkv_kernel(m_ref, g_ref, wk_ref, wv_ref, k_ref, v_ref, kh_ref, vh_ref):
    mn = _rmsnorm(m_ref[...], g_ref[...]).astype(bf16)
    k = jnp.dot(mn, wk_ref[...], preferred_element_type=f32)
    v = jnp.dot(mn, wv_ref[...], preferred_element_type=f32)
    k_ref[...] = k
    v_ref[...] = v
    for hd in range(X_HEADS):
        kh_ref[:, hd, :] = k[:, hd * X_HEAD_DIM:(hd + 1) * X_HEAD_DIM]
        vh_ref[:, hd, :] = v[:, hd * X_HEAD_DIM:(hd + 1) * X_HEAD_DIM]


def _memkv(mem2d, g, wk, wv, tile):
    n, d = mem2d.shape
    row = pl.BlockSpec((tile, d), lambda i: (i, 0))
    hrow = pl.BlockSpec((tile, X_HEADS, X_HEAD_DIM), lambda i: (i, 0, 0))
    flat = jax.ShapeDtypeStruct((n, d), f32)
    heads = jax.ShapeDtypeStruct((n, X_HEADS, X_HEAD_DIM), f32)
    return pl.pallas_call(
        _memkv_kernel, grid=(n // tile,),
        in_specs=[row, _const_spec(g.shape), _const_spec(wk.shape), _const_spec(wv.shape)],
        out_specs=(row, row, hrow, hrow),
        out_shape=(flat, flat, heads, heads),
        name="memory_kv",
        compiler_params=pltpu.CompilerParams(
            dimension_semantics=("arbitrary",), vmem_limit_bytes=VMEM_LIMIT),
    )(mem2d, g, wk, wv)


def _prompt_xattn_kernel(x_ref, gpre_ref, wq_ref, mk_ref, mv_ref, wo_ref, gpost_ref,
                         wg_ref, wu_ref, wd_ref, y_ref, wg_out, wu_out, wd_out):
    hs = range(X_HEADS)
    sl = [slice(hd * X_HEAD_DIM, (hd + 1) * X_HEAD_DIM) for hd in hs]

    def stream(s):
        x = x_ref[s, 0]
        hn = _rmsnorm(x, gpre_ref[...]).astype(bf16)
        yield
        q = []
        for hd in hs:
            q.append(jnp.dot(hn, wq_ref[:, sl[hd]], preferred_element_type=f32))
            yield
        sc = [_dot_nt(q[hd], mk_ref[s, 0, :, sl[hd]]) for hd in hs]
        yield
        m = [jnp.max(sc[hd], axis=-1, keepdims=True) for hd in hs]
        e = [jnp.exp(sc[hd] - m[hd]) for hd in hs]
        yield
        r = [1.0 / jnp.sum(e[hd], axis=-1, keepdims=True) for hd in hs]
        p = [(e[hd] * r[hd]).astype(bf16) for hd in hs]
        yield
        o = jnp.concatenate([_dot(p[hd], mv_ref[s, 0, :, sl[hd]]) for hd in hs], axis=-1).astype(bf16)
        yield
        c = []
        for hd in hs:
            c.append(jnp.dot(o, wo_ref[:, sl[hd]], preferred_element_type=f32))
            yield
        y_ref[s, 0] = x + _rmsnorm(jnp.concatenate(c, axis=-1), gpost_ref[...])

    live = [stream(s) for s in range(STREAMS)]
    for _ in range(XATTN_LEAD):
        next(live[0], None)
    while live:
        live = [g for g in live if next(g, True) is None]

    wg_out[...] = wg_ref[...].astype(bf16)
    wu_out[...] = wu_ref[...].astype(bf16)
    wd_out[...] = wd_ref[...].astype(bf16)


def _prompt_xattn(x, gpre, wq, mk, mv, wo, gpost, ffn_w, tile):
    B, L, D = x.shape
    S = STREAMS
    G = B // S
    nt = L // tile
    xs = pl.BlockSpec((S, 1, tile, D), lambda b, i: (0, b, i, 0))
    ms = pl.BlockSpec((S, 1, N_MEM, D), lambda b, i: (0, b, 0, 0))

    def slab(w):
        return pl.BlockSpec((w.shape[0] // (G * nt), w.shape[1]), lambda b, i: (b * nt + i, 0))

    outs = pl.pallas_call(
        _prompt_xattn_kernel, grid=(G, nt),
        in_specs=[xs, _const_spec(gpre.shape), _const_spec(wq.shape), ms, ms,
                  _const_spec(wo.shape), _const_spec(gpost.shape)] + [slab(w) for w in ffn_w],
        out_specs=(xs,) + tuple(slab(w) for w in ffn_w),
        out_shape=(jax.ShapeDtypeStruct((S, G, L, D), f32),)
        + tuple(jax.ShapeDtypeStruct(w.shape, bf16) for w in ffn_w),
        name="prompt_xattn",
        compiler_params=pltpu.CompilerParams(
            dimension_semantics=("arbitrary", "arbitrary"), vmem_limit_bytes=VMEM_LIMIT),
    )(x.reshape(S, G, L, D), gpre, wq, mk.reshape(S, G, N_MEM, D), mv.reshape(S, G, N_MEM, D),
      wo, gpost, *ffn_w)
    return (outs[0].reshape(B, L, D),) + tuple(outs[1:])


def _ffn_kernel(xp_ref, xs_ref, gpre_ref, wg_ref, wu_ref, wd_ref, gpost_ref, yp_ref, ys_ref):
    i = pl.program_id(0)
    last = pl.num_programs(0) - 1
    x = jnp.where(i < last, xp_ref[...], xs_ref[...])
    hf = _rmsnorm(x, gpre_ref[...]).astype(bf16)
    gate = jnp.dot(hf, wg_ref[...], preferred_element_type=f32)
    up = jnp.dot(hf, wu_ref[...], preferred_element_type=f32)
    act = (_silu(gate) * up).astype(bf16)
    f = jnp.dot(act, wd_ref[...], preferred_element_type=f32)
    y = x + _rmsnorm(f, gpost_ref[...])

    @pl.when(i < last)
    def _prompt_rows():
        yp_ref[...] = y

    @pl.when(i == last)
    def _sample_rows():
        ys_ref[...] = y


def _ffn(xp2d, xs2d, gpre, wg, wu, wd, gpost, tile):
    n, d = xp2d.shape
    assert xs2d.shape == (tile, d)
    steps_p = n // tile
    prow = pl.BlockSpec((tile, d), lambda i: (jnp.minimum(i, steps_p - 1), 0))
    srow = pl.BlockSpec((tile, d), lambda i: (0, 0))
    return pl.pallas_call(
        _ffn_kernel, grid=(steps_p + 1,),
        in_specs=[prow, srow, _const_spec(gpre.shape), _const_spec(wg.shape), _const_spec(wu.shape),
                  _const_spec(wd.shape), _const_spec(gpost.shape)],
        out_specs=(prow, srow),
        out_shape=(jax.ShapeDtypeStruct((n, d), f32), jax.ShapeDtypeStruct((tile, d), f32)),
        name="ffn",
        compiler_params=pltpu.CompilerParams(
            dimension_semantics=("arbitrary",), vmem_limit_bytes=VMEM_LIMIT),
    )(xp2d, xs2d, gpre, wg, wu, wd, gpost)


def _pad_rows(a, rows):
    if a.shape[0] == rows:
        return a
    return jnp.concatenate([a, jnp.zeros((rows - a.shape[0], a.shape[1]), a.dtype)], axis=0)


def _sample_mixer_kernel(x_ref, cprev_ref, st_ref, ck_ref, cv_ref,
                         gpre_ref, win_ref, convw_ref, convb_ref, dtb_ref, alog_ref,
                         dskip_ref, gssd_ref, sinkcol_ref, wout_ref, gpost_ref,
                         y_ref, ssm_ref, cnew_ref, wk_ref, wv_ref, *, bt, steps):
    R = steps * bt
    half = SSD_WIDTH // SSD_GROUPS
    x = x_ref[...].reshape(R, D_MODEL)
    h = _rmsnorm(x, gpre_ref[...]).astype(bf16)
    z = jnp.dot(h, win_ref[:, P_Z:P_XBC], preferred_element_type=f32)
    u = jnp.dot(h, win_ref[:, P_XBC:P_Q], preferred_element_type=f32)
    q = jnp.dot(h, win_ref[:, P_Q:P_K], preferred_element_type=f32)
    k_new = jnp.dot(h, win_ref[:, P_K:P_V], preferred_element_type=f32)
    v_new = jnp.dot(h, win_ref[:, P_V:P_DT], preferred_element_type=f32)
    dt_raw = jnp.dot(h, win_ref[:, P_DT:P_END], preferred_element_type=f32)
    lane = lax.broadcasted_iota(jnp.int32, (1, LANES), 1)
    dt = jnp.where(lane < SSD_HEADS, _softplus(dt_raw + dtb_ref[...]), 0.0)

    def slab(a, t):
        return a[t * bt:(t + 1) * bt]

    HD = ATT_HEAD_DIM
    GH = ATT_HEADS // ATT_KV_HEADS
    GR = GH * R
    reps = GR // bt
    kvg = range(ATT_KV_HEADS)
    keep = WINDOW - steps

    kn_t = _pad_rows(k_new, LANES).T
    vn_t = _pad_rows(v_new, LANES).T

    ridx = lax.broadcasted_iota(jnp.int32, (GR, 1), 0)
    rb = ridx % bt
    rt = (ridx // bt) % steps
    qg = [jnp.concatenate([q[:, (g * GH + hl) * HD:(g * GH + hl + 1) * HD] for hl in range(GH)],
                          axis=0) for g in kvg]
    qg_b = [qg[g].astype(bf16) for g in kvg]
    s_cb = [[_dot(qg_b[g], ck_ref[b, g]) for b in range(bt)] for g in kvg]

    hist = [cprev_ref[j] for j in range(SSD_CONV - 1)] + [slab(u, t) for t in range(steps)]
    xbc_t = []
    for t in range(steps):
        acc = convb_ref[...]
        for j in range(SSD_CONV):
            acc = acc + hist[t + j] * convw_ref[j:j + 1, :]
        xbc_t.append(_silu(acc))
    for j in range(SSD_CONV - 1):
        cnew_ref[j] = hist[steps + j]
    xbc = jnp.concatenate(xbc_t, axis=0)
    xs = xbc[:, 0:SSD_WIDTH]
    bm = xbc[:, SSD_WIDTH:SSD_WIDTH + half]
    cm = xbc[:, SSD_WIDTH + half:SSD_CONV_DIM]
    a_row = -jnp.exp(alog_ref[...])
    adt = dt * a_row
    cs_t = [slab(adt, 0)]
    for t in range(1, steps):
        cs_t.append(cs_t[-1] + slab(adt, t))
    cs = jnp.concatenate(cs_t, axis=0)
    tot = cs_t[-1]
    tot_rows = jnp.concatenate([tot] * steps, axis=0)
    expand = _head_expand_matrix()
    expand2 = jnp.concatenate([expand, expand], axis=0)
    pairs = [(t, s2) for t in range(steps) for s2 in range(t)]
    fac = jnp.concatenate([dt, jnp.exp(cs), jnp.exp(tot_rows - cs), jnp.exp(tot)]
                          + [jnp.exp(cs_t[t] - cs_t[s2]) for t, s2 in pairs], axis=0)
    f_hi, f_lo = _split2(fac)
    fac_e = jnp.dot(jnp.concatenate([f_hi, f_lo], axis=1), expand2, preferred_element_type=f32)
    gr = lax.broadcasted_iota(jnp.int32, (half, SSD_WIDTH), 0)
    gc = lax.broadcasted_iota(jnp.int32, (half, SSD_WIDTH), 1)
    gsum = ((gr >> 7) == (gc >> 8)).astype(bf16)
    gsum2 = jnp.concatenate([gsum, gsum], axis=0)
    cb_pairs = [(t, s2) for t in range(steps) for s2 in range(t + 1)]
    prod = jnp.concatenate([slab(cm, t) * slab(bm, s2) for t, s2 in cb_pairs], axis=0)
    c_hi, c_lo = _split2(prod)
    cb_e = jnp.dot(jnp.concatenate([c_hi, c_lo], axis=1), gsum2, preferred_element_type=f32)

    sel_r = lax.broadcasted_iota(jnp.int32, (LANES, WINDOW), 0)
    sel_l = lax.broadcasted_iota(jnp.int32, (LANES, WINDOW), 1)
    sel = [((sel_r % bt == b) & (sel_r < R) & (sel_l - keep == sel_r // bt)).astype(bf16)
           for b in range(bt)]
    new_k = [_dot_x2(kn_t, sel[b]) for b in range(bt)]
    new_v = [_dot_x2(vn_t, sel[b]) for b in range(bt)]

    jcol = lax.broadcasted_iota(jnp.int32, (GR, WINDOW), 1)
    in_window = jcol > rt
    s_c, m, s_n = [], [], []
    for g in kvg:
        acc = jnp.zeros((GR, WINDOW), f32)
        for b in range(bt):
            acc = jnp.where(rb == b, s_cb[g][b], acc)
        s_c.append(jnp.where(in_window, acc, -jnp.inf))
    sink = [sinkcol_ref[g * GR:(g + 1) * GR, :] for g in kvg]
    for g in kvg:
        mg = jnp.maximum(jnp.max(s_c[g], axis=-1, keepdims=True), sink[g])
        sn_g = []
        for t2 in range(steps):
            kt = jnp.concatenate([slab(k_new, t2)[:, g * HD:(g + 1) * HD]] * reps, axis=0)
            sn = jnp.where(rt >= t2, jnp.sum(qg[g] * kt, axis=-1, keepdims=True), -jnp.inf)
            sn_g.append(sn)
            mg = jnp.maximum(mg, sn)
        m.append(mg)
        s_n.append(sn_g)
    e_c = [jnp.exp(s_c[g] - m[g]) for g in kvg]
    e_n = [[jnp.exp(sn - m[g]) for sn in s_n[g]] for g in kvg]
    rinv = []
    for g in kvg:
        den = jnp.sum(e_c[g], axis=-1, keepdims=True) + jnp.exp(sink[g] - m[g])
        for en in e_n[g]:
            den = den + en
        rinv.append(1.0 / den)
    p_c = [(e_c[g] * rinv[g]).astype(bf16) for g in kvg]

    xdt = xs * fac_e[0:R]
    ecs_e = fac_e[R:2 * R]
    w_end = xdt * fac_e[2 * R:3 * R]
    dec_e = fac_e[3 * R:3 * R + bt]
    pair_decay = {pr: fac_e[3 * R + (n + 1) * bt:3 * R + (n + 2) * bt] for n, pr in enumerate(pairs)}
    y_t = []
    for t in range(steps):
        acc = None
        for s2 in range(t + 1):
            n = cb_pairs.index((t, s2))
            coef = cb_e[n * bt:(n + 1) * bt]
            if s2 < t:
                coef = coef * pair_decay[(t, s2)]
            term = coef * slab(xdt, s2)
            acc = term if acc is None else acc + term
        y_t.append(acc)
    y_intra = jnp.concatenate(y_t, axis=0)
    b_idx = lax.broadcasted_iota(jnp.int32, (bt, 1, LANES), 0)
    l_idx = lax.broadcasted_iota(jnp.int32, (bt, 1, LANES), 2)
    pair = ((l_idx & (bt - 1)) == b_idx) & (l_idx < R)
    own = (l_idx == b_idx)
    gsl = [slice(g * half, (g + 1) * half) for g in range(SSD_GROUPS)]
    h0 = [st_ref[:, gsl[g], :] for g in range(SSD_GROUPS)]
    zz = [_dot_nt(h0[g].reshape(bt * half, SSD_STATE),
                  _pad_rows(cm[:, g * SSD_STATE:(g + 1) * SSD_STATE], LANES)).reshape(bt, half, LANES)
          for g in range(SSD_GROUPS)]
    wt = [_pad_rows(w_end[:, gsl[g]], LANES).T for g in range(SSD_GROUPS)]
    contrib = [_dot(jnp.where(pair, wt[g][None], 0.0).reshape(bt * half, LANES),
                    _pad_rows(bm[:, g * SSD_STATE:(g + 1) * SSD_STATE], LANES)
                    ).reshape(bt, half, SSD_STATE) for g in range(SSD_GROUPS)]
    dec_t = [_pad_rows(dec_e[:, gsl[g]], LANES).T for g in range(SSD_GROUPS)]

    o = []
    for g in kvg:
        pv = [_dot_nt(p_c[g], cv_ref[b, g]) for b in range(bt)]
        og = jnp.zeros((GR, HD), f32)
        for b in range(bt):
            og = jnp.where(rb == b, pv[b], og)
        for t2 in range(steps):
            vt = jnp.concatenate([slab(v_new, t2)[:, g * HD:(g + 1) * HD]] * reps, axis=0)
            og = og + (e_n[g][t2] * rinv[g]) * vt
        o.append(og)

    lane_w = lax.broadcasted_iota(jnp.int32, (1, WINDOW), 1)
    for b in range(bt):
        for g in kvg:
            gs = slice(g * HD, (g + 1) * HD)
            wk_ref[b, g] = jnp.where(lane_w < keep, pltpu.roll(ck_ref[b, g], keep, 1), new_k[b][gs])
            wv_ref[b, g] = jnp.where(lane_w < keep, pltpu.roll(cv_ref[b, g], keep, 1), new_v[b][gs])

    y_off_parts = []
    for g in range(SSD_GROUPS):
        yt = jnp.sum(jnp.where(pair, zz[g], 0.0), axis=0)
        y_off_parts.append(yt.T[0:R, :])
        dec = jnp.sum(jnp.where(own, dec_t[g][None], 0.0), axis=-1, keepdims=True)
        ssm_ref[:, gsl[g], :] = h0[g] * dec + contrib[g]
    y_off = jnp.concatenate(y_off_parts, axis=-1) * ecs_e
    y = y_intra + y_off + xs * dskip_ref[...]
    y_ssd = _gated_group_norm(y, z, gssd_ref[...])

    mix = jnp.dot(y_ssd.astype(bf16), wout_ref[0:SSD_WIDTH, :], preferred_element_type=f32)
    for g in kvg:
        for hl in range(GH):
            hd = g * GH + hl
            mix = mix + jnp.dot(o[g][hl * R:(hl + 1) * R].astype(bf16),
                                wout_ref[SSD_WIDTH + hd * HD:SSD_WIDTH + (hd + 1) * HD, :],
                                preferred_element_type=f32)
    y_ref[...] = (x + _rmsnorm(mix, gpost_ref[...])).reshape(steps, bt, D_MODEL)


def _sample_mixer(x_tm, cprev_tm, st, ck, cv, gpre, win, convw, convb, dtb, alog, dskip_e, gssd,
                  sinkcol, wout, gpost, bt):
    steps, nb, D = x_tm.shape
    kern = functools.partial(_sample_mixer_kernel, bt=bt, steps=steps)
    tm = lambda w: pl.BlockSpec((steps, bt, w), lambda i: (0, i, 0))
    win_spec = pl.BlockSpec((bt, ATT_KV_HEADS, ATT_HEAD_DIM, WINDOW), lambda i: (i, 0, 0, 0))
    in_specs = [
        tm(D),
        pl.BlockSpec((SSD_CONV - 1, bt, SSD_CONV_DIM), lambda i: (0, i, 0)),
        pl.BlockSpec((bt, SSD_WIDTH, SSD_STATE), lambda i: (i, 0, 0)),
        win_spec, win_spec,
    ] + [_const_spec(a.shape) for a in (gpre, win, convw, convb, dtb, alog, dskip_e, gssd,
                                        sinkcol, wout, gpost)]
    out_specs = (
        tm(D),
        pl.BlockSpec((bt, SSD_WIDTH, SSD_STATE), lambda i: (i, 0, 0)),
        pl.BlockSpec((SSD_CONV - 1, bt, SSD_CONV_DIM), lambda i: (0, i, 0)),
        win_spec, win_spec,
    )
    out_shape = (
        jax.ShapeDtypeStruct((steps, nb, D), f32),
        jax.ShapeDtypeStruct((nb, SSD_WIDTH, SSD_STATE), f32),
        jax.ShapeDtypeStruct((SSD_CONV - 1, nb, SSD_CONV_DIM), f32),
        jax.ShapeDtypeStruct(ck.shape, f32),
        jax.ShapeDtypeStruct(cv.shape, f32),
    )
    return pl.pallas_call(
        kern, grid=(nb // bt,), in_specs=in_specs, out_specs=out_specs, out_shape=out_shape,
        name="sample_mixer",
        compiler_params=pltpu.CompilerParams(
            dimension_semantics=("arbitrary",), vmem_limit_bytes=VMEM_LIMIT),
    )(x_tm, cprev_tm, st, ck, cv, gpre, win, convw, convb, dtb, alog, dskip_e, gssd, sinkcol,
      wout, gpost)


def _sample_xattn_kernel(x_ref, mk_ref, mv_ref, gpre_ref, wq_ref, wo_ref, gpost_ref, y_ref,
                         q_sc, o_sc, *, bt, steps):
    i = pl.program_id(0)
    nb = x_ref.shape[1]
    R = steps * bt
    nrow = bt * N_MEM * X_HEADS

    @pl.when(i == 0)
    def _project_queries():
        x_all = x_ref[...].reshape(steps * nb, D_MODEL)
        hn = _rmsnorm(x_all, gpre_ref[...]).astype(bf16)
        q_sc[...] = jnp.dot(hn, wq_ref[...], preferred_element_type=f32)

    tile_rows = [pl.ds(pl.multiple_of(t * nb + i * bt, bt), bt) for t in range(steps)]
    q = jnp.concatenate([q_sc[r, :] for r in tile_rows], axis=0)
    qs = jnp.concatenate([q[:, hd * X_HEAD_DIM:(hd + 1) * X_HEAD_DIM] for hd in range(X_HEADS)],
                         axis=0)
    kall = mk_ref[...].reshape(nrow, X_HEAD_DIM)
    vall = mv_ref[...].reshape(nrow, X_HEAD_DIM)
    ncol = X_HEADS * R
    seq_rows = N_MEM * X_HEADS
    z = _dot_nt(kall, qs).reshape(bt, seq_rows, ncol)
    v_b = [vall[b * seq_rows:(b + 1) * seq_rows].astype(bf16) for b in range(bt)]
    b_i = lax.broadcasted_iota(jnp.int32, (bt, 1, ncol), 0)
    c_i = lax.broadcasted_iota(jnp.int32, (bt, 1, ncol), 2)
    zc = jnp.sum(jnp.where(c_i % bt == b_i, z, 0.0), axis=0)
    zc = zc.reshape(seq_rows // SUBLANES, SUBLANES, ncol)
    r_h = lax.broadcasted_iota(jnp.int32, (1, SUBLANES, ncol), 1) % X_HEADS
    c_h = lax.broadcasted_iota(jnp.int32, (1, SUBLANES, ncol), 2) // R
    zc = jnp.where(r_h == c_h, zc, -jnp.inf).reshape(seq_rows, ncol)
    m = jnp.max(zc, axis=0, keepdims=True)
    e = jnp.exp(zc - m)
    p = e * (1.0 / jnp.sum(e, axis=0, keepdims=True))
    col_b = lax.broadcasted_iota(jnp.int32, (1, ncol), 1) % bt
    tn = (((0,), (0,)), ((), ()))
    o = None
    for b in range(bt):
        p_b = jnp.where(col_b == b, p, 0.0).astype(bf16)
        o_b = lax.dot_general(p_b, v_b[b], tn, preferred_element_type=f32)
        o = o_b if o is None else o + o_b
    o = jnp.concatenate([o[hd * R:(hd + 1) * R] for hd in range(X_HEADS)], axis=-1)
    for t in range(steps):
        o_sc[tile_rows[t], :] = o[t * bt:(t + 1) * bt]

    @pl.when(i == pl.num_programs(0) - 1)
    def _project_outputs():
        x_all = x_ref[...].reshape(steps * nb, D_MODEL)
        cc = _dot(o_sc[...], wo_ref[...])
        y_ref[...] = (x_all + _rmsnorm(cc, gpost_ref[...])).reshape(steps, nb, D_MODEL)


def _sample_xattn(x_tm, mk, mv, gpre, wq, wo, gpost, bt):
    steps, nb, D = x_tm.shape
    kern = functools.partial(_sample_xattn_kernel, bt=bt, steps=steps)
    xs = _const_spec(x_tm.shape)
    ms = pl.BlockSpec((bt, N_MEM, X_HEADS, X_HEAD_DIM), lambda i: (i, 0, 0, 0))
    return pl.pallas_call(
        kern, grid=(nb // bt,),
        in_specs=[xs, ms, ms, _const_spec(gpre.shape), _const_spec(wq.shape),
                  _const_spec(wo.shape), _const_spec(gpost.shape)],
        out_specs=xs, out_shape=jax.ShapeDtypeStruct((steps, nb, D), f32),
        scratch_shapes=[pltpu.VMEM((steps * nb, D), f32), pltpu.VMEM((steps * nb, D), f32)],
        name="sample_xattn",
        compiler_params=pltpu.CompilerParams(
            dimension_semantics=("arbitrary",), vmem_limit_bytes=VMEM_LIMIT),
    )(x_tm, mk, mv, gpre, wq, wo, gpost)


def _win_prep_kernel(wt_ref, o_ref):
    piece = 2 * LANES

    def put(src_lo, n, dst_lo, scale=None):
        for c in range(0, n, piece):
            t = wt_ref[src_lo + c:src_lo + c + piece, :].T
            if scale is not None:
                t = t * scale
            o_ref[:, dst_lo + c:dst_lo + c + piece] = t.astype(bf16)

    put(W_Z, P_Q - P_Z, P_Z)
    put(W_Q, P_K - P_Q, P_Q, ATT_SCALE)
    put(W_K, P_DT - P_K, P_K)
    dt_rows = jnp.concatenate([wt_ref[W_DT:W_DT + SSD_HEADS, :],
                               jnp.zeros((LANES - SSD_HEADS, D_MODEL), f32)], axis=0)
    o_ref[:, P_DT:P_END] = dt_rows.T.astype(bf16)


def _win_prep(w_t):
    return pl.pallas_call(
        _win_prep_kernel, out_shape=jax.ShapeDtypeStruct((D_MODEL, P_END), bf16),
        name="win_prep",
        compiler_params=pltpu.CompilerParams(vmem_limit_bytes=VMEM_LIMIT),
    )(w_t)


def _row(v, width=None):
    v = v.reshape(1, -1).astype(f32)
    if width is not None and v.shape[1] < width:
        v = jnp.pad(v, ((0, 0), (0, width - v.shape[1])))
    return v


def kernel(x_prompt, x_sample, state_ssm, state_conv, cache_win_k, cache_win_v, cache_mem_k, cache_mem_v, mem_prompt, g_mix_pre, w_in, conv_w, conv_b, dt_bias, a_log, d_skip, g_ssd_norm, sinks, w_out, g_mix_post, g_x_pre, w_xq, g_mem, w_xk, w_xv, w_xo, g_x_post, g_ffn_pre, w_gate, w_up, w_down, g_ffn_post):
    depth = w_in.shape[0]
    assert depth == 1
    B, L, D = x_prompt.shape
    NB, steps, _ = x_sample.shape
    li = 0

    win_p = _win_prep(jnp.transpose(w_in[li]))
    wo_b = w_out[li].astype(bf16)

    gpre, gpost = _row(g_mix_pre[li]), _row(g_mix_post[li])
    convw, convb = conv_w[li].astype(f32), _row(conv_b[li])
    dtb, alog = _row(dt_bias[li], LANES), _row(a_log[li], LANES)
    dskip_e = _row(jnp.repeat(d_skip[li], SSD_HEAD_DIM))
    gssd = _row(g_ssd_norm[li])
    sk = sinks[li].astype(f32)

    mk2d, mv2d, mk4, mv4 = _memkv(mem_prompt.reshape(B * N_MEM, D), _row(g_mem[li]),
                                  w_xk[li].astype(bf16), w_xv[li].astype(bf16), tile=512)
    mk3, mv3 = mk2d.reshape(B, N_MEM, D), mv2d.reshape(B, N_MEM, D)
    x1, p_ssm, p_conv8, p_wk, p_wv = _prompt_mixer(
        x_prompt, gpre, win_p, convw, convb, dtb, alog, dskip_e, gssd, sk, wo_b, gpost, tile=512)
    wxq_b, wxo_b = (w_xq[li] * X_SCALE).astype(bf16), w_xo[li].astype(bf16)
    gxpre, gxpost = _row(g_x_pre[li]), _row(g_x_post[li])
    x2, wg_b, wu_b, wd_b = _prompt_xattn(x1, gxpre, wxq_b, mk3, mv3, wxo_b, gxpost,
                                         (w_gate[li], w_up[li], w_down[li]), tile=512)
    gfpre, gfpost = _row(g_ffn_pre[li]), _row(g_ffn_post[li])

    bt, bt_mix = 8, 16
    x_tm = jnp.transpose(x_sample, (1, 0, 2))
    cprev_tm = jnp.transpose(state_conv[li], (1, 0, 2))
    st = state_ssm[li].reshape(NB, SSD_WIDTH, SSD_STATE)
    ck = jnp.transpose(cache_win_k[li], (0, 2, 3, 1))
    cv = jnp.transpose(cache_win_v[li], (0, 2, 3, 1))
    sinkcol = jnp.repeat(sk, steps * bt_mix).reshape(ATT_HEADS * steps * bt_mix, 1)
    x1s, s_ssm, cnew_tm, s_wk, s_wv = _sample_mixer(
        x_tm, cprev_tm, st, ck, cv, gpre, win_p, convw, convb, dtb, alog, dskip_e, gssd,
        sinkcol, wo_b, gpost, bt=bt_mix)
    cmk = cache_mem_k.reshape(NB, N_MEM, X_HEADS, X_HEAD_DIM)
    cmv = cache_mem_v.reshape(NB, N_MEM, X_HEADS, X_HEAD_DIM)
    x2s = _sample_xattn(x1s, cmk, cmv, gxpre, wxq_b, wxo_b, gxpost, bt=bt)
    yp2d, ys_tm = _ffn(x2.reshape(B * L, D), x2s.reshape(steps * NB, D), gfpre, wg_b, wu_b, wd_b,
                       gfpost, tile=steps * NB)
    yp = yp2d.reshape(B, L, D)
    ys = jnp.transpose(ys_tm.reshape(steps, NB, D), (1, 0, 2))

    s_conv = jnp.transpose(cnew_tm, (1, 0, 2))
    kv_shape = (ATT_KV_HEADS, ATT_HEAD_DIM)
    return (
        yp, ys,
        p_ssm.reshape(1, B, SSD_HEADS, SSD_HEAD_DIM, SSD_STATE),
        p_conv8[:, SUBLANES - (SSD_CONV - 1):, :][None],
        jnp.transpose(p_wk.reshape(B, *kv_shape, WINDOW), (0, 3, 1, 2))[None],
        jnp.transpose(p_wv.reshape(B, *kv_shape, WINDOW), (0, 3, 1, 2))[None],
        mk4.reshape(1, B, N_MEM, X_HEADS, X_HEAD_DIM), mv4.reshape(1, B, N_MEM, X_HEADS, X_HEAD_DIM),
        s_ssm.reshape(1, NB, SSD_HEADS, SSD_HEAD_DIM, SSD_STATE),
        s_conv[None],
        jnp.transpose(s_wk, (0, 3, 1, 2))[None], jnp.transpose(s_wv, (0, 3, 1, 2))[None],
    )
```

```python
import functools

import jax
import jax.numpy as jnp
from jax import lax
from jax.experimental import pallas as pl
from jax.experimental.pallas import tpu as pltpu

f32 = jnp.float32
bf16 = jnp.bfloat16

D_MODEL = 1024
EPS = 1e-6
N_MEM = 256
SSD_HEADS = 8
SSD_HEAD_DIM = 64
SSD_WIDTH = 512
SSD_GROUPS = 2
SSD_STATE = 128
SSD_CONV = 4
SSD_CHUNK = 128
SSD_CONV_DIM = 1024
ATT_HEADS = 8
ATT_KV_HEADS = 2
ATT_HEAD_DIM = 64
ATT_WIDTH = 512
ATT_KV_WIDTH = 128
WINDOW = 128
ATT_SCALE = ATT_HEAD_DIM ** -0.5
X_HEADS = 4
X_HEAD_DIM = 256
X_SCALE = X_HEAD_DIM ** -0.5
D_FF = 2816
LANES = 128
SUBLANES = 8
VMEM_LIMIT = 56 * 1024 * 1024
STREAMS = 2
PROJ_PIECES, SCAN_PIECES, OUT_PIECES = 19, 10, 5
CONV_BLOCKS = SSD_CONV_DIM // LANES
XATTN_LEAD = 5

P_Z, P_XBC, P_Q, P_K, P_V, P_DT, P_END = 0, 512, 1536, 2048, 2176, 2304, 2432
W_Z, W_DT, W_Q, W_K = 0, 1536, 1544, 2056


def _dot(a, b):
    return jnp.dot(a.astype(bf16), b.astype(bf16), preferred_element_type=f32)


def _dot_nt(a, b):
    return lax.dot_general(a.astype(bf16), b.astype(bf16), (((1,), (1,)), ((), ())),
                           preferred_element_type=f32)


def _split2(x):
    hi = x.astype(bf16)
    lo = (x - hi.astype(f32)).astype(bf16)
    return hi, lo


def _dot_x2(x, m):
    hi, lo = _split2(x)
    return (jnp.dot(hi, m, preferred_element_type=f32)
            + jnp.dot(lo, m, preferred_element_type=f32))


def _rmsnorm(x, g):
    ms = jnp.mean(x * x, axis=-1, keepdims=True)
    return x * lax.rsqrt(ms + EPS) * g


def _silu(x):
    return x * jax.nn.sigmoid(x)


def _softplus(x):
    return jnp.maximum(x, 0.0) + jnp.log1p(jnp.exp(-jnp.abs(x)))


def _head_expand_matrix():
    r = lax.broadcasted_iota(jnp.int32, (LANES, SSD_WIDTH), 0)
    c = lax.broadcasted_iota(jnp.int32, (LANES, SSD_WIDTH), 1)
    return (r == (c >> 6)).astype(bf16)


def _gated_group_norm(y, z, g):
    u = y * _silu(z)
    half = SSD_WIDTH // SSD_GROUPS
    parts = []
    for gi in range(SSD_GROUPS):
        ug = u[:, gi * half:(gi + 1) * half]
        parts.append(ug * lax.rsqrt(jnp.mean(ug * ug, axis=-1, keepdims=True) + EPS))
    return jnp.concatenate(parts, axis=-1) * g


def _prompt_mixer_kernel(x_ref, gpre_ref, win_ref, convw_ref, convb_ref, dtb_ref, alog_ref,
                         dskip_ref, gssd_ref, sinks_ref, wout_ref, gpost_ref,
                         y_ref, ssm_ref, conv_ref, wk_ref, wv_ref,
                         statet_sc, xbc_ext_sc, xbc_sc, z_sc, q_sc, k_sc, v_sc, dt_sc,
                         kprev_sc, vprev_sc, mix_sc, *, tile):
    i = pl.program_id(1)
    NC = tile // SSD_CHUNK
    ns = range(NC)
    C = SSD_CHUNK
    PW = 2 * LANES

    @pl.when(i == 0)
    def _init():
        statet_sc[...] = jnp.zeros_like(statet_sc)
        xbc_ext_sc[:, :, 0:SUBLANES, :] = jnp.zeros((STREAMS, CONV_BLOCKS, SUBLANES, LANES), f32)
        kprev_sc[...] = jnp.zeros_like(kprev_sc)
        vprev_sc[...] = jnp.zeros_like(vprev_sc)

    lane = lax.broadcasted_iota(jnp.int32, (1, LANES), 1)
    a_row = -jnp.exp(alog_ref[...])
    expand = _head_expand_matrix()
    expand2 = jnp.concatenate([expand, expand], axis=0)
    row_i = lax.broadcasted_iota(jnp.int32, (C, C), 0)
    col_i = lax.broadcasted_iota(jnp.int32, (C, C), 1)
    lower = col_i <= row_i
    tri = lower.astype(bf16)
    tri3 = jnp.concatenate([tri, tri, tri], axis=1)
    lo_half = lane < ATT_HEAD_DIM
    half = SSD_WIDTH // SSD_GROUPS
    head_of_lane = lax.broadcasted_iota(jnp.int32, (1, half), 1) >> 6
    rows = [slice(n * C, (n + 1) * C) for n in ns]

    def stream(s):
        statet_s, xbc_ext_s, xbc_s, z_s = statet_sc.at[s], xbc_ext_sc.at[s], xbc_sc.at[s], z_sc.at[s]
        q_s, k_s, v_s, dt_s = q_sc.at[s], k_sc.at[s], v_sc.at[s], dt_sc.at[s]
        kprev_s, vprev_s, mix_s = kprev_sc.at[s], vprev_sc.at[s], mix_sc.at[s]

        x = x_ref[s, 0]
        h = _rmsnorm(x, gpre_ref[...]).astype(bf16)
        yield

        def proj(col):
            return jnp.dot(h, win_ref[:, col:col + PW], preferred_element_type=f32)

        def conv_cols(cb):
            cols = slice(cb * LANES, (cb + 1) * LANES)
            acc = convb_ref[:, cols]
            for j in range(SSD_CONV):
                off = SUBLANES - (SSD_CONV - 1) + j
                acc = acc + xbc_ext_s[cb, off:off + tile, :] * convw_ref[j:j + 1, cols]
            xbc_s[cb] = _silu(acc)

        def xbc_piece(pc):
            res = proj(P_XBC + pc * PW)
            for half_pc in range(PW // LANES):
                xbc_ext_s[(PW // LANES) * pc + half_pc, SUBLANES:SUBLANES + tile, :] = (
                    res[:, half_pc * LANES:(half_pc + 1) * LANES])

        xbc_piece(0)
        yield
        xbc_piece(1)
        yield
        conv_cols(0)
        yield
        xbc_piece(2)
        yield
        conv_cols(1)
        yield
        xbc_piece(3)
        yield
        conv_cols(2)
        yield
        z_s[:, 0:PW] = proj(P_Z)
        yield
        conv_cols(3)
        yield
        z_s[:, PW:2 * PW] = proj(P_Z + PW)
        yield
        conv_cols(4)
        yield
        q_s[:, 0:PW] = proj(P_Q)
        yield
        conv_cols(5)
        yield
        q_s[:, PW:2 * PW] = proj(P_Q + PW)
        yield
        conv_cols(6)
        yield
        kv = proj(P_K)
        k_s[...] = kv[:, 0:LANES]
        v_s[...] = kv[:, LANES:PW]
        yield
        conv_cols(7)
        yield
        dt_raw = jnp.dot(h, win_ref[:, P_DT:P_END], preferred_element_type=f32)
        dt_s[...] = jnp.where(lane < SSD_HEADS, _softplus(dt_raw + dtb_ref[...]), 0.0)
        tail = xbc_ext_s[:, tile:tile + SUBLANES, :]
        conv_ref[s, 0] = jnp.concatenate([tail[cb] for cb in range(CONV_BLOCKS)], axis=-1)
        xbc_ext_s[:, 0:SUBLANES, :] = tail
        yield

        GH = SSD_HEADS // SSD_GROUPS
        heads = range(ATT_HEADS)
        gs = range(SSD_GROUPS)
        zero_b = jnp.zeros((), bf16)

        yield
        def xbc_cols(n, lo, hi):
            return jnp.concatenate([xbc_s[cb, rows[n], :] for cb in range(lo // LANES, hi // LANES)],
                                   axis=-1)
        xs = [xbc_cols(n, 0, SSD_WIDTH) for n in ns]
        bm = [xbc_cols(n, SSD_WIDTH, SSD_WIDTH + half) for n in ns]
        cm = [xbc_cols(n, SSD_WIDTH + half, SSD_CONV_DIM) for n in ns]
        dtc = [dt_s[rows[n], :] for n in ns]
        cs = []
        for n in ns:
            adt = dtc[n] * a_row
            a_hi = adt.astype(bf16)
            a_r1 = adt - a_hi.astype(f32)
            a_mid = a_r1.astype(bf16)
            a_lo = (a_r1 - a_mid.astype(f32)).astype(bf16)
            cs.append(jnp.dot(tri3, jnp.concatenate([a_hi, a_mid, a_lo], axis=0),
                              preferred_element_type=f32))

        yield
        q = [q_s[rows[n], :].astype(bf16) for n in ns]
        k = [k_s[rows[n], :] for n in ns]
        v = [v_s[rows[n], :] for n in ns]
        k_prev = [kprev_s[...]] + k[:-1]
        v_prev = [vprev_s[...]] + v[:-1]
        kprev_s[...] = k[-1]
        vprev_s[...] = v[-1]
        first_bias = jnp.where(i > 0, 0.0, -jnp.inf)
        k_lo, k_hi, v_lo, v_hi = [], [], [], []
        for n in ns:
            kk = jnp.concatenate([k_prev[n], k[n]], axis=0)
            vv = jnp.concatenate([v_prev[n], v[n]], axis=0)
            kk_r = pltpu.roll(kk, ATT_HEAD_DIM, 1)
            vv_r = pltpu.roll(vv, ATT_HEAD_DIM, 1)
            k_lo.append([jnp.where(lo_half, kk, 0.0).astype(bf16), jnp.where(lo_half, kk_r, 0.0).astype(bf16)])
            k_hi.append([jnp.where(lo_half, 0.0, kk_r).astype(bf16), jnp.where(lo_half, 0.0, kk).astype(bf16)])
            v_lo.append([jnp.where(lo_half, vv, 0.0).astype(bf16), jnp.where(lo_half, vv_r, 0.0).astype(bf16)])
            v_hi.append([jnp.where(lo_half, 0.0, vv_r).astype(bf16), jnp.where(lo_half, 0.0, vv).astype(bf16)])
        s_g = [[_dot_nt(jnp.concatenate([q[n][:, (2 * g) * LANES:(2 * g + 1) * LANES],
                                         q[n][:, (2 * g + 1) * LANES:(2 * g + 2) * LANES]], axis=0),
                        jnp.concatenate([k_lo[n][g], k_hi[n][g]], axis=0))
                for g in range(ATT_KV_HEADS)] for n in ns]

        yield
        cs_t = [cs[n].T for n in ns]
        fac_e, cb_all, bm_t = [], [], []
        for n in ns:
            tot = cs[n][C - 1:C, :]
            fac = jnp.concatenate([dtc[n], jnp.exp(cs[n]), jnp.exp(tot - cs[n])], axis=0)
            f_hi = fac.astype(bf16)
            f_lo = (fac - f_hi.astype(f32)).astype(bf16)
            fac_e.append(jnp.dot(jnp.concatenate([f_hi, f_lo], axis=1), expand2,
                                 preferred_element_type=f32))
            cb_all.append(_dot_nt(
                jnp.concatenate([cm[n][:, 0:SSD_STATE], cm[n][:, SSD_STATE:half]], axis=0),
                jnp.concatenate([bm[n][:, 0:SSD_STATE], bm[n][:, SSD_STATE:half]], axis=0)))
            bm_t.append(bm[n].T)

        yield
        def head_scores(n, hd):
            g, jb, sub = hd // 4, (hd // 2) % 2, hd % 2
            s_prev = s_g[n][g][jb * C:(jb + 1) * C, (2 * sub) * C:(2 * sub + 1) * C]
            s_cur = s_g[n][g][jb * C:(jb + 1) * C, (2 * sub + 1) * C:(2 * sub + 2) * C]
            return jnp.where(lower, s_cur, s_prev + first_bias if n == 0 else s_prev)
        sc = [[head_scores(n, hd) for hd in heads] for n in ns]
        sink = [sinks_ref[hd] for hd in heads]
        m = [[jnp.maximum(jnp.max(sc[n][hd], axis=-1, keepdims=True), sink[hd]) for hd in heads]
             for n in ns]

        yield
        xdt = [xs[n] * fac_e[n][0:C] for n in ns]
        ecs_e = [fac_e[n][C:2 * C] for n in ns]
        w_end = [xdt[n] * fac_e[n][2 * C:3 * C] for n in ns]
        xdt_b = [xdt[n].astype(bf16) for n in ns]
        contrib = [[_dot(bm_t[n][g * SSD_STATE:(g + 1) * SSD_STATE, :],
                         w_end[n][:, g * half:(g + 1) * half]) for g in gs] for n in ns]
        decay = [[jnp.exp(jnp.where(lower, cs[n][:, hh:hh + 1] - cs_t[n][hh:hh + 1, :], -jnp.inf))
                  for hh in range(SSD_HEADS)] for n in ns]

        yield
        e = [[jnp.exp(sc[n][hd] - m[n][hd]) for hd in heads] for n in ns]
        den = [[jnp.sum(e[n][hd], axis=-1, keepdims=True) + jnp.exp(sink[hd] - m[n][hd])
                for hd in heads] for n in ns]

        yield
        y_d = []
        for n in ns:
            y_n = []
            for g in gs:
                gl = slice(g * half, (g + 1) * half)
                cb = cb_all[n][g * C:(g + 1) * C, g * C:(g + 1) * C]
                m_parts = [(cb * decay[n][g * GH + r]).astype(bf16) for r in range(GH)]
                x_parts = [jnp.where(head_of_lane == r, xdt_b[n][:, gl], zero_b) for r in range(GH)]
                y_n.append(jnp.dot(jnp.concatenate(m_parts, axis=1), jnp.concatenate(x_parts, axis=0),
                                   preferred_element_type=f32))
            y_d.append(y_n)
        st = [statet_s[g] for g in gs]
        y_off = []
        for n in ns:
            y_off.append([_dot(cm[n][:, g * SSD_STATE:(g + 1) * SSD_STATE], st[g]) for g in gs])
            st = [st[g] * ecs_e[n][C - 1:C, g * half:(g + 1) * half] + contrib[n][g] for g in gs]
        for g in gs:
            statet_s[g] = st[g]

        yield
        p = [[(e[n][hd] * (1.0 / den[n][hd])).astype(bf16) for hd in heads] for n in ns]
        o_g = []
        for n in ns:
            o_n = []
            for g in range(ATT_KV_HEADS):
                p_rows = []
                for jb in range(2):
                    p_cols = []
                    for sub in range(2):
                        ph = p[n][g * 4 + jb * 2 + sub]
                        p_cols += [jnp.where(lower, zero_b, ph), jnp.where(lower, ph, zero_b)]
                    p_rows.append(jnp.concatenate(p_cols, axis=1))
                o_n.append(jnp.dot(jnp.concatenate(p_rows, axis=0),
                                   jnp.concatenate([v_lo[n][g], v_hi[n][g]], axis=0),
                                   preferred_element_type=f32))
            o_g.append(o_n)

        yield
        for n in ns:
            y = (jnp.concatenate([y_d[n][g] + y_off[n][g] * ecs_e[n][:, g * half:(g + 1) * half]
                                  for g in gs], axis=-1)
                 + xs[n] * dskip_ref[...])
            y_ssd = _gated_group_norm(y, z_s[rows[n], :], gssd_ref[...])
            mix_s[rows[n], 0:SSD_WIDTH] = y_ssd.astype(bf16)
        for n in ns:
            for g in range(ATT_KV_HEADS):
                for jb in range(2):
                    lo_l = SSD_WIDTH + (2 * g + jb) * LANES
                    mix_s[rows[n], lo_l:lo_l + LANES] = o_g[n][g][jb * C:(jb + 1) * C].astype(bf16)

        yield

        mix_in = mix_s[...]
        mix = []
        for pc in range(D_MODEL // PW):
            mix.append(jnp.dot(mix_in, wout_ref[:, pc * PW:(pc + 1) * PW], preferred_element_type=f32))
            yield
        y_ref[s, 0] = x + _rmsnorm(jnp.concatenate(mix, axis=-1), gpost_ref[...])

    def mixed(ga, na, gb, nb):
        done_b = 0
        for ka in range(na):
            next(ga, None)
            want_b = ((ka + 1) * nb) // na
            for _ in range(want_b - done_b):
                next(gb, None)
            done_b = want_b

    g0, g1 = [stream(s) for s in range(STREAMS)]
    for _ in range(PROJ_PIECES):
        next(g0, None)
    mixed(g0, SCAN_PIECES, g1, PROJ_PIECES)
    mixed(g1, SCAN_PIECES, g0, OUT_PIECES)
    for g in (g0, g1):
        for _ in g:
            pass

    @pl.when(i == pl.num_programs(1) - 1)
    def _final_state():
        for s in range(STREAMS):
            for g in range(SSD_GROUPS):
                ssm_ref[s, 0, g * half:(g + 1) * half, :] = statet_sc[s, g].T
            wk_ref[s, 0] = k_sc[s, tile - WINDOW:tile, :].T
            wv_ref[s, 0] = v_sc[s, tile - WINDOW:tile, :].T


def _const_spec(shape):
    nd = len(shape)
    return pl.BlockSpec(shape, lambda *_: (0,) * nd)


def _prompt_mixer(x, gpre, win, convw, convb, dtb, alog, dskip_e, gssd, sinks, wout, gpost, tile):
    B, L, D = x.shape
    S = STREAMS
    G = B // S
    kern = functools.partial(_prompt_mixer_kernel, tile=tile)

    def per_seq(rows, width):
        return pl.BlockSpec((S, 1, rows, width), lambda b, i: (0, b, 0, 0))

    tile_spec = pl.BlockSpec((S, 1, tile, D), lambda b, i: (0, b, i, 0))
    out_shape = (
        jax.ShapeDtypeStruct((S, G, L, D), f32),
        jax.ShapeDtypeStruct((S, G, SSD_WIDTH, SSD_STATE), f32),
        jax.ShapeDtypeStruct((S, G, SUBLANES, SSD_CONV_DIM), f32),
        jax.ShapeDtypeStruct((S, G, ATT_KV_WIDTH, WINDOW), f32),
        jax.ShapeDtypeStruct((S, G, ATT_KV_WIDTH, WINDOW), f32),
    )
    in_specs = [
        tile_spec,
        _const_spec(gpre.shape), _const_spec(win.shape), _const_spec(convw.shape),
        _const_spec(convb.shape), _const_spec(dtb.shape), _const_spec(alog.shape),
        _const_spec(dskip_e.shape), _const_spec(gssd.shape),
        pl.BlockSpec(memory_space=pltpu.SMEM),
        _const_spec(wout.shape), _const_spec(gpost.shape),
    ]
    out_specs = (
        tile_spec,
        per_seq(SSD_WIDTH, SSD_STATE),
        per_seq(SUBLANES, SSD_CONV_DIM),
        per_seq(ATT_KV_WIDTH, WINDOW),
        per_seq(ATT_KV_WIDTH, WINDOW),
    )
    scratch = [
        pltpu.VMEM((S, SSD_GROUPS, SSD_STATE, SSD_WIDTH // SSD_GROUPS), f32),
        pltpu.VMEM((S, CONV_BLOCKS, tile + 2 * SUBLANES, LANES), f32),
        pltpu.VMEM((S, CONV_BLOCKS, tile, LANES), f32),
        pltpu.VMEM((S, tile, SSD_WIDTH), f32),
        pltpu.VMEM((S, tile, ATT_WIDTH), f32),
        pltpu.VMEM((S, tile, ATT_KV_WIDTH), f32),
        pltpu.VMEM((S, tile, ATT_KV_WIDTH), f32),
        pltpu.VMEM((S, tile, LANES), f32),
        pltpu.VMEM((S, WINDOW, ATT_KV_WIDTH), f32),
        pltpu.VMEM((S, WINDOW, ATT_KV_WIDTH), f32),
        pltpu.VMEM((S, tile, 2 * SSD_WIDTH), bf16),
    ]
    outs = pl.pallas_call(
        kern, grid=(G, L // tile), in_specs=in_specs, out_specs=out_specs, out_shape=out_shape,
        scratch_shapes=scratch, name="prompt_mixer",
        compiler_params=pltpu.CompilerParams(
            dimension_semantics=("arbitrary", "arbitrary"), vmem_limit_bytes=VMEM_LIMIT),
    )(x.reshape(S, G, L, D), gpre, win, convw, convb, dtb, alog, dskip_e, gssd, sinks, wout, gpost)
    return tuple(o.reshape(B, *o.shape[2:]) for o in outs)


def _memkv_kernel(m_ref, g_ref, wk_ref, wv_ref, k_ref, v_ref, kh_ref, vh_ref):
    mn = _rmsnorm(m_ref[...], g_ref[...]).astype(bf16)
    k = jnp.dot(mn, wk_ref[...], preferred_element_type=f32)
    v = jnp.dot(mn, wv_ref[...], preferred_element_type=f32)
    k_ref[...] = k.astype(bf16)
    v_ref[...] = v.astype(bf16)
    for hd in range(X_HEADS):
        kh_ref[:, hd, :] = k[:, hd * X_HEAD_DIM:(hd + 1) * X_HEAD_DIM]
        vh_ref[:, hd, :] = v[:, hd * X_HEAD_DIM:(hd + 1) * X_HEAD_DIM]


def _memkv(mem2d, g, wk, wv, tile):
    n, d = mem2d.shape
    row = pl.BlockSpec((tile, d), lambda i: (i, 0))
    hrow = pl.BlockSpec((tile, X_HEADS, X_HEAD_DIM), lambda i: (i, 0, 0))
    flat = jax.ShapeDtypeStruct((n, d), bf16)
    heads = jax.ShapeDtypeStruct((n, X_HEADS, X_HEAD_DIM), f32)
    return pl.pallas_call(
        _memkv_kernel, grid=(n // tile,),
        in_specs=[row, _const_spec(g.shape), _const_spec(wk.shape), _const_spec(wv.shape)],
        out_specs=(row, row, hrow, hrow),
        out_shape=(flat, flat, heads, heads),
        name="memory_kv",
        compiler_params=pltpu.CompilerParams(
            dimension_semantics=("arbitrary",), vmem_limit_bytes=VMEM_LIMIT),
    )(mem2d, g, wk, wv)


def _prompt_xattn_kernel(x_ref, gpre_ref, wq_ref, mk_ref, mv_ref, wo_ref, gpost_ref,
                         wg_ref, wu_ref, wd_ref, y_ref, wg_out, wu_out, wd_out):
    hs = range(X_HEADS)
    sl = [slice(hd * X_HEAD_DIM, (hd + 1) * X_HEAD_DIM) for hd in hs]

    def stream(s):
        x = x_ref[s, 0]
        hn = _rmsnorm(x, gpre_ref[...]).astype(bf16)
        yield
        q = []
        for hd in hs:
            q.append(jnp.dot(hn, wq_ref[:, sl[hd]], preferred_element_type=f32))
            yield
        sc = [_dot_nt(q[hd], mk_ref[s, 0, :, sl[hd]]) for hd in hs]
        yield
        m = [jnp.max(sc[hd], axis=-1, keepdims=True) for hd in hs]
        e = [jnp.exp(sc[hd] - m[hd]) for hd in hs]
        yield
        r = [1.0 / jnp.sum(e[hd], axis=-1, keepdims=True) for hd in hs]
        p = [(e[hd] * r[hd]).astype(bf16) for hd in hs]
        yield
        o = jnp.concatenate([_dot(p[hd], mv_ref[s, 0, :, sl[hd]]) for hd in hs], axis=-1).astype(bf16)
        yield
        c = []
        for hd in hs:
            c.append(jnp.dot(o, wo_ref[:, sl[hd]], preferred_element_type=f32))
            yield
        y_ref[s, 0] = x + _rmsnorm(jnp.concatenate(c, axis=-1), gpost_ref[...])

    live = [stream(s) for s in range(STREAMS)]
    for _ in range(XATTN_LEAD):
        next(live[0], None)
    while live:
        live = [g for g in live if next(g, True) is None]

    wg_out[...] = wg_ref[...].astype(bf16)
    wu_out[...] = wu_ref[...].astype(bf16)
    wd_out[...] = wd_ref[...].astype(bf16)


def _prompt_xattn(x, gpre, wq, mk, mv, wo, gpost, ffn_w, tile):
    B, L, D = x.shape
    S = STREAMS
    G = B // S
    nt = L // tile
    xs = pl.BlockSpec((S, 1, tile, D), lambda b, i: (0, b, i, 0))
    ms = pl.BlockSpec((S, 1, N_MEM, D), lambda b, i: (0, b, 0, 0))

    def slab(w):
        return pl.BlockSpec((w.shape[0] // (G * nt), w.shape[1]), lambda b, i: (b * nt + i, 0))

    outs = pl.pallas_call(
        _prompt_xattn_kernel, grid=(G, nt),
        in_specs=[xs, _const_spec(gpre.shape), _const_spec(wq.shape), ms, ms,
                  _const_spec(wo.shape), _const_spec(gpost.shape)] + [slab(w) for w in ffn_w],
        out_specs=(xs,) + tuple(slab(w) for w in ffn_w),
        out_shape=(jax.ShapeDtypeStruct((S, G, L, D), f32),)
        + tuple(jax.ShapeDtypeStruct(w.shape, bf16) for w in ffn_w),
        name="prompt_xattn",
        compiler_params=pltpu.CompilerParams(
            dimension_semantics=("arbitrary", "arbitrary"), vmem_limit_bytes=VMEM_LIMIT),
    )(x.reshape(S, G, L, D), gpre, wq, mk.reshape(S, G, N_MEM, D), mv.reshape(S, G, N_MEM, D),
      wo, gpost, *ffn_w)
    return (outs[0].reshape(B, L, D),) + tuple(outs[1:])


def _ffn_kernel(xp_ref, xs_ref, gpre_ref, wg_ref, wu_ref, wd_ref, gpost_ref, yp_ref, ys_ref):
    def rows(x_ref, y_ref):
        x = x_ref[...]
        hf = _rmsnorm(x, gpre_ref[...]).astype(bf16)
        gate = jnp.dot(hf, wg_ref[...], preferred_element_type=f32)
        up = jnp.dot(hf, wu_ref[...], preferred_element_type=f32)
        act = (_silu(gate) * up).astype(bf16)
        f = jnp.dot(act, wd_ref[...], preferred_element_type=f32)
        y_ref[...] = x + _rmsnorm(f, gpost_ref[...])

    i = pl.program_id(0)
    last = pl.num_programs(0) - 1

    @pl.when(i < last)
    def _prompt_rows():
        rows(xp_ref, yp_ref)

    @pl.when(i == last)
    def _sample_rows():
        rows(xs_ref, ys_ref)


def _ffn(xp2d, xs2d, gpre, wg, wu, wd, gpost, tile):
    n, d = xp2d.shape
    assert xs2d.shape == (tile, d)
    steps_p = n // tile
    prow = pl.BlockSpec((tile, d), lambda i: (jnp.minimum(i, steps_p - 1), 0))
    srow = pl.BlockSpec((tile, d), lambda i: (0, 0))
    return pl.pallas_call(
        _ffn_kernel, grid=(steps_p + 1,),
        in_specs=[prow, srow, _const_spec(gpre.shape), _const_spec(wg.shape), _const_spec(wu.shape),
                  _const_spec(wd.shape), _const_spec(gpost.shape)],
        out_specs=(prow, srow),
        out_shape=(jax.ShapeDtypeStruct((n, d), f32), jax.ShapeDtypeStruct((tile, d), f32)),
        name="ffn",
        compiler_params=pltpu.CompilerParams(
            dimension_semantics=("arbitrary",), vmem_limit_bytes=VMEM_LIMIT),
    )(xp2d, xs2d, gpre, wg, wu, wd, gpost)


def _pad_rows(a, rows):
    if a.shape[0] == rows:
        return a
    return jnp.concatenate([a, jnp.zeros((rows - a.shape[0], a.shape[1]), a.dtype)], axis=0)


def _sample_mixer_kernel(x_ref, cprev_ref, st_ref, ck_ref, cv_ref,
                         gpre_ref, win_ref, convw_ref, convb_ref, dtb_ref, alog_ref,
                         dskip_ref, gssd_ref, sinkcol_ref, wout_ref, gpost_ref, wxq_ref, wxo_ref,
                         y_ref, ssm_ref, cnew_ref, wk_ref, wv_ref, wxq_out, wxo_out, *, bt, steps):
    R = steps * bt
    half = SSD_WIDTH // SSD_GROUPS
    x = x_ref[...].reshape(R, D_MODEL)
    h = _rmsnorm(x, gpre_ref[...]).astype(bf16)
    z = jnp.dot(h, win_ref[:, P_Z:P_XBC], preferred_element_type=f32)
    u = jnp.dot(h, win_ref[:, P_XBC:P_Q], preferred_element_type=f32)
    q = jnp.dot(h, win_ref[:, P_Q:P_K], preferred_element_type=f32)
    k_new = jnp.dot(h, win_ref[:, P_K:P_V], preferred_element_type=f32)
    v_new = jnp.dot(h, win_ref[:, P_V:P_DT], preferred_element_type=f32)
    dt_raw = jnp.dot(h, win_ref[:, P_DT:P_END], preferred_element_type=f32)
    lane = lax.broadcasted_iota(jnp.int32, (1, LANES), 1)
    dt = jnp.where(lane < SSD_HEADS, _softplus(dt_raw + dtb_ref[...]), 0.0)

    def slab(a, t):
        return a[t * bt:(t + 1) * bt]

    HD = ATT_HEAD_DIM
    GH = ATT_HEADS // ATT_KV_HEADS
    GR = GH * R
    reps = GR // bt
    kvg = range(ATT_KV_HEADS)
    keep = WINDOW - steps

    kn_t = _pad_rows(k_new, LANES).T
    vn_t = _pad_rows(v_new, LANES).T

    ridx = lax.broadcasted_iota(jnp.int32, (GR, 1), 0)
    rb = ridx % bt
    rt = (ridx // bt) % steps
    qg = [jnp.concatenate([q[:, (g * GH + hl) * HD:(g * GH + hl + 1) * HD] for hl in range(GH)],
                          axis=0) for g in kvg]
    qg_b = [qg[g].astype(bf16) for g in kvg]
    s_cb = [[_dot(qg_b[g], ck_ref[b, g]) for b in range(bt)] for g in kvg]

    hist = [cprev_ref[j] for j in range(SSD_CONV - 1)] + [slab(u, t) for t in range(steps)]
    xbc_t = []
    for t in range(steps):
        acc = convb_ref[...]
        for j in range(SSD_CONV):
            acc = acc + hist[t + j] * convw_ref[j:j + 1, :]
        xbc_t.append(_silu(acc))
    for j in range(SSD_CONV - 1):
        cnew_ref[j] = hist[steps + j]
    xbc = jnp.concatenate(xbc_t, axis=0)
    xs = xbc[:, 0:SSD_WIDTH]
    bm = xbc[:, SSD_WIDTH:SSD_WIDTH + half]
    cm = xbc[:, SSD_WIDTH + half:SSD_CONV_DIM]
    a_row = -jnp.exp(alog_ref[...])
    adt = dt * a_row
    cs_t = [slab(adt, 0)]
    for t in range(1, steps):
        cs_t.append(cs_t[-1] + slab(adt, t))
    cs = jnp.concatenate(cs_t, axis=0)
    tot = cs_t[-1]
    tot_rows = jnp.concatenate([tot] * steps, axis=0)
    expand = _head_expand_matrix()
    expand2 = jnp.concatenate([expand, expand], axis=0)
    pairs = [(t, s2) for t in range(steps) for s2 in range(t)]
    fac = jnp.concatenate([dt, jnp.exp(cs), jnp.exp(tot_rows - cs), jnp.exp(tot)]
                          + [jnp.exp(cs_t[t] - cs_t[s2]) for t, s2 in pairs], axis=0)
    f_hi, f_lo = _split2(fac)
    fac_e = jnp.dot(jnp.concatenate([f_hi, f_lo], axis=1), expand2, preferred_element_type=f32)
    gr = lax.broadcasted_iota(jnp.int32, (half, SSD_WIDTH), 0)
    gc = lax.broadcasted_iota(jnp.int32, (half, SSD_WIDTH), 1)
    gsum = ((gr >> 7) == (gc >> 8)).astype(bf16)
    gsum2 = jnp.concatenate([gsum, gsum], axis=0)
    cb_pairs = [(t, s2) for t in range(steps) for s2 in range(t + 1)]
    prod = jnp.concatenate([slab(cm, t) * slab(bm, s2) for t, s2 in cb_pairs], axis=0)
    c_hi, c_lo = _split2(prod)
    cb_e = jnp.dot(jnp.concatenate([c_hi, c_lo], axis=1), gsum2, preferred_element_type=f32)

    sel_r = lax.broadcasted_iota(jnp.int32, (LANES, WINDOW), 0)
    sel_l = lax.broadcasted_iota(jnp.int32, (LANES, WINDOW), 1)
    sel = [((sel_r % bt == b) & (sel_r < R) & (sel_l - keep == sel_r // bt)).astype(bf16)
           for b in range(bt)]
    new_k = [_dot_x2(kn_t, sel[b]) for b in range(bt)]
    new_v = [_dot_x2(vn_t, sel[b]) for b in range(bt)]

    jcol = lax.broadcasted_iota(jnp.int32, (GR, WINDOW), 1)
    in_window = jcol > rt
    s_c, m, s_n = [], [], []
    for g in kvg:
        acc = jnp.zeros((GR, WINDOW), f32)
        for b in range(bt):
            acc = jnp.where(rb == b, s_cb[g][b], acc)
        s_c.append(jnp.where(in_window, acc, -jnp.inf))
    sink = [sinkcol_ref[g * GR:(g + 1) * GR, :] for g in kvg]
    for g in kvg:
        mg = jnp.maximum(jnp.max(s_c[g], axis=-1, keepdims=True), sink[g])
        sn_g = []
        for t2 in range(steps):
            kt = jnp.concatenate([slab(k_new, t2)[:, g * HD:(g + 1) * HD]] * reps, axis=0)
            sn = jnp.where(rt >= t2, jnp.sum(qg[g] * kt, axis=-1, keepdims=True), -jnp.inf)
            sn_g.append(sn)
            mg = jnp.maximum(mg, sn)
        m.append(mg)
        s_n.append(sn_g)
    e_c = [jnp.exp(s_c[g] - m[g]) for g in kvg]
    e_n = [[jnp.exp(sn - m[g]) for sn in s_n[g]] for g in kvg]
    rinv = []
    for g in kvg:
        den = jnp.sum(e_c[g], axis=-1, keepdims=True) + jnp.exp(sink[g] - m[g])
        for en in e_n[g]:
            den = den + en
        rinv.append(1.0 / den)
    p_c = [(e_c[g] * rinv[g]).astype(bf16) for g in kvg]

    xdt = xs * fac_e[0:R]
    ecs_e = fac_e[R:2 * R]
    w_end = xdt * fac_e[2 * R:3 * R]
    dec_e = fac_e[3 * R:3 * R + bt]
    pair_decay = {pr: fac_e[3 * R + (n + 1) * bt:3 * R + (n + 2) * bt] for n, pr in enumerate(pairs)}
    y_t = []
    for t in range(steps):
        acc = None
        for s2 in range(t + 1):
            n = cb_pairs.index((t, s2))
            coef = cb_e[n * bt:(n + 1) * bt]
            if s2 < t:
                coef = coef * pair_decay[(t, s2)]
            term = coef * slab(xdt, s2)
            acc = term if acc is None else acc + term
        y_t.append(acc)
    y_intra = jnp.concatenate(y_t, axis=0)
    b_idx = lax.broadcasted_iota(jnp.int32, (bt, 1, LANES), 0)
    l_idx = lax.broadcasted_iota(jnp.int32, (bt, 1, LANES), 2)
    pair = ((l_idx & (bt - 1)) == b_idx) & (l_idx < R)
    own = (l_idx == b_idx)
    gsl = [slice(g * half, (g + 1) * half) for g in range(SSD_GROUPS)]
    h0 = [st_ref[:, gsl[g], :] for g in range(SSD_GROUPS)]
    zz = [_dot_nt(h0[g].reshape(bt * half, SSD_STATE),
                  _pad_rows(cm[:, g * SSD_STATE:(g + 1) * SSD_STATE], LANES)).reshape(bt, half, LANES)
          for g in range(SSD_GROUPS)]
    wt = [_pad_rows(w_end[:, gsl[g]], LANES).T for g in range(SSD_GROUPS)]
    contrib = [_dot(jnp.where(pair, wt[g][None], 0.0).reshape(bt * half, LANES),
                    _pad_rows(bm[:, g * SSD_STATE:(g + 1) * SSD_STATE], LANES)
                    ).reshape(bt, half, SSD_STATE) for g in range(SSD_GROUPS)]
    dec_t = [_pad_rows(dec_e[:, gsl[g]], LANES).T for g in range(SSD_GROUPS)]

    o = []
    for g in kvg:
        pv = [_dot_nt(p_c[g], cv_ref[b, g]) for b in range(bt)]
        og = jnp.zeros((GR, HD), f32)
        for b in range(bt):
            og = jnp.where(rb == b, pv[b], og)
        for t2 in range(steps):
            vt = jnp.concatenate([slab(v_new, t2)[:, g * HD:(g + 1) * HD]] * reps, axis=0)
            og = og + (e_n[g][t2] * rinv[g]) * vt
        o.append(og)

    lane_w = lax.broadcasted_iota(jnp.int32, (1, WINDOW), 1)
    for b in range(bt):
        for g in kvg:
            gs = slice(g * HD, (g + 1) * HD)
            wk_ref[b, g] = jnp.where(lane_w < keep, pltpu.roll(ck_ref[b, g], keep, 1), new_k[b][gs])
            wv_ref[b, g] = jnp.where(lane_w < keep, pltpu.roll(cv_ref[b, g], keep, 1), new_v[b][gs])

    y_off_parts = []
    for g in range(SSD_GROUPS):
        yt = jnp.sum(jnp.where(pair, zz[g], 0.0), axis=0)
        y_off_parts.append(yt.T[0:R, :])
        dec = jnp.sum(jnp.where(own, dec_t[g][None], 0.0), axis=-1, keepdims=True)
        ssm_ref[:, gsl[g], :] = h0[g] * dec + contrib[g]
    y_off = jnp.concatenate(y_off_parts, axis=-1) * ecs_e
    y = y_intra + y_off + xs * dskip_ref[...]
    y_ssd = _gated_group_norm(y, z, gssd_ref[...])

    mix = jnp.dot(y_ssd.astype(bf16), wout_ref[0:SSD_WIDTH, :], preferred_element_type=f32)
    for g in kvg:
        for hl in range(GH):
            hd = g * GH + hl
            mix = mix + jnp.dot(o[g][hl * R:(hl + 1) * R].astype(bf16),
                                wout_ref[SSD_WIDTH + hd * HD:SSD_WIDTH + (hd + 1) * HD, :],
                                preferred_element_type=f32)
    y_ref[...] = (x + _rmsnorm(mix, gpost_ref[...])).reshape(steps, bt, D_MODEL)

    wxq_out[...] = (wxq_ref[...] * X_SCALE).astype(bf16)
    wxo_out[...] = wxo_ref[...].astype(bf16)


def _sample_mixer(x_tm, cprev_tm, st, ck, cv, gpre, win, convw, convb, dtb, alog, dskip_e, gssd,
                  sinkcol, wout, gpost, wxq, wxo, bt):
    steps, nb, D = x_tm.shape
    kern = functools.partial(_sample_mixer_kernel, bt=bt, steps=steps)
    tm = lambda w: pl.BlockSpec((steps, bt, w), lambda i: (0, i, 0))
    win_spec = pl.BlockSpec((bt, ATT_KV_HEADS, ATT_HEAD_DIM, WINDOW), lambda i: (i, 0, 0, 0))

    def slab(w):
        return pl.BlockSpec((w.shape[0] // (nb // bt), w.shape[1]), lambda i: (i, 0))

    in_specs = [
        tm(D),
        pl.BlockSpec((SSD_CONV - 1, bt, SSD_CONV_DIM), lambda i: (0, i, 0)),
        pl.BlockSpec((bt, SSD_WIDTH, SSD_STATE), lambda i: (i, 0, 0)),
        win_spec, win_spec,
    ] + [_const_spec(a.shape) for a in (gpre, win, convw, convb, dtb, alog, dskip_e, gssd,
                                        sinkcol, wout, gpost)] + [slab(wxq), slab(wxo)]
    out_specs = (
        tm(D),
        pl.BlockSpec((bt, SSD_WIDTH, SSD_STATE), lambda i: (i, 0, 0)),
        pl.BlockSpec((SSD_CONV - 1, bt, SSD_CONV_DIM), lambda i: (0, i, 0)),
        win_spec, win_spec, slab(wxq), slab(wxo),
    )
    out_shape = (
        jax.ShapeDtypeStruct((steps, nb, D), f32),
        jax.ShapeDtypeStruct((nb, SSD_WIDTH, SSD_STATE), f32),
        jax.ShapeDtypeStruct((SSD_CONV - 1, nb, SSD_CONV_DIM), f32),
        jax.ShapeDtypeStruct(ck.shape, f32),
        jax.ShapeDtypeStruct(cv.shape, f32),
        jax.ShapeDtypeStruct(wxq.shape, bf16),
        jax.ShapeDtypeStruct(wxo.shape, bf16),
    )
    return pl.pallas_call(
        kern, grid=(nb // bt,), in_specs=in_specs, out_specs=out_specs, out_shape=out_shape,
        name="sample_mixer",
        compiler_params=pltpu.CompilerParams(
            dimension_semantics=("arbitrary",), vmem_limit_bytes=VMEM_LIMIT),
    )(x_tm, cprev_tm, st, ck, cv, gpre, win, convw, convb, dtb, alog, dskip_e, gssd, sinkcol,
      wout, gpost, wxq, wxo)


def _sample_xattn_kernel(x_ref, mk_ref, mv_ref, gpre_ref, wq_ref, wo_ref, gpost_ref, y_ref,
                         q_sc, o_sc, *, bt, steps):
    i = pl.program_id(0)
    nb = x_ref.shape[1]
    R = steps * bt
    nrow = bt * N_MEM * X_HEADS

    @pl.when(i == 0)
    def _project_queries():
        x_all = x_ref[...].reshape(steps * nb, D_MODEL)
        hn = _rmsnorm(x_all, gpre_ref[...]).astype(bf16)
        q_sc[...] = jnp.dot(hn, wq_ref[...], preferred_element_type=f32)

    tile_rows = [pl.ds(pl.multiple_of(t * nb + i * bt, bt), bt) for t in range(steps)]
    q = jnp.concatenate([q_sc[r, :] for r in tile_rows], axis=0)
    qs = jnp.concatenate([q[:, hd * X_HEAD_DIM:(hd + 1) * X_HEAD_DIM] for hd in range(X_HEADS)],
                         axis=0)
    kall = mk_ref[...].reshape(nrow, X_HEAD_DIM)
    vall = mv_ref[...].reshape(nrow, X_HEAD_DIM)
    ncol = X_HEADS * R
    seq_rows = N_MEM * X_HEADS
    z = _dot_nt(kall, qs).reshape(bt, seq_rows, ncol)
    v_b = [vall[b * seq_rows:(b + 1) * seq_rows].astype(bf16) for b in range(bt)]
    b_i = lax.broadcasted_iota(jnp.int32, (bt, 1, ncol), 0)
    c_i = lax.broadcasted_iota(jnp.int32, (bt, 1, ncol), 2)
    zc = jnp.sum(jnp.where(c_i % bt == b_i, z, 0.0), axis=0)
    zc = zc.reshape(seq_rows // SUBLANES, SUBLANES, ncol)
    r_h = lax.broadcasted_iota(jnp.int32, (1, SUBLANES, ncol), 1) % X_HEADS
    c_h = lax.broadcasted_iota(jnp.int32, (1, SUBLANES, ncol), 2) // R
    zc = jnp.where(r_h == c_h, zc, -jnp.inf).reshape(seq_rows, ncol)
    m = jnp.max(zc, axis=0, keepdims=True)
    e = jnp.exp(zc - m)
    p = e * (1.0 / jnp.sum(e, axis=0, keepdims=True))
    col_b = lax.broadcasted_iota(jnp.int32, (1, ncol), 1) % bt
    tn = (((0,), (0,)), ((), ()))
    o = None
    for b in range(bt):
        p_b = jnp.where(col_b == b, p, 0.0).astype(bf16)
        o_b = lax.dot_general(p_b, v_b[b], tn, preferred_element_type=f32)
        o = o_b if o is None else o + o_b
    o = jnp.concatenate([o[hd * R:(hd + 1) * R] for hd in range(X_HEADS)], axis=-1)
    for t in range(steps):
        o_sc[tile_rows[t], :] = o[t * bt:(t + 1) * bt]

    @pl.when(i == pl.num_programs(0) - 1)
    def _project_outputs():
        x_all = x_ref[...].reshape(steps * nb, D_MODEL)
        cc = _dot(o_sc[...], wo_ref[...])
        y_ref[...] = (x_all + _rmsnorm(cc, gpost_ref[...])).reshape(steps, nb, D_MODEL)


def _sample_xattn(x_tm, mk, mv, gpre, wq, wo, gpost, bt):
    steps, nb, D = x_tm.shape
    kern = functools.partial(_sample_xattn_kernel, bt=bt, steps=steps)
    xs = _const_spec(x_tm.shape)
    ms = pl.BlockSpec((bt, N_MEM, X_HEADS, X_HEAD_DIM), lambda i: (i, 0, 0, 0))
    return pl.pallas_call(
        kern, grid=(nb // bt,),
        in_specs=[xs, ms, ms, _const_spec(gpre.shape), _const_spec(wq.shape),
                  _const_spec(wo.shape), _const_spec(gpost.shape)],
        out_specs=xs, out_shape=jax.ShapeDtypeStruct((steps, nb, D), f32),
        scratch_shapes=[pltpu.VMEM((steps * nb, D), f32), pltpu.VMEM((steps * nb, D), f32)],
        name="sample_xattn",
        compiler_params=pltpu.CompilerParams(
            dimension_semantics=("arbitrary",), vmem_limit_bytes=VMEM_LIMIT),
    )(x_tm, mk, mv, gpre, wq, wo, gpost)


def _win_prep_kernel(wt_ref, o_ref):
    piece = 2 * LANES

    def put(src_lo, n, dst_lo, scale=None):
        for c in range(0, n, piece):
            t = wt_ref[src_lo + c:src_lo + c + piece, :].T
            if scale is not None:
                t = t * scale
            o_ref[:, dst_lo + c:dst_lo + c + piece] = t.astype(bf16)

    put(W_Z, P_Q - P_Z, P_Z)
    put(W_Q, P_K - P_Q, P_Q, ATT_SCALE)
    put(W_K, P_DT - P_K, P_K)
    dt_rows = jnp.concatenate([wt_ref[W_DT:W_DT + SSD_HEADS, :],
                               jnp.zeros((LANES - SSD_HEADS, D_MODEL), f32)], axis=0)
    o_ref[:, P_DT:P_END] = dt_rows.T.astype(bf16)


def _win_prep(w_t):
    return pl.pallas_call(
        _win_prep_kernel, out_shape=jax.ShapeDtypeStruct((D_MODEL, P_END), bf16),
        name="win_prep",
        compiler_params=pltpu.CompilerParams(vmem_limit_bytes=VMEM_LIMIT),
    )(w_t)


def _row(v, width=None):
    v = v.reshape(1, -1).astype(f32)
    if width is not None and v.shape[1] < width:
        v = jnp.pad(v, ((0, 0), (0, width - v.shape[1])))
    return v


def kernel(x_prompt, x_sample, state_ssm, state_conv, cache_win_k, cache_win_v, cache_mem_k, cache_mem_v, mem_prompt, g_mix_pre, w_in, conv_w, conv_b, dt_bias, a_log, d_skip, g_ssd_norm, sinks, w_out, g_mix_post, g_x_pre, w_xq, g_mem, w_xk, w_xv, w_xo, g_x_post, g_ffn_pre, w_gate, w_up, w_down, g_ffn_post):
    depth = w_in.shape[0]
    assert depth == 1
    B, L, D = x_prompt.shape
    NB, steps, _ = x_sample.shape
    li = 0

    win_p = _win_prep(jnp.transpose(w_in[li]))
    wo_b = w_out[li].astype(bf16)

    gpre, gpost = _row(g_mix_pre[li]), _row(g_mix_post[li])
    convw, convb = conv_w[li].astype(f32), _row(conv_b[li])
    dtb, alog = _row(dt_bias[li], LANES), _row(a_log[li], LANES)
    dskip_e = _row(jnp.repeat(d_skip[li], SSD_HEAD_DIM))
    gssd = _row(g_ssd_norm[li])
    sk = sinks[li].astype(f32)

    mk2d, mv2d, mk4, mv4 = _memkv(mem_prompt.reshape(B * N_MEM, D), _row(g_mem[li]),
                                  w_xk[li].astype(bf16), w_xv[li].astype(bf16), tile=512)
    mk3, mv3 = mk2d.reshape(B, N_MEM, D), mv2d.reshape(B, N_MEM, D)
    x1, p_ssm, p_conv8, p_wk, p_wv = _prompt_mixer(
        x_prompt, gpre, win_p, convw, convb, dtb, alog, dskip_e, gssd, sk, wo_b, gpost, tile=512)
    gxpre, gxpost = _row(g_x_pre[li]), _row(g_x_post[li])
    gfpre, gfpost = _row(g_ffn_pre[li]), _row(g_ffn_post[li])

    bt, bt_mix = 8, 16
    x_tm = jnp.transpose(x_sample, (1, 0, 2))
    cprev_tm = jnp.transpose(state_conv[li], (1, 0, 2))
    st = state_ssm[li].reshape(NB, SSD_WIDTH, SSD_STATE)
    ck = jnp.transpose(cache_win_k[li], (0, 2, 3, 1))
    cv = jnp.transpose(cache_win_v[li], (0, 2, 3, 1))
    sinkcol = jnp.repeat(sk, steps * bt_mix).reshape(ATT_HEADS * steps * bt_mix, 1)
    x1s, s_ssm, cnew_tm, s_wk, s_wv, wxq_b, wxo_b = _sample_mixer(
        x_tm, cprev_tm, st, ck, cv, gpre, win_p, convw, convb, dtb, alog, dskip_e, gssd,
        sinkcol, wo_b, gpost, w_xq[li], w_xo[li], bt=bt_mix)

    x2, wg_b, wu_b, wd_b = _prompt_xattn(x1, gxpre, wxq_b, mk3, mv3, wxo_b, gxpost,
                                         (w_gate[li], w_up[li], w_down[li]), tile=512)
    cmk = cache_mem_k.reshape(NB, N_MEM, X_HEADS, X_HEAD_DIM)
    cmv = cache_mem_v.reshape(NB, N_MEM, X_HEADS, X_HEAD_DIM)
    x2s = _sample_xattn(x1s, cmk, cmv, gxpre, wxq_b, wxo_b, gxpost, bt=bt)
    yp2d, ys_tm = _ffn(x2.reshape(B * L, D), x2s.reshape(steps * NB, D), gfpre, wg_b, wu_b, wd_b,
                       gfpost, tile=steps * NB)
    yp = yp2d.reshape(B, L, D)
    ys = jnp.transpose(ys_tm.reshape(steps, NB, D), (1, 0, 2))

    s_conv = jnp.transpose(cnew_tm, (1, 0, 2))
    kv_shape = (ATT_KV_HEADS, ATT_HEAD_DIM)
    return (
        yp, ys,
        p_ssm.reshape(1, B, SSD_HEADS, SSD_HEAD_DIM, SSD_STATE),
        p_conv8[:, SUBLANES - (SSD_CONV - 1):, :][None],
        jnp.transpose(p_wk.reshape(B, *kv_shape, WINDOW), (0, 3, 1, 2))[None],
        jnp.transpose(p_wv.reshape(B, *kv_shape, WINDOW), (0, 3, 1, 2))[None],
        mk4.reshape(1, B, N_MEM, X_HEADS, X_HEAD_DIM), mv4.reshape(1, B, N_MEM, X_HEADS, X_HEAD_DIM),
        s_ssm.reshape(1, NB, SSD_HEADS, SSD_HEAD_DIM, SSD_STATE),
        s_conv[None],
        jnp.transpose(s_wk, (0, 3, 1, 2))[None], jnp.transpose(s_wv, (0, 3, 1, 2))[None],
    )
```

```python
import functools

import jax
import jax.numpy as jnp
from jax import lax
from jax.experimental import pallas as pl
from jax.experimental.pallas import tpu as pltpu

f32 = jnp.float32
bf16 = jnp.bfloat16

D_MODEL = 1024
EPS = 1e-6
N_MEM = 256
SSD_HEADS = 8
SSD_HEAD_DIM = 64
SSD_WIDTH = 512
SSD_GROUPS = 2
SSD_STATE = 128
SSD_CONV = 4
SSD_CHUNK = 128
SSD_CONV_DIM = 1024
ATT_HEADS = 8
ATT_KV_HEADS = 2
ATT_HEAD_DIM = 64
ATT_WIDTH = 512
ATT_KV_WIDTH = 128
WINDOW = 128
ATT_SCALE = ATT_HEAD_DIM ** -0.5
X_HEADS = 4
X_HEAD_DIM = 256
X_SCALE = X_HEAD_DIM ** -0.5
D_FF = 2816
LANES = 128
SUBLANES = 8
VMEM_LIMIT = 56 * 1024 * 1024
STREAMS = 2
PROJ_PIECES, SCAN_PIECES, OUT_PIECES = 19, 10, 5
CONV_BLOCKS = SSD_CONV_DIM // LANES
XATTN_LEAD = 5

P_Z, P_XBC, P_Q, P_K, P_V, P_DT, P_END = 0, 512, 1536, 2048, 2176, 2304, 2432
W_Z, W_DT, W_Q, W_K = 0, 1536, 1544, 2056


def _dot(a, b):
    return jnp.dot(a.astype(bf16), b.astype(bf16), preferred_element_type=f32)


def _dot_nt(a, b):
    return lax.dot_general(a.astype(bf16), b.astype(bf16), (((1,), (1,)), ((), ())),
                           preferred_element_type=f32)


def _split2(x):
    hi = x.astype(bf16)
    lo = (x - hi.astype(f32)).astype(bf16)
    return hi, lo


def _dot_x2(x, m):
    hi, lo = _split2(x)
    return (jnp.dot(hi, m, preferred_element_type=f32)
            + jnp.dot(lo, m, preferred_element_type=f32))


def _rmsnorm(x, g):
    ms = jnp.mean(x * x, axis=-1, keepdims=True)
    return x * lax.rsqrt(ms + EPS) * g


def _silu(x):
    return x * jax.nn.sigmoid(x)


def _softplus(x):
    return jnp.maximum(x, 0.0) + jnp.log1p(jnp.exp(-jnp.abs(x)))


def _head_expand_matrix():
    r = lax.broadcasted_iota(jnp.int32, (LANES, SSD_WIDTH), 0)
    c = lax.broadcasted_iota(jnp.int32, (LANES, SSD_WIDTH), 1)
    return (r == (c >> 6)).astype(bf16)


def _gated_group_norm(y, z, g):
    u = y * _silu(z)
    half = SSD_WIDTH // SSD_GROUPS
    parts = []
    for gi in range(SSD_GROUPS):
        ug = u[:, gi * half:(gi + 1) * half]
        parts.append(ug * lax.rsqrt(jnp.mean(ug * ug, axis=-1, keepdims=True) + EPS))
    return jnp.concatenate(parts, axis=-1) * g


def _prompt_mixer_kernel(x_ref, gpre_ref, win_ref, convw_ref, convb_ref, dtb_ref, alog_ref,
                         dskip_ref, gssd_ref, sinks_ref, wout_ref, gpost_ref,
                         y_ref, ssm_ref, conv_ref, wk_ref, wv_ref,
                         statet_sc, xbc_ext_sc, xbc_sc, z_sc, q_sc, k_sc, v_sc, dt_sc,
                         kprev_sc, vprev_sc, mix_sc, *, tile):
    i = pl.program_id(1)
    NC = tile // SSD_CHUNK
    ns = range(NC)
    C = SSD_CHUNK
    PW = 2 * LANES

    @pl.when(i == 0)
    def _init():
        statet_sc[...] = jnp.zeros_like(statet_sc)
        xbc_ext_sc[:, :, 0:SUBLANES, :] = jnp.zeros((STREAMS, CONV_BLOCKS, SUBLANES, LANES), f32)
        kprev_sc[...] = jnp.zeros_like(kprev_sc)
        vprev_sc[...] = jnp.zeros_like(vprev_sc)

    lane = lax.broadcasted_iota(jnp.int32, (1, LANES), 1)
    a_row = -jnp.exp(alog_ref[...])
    expand = _head_expand_matrix()
    expand2 = jnp.concatenate([expand, expand], axis=0)
    row_i = lax.broadcasted_iota(jnp.int32, (C, C), 0)
    col_i = lax.broadcasted_iota(jnp.int32, (C, C), 1)
    lower = col_i <= row_i
    tri = lower.astype(bf16)
    tri3 = jnp.concatenate([tri, tri, tri], axis=1)
    lo_half = lane < ATT_HEAD_DIM
    half = SSD_WIDTH // SSD_GROUPS
    head_of_lane = lax.broadcasted_iota(jnp.int32, (1, half), 1) >> 6
    rows = [slice(n * C, (n + 1) * C) for n in ns]

    def stream(s):
        statet_s, xbc_ext_s, xbc_s, z_s = statet_sc.at[s], xbc_ext_sc.at[s], xbc_sc.at[s], z_sc.at[s]
        q_s, k_s, v_s, dt_s = q_sc.at[s], k_sc.at[s], v_sc.at[s], dt_sc.at[s]
        kprev_s, vprev_s, mix_s = kprev_sc.at[s], vprev_sc.at[s], mix_sc.at[s]

        x = x_ref[s, 0]
        h = _rmsnorm(x, gpre_ref[...]).astype(bf16)
        yield

        def proj(col):
            return jnp.dot(h, win_ref[:, col:col + PW], preferred_element_type=f32)

        def conv_cols(cb):
            cols = slice(cb * LANES, (cb + 1) * LANES)
            acc = convb_ref[:, cols]
            for j in range(SSD_CONV):
                off = SUBLANES - (SSD_CONV - 1) + j
                acc = acc + xbc_ext_s[cb, off:off + tile, :] * convw_ref[j:j + 1, cols]
            xbc_s[cb] = _silu(acc)

        def xbc_piece(pc):
            res = proj(P_XBC + pc * PW)
            for half_pc in range(PW // LANES):
                xbc_ext_s[(PW // LANES) * pc + half_pc, SUBLANES:SUBLANES + tile, :] = (
                    res[:, half_pc * LANES:(half_pc + 1) * LANES])

        xbc_piece(0)
        yield
        xbc_piece(1)
        yield
        conv_cols(0)
        yield
        xbc_piece(2)
        yield
        conv_cols(1)
        yield
        xbc_piece(3)
        yield
        conv_cols(2)
        yield
        z_s[:, 0:PW] = proj(P_Z)
        yield
        conv_cols(3)
        yield
        z_s[:, PW:2 * PW] = proj(P_Z + PW)
        yield
        conv_cols(4)
        yield
        q_s[:, 0:PW] = proj(P_Q)
        yield
        conv_cols(5)
        yield
        q_s[:, PW:2 * PW] = proj(P_Q + PW)
        yield
        conv_cols(6)
        yield
        kv = proj(P_K)
        k_s[...] = kv[:, 0:LANES]
        v_s[...] = kv[:, LANES:PW]
        yield
        conv_cols(7)
        yield
        dt_raw = jnp.dot(h, win_ref[:, P_DT:P_END], preferred_element_type=f32)
        dt_s[...] = jnp.where(lane < SSD_HEADS, _softplus(dt_raw + dtb_ref[...]), 0.0)
        tail = xbc_ext_s[:, tile:tile + SUBLANES, :]
        conv_ref[s, 0] = jnp.concatenate([tail[cb] for cb in range(CONV_BLOCKS)], axis=-1)
        xbc_ext_s[:, 0:SUBLANES, :] = tail
        yield

        GH = SSD_HEADS // SSD_GROUPS
        heads = range(ATT_HEADS)
        gs = range(SSD_GROUPS)
        zero_b = jnp.zeros((), bf16)

        yield
        def xbc_cols(n, lo, hi):
            return jnp.concatenate([xbc_s[cb, rows[n], :] for cb in range(lo // LANES, hi // LANES)],
                                   axis=-1)
        xs = [xbc_cols(n, 0, SSD_WIDTH) for n in ns]
        bm = [xbc_cols(n, SSD_WIDTH, SSD_WIDTH + half) for n in ns]
        cm = [xbc_cols(n, SSD_WIDTH + half, SSD_CONV_DIM) for n in ns]
        dtc = [dt_s[rows[n], :] for n in ns]
        cs = []
        for n in ns:
            adt = dtc[n] * a_row
            a_hi = adt.astype(bf16)
            a_r1 = adt - a_hi.astype(f32)
            a_mid = a_r1.astype(bf16)
            a_lo = (a_r1 - a_mid.astype(f32)).astype(bf16)
            cs.append(jnp.dot(tri3, jnp.concatenate([a_hi, a_mid, a_lo], axis=0),
                              preferred_element_type=f32))

        yield
        q = [q_s[rows[n], :].astype(bf16) for n in ns]
        k = [k_s[rows[n], :] for n in ns]
        v = [v_s[rows[n], :] for n in ns]
        k_prev = [kprev_s[...]] + k[:-1]
        v_prev = [vprev_s[...]] + v[:-1]
        kprev_s[...] = k[-1]
        vprev_s[...] = v[-1]
        first_bias = jnp.where(i > 0, 0.0, -jnp.inf)
        k_lo, k_hi, v_lo, v_hi = [], [], [], []
        for n in ns:
            kk = jnp.concatenate([k_prev[n], k[n]], axis=0)
            vv = jnp.concatenate([v_prev[n], v[n]], axis=0)
            kk_r = pltpu.roll(kk, ATT_HEAD_DIM, 1)
            vv_r = pltpu.roll(vv, ATT_HEAD_DIM, 1)
            k_lo.append([jnp.where(lo_half, kk, 0.0).astype(bf16), jnp.where(lo_half, kk_r, 0.0).astype(bf16)])
            k_hi.append([jnp.where(lo_half, 0.0, kk_r).astype(bf16), jnp.where(lo_half, 0.0, kk).astype(bf16)])
            v_lo.append([jnp.where(lo_half, vv, 0.0).astype(bf16), jnp.where(lo_half, vv_r, 0.0).astype(bf16)])
            v_hi.append([jnp.where(lo_half, 0.0, vv_r).astype(bf16), jnp.where(lo_half, 0.0, vv).astype(bf16)])
        s_g = [[_dot_nt(jnp.concatenate([q[n][:, (2 * g) * LANES:(2 * g + 1) * LANES],
                                         q[n][:, (2 * g + 1) * LANES:(2 * g + 2) * LANES]], axis=0),
                        jnp.concatenate([k_lo[n][g], k_hi[n][g]], axis=0))
                for g in range(ATT_KV_HEADS)] for n in ns]

        yield
        cs_t = [cs[n].T for n in ns]
        fac_e, cb_all, bm_t = [], [], []
        for n in ns:
            tot = cs[n][C - 1:C, :]
            fac = jnp.concatenate([dtc[n], jnp.exp(cs[n]), jnp.exp(tot - cs[n])], axis=0)
            f_hi = fac.astype(bf16)
            f_lo = (fac - f_hi.astype(f32)).astype(bf16)
            fac_e.append(jnp.dot(jnp.concatenate([f_hi, f_lo], axis=1), expand2,
                                 preferred_element_type=f32))
            cb_all.append(_dot_nt(
                jnp.concatenate([cm[n][:, 0:SSD_STATE], cm[n][:, SSD_STATE:half]], axis=0),
                jnp.concatenate([bm[n][:, 0:SSD_STATE], bm[n][:, SSD_STATE:half]], axis=0)))
            bm_t.append(bm[n].T)

        yield
        def head_scores(n, hd):
            g, jb, sub = hd // 4, (hd // 2) % 2, hd % 2
            s_prev = s_g[n][g][jb * C:(jb + 1) * C, (2 * sub) * C:(2 * sub + 1) * C]
            s_cur = s_g[n][g][jb * C:(jb + 1) * C, (2 * sub + 1) * C:(2 * sub + 2) * C]
            return jnp.where(lower, s_cur, s_prev + first_bias if n == 0 else s_prev)
        sc = [[head_scores(n, hd) for hd in heads] for n in ns]
        sink = [sinks_ref[hd] for hd in heads]
        m = [[jnp.maximum(jnp.max(sc[n][hd], axis=-1, keepdims=True), sink[hd]) for hd in heads]
             for n in ns]

        yield
        xdt = [xs[n] * fac_e[n][0:C] for n in ns]
        ecs_e = [fac_e[n][C:2 * C] for n in ns]
        w_end = [xdt[n] * fac_e[n][2 * C:3 * C] for n in ns]
        xdt_b = [xdt[n].astype(bf16) for n in ns]
        contrib = [[_dot(bm_t[n][g * SSD_STATE:(g + 1) * SSD_STATE, :],
                         w_end[n][:, g * half:(g + 1) * half]) for g in gs] for n in ns]
        decay = [[jnp.exp(jnp.where(lower, cs[n][:, hh:hh + 1] - cs_t[n][hh:hh + 1, :], -jnp.inf))
                  for hh in range(SSD_HEADS)] for n in ns]

        yield
        e = [[jnp.exp(sc[n][hd] - m[n][hd]) for hd in heads] for n in ns]
        den = [[jnp.sum(e[n][hd], axis=-1, keepdims=True) + jnp.exp(sink[hd] - m[n][hd])
                for hd in heads] for n in ns]

        yield
        y_d = []
        for n in ns:
            y_n = []
            for g in gs:
                gl = slice(g * half, (g + 1) * half)
                cb = cb_all[n][g * C:(g + 1) * C, g * C:(g + 1) * C]
                m_parts = [(cb * decay[n][g * GH + r]).astype(bf16) for r in range(GH)]
                x_parts = [jnp.where(head_of_lane == r, xdt_b[n][:, gl], zero_b) for r in range(GH)]
                y_n.append(jnp.dot(jnp.concatenate(m_parts, axis=1), jnp.concatenate(x_parts, axis=0),
                                   preferred_element_type=f32))
            y_d.append(y_n)
        st = [statet_s[g] for g in gs]
        y_off = []
        for n in ns:
            y_off.append([_dot(cm[n][:, g * SSD_STATE:(g + 1) * SSD_STATE], st[g]) for g in gs])
            st = [st[g] * ecs_e[n][C - 1:C, g * half:(g + 1) * half] + contrib[n][g] for g in gs]
        for g in gs:
            statet_s[g] = st[g]

        yield
        p = [[(e[n][hd] * (1.0 / den[n][hd])).astype(bf16) for hd in heads] for n in ns]
        o_g = []
        for n in ns:
            o_n = []
            for g in range(ATT_KV_HEADS):
                p_rows = []
                for jb in range(2):
                    p_cols = []
                    for sub in range(2):
                        ph = p[n][g * 4 + jb * 2 + sub]
                        p_cols += [jnp.where(lower, zero_b, ph), jnp.where(lower, ph, zero_b)]
                    p_rows.append(jnp.concatenate(p_cols, axis=1))
                o_n.append(jnp.dot(jnp.concatenate(p_rows, axis=0),
                                   jnp.concatenate([v_lo[n][g], v_hi[n][g]], axis=0),
                                   preferred_element_type=f32))
            o_g.append(o_n)

        yield
        for n in ns:
            y = (jnp.concatenate([y_d[n][g] + y_off[n][g] * ecs_e[n][:, g * half:(g + 1) * half]
                                  for g in gs], axis=-1)
                 + xs[n] * dskip_ref[...])
            y_ssd = _gated_group_norm(y, z_s[rows[n], :], gssd_ref[...])
            mix_s[rows[n], 0:SSD_WIDTH] = y_ssd.astype(bf16)
        for n in ns:
            for g in range(ATT_KV_HEADS):
                for jb in range(2):
                    lo_l = SSD_WIDTH + (2 * g + jb) * LANES
                    mix_s[rows[n], lo_l:lo_l + LANES] = o_g[n][g][jb * C:(jb + 1) * C].astype(bf16)

        yield

        mix_in = mix_s[...]
        mix = []
        for pc in range(D_MODEL // PW):
            mix.append(jnp.dot(mix_in, wout_ref[:, pc * PW:(pc + 1) * PW], preferred_element_type=f32))
            yield
        y_ref[s, 0] = x + _rmsnorm(jnp.concatenate(mix, axis=-1), gpost_ref[...])

    def mixed(ga, na, gb, nb):
        done_b = 0
        for ka in range(na):
            next(ga, None)
            want_b = ((ka + 1) * nb) // na
            for _ in range(want_b - done_b):
                next(gb, None)
            done_b = want_b

    g0, g1 = [stream(s) for s in range(STREAMS)]
    for _ in range(PROJ_PIECES):
        next(g0, None)
    mixed(g0, SCAN_PIECES, g1, PROJ_PIECES)
    mixed(g1, SCAN_PIECES, g0, OUT_PIECES)
    for g in (g0, g1):
        for _ in g:
            pass

    @pl.when(i == pl.num_programs(1) - 1)
    def _final_state():
        for s in range(STREAMS):
            for g in range(SSD_GROUPS):
                ssm_ref[s, 0, g * half:(g + 1) * half, :] = statet_sc[s, g].T
            wk_ref[s, 0] = k_sc[s, tile - WINDOW:tile, :].T
            wv_ref[s, 0] = v_sc[s, tile - WINDOW:tile, :].T


def _const_spec(shape):
    nd = len(shape)
    return pl.BlockSpec(shape, lambda *_: (0,) * nd)


def _prompt_mixer(x, gpre, win, convw, convb, dtb, alog, dskip_e, gssd, sinks, wout, gpost, tile):
    B, L, D = x.shape
    S = STREAMS
    G = B // S
    kern = functools.partial(_prompt_mixer_kernel, tile=tile)

    def per_seq(rows, width):
        return pl.BlockSpec((S, 1, rows, width), lambda b, i: (0, b, 0, 0))

    tile_spec = pl.BlockSpec((S, 1, tile, D), lambda b, i: (0, b, i, 0))
    out_shape = (
        jax.ShapeDtypeStruct((S, G, L, D), f32),
        jax.ShapeDtypeStruct((S, G, SSD_WIDTH, SSD_STATE), f32),
        jax.ShapeDtypeStruct((S, G, SUBLANES, SSD_CONV_DIM), f32),
        jax.ShapeDtypeStruct((S, G, ATT_KV_WIDTH, WINDOW), f32),
        jax.ShapeDtypeStruct((S, G, ATT_KV_WIDTH, WINDOW), f32),
    )
    in_specs = [
        tile_spec,
        _const_spec(gpre.shape), _const_spec(win.shape), _const_spec(convw.shape),
        _const_spec(convb.shape), _const_spec(dtb.shape), _const_spec(alog.shape),
        _const_spec(dskip_e.shape), _const_spec(gssd.shape),
        pl.BlockSpec(memory_space=pltpu.SMEM),
        _const_spec(wout.shape), _const_spec(gpost.shape),
    ]
    out_specs = (
        tile_spec,
        per_seq(SSD_WIDTH, SSD_STATE),
        per_seq(SUBLANES, SSD_CONV_DIM),
        per_seq(ATT_KV_WIDTH, WINDOW),
        per_seq(ATT_KV_WIDTH, WINDOW),
    )
    scratch = [
        pltpu.VMEM((S, SSD_GROUPS, SSD_STATE, SSD_WIDTH // SSD_GROUPS), f32),
        pltpu.VMEM((S, CONV_BLOCKS, tile + 2 * SUBLANES, LANES), f32),
        pltpu.VMEM((S, CONV_BLOCKS, tile, LANES), f32),
        pltpu.VMEM((S, tile, SSD_WIDTH), f32),
        pltpu.VMEM((S, tile, ATT_WIDTH), f32),
        pltpu.VMEM((S, tile, ATT_KV_WIDTH), f32),
        pltpu.VMEM((S, tile, ATT_KV_WIDTH), f32),
        pltpu.VMEM((S, tile, LANES), f32),
        pltpu.VMEM((S, WINDOW, ATT_KV_WIDTH), f32),
        pltpu.VMEM((S, WINDOW, ATT_KV_WIDTH), f32),
        pltpu.VMEM((S, tile, 2 * SSD_WIDTH), bf16),
    ]
    outs = pl.pallas_call(
        kern, grid=(G, L // tile), in_specs=in_specs, out_specs=out_specs, out_shape=out_shape,
        scratch_shapes=scratch, name="prompt_mixer",
        compiler_params=pltpu.CompilerParams(
            dimension_semantics=("arbitrary", "arbitrary"), vmem_limit_bytes=VMEM_LIMIT),
    )(x.reshape(S, G, L, D), gpre, win, convw, convb, dtb, alog, dskip_e, gssd, sinks, wout, gpost)
    return tuple(o.reshape(B, *o.shape[2:]) for o in outs)


def _memkv_kernel(m_ref, g_ref, wk_ref, wv_ref, k_ref, v_ref, kh_ref, vh_ref):
    mn = _rmsnorm(m_ref[...], g_ref[...]).astype(bf16)
    k = jnp.dot(mn, wk_ref[...], preferred_element_type=f32)
    v = jnp.dot(mn, wv_ref[...], preferred_element_type=f32)
    k_ref[...] = k.astype(bf16)
    v_ref[...] = v.astype(bf16)
    for hd in range(X_HEADS):
        kh_ref[:, hd, :] = k[:, hd * X_HEAD_DIM:(hd + 1) * X_HEAD_DIM]
        vh_ref[:, hd, :] = v[:, hd * X_HEAD_DIM:(hd + 1) * X_HEAD_DIM]


def _memkv(mem2d, g, wk, wv, tile):
    n, d = mem2d.shape
    row = pl.BlockSpec((tile, d), lambda i: (i, 0))
    hrow = pl.BlockSpec((tile, X_HEADS, X_HEAD_DIM), lambda i: (i, 0, 0))
    flat = jax.ShapeDtypeStruct((n, d), bf16)
    heads = jax.ShapeDtypeStruct((n, X_HEADS, X_HEAD_DIM), f32)
    return pl.pallas_call(
        _memkv_kernel, grid=(n // tile,),
        in_specs=[row, _const_spec(g.shape), _const_spec(wk.shape), _const_spec(wv.shape)],
        out_specs=(row, row, hrow, hrow),
        out_shape=(flat, flat, heads, heads),
        name="memory_kv",
        compiler_params=pltpu.CompilerParams(
            dimension_semantics=("arbitrary",), vmem_limit_bytes=VMEM_LIMIT),
    )(mem2d, g, wk, wv)


def _prompt_xattn_kernel(x_ref, gpre_ref, wq_ref, mk_ref, mv_ref, wo_ref, gpost_ref,
                         wg_ref, wu_ref, wd_ref, y_ref, wg_out, wu_out, wd_out):
    hs = range(X_HEADS)
    sl = [slice(hd * X_HEAD_DIM, (hd + 1) * X_HEAD_DIM) for hd in hs]

    def stream(s):
        x = x_ref[s, 0]
        hn = _rmsnorm(x, gpre_ref[...]).astype(bf16)
        yield
        q = []
        for hd in hs:
            q.append(jnp.dot(hn, wq_ref[:, sl[hd]], preferred_element_type=f32))
            yield
        sc = [_dot_nt(q[hd], mk_ref[s, 0, :, sl[hd]]) for hd in hs]
        yield
        m = [jnp.max(sc[hd], axis=-1, keepdims=True) for hd in hs]
        e = [jnp.exp(sc[hd] - m[hd]) for hd in hs]
        yield
        r = [1.0 / jnp.sum(e[hd], axis=-1, keepdims=True) for hd in hs]
        p = [(e[hd] * r[hd]).astype(bf16) for hd in hs]
        yield
        o = jnp.concatenate([_dot(p[hd], mv_ref[s, 0, :, sl[hd]]) for hd in hs], axis=-1).astype(bf16)
        yield
        c = []
        for hd in hs:
            c.append(jnp.dot(o, wo_ref[:, sl[hd]], preferred_element_type=f32))
            yield
        y_ref[s, 0] = x + _rmsnorm(jnp.concatenate(c, axis=-1), gpost_ref[...])

    live = [stream(s) for s in range(STREAMS)]
    for _ in range(XATTN_LEAD):
        next(live[0], None)
    while live:
        live = [g for g in live if next(g, True) is None]

    wg_out[...] = wg_ref[...].astype(bf16)
    wu_out[...] = wu_ref[...].astype(bf16)
    wd_out[...] = wd_ref[...].astype(bf16)


def _prompt_xattn(x, gpre, wq, mk, mv, wo, gpost, ffn_w, tile):
    B, L, D = x.shape
    S = STREAMS
    G = B // S
    nt = L // tile
    xs = pl.BlockSpec((S, 1, tile, D), lambda b, i: (0, b, i, 0))
    ms = pl.BlockSpec((S, 1, N_MEM, D), lambda b, i: (0, b, 0, 0))

    def slab(w):
        return pl.BlockSpec((w.shape[0] // (G * nt), w.shape[1]), lambda b, i: (b * nt + i, 0))

    outs = pl.pallas_call(
        _prompt_xattn_kernel, grid=(G, nt),
        in_specs=[xs, _const_spec(gpre.shape), _const_spec(wq.shape), ms, ms,
                  _const_spec(wo.shape), _const_spec(gpost.shape)] + [slab(w) for w in ffn_w],
        out_specs=(xs,) + tuple(slab(w) for w in ffn_w),
        out_shape=(jax.ShapeDtypeStruct((S, G, L, D), f32),)
        + tuple(jax.ShapeDtypeStruct(w.shape, bf16) for w in ffn_w),
        name="prompt_xattn",
        compiler_params=pltpu.CompilerParams(
            dimension_semantics=("arbitrary", "arbitrary"), vmem_limit_bytes=VMEM_LIMIT),
    )(x.reshape(S, G, L, D), gpre, wq, mk.reshape(S, G, N_MEM, D), mv.reshape(S, G, N_MEM, D),
      wo, gpost, *ffn_w)
    return (outs[0].reshape(B, L, D),) + tuple(outs[1:])


def _ffn_kernel(xp_ref, xs_ref, gpre_ref, wg_ref, wu_ref, wd_ref, gpost_ref, yp_ref, ys_ref):
    def rows(x_ref, y_ref):
        x = x_ref[...]
        hf = _rmsnorm(x, gpre_ref[...]).astype(bf16)
        gate = jnp.dot(hf, wg_ref[...], preferred_element_type=f32)
        up = jnp.dot(hf, wu_ref[...], preferred_element_type=f32)
        act = (_silu(gate) * up).astype(bf16)
        f = jnp.dot(act, wd_ref[...], preferred_element_type=f32)
        y_ref[...] = x + _rmsnorm(f, gpost_ref[...])

    i = pl.program_id(0)
    last = pl.num_programs(0) - 1

    @pl.when(i < last)
    def _prompt_rows():
        rows(xp_ref, yp_ref)

    @pl.when(i == last)
    def _sample_rows():
        rows(xs_ref, ys_ref)


def _ffn(xp2d, xs2d, gpre, wg, wu, wd, gpost, tile):
    n, d = xp2d.shape
    assert xs2d.shape == (tile, d)
    steps_p = n // tile
    prow = pl.BlockSpec((tile, d), lambda i: (jnp.minimum(i, steps_p - 1), 0))
    srow = pl.BlockSpec((tile, d), lambda i: (0, 0))
    return pl.pallas_call(
        _ffn_kernel, grid=(steps_p + 1,),
        in_specs=[prow, srow, _const_spec(gpre.shape), _const_spec(wg.shape), _const_spec(wu.shape),
                  _const_spec(wd.shape), _const_spec(gpost.shape)],
        out_specs=(prow, srow),
        out_shape=(jax.ShapeDtypeStruct((n, d), f32), jax.ShapeDtypeStruct((tile, d), f32)),
        name="ffn",
        compiler_params=pltpu.CompilerParams(
            dimension_semantics=("arbitrary",), vmem_limit_bytes=VMEM_LIMIT),
    )(xp2d, xs2d, gpre, wg, wu, wd, gpost)


def _pad_rows(a, rows):
    if a.shape[0] == rows:
        return a
    return jnp.concatenate([a, jnp.zeros((rows - a.shape[0], a.shape[1]), a.dtype)], axis=0)


def _sample_mixer_kernel(x_ref, cprev_ref, st_ref, ck_ref, cv_ref,
                         gpre_ref, win_ref, convw_ref, convb_ref, dtb_ref, alog_ref,
                         dskip_ref, gssd_ref, sinkcol_ref, wout_ref, gpost_ref,
                         y_ref, ssm_ref, cnew_ref, wk_ref, wv_ref, *, bt, steps):
    R = steps * bt
    half = SSD_WIDTH // SSD_GROUPS
    x = x_ref[...].reshape(R, D_MODEL)
    h = _rmsnorm(x, gpre_ref[...]).astype(bf16)
    z = jnp.dot(h, win_ref[:, P_Z:P_XBC], preferred_element_type=f32)
    u = jnp.dot(h, win_ref[:, P_XBC:P_Q], preferred_element_type=f32)
    q = jnp.dot(h, win_ref[:, P_Q:P_K], preferred_element_type=f32)
    k_new = jnp.dot(h, win_ref[:, P_K:P_V], preferred_element_type=f32)
    v_new = jnp.dot(h, win_ref[:, P_V:P_DT], preferred_element_type=f32)
    dt_raw = jnp.dot(h, win_ref[:, P_DT:P_END], preferred_element_type=f32)
    lane = lax.broadcasted_iota(jnp.int32, (1, LANES), 1)
    dt = jnp.where(lane < SSD_HEADS, _softplus(dt_raw + dtb_ref[...]), 0.0)

    def slab(a, t):
        return a[t * bt:(t + 1) * bt]

    HD = ATT_HEAD_DIM
    GH = ATT_HEADS // ATT_KV_HEADS
    GR = GH * R
    reps = GR // bt
    kvg = range(ATT_KV_HEADS)
    keep = WINDOW - steps

    kn_t = _pad_rows(k_new, LANES).T
    vn_t = _pad_rows(v_new, LANES).T

    ridx = lax.broadcasted_iota(jnp.int32, (GR, 1), 0)
    rb = ridx % bt
    rt = (ridx // bt) % steps
    qg = [jnp.concatenate([q[:, (g * GH + hl) * HD:(g * GH + hl + 1) * HD] for hl in range(GH)],
                          axis=0) for g in kvg]
    qg_b = [qg[g].astype(bf16) for g in kvg]
    s_cb = [[_dot(qg_b[g], ck_ref[b, g]) for b in range(bt)] for g in kvg]

    hist = [cprev_ref[j] for j in range(SSD_CONV - 1)] + [slab(u, t) for t in range(steps)]
    xbc_t = []
    for t in range(steps):
        acc = convb_ref[...]
        for j in range(SSD_CONV):
            acc = acc + hist[t + j] * convw_ref[j:j + 1, :]
        xbc_t.append(_silu(acc))
    for j in range(SSD_CONV - 1):
        cnew_ref[j] = hist[steps + j]
    xbc = jnp.concatenate(xbc_t, axis=0)
    xs = xbc[:, 0:SSD_WIDTH]
    bm = xbc[:, SSD_WIDTH:SSD_WIDTH + half]
    cm = xbc[:, SSD_WIDTH + half:SSD_CONV_DIM]
    a_row = -jnp.exp(alog_ref[...])
    adt = dt * a_row
    cs_t = [slab(adt, 0)]
    for t in range(1, steps):
        cs_t.append(cs_t[-1] + slab(adt, t))
    cs = jnp.concatenate(cs_t, axis=0)
    tot = cs_t[-1]
    tot_rows = jnp.concatenate([tot] * steps, axis=0)
    expand = _head_expand_matrix()
    expand2 = jnp.concatenate([expand, expand], axis=0)
    pairs = [(t, s2) for t in range(steps) for s2 in range(t)]
    fac = jnp.concatenate([dt, jnp.exp(cs), jnp.exp(tot_rows - cs), jnp.exp(tot)]
                          + [jnp.exp(cs_t[t] - cs_t[s2]) for t, s2 in pairs], axis=0)
    f_hi, f_lo = _split2(fac)
    fac_e = jnp.dot(jnp.concatenate([f_hi, f_lo], axis=1), expand2, preferred_element_type=f32)
    gr = lax.broadcasted_iota(jnp.int32, (half, SSD_WIDTH), 0)
    gc = lax.broadcasted_iota(jnp.int32, (half, SSD_WIDTH), 1)
    gsum = ((gr >> 7) == (gc >> 8)).astype(bf16)
    gsum2 = jnp.concatenate([gsum, gsum], axis=0)
    cb_pairs = [(t, s2) for t in range(steps) for s2 in range(t + 1)]
    prod = jnp.concatenate([slab(cm, t) * slab(bm, s2) for t, s2 in cb_pairs], axis=0)
    c_hi, c_lo = _split2(prod)
    cb_e = jnp.dot(jnp.concatenate([c_hi, c_lo], axis=1), gsum2, preferred_element_type=f32)

    sel_r = lax.broadcasted_iota(jnp.int32, (LANES, WINDOW), 0)
    sel_l = lax.broadcasted_iota(jnp.int32, (LANES, WINDOW), 1)
    sel = [((sel_r % bt == b) & (sel_r < R) & (sel_l - keep == sel_r // bt)).astype(bf16)
           for b in range(bt)]
    new_k = [_dot_x2(kn_t, sel[b]) for b in range(bt)]
    new_v = [_dot_x2(vn_t, sel[b]) for b in range(bt)]

    jcol = lax.broadcasted_iota(jnp.int32, (GR, WINDOW), 1)
    in_window = jcol > rt
    s_c, m, s_n = [], [], []
    for g in kvg:
        acc = jnp.zeros((GR, WINDOW), f32)
        for b in range(bt):
            acc = jnp.where(rb == b, s_cb[g][b], acc)
        s_c.append(jnp.where(in_window, acc, -jnp.inf))
    sink = [sinkcol_ref[g * GR:(g + 1) * GR, :] for g in kvg]
    for g in kvg:
        mg = jnp.maximum(jnp.max(s_c[g], axis=-1, keepdims=True), sink[g])
        sn_g = []
        for t2 in range(steps):
            kt = jnp.concatenate([slab(k_new, t2)[:, g * HD:(g + 1) * HD]] * reps, axis=0)
            sn = jnp.where(rt >= t2, jnp.sum(qg[g] * kt, axis=-1, keepdims=True), -jnp.inf)
            sn_g.append(sn)
            mg = jnp.maximum(mg, sn)
        m.append(mg)
        s_n.append(sn_g)
    e_c = [jnp.exp(s_c[g] - m[g]) for g in kvg]
    e_n = [[jnp.exp(sn - m[g]) for sn in s_n[g]] for g in kvg]
    rinv = []
    for g in kvg:
        den = jnp.sum(e_c[g], axis=-1, keepdims=True) + jnp.exp(sink[g] - m[g])
        for en in e_n[g]:
            den = den + en
        rinv.append(1.0 / den)
    p_c = [(e_c[g] * rinv[g]).astype(bf16) for g in kvg]

    xdt = xs * fac_e[0:R]
    ecs_e = fac_e[R:2 * R]
    w_end = xdt * fac_e[2 * R:3 * R]
    dec_e = fac_e[3 * R:3 * R + bt]
    pair_decay = {pr: fac_e[3 * R + (n + 1) * bt:3 * R + (n + 2) * bt] for n, pr in enumerate(pairs)}
    y_t = []
    for t in range(steps):
        acc = None
        for s2 in range(t + 1):
            n = cb_pairs.index((t, s2))
            coef = cb_e[n * bt:(n + 1) * bt]
            if s2 < t:
                coef = coef * pair_decay[(t, s2)]
            term = coef * slab(xdt, s2)
            acc = term if acc is None else acc + term
        y_t.append(acc)
    y_intra = jnp.concatenate(y_t, axis=0)
    b_idx = lax.broadcasted_iota(jnp.int32, (bt, 1, LANES), 0)
    l_idx = lax.broadcasted_iota(jnp.int32, (bt, 1, LANES), 2)
    pair = ((l_idx & (bt - 1)) == b_idx) & (l_idx < R)
    own = (l_idx == b_idx)
    gsl = [slice(g * half, (g + 1) * half) for g in range(SSD_GROUPS)]
    h0 = [st_ref[:, gsl[g], :] for g in range(SSD_GROUPS)]
    zz = [_dot_nt(h0[g].reshape(bt * half, SSD_STATE),
                  _pad_rows(cm[:, g * SSD_STATE:(g + 1) * SSD_STATE], LANES)).reshape(bt, half, LANES)
          for g in range(SSD_GROUPS)]
    wt = [_pad_rows(w_end[:, gsl[g]], LANES).T for g in range(SSD_GROUPS)]
    contrib = [_dot(jnp.where(pair, wt[g][None], 0.0).reshape(bt * half, LANES),
                    _pad_rows(bm[:, g * SSD_STATE:(g + 1) * SSD_STATE], LANES)
                    ).reshape(bt, half, SSD_STATE) for g in range(SSD_GROUPS)]
    dec_t = [_pad_rows(dec_e[:, gsl[g]], LANES).T for g in range(SSD_GROUPS)]

    o = []
    for g in kvg:
        pv = [_dot_nt(p_c[g], cv_ref[b, g]) for b in range(bt)]
        og = jnp.zeros((GR, HD), f32)
        for b in range(bt):
            og = jnp.where(rb == b, pv[b], og)
        for t2 in range(steps):
            vt = jnp.concatenate([slab(v_new, t2)[:, g * HD:(g + 1) * HD]] * reps, axis=0)
            og = og + (e_n[g][t2] * rinv[g]) * vt
        o.append(og)

    lane_w = lax.broadcasted_iota(jnp.int32, (1, WINDOW), 1)
    for b in range(bt):
        for g in kvg:
            gs = slice(g * HD, (g + 1) * HD)
            wk_ref[b, g] = jnp.where(lane_w < keep, pltpu.roll(ck_ref[b, g], keep, 1), new_k[b][gs])
            wv_ref[b, g] = jnp.where(lane_w < keep, pltpu.roll(cv_ref[b, g], keep, 1), new_v[b][gs])

    y_off_parts = []
    for g in range(SSD_GROUPS):
        yt = jnp.sum(jnp.where(pair, zz[g], 0.0), axis=0)
        y_off_parts.append(yt.T[0:R, :])
        dec = jnp.sum(jnp.where(own, dec_t[g][None], 0.0), axis=-1, keepdims=True)
        ssm_ref[:, gsl[g], :] = h0[g] * dec + contrib[g]
    y_off = jnp.concatenate(y_off_parts, axis=-1) * ecs_e
    y = y_intra + y_off + xs * dskip_ref[...]
    y_ssd = _gated_group_norm(y, z, gssd_ref[...])

    mix = jnp.dot(y_ssd.astype(bf16), wout_ref[0:SSD_WIDTH, :], preferred_element_type=f32)
    for g in kvg:
        for hl in range(GH):
            hd = g * GH + hl
            mix = mix + jnp.dot(o[g][hl * R:(hl + 1) * R].astype(bf16),
                                wout_ref[SSD_WIDTH + hd * HD:SSD_WIDTH + (hd + 1) * HD, :],
                                preferred_element_type=f32)
    y_ref[...] = (x + _rmsnorm(mix, gpost_ref[...])).reshape(steps, bt, D_MODEL)


def _sample_mixer(x_tm, cprev_tm, st, ck, cv, gpre, win, convw, convb, dtb, alog, dskip_e, gssd,
                  sinkcol, wout, gpost, bt):
    steps, nb, D = x_tm.shape
    kern = functools.partial(_sample_mixer_kernel, bt=bt, steps=steps)
    tm = lambda w: pl.BlockSpec((steps, bt, w), lambda i: (0, i, 0))
    win_spec = pl.BlockSpec((bt, ATT_KV_HEADS, ATT_HEAD_DIM, WINDOW), lambda i: (i, 0, 0, 0))
    in_specs = [
        tm(D),
        pl.BlockSpec((SSD_CONV - 1, bt, SSD_CONV_DIM), lambda i: (0, i, 0)),
        pl.BlockSpec((bt, SSD_WIDTH, SSD_STATE), lambda i: (i, 0, 0)),
        win_spec, win_spec,
    ] + [_const_spec(a.shape) for a in (gpre, win, convw, convb, dtb, alog, dskip_e, gssd,
                                        sinkcol, wout, gpost)]
    out_specs = (
        tm(D),
        pl.BlockSpec((bt, SSD_WIDTH, SSD_STATE), lambda i: (i, 0, 0)),
        pl.BlockSpec((SSD_CONV - 1, bt, SSD_CONV_DIM), lambda i: (0, i, 0)),
        win_spec, win_spec,
    )
    out_shape = (
        jax.ShapeDtypeStruct((steps, nb, D), f32),
        jax.ShapeDtypeStruct((nb, SSD_WIDTH, SSD_STATE), f32),
        jax.ShapeDtypeStruct((SSD_CONV - 1, nb, SSD_CONV_DIM), f32),
        jax.ShapeDtypeStruct(ck.shape, f32),
        jax.ShapeDtypeStruct(cv.shape, f32),
    )
    return pl.pallas_call(
        kern, grid=(nb // bt,), in_specs=in_specs, out_specs=out_specs, out_shape=out_shape,
        name="sample_mixer",
        compiler_params=pltpu.CompilerParams(
            dimension_semantics=("arbitrary",), vmem_limit_bytes=VMEM_LIMIT),
    )(x_tm, cprev_tm, st, ck, cv, gpre, win, convw, convb, dtb, alog, dskip_e, gssd, sinkcol,
      wout, gpost)


def _sample_xattn_kernel(x_ref, mk_ref, mv_ref, gpre_ref, wq_ref, wo_ref, gpost_ref, y_ref,
                         q_sc, o_sc, *, bt, steps):
    i = pl.program_id(0)
    nb = x_ref.shape[1]
    R = steps * bt
    nrow = bt * N_MEM * X_HEADS

    @pl.when(i == 0)
    def _project_queries():
        x_all = x_ref[...].reshape(steps * nb, D_MODEL)
        hn = _rmsnorm(x_all, gpre_ref[...]).astype(bf16)
        q_sc[...] = jnp.dot(hn, wq_ref[...], preferred_element_type=f32)

    tile_rows = [pl.ds(pl.multiple_of(t * nb + i * bt, bt), bt) for t in range(steps)]
    q = jnp.concatenate([q_sc[r, :] for r in tile_rows], axis=0)
    qs = jnp.concatenate([q[:, hd * X_HEAD_DIM:(hd + 1) * X_HEAD_DIM] for hd in range(X_HEADS)],
                         axis=0)
    kall = mk_ref[...].reshape(nrow, X_HEAD_DIM)
    vall = mv_ref[...].reshape(nrow, X_HEAD_DIM)
    ncol = X_HEADS * R
    seq_rows = N_MEM * X_HEADS
    z = _dot_nt(kall, qs).reshape(bt, seq_rows, ncol)
    v_b = [vall[b * seq_rows:(b + 1) * seq_rows].astype(bf16) for b in range(bt)]
    b_i = lax.broadcasted_iota(jnp.int32, (bt, 1, ncol), 0)
    c_i = lax.broadcasted_iota(jnp.int32, (bt, 1, ncol), 2)
    zc = jnp.sum(jnp.where(c_i % bt == b_i, z, 0.0), axis=0)
    zc = zc.reshape(seq_rows // SUBLANES, SUBLANES, ncol)
    r_h = lax.broadcasted_iota(jnp.int32, (1, SUBLANES, ncol), 1) % X_HEADS
    c_h = lax.broadcasted_iota(jnp.int32, (1, SUBLANES, ncol), 2) // R
    zc = jnp.where(r_h == c_h, zc, -jnp.inf).reshape(seq_rows, ncol)
    m = jnp.max(zc, axis=0, keepdims=True)
    e = jnp.exp(zc - m)
    p = e * (1.0 / jnp.sum(e, axis=0, keepdims=True))
    col_b = lax.broadcasted_iota(jnp.int32, (1, ncol), 1) % bt
    tn = (((0,), (0,)), ((), ()))
    o = None
    for b in range(bt):
        p_b = jnp.where(col_b == b, p, 0.0).astype(bf16)
        o_b = lax.dot_general(p_b, v_b[b], tn, preferred_element_type=f32)
        o = o_b if o is None else o + o_b
    o = jnp.concatenate([o[hd * R:(hd + 1) * R] for hd in range(X_HEADS)], axis=-1)
    for t in range(steps):
        o_sc[tile_rows[t], :] = o[t * bt:(t + 1) * bt]

    @pl.when(i == pl.num_programs(0) - 1)
    def _project_outputs():
        x_all = x_ref[...].reshape(steps * nb, D_MODEL)
        cc = _dot(o_sc[...], wo_ref[...])
        y_ref[...] = (x_all + _rmsnorm(cc, gpost_ref[...])).reshape(steps, nb, D_MODEL)


def _sample_xattn(x_tm, mk, mv, gpre, wq, wo, gpost, bt):
    steps, nb, D = x_tm.shape
    kern = functools.partial(_sample_xattn_kernel, bt=bt, steps=steps)
    xs = _const_spec(x_tm.shape)
    ms = pl.BlockSpec((bt, N_MEM, X_HEADS, X_HEAD_DIM), lambda i: (i, 0, 0, 0))
    return pl.pallas_call(
        kern, grid=(nb // bt,),
        in_specs=[xs, ms, ms, _const_spec(gpre.shape), _const_spec(wq.shape),
                  _const_spec(wo.shape), _const_spec(gpost.shape)],
        out_specs=xs, out_shape=jax.ShapeDtypeStruct((steps, nb, D), f32),
        scratch_shapes=[pltpu.VMEM((steps * nb, D), f32), pltpu.VMEM((steps * nb, D), f32)],
        name="sample_xattn",
        compiler_params=pltpu.CompilerParams(
            dimension_semantics=("arbitrary",), vmem_limit_bytes=VMEM_LIMIT),
    )(x_tm, mk, mv, gpre, wq, wo, gpost)


def _win_prep_kernel(wt_ref, o_ref):
    piece = 2 * LANES

    def put(src_lo, n, dst_lo, scale=None):
        for c in range(0, n, piece):
            t = wt_ref[src_lo + c:src_lo + c + piece, :].T
            if scale is not None:
                t = t * scale
            o_ref[:, dst_lo + c:dst_lo + c + piece] = t.astype(bf16)

    put(W_Z, P_Q - P_Z, P_Z)
    put(W_Q, P_K - P_Q, P_Q, ATT_SCALE)
    put(W_K, P_DT - P_K, P_K)
    dt_rows = jnp.concatenate([wt_ref[W_DT:W_DT + SSD_HEADS, :],
                               jnp.zeros((LANES - SSD_HEADS, D_MODEL), f32)], axis=0)
    o_ref[:, P_DT:P_END] = dt_rows.T.astype(bf16)


def _win_prep(w_t):
    return pl.pallas_call(
        _win_prep_kernel, out_shape=jax.ShapeDtypeStruct((D_MODEL, P_END), bf16),
        name="win_prep",
        compiler_params=pltpu.CompilerParams(vmem_limit_bytes=VMEM_LIMIT),
    )(w_t)


def _row(v, width=None):
    v = v.reshape(1, -1).astype(f32)
    if width is not None and v.shape[1] < width:
        v = jnp.pad(v, ((0, 0), (0, width - v.shape[1])))
    return v


def kernel(x_prompt, x_sample, state_ssm, state_conv, cache_win_k, cache_win_v, cache_mem_k, cache_mem_v, mem_prompt, g_mix_pre, w_in, conv_w, conv_b, dt_bias, a_log, d_skip, g_ssd_norm, sinks, w_out, g_mix_post, g_x_pre, w_xq, g_mem, w_xk, w_xv, w_xo, g_x_post, g_ffn_pre, w_gate, w_up, w_down, g_ffn_post):
    depth = w_in.shape[0]
    assert depth == 1
    B, L, D = x_prompt.shape
    NB, steps, _ = x_sample.shape
    li = 0

    win_p = _win_prep(jnp.transpose(w_in[li]))
    wo_b = w_out[li].astype(bf16)

    gpre, gpost = _row(g_mix_pre[li]), _row(g_mix_post[li])
    convw, convb = conv_w[li].astype(f32), _row(conv_b[li])
    dtb, alog = _row(dt_bias[li], LANES), _row(a_log[li], LANES)
    dskip_e = _row(jnp.repeat(d_skip[li], SSD_HEAD_DIM))
    gssd = _row(g_ssd_norm[li])
    sk = sinks[li].astype(f32)

    mk2d, mv2d, mk4, mv4 = _memkv(mem_prompt.reshape(B * N_MEM, D), _row(g_mem[li]),
                                  w_xk[li].astype(bf16), w_xv[li].astype(bf16), tile=512)
    mk3, mv3 = mk2d.reshape(B, N_MEM, D), mv2d.reshape(B, N_MEM, D)
    x1, p_ssm, p_conv8, p_wk, p_wv = _prompt_mixer(
        x_prompt, gpre, win_p, convw, convb, dtb, alog, dskip_e, gssd, sk, wo_b, gpost, tile=512)
    wxq_b, wxo_b = (w_xq[li] * X_SCALE).astype(bf16), w_xo[li].astype(bf16)
    gxpre, gxpost = _row(g_x_pre[li]), _row(g_x_post[li])
    x2, wg_b, wu_b, wd_b = _prompt_xattn(x1, gxpre, wxq_b, mk3, mv3, wxo_b, gxpost,
                                         (w_gate[li], w_up[li], w_down[li]), tile=512)
    gfpre, gfpost = _row(g_ffn_pre[li]), _row(g_ffn_post[li])

    bt, bt_mix = 8, 16
    x_tm = jnp.transpose(x_sample, (1, 0, 2))
    cprev_tm = jnp.transpose(state_conv[li], (1, 0, 2))
    st = state_ssm[li].reshape(NB, SSD_WIDTH, SSD_STATE)
    ck = jnp.transpose(cache_win_k[li], (0, 2, 3, 1))
    cv = jnp.transpose(cache_win_v[li], (0, 2, 3, 1))
    sinkcol = jnp.repeat(sk, steps * bt_mix).reshape(ATT_HEADS * steps * bt_mix, 1)
    x1s, s_ssm, cnew_tm, s_wk, s_wv = _sample_mixer(
        x_tm, cprev_tm, st, ck, cv, gpre, win_p, convw, convb, dtb, alog, dskip_e, gssd,
        sinkcol, wo_b, gpost, bt=bt_mix)
    cmk = cache_mem_k.reshape(NB, N_MEM, X_HEADS, X_HEAD_DIM)
    cmv = cache_mem_v.reshape(NB, N_MEM, X_HEADS, X_HEAD_DIM)
    x2s = _sample_xattn(x1s, cmk, cmv, gxpre, wxq_b, wxo_b, gxpost, bt=bt)
    yp2d, ys_tm = _ffn(x2.reshape(B * L, D), x2s.reshape(steps * NB, D), gfpre, wg_b, wu_b, wd_b,
                       gfpost, tile=steps * NB)
    yp = yp2d.reshape(B, L, D)
    ys = jnp.transpose(ys_tm.reshape(steps, NB, D), (1, 0, 2))

    s_conv = jnp.transpose(cnew_tm, (1, 0, 2))
    kv_shape = (ATT_KV_HEADS, ATT_HEAD_DIM)
    return (
        yp, ys,
        p_ssm.reshape(1, B, SSD_HEADS, SSD_HEAD_DIM, SSD_STATE),
        p_conv8[:, SUBLANES - (SSD_CONV - 1):, :][None],
        jnp.transpose(p_wk.reshape(B, *kv_shape, WINDOW), (0, 3, 1, 2))[None],
        jnp.transpose(p_wv.reshape(B, *kv_shape, WINDOW), (0, 3, 1, 2))[None],
        mk4.reshape(1, B, N_MEM, X_HEADS, X_HEAD_DIM), mv4.reshape(1, B, N_MEM, X_HEADS, X_HEAD_DIM),
        s_ssm.reshape(1, NB, SSD_HEADS, SSD_HEAD_DIM, SSD_STATE),
        s_conv[None],
        jnp.transpose(s_wk, (0, 3, 1, 2))[None], jnp.transpose(s_wv, (0, 3, 1, 2))[None],
    )
```

```python
import functools

import jax
import jax.numpy as jnp
from jax import lax
from jax.experimental import pallas as pl
from jax.experimental.pallas import tpu as pltpu

f32 = jnp.float32
bf16 = jnp.bfloat16

D_MODEL = 1024
EPS = 1e-6
N_MEM = 256
SSD_HEADS = 8
SSD_HEAD_DIM = 64
SSD_WIDTH = 512
SSD_GROUPS = 2
SSD_STATE = 128
SSD_CONV = 4
SSD_CHUNK = 128
SSD_CONV_DIM = 1024
ATT_HEADS = 8
ATT_KV_HEADS = 2
ATT_HEAD_DIM = 64
ATT_WIDTH = 512
ATT_KV_WIDTH = 128
WINDOW = 128
ATT_SCALE = ATT_HEAD_DIM ** -0.5
X_HEADS = 4
X_HEAD_DIM = 256
X_SCALE = X_HEAD_DIM ** -0.5
D_FF = 2816
LANES = 128
SUBLANES = 8
VMEM_LIMIT = 56 * 1024 * 1024
STREAMS = 2
PROJ_PIECES, SCAN_PIECES, OUT_PIECES = 19, 10, 5
CONV_BLOCKS = SSD_CONV_DIM // LANES
XATTN_LEAD = 5

P_Z, P_XBC, P_Q, P_K, P_V, P_DT, P_END = 0, 512, 1536, 2048, 2176, 2304, 2432
W_Z, W_DT, W_Q, W_K = 0, 1536, 1544, 2056


def _dot(a, b):
    return jnp.dot(a.astype(bf16), b.astype(bf16), preferred_element_type=f32)


def _dot_nt(a, b):
    return lax.dot_general(a.astype(bf16), b.astype(bf16), (((1,), (1,)), ((), ())),
                           preferred_element_type=f32)


def _split2(x):
    hi = x.astype(bf16)
    lo = (x - hi.astype(f32)).astype(bf16)
    return hi, lo


def _dot_x2(x, m):
    hi, lo = _split2(x)
    return (jnp.dot(hi, m, preferred_element_type=f32)
            + jnp.dot(lo, m, preferred_element_type=f32))


def _rmsnorm(x, g):
    ms = jnp.mean(x * x, axis=-1, keepdims=True)
    return x * lax.rsqrt(ms + EPS) * g


def _silu(x):
    return x * jax.nn.sigmoid(x)


def _softplus(x):
    return jnp.maximum(x, 0.0) + jnp.log1p(jnp.exp(-jnp.abs(x)))


def _head_expand_matrix():
    r = lax.broadcasted_iota(jnp.int32, (LANES, SSD_WIDTH), 0)
    c = lax.broadcasted_iota(jnp.int32, (LANES, SSD_WIDTH), 1)
    return (r == (c >> 6)).astype(bf16)


def _gated_group_norm(y, z, g):
    u = y * _silu(z)
    half = SSD_WIDTH // SSD_GROUPS
    parts = []
    for gi in range(SSD_GROUPS):
        ug = u[:, gi * half:(gi + 1) * half]
        parts.append(ug * lax.rsqrt(jnp.mean(ug * ug, axis=-1, keepdims=True) + EPS))
    return jnp.concatenate(parts, axis=-1) * g


def _prompt_mixer_kernel(x_ref, gpre_ref, win_ref, convw_ref, convb_ref, dtb_ref, alog_ref,
                         dskip_ref, gssd_ref, sinks_ref, wout_ref, gpost_ref,
                         y_ref, ssm_ref, conv_ref, wk_ref, wv_ref,
                         statet_sc, xbc_ext_sc, xbc_sc, z_sc, q_sc, k_sc, v_sc, dt_sc,
                         kprev_sc, vprev_sc, mix_sc, *, tile):
    i = pl.program_id(1)
    NC = tile // SSD_CHUNK
    ns = range(NC)
    C = SSD_CHUNK
    PW = 2 * LANES

    @pl.when(i == 0)
    def _init():
        statet_sc[...] = jnp.zeros_like(statet_sc)
        xbc_ext_sc[:, :, 0:SUBLANES, :] = jnp.zeros((STREAMS, CONV_BLOCKS, SUBLANES, LANES), f32)
        kprev_sc[...] = jnp.zeros_like(kprev_sc)
        vprev_sc[...] = jnp.zeros_like(vprev_sc)

    lane = lax.broadcasted_iota(jnp.int32, (1, LANES), 1)
    a_row = -jnp.exp(alog_ref[...])
    expand = _head_expand_matrix()
    expand2 = jnp.concatenate([expand, expand], axis=0)
    row_i = lax.broadcasted_iota(jnp.int32, (C, C), 0)
    col_i = lax.broadcasted_iota(jnp.int32, (C, C), 1)
    lower = col_i <= row_i
    tri = lower.astype(bf16)
    tri3 = jnp.concatenate([tri, tri, tri], axis=1)
    lo_half = lane < ATT_HEAD_DIM
    half = SSD_WIDTH // SSD_GROUPS
    head_of_lane = lax.broadcasted_iota(jnp.int32, (1, half), 1) >> 6
    rows = [slice(n * C, (n + 1) * C) for n in ns]

    def stream(s):
        statet_s, xbc_ext_s, xbc_s, z_s = statet_sc.at[s], xbc_ext_sc.at[s], xbc_sc.at[s], z_sc.at[s]
        q_s, k_s, v_s, dt_s = q_sc.at[s], k_sc.at[s], v_sc.at[s], dt_sc.at[s]
        kprev_s, vprev_s, mix_s = kprev_sc.at[s], vprev_sc.at[s], mix_sc.at[s]

        x = x_ref[s, 0]
        h = _rmsnorm(x, gpre_ref[...]).astype(bf16)
        yield

        def proj(col):
            return jnp.dot(h, win_ref[:, col:col + PW], preferred_element_type=f32)

        def conv_cols(cb):
            cols = slice(cb * LANES, (cb + 1) * LANES)
            acc = convb_ref[:, cols]
            for j in range(SSD_CONV):
                off = SUBLANES - (SSD_CONV - 1) + j
                acc = acc + xbc_ext_s[cb, off:off + tile, :] * convw_ref[j:j + 1, cols]
            xbc_s[cb] = _silu(acc)

        def xbc_piece(pc):
            res = proj(P_XBC + pc * PW)
            for half_pc in range(PW // LANES):
                xbc_ext_s[(PW // LANES) * pc + half_pc, SUBLANES:SUBLANES + tile, :] = (
                    res[:, half_pc * LANES:(half_pc + 1) * LANES])

        xbc_piece(0)
        yield
        xbc_piece(1)
        yield
        conv_cols(0)
        yield
        xbc_piece(2)
        yield
        conv_cols(1)
        yield
        xbc_piece(3)
        yield
        conv_cols(2)
        yield
        z_s[:, 0:PW] = proj(P_Z)
        yield
        conv_cols(3)
        yield
        z_s[:, PW:2 * PW] = proj(P_Z + PW)
        yield
        conv_cols(4)
        yield
        q_s[:, 0:PW] = proj(P_Q)
        yield
        conv_cols(5)
        yield
        q_s[:, PW:2 * PW] = proj(P_Q + PW)
        yield
        conv_cols(6)
        yield
        kv = proj(P_K)
        k_s[...] = kv[:, 0:LANES]
        v_s[...] = kv[:, LANES:PW]
        yield
        conv_cols(7)
        yield
        dt_raw = jnp.dot(h, win_ref[:, P_DT:P_END], preferred_element_type=f32)
        dt_s[...] = jnp.where(lane < SSD_HEADS, _softplus(dt_raw + dtb_ref[...]), 0.0)
        tail = xbc_ext_s[:, tile:tile + SUBLANES, :]
        conv_ref[s, 0] = jnp.concatenate([tail[cb] for cb in range(CONV_BLOCKS)], axis=-1)
        xbc_ext_s[:, 0:SUBLANES, :] = tail
        yield

        GH = SSD_HEADS // SSD_GROUPS
        heads = range(ATT_HEADS)
        gs = range(SSD_GROUPS)
        zero_b = jnp.zeros((), bf16)

        yield
        def xbc_cols(n, lo, hi):
            return jnp.concatenate([xbc_s[cb, rows[n], :] for cb in range(lo // LANES, hi // LANES)],
                                   axis=-1)
        xs = [xbc_cols(n, 0, SSD_WIDTH) for n in ns]
        bm = [xbc_cols(n, SSD_WIDTH, SSD_WIDTH + half) for n in ns]
        cm = [xbc_cols(n, SSD_WIDTH + half, SSD_CONV_DIM) for n in ns]
        dtc = [dt_s[rows[n], :] for n in ns]
        cs = []
        for n in ns:
            adt = dtc[n] * a_row
            a_hi = adt.astype(bf16)
            a_r1 = adt - a_hi.astype(f32)
            a_mid = a_r1.astype(bf16)
            a_lo = (a_r1 - a_mid.astype(f32)).astype(bf16)
            cs.append(jnp.dot(tri3, jnp.concatenate([a_hi, a_mid, a_lo], axis=0),
                              preferred_element_type=f32))

        yield
        q = [q_s[rows[n], :].astype(bf16) for n in ns]
        k = [k_s[rows[n], :] for n in ns]
        v = [v_s[rows[n], :] for n in ns]
        k_prev = [kprev_s[...]] + k[:-1]
        v_prev = [vprev_s[...]] + v[:-1]
        kprev_s[...] = k[-1]
        vprev_s[...] = v[-1]
        first_bias = jnp.where(i > 0, 0.0, -jnp.inf)
        k_lo, k_hi, v_lo, v_hi = [], [], [], []
        for n in ns:
            kk = jnp.concatenate([k_prev[n], k[n]], axis=0)
            vv = jnp.concatenate([v_prev[n], v[n]], axis=0)
            kk_r = pltpu.roll(kk, ATT_HEAD_DIM, 1)
            vv_r = pltpu.roll(vv, ATT_HEAD_DIM, 1)
            k_lo.append([jnp.where(lo_half, kk, 0.0).astype(bf16), jnp.where(lo_half, kk_r, 0.0).astype(bf16)])
            k_hi.append([jnp.where(lo_half, 0.0, kk_r).astype(bf16), jnp.where(lo_half, 0.0, kk).astype(bf16)])
            v_lo.append([jnp.where(lo_half, vv, 0.0).astype(bf16), jnp.where(lo_half, vv_r, 0.0).astype(bf16)])
            v_hi.append([jnp.where(lo_half, 0.0, vv_r).astype(bf16), jnp.where(lo_half, 0.0, vv).astype(bf16)])
        s_g = [[_dot_nt(jnp.concatenate([q[n][:, (2 * g) * LANES:(2 * g + 1) * LANES],
                                         q[n][:, (2 * g + 1) * LANES:(2 * g + 2) * LANES]], axis=0),
                        jnp.concatenate([k_lo[n][g], k_hi[n][g]], axis=0))
                for g in range(ATT_KV_HEADS)] for n in ns]

        yield
        cs_t = [cs[n].T for n in ns]
        fac_e, cb_all, bm_t = [], [], []
        for n in ns:
            tot = cs[n][C - 1:C, :]
            fac = jnp.concatenate([dtc[n], jnp.exp(cs[n]), jnp.exp(tot - cs[n])], axis=0)
            f_hi = fac.astype(bf16)
            f_lo = (fac - f_hi.astype(f32)).astype(bf16)
            fac_e.append(jnp.dot(jnp.concatenate([f_hi, f_lo], axis=1), expand2,
                                 preferred_element_type=f32))
            cb_all.append(_dot_nt(
                jnp.concatenate([cm[n][:, 0:SSD_STATE], cm[n][:, SSD_STATE:half]], axis=0),
                jnp.concatenate([bm[n][:, 0:SSD_STATE], bm[n][:, SSD_STATE:half]], axis=0)))
            bm_t.append(bm[n].T)

        yield
        def head_scores(n, hd):
            g, jb, sub = hd // 4, (hd // 2) % 2, hd % 2
            s_prev = s_g[n][g][jb * C:(jb + 1) * C, (2 * sub) * C:(2 * sub + 1) * C]
            s_cur = s_g[n][g][jb * C:(jb + 1) * C, (2 * sub + 1) * C:(2 * sub + 2) * C]
            return jnp.where(lower, s_cur, s_prev + first_bias if n == 0 else s_prev)
        sc = [[head_scores(n, hd) for hd in heads] for n in ns]
        sink = [sinks_ref[hd] for hd in heads]
        m = [[jnp.maximum(jnp.max(sc[n][hd], axis=-1, keepdims=True), sink[hd]) for hd in heads]
             for n in ns]

        yield
        xdt = [xs[n] * fac_e[n][0:C] for n in ns]
        ecs_e = [fac_e[n][C:2 * C] for n in ns]
        w_end = [xdt[n] * fac_e[n][2 * C:3 * C] for n in ns]
        xdt_b = [xdt[n].astype(bf16) for n in ns]
        contrib = [[_dot(bm_t[n][g * SSD_STATE:(g + 1) * SSD_STATE, :],
                         w_end[n][:, g * half:(g + 1) * half]) for g in gs] for n in ns]
        decay = [[jnp.exp(jnp.where(lower, cs[n][:, hh:hh + 1] - cs_t[n][hh:hh + 1, :], -jnp.inf))
                  for hh in range(SSD_HEADS)] for n in ns]

        yield
        e = [[jnp.exp(sc[n][hd] - m[n][hd]) for hd in heads] for n in ns]
        den = [[jnp.sum(e[n][hd], axis=-1, keepdims=True) + jnp.exp(sink[hd] - m[n][hd])
                for hd in heads] for n in ns]

        yield
        y_d = []
        for n in ns:
            y_n = []
            for g in gs:
                gl = slice(g * half, (g + 1) * half)
                cb = cb_all[n][g * C:(g + 1) * C, g * C:(g + 1) * C]
                m_parts = [(cb * decay[n][g * GH + r]).astype(bf16) for r in range(GH)]
                x_parts = [jnp.where(head_of_lane == r, xdt_b[n][:, gl], zero_b) for r in range(GH)]
                y_n.append(jnp.dot(jnp.concatenate(m_parts, axis=1), jnp.concatenate(x_parts, axis=0),
                                   preferred_element_type=f32))
            y_d.append(y_n)
        st = [statet_s[g] for g in gs]
        y_off = []
        for n in ns:
            y_off.append([_dot(cm[n][:, g * SSD_STATE:(g + 1) * SSD_STATE], st[g]) for g in gs])
            st = [st[g] * ecs_e[n][C - 1:C, g * half:(g + 1) * half] + contrib[n][g] for g in gs]
        for g in gs:
            statet_s[g] = st[g]

        yield
        p = [[(e[n][hd] * (1.0 / den[n][hd])).astype(bf16) for hd in heads] for n in ns]
        o_g = []
        for n in ns:
            o_n = []
            for g in range(ATT_KV_HEADS):
                p_rows = []
                for jb in range(2):
                    p_cols = []
                    for sub in range(2):
                        ph = p[n][g * 4 + jb * 2 + sub]
                        p_cols += [jnp.where(lower, zero_b, ph), jnp.where(lower, ph, zero_b)]
                    p_rows.append(jnp.concatenate(p_cols, axis=1))
                o_n.append(jnp.dot(jnp.concatenate(p_rows, axis=0),
                                   jnp.concatenate([v_lo[n][g], v_hi[n][g]], axis=0),
                                   preferred_element_type=f32))
            o_g.append(o_n)

        yield
        for n in ns:
            y = (jnp.concatenate([y_d[n][g] + y_off[n][g] * ecs_e[n][:, g * half:(g + 1) * half]
                                  for g in gs], axis=-1)
                 + xs[n] * dskip_ref[...])
            y_ssd = _gated_group_norm(y, z_s[rows[n], :], gssd_ref[...])
            mix_s[rows[n], 0:SSD_WIDTH] = y_ssd.astype(bf16)
        for n in ns:
            for g in range(ATT_KV_HEADS):
                for jb in range(2):
                    lo_l = SSD_WIDTH + (2 * g + jb) * LANES
                    mix_s[rows[n], lo_l:lo_l + LANES] = o_g[n][g][jb * C:(jb + 1) * C].astype(bf16)

        yield

        mix_in = mix_s[...]
        mix = []
        for pc in range(D_MODEL // PW):
            mix.append(jnp.dot(mix_in, wout_ref[:, pc * PW:(pc + 1) * PW], preferred_element_type=f32))
            yield
        y_ref[s, 0] = x + _rmsnorm(jnp.concatenate(mix, axis=-1), gpost_ref[...])

    def mixed(ga, na, gb, nb):
        done_b = 0
        for ka in range(na):
            next(ga, None)
            want_b = ((ka + 1) * nb) // na
            for _ in range(want_b - done_b):
                next(gb, None)
            done_b = want_b

    g0, g1 = [stream(s) for s in range(STREAMS)]
    for _ in range(PROJ_PIECES):
        next(g0, None)
    mixed(g0, SCAN_PIECES, g1, PROJ_PIECES)
    mixed(g1, SCAN_PIECES, g0, OUT_PIECES)
    for g in (g0, g1):
        for _ in g:
            pass

    @pl.when(i == pl.num_programs(1) - 1)
    def _final_state():
        for s in range(STREAMS):
            for g in range(SSD_GROUPS):
                ssm_ref[s, 0, g * half:(g + 1) * half, :] = statet_sc[s, g].T
            wk_ref[s, 0] = k_sc[s, tile - WINDOW:tile, :].T
            wv_ref[s, 0] = v_sc[s, tile - WINDOW:tile, :].T


def _const_spec(shape):
    nd = len(shape)
    return pl.BlockSpec(shape, lambda *_: (0,) * nd)


def _prompt_mixer(x, gpre, win, convw, convb, dtb, alog, dskip_e, gssd, sinks, wout, gpost, tile):
    B, L, D = x.shape
    S = STREAMS
    G = B // S
    kern = functools.partial(_prompt_mixer_kernel, tile=tile)

    def per_seq(rows, width):
        return pl.BlockSpec((S, 1, rows, width), lambda b, i: (0, b, 0, 0))

    tile_spec = pl.BlockSpec((S, 1, tile, D), lambda b, i: (0, b, i, 0))
    out_shape = (
        jax.ShapeDtypeStruct((S, G, L, D), f32),
        jax.ShapeDtypeStruct((S, G, SSD_WIDTH, SSD_STATE), f32),
        jax.ShapeDtypeStruct((S, G, SUBLANES, SSD_CONV_DIM), f32),
        jax.ShapeDtypeStruct((S, G, ATT_KV_WIDTH, WINDOW), f32),
        jax.ShapeDtypeStruct((S, G, ATT_KV_WIDTH, WINDOW), f32),
    )
    in_specs = [
        tile_spec,
        _const_spec(gpre.shape), _const_spec(win.shape), _const_spec(convw.shape),
        _const_spec(convb.shape), _const_spec(dtb.shape), _const_spec(alog.shape),
        _const_spec(dskip_e.shape), _const_spec(gssd.shape),
        pl.BlockSpec(memory_space=pltpu.SMEM),
        _const_spec(wout.shape), _const_spec(gpost.shape),
    ]
    out_specs = (
        tile_spec,
        per_seq(SSD_WIDTH, SSD_STATE),
        per_seq(SUBLANES, SSD_CONV_DIM),
        per_seq(ATT_KV_WIDTH, WINDOW),
        per_seq(ATT_KV_WIDTH, WINDOW),
    )
    scratch = [
        pltpu.VMEM((S, SSD_GROUPS, SSD_STATE, SSD_WIDTH // SSD_GROUPS), f32),
        pltpu.VMEM((S, CONV_BLOCKS, tile + 2 * SUBLANES, LANES), f32),
        pltpu.VMEM((S, CONV_BLOCKS, tile, LANES), f32),
        pltpu.VMEM((S, tile, SSD_WIDTH), f32),
        pltpu.VMEM((S, tile, ATT_WIDTH), f32),
        pltpu.VMEM((S, tile, ATT_KV_WIDTH), f32),
        pltpu.VMEM((S, tile, ATT_KV_WIDTH), f32),
        pltpu.VMEM((S, tile, LANES), f32),
        pltpu.VMEM((S, WINDOW, ATT_KV_WIDTH), f32),
        pltpu.VMEM((S, WINDOW, ATT_KV_WIDTH), f32),
        pltpu.VMEM((S, tile, 2 * SSD_WIDTH), bf16),
    ]
    outs = pl.pallas_call(
        kern, grid=(G, L // tile), in_specs=in_specs, out_specs=out_specs, out_shape=out_shape,
        scratch_shapes=scratch, name="prompt_mixer",
        compiler_params=pltpu.CompilerParams(
            dimension_semantics=("arbitrary", "arbitrary"), vmem_limit_bytes=VMEM_LIMIT),
    )(x.reshape(S, G, L, D), gpre, win, convw, convb, dtb, alog, dskip_e, gssd, sinks, wout, gpost)
    return tuple(o.reshape(B, *o.shape[2:]) for o in outs)


def _memkv_kernel(m_ref, g_ref, wk_ref, wv_ref, k_ref, v_ref, kh_ref, vh_ref):
    mn = _rmsnorm(m_ref[...], g_ref[...]).astype(bf16)
    k = jnp.dot(mn, wk_ref[...], preferred_element_type=f32)
    v = jnp.dot(mn, wv_ref[...], preferred_element_type=f32)
    k_ref[...] = k.astype(bf16)
    v_ref[...] = v.astype(bf16)
    for hd in range(X_HEADS):
        kh_ref[:, hd, :] = k[:, hd * X_HEAD_DIM:(hd + 1) * X_HEAD_DIM]
        vh_ref[:, hd, :] = v[:, hd * X_HEAD_DIM:(hd + 1) * X_HEAD_DIM]


def _memkv(mem2d, g, wk, wv, tile):
    n, d = mem2d.shape
    row = pl.BlockSpec((tile, d), lambda i: (i, 0))
    hrow = pl.BlockSpec((tile, X_HEADS, X_HEAD_DIM), lambda i: (i, 0, 0))
    flat = jax.ShapeDtypeStruct((n, d), bf16)
    heads = jax.ShapeDtypeStruct((n, X_HEADS, X_HEAD_DIM), f32)
    return pl.pallas_call(
        _memkv_kernel, grid=(n // tile,),
        in_specs=[row, _const_spec(g.shape), _const_spec(wk.shape), _const_spec(wv.shape)],
        out_specs=(row, row, hrow, hrow),
        out_shape=(flat, flat, heads, heads),
        name="memory_kv",
        compiler_params=pltpu.CompilerParams(
            dimension_semantics=("arbitrary",), vmem_limit_bytes=VMEM_LIMIT),
    )(mem2d, g, wk, wv)


def _prompt_xattn_kernel(x_ref, gpre_ref, wq_ref, mk_ref, mv_ref, wo_ref, gpost_ref,
                         wg_ref, wu_ref, wd_ref, y_ref, wg_out, wu_out, wd_out):
    hs = range(X_HEADS)
    sl = [slice(hd * X_HEAD_DIM, (hd + 1) * X_HEAD_DIM) for hd in hs]

    def stream(s):
        x = x_ref[s, 0]
        hn = _rmsnorm(x, gpre_ref[...]).astype(bf16)
        yield
        q = []
        for hd in hs:
            q.append(jnp.dot(hn, wq_ref[:, sl[hd]], preferred_element_type=f32))
            yield
        sc = [_dot_nt(q[hd], mk_ref[s, 0, :, sl[hd]]) for hd in hs]
        yield
        m = [jnp.max(sc[hd], axis=-1, keepdims=True) for hd in hs]
        e = [jnp.exp(sc[hd] - m[hd]) for hd in hs]
        yield
        r = [1.0 / jnp.sum(e[hd], axis=-1, keepdims=True) for hd in hs]
        p = [(e[hd] * r[hd]).astype(bf16) for hd in hs]
        yield
        o = jnp.concatenate([_dot(p[hd], mv_ref[s, 0, :, sl[hd]]) for hd in hs], axis=-1).astype(bf16)
        yield
        c = []
        for hd in hs:
            c.append(jnp.dot(o, wo_ref[:, sl[hd]], preferred_element_type=f32))
            yield
        y_ref[s, 0] = x + _rmsnorm(jnp.concatenate(c, axis=-1), gpost_ref[...])

    live = [stream(s) for s in range(STREAMS)]
    for _ in range(XATTN_LEAD):
        next(live[0], None)
    while live:
        live = [g for g in live if next(g, True) is None]

    wg_out[...] = wg_ref[...].astype(bf16)
    wu_out[...] = wu_ref[...].astype(bf16)
    wd_out[...] = wd_ref[...].astype(bf16)


def _prompt_xattn(x, gpre, wq, mk, mv, wo, gpost, ffn_w, tile):
    B, L, D = x.shape
    S = STREAMS
    G = B // S
    nt = L // tile
    xs = pl.BlockSpec((S, 1, tile, D), lambda b, i: (0, b, i, 0))
    ms = pl.BlockSpec((S, 1, N_MEM, D), lambda b, i: (0, b, 0, 0))

    def slab(w):
        return pl.BlockSpec((w.shape[0] // (G * nt), w.shape[1]), lambda b, i: (b * nt + i, 0))

    outs = pl.pallas_call(
        _prompt_xattn_kernel, grid=(G, nt),
        in_specs=[xs, _const_spec(gpre.shape), _const_spec(wq.shape), ms, ms,
                  _const_spec(wo.shape), _const_spec(gpost.shape)] + [slab(w) for w in ffn_w],
        out_specs=(xs,) + tuple(slab(w) for w in ffn_w),
        out_shape=(jax.ShapeDtypeStruct((S, G, L, D), f32),)
        + tuple(jax.ShapeDtypeStruct(w.shape, bf16) for w in ffn_w),
        name="prompt_xattn",
        compiler_params=pltpu.CompilerParams(
            dimension_semantics=("arbitrary", "arbitrary"), vmem_limit_bytes=VMEM_LIMIT),
    )(x.reshape(S, G, L, D), gpre, wq, mk.reshape(S, G, N_MEM, D), mv.reshape(S, G, N_MEM, D),
      wo, gpost, *ffn_w)
    return (outs[0].reshape(B, L, D),) + tuple(outs[1:])


def _ffn_kernel(xp_ref, xs_ref, gpre_ref, wg_ref, wu_ref, wd_ref, gpost_ref, yp_ref, ys_ref):
    def rows(x_ref):
        x = x_ref[...]
        hf = _rmsnorm(x, gpre_ref[...]).astype(bf16)
        gate = jnp.dot(hf, wg_ref[...], preferred_element_type=f32)
        up = jnp.dot(hf, wu_ref[...], preferred_element_type=f32)
        act = (_silu(gate) * up).astype(bf16)
        f = jnp.dot(act, wd_ref[...], preferred_element_type=f32)
        return x + _rmsnorm(f, gpost_ref[...])

    i = pl.program_id(0)
    last = pl.num_programs(0) - 1

    @pl.when(i < last)
    def _prompt_rows():
        yp_ref[...] = rows(xp_ref)

    @pl.when(i == last)
    def _sample_rows():
        nb, steps, _ = ys_ref.shape
        y = rows(xs_ref)
        for t in range(steps):
            ys_ref[:, t, :] = y[t * nb:(t + 1) * nb]


def _ffn(xp2d, xs2d, gpre, wg, wu, wd, gpost, tile, steps):
    n, d = xp2d.shape
    assert xs2d.shape == (tile, d)
    steps_p = n // tile
    nb = tile // steps
    prow = pl.BlockSpec((tile, d), lambda i: (jnp.minimum(i, steps_p - 1), 0))
    srow = pl.BlockSpec((tile, d), lambda i: (0, 0))
    return pl.pallas_call(
        _ffn_kernel, grid=(steps_p + 1,),
        in_specs=[prow, srow, _const_spec(gpre.shape), _const_spec(wg.shape), _const_spec(wu.shape),
                  _const_spec(wd.shape), _const_spec(gpost.shape)],
        out_specs=(prow, _const_spec((nb, steps, d))),
        out_shape=(jax.ShapeDtypeStruct((n, d), f32), jax.ShapeDtypeStruct((nb, steps, d), f32)),
        name="ffn",
        compiler_params=pltpu.CompilerParams(
            dimension_semantics=("arbitrary",), vmem_limit_bytes=VMEM_LIMIT),
    )(xp2d, xs2d, gpre, wg, wu, wd, gpost)


def _pad_rows(a, rows):
    if a.shape[0] == rows:
        return a
    return jnp.concatenate([a, jnp.zeros((rows - a.shape[0], a.shape[1]), a.dtype)], axis=0)


def _sample_mixer_kernel(x_ref, cprev_ref, st_ref, ck_ref, cv_ref,
                         gpre_ref, win_ref, convw_ref, convb_ref, dtb_ref, alog_ref,
                         dskip_ref, gssd_ref, sinkcol_ref, wout_ref, gpost_ref,
                         y_ref, ssm_ref, cnew_ref, wk_ref, wv_ref, *, bt, steps):
    R = steps * bt
    half = SSD_WIDTH // SSD_GROUPS
    x = jnp.concatenate([x_ref[:, t, :] for t in range(steps)], axis=0)
    h = _rmsnorm(x, gpre_ref[...]).astype(bf16)
    z = jnp.dot(h, win_ref[:, P_Z:P_XBC], preferred_element_type=f32)
    u = jnp.dot(h, win_ref[:, P_XBC:P_Q], preferred_element_type=f32)
    q = jnp.dot(h, win_ref[:, P_Q:P_K], preferred_element_type=f32)
    k_new = jnp.dot(h, win_ref[:, P_K:P_V], preferred_element_type=f32)
    v_new = jnp.dot(h, win_ref[:, P_V:P_DT], preferred_element_type=f32)
    dt_raw = jnp.dot(h, win_ref[:, P_DT:P_END], preferred_element_type=f32)
    lane = lax.broadcasted_iota(jnp.int32, (1, LANES), 1)
    dt = jnp.where(lane < SSD_HEADS, _softplus(dt_raw + dtb_ref[...]), 0.0)

    def slab(a, t):
        return a[t * bt:(t + 1) * bt]

    HD = ATT_HEAD_DIM
    GH = ATT_HEADS // ATT_KV_HEADS
    GR = GH * R
    reps = GR // bt
    kvg = range(ATT_KV_HEADS)
    keep = WINDOW - steps

    kn_t = _pad_rows(k_new, LANES).T
    vn_t = _pad_rows(v_new, LANES).T

    ridx = lax.broadcasted_iota(jnp.int32, (GR, 1), 0)
    rb = ridx % bt
    rt = (ridx // bt) % steps
    qg = [jnp.concatenate([q[:, (g * GH + hl) * HD:(g * GH + hl + 1) * HD] for hl in range(GH)],
                          axis=0) for g in kvg]
    qg_b = [qg[g].astype(bf16) for g in kvg]
    s_cb = [[_dot(qg_b[g], ck_ref[b, g]) for b in range(bt)] for g in kvg]

    hist = [cprev_ref[j] for j in range(SSD_CONV - 1)] + [slab(u, t) for t in range(steps)]
    xbc_t = []
    for t in range(steps):
        acc = convb_ref[...]
        for j in range(SSD_CONV):
            acc = acc + hist[t + j] * convw_ref[j:j + 1, :]
        xbc_t.append(_silu(acc))
    for j in range(SSD_CONV - 1):
        cnew_ref[j] = hist[steps + j]
    xbc = jnp.concatenate(xbc_t, axis=0)
    xs = xbc[:, 0:SSD_WIDTH]
    bm = xbc[:, SSD_WIDTH:SSD_WIDTH + half]
    cm = xbc[:, SSD_WIDTH + half:SSD_CONV_DIM]
    a_row = -jnp.exp(alog_ref[...])
    adt = dt * a_row
    cs_t = [slab(adt, 0)]
    for t in range(1, steps):
        cs_t.append(cs_t[-1] + slab(adt, t))
    cs = jnp.concatenate(cs_t, axis=0)
    tot = cs_t[-1]
    tot_rows = jnp.concatenate([tot] * steps, axis=0)
    expand = _head_expand_matrix()
    expand2 = jnp.concatenate([expand, expand], axis=0)
    pairs = [(t, s2) for t in range(steps) for s2 in range(t)]
    fac = jnp.concatenate([dt, jnp.exp(cs), jnp.exp(tot_rows - cs), jnp.exp(tot)]
                          + [jnp.exp(cs_t[t] - cs_t[s2]) for t, s2 in pairs], axis=0)
    f_hi, f_lo = _split2(fac)
    fac_e = jnp.dot(jnp.concatenate([f_hi, f_lo], axis=1), expand2, preferred_element_type=f32)
    gr = lax.broadcasted_iota(jnp.int32, (half, SSD_WIDTH), 0)
    gc = lax.broadcasted_iota(jnp.int32, (half, SSD_WIDTH), 1)
    gsum = ((gr >> 7) == (gc >> 8)).astype(bf16)
    gsum2 = jnp.concatenate([gsum, gsum], axis=0)
    cb_pairs = [(t, s2) for t in range(steps) for s2 in range(t + 1)]
    prod = jnp.concatenate([slab(cm, t) * slab(bm, s2) for t, s2 in cb_pairs], axis=0)
    c_hi, c_lo = _split2(prod)
    cb_e = jnp.dot(jnp.concatenate([c_hi, c_lo], axis=1), gsum2, preferred_element_type=f32)

    sel_r = lax.broadcasted_iota(jnp.int32, (LANES, WINDOW), 0)
    sel_l = lax.broadcasted_iota(jnp.int32, (LANES, WINDOW), 1)
    sel = [((sel_r % bt == b) & (sel_r < R) & (sel_l - keep == sel_r // bt)).astype(bf16)
           for b in range(bt)]
    new_k = [_dot_x2(kn_t, sel[b]) for b in range(bt)]
    new_v = [_dot_x2(vn_t, sel[b]) for b in range(bt)]

    jcol = lax.broadcasted_iota(jnp.int32, (GR, WINDOW), 1)
    in_window = jcol > rt
    s_c, m, s_n = [], [], []
    for g in kvg:
        acc = jnp.zeros((GR, WINDOW), f32)
        for b in range(bt):
            acc = jnp.where(rb == b, s_cb[g][b], acc)
        s_c.append(jnp.where(in_window, acc, -jnp.inf))
    sink = [sinkcol_ref[g * GR:(g + 1) * GR, :] for g in kvg]
    for g in kvg:
        mg = jnp.maximum(jnp.max(s_c[g], axis=-1, keepdims=True), sink[g])
        sn_g = []
        for t2 in range(steps):
            kt = jnp.concatenate([slab(k_new, t2)[:, g * HD:(g + 1) * HD]] * reps, axis=0)
            sn = jnp.where(rt >= t2, jnp.sum(qg[g] * kt, axis=-1, keepdims=True), -jnp.inf)
            sn_g.append(sn)
            mg = jnp.maximum(mg, sn)
        m.append(mg)
        s_n.append(sn_g)
    e_c = [jnp.exp(s_c[g] - m[g]) for g in kvg]
    e_n = [[jnp.exp(sn - m[g]) for sn in s_n[g]] for g in kvg]
    rinv = []
    for g in kvg:
        den = jnp.sum(e_c[g], axis=-1, keepdims=True) + jnp.exp(sink[g] - m[g])
        for en in e_n[g]:
            den = den + en
        rinv.append(1.0 / den)
    p_c = [(e_c[g] * rinv[g]).astype(bf16) for g in kvg]

    xdt = xs * fac_e[0:R]
    ecs_e = fac_e[R:2 * R]
    w_end = xdt * fac_e[2 * R:3 * R]
    dec_e = fac_e[3 * R:3 * R + bt]
    pair_decay = {pr: fac_e[3 * R + (n + 1) * bt:3 * R + (n + 2) * bt] for n, pr in enumerate(pairs)}
    y_t = []
    for t in range(steps):
        acc = None
        for s2 in range(t + 1):
            n = cb_pairs.index((t, s2))
            coef = cb_e[n * bt:(n + 1) * bt]
            if s2 < t:
                coef = coef * pair_decay[(t, s2)]
            term = coef * slab(xdt, s2)
            acc = term if acc is None else acc + term
        y_t.append(acc)
    y_intra = jnp.concatenate(y_t, axis=0)
    b_idx = lax.broadcasted_iota(jnp.int32, (bt, 1, LANES), 0)
    l_idx = lax.broadcasted_iota(jnp.int32, (bt, 1, LANES), 2)
    pair = ((l_idx & (bt - 1)) == b_idx) & (l_idx < R)
    own = (l_idx == b_idx)
    gsl = [slice(g * half, (g + 1) * half) for g in range(SSD_GROUPS)]
    h0 = [st_ref[:, gsl[g], :] for g in range(SSD_GROUPS)]
    zz = [_dot_nt(h0[g].reshape(bt * half, SSD_STATE),
                  _pad_rows(cm[:, g * SSD_STATE:(g + 1) * SSD_STATE], LANES)).reshape(bt, half, LANES)
          for g in range(SSD_GROUPS)]
    wt = [_pad_rows(w_end[:, gsl[g]], LANES).T for g in range(SSD_GROUPS)]
    contrib = [_dot(jnp.where(pair, wt[g][None], 0.0).reshape(bt * half, LANES),
                    _pad_rows(bm[:, g * SSD_STATE:(g + 1) * SSD_STATE], LANES)
                    ).reshape(bt, half, SSD_STATE) for g in range(SSD_GROUPS)]
    dec_t = [_pad_rows(dec_e[:, gsl[g]], LANES).T for g in range(SSD_GROUPS)]

    o = []
    for g in kvg:
        pv = [_dot_nt(p_c[g], cv_ref[b, g]) for b in range(bt)]
        og = jnp.zeros((GR, HD), f32)
        for b in range(bt):
            og = jnp.where(rb == b, pv[b], og)
        for t2 in range(steps):
            vt = jnp.concatenate([slab(v_new, t2)[:, g * HD:(g + 1) * HD]] * reps, axis=0)
            og = og + (e_n[g][t2] * rinv[g]) * vt
        o.append(og)

    lane_w = lax.broadcasted_iota(jnp.int32, (1, WINDOW), 1)
    for b in range(bt):
        for g in kvg:
            gs = slice(g * HD, (g + 1) * HD)
            wk_ref[b, g] = jnp.where(lane_w < keep, pltpu.roll(ck_ref[b, g], keep, 1), new_k[b][gs])
            wv_ref[b, g] = jnp.where(lane_w < keep, pltpu.roll(cv_ref[b, g], keep, 1), new_v[b][gs])

    y_off_parts = []
    for g in range(SSD_GROUPS):
        yt = jnp.sum(jnp.where(pair, zz[g], 0.0), axis=0)
        y_off_parts.append(yt.T[0:R, :])
        dec = jnp.sum(jnp.where(own, dec_t[g][None], 0.0), axis=-1, keepdims=True)
        ssm_ref[:, gsl[g], :] = h0[g] * dec + contrib[g]
    y_off = jnp.concatenate(y_off_parts, axis=-1) * ecs_e
    y = y_intra + y_off + xs * dskip_ref[...]
    y_ssd = _gated_group_norm(y, z, gssd_ref[...])

    mix = jnp.dot(y_ssd.astype(bf16), wout_ref[0:SSD_WIDTH, :], preferred_element_type=f32)
    for g in kvg:
        for hl in range(GH):
            hd = g * GH + hl
            mix = mix + jnp.dot(o[g][hl * R:(hl + 1) * R].astype(bf16),
                                wout_ref[SSD_WIDTH + hd * HD:SSD_WIDTH + (hd + 1) * HD, :],
                                preferred_element_type=f32)
    y_ref[...] = (x + _rmsnorm(mix, gpost_ref[...])).reshape(steps, bt, D_MODEL)


def _sample_mixer(x, cprev_tm, st, ck, cv, gpre, win, convw, convb, dtb, alog, dskip_e, gssd,
                  sinkcol, wout, gpost, bt):
    nb, steps, D = x.shape
    kern = functools.partial(_sample_mixer_kernel, bt=bt, steps=steps)
    tm = lambda w: pl.BlockSpec((steps, bt, w), lambda i: (0, i, 0))
    win_spec = pl.BlockSpec((bt, ATT_KV_HEADS, ATT_HEAD_DIM, WINDOW), lambda i: (i, 0, 0, 0))
    in_specs = [
        pl.BlockSpec((bt, steps, D), lambda i: (i, 0, 0)),
        pl.BlockSpec((SSD_CONV - 1, bt, SSD_CONV_DIM), lambda i: (0, i, 0)),
        pl.BlockSpec((bt, SSD_WIDTH, SSD_STATE), lambda i: (i, 0, 0)),
        win_spec, win_spec,
    ] + [_const_spec(a.shape) for a in (gpre, win, convw, convb, dtb, alog, dskip_e, gssd,
                                        sinkcol, wout, gpost)]
    out_specs = (
        tm(D),
        pl.BlockSpec((bt, SSD_WIDTH, SSD_STATE), lambda i: (i, 0, 0)),
        pl.BlockSpec((SSD_CONV - 1, bt, SSD_CONV_DIM), lambda i: (0, i, 0)),
        win_spec, win_spec,
    )
    out_shape = (
        jax.ShapeDtypeStruct((steps, nb, D), f32),
        jax.ShapeDtypeStruct((nb, SSD_WIDTH, SSD_STATE), f32),
        jax.ShapeDtypeStruct((SSD_CONV - 1, nb, SSD_CONV_DIM), f32),
        jax.ShapeDtypeStruct(ck.shape, f32),
        jax.ShapeDtypeStruct(cv.shape, f32),
    )
    return pl.pallas_call(
        kern, grid=(nb // bt,), in_specs=in_specs, out_specs=out_specs, out_shape=out_shape,
        name="sample_mixer",
        compiler_params=pltpu.CompilerParams(
            dimension_semantics=("arbitrary",), vmem_limit_bytes=VMEM_LIMIT),
    )(x, cprev_tm, st, ck, cv, gpre, win, convw, convb, dtb, alog, dskip_e, gssd, sinkcol,
      wout, gpost)


def _sample_xattn_kernel(x_ref, mk_ref, mv_ref, gpre_ref, wq_ref, wo_ref, gpost_ref, y_ref,
                         q_sc, o_sc, *, bt, steps):
    i = pl.program_id(0)
    nb = x_ref.shape[1]
    R = steps * bt
    nrow = bt * N_MEM * X_HEADS

    @pl.when(i == 0)
    def _project_queries():
        x_all = x_ref[...].reshape(steps * nb, D_MODEL)
        hn = _rmsnorm(x_all, gpre_ref[...]).astype(bf16)
        q_sc[...] = jnp.dot(hn, wq_ref[...], preferred_element_type=f32)

    tile_rows = [pl.ds(pl.multiple_of(t * nb + i * bt, bt), bt) for t in range(steps)]
    q = jnp.concatenate([q_sc[r, :] for r in tile_rows], axis=0)
    qs = jnp.concatenate([q[:, hd * X_HEAD_DIM:(hd + 1) * X_HEAD_DIM] for hd in range(X_HEADS)],
                         axis=0)
    kall = mk_ref[...].reshape(nrow, X_HEAD_DIM)
    vall = mv_ref[...].reshape(nrow, X_HEAD_DIM)
    ncol = X_HEADS * R
    seq_rows = N_MEM * X_HEADS
    z = _dot_nt(kall, qs).reshape(bt, seq_rows, ncol)
    v_b = [vall[b * seq_rows:(b + 1) * seq_rows].astype(bf16) for b in range(bt)]
    b_i = lax.broadcasted_iota(jnp.int32, (bt, 1, ncol), 0)
    c_i = lax.broadcasted_iota(jnp.int32, (bt, 1, ncol), 2)
    zc = jnp.sum(jnp.where(c_i % bt == b_i, z, 0.0), axis=0)
    zc = zc.reshape(seq_rows // SUBLANES, SUBLANES, ncol)
    r_h = lax.broadcasted_iota(jnp.int32, (1, SUBLANES, ncol), 1) % X_HEADS
    c_h = lax.broadcasted_iota(jnp.int32, (1, SUBLANES, ncol), 2) // R
    zc = jnp.where(r_h == c_h, zc, -jnp.inf).reshape(seq_rows, ncol)
    m = jnp.max(zc, axis=0, keepdims=True)
    e = jnp.exp(zc - m)
    p = e * (1.0 / jnp.sum(e, axis=0, keepdims=True))
    col_b = lax.broadcasted_iota(jnp.int32, (1, ncol), 1) % bt
    tn = (((0,), (0,)), ((), ()))
    o = None
    for b in range(bt):
        p_b = jnp.where(col_b == b, p, 0.0).astype(bf16)
        o_b = lax.dot_general(p_b, v_b[b], tn, preferred_element_type=f32)
        o = o_b if o is None else o + o_b
    o = jnp.concatenate([o[hd * R:(hd + 1) * R] for hd in range(X_HEADS)], axis=-1)
    for t in range(steps):
        o_sc[tile_rows[t], :] = o[t * bt:(t + 1) * bt]

    @pl.when(i == pl.num_programs(0) - 1)
    def _project_outputs():
        x_all = x_ref[...].reshape(steps * nb, D_MODEL)
        cc = _dot(o_sc[...], wo_ref[...])
        y_ref[...] = (x_all + _rmsnorm(cc, gpost_ref[...])).reshape(steps, nb, D_MODEL)


def _sample_xattn(x_tm, mk, mv, gpre, wq, wo, gpost, bt):
    steps, nb, D = x_tm.shape
    kern = functools.partial(_sample_xattn_kernel, bt=bt, steps=steps)
    xs = _const_spec(x_tm.shape)
    ms = pl.BlockSpec((bt, N_MEM, X_HEADS, X_HEAD_DIM), lambda i: (i, 0, 0, 0))
    return pl.pallas_call(
        kern, grid=(nb // bt,),
        in_specs=[xs, ms, ms, _const_spec(gpre.shape), _const_spec(wq.shape),
                  _const_spec(wo.shape), _const_spec(gpost.shape)],
        out_specs=xs, out_shape=jax.ShapeDtypeStruct((steps, nb, D), f32),
        scratch_shapes=[pltpu.VMEM((steps * nb, D), f32), pltpu.VMEM((steps * nb, D), f32)],
        name="sample_xattn",
        compiler_params=pltpu.CompilerParams(
            dimension_semantics=("arbitrary",), vmem_limit_bytes=VMEM_LIMIT),
    )(x_tm, mk, mv, gpre, wq, wo, gpost)


def _win_prep_kernel(wt_ref, o_ref):
    piece = 2 * LANES

    def put(src_lo, n, dst_lo, scale=None):
        for c in range(0, n, piece):
            t = wt_ref[src_lo + c:src_lo + c + piece, :].T
            if scale is not None:
                t = t * scale
            o_ref[:, dst_lo + c:dst_lo + c + piece] = t.astype(bf16)

    put(W_Z, P_Q - P_Z, P_Z)
    put(W_Q, P_K - P_Q, P_Q, ATT_SCALE)
    put(W_K, P_DT - P_K, P_K)
    dt_rows = jnp.concatenate([wt_ref[W_DT:W_DT + SSD_HEADS, :],
                               jnp.zeros((LANES - SSD_HEADS, D_MODEL), f32)], axis=0)
    o_ref[:, P_DT:P_END] = dt_rows.T.astype(bf16)


def _win_prep(w_t):
    return pl.pallas_call(
        _win_prep_kernel, out_shape=jax.ShapeDtypeStruct((D_MODEL, P_END), bf16),
        name="win_prep",
        compiler_params=pltpu.CompilerParams(vmem_limit_bytes=VMEM_LIMIT),
    )(w_t)


def _row(v, width=None):
    v = v.reshape(1, -1).astype(f32)
    if width is not None and v.shape[1] < width:
        v = jnp.pad(v, ((0, 0), (0, width - v.shape[1])))
    return v


def kernel(x_prompt, x_sample, state_ssm, state_conv, cache_win_k, cache_win_v, cache_mem_k, cache_mem_v, mem_prompt, g_mix_pre, w_in, conv_w, conv_b, dt_bias, a_log, d_skip, g_ssd_norm, sinks, w_out, g_mix_post, g_x_pre, w_xq, g_mem, w_xk, w_xv, w_xo, g_x_post, g_ffn_pre, w_gate, w_up, w_down, g_ffn_post):
    depth = w_in.shape[0]
    assert depth == 1
    B, L, D = x_prompt.shape
    NB, steps, _ = x_sample.shape
    li = 0

    win_p = _win_prep(jnp.transpose(w_in[li]))
    wo_b = w_out[li].astype(bf16)

    gpre, gpost = _row(g_mix_pre[li]), _row(g_mix_post[li])
    convw, convb = conv_w[li].astype(f32), _row(conv_b[li])
    dtb, alog = _row(dt_bias[li], LANES), _row(a_log[li], LANES)
    dskip_e = _row(jnp.repeat(d_skip[li], SSD_HEAD_DIM))
    gssd = _row(g_ssd_norm[li])
    sk = sinks[li].astype(f32)

    mk2d, mv2d, mk4, mv4 = _memkv(mem_prompt.reshape(B * N_MEM, D), _row(g_mem[li]),
                                  w_xk[li].astype(bf16), w_xv[li].astype(bf16), tile=512)
    mk3, mv3 = mk2d.reshape(B, N_MEM, D), mv2d.reshape(B, N_MEM, D)
    x1, p_ssm, p_conv8, p_wk, p_wv = _prompt_mixer(
        x_prompt, gpre, win_p, convw, convb, dtb, alog, dskip_e, gssd, sk, wo_b, gpost, tile=512)
    wxq_b, wxo_b = (w_xq[li] * X_SCALE).astype(bf16), w_xo[li].astype(bf16)
    gxpre, gxpost = _row(g_x_pre[li]), _row(g_x_post[li])
    x2, wg_b, wu_b, wd_b = _prompt_xattn(x1, gxpre, wxq_b, mk3, mv3, wxo_b, gxpost,
                                         (w_gate[li], w_up[li], w_down[li]), tile=512)
    gfpre, gfpost = _row(g_ffn_pre[li]), _row(g_ffn_post[li])

    bt, bt_mix = 8, 16
    cprev_tm = jnp.transpose(state_conv[li], (1, 0, 2))
    st = state_ssm[li].reshape(NB, SSD_WIDTH, SSD_STATE)
    ck = jnp.transpose(cache_win_k[li], (0, 2, 3, 1))
    cv = jnp.transpose(cache_win_v[li], (0, 2, 3, 1))
    sinkcol = jnp.repeat(sk, steps * bt_mix).reshape(ATT_HEADS * steps * bt_mix, 1)
    x1s, s_ssm, cnew_tm, s_wk, s_wv = _sample_mixer(
        x_sample, cprev_tm, st, ck, cv, gpre, win_p, convw, convb, dtb, alog, dskip_e, gssd,
        sinkcol, wo_b, gpost, bt=bt_mix)
    cmk = cache_mem_k.reshape(NB, N_MEM, X_HEADS, X_HEAD_DIM)
    cmv = cache_mem_v.reshape(NB, N_MEM, X_HEADS, X_HEAD_DIM)
    x2s = _sample_xattn(x1s, cmk, cmv, gxpre, wxq_b, wxo_b, gxpost, bt=bt)
    yp2d, ys = _ffn(x2.reshape(B * L, D), x2s.reshape(steps * NB, D), gfpre, wg_b, wu_b, wd_b,
                    gfpost, tile=steps * NB, steps=steps)
    yp = yp2d.reshape(B, L, D)

    s_conv = jnp.transpose(cnew_tm, (1, 0, 2))
    kv_shape = (ATT_KV_HEADS, ATT_HEAD_DIM)
    return (
        yp, ys,
        p_ssm.reshape(1, B, SSD_HEADS, SSD_HEAD_DIM, SSD_STATE),
        p_conv8[:, SUBLANES - (SSD_CONV - 1):, :][None],
        jnp.transpose(p_wk.reshape(B, *kv_shape, WINDOW), (0, 3, 1, 2))[None],
        jnp.transpose(p_wv.reshape(B, *kv_shape, WINDOW), (0, 3, 1, 2))[None],
        mk4.reshape(1, B, N_MEM, X_HEADS, X_HEAD_DIM), mv4.reshape(1, B, N_MEM, X_HEADS, X_HEAD_DIM),
        s_ssm.reshape(1, NB, SSD_HEADS, SSD_HEAD_DIM, SSD_STATE),
        s_conv[None],
        jnp.transpose(s_wk, (0, 3, 1, 2))[None], jnp.transpose(s_wv, (0, 3, 1, 2))[None],
    )
```

```python
import functools

import jax
import jax.numpy as jnp
from jax import lax
from jax.experimental import pallas as pl
from jax.experimental.pallas import tpu as pltpu

f32 = jnp.float32
bf16 = jnp.bfloat16

D_MODEL = 1024
EPS = 1e-6
N_MEM = 256
SSD_HEADS = 8
SSD_HEAD_DIM = 64
SSD_WIDTH = 512
SSD_GROUPS = 2
SSD_STATE = 128
SSD_CONV = 4
SSD_CHUNK = 128
SSD_CONV_DIM = 1024
ATT_HEADS = 8
ATT_KV_HEADS = 2
ATT_HEAD_DIM = 64
ATT_WIDTH = 512
ATT_KV_WIDTH = 128
WINDOW = 128
ATT_SCALE = ATT_HEAD_DIM ** -0.5
X_HEADS = 4
X_HEAD_DIM = 256
X_SCALE = X_HEAD_DIM ** -0.5
D_FF = 2816
LANES = 128
SUBLANES = 8
VMEM_LIMIT = 56 * 1024 * 1024
STREAMS = 2
PROJ_PIECES, SCAN_PIECES, OUT_PIECES = 19, 10, 5
CONV_BLOCKS = SSD_CONV_DIM // LANES
XATTN_LEAD = 5

P_Z, P_XBC, P_Q, P_K, P_V, P_DT, P_END = 0, 512, 1536, 2048, 2176, 2304, 2432
W_Z, W_DT, W_Q, W_K = 0, 1536, 1544, 2056


def _dot(a, b):
    return jnp.dot(a.astype(bf16), b.astype(bf16), preferred_element_type=f32)


def _dot_nt(a, b):
    return lax.dot_general(a.astype(bf16), b.astype(bf16), (((1,), (1,)), ((), ())),
                           preferred_element_type=f32)


def _split2(x):
    hi = x.astype(bf16)
    lo = (x - hi.astype(f32)).astype(bf16)
    return hi, lo


def _dot_x2(x, m):
    hi, lo = _split2(x)
    return (jnp.dot(hi, m, preferred_element_type=f32)
            + jnp.dot(lo, m, preferred_element_type=f32))


def _rmsnorm(x, g):
    ms = jnp.mean(x * x, axis=-1, keepdims=True)
    return x * lax.rsqrt(ms + EPS) * g


def _silu(x):
    return x * jax.nn.sigmoid(x)


def _softplus(x):
    return jnp.maximum(x, 0.0) + jnp.log1p(jnp.exp(-jnp.abs(x)))


def _head_expand_matrix():
    r = lax.broadcasted_iota(jnp.int32, (LANES, SSD_WIDTH), 0)
    c = lax.broadcasted_iota(jnp.int32, (LANES, SSD_WIDTH), 1)
    return (r == (c >> 6)).astype(bf16)


def _gated_group_norm(y, z, g):
    u = y * _silu(z)
    half = SSD_WIDTH // SSD_GROUPS
    parts = []
    for gi in range(SSD_GROUPS):
        ug = u[:, gi * half:(gi + 1) * half]
        parts.append(ug * lax.rsqrt(jnp.mean(ug * ug, axis=-1, keepdims=True) + EPS))
    return jnp.concatenate(parts, axis=-1) * g


def _prompt_mixer_kernel(x_ref, gpre_ref, win_ref, convw_ref, convb_ref, dtb_ref, alog_ref,
                         dskip_ref, gssd_ref, sinks_ref, wout_ref, gpost_ref,
                         y_ref, ssm_ref, conv_ref, wk_ref, wv_ref,
                         statet_sc, xbc_ext_sc, xbc_sc, z_sc, q_sc, k_sc, v_sc, dt_sc,
                         kprev_sc, vprev_sc, mix_sc, *, tile):
    i = pl.program_id(1)
    NC = tile // SSD_CHUNK
    ns = range(NC)
    C = SSD_CHUNK
    PW = 2 * LANES

    @pl.when(i == 0)
    def _init():
        statet_sc[...] = jnp.zeros_like(statet_sc)
        xbc_ext_sc[:, :, 0:SUBLANES, :] = jnp.zeros((STREAMS, CONV_BLOCKS, SUBLANES, LANES), f32)
        kprev_sc[...] = jnp.zeros_like(kprev_sc)
        vprev_sc[...] = jnp.zeros_like(vprev_sc)

    lane = lax.broadcasted_iota(jnp.int32, (1, LANES), 1)
    a_row = -jnp.exp(alog_ref[...])
    expand = _head_expand_matrix()
    expand2 = jnp.concatenate([expand, expand], axis=0)
    row_i = lax.broadcasted_iota(jnp.int32, (C, C), 0)
    col_i = lax.broadcasted_iota(jnp.int32, (C, C), 1)
    lower = col_i <= row_i
    tri = lower.astype(bf16)
    tri3 = jnp.concatenate([tri, tri, tri], axis=1)
    lo_half = lane < ATT_HEAD_DIM
    half = SSD_WIDTH // SSD_GROUPS
    head_of_lane = lax.broadcasted_iota(jnp.int32, (1, half), 1) >> 6
    rows = [slice(n * C, (n + 1) * C) for n in ns]

    def stream(s):
        statet_s, xbc_ext_s, xbc_s, z_s = statet_sc.at[s], xbc_ext_sc.at[s], xbc_sc.at[s], z_sc.at[s]
        q_s, k_s, v_s, dt_s = q_sc.at[s], k_sc.at[s], v_sc.at[s], dt_sc.at[s]
        kprev_s, vprev_s, mix_s = kprev_sc.at[s], vprev_sc.at[s], mix_sc.at[s]

        x = x_ref[s, 0]
        h = _rmsnorm(x, gpre_ref[...]).astype(bf16)
        yield

        def proj(col):
            return jnp.dot(h, win_ref[:, col:col + PW], preferred_element_type=f32)

        def conv_cols(cb):
            cols = slice(cb * LANES, (cb + 1) * LANES)
            acc = convb_ref[:, cols]
            for j in range(SSD_CONV):
                off = SUBLANES - (SSD_CONV - 1) + j
                acc = acc + xbc_ext_s[cb, off:off + tile, :] * convw_ref[j:j + 1, cols]
            xbc_s[cb] = _silu(acc)

        def xbc_piece(pc):
            res = proj(P_XBC + pc * PW)
            for half_pc in range(PW // LANES):
                xbc_ext_s[(PW // LANES) * pc + half_pc, SUBLANES:SUBLANES + tile, :] = (
                    res[:, half_pc * LANES:(half_pc + 1) * LANES])

        xbc_piece(0)
        yield
        xbc_piece(1)
        yield
        conv_cols(0)
        yield
        xbc_piece(2)
        yield
        conv_cols(1)
        yield
        xbc_piece(3)
        yield
        conv_cols(2)
        yield
        z_s[:, 0:PW] = proj(P_Z)
        yield
        conv_cols(3)
        yield
        z_s[:, PW:2 * PW] = proj(P_Z + PW)
        yield
        conv_cols(4)
        yield
        q_s[:, 0:PW] = proj(P_Q)
        yield
        conv_cols(5)
        yield
        q_s[:, PW:2 * PW] = proj(P_Q + PW)
        yield
        conv_cols(6)
        yield
        kv = proj(P_K)
        k_s[...] = kv[:, 0:LANES]
        v_s[...] = kv[:, LANES:PW]
        yield
        conv_cols(7)
        yield
        dt_raw = jnp.dot(h, win_ref[:, P_DT:P_END], preferred_element_type=f32)
        dt_s[...] = jnp.where(lane < SSD_HEADS, _softplus(dt_raw + dtb_ref[...]), 0.0)
        tail = xbc_ext_s[:, tile:tile + SUBLANES, :]
        conv_ref[s, 0] = jnp.concatenate([tail[cb] for cb in range(CONV_BLOCKS)], axis=-1)
        xbc_ext_s[:, 0:SUBLANES, :] = tail
        yield

        GH = SSD_HEADS // SSD_GROUPS
        heads = range(ATT_HEADS)
        gs = range(SSD_GROUPS)
        zero_b = jnp.zeros((), bf16)

        yield
        def xbc_cols(n, lo, hi):
            return jnp.concatenate([xbc_s[cb, rows[n], :] for cb in range(lo // LANES, hi // LANES)],
                                   axis=-1)
        xs = [xbc_cols(n, 0, SSD_WIDTH) for n in ns]
        bm = [xbc_cols(n, SSD_WIDTH, SSD_WIDTH + half) for n in ns]
        cm = [xbc_cols(n, SSD_WIDTH + half, SSD_CONV_DIM) for n in ns]
        dtc = [dt_s[rows[n], :] for n in ns]
        cs = []
        for n in ns:
            adt = dtc[n] * a_row
            a_hi = adt.astype(bf16)
            a_r1 = adt - a_hi.astype(f32)
            a_mid = a_r1.astype(bf16)
            a_lo = (a_r1 - a_mid.astype(f32)).astype(bf16)
            cs.append(jnp.dot(tri3, jnp.concatenate([a_hi, a_mid, a_lo], axis=0),
                              preferred_element_type=f32))

        yield
        q = [q_s[rows[n], :].astype(bf16) for n in ns]
        k = [k_s[rows[n], :] for n in ns]
        v = [v_s[rows[n], :] for n in ns]
        k_prev = [kprev_s[...]] + k[:-1]
        v_prev = [vprev_s[...]] + v[:-1]
        kprev_s[...] = k[-1]
        vprev_s[...] = v[-1]
        first_bias = jnp.where(i > 0, 0.0, -jnp.inf)
        k_lo, k_hi, v_lo, v_hi = [], [], [], []
        for n in ns:
            kk = jnp.concatenate([k_prev[n], k[n]], axis=0)
            vv = jnp.concatenate([v_prev[n], v[n]], axis=0)
            kk_r = pltpu.roll(kk, ATT_HEAD_DIM, 1)
            vv_r = pltpu.roll(vv, ATT_HEAD_DIM, 1)
            k_lo.append([jnp.where(lo_half, kk, 0.0).astype(bf16), jnp.where(lo_half, kk_r, 0.0).astype(bf16)])
            k_hi.append([jnp.where(lo_half, 0.0, kk_r).astype(bf16), jnp.where(lo_half, 0.0, kk).astype(bf16)])
            v_lo.append([jnp.where(lo_half, vv, 0.0).astype(bf16), jnp.where(lo_half, vv_r, 0.0).astype(bf16)])
            v_hi.append([jnp.where(lo_half, 0.0, vv_r).astype(bf16), jnp.where(lo_half, 0.0, vv).astype(bf16)])
        s_g = [[_dot_nt(jnp.concatenate([q[n][:, (2 * g) * LANES:(2 * g + 1) * LANES],
                                         q[n][:, (2 * g + 1) * LANES:(2 * g + 2) * LANES]], axis=0),
                        jnp.concatenate([k_lo[n][g], k_hi[n][g]], axis=0))
                for g in range(ATT_KV_HEADS)] for n in ns]

        yield
        cs_t = [cs[n].T for n in ns]
        fac_e, cb_all, bm_t = [], [], []
        for n in ns:
            tot = cs[n][C - 1:C, :]
            fac = jnp.concatenate([dtc[n], jnp.exp(cs[n]), jnp.exp(tot - cs[n])], axis=0)
            f_hi = fac.astype(bf16)
            f_lo = (fac - f_hi.astype(f32)).astype(bf16)
            fac_e.append(jnp.dot(jnp.concatenate([f_hi, f_lo], axis=1), expand2,
                                 preferred_element_type=f32))
            cb_all.append(_dot_nt(
                jnp.concatenate([cm[n][:, 0:SSD_STATE], cm[n][:, SSD_STATE:half]], axis=0),
                jnp.concatenate([bm[n][:, 0:SSD_STATE], bm[n][:, SSD_STATE:half]], axis=0)))
            bm_t.append(bm[n].T)

        yield
        def head_scores(n, hd):
            g, jb, sub = hd // 4, (hd // 2) % 2, hd % 2
            s_prev = s_g[n][g][jb * C:(jb + 1) * C, (2 * sub) * C:(2 * sub + 1) * C]
            s_cur = s_g[n][g][jb * C:(jb + 1) * C, (2 * sub + 1) * C:(2 * sub + 2) * C]
            return jnp.where(lower, s_cur, s_prev + first_bias if n == 0 else s_prev)
        sc = [[head_scores(n, hd) for hd in heads] for n in ns]
        sink = [sinks_ref[hd] for hd in heads]
        m = [[jnp.maximum(jnp.max(sc[n][hd], axis=-1, keepdims=True), sink[hd]) for hd in heads]
             for n in ns]

        yield
        xdt = [xs[n] * fac_e[n][0:C] for n in ns]
        ecs_e = [fac_e[n][C:2 * C] for n in ns]
        w_end = [xdt[n] * fac_e[n][2 * C:3 * C] for n in ns]
        xdt_b = [xdt[n].astype(bf16) for n in ns]
        contrib = [[_dot(bm_t[n][g * SSD_STATE:(g + 1) * SSD_STATE, :],
                         w_end[n][:, g * half:(g + 1) * half]) for g in gs] for n in ns]
        decay = [[jnp.exp(jnp.where(lower, cs[n][:, hh:hh + 1] - cs_t[n][hh:hh + 1, :], -jnp.inf))
                  for hh in range(SSD_HEADS)] for n in ns]

        yield
        e = [[jnp.exp(sc[n][hd] - m[n][hd]) for hd in heads] for n in ns]
        den = [[jnp.sum(e[n][hd], axis=-1, keepdims=True) + jnp.exp(sink[hd] - m[n][hd])
                for hd in heads] for n in ns]

        yield
        y_d = []
        for n in ns:
            y_n = []
            for g in gs:
                gl = slice(g * half, (g + 1) * half)
                cb = cb_all[n][g * C:(g + 1) * C, g * C:(g + 1) * C]
                m_parts = [(cb * decay[n][g * GH + r]).astype(bf16) for r in range(GH)]
                x_parts = [jnp.where(head_of_lane == r, xdt_b[n][:, gl], zero_b) for r in range(GH)]
                y_n.append(jnp.dot(jnp.concatenate(m_parts, axis=1), jnp.concatenate(x_parts, axis=0),
                                   preferred_element_type=f32))
            y_d.append(y_n)
        st = [statet_s[g] for g in gs]
        y_off = []
        for n in ns:
            y_off.append([_dot(cm[n][:, g * SSD_STATE:(g + 1) * SSD_STATE], st[g]) for g in gs])
            st = [st[g] * ecs_e[n][C - 1:C, g * half:(g + 1) * half] + contrib[n][g] for g in gs]
        for g in gs:
            statet_s[g] = st[g]

        yield
        p = [[(e[n][hd] * (1.0 / den[n][hd])).astype(bf16) for hd in heads] for n in ns]
        o_g = []
        for n in ns:
            o_n = []
            for g in range(ATT_KV_HEADS):
                p_rows = []
                for jb in range(2):
                    p_cols = []
                    for sub in range(2):
                        ph = p[n][g * 4 + jb * 2 + sub]
                        p_cols += [jnp.where(lower, zero_b, ph), jnp.where(lower, ph, zero_b)]
                    p_rows.append(jnp.concatenate(p_cols, axis=1))
                o_n.append(jnp.dot(jnp.concatenate(p_rows, axis=0),
                                   jnp.concatenate([v_lo[n][g], v_hi[n][g]], axis=0),
                                   preferred_element_type=f32))
            o_g.append(o_n)

        yield
        for n in ns:
            y = (jnp.concatenate([y_d[n][g] + y_off[n][g] * ecs_e[n][:, g * half:(g + 1) * half]
                                  for g in gs], axis=-1)
                 + xs[n] * dskip_ref[...])
            y_ssd = _gated_group_norm(y, z_s[rows[n], :], gssd_ref[...])
            mix_s[rows[n], 0:SSD_WIDTH] = y_ssd.astype(bf16)
        for n in ns:
            for g in range(ATT_KV_HEADS):
                for jb in range(2):
                    lo_l = SSD_WIDTH + (2 * g + jb) * LANES
                    mix_s[rows[n], lo_l:lo_l + LANES] = o_g[n][g][jb * C:(jb + 1) * C].astype(bf16)

        yield

        mix_in = mix_s[...]
        mix = []
        for pc in range(D_MODEL // PW):
            mix.append(jnp.dot(mix_in, wout_ref[:, pc * PW:(pc + 1) * PW], preferred_element_type=f32))
            yield
        y_ref[s, 0] = x + _rmsnorm(jnp.concatenate(mix, axis=-1), gpost_ref[...])

    def mixed(ga, na, gb, nb):
        done_b = 0
        for ka in range(na):
            next(ga, None)
            want_b = ((ka + 1) * nb) // na
            for _ in range(want_b - done_b):
                next(gb, None)
            done_b = want_b

    g0, g1 = [stream(s) for s in range(STREAMS)]
    for _ in range(PROJ_PIECES):
        next(g0, None)
    mixed(g0, SCAN_PIECES, g1, PROJ_PIECES)
    mixed(g1, SCAN_PIECES, g0, OUT_PIECES)
    for g in (g0, g1):
        for _ in g:
            pass

    @pl.when(i == pl.num_programs(1) - 1)
    def _final_state():
        for s in range(STREAMS):
            for g in range(SSD_GROUPS):
                ssm_ref[s, 0, g * half:(g + 1) * half, :] = statet_sc[s, g].T
            wk_ref[s, 0] = k_sc[s, tile - WINDOW:tile, :].T
            wv_ref[s, 0] = v_sc[s, tile - WINDOW:tile, :].T


def _const_spec(shape):
    nd = len(shape)
    return pl.BlockSpec(shape, lambda *_: (0,) * nd)


def _prompt_mixer(x, gpre, win, convw, convb, dtb, alog, dskip_e, gssd, sinks, wout, gpost, tile):
    B, L, D = x.shape
    S = STREAMS
    G = B // S
    kern = functools.partial(_prompt_mixer_kernel, tile=tile)

    def per_seq(rows, width):
        return pl.BlockSpec((S, 1, rows, width), lambda b, i: (0, b, 0, 0))

    tile_spec = pl.BlockSpec((S, 1, tile, D), lambda b, i: (0, b, i, 0))
    out_shape = (
        jax.ShapeDtypeStruct((S, G, L, D), f32),
        jax.ShapeDtypeStruct((S, G, SSD_WIDTH, SSD_STATE), f32),
        jax.ShapeDtypeStruct((S, G, SUBLANES, SSD_CONV_DIM), f32),
        jax.ShapeDtypeStruct((S, G, ATT_KV_WIDTH, WINDOW), f32),
        jax.ShapeDtypeStruct((S, G, ATT_KV_WIDTH, WINDOW), f32),
    )
    in_specs = [
        tile_spec,
        _const_spec(gpre.shape), _const_spec(win.shape), _const_spec(convw.shape),
        _const_spec(convb.shape), _const_spec(dtb.shape), _const_spec(alog.shape),
        _const_spec(dskip_e.shape), _const_spec(gssd.shape),
        pl.BlockSpec(memory_space=pltpu.SMEM),
        _const_spec(wout.shape), _const_spec(gpost.shape),
    ]
    out_specs = (
        tile_spec,
        per_seq(SSD_WIDTH, SSD_STATE),
        per_seq(SUBLANES, SSD_CONV_DIM),
        per_seq(ATT_KV_WIDTH, WINDOW),
        per_seq(ATT_KV_WIDTH, WINDOW),
    )
    scratch = [
        pltpu.VMEM((S, SSD_GROUPS, SSD_STATE, SSD_WIDTH // SSD_GROUPS), f32),
        pltpu.VMEM((S, CONV_BLOCKS, tile + 2 * SUBLANES, LANES), f32),
        pltpu.VMEM((S, CONV_BLOCKS, tile, LANES), f32),
        pltpu.VMEM((S, tile, SSD_WIDTH), f32),
        pltpu.VMEM((S, tile, ATT_WIDTH), f32),
        pltpu.VMEM((S, tile, ATT_KV_WIDTH), f32),
        pltpu.VMEM((S, tile, ATT_KV_WIDTH), f32),
        pltpu.VMEM((S, tile, LANES), f32),
        pltpu.VMEM((S, WINDOW, ATT_KV_WIDTH), f32),
        pltpu.VMEM((S, WINDOW, ATT_KV_WIDTH), f32),
        pltpu.VMEM((S, tile, 2 * SSD_WIDTH), bf16),
    ]
    outs = pl.pallas_call(
        kern, grid=(G, L // tile), in_specs=in_specs, out_specs=out_specs, out_shape=out_shape,
        scratch_shapes=scratch, name="prompt_mixer",
        compiler_params=pltpu.CompilerParams(
            dimension_semantics=("arbitrary", "arbitrary"), vmem_limit_bytes=VMEM_LIMIT),
    )(x.reshape(S, G, L, D), gpre, win, convw, convb, dtb, alog, dskip_e, gssd, sinks, wout, gpost)
    return tuple(o.reshape(B, *o.shape[2:]) for o in outs)


def _memkv_kernel(m_ref, g_ref, wk_ref, wv_ref, k_ref, v_ref, kh_ref, vh_ref):
    mn = _rmsnorm(m_ref[...], g_ref[...]).astype(bf16)
    k = jnp.dot(mn, wk_ref[...], preferred_element_type=f32)
    v = jnp.dot(mn, wv_ref[...], preferred_element_type=f32)
    k_ref[...] = k.astype(bf16)
    v_ref[...] = v.astype(bf16)
    for hd in range(X_HEADS):
        kh_ref[:, hd, :] = k[:, hd * X_HEAD_DIM:(hd + 1) * X_HEAD_DIM]
        vh_ref[:, hd, :] = v[:, hd * X_HEAD_DIM:(hd + 1) * X_HEAD_DIM]


def _memkv(mem2d, g, wk, wv, tile):
    n, d = mem2d.shape
    row = pl.BlockSpec((tile, d), lambda i: (i, 0))
    hrow = pl.BlockSpec((tile, X_HEADS, X_HEAD_DIM), lambda i: (i, 0, 0))
    flat = jax.ShapeDtypeStruct((n, d), bf16)
    heads = jax.ShapeDtypeStruct((n, X_HEADS, X_HEAD_DIM), f32)
    return pl.pallas_call(
        _memkv_kernel, grid=(n // tile,),
        in_specs=[row, _const_spec(g.shape), _const_spec(wk.shape), _const_spec(wv.shape)],
        out_specs=(row, row, hrow, hrow),
        out_shape=(flat, flat, heads, heads),
        name="memory_kv",
        compiler_params=pltpu.CompilerParams(
            dimension_semantics=("arbitrary",), vmem_limit_bytes=VMEM_LIMIT),
    )(mem2d, g, wk, wv)


def _prompt_xattn_kernel(x_ref, gpre_ref, wq_ref, mk_ref, mv_ref, wo_ref, gpost_ref,
                         wg_ref, wu_ref, wd_ref, y_ref, wg_out, wu_out, wd_out):
    hs = range(X_HEADS)
    sl = [slice(hd * X_HEAD_DIM, (hd + 1) * X_HEAD_DIM) for hd in hs]

    def stream(s):
        x = x_ref[s, 0]
        hn = _rmsnorm(x, gpre_ref[...]).astype(bf16)
        yield
        q = []
        for hd in hs:
            q.append(jnp.dot(hn, wq_ref[:, sl[hd]], preferred_element_type=f32))
            yield
        sc = [_dot_nt(q[hd], mk_ref[s, 0, :, sl[hd]]) for hd in hs]
        yield
        m = [jnp.max(sc[hd], axis=-1, keepdims=True) for hd in hs]
        e = [jnp.exp(sc[hd] - m[hd]) for hd in hs]
        yield
        r = [1.0 / jnp.sum(e[hd], axis=-1, keepdims=True) for hd in hs]
        p = [(e[hd] * r[hd]).astype(bf16) for hd in hs]
        yield
        o = jnp.concatenate([_dot(p[hd], mv_ref[s, 0, :, sl[hd]]) for hd in hs], axis=-1).astype(bf16)
        yield
        c = []
        for hd in hs:
            c.append(jnp.dot(o, wo_ref[:, sl[hd]], preferred_element_type=f32))
            yield
        y_ref[s, 0] = x + _rmsnorm(jnp.concatenate(c, axis=-1), gpost_ref[...])

    live = [stream(s) for s in range(STREAMS)]
    for _ in range(XATTN_LEAD):
        next(live[0], None)
    while live:
        live = [g for g in live if next(g, True) is None]

    wg_out[...] = wg_ref[...].astype(bf16)
    wu_out[...] = wu_ref[...].astype(bf16)
    wd_out[...] = wd_ref[...].astype(bf16)


def _prompt_xattn(x, gpre, wq, mk, mv, wo, gpost, ffn_w, tile):
    B, L, D = x.shape
    S = STREAMS
    G = B // S
    nt = L // tile
    xs = pl.BlockSpec((S, 1, tile, D), lambda b, i: (0, b, i, 0))
    ms = pl.BlockSpec((S, 1, N_MEM, D), lambda b, i: (0, b, 0, 0))

    def slab(w):
        return pl.BlockSpec((w.shape[0] // (G * nt), w.shape[1]), lambda b, i: (b * nt + i, 0))

    outs = pl.pallas_call(
        _prompt_xattn_kernel, grid=(G, nt),
        in_specs=[xs, _const_spec(gpre.shape), _const_spec(wq.shape), ms, ms,
                  _const_spec(wo.shape), _const_spec(gpost.shape)] + [slab(w) for w in ffn_w],
        out_specs=(xs,) + tuple(slab(w) for w in ffn_w),
        out_shape=(jax.ShapeDtypeStruct((S, G, L, D), f32),)
        + tuple(jax.ShapeDtypeStruct(w.shape, bf16) for w in ffn_w),
        name="prompt_xattn",
        compiler_params=pltpu.CompilerParams(
            dimension_semantics=("arbitrary", "arbitrary"), vmem_limit_bytes=VMEM_LIMIT),
    )(x.reshape(S, G, L, D), gpre, wq, mk.reshape(S, G, N_MEM, D), mv.reshape(S, G, N_MEM, D),
      wo, gpost, *ffn_w)
    return (outs[0].reshape(B, L, D),) + tuple(outs[1:])


def _ffn_kernel(xp_ref, xs_ref, gpre_ref, wg_ref, wu_ref, wd_ref, gpost_ref, yp_ref, ys_ref):
    def rows(x_ref):
        x = x_ref[...]
        hf = _rmsnorm(x, gpre_ref[...]).astype(bf16)
        gate = jnp.dot(hf, wg_ref[...], preferred_element_type=f32)
        up = jnp.dot(hf, wu_ref[...], preferred_element_type=f32)
        act = (_silu(gate) * up).astype(bf16)
        f = jnp.dot(act, wd_ref[...], preferred_element_type=f32)
        return x + _rmsnorm(f, gpost_ref[...])

    i = pl.program_id(0)
    last = pl.num_programs(0) - 1

    @pl.when(i < last)
    def _prompt_rows():
        yp_ref[...] = rows(xp_ref)

    @pl.when(i == last)
    def _sample_rows():
        nb, steps, _ = ys_ref.shape
        y = rows(xs_ref)
        for t in range(steps):
            ys_ref[:, t, :] = y[t * nb:(t + 1) * nb]


def _ffn(xp2d, xs2d, gpre, wg, wu, wd, gpost, tile, steps):
    n, d = xp2d.shape
    assert xs2d.shape == (tile, d)
    steps_p = n // tile
    nb = tile // steps
    prow = pl.BlockSpec((tile, d), lambda i: (jnp.minimum(i, steps_p - 1), 0))
    srow = pl.BlockSpec((tile, d), lambda i: (0, 0))
    return pl.pallas_call(
        _ffn_kernel, grid=(steps_p + 1,),
        in_specs=[prow, srow, _const_spec(gpre.shape), _const_spec(wg.shape), _const_spec(wu.shape),
                  _const_spec(wd.shape), _const_spec(gpost.shape)],
        out_specs=(prow, _const_spec((nb, steps, d))),
        out_shape=(jax.ShapeDtypeStruct((n, d), f32), jax.ShapeDtypeStruct((nb, steps, d), f32)),
        name="ffn",
        compiler_params=pltpu.CompilerParams(
            dimension_semantics=("arbitrary",), vmem_limit_bytes=VMEM_LIMIT),
    )(xp2d, xs2d, gpre, wg, wu, wd, gpost)


def _pad_rows(a, rows):
    if a.shape[0] == rows:
        return a
    return jnp.concatenate([a, jnp.zeros((rows - a.shape[0], a.shape[1]), a.dtype)], axis=0)


def _sample_mixer_kernel(x_ref, cprev_ref, st_ref, ck_ref, cv_ref,
                         gpre_ref, win_ref, convw_ref, convb_ref, dtb_ref, alog_ref,
                         dskip_ref, gssd_ref, sinkcol_ref, wout_ref, gpost_ref,
                         y_ref, ssm_ref, cnew_ref, wk_ref, wv_ref, *, bt, steps):
    R = steps * bt
    half = SSD_WIDTH // SSD_GROUPS
    x = jnp.concatenate([x_ref[:, t, :] for t in range(steps)], axis=0)
    h = _rmsnorm(x, gpre_ref[...]).astype(bf16)
    z = jnp.dot(h, win_ref[:, P_Z:P_XBC], preferred_element_type=f32)
    u = jnp.dot(h, win_ref[:, P_XBC:P_Q], preferred_element_type=f32)
    q = jnp.dot(h, win_ref[:, P_Q:P_K], preferred_element_type=f32)
    k_new = jnp.dot(h, win_ref[:, P_K:P_V], preferred_element_type=f32)
    v_new = jnp.dot(h, win_ref[:, P_V:P_DT], preferred_element_type=f32)
    dt_raw = jnp.dot(h, win_ref[:, P_DT:P_END], preferred_element_type=f32)
    lane = lax.broadcasted_iota(jnp.int32, (1, LANES), 1)
    dt = jnp.where(lane < SSD_HEADS, _softplus(dt_raw + dtb_ref[...]), 0.0)

    def slab(a, t):
        return a[t * bt:(t + 1) * bt]

    HD = ATT_HEAD_DIM
    GH = ATT_HEADS // ATT_KV_HEADS
    GR = GH * R
    reps = GR // bt
    kvg = range(ATT_KV_HEADS)
    keep = WINDOW - steps

    kn_t = _pad_rows(k_new, LANES).T
    vn_t = _pad_rows(v_new, LANES).T

    ridx = lax.broadcasted_iota(jnp.int32, (GR, 1), 0)
    rb = ridx % bt
    rt = (ridx // bt) % steps
    qg = [jnp.concatenate([q[:, (g * GH + hl) * HD:(g * GH + hl + 1) * HD] for hl in range(GH)],
                          axis=0) for g in kvg]
    qg_b = [qg[g].astype(bf16) for g in kvg]
    s_cb = [[_dot(qg_b[g], ck_ref[b, g]) for b in range(bt)] for g in kvg]

    hist = [cprev_ref[j] for j in range(SSD_CONV - 1)] + [slab(u, t) for t in range(steps)]
    xbc_t = []
    for t in range(steps):
        acc = convb_ref[...]
        for j in range(SSD_CONV):
            acc = acc + hist[t + j] * convw_ref[j:j + 1, :]
        xbc_t.append(_silu(acc))
    for j in range(SSD_CONV - 1):
        cnew_ref[j] = hist[steps + j]
    xbc = jnp.concatenate(xbc_t, axis=0)
    xs = xbc[:, 0:SSD_WIDTH]
    bm = xbc[:, SSD_WIDTH:SSD_WIDTH + half]
    cm = xbc[:, SSD_WIDTH + half:SSD_CONV_DIM]
    a_row = -jnp.exp(alog_ref[...])
    adt = dt * a_row
    cs_t = [slab(adt, 0)]
    for t in range(1, steps):
        cs_t.append(cs_t[-1] + slab(adt, t))
    cs = jnp.concatenate(cs_t, axis=0)
    tot = cs_t[-1]
    tot_rows = jnp.concatenate([tot] * steps, axis=0)
    expand = _head_expand_matrix()
    expand2 = jnp.concatenate([expand, expand], axis=0)
    pairs = [(t, s2) for t in range(steps) for s2 in range(t)]
    fac = jnp.concatenate([dt, jnp.exp(cs), jnp.exp(tot_rows - cs), jnp.exp(tot)]
                          + [jnp.exp(cs_t[t] - cs_t[s2]) for t, s2 in pairs], axis=0)
    f_hi, f_lo = _split2(fac)
    fac_e = jnp.dot(jnp.concatenate([f_hi, f_lo], axis=1), expand2, preferred_element_type=f32)
    gr = lax.broadcasted_iota(jnp.int32, (half, SSD_WIDTH), 0)
    gc = lax.broadcasted_iota(jnp.int32, (half, SSD_WIDTH), 1)
    gsum = ((gr >> 7) == (gc >> 8)).astype(bf16)
    gsum2 = jnp.concatenate([gsum, gsum], axis=0)
    cb_pairs = [(t, s2) for t in range(steps) for s2 in range(t + 1)]
    prod = jnp.concatenate([slab(cm, t) * slab(bm, s2) for t, s2 in cb_pairs], axis=0)
    c_hi, c_lo = _split2(prod)
    cb_e = jnp.dot(jnp.concatenate([c_hi, c_lo], axis=1), gsum2, preferred_element_type=f32)

    sel_r = lax.broadcasted_iota(jnp.int32, (LANES, WINDOW), 0)
    sel_l = lax.broadcasted_iota(jnp.int32, (LANES, WINDOW), 1)
    sel = [((sel_r % bt == b) & (sel_r < R) & (sel_l - keep == sel_r // bt)).astype(bf16)
           for b in range(bt)]
    new_k = [_dot_x2(kn_t, sel[b]) for b in range(bt)]
    new_v = [_dot_x2(vn_t, sel[b]) for b in range(bt)]

    jcol = lax.broadcasted_iota(jnp.int32, (GR, WINDOW), 1)
    in_window = jcol > rt
    s_c, m, s_n = [], [], []
    for g in kvg:
        acc = jnp.zeros((GR, WINDOW), f32)
        for b in range(bt):
            acc = jnp.where(rb == b, s_cb[g][b], acc)
        s_c.append(jnp.where(in_window, acc, -jnp.inf))
    sink = [sinkcol_ref[g * GR:(g + 1) * GR, :] for g in kvg]
    for g in kvg:
        mg = jnp.maximum(jnp.max(s_c[g], axis=-1, keepdims=True), sink[g])
        sn_g = []
        for t2 in range(steps):
            kt = jnp.concatenate([slab(k_new, t2)[:, g * HD:(g + 1) * HD]] * reps, axis=0)
            sn = jnp.where(rt >= t2, jnp.sum(qg[g] * kt, axis=-1, keepdims=True), -jnp.inf)
            sn_g.append(sn)
            mg = jnp.maximum(mg, sn)
        m.append(mg)
        s_n.append(sn_g)
    e_c = [jnp.exp(s_c[g] - m[g]) for g in kvg]
    e_n = [[jnp.exp(sn - m[g]) for sn in s_n[g]] for g in kvg]
    rinv = []
    for g in kvg:
        den = jnp.sum(e_c[g], axis=-1, keepdims=True) + jnp.exp(sink[g] - m[g])
        for en in e_n[g]:
            den = den + en
        rinv.append(1.0 / den)
    p_c = [(e_c[g] * rinv[g]).astype(bf16) for g in kvg]

    xdt = xs * fac_e[0:R]
    ecs_e = fac_e[R:2 * R]
    w_end = xdt * fac_e[2 * R:3 * R]
    dec_e = fac_e[3 * R:3 * R + bt]
    pair_decay = {pr: fac_e[3 * R + (n + 1) * bt:3 * R + (n + 2) * bt] for n, pr in enumerate(pairs)}
    y_t = []
    for t in range(steps):
        acc = None
        for s2 in range(t + 1):
            n = cb_pairs.index((t, s2))
            coef = cb_e[n * bt:(n + 1) * bt]
            if s2 < t:
                coef = coef * pair_decay[(t, s2)]
            term = coef * slab(xdt, s2)
            acc = term if acc is None else acc + term
        y_t.append(acc)
    y_intra = jnp.concatenate(y_t, axis=0)
    b_idx = lax.broadcasted_iota(jnp.int32, (bt, 1, LANES), 0)
    l_idx = lax.broadcasted_iota(jnp.int32, (bt, 1, LANES), 2)
    pair = ((l_idx & (bt - 1)) == b_idx) & (l_idx < R)
    own = (l_idx == b_idx)
    gsl = [slice(g * half, (g + 1) * half) for g in range(SSD_GROUPS)]
    h0 = [st_ref[:, gsl[g], :] for g in range(SSD_GROUPS)]
    zz = [_dot_nt(h0[g].reshape(bt * half, SSD_STATE),
                  _pad_rows(cm[:, g * SSD_STATE:(g + 1) * SSD_STATE], LANES)).reshape(bt, half, LANES)
          for g in range(SSD_GROUPS)]
    wt = [_pad_rows(w_end[:, gsl[g]], LANES).T for g in range(SSD_GROUPS)]
    contrib = [_dot(jnp.where(pair, wt[g][None], 0.0).reshape(bt * half, LANES),
                    _pad_rows(bm[:, g * SSD_STATE:(g + 1) * SSD_STATE], LANES)
                    ).reshape(bt, half, SSD_STATE) for g in range(SSD_GROUPS)]
    dec_t = [_pad_rows(dec_e[:, gsl[g]], LANES).T for g in range(SSD_GROUPS)]

    o = []
    for g in kvg:
        pv = [_dot_nt(p_c[g], cv_ref[b, g]) for b in range(bt)]
        og = jnp.zeros((GR, HD), f32)
        for b in range(bt):
            og = jnp.where(rb == b, pv[b], og)
        for t2 in range(steps):
            vt = jnp.concatenate([slab(v_new, t2)[:, g * HD:(g + 1) * HD]] * reps, axis=0)
            og = og + (e_n[g][t2] * rinv[g]) * vt
        o.append(og)

    lane_w = lax.broadcasted_iota(jnp.int32, (1, WINDOW), 1)
    for b in range(bt):
        for g in kvg:
            gs = slice(g * HD, (g + 1) * HD)
            wk_ref[b, g] = jnp.where(lane_w < keep, pltpu.roll(ck_ref[b, g], keep, 1), new_k[b][gs])
            wv_ref[b, g] = jnp.where(lane_w < keep, pltpu.roll(cv_ref[b, g], keep, 1), new_v[b][gs])

    y_off_parts = []
    for g in range(SSD_GROUPS):
        yt = jnp.sum(jnp.where(pair, zz[g], 0.0), axis=0)
        y_off_parts.append(yt.T[0:R, :])
        dec = jnp.sum(jnp.where(own, dec_t[g][None], 0.0), axis=-1, keepdims=True)
        ssm_ref[:, gsl[g], :] = h0[g] * dec + contrib[g]
    y_off = jnp.concatenate(y_off_parts, axis=-1) * ecs_e
    y = y_intra + y_off + xs * dskip_ref[...]
    y_ssd = _gated_group_norm(y, z, gssd_ref[...])

    mix = jnp.dot(y_ssd.astype(bf16), wout_ref[0:SSD_WIDTH, :], preferred_element_type=f32)
    for g in kvg:
        for hl in range(GH):
            hd = g * GH + hl
            mix = mix + jnp.dot(o[g][hl * R:(hl + 1) * R].astype(bf16),
                                wout_ref[SSD_WIDTH + hd * HD:SSD_WIDTH + (hd + 1) * HD, :],
                                preferred_element_type=f32)
    y_ref[...] = (x + _rmsnorm(mix, gpost_ref[...])).reshape(steps, bt, D_MODEL)


def _sample_mixer(x, cprev_tm, st, ck, cv, gpre, win, convw, convb, dtb, alog, dskip_e, gssd,
                  sinkcol, wout, gpost, bt):
    nb, steps, D = x.shape
    kern = functools.partial(_sample_mixer_kernel, bt=bt, steps=steps)
    tm = lambda w: pl.BlockSpec((steps, bt, w), lambda i: (0, i, 0))
    win_spec = pl.BlockSpec((bt, ATT_KV_HEADS, ATT_HEAD_DIM, WINDOW), lambda i: (i, 0, 0, 0))
    in_specs = [
        pl.BlockSpec((bt, steps, D), lambda i: (i, 0, 0)),
        pl.BlockSpec((SSD_CONV - 1, bt, SSD_CONV_DIM), lambda i: (0, i, 0)),
        pl.BlockSpec((bt, SSD_WIDTH, SSD_STATE), lambda i: (i, 0, 0)),
        win_spec, win_spec,
    ] + [_const_spec(a.shape) for a in (gpre, win, convw, convb, dtb, alog, dskip_e, gssd,
                                        sinkcol, wout, gpost)]
    out_specs = (
        tm(D),
        pl.BlockSpec((bt, SSD_WIDTH, SSD_STATE), lambda i: (i, 0, 0)),
        pl.BlockSpec((SSD_CONV - 1, bt, SSD_CONV_DIM), lambda i: (0, i, 0)),
        win_spec, win_spec,
    )
    out_shape = (
        jax.ShapeDtypeStruct((steps, nb, D), f32),
        jax.ShapeDtypeStruct((nb, SSD_WIDTH, SSD_STATE), f32),
        jax.ShapeDtypeStruct((SSD_CONV - 1, nb, SSD_CONV_DIM), f32),
        jax.ShapeDtypeStruct(ck.shape, f32),
        jax.ShapeDtypeStruct(cv.shape, f32),
    )
    return pl.pallas_call(
        kern, grid=(nb // bt,), in_specs=in_specs, out_specs=out_specs, out_shape=out_shape,
        name="sample_mixer",
        compiler_params=pltpu.CompilerParams(
            dimension_semantics=("arbitrary",), vmem_limit_bytes=VMEM_LIMIT),
    )(x, cprev_tm, st, ck, cv, gpre, win, convw, convb, dtb, alog, dskip_e, gssd, sinkcol,
      wout, gpost)


def _sample_xattn_kernel(x_ref, mk_ref, mv_ref, gpre_ref, wq_ref, wo_ref, gpost_ref, y_ref,
                         q_sc, o_sc, *, bt, steps):
    i = pl.program_id(0)
    nb = x_ref.shape[1]
    R = steps * bt
    nrow = bt * N_MEM * X_HEADS

    @pl.when(i == 0)
    def _project_queries():
        x_all = x_ref[...].reshape(steps * nb, D_MODEL)
        hn = _rmsnorm(x_all, gpre_ref[...]).astype(bf16)
        q_sc[...] = jnp.dot(hn, wq_ref[...], preferred_element_type=f32)

    tile_rows = [pl.ds(pl.multiple_of(t * nb + i * bt, bt), bt) for t in range(steps)]
    q = jnp.concatenate([q_sc[r, :] for r in tile_rows], axis=0)
    qs = jnp.concatenate([q[:, hd * X_HEAD_DIM:(hd + 1) * X_HEAD_DIM] for hd in range(X_HEADS)],
                         axis=0)
    kall = mk_ref[...].reshape(nrow, X_HEAD_DIM)
    vall = mv_ref[...].reshape(nrow, X_HEAD_DIM)
    ncol = X_HEADS * R
    seq_rows = N_MEM * X_HEADS
    z = _dot_nt(kall, qs).reshape(bt, seq_rows, ncol)
    v_b = [vall[b * seq_rows:(b + 1) * seq_rows].astype(bf16) for b in range(bt)]
    b_i = lax.broadcasted_iota(jnp.int32, (bt, 1, ncol), 0)
    c_i = lax.broadcasted_iota(jnp.int32, (bt, 1, ncol), 2)
    zc = jnp.sum(jnp.where(c_i % bt == b_i, z, 0.0), axis=0)
    zc = zc.reshape(seq_rows // SUBLANES, SUBLANES, ncol)
    r_h = lax.broadcasted_iota(jnp.int32, (1, SUBLANES, ncol), 1) % X_HEADS
    c_h = lax.broadcasted_iota(jnp.int32, (1, SUBLANES, ncol), 2) // R
    zc = jnp.where(r_h == c_h, zc, -jnp.inf).reshape(seq_rows, ncol)
    m = jnp.max(zc, axis=0, keepdims=True)
    e = jnp.exp(zc - m)
    p = e * (1.0 / jnp.sum(e, axis=0, keepdims=True))
    col_b = lax.broadcasted_iota(jnp.int32, (1, ncol), 1) % bt
    tn = (((0,), (0,)), ((), ()))
    o = None
    for b in range(bt):
        p_b = jnp.where(col_b == b, p, 0.0).astype(bf16)
        o_b = lax.dot_general(p_b, v_b[b], tn, preferred_element_type=f32)
        o = o_b if o is None else o + o_b
    o = jnp.concatenate([o[hd * R:(hd + 1) * R] for hd in range(X_HEADS)], axis=-1)
    for t in range(steps):
        o_sc[tile_rows[t], :] = o[t * bt:(t + 1) * bt]

    @pl.when(i == pl.num_programs(0) - 1)
    def _project_outputs():
        x_all = x_ref[...].reshape(steps * nb, D_MODEL)
        cc = _dot(o_sc[...], wo_ref[...])
        y_ref[...] = (x_all + _rmsnorm(cc, gpost_ref[...])).reshape(steps, nb, D_MODEL)


def _sample_xattn(x_tm, mk, mv, gpre, wq, wo, gpost, bt):
    steps, nb, D = x_tm.shape
    kern = functools.partial(_sample_xattn_kernel, bt=bt, steps=steps)
    xs = _const_spec(x_tm.shape)
    ms = pl.BlockSpec((bt, N_MEM, X_HEADS, X_HEAD_DIM), lambda i: (i, 0, 0, 0))
    return pl.pallas_call(
        kern, grid=(nb // bt,),
        in_specs=[xs, ms, ms, _const_spec(gpre.shape), _const_spec(wq.shape),
                  _const_spec(wo.shape), _const_spec(gpost.shape)],
        out_specs=xs, out_shape=jax.ShapeDtypeStruct((steps, nb, D), f32),
        scratch_shapes=[pltpu.VMEM((steps * nb, D), f32), pltpu.VMEM((steps * nb, D), f32)],
        name="sample_xattn",
        compiler_params=pltpu.CompilerParams(
            dimension_semantics=("arbitrary",), vmem_limit_bytes=VMEM_LIMIT),
    )(x_tm, mk, mv, gpre, wq, wo, gpost)


def _win_prep_kernel(wt_ref, o_ref):
    piece = 2 * LANES

    def put(src_lo, n, dst_lo, scale=None):
        for c in range(0, n, piece):
            t = wt_ref[src_lo + c:src_lo + c + piece, :].T
            if scale is not None:
                t = t * scale
            o_ref[:, dst_lo + c:dst_lo + c + piece] = t.astype(bf16)

    put(W_Z, P_Q - P_Z, P_Z)
    put(W_Q, P_K - P_Q, P_Q, ATT_SCALE)
    put(W_K, P_DT - P_K, P_K)
    dt_rows = jnp.concatenate([wt_ref[W_DT:W_DT + SSD_HEADS, :],
                               jnp.zeros((LANES - SSD_HEADS, D_MODEL), f32)], axis=0)
    o_ref[:, P_DT:P_END] = dt_rows.T.astype(bf16)


def _win_prep(w_t):
    return pl.pallas_call(
        _win_prep_kernel, out_shape=jax.ShapeDtypeStruct((D_MODEL, P_END), bf16),
        name="win_prep",
        compiler_params=pltpu.CompilerParams(vmem_limit_bytes=VMEM_LIMIT),
    )(w_t)


def _row(v, width=None):
    v = v.reshape(1, -1).astype(f32)
    if width is not None and v.shape[1] < width:
        v = jnp.pad(v, ((0, 0), (0, width - v.shape[1])))
    return v


def kernel(x_prompt, x_sample, state_ssm, state_conv, cache_win_k, cache_win_v, cache_mem_k, cache_mem_v, mem_prompt, g_mix_pre, w_in, conv_w, conv_b, dt_bias, a_log, d_skip, g_ssd_norm, sinks, w_out, g_mix_post, g_x_pre, w_xq, g_mem, w_xk, w_xv, w_xo, g_x_post, g_ffn_pre, w_gate, w_up, w_down, g_ffn_post):
    depth = w_in.shape[0]
    assert depth == 1
    B, L, D = x_prompt.shape
    NB, steps, _ = x_sample.shape
    li = 0

    win_p = _win_prep(jnp.transpose(w_in[li]))
    wo_b = w_out[li].astype(bf16)

    gpre, gpost = _row(g_mix_pre[li]), _row(g_mix_post[li])
    convw, convb = conv_w[li].astype(f32), _row(conv_b[li])
    dtb, alog = _row(dt_bias[li], LANES), _row(a_log[li], LANES)
    dskip_e = _row(jnp.repeat(d_skip[li], SSD_HEAD_DIM))
    gssd = _row(g_ssd_norm[li])
    sk = sinks[li].astype(f32)

    wxq_b, wxo_b = (w_xq[li] * X_SCALE).astype(bf16), w_xo[li].astype(bf16)
    gxpre, gxpost = _row(g_x_pre[li]), _row(g_x_post[li])
    gfpre, gfpost = _row(g_ffn_pre[li]), _row(g_ffn_post[li])

    bt, bt_mix = 8, 16
    cprev_tm = jnp.transpose(state_conv[li], (1, 0, 2))
    st = state_ssm[li].reshape(NB, SSD_WIDTH, SSD_STATE)
    ck = jnp.transpose(cache_win_k[li], (0, 2, 3, 1))
    cv = jnp.transpose(cache_win_v[li], (0, 2, 3, 1))
    sinkcol = jnp.repeat(sk, steps * bt_mix).reshape(ATT_HEADS * steps * bt_mix, 1)
    x1s, s_ssm, cnew_tm, s_wk, s_wv = _sample_mixer(
        x_sample, cprev_tm, st, ck, cv, gpre, win_p, convw, convb, dtb, alog, dskip_e, gssd,
        sinkcol, wo_b, gpost, bt=bt_mix)
    cmk = cache_mem_k.reshape(NB, N_MEM, X_HEADS, X_HEAD_DIM)
    cmv = cache_mem_v.reshape(NB, N_MEM, X_HEADS, X_HEAD_DIM)
    x2s = _sample_xattn(x1s, cmk, cmv, gxpre, wxq_b, wxo_b, gxpost, bt=bt)

    mk2d, mv2d, mk4, mv4 = _memkv(mem_prompt.reshape(B * N_MEM, D), _row(g_mem[li]),
                                  w_xk[li].astype(bf16), w_xv[li].astype(bf16), tile=512)
    mk3, mv3 = mk2d.reshape(B, N_MEM, D), mv2d.reshape(B, N_MEM, D)
    x1, p_ssm, p_conv8, p_wk, p_wv = _prompt_mixer(
        x_prompt, gpre, win_p, convw, convb, dtb, alog, dskip_e, gssd, sk, wo_b, gpost, tile=512)
    x2, wg_b, wu_b, wd_b = _prompt_xattn(x1, gxpre, wxq_b, mk3, mv3, wxo_b, gxpost,
                                         (w_gate[li], w_up[li], w_down[li]), tile=512)

    yp2d, ys = _ffn(x2.reshape(B * L, D), x2s.reshape(steps * NB, D), gfpre, wg_b, wu_b, wd_b,
                    gfpost, tile=steps * NB, steps=steps)
    yp = yp2d.reshape(B, L, D)

    s_conv = jnp.transpose(cnew_tm, (1, 0, 2))
    kv_shape = (ATT_KV_HEADS, ATT_HEAD_DIM)
    return (
        yp, ys,
        p_ssm.reshape(1, B, SSD_HEADS, SSD_HEAD_DIM, SSD_STATE),
        p_conv8[:, SUBLANES - (SSD_CONV - 1):, :][None],
        jnp.transpose(p_wk.reshape(B, *kv_shape, WINDOW), (0, 3, 1, 2))[None],
        jnp.transpose(p_wv.reshape(B, *kv_shape, WINDOW), (0, 3, 1, 2))[None],
        mk4.reshape(1, B, N_MEM, X_HEADS, X_HEAD_DIM), mv4.reshape(1, B, N_MEM, X_HEADS, X_HEAD_DIM),
        s_ssm.reshape(1, NB, SSD_HEADS, SSD_HEAD_DIM, SSD_STATE),
        s_conv[None],
        jnp.transpose(s_wk, (0, 3, 1, 2))[None], jnp.transpose(s_wv, (0, 3, 1, 2))[None],
    )
```

```python
import functools

import jax
import jax.numpy as jnp
from jax import lax
from jax.experimental import pallas as pl
from jax.experimental.pallas import tpu as pltpu

f32 = jnp.float32
bf16 = jnp.bfloat16

D_MODEL = 1024
EPS = 1e-6
N_MEM = 256
SSD_HEADS = 8
SSD_HEAD_DIM = 64
SSD_WIDTH = 512
SSD_GROUPS = 2
SSD_STATE = 128
SSD_CONV = 4
SSD_CHUNK = 128
SSD_CONV_DIM = 1024
ATT_HEADS = 8
ATT_KV_HEADS = 2
ATT_HEAD_DIM = 64
ATT_WIDTH = 512
ATT_KV_WIDTH = 128
WINDOW = 128
ATT_SCALE = ATT_HEAD_DIM ** -0.5
X_HEADS = 4
X_HEAD_DIM = 256
X_SCALE = X_HEAD_DIM ** -0.5
D_FF = 2816
LANES = 128
SUBLANES = 8
VMEM_LIMIT = 56 * 1024 * 1024
PROMPT_TILE = 512
XATTN_SEQS = 8
MIXER_SEQS = 16
STREAMS = 2
PROJ_PIECES, SCAN_PIECES, OUT_PIECES = 19, 10, 5
CONV_BLOCKS = SSD_CONV_DIM // LANES
XATTN_LEAD = 5

P_Z, P_XBC, P_Q, P_K, P_V, P_DT, P_END = 0, 512, 1536, 2048, 2176, 2304, 2432
W_Z, W_DT, W_Q, W_K = 0, 1536, 1544, 2056


def _dot(a, b):
    return jnp.dot(a.astype(bf16), b.astype(bf16), preferred_element_type=f32)


def _dot_nt(a, b):
    return lax.dot_general(a.astype(bf16), b.astype(bf16), (((1,), (1,)), ((), ())),
                           preferred_element_type=f32)


def _split2(x):
    hi = x.astype(bf16)
    lo = (x - hi.astype(f32)).astype(bf16)
    return hi, lo


def _dot_x2(x, m):
    hi, lo = _split2(x)
    return (jnp.dot(hi, m, preferred_element_type=f32)
            + jnp.dot(lo, m, preferred_element_type=f32))


def _rmsnorm(x, g):
    ms = jnp.mean(x * x, axis=-1, keepdims=True)
    return x * lax.rsqrt(ms + EPS) * g


def _silu(x):
    return x * jax.nn.sigmoid(x)


def _softplus(x):
    return jnp.maximum(x, 0.0) + jnp.log1p(jnp.exp(-jnp.abs(x)))


def _head_expand_matrix():
    r = lax.broadcasted_iota(jnp.int32, (LANES, SSD_WIDTH), 0)
    c = lax.broadcasted_iota(jnp.int32, (LANES, SSD_WIDTH), 1)
    return (r == (c >> 6)).astype(bf16)


def _gated_group_norm(y, z, g):
    u = y * _silu(z)
    half = SSD_WIDTH // SSD_GROUPS
    parts = []
    for gi in range(SSD_GROUPS):
        ug = u[:, gi * half:(gi + 1) * half]
        parts.append(ug * lax.rsqrt(jnp.mean(ug * ug, axis=-1, keepdims=True) + EPS))
    return jnp.concatenate(parts, axis=-1) * g


def _prompt_mixer_kernel(x_ref, gpre_ref, win_ref, convw_ref, convb_ref, dtb_ref, alog_ref,
                         dskip_ref, gssd_ref, sinks_ref, wout_ref, gpost_ref,
                         y_ref, ssm_ref, conv_ref, wk_ref, wv_ref,
                         statet_sc, xbc_ext_sc, xbc_sc, z_sc, q_sc, k_sc, v_sc, dt_sc,
                         kprev_sc, vprev_sc, mix_sc, *, tile):
    i = pl.program_id(1)
    NC = tile // SSD_CHUNK
    ns = range(NC)
    C = SSD_CHUNK
    PW = 2 * LANES

    @pl.when(i == 0)
    def _init():
        statet_sc[...] = jnp.zeros_like(statet_sc)
        xbc_ext_sc[:, :, 0:SUBLANES, :] = jnp.zeros((STREAMS, CONV_BLOCKS, SUBLANES, LANES), f32)
        kprev_sc[...] = jnp.zeros_like(kprev_sc)
        vprev_sc[...] = jnp.zeros_like(vprev_sc)

    lane = lax.broadcasted_iota(jnp.int32, (1, LANES), 1)
    a_row = -jnp.exp(alog_ref[...])
    expand = _head_expand_matrix()
    expand2 = jnp.concatenate([expand, expand], axis=0)
    row_i = lax.broadcasted_iota(jnp.int32, (C, C), 0)
    col_i = lax.broadcasted_iota(jnp.int32, (C, C), 1)
    lower = col_i <= row_i
    tri = lower.astype(bf16)
    tri3 = jnp.concatenate([tri, tri, tri], axis=1)
    lo_half = lane < ATT_HEAD_DIM
    half = SSD_WIDTH // SSD_GROUPS
    head_of_lane = lax.broadcasted_iota(jnp.int32, (1, half), 1) >> 6
    rows = [slice(n * C, (n + 1) * C) for n in ns]

    def stream(s):
        statet_s, xbc_ext_s, xbc_s, z_s = statet_sc.at[s], xbc_ext_sc.at[s], xbc_sc.at[s], z_sc.at[s]
        q_s, k_s, v_s, dt_s = q_sc.at[s], k_sc.at[s], v_sc.at[s], dt_sc.at[s]
        kprev_s, vprev_s, mix_s = kprev_sc.at[s], vprev_sc.at[s], mix_sc.at[s]

        x = x_ref[s, 0]
        h = _rmsnorm(x, gpre_ref[...]).astype(bf16)
        yield

        def proj(col):
            return jnp.dot(h, win_ref[:, col:col + PW], preferred_element_type=f32)

        def conv_cols(cb):
            cols = slice(cb * LANES, (cb + 1) * LANES)
            acc = convb_ref[:, cols]
            for j in range(SSD_CONV):
                off = SUBLANES - (SSD_CONV - 1) + j
                acc = acc + xbc_ext_s[cb, off:off + tile, :] * convw_ref[j:j + 1, cols]
            xbc_s[cb] = _silu(acc)

        def xbc_piece(pc):
            res = proj(P_XBC + pc * PW)
            for half_pc in range(PW // LANES):
                xbc_ext_s[(PW // LANES) * pc + half_pc, SUBLANES:SUBLANES + tile, :] = (
                    res[:, half_pc * LANES:(half_pc + 1) * LANES])

        xbc_piece(0)
        yield
        xbc_piece(1)
        yield
        conv_cols(0)
        yield
        xbc_piece(2)
        yield
        conv_cols(1)
        yield
        xbc_piece(3)
        yield
        conv_cols(2)
        yield
        z_s[:, 0:PW] = proj(P_Z)
        yield
        conv_cols(3)
        yield
        z_s[:, PW:2 * PW] = proj(P_Z + PW)
        yield
        conv_cols(4)
        yield
        q_s[:, 0:PW] = proj(P_Q)
        yield
        conv_cols(5)
        yield
        q_s[:, PW:2 * PW] = proj(P_Q + PW)
        yield
        conv_cols(6)
        yield
        kv = proj(P_K)
        k_s[...] = kv[:, 0:LANES]
        v_s[...] = kv[:, LANES:PW]
        yield
        conv_cols(7)
        yield
        dt_raw = jnp.dot(h, win_ref[:, P_DT:P_END], preferred_element_type=f32)
        dt_s[...] = jnp.where(lane < SSD_HEADS, _softplus(dt_raw + dtb_ref[...]), 0.0)
        tail = xbc_ext_s[:, tile:tile + SUBLANES, :]
        conv_ref[s, 0] = jnp.concatenate([tail[cb] for cb in range(CONV_BLOCKS)], axis=-1)
        xbc_ext_s[:, 0:SUBLANES, :] = tail
        yield

        GH = SSD_HEADS // SSD_GROUPS
        heads = range(ATT_HEADS)
        gs = range(SSD_GROUPS)
        zero_b = jnp.zeros((), bf16)

        yield
        def xbc_cols(n, lo, hi):
            return jnp.concatenate([xbc_s[cb, rows[n], :] for cb in range(lo // LANES, hi // LANES)],
                                   axis=-1)
        xs = [xbc_cols(n, 0, SSD_WIDTH) for n in ns]
        bm = [xbc_cols(n, SSD_WIDTH, SSD_WIDTH + half) for n in ns]
        cm = [xbc_cols(n, SSD_WIDTH + half, SSD_CONV_DIM) for n in ns]
        dtc = [dt_s[rows[n], :] for n in ns]
        cs = []
        for n in ns:
            adt = dtc[n] * a_row
            a_hi = adt.astype(bf16)
            a_r1 = adt - a_hi.astype(f32)
            a_mid = a_r1.astype(bf16)
            a_lo = (a_r1 - a_mid.astype(f32)).astype(bf16)
            cs.append(jnp.dot(tri3, jnp.concatenate([a_hi, a_mid, a_lo], axis=0),
                              preferred_element_type=f32))

        yield
        q = [q_s[rows[n], :].astype(bf16) for n in ns]
        k = [k_s[rows[n], :] for n in ns]
        v = [v_s[rows[n], :] for n in ns]
        k_prev = [kprev_s[...]] + k[:-1]
        v_prev = [vprev_s[...]] + v[:-1]
        kprev_s[...] = k[-1]
        vprev_s[...] = v[-1]
        first_bias = jnp.where(i > 0, 0.0, -jnp.inf)
        k_lo, k_hi, v_lo, v_hi = [], [], [], []
        for n in ns:
            kk = jnp.concatenate([k_prev[n], k[n]], axis=0)
            vv = jnp.concatenate([v_prev[n], v[n]], axis=0)
            kk_r = pltpu.roll(kk, ATT_HEAD_DIM, 1)
            vv_r = pltpu.roll(vv, ATT_HEAD_DIM, 1)
            k_lo.append([jnp.where(lo_half, kk, 0.0).astype(bf16), jnp.where(lo_half, kk_r, 0.0).astype(bf16)])
            k_hi.append([jnp.where(lo_half, 0.0, kk_r).astype(bf16), jnp.where(lo_half, 0.0, kk).astype(bf16)])
            v_lo.append([jnp.where(lo_half, vv, 0.0).astype(bf16), jnp.where(lo_half, vv_r, 0.0).astype(bf16)])
            v_hi.append([jnp.where(lo_half, 0.0, vv_r).astype(bf16), jnp.where(lo_half, 0.0, vv).astype(bf16)])
        s_g = [[_dot_nt(jnp.concatenate([q[n][:, (2 * g) * LANES:(2 * g + 1) * LANES],
                                         q[n][:, (2 * g + 1) * LANES:(2 * g + 2) * LANES]], axis=0),
                        jnp.concatenate([k_lo[n][g], k_hi[n][g]], axis=0))
                for g in range(ATT_KV_HEADS)] for n in ns]

        yield
        cs_t = [cs[n].T for n in ns]
        fac_e, cb_all, bm_t = [], [], []
        for n in ns:
            tot = cs[n][C - 1:C, :]
            fac = jnp.concatenate([dtc[n], jnp.exp(cs[n]), jnp.exp(tot - cs[n])], axis=0)
            f_hi = fac.astype(bf16)
            f_lo = (fac - f_hi.astype(f32)).astype(bf16)
            fac_e.append(jnp.dot(jnp.concatenate([f_hi, f_lo], axis=1), expand2,
                                 preferred_element_type=f32))
            cb_all.append(_dot_nt(
                jnp.concatenate([cm[n][:, 0:SSD_STATE], cm[n][:, SSD_STATE:half]], axis=0),
                jnp.concatenate([bm[n][:, 0:SSD_STATE], bm[n][:, SSD_STATE:half]], axis=0)))
            bm_t.append(bm[n].T)

        yield
        def head_scores(n, hd):
            g, jb, sub = hd // 4, (hd // 2) % 2, hd % 2
            s_prev = s_g[n][g][jb * C:(jb + 1) * C, (2 * sub) * C:(2 * sub + 1) * C]
            s_cur = s_g[n][g][jb * C:(jb + 1) * C, (2 * sub + 1) * C:(2 * sub + 2) * C]
            return jnp.where(lower, s_cur, s_prev + first_bias if n == 0 else s_prev)
        sc = [[head_scores(n, hd) for hd in heads] for n in ns]
        sink = [sinks_ref[hd] for hd in heads]
        m = [[jnp.maximum(jnp.max(sc[n][hd], axis=-1, keepdims=True), sink[hd]) for hd in heads]
             for n in ns]

        yield
        xdt = [xs[n] * fac_e[n][0:C] for n in ns]
        ecs_e = [fac_e[n][C:2 * C] for n in ns]
        w_end = [xdt[n] * fac_e[n][2 * C:3 * C] for n in ns]
        xdt_b = [xdt[n].astype(bf16) for n in ns]
        contrib = [[_dot(bm_t[n][g * SSD_STATE:(g + 1) * SSD_STATE, :],
                         w_end[n][:, g * half:(g + 1) * half]) for g in gs] for n in ns]
        decay = [[jnp.exp(jnp.where(lower, cs[n][:, hh:hh + 1] - cs_t[n][hh:hh + 1, :], -jnp.inf))
                  for hh in range(SSD_HEADS)] for n in ns]

        yield
        e = [[jnp.exp(sc[n][hd] - m[n][hd]) for hd in heads] for n in ns]
        den = [[jnp.sum(e[n][hd], axis=-1, keepdims=True) + jnp.exp(sink[hd] - m[n][hd])
                for hd in heads] for n in ns]

        yield
        y_d = []
        for n in ns:
            y_n = []
            for g in gs:
                gl = slice(g * half, (g + 1) * half)
                cb = cb_all[n][g * C:(g + 1) * C, g * C:(g + 1) * C]
                m_parts = [(cb * decay[n][g * GH + r]).astype(bf16) for r in range(GH)]
                x_parts = [jnp.where(head_of_lane == r, xdt_b[n][:, gl], zero_b) for r in range(GH)]
                y_n.append(jnp.dot(jnp.concatenate(m_parts, axis=1), jnp.concatenate(x_parts, axis=0),
                                   preferred_element_type=f32))
            y_d.append(y_n)
        st = [statet_s[g] for g in gs]
        y_off = []
        for n in ns:
            y_off.append([_dot(cm[n][:, g * SSD_STATE:(g + 1) * SSD_STATE], st[g]) for g in gs])
            st = [st[g] * ecs_e[n][C - 1:C, g * half:(g + 1) * half] + contrib[n][g] for g in gs]
        for g in gs:
            statet_s[g] = st[g]

        yield
        p = [[(e[n][hd] * (1.0 / den[n][hd])).astype(bf16) for hd in heads] for n in ns]
        o_g = []
        for n in ns:
            o_n = []
            for g in range(ATT_KV_HEADS):
                p_rows = []
                for jb in range(2):
                    p_cols = []
                    for sub in range(2):
                        ph = p[n][g * 4 + jb * 2 + sub]
                        p_cols += [jnp.where(lower, zero_b, ph), jnp.where(lower, ph, zero_b)]
                    p_rows.append(jnp.concatenate(p_cols, axis=1))
                o_n.append(jnp.dot(jnp.concatenate(p_rows, axis=0),
                                   jnp.concatenate([v_lo[n][g], v_hi[n][g]], axis=0),
                                   preferred_element_type=f32))
            o_g.append(o_n)

        yield
        for n in ns:
            y = (jnp.concatenate([y_d[n][g] + y_off[n][g] * ecs_e[n][:, g * half:(g + 1) * half]
                                  for g in gs], axis=-1)
                 + xs[n] * dskip_ref[...])
            y_ssd = _gated_group_norm(y, z_s[rows[n], :], gssd_ref[...])
            mix_s[rows[n], 0:SSD_WIDTH] = y_ssd.astype(bf16)
        for n in ns:
            for g in range(ATT_KV_HEADS):
                for jb in range(2):
                    lo_l = SSD_WIDTH + (2 * g + jb) * LANES
                    mix_s[rows[n], lo_l:lo_l + LANES] = o_g[n][g][jb * C:(jb + 1) * C].astype(bf16)

        yield

        mix_in = mix_s[...]
        mix = []
        for pc in range(D_MODEL // PW):
            mix.append(jnp.dot(mix_in, wout_ref[:, pc * PW:(pc + 1) * PW], preferred_element_type=f32))
            yield
        y_ref[s, 0] = x + _rmsnorm(jnp.concatenate(mix, axis=-1), gpost_ref[...])

    def mixed(ga, na, gb, nb):
        done_b = 0
        for ka in range(na):
            next(ga, None)
            want_b = ((ka + 1) * nb) // na
            for _ in range(want_b - done_b):
                next(gb, None)
            done_b = want_b

    g0, g1 = [stream(s) for s in range(STREAMS)]
    for _ in range(PROJ_PIECES):
        next(g0, None)
    mixed(g0, SCAN_PIECES, g1, PROJ_PIECES)
    mixed(g1, SCAN_PIECES, g0, OUT_PIECES)
    for g in (g0, g1):
        for _ in g:
            pass

    @pl.when(i == pl.num_programs(1) - 1)
    def _final_state():
        for s in range(STREAMS):
            for g in range(SSD_GROUPS):
                ssm_ref[s, 0, g * half:(g + 1) * half, :] = statet_sc[s, g].T
            wk_ref[s, 0] = k_sc[s, tile - WINDOW:tile, :].T
            wv_ref[s, 0] = v_sc[s, tile - WINDOW:tile, :].T


def _const_spec(shape):
    nd = len(shape)
    return pl.BlockSpec(shape, lambda *_: (0,) * nd)


def _prompt_mixer(x, gpre, win, convw, convb, dtb, alog, dskip_e, gssd, sinks, wout, gpost, tile):
    B, L, D = x.shape
    S = STREAMS
    G = B // S
    kern = functools.partial(_prompt_mixer_kernel, tile=tile)

    def per_seq(rows, width):
        return pl.BlockSpec((S, 1, rows, width), lambda b, i: (0, b, 0, 0))

    tile_spec = pl.BlockSpec((S, 1, tile, D), lambda b, i: (0, b, i, 0))
    out_shape = (
        jax.ShapeDtypeStruct((S, G, L, D), f32),
        jax.ShapeDtypeStruct((S, G, SSD_WIDTH, SSD_STATE), f32),
        jax.ShapeDtypeStruct((S, G, SUBLANES, SSD_CONV_DIM), f32),
        jax.ShapeDtypeStruct((S, G, ATT_KV_WIDTH, WINDOW), f32),
        jax.ShapeDtypeStruct((S, G, ATT_KV_WIDTH, WINDOW), f32),
    )
    in_specs = [
        tile_spec,
        _const_spec(gpre.shape), _const_spec(win.shape), _const_spec(convw.shape),
        _const_spec(convb.shape), _const_spec(dtb.shape), _const_spec(alog.shape),
        _const_spec(dskip_e.shape), _const_spec(gssd.shape),
        pl.BlockSpec(memory_space=pltpu.SMEM),
        _const_spec(wout.shape), _const_spec(gpost.shape),
    ]
    out_specs = (
        tile_spec,
        per_seq(SSD_WIDTH, SSD_STATE),
        per_seq(SUBLANES, SSD_CONV_DIM),
        per_seq(ATT_KV_WIDTH, WINDOW),
        per_seq(ATT_KV_WIDTH, WINDOW),
    )
    scratch = [
        pltpu.VMEM((S, SSD_GROUPS, SSD_STATE, SSD_WIDTH // SSD_GROUPS), f32),
        pltpu.VMEM((S, CONV_BLOCKS, tile + 2 * SUBLANES, LANES), f32),
        pltpu.VMEM((S, CONV_BLOCKS, tile, LANES), f32),
        pltpu.VMEM((S, tile, SSD_WIDTH), f32),
        pltpu.VMEM((S, tile, ATT_WIDTH), f32),
        pltpu.VMEM((S, tile, ATT_KV_WIDTH), f32),
        pltpu.VMEM((S, tile, ATT_KV_WIDTH), f32),
        pltpu.VMEM((S, tile, LANES), f32),
        pltpu.VMEM((S, WINDOW, ATT_KV_WIDTH), f32),
        pltpu.VMEM((S, WINDOW, ATT_KV_WIDTH), f32),
        pltpu.VMEM((S, tile, 2 * SSD_WIDTH), bf16),
    ]
    outs = pl.pallas_call(
        kern, grid=(G, L // tile), in_specs=in_specs, out_specs=out_specs, out_shape=out_shape,
        scratch_shapes=scratch, name="prompt_mixer",
        compiler_params=pltpu.CompilerParams(
            dimension_semantics=("arbitrary", "arbitrary"), vmem_limit_bytes=VMEM_LIMIT),
    )(x.reshape(S, G, L, D), gpre, win, convw, convb, dtb, alog, dskip_e, gssd, sinks, wout, gpost)
    return tuple(o.reshape(B, *o.shape[2:]) for o in outs)


def _memkv_kernel(m_ref, g_ref, wk_ref, wv_ref, k_ref, v_ref, kh_ref, vh_ref):
    mn = _rmsnorm(m_ref[...], g_ref[...]).astype(bf16)
    k = jnp.dot(mn, wk_ref[...], preferred_element_type=f32)
    v = jnp.dot(mn, wv_ref[...], preferred_element_type=f32)
    k_ref[...] = k.astype(bf16)
    v_ref[...] = v.astype(bf16)
    for hd in range(X_HEADS):
        kh_ref[:, hd, :] = k[:, hd * X_HEAD_DIM:(hd + 1) * X_HEAD_DIM]
        vh_ref[:, hd, :] = v[:, hd * X_HEAD_DIM:(hd + 1) * X_HEAD_DIM]


def _memkv(mem2d, g, wk, wv, tile):
    n, d = mem2d.shape
    row = pl.BlockSpec((tile, d), lambda i: (i, 0))
    hrow = pl.BlockSpec((tile, X_HEADS, X_HEAD_DIM), lambda i: (i, 0, 0))
    flat = jax.ShapeDtypeStruct((n, d), bf16)
    heads = jax.ShapeDtypeStruct((n, X_HEADS, X_HEAD_DIM), f32)
    return pl.pallas_call(
        _memkv_kernel, grid=(n // tile,),
        in_specs=[row, _const_spec(g.shape), _const_spec(wk.shape), _const_spec(wv.shape)],
        out_specs=(row, row, hrow, hrow),
        out_shape=(flat, flat, heads, heads),
        name="memory_kv",
        compiler_params=pltpu.CompilerParams(
            dimension_semantics=("arbitrary",), vmem_limit_bytes=VMEM_LIMIT),
    )(mem2d, g, wk, wv)


def _prompt_xattn_kernel(x_ref, gpre_ref, wq_ref, mk_ref, mv_ref, wo_ref, gpost_ref,
                         wg_ref, wu_ref, wd_ref, y_ref, wg_out, wu_out, wd_out):
    hs = range(X_HEADS)
    sl = [slice(hd * X_HEAD_DIM, (hd + 1) * X_HEAD_DIM) for hd in hs]

    def stream(s):
        x = x_ref[s, 0]
        hn = _rmsnorm(x, gpre_ref[...]).astype(bf16)
        yield
        q = []
        for hd in hs:
            q.append(jnp.dot(hn, wq_ref[:, sl[hd]], preferred_element_type=f32))
            yield
        sc = [_dot_nt(q[hd], mk_ref[s, 0, :, sl[hd]]) for hd in hs]
        yield
        m = [jnp.max(sc[hd], axis=-1, keepdims=True) for hd in hs]
        e = [jnp.exp(sc[hd] - m[hd]) for hd in hs]
        yield
        r = [1.0 / jnp.sum(e[hd], axis=-1, keepdims=True) for hd in hs]
        p = [(e[hd] * r[hd]).astype(bf16) for hd in hs]
        yield
        o = jnp.concatenate([_dot(p[hd], mv_ref[s, 0, :, sl[hd]]) for hd in hs], axis=-1).astype(bf16)
        yield
        c = []
        for hd in hs:
            c.append(jnp.dot(o, wo_ref[:, sl[hd]], preferred_element_type=f32))
            yield
        y_ref[s, 0] = x + _rmsnorm(jnp.concatenate(c, axis=-1), gpost_ref[...])

    live = [stream(s) for s in range(STREAMS)]
    for _ in range(XATTN_LEAD):
        next(live[0], None)
    while live:
        live = [g for g in live if next(g, True) is None]

    wg_out[...] = wg_ref[...].astype(bf16)
    wu_out[...] = wu_ref[...].astype(bf16)
    wd_out[...] = wd_ref[...].astype(bf16)


def _prompt_xattn(x, gpre, wq, mk, mv, wo, gpost, ffn_w, tile):
    B, L, D = x.shape
    S = STREAMS
    G = B // S
    nt = L // tile
    xs = pl.BlockSpec((S, 1, tile, D), lambda b, i: (0, b, i, 0))
    ms = pl.BlockSpec((S, 1, N_MEM, D), lambda b, i: (0, b, 0, 0))

    def slab(w):
        return pl.BlockSpec((w.shape[0] // (G * nt), w.shape[1]), lambda b, i: (b * nt + i, 0))

    outs = pl.pallas_call(
        _prompt_xattn_kernel, grid=(G, nt),
        in_specs=[xs, _const_spec(gpre.shape), _const_spec(wq.shape), ms, ms,
                  _const_spec(wo.shape), _const_spec(gpost.shape)] + [slab(w) for w in ffn_w],
        out_specs=(xs,) + tuple(slab(w) for w in ffn_w),
        out_shape=(jax.ShapeDtypeStruct((S, G, L, D), f32),)
        + tuple(jax.ShapeDtypeStruct(w.shape, bf16) for w in ffn_w),
        name="prompt_xattn",
        compiler_params=pltpu.CompilerParams(
            dimension_semantics=("arbitrary", "arbitrary"), vmem_limit_bytes=VMEM_LIMIT),
    )(x.reshape(S, G, L, D), gpre, wq, mk.reshape(S, G, N_MEM, D), mv.reshape(S, G, N_MEM, D),
      wo, gpost, *ffn_w)
    return (outs[0].reshape(B, L, D),) + tuple(outs[1:])


def _ffn_kernel(xp_ref, xs_ref, gpre_ref, wg_ref, wu_ref, wd_ref, gpost_ref, yp_ref, ys_ref):
    def rows(x_ref):
        x = x_ref[...]
        hf = _rmsnorm(x, gpre_ref[...]).astype(bf16)
        gate = jnp.dot(hf, wg_ref[...], preferred_element_type=f32)
        up = jnp.dot(hf, wu_ref[...], preferred_element_type=f32)
        act = (_silu(gate) * up).astype(bf16)
        f = jnp.dot(act, wd_ref[...], preferred_element_type=f32)
        return x + _rmsnorm(f, gpost_ref[...])

    i = pl.program_id(0)
    last = pl.num_programs(0) - 1

    @pl.when(i < last)
    def _prompt_rows():
        yp_ref[...] = rows(xp_ref)

    @pl.when(i == last)
    def _sample_rows():
        nb, steps, _ = ys_ref.shape
        y = rows(xs_ref)
        for t in range(steps):
            ys_ref[:, t, :] = y[t * nb:(t + 1) * nb]


def _ffn(xp2d, xs2d, gpre, wg, wu, wd, gpost, tile, steps):
    n, d = xp2d.shape
    assert xs2d.shape == (tile, d)
    steps_p = n // tile
    nb = tile // steps
    prow = pl.BlockSpec((tile, d), lambda i: (jnp.minimum(i, steps_p - 1), 0))
    srow = pl.BlockSpec((tile, d), lambda i: (0, 0))
    return pl.pallas_call(
        _ffn_kernel, grid=(steps_p + 1,),
        in_specs=[prow, srow, _const_spec(gpre.shape), _const_spec(wg.shape), _const_spec(wu.shape),
                  _const_spec(wd.shape), _const_spec(gpost.shape)],
        out_specs=(prow, _const_spec((nb, steps, d))),
        out_shape=(jax.ShapeDtypeStruct((n, d), f32), jax.ShapeDtypeStruct((nb, steps, d), f32)),
        name="ffn",
        compiler_params=pltpu.CompilerParams(
            dimension_semantics=("arbitrary",), vmem_limit_bytes=VMEM_LIMIT),
    )(xp2d, xs2d, gpre, wg, wu, wd, gpost)


def _pad_rows(a, rows):
    if a.shape[0] == rows:
        return a
    return jnp.concatenate([a, jnp.zeros((rows - a.shape[0], a.shape[1]), a.dtype)], axis=0)


def _sample_mixer_kernel(x_ref, cprev_ref, st_ref, ck_ref, cv_ref,
                         gpre_ref, win_ref, convw_ref, convb_ref, dtb_ref, alog_ref,
                         dskip_ref, gssd_ref, sinkcol_ref, wout_ref, gpost_ref,
                         y_ref, ssm_ref, cnew_ref, wk_ref, wv_ref, *, bt, steps):
    R = steps * bt
    half = SSD_WIDTH // SSD_GROUPS
    x = jnp.concatenate([x_ref[:, t, :] for t in range(steps)], axis=0)
    h = _rmsnorm(x, gpre_ref[...]).astype(bf16)
    z = jnp.dot(h, win_ref[:, P_Z:P_XBC], preferred_element_type=f32)
    u = jnp.dot(h, win_ref[:, P_XBC:P_Q], preferred_element_type=f32)
    q = jnp.dot(h, win_ref[:, P_Q:P_K], preferred_element_type=f32)
    k_new = jnp.dot(h, win_ref[:, P_K:P_V], preferred_element_type=f32)
    v_new = jnp.dot(h, win_ref[:, P_V:P_DT], preferred_element_type=f32)
    dt_raw = jnp.dot(h, win_ref[:, P_DT:P_END], preferred_element_type=f32)
    lane = lax.broadcasted_iota(jnp.int32, (1, LANES), 1)
    dt = jnp.where(lane < SSD_HEADS, _softplus(dt_raw + dtb_ref[...]), 0.0)

    def slab(a, t):
        return a[t * bt:(t + 1) * bt]

    HD = ATT_HEAD_DIM
    GH = ATT_HEADS // ATT_KV_HEADS
    GR = GH * R
    reps = GR // bt
    kvg = range(ATT_KV_HEADS)
    keep = WINDOW - steps

    kn_t = _pad_rows(k_new, LANES).T
    vn_t = _pad_rows(v_new, LANES).T

    ridx = lax.broadcasted_iota(jnp.int32, (GR, 1), 0)
    rb = ridx % bt
    rt = (ridx // bt) % steps
    qg = [jnp.concatenate([q[:, (g * GH + hl) * HD:(g * GH + hl + 1) * HD] for hl in range(GH)],
                          axis=0) for g in kvg]
    qg_b = [qg[g].astype(bf16) for g in kvg]
    s_cb = [[_dot(qg_b[g], ck_ref[b, g]) for b in range(bt)] for g in kvg]

    hist = [cprev_ref[j] for j in range(SSD_CONV - 1)] + [slab(u, t) for t in range(steps)]
    xbc_t = []
    for t in range(steps):
        acc = convb_ref[...]
        for j in range(SSD_CONV):
            acc = acc + hist[t + j] * convw_ref[j:j + 1, :]
        xbc_t.append(_silu(acc))
    for j in range(SSD_CONV - 1):
        cnew_ref[j] = hist[steps + j]
    xbc = jnp.concatenate(xbc_t, axis=0)
    xs = xbc[:, 0:SSD_WIDTH]
    bm = xbc[:, SSD_WIDTH:SSD_WIDTH + half]
    cm = xbc[:, SSD_WIDTH + half:SSD_CONV_DIM]
    a_row = -jnp.exp(alog_ref[...])
    adt = dt * a_row
    cs_t = [slab(adt, 0)]
    for t in range(1, steps):
        cs_t.append(cs_t[-1] + slab(adt, t))
    cs = jnp.concatenate(cs_t, axis=0)
    tot = cs_t[-1]
    tot_rows = jnp.concatenate([tot] * steps, axis=0)
    expand = _head_expand_matrix()
    expand2 = jnp.concatenate([expand, expand], axis=0)
    pairs = [(t, s2) for t in range(steps) for s2 in range(t)]
    fac = jnp.concatenate([dt, jnp.exp(cs), jnp.exp(tot_rows - cs), jnp.exp(tot)]
                          + [jnp.exp(cs_t[t] - cs_t[s2]) for t, s2 in pairs], axis=0)
    f_hi, f_lo = _split2(fac)
    fac_e = jnp.dot(jnp.concatenate([f_hi, f_lo], axis=1), expand2, preferred_element_type=f32)
    gr = lax.broadcasted_iota(jnp.int32, (half, SSD_WIDTH), 0)
    gc = lax.broadcasted_iota(jnp.int32, (half, SSD_WIDTH), 1)
    gsum = ((gr >> 7) == (gc >> 8)).astype(bf16)
    gsum2 = jnp.concatenate([gsum, gsum], axis=0)
    cb_pairs = [(t, s2) for t in range(steps) for s2 in range(t + 1)]
    prod = jnp.concatenate([slab(cm, t) * slab(bm, s2) for t, s2 in cb_pairs], axis=0)
    c_hi, c_lo = _split2(prod)
    cb_e = jnp.dot(jnp.concatenate([c_hi, c_lo], axis=1), gsum2, preferred_element_type=f32)

    sel_r = lax.broadcasted_iota(jnp.int32, (LANES, WINDOW), 0)
    sel_l = lax.broadcasted_iota(jnp.int32, (LANES, WINDOW), 1)
    sel = [((sel_r % bt == b) & (sel_r < R) & (sel_l - keep == sel_r // bt)).astype(bf16)
           for b in range(bt)]
    new_k = [_dot_x2(kn_t, sel[b]) for b in range(bt)]
    new_v = [_dot_x2(vn_t, sel[b]) for b in range(bt)]

    jcol = lax.broadcasted_iota(jnp.int32, (GR, WINDOW), 1)
    in_window = jcol > rt
    s_c, m, s_n = [], [], []
    for g in kvg:
        acc = jnp.zeros((GR, WINDOW), f32)
        for b in range(bt):
            acc = jnp.where(rb == b, s_cb[g][b], acc)
        s_c.append(jnp.where(in_window, acc, -jnp.inf))
    sink = [sinkcol_ref[g * GR:(g + 1) * GR, :] for g in kvg]
    for g in kvg:
        mg = jnp.maximum(jnp.max(s_c[g], axis=-1, keepdims=True), sink[g])
        sn_g = []
        for t2 in range(steps):
            kt = jnp.concatenate([slab(k_new, t2)[:, g * HD:(g + 1) * HD]] * reps, axis=0)
            sn = jnp.where(rt >= t2, jnp.sum(qg[g] * kt, axis=-1, keepdims=True), -jnp.inf)
            sn_g.append(sn)
            mg = jnp.maximum(mg, sn)
        m.append(mg)
        s_n.append(sn_g)
    e_c = [jnp.exp(s_c[g] - m[g]) for g in kvg]
    e_n = [[jnp.exp(sn - m[g]) for sn in s_n[g]] for g in kvg]
    rinv = []
    for g in kvg:
        den = jnp.sum(e_c[g], axis=-1, keepdims=True) + jnp.exp(sink[g] - m[g])
        for en in e_n[g]:
            den = den + en
        rinv.append(1.0 / den)
    p_c = [(e_c[g] * rinv[g]).astype(bf16) for g in kvg]

    xdt = xs * fac_e[0:R]
    ecs_e = fac_e[R:2 * R]
    w_end = xdt * fac_e[2 * R:3 * R]
    dec_e = fac_e[3 * R:3 * R + bt]
    pair_decay = {pr: fac_e[3 * R + (n + 1) * bt:3 * R + (n + 2) * bt] for n, pr in enumerate(pairs)}
    y_t = []
    for t in range(steps):
        acc = None
        for s2 in range(t + 1):
            n = cb_pairs.index((t, s2))
            coef = cb_e[n * bt:(n + 1) * bt]
            if s2 < t:
                coef = coef * pair_decay[(t, s2)]
            term = coef * slab(xdt, s2)
            acc = term if acc is None else acc + term
        y_t.append(acc)
    y_intra = jnp.concatenate(y_t, axis=0)
    b_idx = lax.broadcasted_iota(jnp.int32, (bt, 1, LANES), 0)
    l_idx = lax.broadcasted_iota(jnp.int32, (bt, 1, LANES), 2)
    pair = ((l_idx & (bt - 1)) == b_idx) & (l_idx < R)
    own = (l_idx == b_idx)
    gsl = [slice(g * half, (g + 1) * half) for g in range(SSD_GROUPS)]
    h0 = [st_ref[:, gsl[g], :] for g in range(SSD_GROUPS)]
    zz = [_dot_nt(h0[g].reshape(bt * half, SSD_STATE),
                  _pad_rows(cm[:, g * SSD_STATE:(g + 1) * SSD_STATE], LANES)).reshape(bt, half, LANES)
          for g in range(SSD_GROUPS)]
    wt = [_pad_rows(w_end[:, gsl[g]], LANES).T for g in range(SSD_GROUPS)]
    contrib = [_dot(jnp.where(pair, wt[g][None], 0.0).reshape(bt * half, LANES),
                    _pad_rows(bm[:, g * SSD_STATE:(g + 1) * SSD_STATE], LANES)
                    ).reshape(bt, half, SSD_STATE) for g in range(SSD_GROUPS)]
    dec_t = [_pad_rows(dec_e[:, gsl[g]], LANES).T for g in range(SSD_GROUPS)]

    o = []
    for g in kvg:
        pv = [_dot_nt(p_c[g], cv_ref[b, g]) for b in range(bt)]
        og = jnp.zeros((GR, HD), f32)
        for b in range(bt):
            og = jnp.where(rb == b, pv[b], og)
        for t2 in range(steps):
            vt = jnp.concatenate([slab(v_new, t2)[:, g * HD:(g + 1) * HD]] * reps, axis=0)
            og = og + (e_n[g][t2] * rinv[g]) * vt
        o.append(og)

    lane_w = lax.broadcasted_iota(jnp.int32, (1, WINDOW), 1)
    for b in range(bt):
        for g in kvg:
            gs = slice(g * HD, (g + 1) * HD)
            wk_ref[b, g] = jnp.where(lane_w < keep, pltpu.roll(ck_ref[b, g], keep, 1), new_k[b][gs])
            wv_ref[b, g] = jnp.where(lane_w < keep, pltpu.roll(cv_ref[b, g], keep, 1), new_v[b][gs])

    y_off_parts = []
    for g in range(SSD_GROUPS):
        yt = jnp.sum(jnp.where(pair, zz[g], 0.0), axis=0)
        y_off_parts.append(yt.T[0:R, :])
        dec = jnp.sum(jnp.where(own, dec_t[g][None], 0.0), axis=-1, keepdims=True)
        ssm_ref[:, gsl[g], :] = h0[g] * dec + contrib[g]
    y_off = jnp.concatenate(y_off_parts, axis=-1) * ecs_e
    y = y_intra + y_off + xs * dskip_ref[...]
    y_ssd = _gated_group_norm(y, z, gssd_ref[...])

    mix = jnp.dot(y_ssd.astype(bf16), wout_ref[0:SSD_WIDTH, :], preferred_element_type=f32)
    for g in kvg:
        for hl in range(GH):
            hd = g * GH + hl
            mix = mix + jnp.dot(o[g][hl * R:(hl + 1) * R].astype(bf16),
                                wout_ref[SSD_WIDTH + hd * HD:SSD_WIDTH + (hd + 1) * HD, :],
                                preferred_element_type=f32)
    y_ref[...] = (x + _rmsnorm(mix, gpost_ref[...])).reshape(steps, bt, D_MODEL)


def _sample_mixer(x, cprev_tm, st, ck, cv, gpre, win, convw, convb, dtb, alog, dskip_e, gssd,
                  sinkcol, wout, gpost, bt):
    nb, steps, D = x.shape
    kern = functools.partial(_sample_mixer_kernel, bt=bt, steps=steps)
    tm = lambda w: pl.BlockSpec((steps, bt, w), lambda i: (0, i, 0))
    win_spec = pl.BlockSpec((bt, ATT_KV_HEADS, ATT_HEAD_DIM, WINDOW), lambda i: (i, 0, 0, 0))
    in_specs = [
        pl.BlockSpec((bt, steps, D), lambda i: (i, 0, 0)),
        pl.BlockSpec((SSD_CONV - 1, bt, SSD_CONV_DIM), lambda i: (0, i, 0)),
        pl.BlockSpec((bt, SSD_WIDTH, SSD_STATE), lambda i: (i, 0, 0)),
        win_spec, win_spec,
    ] + [_const_spec(a.shape) for a in (gpre, win, convw, convb, dtb, alog, dskip_e, gssd,
                                        sinkcol, wout, gpost)]
    out_specs = (
        tm(D),
        pl.BlockSpec((bt, SSD_WIDTH, SSD_STATE), lambda i: (i, 0, 0)),
        pl.BlockSpec((SSD_CONV - 1, bt, SSD_CONV_DIM), lambda i: (0, i, 0)),
        win_spec, win_spec,
    )
    out_shape = (
        jax.ShapeDtypeStruct((steps, nb, D), f32),
        jax.ShapeDtypeStruct((nb, SSD_WIDTH, SSD_STATE), f32),
        jax.ShapeDtypeStruct((SSD_CONV - 1, nb, SSD_CONV_DIM), f32),
        jax.ShapeDtypeStruct(ck.shape, f32),
        jax.ShapeDtypeStruct(cv.shape, f32),
    )
    return pl.pallas_call(
        kern, grid=(nb // bt,), in_specs=in_specs, out_specs=out_specs, out_shape=out_shape,
        name="sample_mixer",
        compiler_params=pltpu.CompilerParams(
            dimension_semantics=("arbitrary",), vmem_limit_bytes=VMEM_LIMIT),
    )(x, cprev_tm, st, ck, cv, gpre, win, convw, convb, dtb, alog, dskip_e, gssd, sinkcol,
      wout, gpost)


def _sample_xattn_kernel(x_ref, mk_ref, mv_ref, gpre_ref, wq_ref, wo_ref, gpost_ref, y_ref,
                         q_sc, o_sc, *, bt, steps):
    i = pl.program_id(0)
    nb = x_ref.shape[1]
    R = steps * bt
    nrow = bt * N_MEM * X_HEADS

    @pl.when(i == 0)
    def _project_queries():
        x_all = x_ref[...].reshape(steps * nb, D_MODEL)
        hn = _rmsnorm(x_all, gpre_ref[...]).astype(bf16)
        q_sc[...] = jnp.dot(hn, wq_ref[...], preferred_element_type=f32)

    tile_rows = [pl.ds(pl.multiple_of(t * nb + i * bt, bt), bt) for t in range(steps)]
    q = jnp.concatenate([q_sc[r, :] for r in tile_rows], axis=0)
    qs = jnp.concatenate([q[:, hd * X_HEAD_DIM:(hd + 1) * X_HEAD_DIM] for hd in range(X_HEADS)],
                         axis=0)
    kall = mk_ref[...].reshape(nrow, X_HEAD_DIM)
    vall = mv_ref[...].reshape(nrow, X_HEAD_DIM)
    ncol = X_HEADS * R
    seq_rows = N_MEM * X_HEADS
    z = _dot_nt(kall, qs).reshape(bt, seq_rows, ncol)
    v_b = [vall[b * seq_rows:(b + 1) * seq_rows].astype(bf16) for b in range(bt)]
    b_i = lax.broadcasted_iota(jnp.int32, (bt, 1, ncol), 0)
    c_i = lax.broadcasted_iota(jnp.int32, (bt, 1, ncol), 2)
    zc = jnp.sum(jnp.where(c_i % bt == b_i, z, 0.0), axis=0)
    zc = zc.reshape(seq_rows // SUBLANES, SUBLANES, ncol)
    r_h = lax.broadcasted_iota(jnp.int32, (1, SUBLANES, ncol), 1) % X_HEADS
    c_h = lax.broadcasted_iota(jnp.int32, (1, SUBLANES, ncol), 2) // R
    zc = jnp.where(r_h == c_h, zc, -jnp.inf).reshape(seq_rows, ncol)
    m = jnp.max(zc, axis=0, keepdims=True)
    e = jnp.exp(zc - m)
    p = e * (1.0 / jnp.sum(e, axis=0, keepdims=True))
    col_b = lax.broadcasted_iota(jnp.int32, (1, ncol), 1) % bt
    tn = (((0,), (0,)), ((), ()))
    o = None
    for b in range(bt):
        p_b = jnp.where(col_b == b, p, 0.0).astype(bf16)
        o_b = lax.dot_general(p_b, v_b[b], tn, preferred_element_type=f32)
        o = o_b if o is None else o + o_b
    o = jnp.concatenate([o[hd * R:(hd + 1) * R] for hd in range(X_HEADS)], axis=-1)
    for t in range(steps):
        o_sc[tile_rows[t], :] = o[t * bt:(t + 1) * bt]

    @pl.when(i == pl.num_programs(0) - 1)
    def _project_outputs():
        x_all = x_ref[...].reshape(steps * nb, D_MODEL)
        cc = _dot(o_sc[...], wo_ref[...])
        y_ref[...] = (x_all + _rmsnorm(cc, gpost_ref[...])).reshape(steps, nb, D_MODEL)


def _sample_xattn(x_tm, mk, mv, gpre, wq, wo, gpost, bt):
    steps, nb, D = x_tm.shape
    kern = functools.partial(_sample_xattn_kernel, bt=bt, steps=steps)
    xs = _const_spec(x_tm.shape)
    ms = pl.BlockSpec((bt, N_MEM, X_HEADS, X_HEAD_DIM), lambda i: (i, 0, 0, 0))
    return pl.pallas_call(
        kern, grid=(nb // bt,),
        in_specs=[xs, ms, ms, _const_spec(gpre.shape), _const_spec(wq.shape),
                  _const_spec(wo.shape), _const_spec(gpost.shape)],
        out_specs=xs, out_shape=jax.ShapeDtypeStruct((steps, nb, D), f32),
        scratch_shapes=[pltpu.VMEM((steps * nb, D), f32), pltpu.VMEM((steps * nb, D), f32)],
        name="sample_xattn",
        compiler_params=pltpu.CompilerParams(
            dimension_semantics=("arbitrary",), vmem_limit_bytes=VMEM_LIMIT),
    )(x_tm, mk, mv, gpre, wq, wo, gpost)


def _win_prep_kernel(wt_ref, o_ref):
    piece = 2 * LANES

    def put(src_lo, n, dst_lo, scale=None):
        for c in range(0, n, piece):
            t = wt_ref[src_lo + c:src_lo + c + piece, :].T
            if scale is not None:
                t = t * scale
            o_ref[:, dst_lo + c:dst_lo + c + piece] = t.astype(bf16)

    put(W_Z, P_Q - P_Z, P_Z)
    put(W_Q, P_K - P_Q, P_Q, ATT_SCALE)
    put(W_K, P_DT - P_K, P_K)
    dt_rows = jnp.concatenate([wt_ref[W_DT:W_DT + SSD_HEADS, :],
                               jnp.zeros((LANES - SSD_HEADS, D_MODEL), f32)], axis=0)
    o_ref[:, P_DT:P_END] = dt_rows.T.astype(bf16)


def _win_prep(w_t):
    return pl.pallas_call(
        _win_prep_kernel, out_shape=jax.ShapeDtypeStruct((D_MODEL, P_END), bf16),
        name="win_prep",
        compiler_params=pltpu.CompilerParams(vmem_limit_bytes=VMEM_LIMIT),
    )(w_t)


def _row(v, width=None):
    v = v.reshape(1, -1).astype(f32)
    if width is not None and v.shape[1] < width:
        v = jnp.pad(v, ((0, 0), (0, width - v.shape[1])))
    return v


def kernel(x_prompt, x_sample, state_ssm, state_conv, cache_win_k, cache_win_v, cache_mem_k, cache_mem_v, mem_prompt, g_mix_pre, w_in, conv_w, conv_b, dt_bias, a_log, d_skip, g_ssd_norm, sinks, w_out, g_mix_post, g_x_pre, w_xq, g_mem, w_xk, w_xv, w_xo, g_x_post, g_ffn_pre, w_gate, w_up, w_down, g_ffn_post):
    depth = w_in.shape[0]
    assert depth == 1
    B, L, D = x_prompt.shape
    NB, steps, _ = x_sample.shape
    li = 0

    win_p = _win_prep(jnp.transpose(w_in[li]))
    wo_b = w_out[li].astype(bf16)

    gpre, gpost = _row(g_mix_pre[li]), _row(g_mix_post[li])
    convw, convb = conv_w[li].astype(f32), _row(conv_b[li])
    dtb, alog = _row(dt_bias[li], LANES), _row(a_log[li], LANES)
    dskip_e = _row(jnp.repeat(d_skip[li], SSD_HEAD_DIM))
    gssd = _row(g_ssd_norm[li])
    sk = sinks[li].astype(f32)

    mk2d, mv2d, mk4, mv4 = _memkv(mem_prompt.reshape(B * N_MEM, D), _row(g_mem[li]),
                                  w_xk[li].astype(bf16), w_xv[li].astype(bf16), tile=PROMPT_TILE)
    mk3, mv3 = mk2d.reshape(B, N_MEM, D), mv2d.reshape(B, N_MEM, D)
    x1, p_ssm, p_conv8, p_wk, p_wv = _prompt_mixer(
        x_prompt, gpre, win_p, convw, convb, dtb, alog, dskip_e, gssd, sk, wo_b, gpost,
        tile=PROMPT_TILE)
    wxq_b, wxo_b = (w_xq[li] * X_SCALE).astype(bf16), w_xo[li].astype(bf16)
    gxpre, gxpost = _row(g_x_pre[li]), _row(g_x_post[li])
    x2, wg_b, wu_b, wd_b = _prompt_xattn(x1, gxpre, wxq_b, mk3, mv3, wxo_b, gxpost,
                                         (w_gate[li], w_up[li], w_down[li]), tile=PROMPT_TILE)
    gfpre, gfpost = _row(g_ffn_pre[li]), _row(g_ffn_post[li])

    bt, bt_mix = XATTN_SEQS, MIXER_SEQS
    cprev_tm = jnp.transpose(state_conv[li], (1, 0, 2))
    st = state_ssm[li].reshape(NB, SSD_WIDTH, SSD_STATE)
    ck = jnp.transpose(cache_win_k[li], (0, 2, 3, 1))
    cv = jnp.transpose(cache_win_v[li], (0, 2, 3, 1))
    sinkcol = jnp.repeat(sk, steps * bt_mix).reshape(ATT_HEADS * steps * bt_mix, 1)
    x1s, s_ssm, cnew_tm, s_wk, s_wv = _sample_mixer(
        x_sample, cprev_tm, st, ck, cv, gpre, win_p, convw, convb, dtb, alog, dskip_e, gssd,
        sinkcol, wo_b, gpost, bt=bt_mix)
    cmk = cache_mem_k.reshape(NB, N_MEM, X_HEADS, X_HEAD_DIM)
    cmv = cache_mem_v.reshape(NB, N_MEM, X_HEADS, X_HEAD_DIM)
    x2s = _sample_xattn(x1s, cmk, cmv, gxpre, wxq_b, wxo_b, gxpost, bt=bt)
    yp2d, ys = _ffn(x2.reshape(B * L, D), x2s.reshape(steps * NB, D), gfpre, wg_b, wu_b, wd_b,
                    gfpost, tile=steps * NB, steps=steps)
    yp = yp2d.reshape(B, L, D)

    s_conv = jnp.transpose(cnew_tm, (1, 0, 2))
    kv_shape = (ATT_KV_HEADS, ATT_HEAD_DIM)
    return (
        yp, ys,
        p_ssm.reshape(1, B, SSD_HEADS, SSD_HEAD_DIM, SSD_STATE),
        p_conv8[:, SUBLANES - (SSD_CONV - 1):, :][None],
        jnp.transpose(p_wk.reshape(B, *kv_shape, WINDOW), (0, 3, 1, 2))[None],
        jnp.transpose(p_wv.reshape(B, *kv_shape, WINDOW), (0, 3, 1, 2))[None],
        mk4.reshape(1, B, N_MEM, X_HEADS, X_HEAD_DIM), mv4.reshape(1, B, N_MEM, X_HEADS, X_HEAD_DIM),
        s_ssm.reshape(1, NB, SSD_HEADS, SSD_HEAD_DIM, SSD_STATE),
        s_conv[None],
        jnp.transpose(s_wk, (0, 3, 1, 2))[None], jnp.transpose(s_wv, (0, 3, 1, 2))[None],
    )
```

```python
import functools

import jax
import jax.numpy as jnp
from jax import lax
from jax.experimental import pallas as pl
from jax.experimental.pallas import tpu as pltpu

f32 = jnp.float32
bf16 = jnp.bfloat16

D_MODEL = 1024
EPS = 1e-6
N_MEM = 256
SSD_HEADS = 8
SSD_HEAD_DIM = 64
SSD_WIDTH = 512
SSD_GROUPS = 2
SSD_STATE = 128
SSD_CONV = 4
SSD_CHUNK = 128
SSD_CONV_DIM = 1024
ATT_HEADS = 8
ATT_KV_HEADS = 2
ATT_HEAD_DIM = 64
ATT_WIDTH = 512
ATT_KV_WIDTH = 128
WINDOW = 128
ATT_SCALE = ATT_HEAD_DIM ** -0.5
X_HEADS = 4
X_HEAD_DIM = 256
X_SCALE = X_HEAD_DIM ** -0.5
D_FF = 2816
LANES = 128
SUBLANES = 8
VMEM_LIMIT = 56 * 1024 * 1024
PROMPT_TILE = 512
XATTN_SEQS = 4
MIXER_SEQS = 16
STREAMS = 2
PROJ_PIECES, SCAN_PIECES, OUT_PIECES = 19, 10, 5
CONV_BLOCKS = SSD_CONV_DIM // LANES
XATTN_LEAD = 5

P_Z, P_XBC, P_Q, P_K, P_V, P_DT, P_END = 0, 512, 1536, 2048, 2176, 2304, 2432
W_Z, W_DT, W_Q, W_K = 0, 1536, 1544, 2056


def _dot(a, b):
    return jnp.dot(a.astype(bf16), b.astype(bf16), preferred_element_type=f32)


def _dot_nt(a, b):
    return lax.dot_general(a.astype(bf16), b.astype(bf16), (((1,), (1,)), ((), ())),
                           preferred_element_type=f32)


def _split2(x):
    hi = x.astype(bf16)
    lo = (x - hi.astype(f32)).astype(bf16)
    return hi, lo


def _dot_x2(x, m):
    hi, lo = _split2(x)
    return (jnp.dot(hi, m, preferred_element_type=f32)
            + jnp.dot(lo, m, preferred_element_type=f32))


def _rmsnorm(x, g):
    ms = jnp.mean(x * x, axis=-1, keepdims=True)
    return x * lax.rsqrt(ms + EPS) * g


def _silu(x):
    return x * jax.nn.sigmoid(x)


def _softplus(x):
    return jnp.maximum(x, 0.0) + jnp.log1p(jnp.exp(-jnp.abs(x)))


def _head_expand_matrix():
    r = lax.broadcasted_iota(jnp.int32, (LANES, SSD_WIDTH), 0)
    c = lax.broadcasted_iota(jnp.int32, (LANES, SSD_WIDTH), 1)
    return (r == (c >> 6)).astype(bf16)


def _gated_group_norm(y, z, g):
    u = y * _silu(z)
    half = SSD_WIDTH // SSD_GROUPS
    parts = []
    for gi in range(SSD_GROUPS):
        ug = u[:, gi * half:(gi + 1) * half]
        parts.append(ug * lax.rsqrt(jnp.mean(ug * ug, axis=-1, keepdims=True) + EPS))
    return jnp.concatenate(parts, axis=-1) * g


def _prompt_mixer_kernel(x_ref, gpre_ref, win_ref, convw_ref, convb_ref, dtb_ref, alog_ref,
                         dskip_ref, gssd_ref, sinks_ref, wout_ref, gpost_ref,
                         y_ref, ssm_ref, conv_ref, wk_ref, wv_ref,
                         statet_sc, xbc_ext_sc, xbc_sc, z_sc, q_sc, k_sc, v_sc, dt_sc,
                         kprev_sc, vprev_sc, mix_sc, *, tile):
    i = pl.program_id(1)
    NC = tile // SSD_CHUNK
    ns = range(NC)
    C = SSD_CHUNK
    PW = 2 * LANES

    @pl.when(i == 0)
    def _init():
        statet_sc[...] = jnp.zeros_like(statet_sc)
        xbc_ext_sc[:, :, 0:SUBLANES, :] = jnp.zeros((STREAMS, CONV_BLOCKS, SUBLANES, LANES), f32)
        kprev_sc[...] = jnp.zeros_like(kprev_sc)
        vprev_sc[...] = jnp.zeros_like(vprev_sc)

    lane = lax.broadcasted_iota(jnp.int32, (1, LANES), 1)
    a_row = -jnp.exp(alog_ref[...])
    expand = _head_expand_matrix()
    expand2 = jnp.concatenate([expand, expand], axis=0)
    row_i = lax.broadcasted_iota(jnp.int32, (C, C), 0)
    col_i = lax.broadcasted_iota(jnp.int32, (C, C), 1)
    lower = col_i <= row_i
    tri = lower.astype(bf16)
    tri3 = jnp.concatenate([tri, tri, tri], axis=1)
    lo_half = lane < ATT_HEAD_DIM
    half = SSD_WIDTH // SSD_GROUPS
    head_of_lane = lax.broadcasted_iota(jnp.int32, (1, half), 1) >> 6
    rows = [slice(n * C, (n + 1) * C) for n in ns]

    def stream(s):
        statet_s, xbc_ext_s, xbc_s, z_s = statet_sc.at[s], xbc_ext_sc.at[s], xbc_sc.at[s], z_sc.at[s]
        q_s, k_s, v_s, dt_s = q_sc.at[s], k_sc.at[s], v_sc.at[s], dt_sc.at[s]
        kprev_s, vprev_s, mix_s = kprev_sc.at[s], vprev_sc.at[s], mix_sc.at[s]

        x = x_ref[s, 0]
        h = _rmsnorm(x, gpre_ref[...]).astype(bf16)
        yield

        def proj(col):
            return jnp.dot(h, win_ref[:, col:col + PW], preferred_element_type=f32)

        def conv_cols(cb):
            cols = slice(cb * LANES, (cb + 1) * LANES)
            acc = convb_ref[:, cols]
            for j in range(SSD_CONV):
                off = SUBLANES - (SSD_CONV - 1) + j
                acc = acc + xbc_ext_s[cb, off:off + tile, :] * convw_ref[j:j + 1, cols]
            xbc_s[cb] = _silu(acc)

        def xbc_piece(pc):
            res = proj(P_XBC + pc * PW)
            for half_pc in range(PW // LANES):
                xbc_ext_s[(PW // LANES) * pc + half_pc, SUBLANES:SUBLANES + tile, :] = (
                    res[:, half_pc * LANES:(half_pc + 1) * LANES])

        xbc_piece(0)
        yield
        xbc_piece(1)
        yield
        conv_cols(0)
        yield
        xbc_piece(2)
        yield
        conv_cols(1)
        yield
        xbc_piece(3)
        yield
        conv_cols(2)
        yield
        z_s[:, 0:PW] = proj(P_Z)
        yield
        conv_cols(3)
        yield
        z_s[:, PW:2 * PW] = proj(P_Z + PW)
        yield
        conv_cols(4)
        yield
        q_s[:, 0:PW] = proj(P_Q)
        yield
        conv_cols(5)
        yield
        q_s[:, PW:2 * PW] = proj(P_Q + PW)
        yield
        conv_cols(6)
        yield
        kv = proj(P_K)
        k_s[...] = kv[:, 0:LANES]
        v_s[...] = kv[:, LANES:PW]
        yield
        conv_cols(7)
        yield
        dt_raw = jnp.dot(h, win_ref[:, P_DT:P_END], preferred_element_type=f32)
        dt_s[...] = jnp.where(lane < SSD_HEADS, _softplus(dt_raw + dtb_ref[...]), 0.0)
        tail = xbc_ext_s[:, tile:tile + SUBLANES, :]
        conv_ref[s, 0] = jnp.concatenate([tail[cb] for cb in range(CONV_BLOCKS)], axis=-1)
        xbc_ext_s[:, 0:SUBLANES, :] = tail
        yield

        GH = SSD_HEADS // SSD_GROUPS
        heads = range(ATT_HEADS)
        gs = range(SSD_GROUPS)
        zero_b = jnp.zeros((), bf16)

        yield
        def xbc_cols(n, lo, hi):
            return jnp.concatenate([xbc_s[cb, rows[n], :] for cb in range(lo // LANES, hi // LANES)],
                                   axis=-1)
        xs = [xbc_cols(n, 0, SSD_WIDTH) for n in ns]
        bm = [xbc_cols(n, SSD_WIDTH, SSD_WIDTH + half) for n in ns]
        cm = [xbc_cols(n, SSD_WIDTH + half, SSD_CONV_DIM) for n in ns]
        dtc = [dt_s[rows[n], :] for n in ns]
        cs = []
        for n in ns:
            adt = dtc[n] * a_row
            a_hi = adt.astype(bf16)
            a_r1 = adt - a_hi.astype(f32)
            a_mid = a_r1.astype(bf16)
            a_lo = (a_r1 - a_mid.astype(f32)).astype(bf16)
            cs.append(jnp.dot(tri3, jnp.concatenate([a_hi, a_mid, a_lo], axis=0),
                              preferred_element_type=f32))

        yield
        q = [q_s[rows[n], :].astype(bf16) for n in ns]
        k = [k_s[rows[n], :] for n in ns]
        v = [v_s[rows[n], :] for n in ns]
        k_prev = [kprev_s[...]] + k[:-1]
        v_prev = [vprev_s[...]] + v[:-1]
        kprev_s[...] = k[-1]
        vprev_s[...] = v[-1]
        first_bias = jnp.where(i > 0, 0.0, -jnp.inf)
        k_lo, k_hi, v_lo, v_hi = [], [], [], []
        for n in ns:
            kk = jnp.concatenate([k_prev[n], k[n]], axis=0)
            vv = jnp.concatenate([v_prev[n], v[n]], axis=0)
            kk_r = pltpu.roll(kk, ATT_HEAD_DIM, 1)
            vv_r = pltpu.roll(vv, ATT_HEAD_DIM, 1)
            k_lo.append([jnp.where(lo_half, kk, 0.0).astype(bf16), jnp.where(lo_half, kk_r, 0.0).astype(bf16)])
            k_hi.append([jnp.where(lo_half, 0.0, kk_r).astype(bf16), jnp.where(lo_half, 0.0, kk).astype(bf16)])
            v_lo.append([jnp.where(lo_half, vv, 0.0).astype(bf16), jnp.where(lo_half, vv_r, 0.0).astype(bf16)])
            v_hi.append([jnp.where(lo_half, 0.0, vv_r).astype(bf16), jnp.where(lo_half, 0.0, vv).astype(bf16)])
        s_g = [[_dot_nt(jnp.concatenate([q[n][:, (2 * g) * LANES:(2 * g + 1) * LANES],
                                         q[n][:, (2 * g + 1) * LANES:(2 * g + 2) * LANES]], axis=0),
                        jnp.concatenate([k_lo[n][g], k_hi[n][g]], axis=0))
                for g in range(ATT_KV_HEADS)] for n in ns]

        yield
        cs_t = [cs[n].T for n in ns]
        fac_e, cb_all, bm_t = [], [], []
        for n in ns:
            tot = cs[n][C - 1:C, :]
            fac = jnp.concatenate([dtc[n], jnp.exp(cs[n]), jnp.exp(tot - cs[n])], axis=0)
            f_hi = fac.astype(bf16)
            f_lo = (fac - f_hi.astype(f32)).astype(bf16)
            fac_e.append(jnp.dot(jnp.concatenate([f_hi, f_lo], axis=1), expand2,
                                 preferred_element_type=f32))
            cb_all.append(_dot_nt(
                jnp.concatenate([cm[n][:, 0:SSD_STATE], cm[n][:, SSD_STATE:half]], axis=0),
                jnp.concatenate([bm[n][:, 0:SSD_STATE], bm[n][:, SSD_STATE:half]], axis=0)))
            bm_t.append(bm[n].T)

        yield
        def head_scores(n, hd):
            g, jb, sub = hd // 4, (hd // 2) % 2, hd % 2
            s_prev = s_g[n][g][jb * C:(jb + 1) * C, (2 * sub) * C:(2 * sub + 1) * C]
            s_cur = s_g[n][g][jb * C:(jb + 1) * C, (2 * sub + 1) * C:(2 * sub + 2) * C]
            return jnp.where(lower, s_cur, s_prev + first_bias if n == 0 else s_prev)
        sc = [[head_scores(n, hd) for hd in heads] for n in ns]
        sink = [sinks_ref[hd] for hd in heads]
        m = [[jnp.maximum(jnp.max(sc[n][hd], axis=-1, keepdims=True), sink[hd]) for hd in heads]
             for n in ns]

        yield
        xdt = [xs[n] * fac_e[n][0:C] for n in ns]
        ecs_e = [fac_e[n][C:2 * C] for n in ns]
        w_end = [xdt[n] * fac_e[n][2 * C:3 * C] for n in ns]
        xdt_b = [xdt[n].astype(bf16) for n in ns]
        contrib = [[_dot(bm_t[n][g * SSD_STATE:(g + 1) * SSD_STATE, :],
                         w_end[n][:, g * half:(g + 1) * half]) for g in gs] for n in ns]
        decay = [[jnp.exp(jnp.where(lower, cs[n][:, hh:hh + 1] - cs_t[n][hh:hh + 1, :], -jnp.inf))
                  for hh in range(SSD_HEADS)] for n in ns]

        yield
        e = [[jnp.exp(sc[n][hd] - m[n][hd]) for hd in heads] for n in ns]
        den = [[jnp.sum(e[n][hd], axis=-1, keepdims=True) + jnp.exp(sink[hd] - m[n][hd])
                for hd in heads] for n in ns]

        yield
        y_d = []
        for n in ns:
            y_n = []
            for g in gs:
                gl = slice(g * half, (g + 1) * half)
                cb = cb_all[n][g * C:(g + 1) * C, g * C:(g + 1) * C]
                m_parts = [(cb * decay[n][g * GH + r]).astype(bf16) for r in range(GH)]
                x_parts = [jnp.where(head_of_lane == r, xdt_b[n][:, gl], zero_b) for r in range(GH)]
                y_n.append(jnp.dot(jnp.concatenate(m_parts, axis=1), jnp.concatenate(x_parts, axis=0),
                                   preferred_element_type=f32))
            y_d.append(y_n)
        st = [statet_s[g] for g in gs]
        y_off = []
        for n in ns:
            y_off.append([_dot(cm[n][:, g * SSD_STATE:(g + 1) * SSD_STATE], st[g]) for g in gs])
            st = [st[g] * ecs_e[n][C - 1:C, g * half:(g + 1) * half] + contrib[n][g] for g in gs]
        for g in gs:
            statet_s[g] = st[g]

        yield
        p = [[(e[n][hd] * (1.0 / den[n][hd])).astype(bf16) for hd in heads] for n in ns]
        o_g = []
        for n in ns:
            o_n = []
            for g in range(ATT_KV_HEADS):
                p_rows = []
                for jb in range(2):
                    p_cols = []
                    for sub in range(2):
                        ph = p[n][g * 4 + jb * 2 + sub]
                        p_cols += [jnp.where(lower, zero_b, ph), jnp.where(lower, ph, zero_b)]
                    p_rows.append(jnp.concatenate(p_cols, axis=1))
                o_n.append(jnp.dot(jnp.concatenate(p_rows, axis=0),
                                   jnp.concatenate([v_lo[n][g], v_hi[n][g]], axis=0),
                                   preferred_element_type=f32))
            o_g.append(o_n)

        yield
        for n in ns:
            y = (jnp.concatenate([y_d[n][g] + y_off[n][g] * ecs_e[n][:, g * half:(g + 1) * half]
                                  for g in gs], axis=-1)
                 + xs[n] * dskip_ref[...])
            y_ssd = _gated_group_norm(y, z_s[rows[n], :], gssd_ref[...])
            mix_s[rows[n], 0:SSD_WIDTH] = y_ssd.astype(bf16)
        for n in ns:
            for g in range(ATT_KV_HEADS):
                for jb in range(2):
                    lo_l = SSD_WIDTH + (2 * g + jb) * LANES
                    mix_s[rows[n], lo_l:lo_l + LANES] = o_g[n][g][jb * C:(jb + 1) * C].astype(bf16)

        yield

        mix_in = mix_s[...]
        mix = []
        for pc in range(D_MODEL // PW):
            mix.append(jnp.dot(mix_in, wout_ref[:, pc * PW:(pc + 1) * PW], preferred_element_type=f32))
            yield
        y_ref[s, 0] = x + _rmsnorm(jnp.concatenate(mix, axis=-1), gpost_ref[...])

    def mixed(ga, na, gb, nb):
        done_b = 0
        for ka in range(na):
            next(ga, None)
            want_b = ((ka + 1) * nb) // na
            for _ in range(want_b - done_b):
                next(gb, None)
            done_b = want_b

    g0, g1 = [stream(s) for s in range(STREAMS)]
    for _ in range(PROJ_PIECES):
        next(g0, None)
    mixed(g0, SCAN_PIECES, g1, PROJ_PIECES)
    mixed(g1, SCAN_PIECES, g0, OUT_PIECES)
    for g in (g0, g1):
        for _ in g:
            pass

    @pl.when(i == pl.num_programs(1) - 1)
    def _final_state():
        for s in range(STREAMS):
            for g in range(SSD_GROUPS):
                ssm_ref[s, 0, g * half:(g + 1) * half, :] = statet_sc[s, g].T
            wk_ref[s, 0] = k_sc[s, tile - WINDOW:tile, :].T
            wv_ref[s, 0] = v_sc[s, tile - WINDOW:tile, :].T


def _const_spec(shape):
    nd = len(shape)
    return pl.BlockSpec(shape, lambda *_: (0,) * nd)


def _prompt_mixer(x, gpre, win, convw, convb, dtb, alog, dskip_e, gssd, sinks, wout, gpost, tile):
    B, L, D = x.shape
    S = STREAMS
    G = B // S
    kern = functools.partial(_prompt_mixer_kernel, tile=tile)

    def per_seq(rows, width):
        return pl.BlockSpec((S, 1, rows, width), lambda b, i: (0, b, 0, 0))

    tile_spec = pl.BlockSpec((S, 1, tile, D), lambda b, i: (0, b, i, 0))
    out_shape = (
        jax.ShapeDtypeStruct((S, G, L, D), f32),
        jax.ShapeDtypeStruct((S, G, SSD_WIDTH, SSD_STATE), f32),
        jax.ShapeDtypeStruct((S, G, SUBLANES, SSD_CONV_DIM), f32),
        jax.ShapeDtypeStruct((S, G, ATT_KV_WIDTH, WINDOW), f32),
        jax.ShapeDtypeStruct((S, G, ATT_KV_WIDTH, WINDOW), f32),
    )
    in_specs = [
        tile_spec,
        _const_spec(gpre.shape), _const_spec(win.shape), _const_spec(convw.shape),
        _const_spec(convb.shape), _const_spec(dtb.shape), _const_spec(alog.shape),
        _const_spec(dskip_e.shape), _const_spec(gssd.shape),
        pl.BlockSpec(memory_space=pltpu.SMEM),
        _const_spec(wout.shape), _const_spec(gpost.shape),
    ]
    out_specs = (
        tile_spec,
        per_seq(SSD_WIDTH, SSD_STATE),
        per_seq(SUBLANES, SSD_CONV_DIM),
        per_seq(ATT_KV_WIDTH, WINDOW),
        per_seq(ATT_KV_WIDTH, WINDOW),
    )
    scratch = [
        pltpu.VMEM((S, SSD_GROUPS, SSD_STATE, SSD_WIDTH // SSD_GROUPS), f32),
        pltpu.VMEM((S, CONV_BLOCKS, tile + 2 * SUBLANES, LANES), f32),
        pltpu.VMEM((S, CONV_BLOCKS, tile, LANES), f32),
        pltpu.VMEM((S, tile, SSD_WIDTH), f32),
        pltpu.VMEM((S, tile, ATT_WIDTH), f32),
        pltpu.VMEM((S, tile, ATT_KV_WIDTH), f32),
        pltpu.VMEM((S, tile, ATT_KV_WIDTH), f32),
        pltpu.VMEM((S, tile, LANES), f32),
        pltpu.VMEM((S, WINDOW, ATT_KV_WIDTH), f32),
        pltpu.VMEM((S, WINDOW, ATT_KV_WIDTH), f32),
        pltpu.VMEM((S, tile, 2 * SSD_WIDTH), bf16),
    ]
    outs = pl.pallas_call(
        kern, grid=(G, L // tile), in_specs=in_specs, out_specs=out_specs, out_shape=out_shape,
        scratch_shapes=scratch, name="prompt_mixer",
        compiler_params=pltpu.CompilerParams(
            dimension_semantics=("arbitrary", "arbitrary"), vmem_limit_bytes=VMEM_LIMIT),
    )(x.reshape(S, G, L, D), gpre, win, convw, convb, dtb, alog, dskip_e, gssd, sinks, wout, gpost)
    return tuple(o.reshape(B, *o.shape[2:]) for o in outs)


def _memkv_kernel(m_ref, g_ref, wk_ref, wv_ref, k_ref, v_ref, kh_ref, vh_ref):
    mn = _rmsnorm(m_ref[...], g_ref[...]).astype(bf16)
    k = jnp.dot(mn, wk_ref[...], preferred_element_type=f32)
    v = jnp.dot(mn, wv_ref[...], preferred_element_type=f32)
    k_ref[...] = k.astype(bf16)
    v_ref[...] = v.astype(bf16)
    for hd in range(X_HEADS):
        kh_ref[:, hd, :] = k[:, hd * X_HEAD_DIM:(hd + 1) * X_HEAD_DIM]
        vh_ref[:, hd, :] = v[:, hd * X_HEAD_DIM:(hd + 1) * X_HEAD_DIM]


def _memkv(mem2d, g, wk, wv, tile):
    n, d = mem2d.shape
    row = pl.BlockSpec((tile, d), lambda i: (i, 0))
    hrow = pl.BlockSpec((tile, X_HEADS, X_HEAD_DIM), lambda i: (i, 0, 0))
    flat = jax.ShapeDtypeStruct((n, d), bf16)
    heads = jax.ShapeDtypeStruct((n, X_HEADS, X_HEAD_DIM), f32)
    return pl.pallas_call(
        _memkv_kernel, grid=(n // tile,),
        in_specs=[row, _const_spec(g.shape), _const_spec(wk.shape), _const_spec(wv.shape)],
        out_specs=(row, row, hrow, hrow),
        out_shape=(flat, flat, heads, heads),
        name="memory_kv",
        compiler_params=pltpu.CompilerParams(
            dimension_semantics=("arbitrary",), vmem_limit_bytes=VMEM_LIMIT),
    )(mem2d, g, wk, wv)


def _prompt_xattn_kernel(x_ref, gpre_ref, wq_ref, mk_ref, mv_ref, wo_ref, gpost_ref,
                         wg_ref, wu_ref, wd_ref, y_ref, wg_out, wu_out, wd_out):
    hs = range(X_HEADS)
    sl = [slice(hd * X_HEAD_DIM, (hd + 1) * X_HEAD_DIM) for hd in hs]

    def stream(s):
        x = x_ref[s, 0]
        hn = _rmsnorm(x, gpre_ref[...]).astype(bf16)
        yield
        q = []
        for hd in hs:
            q.append(jnp.dot(hn, wq_ref[:, sl[hd]], preferred_element_type=f32))
            yield
        sc = [_dot_nt(q[hd], mk_ref[s, 0, :, sl[hd]]) for hd in hs]
        yield
        m = [jnp.max(sc[hd], axis=-1, keepdims=True) for hd in hs]
        e = [jnp.exp(sc[hd] - m[hd]) for hd in hs]
        yield
        r = [1.0 / jnp.sum(e[hd], axis=-1, keepdims=True) for hd in hs]
        p = [(e[hd] * r[hd]).astype(bf16) for hd in hs]
        yield
        o = jnp.concatenate([_dot(p[hd], mv_ref[s, 0, :, sl[hd]]) for hd in hs], axis=-1).astype(bf16)
        yield
        c = []
        for hd in hs:
            c.append(jnp.dot(o, wo_ref[:, sl[hd]], preferred_element_type=f32))
            yield
        y_ref[s, 0] = x + _rmsnorm(jnp.concatenate(c, axis=-1), gpost_ref[...])

    live = [stream(s) for s in range(STREAMS)]
    for _ in range(XATTN_LEAD):
        next(live[0], None)
    while live:
        live = [g for g in live if next(g, True) is None]

    wg_out[...] = wg_ref[...].astype(bf16)
    wu_out[...] = wu_ref[...].astype(bf16)
    wd_out[...] = wd_ref[...].astype(bf16)


def _prompt_xattn(x, gpre, wq, mk, mv, wo, gpost, ffn_w, tile):
    B, L, D = x.shape
    S = STREAMS
    G = B // S
    nt = L // tile
    xs = pl.BlockSpec((S, 1, tile, D), lambda b, i: (0, b, i, 0))
    ms = pl.BlockSpec((S, 1, N_MEM, D), lambda b, i: (0, b, 0, 0))

    def slab(w):
        return pl.BlockSpec((w.shape[0] // (G * nt), w.shape[1]), lambda b, i: (b * nt + i, 0))

    outs = pl.pallas_call(
        _prompt_xattn_kernel, grid=(G, nt),
        in_specs=[xs, _const_spec(gpre.shape), _const_spec(wq.shape), ms, ms,
                  _const_spec(wo.shape), _const_spec(gpost.shape)] + [slab(w) for w in ffn_w],
        out_specs=(xs,) + tuple(slab(w) for w in ffn_w),
        out_shape=(jax.ShapeDtypeStruct((S, G, L, D), f32),)
        + tuple(jax.ShapeDtypeStruct(w.shape, bf16) for w in ffn_w),
        name="prompt_xattn",
        compiler_params=pltpu.CompilerParams(
            dimension_semantics=("arbitrary", "arbitrary"), vmem_limit_bytes=VMEM_LIMIT),
    )(x.reshape(S, G, L, D), gpre, wq, mk.reshape(S, G, N_MEM, D), mv.reshape(S, G, N_MEM, D),
      wo, gpost, *ffn_w)
    return (outs[0].reshape(B, L, D),) + tuple(outs[1:])


def _ffn_kernel(xp_ref, xs_ref, gpre_ref, wg_ref, wu_ref, wd_ref, gpost_ref, yp_ref, ys_ref):
    def rows(x_ref):
        x = x_ref[...]
        hf = _rmsnorm(x, gpre_ref[...]).astype(bf16)
        gate = jnp.dot(hf, wg_ref[...], preferred_element_type=f32)
        up = jnp.dot(hf, wu_ref[...], preferred_element_type=f32)
        act = (_silu(gate) * up).astype(bf16)
        f = jnp.dot(act, wd_ref[...], preferred_element_type=f32)
        return x + _rmsnorm(f, gpost_ref[...])

    i = pl.program_id(0)
    last = pl.num_programs(0) - 1

    @pl.when(i < last)
    def _prompt_rows():
        yp_ref[...] = rows(xp_ref)

    @pl.when(i == last)
    def _sample_rows():
        nb, steps, _ = ys_ref.shape
        y = rows(xs_ref)
        for t in range(steps):
            ys_ref[:, t, :] = y[t * nb:(t + 1) * nb]


def _ffn(xp2d, xs2d, gpre, wg, wu, wd, gpost, tile, steps):
    n, d = xp2d.shape
    assert xs2d.shape == (tile, d)
    steps_p = n // tile
    nb = tile // steps
    prow = pl.BlockSpec((tile, d), lambda i: (jnp.minimum(i, steps_p - 1), 0))
    srow = pl.BlockSpec((tile, d), lambda i: (0, 0))
    return pl.pallas_call(
        _ffn_kernel, grid=(steps_p + 1,),
        in_specs=[prow, srow, _const_spec(gpre.shape), _const_spec(wg.shape), _const_spec(wu.shape),
                  _const_spec(wd.shape), _const_spec(gpost.shape)],
        out_specs=(prow, _const_spec((nb, steps, d))),
        out_shape=(jax.ShapeDtypeStruct((n, d), f32), jax.ShapeDtypeStruct((nb, steps, d), f32)),
        name="ffn",
        compiler_params=pltpu.CompilerParams(
            dimension_semantics=("arbitrary",), vmem_limit_bytes=VMEM_LIMIT),
    )(xp2d, xs2d, gpre, wg, wu, wd, gpost)


def _pad_rows(a, rows):
    if a.shape[0] == rows:
        return a
    return jnp.concatenate([a, jnp.zeros((rows - a.shape[0], a.shape[1]), a.dtype)], axis=0)


def _sample_mixer_kernel(x_ref, cprev_ref, st_ref, ck_ref, cv_ref,
                         gpre_ref, win_ref, convw_ref, convb_ref, dtb_ref, alog_ref,
                         dskip_ref, gssd_ref, sinkcol_ref, wout_ref, gpost_ref,
                         y_ref, ssm_ref, cnew_ref, wk_ref, wv_ref, *, bt, steps):
    R = steps * bt
    half = SSD_WIDTH // SSD_GROUPS
    x = jnp.concatenate([x_ref[:, t, :] for t in range(steps)], axis=0)
    h = _rmsnorm(x, gpre_ref[...]).astype(bf16)
    z = jnp.dot(h, win_ref[:, P_Z:P_XBC], preferred_element_type=f32)
    u = jnp.dot(h, win_ref[:, P_XBC:P_Q], preferred_element_type=f32)
    q = jnp.dot(h, win_ref[:, P_Q:P_K], preferred_element_type=f32)
    k_new = jnp.dot(h, win_ref[:, P_K:P_V], preferred_element_type=f32)
    v_new = jnp.dot(h, win_ref[:, P_V:P_DT], preferred_element_type=f32)
    dt_raw = jnp.dot(h, win_ref[:, P_DT:P_END], preferred_element_type=f32)
    lane = lax.broadcasted_iota(jnp.int32, (1, LANES), 1)
    dt = jnp.where(lane < SSD_HEADS, _softplus(dt_raw + dtb_ref[...]), 0.0)

    def slab(a, t):
        return a[t * bt:(t + 1) * bt]

    HD = ATT_HEAD_DIM
    GH = ATT_HEADS // ATT_KV_HEADS
    GR = GH * R
    reps = GR // bt
    kvg = range(ATT_KV_HEADS)
    keep = WINDOW - steps

    kn_t = _pad_rows(k_new, LANES).T
    vn_t = _pad_rows(v_new, LANES).T

    ridx = lax.broadcasted_iota(jnp.int32, (GR, 1), 0)
    rb = ridx % bt
    rt = (ridx // bt) % steps
    qg = [jnp.concatenate([q[:, (g * GH + hl) * HD:(g * GH + hl + 1) * HD] for hl in range(GH)],
                          axis=0) for g in kvg]
    qg_b = [qg[g].astype(bf16) for g in kvg]
    s_cb = [[_dot(qg_b[g], ck_ref[b, g]) for b in range(bt)] for g in kvg]

    hist = [cprev_ref[j] for j in range(SSD_CONV - 1)] + [slab(u, t) for t in range(steps)]
    xbc_t = []
    for t in range(steps):
        acc = convb_ref[...]
        for j in range(SSD_CONV):
            acc = acc + hist[t + j] * convw_ref[j:j + 1, :]
        xbc_t.append(_silu(acc))
    for j in range(SSD_CONV - 1):
        cnew_ref[j] = hist[steps + j]
    xbc = jnp.concatenate(xbc_t, axis=0)
    xs = xbc[:, 0:SSD_WIDTH]
    bm = xbc[:, SSD_WIDTH:SSD_WIDTH + half]
    cm = xbc[:, SSD_WIDTH + half:SSD_CONV_DIM]
    a_row = -jnp.exp(alog_ref[...])
    adt = dt * a_row
    cs_t = [slab(adt, 0)]
    for t in range(1, steps):
        cs_t.append(cs_t[-1] + slab(adt, t))
    cs = jnp.concatenate(cs_t, axis=0)
    tot = cs_t[-1]
    tot_rows = jnp.concatenate([tot] * steps, axis=0)
    expand = _head_expand_matrix()
    expand2 = jnp.concatenate([expand, expand], axis=0)
    pairs = [(t, s2) for t in range(steps) for s2 in range(t)]
    fac = jnp.concatenate([dt, jnp.exp(cs), jnp.exp(tot_rows - cs), jnp.exp(tot)]
                          + [jnp.exp(cs_t[t] - cs_t[s2]) for t, s2 in pairs], axis=0)
    f_hi, f_lo = _split2(fac)
    fac_e = jnp.dot(jnp.concatenate([f_hi, f_lo], axis=1), expand2, preferred_element_type=f32)
    gr = lax.broadcasted_iota(jnp.int32, (half, SSD_WIDTH), 0)
    gc = lax.broadcasted_iota(jnp.int32, (half, SSD_WIDTH), 1)
    gsum = ((gr >> 7) == (gc >> 8)).astype(bf16)
    gsum2 = jnp.concatenate([gsum, gsum], axis=0)
    cb_pairs = [(t, s2) for t in range(steps) for s2 in range(t + 1)]
    prod = jnp.concatenate([slab(cm, t) * slab(bm, s2) for t, s2 in cb_pairs], axis=0)
    c_hi, c_lo = _split2(prod)
    cb_e = jnp.dot(jnp.concatenate([c_hi, c_lo], axis=1), gsum2, preferred_element_type=f32)

    sel_r = lax.broadcasted_iota(jnp.int32, (LANES, WINDOW), 0)
    sel_l = lax.broadcasted_iota(jnp.int32, (LANES, WINDOW), 1)
    sel = [((sel_r % bt == b) & (sel_r < R) & (sel_l - keep == sel_r // bt)).astype(bf16)
           for b in range(bt)]
    new_k = [_dot_x2(kn_t, sel[b]) for b in range(bt)]
    new_v = [_dot_x2(vn_t, sel[b]) for b in range(bt)]

    jcol = lax.broadcasted_iota(jnp.int32, (GR, WINDOW), 1)
    in_window = jcol > rt
    s_c, m, s_n = [], [], []
    for g in kvg:
        acc = jnp.zeros((GR, WINDOW), f32)
        for b in range(bt):
            acc = jnp.where(rb == b, s_cb[g][b], acc)
        s_c.append(jnp.where(in_window, acc, -jnp.inf))
    sink = [sinkcol_ref[g * GR:(g + 1) * GR, :] for g in kvg]
    for g in kvg:
        mg = jnp.maximum(jnp.max(s_c[g], axis=-1, keepdims=True), sink[g])
        sn_g = []
        for t2 in range(steps):
            kt = jnp.concatenate([slab(k_new, t2)[:, g * HD:(g + 1) * HD]] * reps, axis=0)
            sn = jnp.where(rt >= t2, jnp.sum(qg[g] * kt, axis=-1, keepdims=True), -jnp.inf)
            sn_g.append(sn)
            mg = jnp.maximum(mg, sn)
        m.append(mg)
        s_n.append(sn_g)
    e_c = [jnp.exp(s_c[g] - m[g]) for g in kvg]
    e_n = [[jnp.exp(sn - m[g]) for sn in s_n[g]] for g in kvg]
    rinv = []
    for g in kvg:
        den = jnp.sum(e_c[g], axis=-1, keepdims=True) + jnp.exp(sink[g] - m[g])
        for en in e_n[g]:
            den = den + en
        rinv.append(1.0 / den)
    p_c = [(e_c[g] * rinv[g]).astype(bf16) for g in kvg]

    xdt = xs * fac_e[0:R]
    ecs_e = fac_e[R:2 * R]
    w_end = xdt * fac_e[2 * R:3 * R]
    dec_e = fac_e[3 * R:3 * R + bt]
    pair_decay = {pr: fac_e[3 * R + (n + 1) * bt:3 * R + (n + 2) * bt] for n, pr in enumerate(pairs)}
    y_t = []
    for t in range(steps):
        acc = None
        for s2 in range(t + 1):
            n = cb_pairs.index((t, s2))
            coef = cb_e[n * bt:(n + 1) * bt]
            if s2 < t:
                coef = coef * pair_decay[(t, s2)]
            term = coef * slab(xdt, s2)
            acc = term if acc is None else acc + term
        y_t.append(acc)
    y_intra = jnp.concatenate(y_t, axis=0)
    b_idx = lax.broadcasted_iota(jnp.int32, (bt, 1, LANES), 0)
    l_idx = lax.broadcasted_iota(jnp.int32, (bt, 1, LANES), 2)
    pair = ((l_idx & (bt - 1)) == b_idx) & (l_idx < R)
    own = (l_idx == b_idx)
    gsl = [slice(g * half, (g + 1) * half) for g in range(SSD_GROUPS)]
    h0 = [st_ref[:, gsl[g], :] for g in range(SSD_GROUPS)]
    zz = [_dot_nt(h0[g].reshape(bt * half, SSD_STATE),
                  _pad_rows(cm[:, g * SSD_STATE:(g + 1) * SSD_STATE], LANES)).reshape(bt, half, LANES)
          for g in range(SSD_GROUPS)]
    wt = [_pad_rows(w_end[:, gsl[g]], LANES).T for g in range(SSD_GROUPS)]
    contrib = [_dot(jnp.where(pair, wt[g][None], 0.0).reshape(bt * half, LANES),
                    _pad_rows(bm[:, g * SSD_STATE:(g + 1) * SSD_STATE], LANES)
                    ).reshape(bt, half, SSD_STATE) for g in range(SSD_GROUPS)]
    dec_t = [_pad_rows(dec_e[:, gsl[g]], LANES).T for g in range(SSD_GROUPS)]

    o = []
    for g in kvg:
        pv = [_dot_nt(p_c[g], cv_ref[b, g]) for b in range(bt)]
        og = jnp.zeros((GR, HD), f32)
        for b in range(bt):
            og = jnp.where(rb == b, pv[b], og)
        for t2 in range(steps):
            vt = jnp.concatenate([slab(v_new, t2)[:, g * HD:(g + 1) * HD]] * reps, axis=0)
            og = og + (e_n[g][t2] * rinv[g]) * vt
        o.append(og)

    lane_w = lax.broadcasted_iota(jnp.int32, (1, WINDOW), 1)
    for b in range(bt):
        for g in kvg:
            gs = slice(g * HD, (g + 1) * HD)
            wk_ref[b, g] = jnp.where(lane_w < keep, pltpu.roll(ck_ref[b, g], keep, 1), new_k[b][gs])
            wv_ref[b, g] = jnp.where(lane_w < keep, pltpu.roll(cv_ref[b, g], keep, 1), new_v[b][gs])

    y_off_parts = []
    for g in range(SSD_GROUPS):
        yt = jnp.sum(jnp.where(pair, zz[g], 0.0), axis=0)
        y_off_parts.append(yt.T[0:R, :])
        dec = jnp.sum(jnp.where(own, dec_t[g][None], 0.0), axis=-1, keepdims=True)
        ssm_ref[:, gsl[g], :] = h0[g] * dec + contrib[g]
    y_off = jnp.concatenate(y_off_parts, axis=-1) * ecs_e
    y = y_intra + y_off + xs * dskip_ref[...]
    y_ssd = _gated_group_norm(y, z, gssd_ref[...])

    mix = jnp.dot(y_ssd.astype(bf16), wout_ref[0:SSD_WIDTH, :], preferred_element_type=f32)
    for g in kvg:
        for hl in range(GH):
            hd = g * GH + hl
            mix = mix + jnp.dot(o[g][hl * R:(hl + 1) * R].astype(bf16),
                                wout_ref[SSD_WIDTH + hd * HD:SSD_WIDTH + (hd + 1) * HD, :],
                                preferred_element_type=f32)
    y_ref[...] = (x + _rmsnorm(mix, gpost_ref[...])).reshape(steps, bt, D_MODEL)


def _sample_mixer(x, cprev_tm, st, ck, cv, gpre, win, convw, convb, dtb, alog, dskip_e, gssd,
                  sinkcol, wout, gpost, bt):
    nb, steps, D = x.shape
    kern = functools.partial(_sample_mixer_kernel, bt=bt, steps=steps)
    tm = lambda w: pl.BlockSpec((steps, bt, w), lambda i: (0, i, 0))
    win_spec = pl.BlockSpec((bt, ATT_KV_HEADS, ATT_HEAD_DIM, WINDOW), lambda i: (i, 0, 0, 0))
    in_specs = [
        pl.BlockSpec((bt, steps, D), lambda i: (i, 0, 0)),
        pl.BlockSpec((SSD_CONV - 1, bt, SSD_CONV_DIM), lambda i: (0, i, 0)),
        pl.BlockSpec((bt, SSD_WIDTH, SSD_STATE), lambda i: (i, 0, 0)),
        win_spec, win_spec,
    ] + [_const_spec(a.shape) for a in (gpre, win, convw, convb, dtb, alog, dskip_e, gssd,
                                        sinkcol, wout, gpost)]
    out_specs = (
        tm(D),
        pl.BlockSpec((bt, SSD_WIDTH, SSD_STATE), lambda i: (i, 0, 0)),
        pl.BlockSpec((SSD_CONV - 1, bt, SSD_CONV_DIM), lambda i: (0, i, 0)),
        win_spec, win_spec,
    )
    out_shape = (
        jax.ShapeDtypeStruct((steps, nb, D), f32),
        jax.ShapeDtypeStruct((nb, SSD_WIDTH, SSD_STATE), f32),
        jax.ShapeDtypeStruct((SSD_CONV - 1, nb, SSD_CONV_DIM), f32),
        jax.ShapeDtypeStruct(ck.shape, f32),
        jax.ShapeDtypeStruct(cv.shape, f32),
    )
    return pl.pallas_call(
        kern, grid=(nb // bt,), in_specs=in_specs, out_specs=out_specs, out_shape=out_shape,
        name="sample_mixer",
        compiler_params=pltpu.CompilerParams(
            dimension_semantics=("arbitrary",), vmem_limit_bytes=VMEM_LIMIT),
    )(x, cprev_tm, st, ck, cv, gpre, win, convw, convb, dtb, alog, dskip_e, gssd, sinkcol,
      wout, gpost)


def _sample_xattn_kernel(x_ref, mk_ref, mv_ref, gpre_ref, wq_ref, wo_ref, gpost_ref, y_ref,
                         q_sc, o_sc, *, bt, steps):
    i = pl.program_id(0)
    nb = x_ref.shape[1]
    qg = SUBLANES
    grp, part = i // (qg // bt), i % (qg // bt)
    R = steps * qg
    nrow = bt * N_MEM * X_HEADS

    @pl.when(i == 0)
    def _project_queries():
        x_all = x_ref[...].reshape(steps * nb, D_MODEL)
        hn = _rmsnorm(x_all, gpre_ref[...]).astype(bf16)
        q_sc[...] = jnp.dot(hn, wq_ref[...], preferred_element_type=f32)

    tile_rows = [pl.ds(pl.multiple_of(t * nb + grp * qg, qg), qg) for t in range(steps)]
    q = jnp.concatenate([q_sc[r, :] for r in tile_rows], axis=0)
    qs = jnp.concatenate([q[:, hd * X_HEAD_DIM:(hd + 1) * X_HEAD_DIM] for hd in range(X_HEADS)],
                         axis=0)
    kall = mk_ref[...].reshape(nrow, X_HEAD_DIM)
    vall = mv_ref[...].reshape(nrow, X_HEAD_DIM)
    ncol = X_HEADS * R
    seq_rows = N_MEM * X_HEADS
    z = _dot_nt(kall, qs).reshape(bt, seq_rows, ncol)
    v_b = [vall[b * seq_rows:(b + 1) * seq_rows].astype(bf16) for b in range(bt)]
    b_i = lax.broadcasted_iota(jnp.int32, (bt, 1, ncol), 0)
    c_i = lax.broadcasted_iota(jnp.int32, (bt, 1, ncol), 2)
    zc = jnp.sum(jnp.where(c_i % qg == b_i + part * bt, z, 0.0), axis=0)
    zc = zc.reshape(seq_rows // SUBLANES, SUBLANES, ncol)
    r_h = lax.broadcasted_iota(jnp.int32, (1, SUBLANES, ncol), 1) % X_HEADS
    c_h = lax.broadcasted_iota(jnp.int32, (1, SUBLANES, ncol), 2) // R
    zc = jnp.where(r_h == c_h, zc, -jnp.inf).reshape(seq_rows, ncol)
    m = jnp.max(zc, axis=0, keepdims=True)
    e = jnp.exp(zc - m)
    p = e * (1.0 / jnp.sum(e, axis=0, keepdims=True))
    col_b = lax.broadcasted_iota(jnp.int32, (1, ncol), 1) % qg
    tn = (((0,), (0,)), ((), ()))
    o = None
    for b in range(bt):
        p_b = jnp.where(col_b == b + part * bt, p, 0.0).astype(bf16)
        o_b = lax.dot_general(p_b, v_b[b], tn, preferred_element_type=f32)
        o = o_b if o is None else o + o_b
    o = jnp.concatenate([o[hd * R:(hd + 1) * R] for hd in range(X_HEADS)], axis=-1)
    mine = lax.broadcasted_iota(jnp.int32, (qg, 1), 0) // bt == part

    @pl.when(part == 0)
    def _write_rows():
        for t in range(steps):
            o_sc[tile_rows[t], :] = o[t * qg:(t + 1) * qg]

    @pl.when(part != 0)
    def _merge_rows():
        for t in range(steps):
            o_sc[tile_rows[t], :] = jnp.where(mine, o[t * qg:(t + 1) * qg], o_sc[tile_rows[t], :])

    @pl.when(i == pl.num_programs(0) - 1)
    def _project_outputs():
        x_all = x_ref[...].reshape(steps * nb, D_MODEL)
        cc = _dot(o_sc[...], wo_ref[...])
        y_ref[...] = (x_all + _rmsnorm(cc, gpost_ref[...])).reshape(steps, nb, D_MODEL)


def _sample_xattn(x_tm, mk, mv, gpre, wq, wo, gpost, bt):
    steps, nb, D = x_tm.shape
    kern = functools.partial(_sample_xattn_kernel, bt=bt, steps=steps)
    xs = _const_spec(x_tm.shape)
    ms = pl.BlockSpec((bt, N_MEM, X_HEADS, X_HEAD_DIM), lambda i: (i, 0, 0, 0))
    return pl.pallas_call(
        kern, grid=(nb // bt,),
        in_specs=[xs, ms, ms, _const_spec(gpre.shape), _const_spec(wq.shape),
                  _const_spec(wo.shape), _const_spec(gpost.shape)],
        out_specs=xs, out_shape=jax.ShapeDtypeStruct((steps, nb, D), f32),
        scratch_shapes=[pltpu.VMEM((steps * nb, D), f32), pltpu.VMEM((steps * nb, D), f32)],
        name="sample_xattn",
        compiler_params=pltpu.CompilerParams(
            dimension_semantics=("arbitrary",), vmem_limit_bytes=VMEM_LIMIT),
    )(x_tm, mk, mv, gpre, wq, wo, gpost)


def _win_prep_kernel(wt_ref, o_ref):
    piece = 2 * LANES

    def put(src_lo, n, dst_lo, scale=None):
        for c in range(0, n, piece):
            t = wt_ref[src_lo + c:src_lo + c + piece, :].T
            if scale is not None:
                t = t * scale
            o_ref[:, dst_lo + c:dst_lo + c + piece] = t.astype(bf16)

    put(W_Z, P_Q - P_Z, P_Z)
    put(W_Q, P_K - P_Q, P_Q, ATT_SCALE)
    put(W_K, P_DT - P_K, P_K)
    dt_rows = jnp.concatenate([wt_ref[W_DT:W_DT + SSD_HEADS, :],
                               jnp.zeros((LANES - SSD_HEADS, D_MODEL), f32)], axis=0)
    o_ref[:, P_DT:P_END] = dt_rows.T.astype(bf16)


def _win_prep(w_t):
    return pl.pallas_call(
        _win_prep_kernel, out_shape=jax.ShapeDtypeStruct((D_MODEL, P_END), bf16),
        name="win_prep",
        compiler_params=pltpu.CompilerParams(vmem_limit_bytes=VMEM_LIMIT),
    )(w_t)


def _row(v, width=None):
    v = v.reshape(1, -1).astype(f32)
    if width is not None and v.shape[1] < width:
        v = jnp.pad(v, ((0, 0), (0, width - v.shape[1])))
    return v


def kernel(x_prompt, x_sample, state_ssm, state_conv, cache_win_k, cache_win_v, cache_mem_k, cache_mem_v, mem_prompt, g_mix_pre, w_in, conv_w, conv_b, dt_bias, a_log, d_skip, g_ssd_norm, sinks, w_out, g_mix_post, g_x_pre, w_xq, g_mem, w_xk, w_xv, w_xo, g_x_post, g_ffn_pre, w_gate, w_up, w_down, g_ffn_post):
    depth = w_in.shape[0]
    assert depth == 1
    B, L, D = x_prompt.shape
    NB, steps, _ = x_sample.shape
    li = 0

    win_p = _win_prep(jnp.transpose(w_in[li]))
    wo_b = w_out[li].astype(bf16)

    gpre, gpost = _row(g_mix_pre[li]), _row(g_mix_post[li])
    convw, convb = conv_w[li].astype(f32), _row(conv_b[li])
    dtb, alog = _row(dt_bias[li], LANES), _row(a_log[li], LANES)
    dskip_e = _row(jnp.repeat(d_skip[li], SSD_HEAD_DIM))
    gssd = _row(g_ssd_norm[li])
    sk = sinks[li].astype(f32)

    mk2d, mv2d, mk4, mv4 = _memkv(mem_prompt.reshape(B * N_MEM, D), _row(g_mem[li]),
                                  w_xk[li].astype(bf16), w_xv[li].astype(bf16), tile=PROMPT_TILE)
    mk3, mv3 = mk2d.reshape(B, N_MEM, D), mv2d.reshape(B, N_MEM, D)
    x1, p_ssm, p_conv8, p_wk, p_wv = _prompt_mixer(
        x_prompt, gpre, win_p, convw, convb, dtb, alog, dskip_e, gssd, sk, wo_b, gpost,
        tile=PROMPT_TILE)
    wxq_b, wxo_b = (w_xq[li] * X_SCALE).astype(bf16), w_xo[li].astype(bf16)
    gxpre, gxpost = _row(g_x_pre[li]), _row(g_x_post[li])
    x2, wg_b, wu_b, wd_b = _prompt_xattn(x1, gxpre, wxq_b, mk3, mv3, wxo_b, gxpost,
                                         (w_gate[li], w_up[li], w_down[li]), tile=PROMPT_TILE)
    gfpre, gfpost = _row(g_ffn_pre[li]), _row(g_ffn_post[li])

    bt, bt_mix = XATTN_SEQS, MIXER_SEQS
    cprev_tm = jnp.transpose(state_conv[li], (1, 0, 2))
    st = state_ssm[li].reshape(NB, SSD_WIDTH, SSD_STATE)
    ck = jnp.transpose(cache_win_k[li], (0, 2, 3, 1))
    cv = jnp.transpose(cache_win_v[li], (0, 2, 3, 1))
    sinkcol = jnp.repeat(sk, steps * bt_mix).reshape(ATT_HEADS * steps * bt_mix, 1)
    x1s, s_ssm, cnew_tm, s_wk, s_wv = _sample_mixer(
        x_sample, cprev_tm, st, ck, cv, gpre, win_p, convw, convb, dtb, alog, dskip_e, gssd,
        sinkcol, wo_b, gpost, bt=bt_mix)
    cmk = cache_mem_k.reshape(NB, N_MEM, X_HEADS, X_HEAD_DIM)
    cmv = cache_mem_v.reshape(NB, N_MEM, X_HEADS, X_HEAD_DIM)
    x2s = _sample_xattn(x1s, cmk, cmv, gxpre, wxq_b, wxo_b, gxpost, bt=bt)
    yp2d, ys = _ffn(x2.reshape(B * L, D), x2s.reshape(steps * NB, D), gfpre, wg_b, wu_b, wd_b,
                    gfpost, tile=steps * NB, steps=steps)
    yp = yp2d.reshape(B, L, D)

    s_conv = jnp.transpose(cnew_tm, (1, 0, 2))
    kv_shape = (ATT_KV_HEADS, ATT_HEAD_DIM)
    return (
        yp, ys,
        p_ssm.reshape(1, B, SSD_HEADS, SSD_HEAD_DIM, SSD_STATE),
        p_conv8[:, SUBLANES - (SSD_CONV - 1):, :][None],
        jnp.transpose(p_wk.reshape(B, *kv_shape, WINDOW), (0, 3, 1, 2))[None],
        jnp.transpose(p_wv.reshape(B, *kv_shape, WINDOW), (0, 3, 1, 2))[None],
        mk4.reshape(1, B, N_MEM, X_HEADS, X_HEAD_DIM), mv4.reshape(1, B, N_MEM, X_HEADS, X_HEAD_DIM),
        s_ssm.reshape(1, NB, SSD_HEADS, SSD_HEAD_DIM, SSD_STATE),
        s_conv[None],
        jnp.transpose(s_wk, (0, 3, 1, 2))[None], jnp.transpose(s_wv, (0, 3, 1, 2))[None],
    )
```
